```python
import math
import jax, jax.numpy as jnp
from jax import lax
import numpy as np

D_MODEL = 1024
BATCH = 8
SEQ = 8192
DEPTH = 1

LRU_WIDTH = D_MODEL
LRU_BLOCKS = 16
LRU_BLOCK_W = LRU_WIDTH // LRU_BLOCKS
LRU_C = 8.0
CONV_K = 4
SSD_HEAD_DIM = 64
SSD_INNER = D_MODEL
SSD_HEADS = SSD_INNER // SSD_HEAD_DIM
SSD_GROUPS = 2
SSD_HPG = SSD_HEADS // SSD_GROUPS
SSD_STATE = 128
SSD_CHUNK = 128
MIX_WIDTH = LRU_WIDTH + SSD_INNER
SPLITS = (LRU_WIDTH, LRU_WIDTH, SSD_INNER, SSD_INNER,
          SSD_GROUPS * SSD_STATE, SSD_GROUPS * SSD_STATE, SSD_HEADS)
IN_COLS = sum(SPLITS)
SSD_CONV_CH = SSD_INNER + 2 * SSD_GROUPS * SSD_STATE
D_FF = -(-8 * D_MODEL // (3 * 256)) * 256
EPS = 1e-6

kernel_name = "hymba_style_rglru_ssd_hybrid"


def rms_norm(x, w):
    xf = x.astype(jnp.float32)
    y = xf * lax.rsqrt(jnp.mean(xf * xf, axis=-1, keepdims=True) + EPS)
    return (y * w.astype(jnp.float32)).astype(x.dtype)


def causal_depthwise_conv(x, w, b):
    K = w.shape[0]
    T = x.shape[1]
    xp = jnp.pad(x, ((0, 0), (K - 1, 0), (0, 0)))
    y = b + xp[:, 0:T] * w[0]
    for k in range(1, K):
        y = y + xp[:, k:k + T] * w[k]
    return y


def rg_lru(x, w_a, b_a, w_x, b_x, lam):
    bsz, T, W = x.shape
    xf = x.astype(jnp.float32)
    xb = xf.reshape(bsz, T, LRU_BLOCKS, LRU_BLOCK_W)
    r = jax.nn.sigmoid(jnp.einsum("btki,kij->btkj", xb, w_a.astype(jnp.float32)).reshape(bsz, T, W) + b_a.astype(jnp.float32))
    i = jax.nn.sigmoid(jnp.einsum("btki,kij->btkj", xb, w_x.astype(jnp.float32)).reshape(bsz, T, W) + b_x.astype(jnp.float32))
    log_a = -LRU_C * r * jax.nn.softplus(-lam.astype(jnp.float32))
    a = jnp.exp(log_a)
    u = jnp.sqrt(-jnp.expm1(2.0 * log_a)) * (i * xf)

    def combine(left, right):
        a1, b1 = left
        a2, b2 = right
        return a1 * a2, a2 * b1 + b2

    _, h = lax.associative_scan(combine, (a, u), axis=1)
    return h.astype(x.dtype)


def segsum(a):
    L = a.shape[-1]
    cs = jnp.cumsum(a, axis=-1)
    diff = cs[..., :, None] - cs[..., None, :]
    mask = jnp.tril(jnp.ones((L, L), dtype=bool))
    return jnp.where(mask, diff, -jnp.inf)


def ssd_chunked(xs, a, Bm, Cm):
    b, t, g, e, p = xs.shape
    n = Bm.shape[-1]
    c = t // SSD_CHUNK
    xs = xs.reshape(b, c, SSD_CHUNK, g, e, p)
    Bm = Bm.reshape(b, c, SSD_CHUNK, g, n)
    Cm = Cm.reshape(b, c, SSD_CHUNK, g, n)
    a = a.reshape(b, c, SSD_CHUNK, g, e).transpose(0, 3, 4, 1, 2)
    a_cs = jnp.cumsum(a, axis=-1)
    Lmat = jnp.exp(segsum(a))
    scores = jnp.einsum("bclgn,bcsgn->bgcls", Cm, Bm)
    M = scores[:, :, None] * Lmat
    y_diag = jnp.einsum("bgecls,bcsgep->bclgep", M, xs)
    decay_states = jnp.exp(a_cs[..., -1:] - a_cs)
    states = jnp.einsum("bclgn,bgecl,bclgep->bcgepn", Bm, decay_states, xs)
    chunk_a = jnp.pad(a_cs[..., -1], ((0, 0), (0, 0), (0, 0), (1, 0)))
    decay_chunk = jnp.exp(segsum(chunk_a))
    states = jnp.concatenate([jnp.zeros_like(states[:, :1]), states], axis=1)
    prev_states = jnp.einsum("bgezc,bcgepn->bzgepn", decay_chunk, states)[:, :-1]
    y_off = jnp.einsum("bclgn,bcgepn,bgecl->bclgep", Cm, prev_states, jnp.exp(a_cs))
    return (y_diag + y_off).reshape(b, t, g, e, p)


def _fwd_setup_inputs(seed: int = 0) -> dict:
    key = jax.random.key(seed)
    ks = jax.random.split(key, 24)
    f32 = jnp.float32
    L = DEPTH

    def nrm(k, shape, scale):
        return jax.random.normal(k, shape, f32) * scale

    def gain(k, shape):
        return 1.0 + 0.05 * jax.random.normal(k, shape, f32)

    x = jax.random.normal(ks[0], (BATCH, SEQ, D_MODEL), f32)
    a_init = jax.random.uniform(ks[9], (L, LRU_WIDTH), f32, 0.9, 0.999)
    s = a_init ** (1.0 / LRU_C)
    lru_lambda = jnp.log(s) - jnp.log1p(-s)
    dt0 = jnp.exp(jax.random.uniform(ks[13], (L, SSD_HEADS), f32, math.log(1e-3), math.log(1e-1)))
    ssd_dt_bias = dt0 + jnp.log(-jnp.expm1(-dt0))
    ssd_a_log = jnp.log(jax.random.uniform(ks[14], (L, SSD_HEADS), f32, 1.0, 16.0))
    return {
        "x": x,
        "pre_mix_norm": gain(ks[1], (L, D_MODEL)),
        "w_in": nrm(ks[2], (L, D_MODEL, IN_COLS), D_MODEL ** -0.5),
        "lru_conv_w": nrm(ks[3], (L, CONV_K, LRU_WIDTH), CONV_K ** -0.5),
        "lru_conv_b": nrm(ks[4], (L, LRU_WIDTH), 0.01),
        "lru_wa": nrm(ks[5], (L, LRU_BLOCKS, LRU_BLOCK_W, LRU_BLOCK_W), LRU_BLOCK_W ** -0.5),
        "lru_ba": nrm(ks[6], (L, LRU_WIDTH), 0.01),
        "lru_wx": nrm(ks[7], (L, LRU_BLOCKS, LRU_BLOCK_W, LRU_BLOCK_W), LRU_BLOCK_W ** -0.5),
        "lru_bx": nrm(ks[8], (L, LRU_WIDTH), 0.01),
        "lru_lambda": lru_lambda,
        "lru_out_norm": gain(ks[10], (L, LRU_WIDTH)),
        "ssd_conv_w": nrm(ks[11], (L, CONV_K, SSD_CONV_CH), CONV_K ** -0.5),
        "ssd_conv_b": nrm(ks[12], (L, SSD_CONV_CH), 0.01),
        "ssd_dt_bias": ssd_dt_bias,
        "ssd_a_log": ssd_a_log,
        "ssd_d": gain(ks[15], (L, SSD_HEADS)),
        "ssd_out_norm": gain(ks[16], (L, SSD_INNER)),
        "w_out": nrm(ks[17], (L, MIX_WIDTH, D_MODEL), MIX_WIDTH ** -0.5),
        "post_mix_norm": gain(ks[18], (L, D_MODEL)),
        "pre_ffn_norm": gain(ks[19], (L, D_MODEL)),
        "w_gate": nrm(ks[20], (L, D_MODEL, D_FF), D_MODEL ** -0.5),
        "w_up": nrm(ks[21], (L, D_MODEL, D_FF), D_MODEL ** -0.5),
        "w_down": nrm(ks[22], (L, D_FF, D_MODEL), D_FF ** -0.5),
        "post_ffn_norm": gain(ks[23], (L, D_MODEL)),
    }


def _fwd_reference(x, pre_mix_norm, w_in, lru_conv_w, lru_conv_b, lru_wa, lru_ba, lru_wx, lru_bx,
              lru_lambda, lru_out_norm, ssd_conv_w, ssd_conv_b, ssd_dt_bias, ssd_a_log, ssd_d,
              ssd_out_norm, w_out, post_mix_norm, pre_ffn_norm, w_gate, w_up, w_down, post_ffn_norm):
    bsz, T, _ = x.shape
    offs = np.cumsum((0,) + SPLITS)
    for li in range(DEPTH):
        h = rms_norm(x, pre_mix_norm[li])
        proj = h @ w_in[li]
        lru_x = proj[..., offs[0]:offs[1]]
        lru_gate = proj[..., offs[1]:offs[2]]
        ssd_z = proj[..., offs[2]:offs[3]]
        ssd_xbc = proj[..., offs[3]:offs[6]]
        ssd_dt = proj[..., offs[6]:offs[7]]

        lx = causal_depthwise_conv(lru_x, lru_conv_w[li], lru_conv_b[li])
        lh = rg_lru(lx, lru_wa[li], lru_ba[li], lru_wx[li], lru_bx[li], lru_lambda[li])
        y_lru = rms_norm(lh * jax.nn.gelu(lru_gate), lru_out_norm[li])

        xbc = jax.nn.silu(causal_depthwise_conv(ssd_xbc, ssd_conv_w[li], ssd_conv_b[li]))
        sx = xbc[..., :SSD_INNER].astype(jnp.float32).reshape(bsz, T, SSD_GROUPS, SSD_HPG, SSD_HEAD_DIM)
        sB = xbc[..., SSD_INNER:SSD_INNER + SSD_GROUPS * SSD_STATE].astype(jnp.float32).reshape(bsz, T, SSD_GROUPS, SSD_STATE)
        sC = xbc[..., SSD_INNER + SSD_GROUPS * SSD_STATE:].astype(jnp.float32).reshape(bsz, T, SSD_GROUPS, SSD_STATE)
        dt = jax.nn.softplus(ssd_dt.astype(jnp.float32) + ssd_dt_bias[li].astype(jnp.float32))
        dt = dt.reshape(bsz, T, SSD_GROUPS, SSD_HPG)
        A = -jnp.exp(ssd_a_log[li].astype(jnp.float32)).reshape(SSD_GROUPS, SSD_HPG)
        y = ssd_chunked(sx * dt[..., None], dt * A, sB, sC)
        y = y + ssd_d[li].astype(jnp.float32).reshape(SSD_GROUPS, SSD_HPG)[..., None] * sx
        y = y.reshape(bsz, T, SSD_INNER).astype(x.dtype)
        y_ssd = rms_norm(y * jax.nn.silu(ssd_z), ssd_out_norm[li])

        mix = jnp.concatenate([y_lru, y_ssd], axis=-1) @ w_out[li]
        x = x + rms_norm(mix, post_mix_norm[li])

        h = rms_norm(x, pre_ffn_norm[li])
        f = (jax.nn.silu(h @ w_gate[li]) * (h @ w_up[li])) @ w_down[li]
        x = x + rms_norm(f, post_ffn_norm[li])
    return x


import jax as _jax
import jax.numpy as _jnp

TWIN_FORMAT = 'train_step'
FWD_PARAMS = ['x', 'pre_mix_norm', 'w_in', 'lru_conv_w', 'lru_conv_b', 'lru_wa', 'lru_ba', 'lru_wx', 'lru_bx', 'lru_lambda', 'lru_out_norm', 'ssd_conv_w', 'ssd_conv_b', 'ssd_dt_bias', 'ssd_a_log', 'ssd_d', 'ssd_out_norm', 'w_out', 'post_mix_norm', 'pre_ffn_norm', 'w_gate', 'w_up', 'w_down', 'post_ffn_norm']
TWIN_WEIGHTS = ['pre_mix_norm', 'w_in', 'lru_conv_w', 'lru_conv_b', 'lru_wa', 'lru_ba', 'lru_wx', 'lru_bx', 'lru_lambda', 'lru_out_norm', 'ssd_conv_w', 'ssd_conv_b', 'ssd_dt_bias', 'ssd_a_log', 'ssd_d', 'ssd_out_norm', 'w_out', 'post_mix_norm', 'pre_ffn_norm', 'w_gate', 'w_up', 'w_down', 'post_ffn_norm']
TWIN_DIFF_INPUT = 'x'
TWIN_INPUTS = ['x', 'pre_mix_norm', 'w_in', 'lru_conv_w', 'lru_conv_b', 'lru_wa', 'lru_ba', 'lru_wx', 'lru_bx', 'lru_lambda', 'lru_out_norm', 'ssd_conv_w', 'ssd_conv_b', 'ssd_dt_bias', 'ssd_a_log', 'ssd_d', 'ssd_out_norm', 'w_out', 'post_mix_norm', 'pre_ffn_norm', 'w_gate', 'w_up', 'w_down', 'post_ffn_norm', 'loss_target', 'm_pre_mix_norm', 'm_w_in', 'm_lru_conv_w', 'm_lru_conv_b', 'm_lru_wa', 'm_lru_ba', 'm_lru_wx', 'm_lru_bx', 'm_lru_lambda', 'm_lru_out_norm', 'm_ssd_conv_w', 'm_ssd_conv_b', 'm_ssd_dt_bias', 'm_ssd_a_log', 'm_ssd_d', 'm_ssd_out_norm', 'm_w_out', 'm_post_mix_norm', 'm_pre_ffn_norm', 'm_w_gate', 'm_w_up', 'm_w_down', 'm_post_ffn_norm', 'v_pre_mix_norm', 'v_w_in', 'v_lru_conv_w', 'v_lru_conv_b', 'v_lru_wa', 'v_lru_ba', 'v_lru_wx', 'v_lru_bx', 'v_lru_lambda', 'v_lru_out_norm', 'v_ssd_conv_w', 'v_ssd_conv_b', 'v_ssd_dt_bias', 'v_ssd_a_log', 'v_ssd_d', 'v_ssd_out_norm', 'v_w_out', 'v_post_mix_norm', 'v_pre_ffn_norm', 'v_w_gate', 'v_w_up', 'v_w_down', 'v_post_ffn_norm']
TWIN_OUTPUTS = ['loss', 'grad_x', 'grad_pre_mix_norm', 'grad_w_in', 'grad_lru_conv_w', 'grad_lru_conv_b', 'grad_lru_wa', 'grad_lru_ba', 'grad_lru_wx', 'grad_lru_bx', 'grad_lru_lambda', 'grad_lru_out_norm', 'grad_ssd_conv_w', 'grad_ssd_conv_b', 'grad_ssd_dt_bias', 'grad_ssd_a_log', 'grad_ssd_d', 'grad_ssd_out_norm', 'grad_w_out', 'grad_post_mix_norm', 'grad_pre_ffn_norm', 'grad_w_gate', 'grad_w_up', 'grad_w_down', 'grad_post_ffn_norm', 'delta_pre_mix_norm', 'delta_w_in', 'delta_lru_conv_w', 'delta_lru_conv_b', 'delta_lru_wa', 'delta_lru_ba', 'delta_lru_wx', 'delta_lru_bx', 'delta_lru_lambda', 'delta_lru_out_norm', 'delta_ssd_conv_w', 'delta_ssd_conv_b', 'delta_ssd_dt_bias', 'delta_ssd_a_log', 'delta_ssd_d', 'delta_ssd_out_norm', 'delta_w_out', 'delta_post_mix_norm', 'delta_pre_ffn_norm', 'delta_w_gate', 'delta_w_up', 'delta_w_down', 'delta_post_ffn_norm', 'new_m_pre_mix_norm', 'new_m_w_in', 'new_m_lru_conv_w', 'new_m_lru_conv_b', 'new_m_lru_wa', 'new_m_lru_ba', 'new_m_lru_wx', 'new_m_lru_bx', 'new_m_lru_lambda', 'new_m_lru_out_norm', 'new_m_ssd_conv_w', 'new_m_ssd_conv_b', 'new_m_ssd_dt_bias', 'new_m_ssd_a_log', 'new_m_ssd_d', 'new_m_ssd_out_norm', 'new_m_w_out', 'new_m_post_mix_norm', 'new_m_pre_ffn_norm', 'new_m_w_gate', 'new_m_w_up', 'new_m_w_down', 'new_m_post_ffn_norm', 'new_v_pre_mix_norm', 'new_v_w_in', 'new_v_lru_conv_w', 'new_v_lru_conv_b', 'new_v_lru_wa', 'new_v_lru_ba', 'new_v_lru_wx', 'new_v_lru_bx', 'new_v_lru_lambda', 'new_v_lru_out_norm', 'new_v_ssd_conv_w', 'new_v_ssd_conv_b', 'new_v_ssd_dt_bias', 'new_v_ssd_a_log', 'new_v_ssd_d', 'new_v_ssd_out_norm', 'new_v_w_out', 'new_v_post_mix_norm', 'new_v_pre_ffn_norm', 'new_v_w_gate', 'new_v_w_up', 'new_v_w_down', 'new_v_post_ffn_norm']
TWIN_LEAF_KINDS = {'loss': 'loss', 'grad_x': 'grad_x', 'grad_pre_mix_norm': 'grad_w', 'grad_w_in': 'grad_w', 'grad_lru_conv_w': 'grad_w', 'grad_lru_conv_b': 'grad_w', 'grad_lru_wa': 'grad_w', 'grad_lru_ba': 'grad_w', 'grad_lru_wx': 'grad_w', 'grad_lru_bx': 'grad_w', 'grad_lru_lambda': 'grad_w', 'grad_lru_out_norm': 'grad_w', 'grad_ssd_conv_w': 'grad_w', 'grad_ssd_conv_b': 'grad_w', 'grad_ssd_dt_bias': 'grad_w', 'grad_ssd_a_log': 'grad_w', 'grad_ssd_d': 'grad_w', 'grad_ssd_out_norm': 'grad_w', 'grad_w_out': 'grad_w', 'grad_post_mix_norm': 'grad_w', 'grad_pre_ffn_norm': 'grad_w', 'grad_w_gate': 'grad_w', 'grad_w_up': 'grad_w', 'grad_w_down': 'grad_w', 'grad_post_ffn_norm': 'grad_w', 'delta_pre_mix_norm': 'delta_w', 'delta_w_in': 'delta_w', 'delta_lru_conv_w': 'delta_w', 'delta_lru_conv_b': 'delta_w', 'delta_lru_wa': 'delta_w', 'delta_lru_ba': 'delta_w', 'delta_lru_wx': 'delta_w', 'delta_lru_bx': 'delta_w', 'delta_lru_lambda': 'delta_w', 'delta_lru_out_norm': 'delta_w', 'delta_ssd_conv_w': 'delta_w', 'delta_ssd_conv_b': 'delta_w', 'delta_ssd_dt_bias': 'delta_w', 'delta_ssd_a_log': 'delta_w', 'delta_ssd_d': 'delta_w', 'delta_ssd_out_norm': 'delta_w', 'delta_w_out': 'delta_w', 'delta_post_mix_norm': 'delta_w', 'delta_pre_ffn_norm': 'delta_w', 'delta_w_gate': 'delta_w', 'delta_w_up': 'delta_w', 'delta_w_down': 'delta_w', 'delta_post_ffn_norm': 'delta_w', 'new_m_pre_mix_norm': 'new_m', 'new_m_w_in': 'new_m', 'new_m_lru_conv_w': 'new_m', 'new_m_lru_conv_b': 'new_m', 'new_m_lru_wa': 'new_m', 'new_m_lru_ba': 'new_m', 'new_m_lru_wx': 'new_m', 'new_m_lru_bx': 'new_m', 'new_m_lru_lambda': 'new_m', 'new_m_lru_out_norm': 'new_m', 'new_m_ssd_conv_w': 'new_m', 'new_m_ssd_conv_b': 'new_m', 'new_m_ssd_dt_bias': 'new_m', 'new_m_ssd_a_log': 'new_m', 'new_m_ssd_d': 'new_m', 'new_m_ssd_out_norm': 'new_m', 'new_m_w_out': 'new_m', 'new_m_post_mix_norm': 'new_m', 'new_m_pre_ffn_norm': 'new_m', 'new_m_w_gate': 'new_m', 'new_m_w_up': 'new_m', 'new_m_w_down': 'new_m', 'new_m_post_ffn_norm': 'new_m', 'new_v_pre_mix_norm': 'new_v', 'new_v_w_in': 'new_v', 'new_v_lru_conv_w': 'new_v', 'new_v_lru_conv_b': 'new_v', 'new_v_lru_wa': 'new_v', 'new_v_lru_ba': 'new_v', 'new_v_lru_wx': 'new_v', 'new_v_lru_bx': 'new_v', 'new_v_lru_lambda': 'new_v', 'new_v_lru_out_norm': 'new_v', 'new_v_ssd_conv_w': 'new_v', 'new_v_ssd_conv_b': 'new_v', 'new_v_ssd_dt_bias': 'new_v', 'new_v_ssd_a_log': 'new_v', 'new_v_ssd_d': 'new_v', 'new_v_ssd_out_norm': 'new_v', 'new_v_w_out': 'new_v', 'new_v_post_mix_norm': 'new_v', 'new_v_pre_ffn_norm': 'new_v', 'new_v_w_gate': 'new_v', 'new_v_w_up': 'new_v', 'new_v_w_down': 'new_v', 'new_v_post_ffn_norm': 'new_v'}


def _forward(args):
    return _fwd_reference(*[args[k] for k in FWD_PARAMS])


def _output_shape():
    out = _jax.eval_shape(lambda: _forward(_fwd_setup_inputs(0)))
    return out.shape, out.dtype

N_MICROBATCH = 1
ADAM_LR = 0.001
ADAM_B1 = 0.9
ADAM_B2 = 0.999
ADAM_EPS = 1e-08
ADAM_WD = 0.01
ADAM_STEP = 10
PER_EXAMPLE_BATCH_AXIS = {'x': 0, 'loss_target': 0}
SHARED_INPUTS = []
_WEIGHT_DTYPES = {'pre_mix_norm': _jnp.float32, 'w_in': _jnp.float32, 'lru_conv_w': _jnp.float32, 'lru_conv_b': _jnp.float32, 'lru_wa': _jnp.float32, 'lru_ba': _jnp.float32, 'lru_wx': _jnp.float32, 'lru_bx': _jnp.float32, 'lru_lambda': _jnp.float32, 'lru_out_norm': _jnp.float32, 'ssd_conv_w': _jnp.float32, 'ssd_conv_b': _jnp.float32, 'ssd_dt_bias': _jnp.float32, 'ssd_a_log': _jnp.float32, 'ssd_d': _jnp.float32, 'ssd_out_norm': _jnp.float32, 'w_out': _jnp.float32, 'post_mix_norm': _jnp.float32, 'pre_ffn_norm': _jnp.float32, 'w_gate': _jnp.float32, 'w_up': _jnp.float32, 'w_down': _jnp.float32, 'post_ffn_norm': _jnp.float32}
MOMENT_SCALE = {'pre_mix_norm': 9.339614e-01, 'w_in': 4.586736e-01, 'lru_conv_w': 6.637143e-01, 'lru_conv_b': 1.545462e+01, 'lru_wa': 4.428030e-01, 'lru_ba': 3.198983e-01, 'lru_wx': 8.358611e-01, 'lru_bx': 1.870127e-01, 'lru_lambda': 4.228967e-01, 'lru_out_norm': 8.809542e-01, 'ssd_conv_w': 4.981860e-01, 'ssd_conv_b': 1.201663e+00, 'ssd_dt_bias': 1.196631e+00, 'ssd_a_log': 1.202812e+00, 'ssd_d': 2.671061e+00, 'ssd_out_norm': 7.843822e-01, 'w_out': 1.168503e+00, 'post_mix_norm': 6.400072e+01, 'pre_ffn_norm': 1.047587e+00, 'w_gate': 3.525331e-01, 'w_up': 5.672271e-01, 'w_down': 9.318513e-01, 'post_ffn_norm': 6.401656e+01}


def _to_microbatches(a, axis):
    t = _jnp.moveaxis(a, axis, 0)
    t = t.reshape((N_MICROBATCH, t.shape[0] // N_MICROBATCH) + t.shape[1:])
    return _jnp.moveaxis(t, 1, axis + 1)


def setup_inputs(seed: int = 0) -> dict:
    inp = _fwd_setup_inputs(seed)
    key = _jax.random.fold_in(_jax.random.key(seed), 7919)
    shape, _ = _output_shape()
    out = dict(inp)
    out["loss_target"] = _jax.random.normal(_jax.random.fold_in(key, 0), shape, _jnp.float32)
    for i, name in enumerate(TWIN_WEIGHTS):
        w = inp[name].astype(_jnp.float32)
        if MOMENT_SCALE is None:
            s = _jnp.sqrt(_jnp.mean(_jnp.square(w)) + 1e-30)
        else:
            s = MOMENT_SCALE[name]
        km, kv = _jax.random.split(_jax.random.fold_in(key, i + 1))
        out[name] = w
        out["m_" + name] = s * _jax.random.normal(km, w.shape, _jnp.float32)
        out["v_" + name] = (s * s) * _jax.random.uniform(kv, w.shape, _jnp.float32, 0.5, 1.5)
    if N_MICROBATCH > 1:
        for name, axis in PER_EXAMPLE_BATCH_AXIS.items():
            out[name] = _to_microbatches(out[name], axis)
    return {'x': out['x'], 'pre_mix_norm': out['pre_mix_norm'], 'w_in': out['w_in'], 'lru_conv_w': out['lru_conv_w'], 'lru_conv_b': out['lru_conv_b'], 'lru_wa': out['lru_wa'], 'lru_ba': out['lru_ba'], 'lru_wx': out['lru_wx'], 'lru_bx': out['lru_bx'], 'lru_lambda': out['lru_lambda'], 'lru_out_norm': out['lru_out_norm'], 'ssd_conv_w': out['ssd_conv_w'], 'ssd_conv_b': out['ssd_conv_b'], 'ssd_dt_bias': out['ssd_dt_bias'], 'ssd_a_log': out['ssd_a_log'], 'ssd_d': out['ssd_d'], 'ssd_out_norm': out['ssd_out_norm'], 'w_out': out['w_out'], 'post_mix_norm': out['post_mix_norm'], 'pre_ffn_norm': out['pre_ffn_norm'], 'w_gate': out['w_gate'], 'w_up': out['w_up'], 'w_down': out['w_down'], 'post_ffn_norm': out['post_ffn_norm'], 'loss_target': out['loss_target'], 'm_pre_mix_norm': out['m_pre_mix_norm'], 'm_w_in': out['m_w_in'], 'm_lru_conv_w': out['m_lru_conv_w'], 'm_lru_conv_b': out['m_lru_conv_b'], 'm_lru_wa': out['m_lru_wa'], 'm_lru_ba': out['m_lru_ba'], 'm_lru_wx': out['m_lru_wx'], 'm_lru_bx': out['m_lru_bx'], 'm_lru_lambda': out['m_lru_lambda'], 'm_lru_out_norm': out['m_lru_out_norm'], 'm_ssd_conv_w': out['m_ssd_conv_w'], 'm_ssd_conv_b': out['m_ssd_conv_b'], 'm_ssd_dt_bias': out['m_ssd_dt_bias'], 'm_ssd_a_log': out['m_ssd_a_log'], 'm_ssd_d': out['m_ssd_d'], 'm_ssd_out_norm': out['m_ssd_out_norm'], 'm_w_out': out['m_w_out'], 'm_post_mix_norm': out['m_post_mix_norm'], 'm_pre_ffn_norm': out['m_pre_ffn_norm'], 'm_w_gate': out['m_w_gate'], 'm_w_up': out['m_w_up'], 'm_w_down': out['m_w_down'], 'm_post_ffn_norm': out['m_post_ffn_norm'], 'v_pre_mix_norm': out['v_pre_mix_norm'], 'v_w_in': out['v_w_in'], 'v_lru_conv_w': out['v_lru_conv_w'], 'v_lru_conv_b': out['v_lru_conv_b'], 'v_lru_wa': out['v_lru_wa'], 'v_lru_ba': out['v_lru_ba'], 'v_lru_wx': out['v_lru_wx'], 'v_lru_bx': out['v_lru_bx'], 'v_lru_lambda': out['v_lru_lambda'], 'v_lru_out_norm': out['v_lru_out_norm'], 'v_ssd_conv_w': out['v_ssd_conv_w'], 'v_ssd_conv_b': out['v_ssd_conv_b'], 'v_ssd_dt_bias': out['v_ssd_dt_bias'], 'v_ssd_a_log': out['v_ssd_a_log'], 'v_ssd_d': out['v_ssd_d'], 'v_ssd_out_norm': out['v_ssd_out_norm'], 'v_w_out': out['v_w_out'], 'v_post_mix_norm': out['v_post_mix_norm'], 'v_pre_ffn_norm': out['v_pre_ffn_norm'], 'v_w_gate': out['v_w_gate'], 'v_w_up': out['v_w_up'], 'v_w_down': out['v_w_down'], 'v_post_ffn_norm': out['v_post_ffn_norm']}


def _loss(weights, diff, rest, loss_target):
    with _jax.named_scope("forward"):
        args = {**rest, TWIN_DIFF_INPUT: diff, **{k: w.astype(_WEIGHT_DTYPES[k]) for k, w in weights.items()}}
        y = _forward(args)
    with _jax.named_scope("loss_head"):
        err = _jnp.square(y.astype(_jnp.float32) - loss_target)
        return 0.5 * _jnp.sum(_jnp.mean(err, axis=-1)) if err.ndim else 0.5 * err


def _adamw(w, g, m, v):
    m = ADAM_B1 * m + (1.0 - ADAM_B1) * g
    v = ADAM_B2 * v + (1.0 - ADAM_B2) * _jnp.square(g)
    m_hat = m / (1.0 - ADAM_B1 ** ADAM_STEP)
    v_hat = v / (1.0 - ADAM_B2 ** ADAM_STEP)
    delta = -ADAM_LR * (m_hat / (_jnp.sqrt(v_hat) + ADAM_EPS) + ADAM_WD * w)
    return delta, m, v


def reference(x, pre_mix_norm, w_in, lru_conv_w, lru_conv_b, lru_wa, lru_ba, lru_wx, lru_bx, lru_lambda, lru_out_norm, ssd_conv_w, ssd_conv_b, ssd_dt_bias, ssd_a_log, ssd_d, ssd_out_norm, w_out, post_mix_norm, pre_ffn_norm, w_gate, w_up, w_down, post_ffn_norm, loss_target, m_pre_mix_norm, m_w_in, m_lru_conv_w, m_lru_conv_b, m_lru_wa, m_lru_ba, m_lru_wx, m_lru_bx, m_lru_lambda, m_lru_out_norm, m_ssd_conv_w, m_ssd_conv_b, m_ssd_dt_bias, m_ssd_a_log, m_ssd_d, m_ssd_out_norm, m_w_out, m_post_mix_norm, m_pre_ffn_norm, m_w_gate, m_w_up, m_w_down, m_post_ffn_norm, v_pre_mix_norm, v_w_in, v_lru_conv_w, v_lru_conv_b, v_lru_wa, v_lru_ba, v_lru_wx, v_lru_bx, v_lru_lambda, v_lru_out_norm, v_ssd_conv_w, v_ssd_conv_b, v_ssd_dt_bias, v_ssd_a_log, v_ssd_d, v_ssd_out_norm, v_w_out, v_post_mix_norm, v_pre_ffn_norm, v_w_gate, v_w_up, v_w_down, v_post_ffn_norm):
    given = dict(x=x, pre_mix_norm=pre_mix_norm, w_in=w_in, lru_conv_w=lru_conv_w, lru_conv_b=lru_conv_b, lru_wa=lru_wa, lru_ba=lru_ba, lru_wx=lru_wx, lru_bx=lru_bx, lru_lambda=lru_lambda, lru_out_norm=lru_out_norm, ssd_conv_w=ssd_conv_w, ssd_conv_b=ssd_conv_b, ssd_dt_bias=ssd_dt_bias, ssd_a_log=ssd_a_log, ssd_d=ssd_d, ssd_out_norm=ssd_out_norm, w_out=w_out, post_mix_norm=post_mix_norm, pre_ffn_norm=pre_ffn_norm, w_gate=w_gate, w_up=w_up, w_down=w_down, post_ffn_norm=post_ffn_norm, loss_target=loss_target, m_pre_mix_norm=m_pre_mix_norm, m_w_in=m_w_in, m_lru_conv_w=m_lru_conv_w, m_lru_conv_b=m_lru_conv_b, m_lru_wa=m_lru_wa, m_lru_ba=m_lru_ba, m_lru_wx=m_lru_wx, m_lru_bx=m_lru_bx, m_lru_lambda=m_lru_lambda, m_lru_out_norm=m_lru_out_norm, m_ssd_conv_w=m_ssd_conv_w, m_ssd_conv_b=m_ssd_conv_b, m_ssd_dt_bias=m_ssd_dt_bias, m_ssd_a_log=m_ssd_a_log, m_ssd_d=m_ssd_d, m_ssd_out_norm=m_ssd_out_norm, m_w_out=m_w_out, m_post_mix_norm=m_post_mix_norm, m_pre_ffn_norm=m_pre_ffn_norm, m_w_gate=m_w_gate, m_w_up=m_w_up, m_w_down=m_w_down, m_post_ffn_norm=m_post_ffn_norm, v_pre_mix_norm=v_pre_mix_norm, v_w_in=v_w_in, v_lru_conv_w=v_lru_conv_w, v_lru_conv_b=v_lru_conv_b, v_lru_wa=v_lru_wa, v_lru_ba=v_lru_ba, v_lru_wx=v_lru_wx, v_lru_bx=v_lru_bx, v_lru_lambda=v_lru_lambda, v_lru_out_norm=v_lru_out_norm, v_ssd_conv_w=v_ssd_conv_w, v_ssd_conv_b=v_ssd_conv_b, v_ssd_dt_bias=v_ssd_dt_bias, v_ssd_a_log=v_ssd_a_log, v_ssd_d=v_ssd_d, v_ssd_out_norm=v_ssd_out_norm, v_w_out=v_w_out, v_post_mix_norm=v_post_mix_norm, v_pre_ffn_norm=v_pre_ffn_norm, v_w_gate=v_w_gate, v_w_up=v_w_up, v_w_down=v_w_down, v_post_ffn_norm=v_post_ffn_norm)
    weights = {n: given[n] for n in TWIN_WEIGHTS}
    shared = {n: given[n] for n in SHARED_INPUTS}
    per_example = {n: given[n] for n in ['x']}
    grad_fn = _jax.value_and_grad(_loss, argnums=(0, 1))

    def one_microbatch(ex, loss_target):
        ex = dict(ex)
        diff = ex.pop(TWIN_DIFF_INPUT)
        return grad_fn(weights, diff, {**shared, **ex}, loss_target)

    if N_MICROBATCH == 1:
        loss, (grad_w, grad_x) = one_microbatch(per_example, given["loss_target"])
    else:
        def body(carry, xs):
            loss_sum, grad_sum = carry
            l_k, (gw_k, gx_k) = one_microbatch(xs[0], xs[1])
            with _jax.named_scope("update"):
                return (loss_sum + l_k, _jax.tree.map(_jnp.add, grad_sum, gw_k)), gx_k

        init = (_jnp.zeros((), _jnp.float32), _jax.tree.map(_jnp.zeros_like, weights))
        (loss, grad_w), grad_x = _jax.lax.scan(body, init, (per_example, given["loss_target"]))
    with _jax.named_scope("update"):
        delta_w, new_m, new_v = {}, {}, {}
        for n in TWIN_WEIGHTS:
            delta_w[n], new_m[n], new_v[n] = _adamw(weights[n], grad_w[n], given["m_" + n], given["v_" + n])
    return (loss, grad_x, *[grad_w[n] for n in TWIN_WEIGHTS], *[delta_w[n] for n in TWIN_WEIGHTS],
            *[new_m[n] for n in TWIN_WEIGHTS], *[new_v[n] for n in TWIN_WEIGHTS])
```

```python
import functools

import jax
import jax.numpy as jnp
from jax import lax
from jax.experimental import pallas as pl
from jax.experimental.pallas import tpu as pltpu

F32 = jnp.float32
BF16 = jnp.bfloat16
HI = lax.Precision.HIGHEST
EPS = 1e-6
N_DEV = 8
D_MODEL = 1024
LRU_W = 1024
SSD_INNER = 1024
SSD_HEADS = 16
SSD_HEAD_DIM = 64
SSD_STATE = 128
SSD_GROUPS = 2
SSD_CONV_CH = 1536
CHUNK = 128
D_FF = 2816
FF_HALF = D_FF // 2
IN_MAIN = 4608
IN_COLS = 4624
LANE = 128
TT = 256
VMEM_LIMIT = 56 * 1024 * 1024
ADAM_LR, ADAM_B1, ADAM_B2, ADAM_EPS, ADAM_WD, ADAM_STEP = 0.001, 0.9, 0.999, 1e-08, 0.01, 10
PACK_ROWS = 1920
SMALL_ROWS = 160

NT = (((1,), (1,)), ((), ()))
TN = (((0,), (0,)), ((), ()))


def _params(n_grid):
    return pltpu.CompilerParams(dimension_semantics=("arbitrary",) * n_grid, vmem_limit_bytes=VMEM_LIMIT)


def _dot(a, b, dims=None, precision=None):
    if dims is None:
        return jnp.dot(a, b, preferred_element_type=F32, precision=precision)
    return lax.dot_general(a, b, dims, preferred_element_type=F32, precision=precision)


def _sigmoid(x):
    return 1.0 / (1.0 + jnp.exp(-x))


def _softplus(x):
    e = jnp.exp(-jnp.abs(x))
    l1p = jnp.where(e < 1e-3, e * (1.0 - e * (0.5 - e * (1.0 / 3.0))), jnp.log(1.0 + e))
    return jnp.maximum(x, 0.0) + l1p


def _neg_expm1(x):
    series = -x * (1.0 + x * (0.5 + x * (1.0 / 6.0 + x * (1.0 / 24.0))))
    return jnp.where(x > -0.01, series, 1.0 - jnp.exp(x))


_GELU_C = 0.7978845608028654


def _gelu(x):
    t = jnp.tanh(_GELU_C * (x + 0.044715 * x * x * x))
    return 0.5 * x * (1.0 + t), t


def _gelu_grad(x, t):
    return 0.5 * (1.0 + t) + 0.5 * x * (1.0 - t * t) * _GELU_C * (1.0 + 3.0 * 0.044715 * x * x)


def _rms(x):
    return lax.rsqrt(jnp.mean(x * x, axis=-1, keepdims=True) + EPS)


def _rms_bwd(dyn, x, rn):
    return rn * dyn - x * (rn * rn * rn) * jnp.mean(dyn * x, axis=-1, keepdims=True)


def _row(x, r):
    idx = lax.broadcasted_iota(jnp.int32, x.shape, 0)
    return jnp.sum(jnp.where(idx == r, x, 0.0), axis=0, keepdims=True)


def _shift_down(cur, prev8, j):
    s = pltpu.roll(cur, j, 0)
    p = pltpu.roll(prev8, j, 0)
    r8 = lax.broadcasted_iota(jnp.int32, prev8.shape, 0)
    top = jnp.where(r8 < j, p, s[0:8])
    return jnp.concatenate([top, s[8:]], axis=0)


def _shift_up(cur, next8, j):
    n = cur.shape[0]
    s = pltpu.roll(cur, n - j, 0)
    p = pltpu.roll(next8, 8 - j, 0)
    r8 = lax.broadcasted_iota(jnp.int32, next8.shape, 0)
    bot = jnp.where(r8 >= 8 - j, p, s[n - 8:n])
    return jnp.concatenate([s[:n - 8], bot], axis=0)


def _scan_fwd(a, u):
    n = a.shape[0]
    row = lax.broadcasted_iota(jnp.int32, a.shape, 0)
    k = 1
    while k < n:
        ok = row >= k
        a_s = jnp.where(ok, pltpu.roll(a, k, 0), 1.0)
        u_s = jnp.where(ok, pltpu.roll(u, k, 0), 0.0)
        u = a * u_s + u
        a = a * a_s
        k *= 2
    return a, u


def _scan_bwd(b, d):
    n = b.shape[0]
    row = lax.broadcasted_iota(jnp.int32, b.shape, 0)
    k = 1
    while k < n:
        ok = row < n - k
        b_s = jnp.where(ok, pltpu.roll(b, n - k, 0), 1.0)
        d_s = jnp.where(ok, pltpu.roll(d, n - k, 0), 0.0)
        d = b * d_s + d
        b = b * b_s
        k *= 2
    return b, d


def _inproj_fwd(x, g_pre, w_main, w_dt):
    T = x.shape[0]

    def body(x_ref, g_ref, wm_hbm, wd_hbm, h_ref, proj_ref, dtp_ref, wm, wd, sem):
        @pl.when(pl.program_id(0) == 0)
        def _():
            c1 = pltpu.make_async_copy(wm_hbm, wm, sem.at[0])
            c2 = pltpu.make_async_copy(wd_hbm, wd, sem.at[1])
            c1.start()
            c2.start()
            c1.wait()
            c2.wait()

        xv = x_ref[...]
        h = (xv * _rms(xv) * g_ref[...]).astype(BF16)
        h_ref[...] = h
        proj_ref[...] = _dot(h, wm[...])
        dtp_ref[...] = _dot(h, wd[...])

    return pl.pallas_call(
        body, name="inproj_fwd", grid=(T // TT,),
        in_specs=[pl.BlockSpec((TT, D_MODEL), lambda i: (i, 0)),
                  pl.BlockSpec((1, D_MODEL), lambda i: (0, 0)),
                  pl.BlockSpec(memory_space=pl.ANY), pl.BlockSpec(memory_space=pl.ANY)],
        out_specs=[pl.BlockSpec((TT, D_MODEL), lambda i: (i, 0)),
                   pl.BlockSpec((TT, IN_MAIN), lambda i: (i, 0)),
                   pl.BlockSpec((TT, LANE), lambda i: (i, 0))],
        out_shape=[jax.ShapeDtypeStruct((T, D_MODEL), BF16), jax.ShapeDtypeStruct((T, IN_MAIN), F32),
                   jax.ShapeDtypeStruct((T, LANE), F32)],
        scratch_shapes=[pltpu.VMEM((D_MODEL, IN_MAIN), BF16), pltpu.VMEM((D_MODEL, LANE), BF16),
                        pltpu.SemaphoreType.DMA((2,))],
        compiler_params=_params(1),
    )(x, g_pre, w_main, w_dt)


def _inproj_bwd(dp_lru, dp_ssd, ddtp, dx1, x, g_pre, w_main, w_dt):
    T = x.shape[0]

    def body(dl_ref, ds_ref, dd_ref, dx1_ref, x_ref, g_ref, wm_hbm, wd_hbm, gx_ref, dg_ref, wm, wd, sem):
        @pl.when(pl.program_id(0) == 0)
        def _():
            c1 = pltpu.make_async_copy(wm_hbm, wm, sem.at[0])
            c2 = pltpu.make_async_copy(wd_hbm, wd, sem.at[1])
            c1.start()
            c2.start()
            c1.wait()
            c2.wait()
            dg_ref[...] = jnp.zeros_like(dg_ref)

        dh = _dot(dl_ref[...], wm[:, 0:2048], NT)
        dh += _dot(ds_ref[...], wm[:, 2048:IN_MAIN], NT)
        dh += _dot(dd_ref[...], wd[...], NT)
        xv = x_ref[...]
        rn = _rms(xv)
        dg_ref[...] += jnp.sum(dh * xv * rn, axis=0, keepdims=True)
        gx_ref[...] = dx1_ref[...] + _rms_bwd(dh * g_ref[...], xv, rn)

    return pl.pallas_call(
        body, name="inproj_bwd", grid=(T // TT,),
        in_specs=[pl.BlockSpec((TT, 2048), lambda i: (i, 0)),
                  pl.BlockSpec((TT, 2560), lambda i: (i, 0)),
                  pl.BlockSpec((TT, LANE), lambda i: (i, 0)),
                  pl.BlockSpec((TT, D_MODEL), lambda i: (i, 0)),
                  pl.BlockSpec((TT, D_MODEL), lambda i: (i, 0)),
                  pl.BlockSpec((1, D_MODEL), lambda i: (0, 0)),
                  pl.BlockSpec(memory_space=pl.ANY), pl.BlockSpec(memory_space=pl.ANY)],
        out_specs=[pl.BlockSpec((TT, D_MODEL), lambda i: (i, 0)),
                   pl.BlockSpec((1, D_MODEL), lambda i: (0, 0))],
        out_shape=[jax.ShapeDtypeStruct((T, D_MODEL), F32), jax.ShapeDtypeStruct((1, D_MODEL), F32)],
        scratch_shapes=[pltpu.VMEM((D_MODEL, IN_MAIN), BF16), pltpu.VMEM((D_MODEL, LANE), BF16),
                        pltpu.SemaphoreType.DMA((2,))],
        compiler_params=_params(1),
    )(dp_lru, dp_ssd, ddtp, dx1, x, g_pre, w_main, w_dt)


def _lru_gates(lx, wa_ref, wx_ref, ba, bx, lam):
    lxb = lx.astype(BF16)
    pa = jnp.concatenate([_dot(lxb[:, 256 * k:256 * (k + 1)], wa_ref[k]) for k in range(4)], axis=1) + ba
    px = jnp.concatenate([_dot(lxb[:, 256 * k:256 * (k + 1)], wx_ref[k]) for k in range(4)], axis=1) + bx
    r = _sigmoid(pa)
    ig = _sigmoid(px)
    sp = _softplus(-lam)
    log_a = -8.0 * r * sp
    a = jnp.exp(log_a)
    mult = jnp.sqrt(_neg_expm1(2.0 * log_a))
    return r, ig, sp, a, mult


def _lru_fwd(proj, conv_w, conv_b, wa_bd, wx_bd, ba, bx, lam, g_lru):
    T = proj.shape[0]

    def body(cx_ref, gate_ref, cw_ref, cb_ref, wa_ref, wx_ref, ba_ref, bx_ref, lam_ref, g_ref,
             lx_ref, hl_ref, y_ref, tail, hcar):
        @pl.when(pl.program_id(0) == 0)
        def _():
            tail[...] = jnp.zeros_like(tail)
            hcar[...] = jnp.zeros_like(hcar)

        cx = cx_ref[...]
        prev8 = tail[...]
        lx = cb_ref[...] + cw_ref[3:4, :] * cx
        for j in range(1, 4):
            lx += cw_ref[3 - j:4 - j, :] * _shift_down(cx, prev8, j)
        tail[...] = cx[TT - 8:TT]
        lx_ref[...] = lx
        r, ig, sp, a, mult = _lru_gates(lx, wa_ref, wx_ref, ba_ref[...], bx_ref[...], lam_ref[...])
        acum, h0 = _scan_fwd(a, mult * (ig * lx))
        h = h0 + acum * hcar[...]
        hl_ref[...] = h
        hcar[...] = hl_ref[TT - 1:TT, :]
        ge, _ = _gelu(gate_ref[...])
        p = h * ge
        y_ref[...] = (p * _rms(p) * g_ref[...]).astype(BF16)

    vec = pl.BlockSpec((1, LRU_W), lambda i: (0, 0))
    bd = pl.BlockSpec((4, 256, 256), lambda i: (0, 0, 0))
    tile = pl.BlockSpec((TT, LRU_W), lambda i: (i, 0))
    return pl.pallas_call(
        body, name="lru_fwd", grid=(T // TT,),
        in_specs=[tile, pl.BlockSpec((TT, LRU_W), lambda i: (i, 1)),
                  pl.BlockSpec((4, LRU_W), lambda i: (0, 0)), vec, bd, bd, vec, vec, vec, vec],
        out_specs=[tile, tile, tile],
        out_shape=[jax.ShapeDtypeStruct((T, LRU_W), F32), jax.ShapeDtypeStruct((T, LRU_W), F32),
                   jax.ShapeDtypeStruct((T, LRU_W), BF16)],
        scratch_shapes=[pltpu.VMEM((8, LRU_W), F32), pltpu.VMEM((1, LRU_W), F32)],
        compiler_params=_params(1),
    )(proj, proj, conv_w, conv_b, wa_bd, wx_bd, ba, bx, lam, g_lru)


def _lru_bwd(dy, proj, lx, hl, conv_w, wa_bd, wx_bd, ba, bx, lam, g_lru):
    T = proj.shape[0]
    nt = T // TT

    def body(dy_ref, cx_ref, gate_ref, lx_ref, hl_ref, halo_ref, cw_ref, wa_ref, wx_ref, ba_ref, bx_ref,
             lam_ref, g_ref, dp_ref, dpa_ref, dpx_ref, dcw_ref, dcb_ref, dba_ref, dbx_ref, dlam_ref, dg_ref,
             gcar, acar, head):
        i = pl.program_id(0)

        @pl.when(i == 0)
        def _():
            gcar[...] = jnp.zeros_like(gcar)
            acar[...] = jnp.zeros_like(acar)
            head[...] = jnp.zeros_like(head)
            for ref in (dcw_ref, dcb_ref, dba_ref, dbx_ref, dlam_ref, dg_ref):
                ref[...] = jnp.zeros_like(ref)

        lx = lx_ref[...]
        h = hl_ref[...]
        gate = gate_ref[...]
        cx = cx_ref[...]
        lam = lam_ref[...]
        r, ig, sp, a, mult = _lru_gates(lx, wa_ref, wx_ref, ba_ref[...], bx_ref[...], lam)
        ge, th = _gelu(gate)
        p = h * ge
        rn = _rms(p)
        dyv = dy_ref[...]
        dg_ref[...] += jnp.sum(dyv * p * rn, axis=0, keepdims=True)
        dp = _rms_bwd(dyv * g_ref[...], p, rn)
        dp_ref[:, LRU_W:2 * LRU_W] = (dp * h * _gelu_grad(gate, th)).astype(BF16)
        dh = dp * ge
        row = lax.broadcasted_iota(jnp.int32, a.shape, 0)
        b = jnp.where(row == TT - 1, acar[...], pltpu.roll(a, TT - 1, 0))
        bcum, g0 = _scan_bwd(b, dh)
        g = g0 + bcum * gcar[...]
        gcar[...] = _row(g[0:8], 0)
        acar[...] = _row(a[0:8], 0)
        h_last_prev = halo_ref[7:8, :] * (i < nt - 1).astype(F32)
        hprev = jnp.where(row == 0, h_last_prev, pltpu.roll(h, 1, 0))
        da = g * hprev
        dm2 = (g * (ig * lx)) * 0.5 / mult
        dlog_a = da * a - 2.0 * a * a * dm2
        dlam_ref[...] += jnp.sum(-8.0 * r * dlog_a, axis=0, keepdims=True) * (-_sigmoid(-lam))
        dpa = (-8.0 * sp * dlog_a) * r * (1.0 - r)
        dpx = (g * mult * lx) * ig * (1.0 - ig)
        dba_ref[...] += jnp.sum(dpa, axis=0, keepdims=True)
        dbx_ref[...] += jnp.sum(dpx, axis=0, keepdims=True)
        dpab = dpa.astype(BF16)
        dpxb = dpx.astype(BF16)
        dpa_ref[...] = dpab
        dpx_ref[...] = dpxb
        dlx = g * mult * ig + jnp.concatenate(
            [_dot(dpab[:, 256 * k:256 * (k + 1)], wa_ref[k], NT) + _dot(dpxb[:, 256 * k:256 * (k + 1)], wx_ref[k], NT)
             for k in range(4)], axis=1)
        nxt = head[...]
        dcb_ref[...] += jnp.sum(dlx, axis=0, keepdims=True)
        dcx = cw_ref[3:4, :] * dlx
        dcw_ref[3:4, :] += jnp.sum(cx * dlx, axis=0, keepdims=True)
        for j in range(1, 4):
            sh = _shift_up(dlx, nxt, j)
            dcx += cw_ref[3 - j:4 - j, :] * sh
            dcw_ref[3 - j:4 - j, :] += jnp.sum(cx * sh, axis=0, keepdims=True)
        head[...] = dlx[0:8]
        dp_ref[:, 0:LRU_W] = dcx.astype(BF16)

    rev = lambda i: (nt - 1 - i, 0)
    vec = pl.BlockSpec((1, LRU_W), lambda i: (0, 0))
    bd = pl.BlockSpec((4, 256, 256), lambda i: (0, 0, 0))
    tile = pl.BlockSpec((TT, LRU_W), rev)
    halo = pl.BlockSpec((8, LRU_W), lambda i: (jnp.maximum((nt - 1 - i) * (TT // 8) - 1, 0), 0))
    cw = pl.BlockSpec((4, LRU_W), lambda i: (0, 0))
    return pl.pallas_call(
        body, name="lru_bwd", grid=(nt,),
        in_specs=[tile, tile, pl.BlockSpec((TT, LRU_W), lambda i: (nt - 1 - i, 1)), tile, tile, halo,
                  cw, bd, bd, vec, vec, vec, vec],
        out_specs=[pl.BlockSpec((TT, 2 * LRU_W), rev), tile, tile, cw, vec, vec, vec, vec, vec],
        out_shape=[jax.ShapeDtypeStruct((T, 2 * LRU_W), BF16), jax.ShapeDtypeStruct((T, LRU_W), BF16),
                   jax.ShapeDtypeStruct((T, LRU_W), BF16), jax.ShapeDtypeStruct((4, LRU_W), F32)]
                  + [jax.ShapeDtypeStruct((1, LRU_W), F32)] * 5,
        scratch_shapes=[pltpu.VMEM((1, LRU_W), F32), pltpu.VMEM((1, LRU_W), F32), pltpu.VMEM((8, LRU_W), F32)],
        compiler_params=_params(1),
    )(dy, proj, proj, lx, hl, hl, conv_w, wa_bd, wx_bd, ba, bx, lam, g_lru)


def _ssd_chunk_terms(xc, dtp, bias, alog, expand):
    sg = _sigmoid(xc)
    xbc = xc * sg
    pre = dtp + bias
    dt = _softplus(pre)
    A = -jnp.exp(alog)
    ri = lax.broadcasted_iota(jnp.int32, (CHUNK, CHUNK), 0)
    ci = lax.broadcasted_iota(jnp.int32, (CHUNK, CHUNK), 1)
    tril = (ri >= ci).astype(F32)
    cs = _dot(tril, dt * A, precision=HI)
    cs_last = _row(cs, CHUNK - 1)
    ecs = jnp.exp(cs)
    dec = jnp.exp(cs_last - cs)
    return dict(sg=sg, xbc=xbc, pre=pre, dt=dt, A=A, cs=cs, csT=cs.T, ecs=ecs, dec=dec, ri=ri, ci=ci,
                dt_e=_dot(dt, expand, precision=HI), ecs_e=_dot(ecs, expand, precision=HI),
                dec_e=_dot(dec, expand, precision=HI))


def _head_lambda(t, h):
    col = jnp.sum(jnp.where(t["ci"] == h, t["cs"], 0.0), axis=1, keepdims=True)
    rowv = jnp.sum(jnp.where(t["ri"] == h, t["csT"], 0.0), axis=0, keepdims=True)
    return jnp.exp(jnp.where(t["ri"] >= t["ci"], col - rowv, -1e30))


def _ssd_fwd(proj, dtp, conv_w, conv_b, dt_bias, a_log, d_e, g_ssd, expand):
    T = proj.shape[0]
    nc = T // CHUNK

    def body(z_ref, xp_ref, dtp_ref, cw_ref, cb_ref, bias_ref, alog_ref, de_ref, g_ref, ex_ref,
             xc_ref, y_ref, yn_ref, sprev_ref, tail, S):
        @pl.when(pl.program_id(0) == 0)
        def _():
            tail[...] = jnp.zeros_like(tail)
            S[...] = jnp.zeros_like(S)

        xp = xp_ref[...]
        prev8 = tail[...]
        xc = cb_ref[...] + cw_ref[3:4, :] * xp
        for j in range(1, 4):
            xc += cw_ref[3 - j:4 - j, :] * _shift_down(xp, prev8, j)
        tail[...] = xp[CHUNK - 8:CHUNK]
        xc_ref[...] = xc
        t = _ssd_chunk_terms(xc, dtp_ref[...], bias_ref[...], alog_ref[...], ex_ref[...])
        xbc = t["xbc"]
        sx = xbc[:, 0:SSD_INNER]
        Bb = xbc[:, SSD_INNER:SSD_INNER + 256].astype(BF16)
        Cb = xbc[:, SSD_INNER + 256:SSD_CONV_CH].astype(BF16)
        X = t["dt_e"] * sx
        lane = lax.broadcasted_iota(jnp.int32, (CHUNK, LANE), 1)
        G = [_dot(Cb[:, 128 * g:128 * (g + 1)], Bb[:, 128 * g:128 * (g + 1)], NT) for g in range(SSD_GROUPS)]
        for k in range(SSD_HEADS // 2):
            Xp = X[:, 128 * k:128 * (k + 1)]
            acc = jnp.zeros((CHUNK, LANE), F32)
            for half in range(2):
                M = (G[k // 4] * _head_lambda(t, 2 * k + half)).astype(BF16)
                Xh = jnp.where((lane >= 64) if half else (lane < 64), Xp, 0.0).astype(BF16)
                acc += _dot(M, Xh)
            y_ref[:, 128 * k:128 * (k + 1)] = acc
        sprev_ref[0] = S[...]
        eL_e = _row(t["ecs_e"], CHUNK - 1)
        Xd = (X * t["dec_e"]).astype(BF16)
        for g in range(SSD_GROUPS):
            sl = slice(512 * g, 512 * (g + 1))
            Sg = S[:, sl]
            y_ref[:, sl] += t["ecs_e"][:, sl] * _dot(Cb[:, 128 * g:128 * (g + 1)], Sg.astype(BF16))
            S[:, sl] = eL_e[:, sl] * Sg + _dot(Bb[:, 128 * g:128 * (g + 1)], Xd[:, sl], TN)
        y = y_ref[...] + de_ref[...] * sx
        y_ref[...] = y
        z = z_ref[...]
        q = y * (z * _sigmoid(z))
        yn_ref[...] = (q * _rms(q) * g_ref[...]).astype(BF16)

    c0 = lambda i: (0, 0)
    return pl.pallas_call(
        body, name="ssd_fwd", grid=(nc,),
        in_specs=[pl.BlockSpec((CHUNK, SSD_INNER), lambda i: (i, 2)),
                  pl.BlockSpec((CHUNK, SSD_CONV_CH), lambda i: (i, 2)),
                  pl.BlockSpec((CHUNK, LANE), lambda i: (i, 0)),
                  pl.BlockSpec((4, SSD_CONV_CH), c0), pl.BlockSpec((1, SSD_CONV_CH), c0),
                  pl.BlockSpec((1, LANE), c0), pl.BlockSpec((1, LANE), c0),
                  pl.BlockSpec((1, SSD_INNER), c0), pl.BlockSpec((1, SSD_INNER), c0),
                  pl.BlockSpec((LANE, SSD_INNER), c0)],
        out_specs=[pl.BlockSpec((CHUNK, SSD_CONV_CH), lambda i: (i, 0)),
                   pl.BlockSpec((CHUNK, SSD_INNER), lambda i: (i, 0)),
                   pl.BlockSpec((CHUNK, SSD_INNER), lambda i: (i, 0)),
                   pl.BlockSpec((1, SSD_STATE, SSD_INNER), lambda i: (i, 0, 0))],
        out_shape=[jax.ShapeDtypeStruct((T, SSD_CONV_CH), F32), jax.ShapeDtypeStruct((T, SSD_INNER), F32),
                   jax.ShapeDtypeStruct((T, SSD_INNER), BF16),
                   jax.ShapeDtypeStruct((nc, SSD_STATE, SSD_INNER), F32)],
        scratch_shapes=[pltpu.VMEM((8, SSD_CONV_CH), F32), pltpu.VMEM((SSD_STATE, SSD_INNER), F32)],
        compiler_params=_params(1),
    )(proj, proj, dtp, conv_w, conv_b, dt_bias, a_log, d_e, g_ssd, expand)


def _ssd_bwd(dyn, proj, dtp, xc, y, sprev, conv_w, dt_bias, a_log, d_e, g_ssd, expand):
    T = proj.shape[0]
    nc = T // CHUNK

    def body(dyn_ref, z_ref, xp_ref, dtp_ref, xc_ref, y_ref, sprev_ref, cw_ref, bias_ref, alog_ref, de_ref,
             g_ref, ex_ref, dp_ref, ddtp_ref, dcw_ref, dcb_ref, dbias_ref, dA_ref, dD_ref, dg_ref,
             dS, head, dX_s, dxbc_s):
        @pl.when(pl.program_id(0) == 0)
        def _():
            dS[...] = jnp.zeros_like(dS)
            head[...] = jnp.zeros_like(head)
            for ref in (dcw_ref, dcb_ref, dbias_ref, dA_ref, dD_ref, dg_ref):
                ref[...] = jnp.zeros_like(ref)

        ex = ex_ref[...]
        xc = xc_ref[...]
        t = _ssd_chunk_terms(xc, dtp_ref[...], bias_ref[...], alog_ref[...], ex)
        ri, ci = t["ri"], t["ci"]
        xbc = t["xbc"]
        sx = xbc[:, 0:SSD_INNER]
        Bb = xbc[:, SSD_INNER:SSD_INNER + 256].astype(BF16)
        Cb = xbc[:, SSD_INNER + 256:SSD_CONV_CH].astype(BF16)
        X = t["dt_e"] * sx
        z = z_ref[...]
        sz = _sigmoid(z)
        siluz = z * sz
        yv = y_ref[...]
        q = yv * siluz
        rn = _rms(q)
        dynv = dyn_ref[...]
        dg_ref[...] += jnp.sum(dynv * q * rn, axis=0, keepdims=True)
        dq = _rms_bwd(dynv * g_ref[...], q, rn)
        dp_ref[:, 0:SSD_INNER] = (dq * yv * (sz * (1.0 + z * (1.0 - sz)))).astype(BF16)
        dY = dq * siluz
        dD_ref[...] += jnp.sum(dY * sx, axis=0, keepdims=True)
        dYb = dY.astype(BF16)
        lane = lax.broadcasted_iota(jnp.int32, (CHUNK, LANE), 1)
        dcs = jnp.zeros((CHUNK, CHUNK), F32)
        dcsT = jnp.zeros((CHUNK, CHUNK), F32)
        Xb = X.astype(BF16)
        for g in range(SSD_GROUPS):
            Bg = Bb[:, 128 * g:128 * (g + 1)]
            Cg = Cb[:, 128 * g:128 * (g + 1)]
            G = _dot(Cg, Bg, NT)
            dGsum = jnp.zeros((CHUNK, CHUNK), F32)
            for k in range(4 * g, 4 * g + 4):
                Xp = Xb[:, 128 * k:128 * (k + 1)]
                dYp = dY[:, 128 * k:128 * (k + 1)]
                dXp = jnp.zeros((CHUNK, LANE), F32)
                for half in range(2):
                    h = 2 * k + half
                    lam = _head_lambda(t, h)
                    M = G * lam
                    dYh = jnp.where((lane >= 64) if half else (lane < 64), dYp, 0.0).astype(BF16)
                    dM = _dot(dYh, Xp, NT)
                    W = dM * M
                    dcs += jnp.where(ci == h, jnp.sum(W, axis=1, keepdims=True), 0.0)
                    dcsT += jnp.where(ri == h, jnp.sum(W, axis=0, keepdims=True), 0.0)
                    dGsum += dM * lam
                    dXp += _dot(M.astype(BF16), dYh, TN)
                dX_s[:, 128 * k:128 * (k + 1)] = dXp
            dGb = dGsum.astype(BF16)
            dxbc_s[:, SSD_INNER + 256 + 128 * g:SSD_INNER + 256 + 128 * (g + 1)] = _dot(dGb, Bg)
            dxbc_s[:, SSD_INNER + 128 * g:SSD_INNER + 128 * (g + 1)] = _dot(dGb, Cg, TN)
        dcs = dcs - dcsT.T
        Sp = sprev_ref[0]
        dSv = dS[...]
        ecs_e, dec_e = t["ecs_e"], t["dec_e"]
        eL_e = _row(ecs_e, CHUNK - 1)
        dYe = dY * ecs_e
        dYeb = dYe.astype(BF16)
        Xd = X * dec_e
        Xdb = Xd.astype(BF16)
        for g in range(SSD_GROUPS):
            sl = slice(512 * g, 512 * (g + 1))
            Bg = Bb[:, 128 * g:128 * (g + 1)]
            Cg = Cb[:, 128 * g:128 * (g + 1)]
            Spb = Sp[:, sl].astype(BF16)
            dSb = dSv[:, sl].astype(BF16)
            CS = _dot(Cg, Spb)
            BS = _dot(Bg, dSb)
            dxbc_s[:, SSD_INNER + 256 + 128 * g:SSD_INNER + 256 + 128 * (g + 1)] += _dot(dYeb[:, sl], Spb, NT)
            dxbc_s[:, SSD_INNER + 128 * g:SSD_INNER + 128 * (g + 1)] += _dot(Xdb[:, sl], dSb, NT)
            dS[:, sl] = eL_e[:, sl] * dSv[:, sl] + _dot(Cg, dYeb[:, sl], TN)
            dX_s[:, sl] += dec_e[:, sl] * BS
            dcs += _dot(dYe[:, sl] * CS, ex[:, sl], NT, precision=HI)
            tdec = _dot(X[:, sl] * BS, ex[:, sl], NT, precision=HI) * t["dec"]
            dcs -= tdec
            last = jnp.sum(tdec, axis=0, keepdims=True)
            last += jnp.sum(_dot(Sp[:, sl] * dSv[:, sl], ex[:, sl], NT, precision=HI), axis=0, keepdims=True) \
                * _row(t["ecs"], CHUNK - 1)
            dcs += jnp.where(ri == CHUNK - 1, last, 0.0)
        triu = (ci >= ri).astype(F32)
        da = _dot(triu, dcs, precision=HI)
        dX = dX_s[...]
        ddt = da * t["A"] + _dot(dX * sx, ex, NT, precision=HI)
        dA_ref[...] += jnp.sum(da * t["dt"], axis=0, keepdims=True)
        ddtp = ddt * _sigmoid(t["pre"])
        dbias_ref[...] += jnp.sum(ddtp, axis=0, keepdims=True)
        ddtp_ref[...] = ddtp.astype(BF16)
        dxbc_s[:, 0:SSD_INNER] = dX * t["dt_e"] + de_ref[...] * dY
        sg = t["sg"]
        dxc = dxbc_s[...] * (sg * (1.0 + xc * (1.0 - sg)))
        xp = xp_ref[...]
        nxt = head[...]
        dcb_ref[...] += jnp.sum(dxc, axis=0, keepdims=True)
        dpre = cw_ref[3:4, :] * dxc
        dcw_ref[3:4, :] += jnp.sum(xp * dxc, axis=0, keepdims=True)
        for j in range(1, 4):
            sh = _shift_up(dxc, nxt, j)
            dpre += cw_ref[3 - j:4 - j, :] * sh
            dcw_ref[3 - j:4 - j, :] += jnp.sum(xp * sh, axis=0, keepdims=True)
        head[...] = dxc[0:8]
        dp_ref[:, SSD_INNER:SSD_INNER + SSD_CONV_CH] = dpre.astype(BF16)

    c0 = lambda i: (0, 0)
    rev = lambda i: (nc - 1 - i, 0)
    return pl.pallas_call(
        body, name="ssd_bwd", grid=(nc,),
        in_specs=[pl.BlockSpec((CHUNK, SSD_INNER), rev),
                  pl.BlockSpec((CHUNK, SSD_INNER), lambda i: (nc - 1 - i, 2)),
                  pl.BlockSpec((CHUNK, SSD_CONV_CH), lambda i: (nc - 1 - i, 2)),
                  pl.BlockSpec((CHUNK, LANE), rev),
                  pl.BlockSpec((CHUNK, SSD_CONV_CH), rev),
                  pl.BlockSpec((CHUNK, SSD_INNER), rev),
                  pl.BlockSpec((1, SSD_STATE, SSD_INNER), lambda i: (nc - 1 - i, 0, 0)),
                  pl.BlockSpec((4, SSD_CONV_CH), c0), pl.BlockSpec((1, LANE), c0), pl.BlockSpec((1, LANE), c0),
                  pl.BlockSpec((1, SSD_INNER), c0), pl.BlockSpec((1, SSD_INNER), c0),
                  pl.BlockSpec((LANE, SSD_INNER), c0)],
        out_specs=[pl.BlockSpec((CHUNK, 2560), rev), pl.BlockSpec((CHUNK, LANE), rev),
                   pl.BlockSpec((4, SSD_CONV_CH), c0), pl.BlockSpec((1, SSD_CONV_CH), c0),
                   pl.BlockSpec((1, LANE), c0), pl.BlockSpec((1, LANE), c0),
                   pl.BlockSpec((1, SSD_INNER), c0), pl.BlockSpec((1, SSD_INNER), c0)],
        out_shape=[jax.ShapeDtypeStruct((T, 2560), BF16), jax.ShapeDtypeStruct((T, LANE), BF16),
                   jax.ShapeDtypeStruct((4, SSD_CONV_CH), F32), jax.ShapeDtypeStruct((1, SSD_CONV_CH), F32),
                   jax.ShapeDtypeStruct((1, LANE), F32), jax.ShapeDtypeStruct((1, LANE), F32),
                   jax.ShapeDtypeStruct((1, SSD_INNER), F32), jax.ShapeDtypeStruct((1, SSD_INNER), F32)],
        scratch_shapes=[pltpu.VMEM((SSD_STATE, SSD_INNER), F32), pltpu.VMEM((8, SSD_CONV_CH), F32),
                        pltpu.VMEM((CHUNK, SSD_INNER), F32), pltpu.VMEM((CHUNK, SSD_CONV_CH), F32)],
        compiler_params=_params(1),
    )(dyn, proj, proj, dtp, xc, y, sprev, conv_w, dt_bias, a_log, d_e, g_ssd, expand)


def _outproj_fwd(x, y_lru, y_ssd, w_out, g_pm, g_pf):
    T = x.shape[0]

    def body(x_ref, yl_ref, ys_ref, wo_ref, gpm_ref, gpf_ref, mix_ref, x1_ref, h2_ref):
        mix = _dot(yl_ref[...], wo_ref[0:LRU_W, :]) + _dot(ys_ref[...], wo_ref[LRU_W:2 * LRU_W, :])
        mix_ref[...] = mix
        x1 = x_ref[...] + mix * _rms(mix) * gpm_ref[...]
        x1_ref[...] = x1
        h2_ref[...] = (x1 * _rms(x1) * gpf_ref[...]).astype(BF16)

    tile = pl.BlockSpec((TT, D_MODEL), lambda i: (i, 0))
    vec = pl.BlockSpec((1, D_MODEL), lambda i: (0, 0))
    return pl.pallas_call(
        body, name="outproj_fwd", grid=(T // TT,),
        in_specs=[tile, tile, tile, pl.BlockSpec((2 * LRU_W, D_MODEL), lambda i: (0, 0)), vec, vec],
        out_specs=[tile, tile, tile],
        out_shape=[jax.ShapeDtypeStruct((T, D_MODEL), F32), jax.ShapeDtypeStruct((T, D_MODEL), F32),
                   jax.ShapeDtypeStruct((T, D_MODEL), BF16)],
        compiler_params=_params(1),
    )(x, y_lru, y_ssd, w_out, g_pm, g_pf)


def _ffn_fwd_bwd(x1, h2, target, w_gate, w_up, w_down, g_pf, g_ff):
    T = x1.shape[0]

    def body(x1_ref, h2_ref, tg_ref, wg_hbm, wu_hbm, wd_hbm, gpf_ref, gff_ref,
             dx1_ref, act_ref, df_ref, dgt_ref, dup_ref, dgpf_ref, dgff_ref, loss_ref,
             wg, wu, wd, gt_s, up_s, sem):
        @pl.when(pl.program_id(0) == 0)
        def _():
            cps = [pltpu.make_async_copy(s, d, sem.at[n]) for n, (s, d) in
                   enumerate(((wg_hbm, wg), (wu_hbm, wu), (wd_hbm, wd)))]
            for c in cps:
                c.start()
            for c in cps:
                c.wait()
            for ref in (dgpf_ref, dgff_ref, loss_ref):
                ref[...] = jnp.zeros_like(ref)

        h2 = h2_ref[...]
        f = jnp.zeros((TT, D_MODEL), F32)
        for c in range(2):
            sl = slice(FF_HALF * c, FF_HALF * (c + 1))
            gt = _dot(h2, wg[:, sl])
            up = _dot(h2, wu[:, sl])
            gt_s[:, sl] = gt
            up_s[:, sl] = up
            act = (gt * _sigmoid(gt) * up).astype(BF16)
            act_ref[:, sl] = act
            f += _dot(act, wd[sl, :])
        x1 = x1_ref[...]
        rnf = _rms(f)
        e = x1 + f * rnf * gff_ref[...] - tg_ref[...]
        part = 0.5 * jnp.sum(jnp.sum(e * e, axis=1, keepdims=True), axis=0, keepdims=True) * (1.0 / D_MODEL)
        lane = lax.broadcasted_iota(jnp.int32, (1, LANE), 1)
        loss_ref[...] += jnp.where(lane == 0, part, 0.0)
        dx2 = e * (1.0 / D_MODEL)
        dgff_ref[...] += jnp.sum(dx2 * f * rnf, axis=0, keepdims=True)
        df = _rms_bwd(dx2 * gff_ref[...], f, rnf).astype(BF16)
        df_ref[...] = df
        dh2 = jnp.zeros((TT, D_MODEL), F32)
        for c in range(2):
            sl = slice(FF_HALF * c, FF_HALF * (c + 1))
            dact = _dot(df, wd[sl, :], NT)
            gt = gt_s[:, sl]
            up = up_s[:, sl]
            sg = _sigmoid(gt)
            dgt = (dact * up * (sg * (1.0 + gt * (1.0 - sg)))).astype(BF16)
            dup = (dact * gt * sg).astype(BF16)
            dgt_ref[:, sl] = dgt
            dup_ref[:, sl] = dup
            dh2 += _dot(dgt, wg[:, sl], NT) + _dot(dup, wu[:, sl], NT)
        rn2 = _rms(x1)
        dgpf_ref[...] += jnp.sum(dh2 * x1 * rn2, axis=0, keepdims=True)
        dx1_ref[...] = dx2 + _rms_bwd(dh2 * gpf_ref[...], x1, rn2)

    tile = pl.BlockSpec((TT, D_MODEL), lambda i: (i, 0))
    wide = pl.BlockSpec((TT, D_FF), lambda i: (i, 0))
    vec = pl.BlockSpec((1, D_MODEL), lambda i: (0, 0))
    hbm = pl.BlockSpec(memory_space=pl.ANY)
    return pl.pallas_call(
        body, name="ffn_fwd_bwd", grid=(T // TT,),
        in_specs=[tile, tile, tile, hbm, hbm, hbm, vec, vec],
        out_specs=[tile, wide, tile, wide, wide, vec, vec, pl.BlockSpec((1, LANE), lambda i: (0, 0))],
        out_shape=[jax.ShapeDtypeStruct((T, D_MODEL), F32), jax.ShapeDtypeStruct((T, D_FF), BF16),
                   jax.ShapeDtypeStruct((T, D_MODEL), BF16), jax.ShapeDtypeStruct((T, D_FF), BF16),
                   jax.ShapeDtypeStruct((T, D_FF), BF16), jax.ShapeDtypeStruct((1, D_MODEL), F32),
                   jax.ShapeDtypeStruct((1, D_MODEL), F32), jax.ShapeDtypeStruct((1, LANE), F32)],
        scratch_shapes=[pltpu.VMEM((D_MODEL, D_FF), BF16), pltpu.VMEM((D_MODEL, D_FF), BF16),
                        pltpu.VMEM((D_FF, D_MODEL), BF16), pltpu.VMEM((TT, D_FF), F32),
                        pltpu.VMEM((TT, D_FF), F32), pltpu.SemaphoreType.DMA((3,))],
        compiler_params=_params(1),
    )(x1, h2, target, w_gate, w_up, w_down, g_pf, g_ff)


def _outproj_bwd(dx1, mix, w_out, g_pm):
    T = dx1.shape[0]

    def body(dx1_ref, mix_ref, wo_ref, gpm_ref, dyl_ref, dys_ref, dmix_ref, dg_ref):
        @pl.when(pl.program_id(0) == 0)
        def _():
            dg_ref[...] = jnp.zeros_like(dg_ref)

        mix = mix_ref[...]
        rn = _rms(mix)
        dx1v = dx1_ref[...]
        dg_ref[...] += jnp.sum(dx1v * mix * rn, axis=0, keepdims=True)
        dmix = _rms_bwd(dx1v * gpm_ref[...], mix, rn).astype(BF16)
        dmix_ref[...] = dmix
        dyl_ref[...] = _dot(dmix, wo_ref[0:LRU_W, :], NT)
        dys_ref[...] = _dot(dmix, wo_ref[LRU_W:2 * LRU_W, :], NT)

    tile = pl.BlockSpec((TT, D_MODEL), lambda i: (i, 0))
    vec = pl.BlockSpec((1, D_MODEL), lambda i: (0, 0))
    return pl.pallas_call(
        body, name="outproj_bwd", grid=(T // TT,),
        in_specs=[tile, tile, pl.BlockSpec((2 * LRU_W, D_MODEL), lambda i: (0, 0)), vec],
        out_specs=[tile, tile, tile, vec],
        out_shape=[jax.ShapeDtypeStruct((T, D_MODEL), F32), jax.ShapeDtypeStruct((T, D_MODEL), F32),
                   jax.ShapeDtypeStruct((T, D_MODEL), BF16), jax.ShapeDtypeStruct((1, D_MODEL), F32)],
        compiler_params=_params(1),
    )(dx1, mix, w_out, g_pm)


def _tn_matmul(a, b, name, tn, tk=512):
    T, M = a.shape
    N = b.shape[1]

    def body(a_ref, b_ref, o_ref):
        @pl.when(pl.program_id(1) == 0)
        def _():
            o_ref[...] = jnp.zeros_like(o_ref)

        o_ref[...] += _dot(a_ref[...].astype(BF16), b_ref[...].astype(BF16), TN)

    return pl.pallas_call(
        body, name=name, grid=(N // tn, T // tk),
        in_specs=[pl.BlockSpec((tk, M), lambda j, k: (k, 0)), pl.BlockSpec((tk, tn), lambda j, k: (k, j))],
        out_specs=pl.BlockSpec((M, tn), lambda j, k: (0, j)),
        out_shape=jax.ShapeDtypeStruct((M, N), F32),
        compiler_params=_params(2),
    )(a, b)


def _tn_blockdiag(a, b, name, tk=512):
    T = a.shape[0]

    def body(a_ref, b_ref, o_ref):
        @pl.when(pl.program_id(1) == 0)
        def _():
            o_ref[...] = jnp.zeros_like(o_ref)

        o_ref[0] += _dot(a_ref[...].astype(BF16), b_ref[...].astype(BF16), TN)

    return pl.pallas_call(
        body, name=name, grid=(4, T // tk),
        in_specs=[pl.BlockSpec((tk, 256), lambda j, k: (k, j)), pl.BlockSpec((tk, 256), lambda j, k: (k, j))],
        out_specs=pl.BlockSpec((1, 256, 256), lambda j, k: (j, 0, 0)),
        out_shape=jax.ShapeDtypeStruct((4, 256, 256), F32),
        compiler_params=_params(2),
    )(a, b)


def _adamw(parts, w, m, v, name, tr):
    P, R, C = parts.shape

    def body(p_ref, w_ref, m_ref, v_ref, g_ref, d_ref, nm_ref, nv_ref):
        g = p_ref[0]
        for k in range(1, P):
            g = g + p_ref[k]
        g_ref[...] = g
        nm = ADAM_B1 * m_ref[...] + (1.0 - ADAM_B1) * g
        nv = ADAM_B2 * v_ref[...] + (1.0 - ADAM_B2) * (g * g)
        nm_ref[...] = nm
        nv_ref[...] = nv
        m_hat = nm / (1.0 - ADAM_B1 ** ADAM_STEP)
        v_hat = nv / (1.0 - ADAM_B2 ** ADAM_STEP)
        d_ref[...] = -ADAM_LR * (m_hat / (jnp.sqrt(v_hat) + ADAM_EPS) + ADAM_WD * w_ref[...])

    blk = pl.BlockSpec((tr, C), lambda i: (i, 0))
    return pl.pallas_call(
        body, name=name, grid=(R // tr,),
        in_specs=[pl.BlockSpec((P, tr, C), lambda i: (0, i, 0)), blk, blk, blk],
        out_specs=[blk, blk, blk, blk],
        out_shape=[jax.ShapeDtypeStruct((R, C), F32)] * 4,
        compiler_params=_params(1),
    )(parts, w, m, v)


def _peer(k):
    x, y, c = lax.axis_index("x"), lax.axis_index("y"), lax.axis_index("c")
    px = x ^ ((k >> 2) & 1)
    py = y ^ ((k >> 1) & 1)
    pc = c ^ (k & 1)
    return (px, py, pc), 4 * px + 2 * py + pc


def _my_block():
    return 4 * lax.axis_index("x") + 2 * lax.axis_index("y") + lax.axis_index("c")


def _all_gather(wpack, cpack):
    def body(w_ref, c_ref, wo_ref, co_ref, send, recv, loc):
        me = _my_block()
        l1 = pltpu.make_async_copy(w_ref, wo_ref.at[me], loc.at[0])
        l2 = pltpu.make_async_copy(c_ref, co_ref.at[me], loc.at[1])
        l1.start()
        l2.start()
        cps = []
        for k in range(1, N_DEV):
            to, _ = _peer(k)
            for n, (src, dst) in enumerate(((w_ref, wo_ref), (c_ref, co_ref))):
                cp = pltpu.make_async_remote_copy(src_ref=src, dst_ref=dst.at[me], send_sem=send.at[n, k - 1],
                                                  recv_sem=recv.at[n, k - 1], device_id=to,
                                                  device_id_type=pl.DeviceIdType.MESH)
                cp.start()
                cps.append(cp)
        for cp in cps:
            cp.wait()
        l1.wait()
        l2.wait()

    hbm = pl.BlockSpec(memory_space=pl.ANY)
    return pl.pallas_call(
        body, name="all_gather_weights",
        in_specs=[hbm, hbm], out_specs=[hbm, hbm],
        out_shape=[jax.ShapeDtypeStruct((N_DEV,) + wpack.shape, wpack.dtype),
                   jax.ShapeDtypeStruct((N_DEV,) + cpack.shape, cpack.dtype)],
        scratch_shapes=[pltpu.SemaphoreType.DMA((2, N_DEV - 1)), pltpu.SemaphoreType.DMA((2, N_DEV - 1)),
                        pltpu.SemaphoreType.DMA((2,))],
    )(wpack, cpack)


def _exchange_grads(slabs, small):
    def body(s_ref, m_ref, so_ref, mo_ref, send, recv, loc):
        me = _my_block()
        l1 = pltpu.make_async_copy(s_ref.at[me], so_ref.at[me], loc.at[0])
        l2 = pltpu.make_async_copy(m_ref, mo_ref.at[me], loc.at[1])
        l1.start()
        l2.start()
        cps = []
        for k in range(1, N_DEV):
            to, blk = _peer(k)
            c1 = pltpu.make_async_remote_copy(src_ref=s_ref.at[blk], dst_ref=so_ref.at[me], send_sem=send.at[0, k - 1],
                                              recv_sem=recv.at[0, k - 1], device_id=to,
                                              device_id_type=pl.DeviceIdType.MESH)
            c2 = pltpu.make_async_remote_copy(src_ref=m_ref, dst_ref=mo_ref.at[me], send_sem=send.at[1, k - 1],
                                              recv_sem=recv.at[1, k - 1], device_id=to,
                                              device_id_type=pl.DeviceIdType.MESH)
            c1.start()
            c2.start()
            cps += [c1, c2]
        for cp in cps:
            cp.wait()
        l1.wait()
        l2.wait()

    hbm = pl.BlockSpec(memory_space=pl.ANY)
    return pl.pallas_call(
        body, name="exchange_grads",
        in_specs=[hbm, hbm], out_specs=[hbm, hbm],
        out_shape=[jax.ShapeDtypeStruct(slabs.shape, slabs.dtype),
                   jax.ShapeDtypeStruct((N_DEV,) + small.shape, small.dtype)],
        scratch_shapes=[pltpu.SemaphoreType.DMA((2, N_DEV - 1)), pltpu.SemaphoreType.DMA((2, N_DEV - 1)),
                        pltpu.SemaphoreType.DMA((2,))],
    )(slabs, small)


BIG = ("w_in", "w_out", "w_gate", "w_up", "w_down")
BIG_SHARD = {"w_in": (1024, 578), "w_out": (256, 1024), "w_gate": (1024, 352), "w_up": (1024, 352),
             "w_down": (352, 1024)}
BIG_SHARD_AXIS = {"w_in": 1, "w_out": 0, "w_gate": 1, "w_up": 1, "w_down": 0}
PACK_USED = sum(a * b for a, b in BIG_SHARD.values())


def _pack_shards(d, dtype):
    flat = jnp.concatenate([d[n].astype(dtype).reshape(-1) for n in BIG])
    flat = jnp.pad(flat, (0, PACK_ROWS * 1024 - PACK_USED))
    return flat.reshape(PACK_ROWS, 1024)


def _unpack_shards(p):
    lead = p.shape[:-2]
    flat = p.reshape(lead + (PACK_ROWS * 1024,))
    out, off = {}, 0
    for n in BIG:
        r, c = BIG_SHARD[n]
        out[n] = flat[..., off:off + r * c].reshape(lead + (r, c))
        off += r * c
    return out


def _join(parts, axis):
    if axis == 0:
        return parts.reshape((-1,) + parts.shape[2:])
    return jnp.transpose(parts, (1, 0, 2)).reshape(parts.shape[1], -1)


def _split(full, axis):
    if axis == 0:
        return full.reshape((N_DEV, full.shape[0] // N_DEV) + full.shape[1:])
    r, c = full.shape
    return jnp.transpose(full.reshape(r, N_DEV, c // N_DEV), (1, 0, 2))


SMALL = (("pre_mix_norm", 1024), ("lru_conv_w", 4096), ("lru_conv_b", 1024), ("lru_wa", 65536), ("lru_ba", 1024),
         ("lru_wx", 65536), ("lru_bx", 1024), ("lru_lambda", 1024), ("lru_out_norm", 1024), ("ssd_conv_w", 6144),
         ("ssd_conv_b", 1536), ("ssd_dt_bias", 16), ("ssd_a_log", 16), ("ssd_d", 16), ("ssd_out_norm", 1024),
         ("post_mix_norm", 1024), ("pre_ffn_norm", 1024), ("post_ffn_norm", 1024), ("loss", 1))
REPLICATED = tuple(n for n, _ in SMALL if n not in ("lru_conv_w", "ssd_conv_w", "loss"))


def _pack_small(d):
    flat = jnp.concatenate([d[n].astype(F32).reshape(-1) for n, _ in SMALL])
    return jnp.pad(flat, (0, SMALL_ROWS * 1024 - flat.shape[0])).reshape(SMALL_ROWS, 1024)


def _unpack_small(p):
    flat = p.reshape(-1)
    out, off = {}, 0
    for n, size in SMALL:
        out[n] = flat[off:off + size]
        off += size
    return out


def _blockdiag4(w):
    z = jnp.zeros((4, 4, 64, 4, 64), w.dtype)
    w4 = w.reshape(4, 4, 64, 64)
    for b in range(4):
        z = z.at[:, b, :, b, :].set(w4[:, b])
    return z.reshape(4, 256, 256)


def _diag_blocks(g):
    g5 = g.reshape(4, 4, 64, 4, 64)
    return jnp.stack([g5[:, b, :, b, :] for b in range(4)], axis=1).reshape(16, 64, 64)


def _local_step(x, target, W, P):
    w_in = W["w_in"]
    w_main = w_in[:, :IN_MAIN]
    w_dt = jnp.pad(w_in[:, IN_MAIN:], ((0, 0), (0, LANE - SSD_HEADS)))
    pad16 = lambda v: jnp.pad(v.reshape(1, SSD_HEADS), ((0, 0), (0, LANE - SSD_HEADS)))
    dt_bias, a_log = pad16(P["ssd_dt_bias"]), pad16(P["ssd_a_log"])
    d_e = jnp.repeat(P["ssd_d"].reshape(SSD_HEADS), SSD_HEAD_DIM).reshape(1, SSD_INNER)
    expand = (jnp.arange(LANE)[:, None] == (jnp.arange(SSD_INNER)[None, :] // SSD_HEAD_DIM)).astype(F32)
    wa_bd = _blockdiag4(P["lru_wa"].astype(BF16))
    wx_bd = _blockdiag4(P["lru_wx"].astype(BF16))
    vec = lambda n: P[n].reshape(1, -1)

    h, proj, dtp = _inproj_fwd(x, vec("pre_mix_norm"), w_main, w_dt)
    lx, hl, y_lru = _lru_fwd(proj, P["lru_conv_w"], vec("lru_conv_b"), wa_bd, wx_bd, vec("lru_ba"), vec("lru_bx"),
                             vec("lru_lambda"), vec("lru_out_norm"))
    xc, y, y_ssd, sprev = _ssd_fwd(proj, dtp, P["ssd_conv_w"], vec("ssd_conv_b"), dt_bias, a_log, d_e,
                                   vec("ssd_out_norm"), expand)
    mix, x1, h2 = _outproj_fwd(x, y_lru, y_ssd, W["w_out"], vec("post_mix_norm"), vec("pre_ffn_norm"))
    dx1, act, df, dgt, dup, dg_pf, dg_ff, loss = _ffn_fwd_bwd(
        x1, h2, target, W["w_gate"], W["w_up"], W["w_down"], vec("pre_ffn_norm"), vec("post_ffn_norm"))
    dy_lru, dy_ssd, dmix, dg_pm = _outproj_bwd(dx1, mix, W["w_out"], vec("post_mix_norm"))
    dp_ssd, ddtp, dcw_s, dcb_s, dbias, dA, dD_e, dg_ssd = _ssd_bwd(
        dy_ssd, proj, dtp, xc, y, sprev, P["ssd_conv_w"], dt_bias, a_log, d_e, vec("ssd_out_norm"), expand)
    dp_lru, dpa, dpx, dcw_l, dcb_l, dba, dbx, dlam, dg_lru = _lru_bwd(
        dy_lru, proj, lx, hl, P["lru_conv_w"], wa_bd, wx_bd, vec("lru_ba"), vec("lru_bx"), vec("lru_lambda"),
        vec("lru_out_norm"))
    grad_x, dg_pre = _inproj_bwd(dp_lru, dp_ssd, ddtp, dx1, x, vec("pre_mix_norm"), w_main, w_dt)

    big = {
        "w_in": jnp.concatenate([_tn_matmul(h, dp_lru, "dw_in_lru", 512), _tn_matmul(h, dp_ssd, "dw_in_ssd", 512),
                                 _tn_matmul(h, ddtp, "dw_in_dt", 128)[:, :SSD_HEADS]], axis=1),
        "w_out": jnp.concatenate([_tn_matmul(y_lru, dmix, "dw_out_lru", 512),
                                  _tn_matmul(y_ssd, dmix, "dw_out_ssd", 512)], axis=0),
        "w_gate": _tn_matmul(h2, dgt, "dw_gate", FF_HALF),
        "w_up": _tn_matmul(h2, dup, "dw_up", FF_HALF),
        "w_down": _tn_matmul(act, df, "dw_down", 256),
    }
    a_neg = -jnp.exp(P["ssd_a_log"].reshape(SSD_HEADS))
    small = {
        "pre_mix_norm": dg_pre, "lru_conv_w": dcw_l, "lru_conv_b": dcb_l,
        "lru_wa": _diag_blocks(_tn_blockdiag(lx, dpa, "dw_lru_a")), "lru_ba": dba,
        "lru_wx": _diag_blocks(_tn_blockdiag(lx, dpx, "dw_lru_x")), "lru_bx": dbx,
        "lru_lambda": dlam, "lru_out_norm": dg_lru, "ssd_conv_w": dcw_s, "ssd_conv_b": dcb_s,
        "ssd_dt_bias": dbias[0, :SSD_HEADS], "ssd_a_log": dA[0, :SSD_HEADS] * a_neg,
        "ssd_d": jnp.sum(dD_e.reshape(SSD_HEADS, SSD_HEAD_DIM), axis=1), "ssd_out_norm": dg_ssd,
        "post_mix_norm": dg_pm, "pre_ffn_norm": dg_pf, "post_ffn_norm": dg_ff, "loss": loss[0, 0:1],
    }
    return grad_x, big, small


def kernel(x, pre_mix_norm, w_in, lru_conv_w, lru_conv_b, lru_wa, lru_ba, lru_wx, lru_bx, lru_lambda, lru_out_norm, ssd_conv_w, ssd_conv_b, ssd_dt_bias, ssd_a_log, ssd_d, ssd_out_norm, w_out, post_mix_norm, pre_ffn_norm, w_gate, w_up, w_down, post_ffn_norm, loss_target, m_pre_mix_norm, m_w_in, m_lru_conv_w, m_lru_conv_b, m_lru_wa, m_lru_ba, m_lru_wx, m_lru_bx, m_lru_lambda, m_lru_out_norm, m_ssd_conv_w, m_ssd_conv_b, m_ssd_dt_bias, m_ssd_a_log, m_ssd_d, m_ssd_out_norm, m_w_out, m_post_mix_norm, m_pre_ffn_norm, m_w_gate, m_w_up, m_w_down, m_post_ffn_norm, v_pre_mix_norm, v_w_in, v_lru_conv_w, v_lru_conv_b, v_lru_wa, v_lru_ba, v_lru_wx, v_lru_bx, v_lru_lambda, v_lru_out_norm, v_ssd_conv_w, v_ssd_conv_b, v_ssd_dt_bias, v_ssd_a_log, v_ssd_d, v_ssd_out_norm, v_w_out, v_post_mix_norm, v_pre_ffn_norm, v_w_gate, v_w_up, v_w_down, v_post_ffn_norm):
    a = dict(locals())
    names = [n for n, _ in SMALL if n != "loss"] + list(BIG)
    w = {n: a[n][0] for n in names}
    m = {n: a["m_" + n][0] for n in names}
    v = {n: a["v_" + n][0] for n in names}

    cpack = jnp.concatenate([w["lru_conv_w"], w["ssd_conv_w"], jnp.zeros((4, 64), F32)], axis=1)
    cpack = jnp.pad(cpack, ((0, 4), (0, 0)))
    wg, cg = _all_gather(_pack_shards(w, BF16), cpack)
    Wfull = {n: _join(p, BIG_SHARD_AXIS[n]) for n, p in _unpack_shards(wg).items()}
    P = {n: w[n] for n in REPLICATED}
    P["lru_conv_w"] = _join(cg[:, 0:4, 0:128], 1)
    P["ssd_conv_w"] = _join(cg[:, 0:4, 128:320], 1)

    grad_x, big, small = _local_step(x[0], loss_target[0], Wfull, P)

    slabs = {n: _split(big[n], BIG_SHARD_AXIS[n]) for n in BIG}
    flat = jnp.concatenate([slabs[n].reshape(N_DEV, -1) for n in BIG], axis=1)
    flat = jnp.pad(flat, ((0, 0), (0, PACK_ROWS * 1024 - PACK_USED))).reshape(N_DEV, PACK_ROWS, 1024)
    got_big, got_small = _exchange_grads(flat, _pack_small(small))

    outs = {}
    res = _adamw(got_big, _pack_shards(w, F32), _pack_shards(m, F32), _pack_shards(v, F32), "adamw_big", 128)
    for kind, r in zip(("grad", "delta", "new_m", "new_v"), res):
        for n, val in _unpack_shards(r).items():
            outs[kind + "_" + n] = val

    zero_extra = {"loss": jnp.zeros((1,), F32)}
    full_conv = {"lru_conv_w": jnp.zeros((4, LRU_W), F32), "ssd_conv_w": jnp.zeros((4, SSD_CONV_CH), F32)}
    rep = lambda d: _pack_small({**{n: d[n] for n in REPLICATED}, **full_conv, **zero_extra})
    res = _adamw(got_small, rep(w), rep(m), rep(v), "adamw_small", SMALL_ROWS)
    g_small = _unpack_small(res[0])
    for kind, r in zip(("grad", "delta", "new_m", "new_v"), res):
        for n, val in _unpack_small(r).items():
            if n in REPLICATED:
                outs[kind + "_" + n] = val.reshape(w[n].shape)

    me = _my_block()
    gl = lax.dynamic_slice(g_small["lru_conv_w"].reshape(4, LRU_W), (0, me * 128), (4, 128))
    gs = lax.dynamic_slice(g_small["ssd_conv_w"].reshape(4, SSD_CONV_CH), (0, me * 192), (4, 192))
    cat = lambda d: jnp.pad(jnp.concatenate([d["lru_conv_w"], d["ssd_conv_w"]], axis=1), ((0, 4), (0, 64)))
    res = _adamw(cat({"lru_conv_w": gl, "ssd_conv_w": gs})[None], cat(w), cat(m), cat(v), "adamw_conv", 8)
    for kind, r in zip(("grad", "delta", "new_m", "new_v"), res):
        outs[kind + "_lru_conv_w"] = r[0:4, 0:128]
        outs[kind + "_ssd_conv_w"] = r[0:4, 128:320]

    order = ["pre_mix_norm", "w_in", "lru_conv_w", "lru_conv_b", "lru_wa", "lru_ba", "lru_wx", "lru_bx", "lru_lambda",
             "lru_out_norm", "ssd_conv_w", "ssd_conv_b", "ssd_dt_bias", "ssd_a_log", "ssd_d", "ssd_out_norm", "w_out",
             "post_mix_norm", "pre_ffn_norm", "w_gate", "w_up", "w_down", "post_ffn_norm"]
    result = [g_small["loss"].reshape(()), grad_x[None]]
    for kind in ("grad", "delta", "new_m", "new_v"):
        result += [outs[kind + "_" + n][None] for n in order]
    return tuple(result)
```

```python
import functools

import jax
import jax.numpy as jnp
from jax import lax
from jax.experimental import pallas as pl
from jax.experimental.pallas import tpu as pltpu

F32 = jnp.float32
BF16 = jnp.bfloat16
HI = lax.Precision.HIGHEST
EPS = 1e-6
N_DEV = 8
D_MODEL = 1024
LRU_W = 1024
SSD_INNER = 1024
SSD_HEADS = 16
SSD_HEAD_DIM = 64
SSD_STATE = 128
SSD_GROUPS = 2
SSD_CONV_CH = 1536
CHUNK = 128
D_FF = 2816
FF_HALF = D_FF // 2
IN_MAIN = 4608
IN_COLS = 4624
LANE = 128
TT = 256
VMEM_LIMIT = 56 * 1024 * 1024
ADAM_LR, ADAM_B1, ADAM_B2, ADAM_EPS, ADAM_WD, ADAM_STEP = 0.001, 0.9, 0.999, 1e-08, 0.01, 10
PACK_ROWS = 1920
SMALL_ROWS = 160

NT = (((1,), (1,)), ((), ()))
TN = (((0,), (0,)), ((), ()))


def _params(n_grid):
    return pltpu.CompilerParams(dimension_semantics=("arbitrary",) * n_grid, vmem_limit_bytes=VMEM_LIMIT)


def _dot(a, b, dims=None, precision=None):
    if dims is None:
        return jnp.dot(a, b, preferred_element_type=F32, precision=precision)
    return lax.dot_general(a, b, dims, preferred_element_type=F32, precision=precision)


def _sigmoid(x):
    return 1.0 / (1.0 + jnp.exp(-x))


def _softplus(x):
    e = jnp.exp(-jnp.abs(x))
    l1p = jnp.where(e < 1e-3, e * (1.0 - e * (0.5 - e * (1.0 / 3.0))), jnp.log(1.0 + e))
    return jnp.maximum(x, 0.0) + l1p


def _neg_expm1(x):
    series = -x * (1.0 + x * (0.5 + x * (1.0 / 6.0 + x * (1.0 / 24.0))))
    return jnp.where(x > -0.01, series, 1.0 - jnp.exp(x))


_GELU_C = 0.7978845608028654


def _gelu(x):
    t = jnp.tanh(_GELU_C * (x + 0.044715 * x * x * x))
    return 0.5 * x * (1.0 + t), t


def _gelu_grad(x, t):
    return 0.5 * (1.0 + t) + 0.5 * x * (1.0 - t * t) * _GELU_C * (1.0 + 3.0 * 0.044715 * x * x)


def _rms(x):
    return lax.rsqrt(jnp.mean(x * x, axis=-1, keepdims=True) + EPS)


def _rms_bwd(dyn, x, rn):
    return rn * dyn - x * (rn * rn * rn) * jnp.mean(dyn * x, axis=-1, keepdims=True)


def _row(x, r):
    idx = lax.broadcasted_iota(jnp.int32, x.shape, 0)
    return jnp.sum(jnp.where(idx == r, x, 0.0), axis=0, keepdims=True)


def _shift_down(cur, prev8, j):
    s = pltpu.roll(cur, j, 0)
    p = pltpu.roll(prev8, j, 0)
    r8 = lax.broadcasted_iota(jnp.int32, prev8.shape, 0)
    top = jnp.where(r8 < j, p, s[0:8])
    return jnp.concatenate([top, s[8:]], axis=0)


def _shift_up(cur, next8, j):
    n = cur.shape[0]
    s = pltpu.roll(cur, n - j, 0)
    p = pltpu.roll(next8, 8 - j, 0)
    r8 = lax.broadcasted_iota(jnp.int32, next8.shape, 0)
    bot = jnp.where(r8 >= 8 - j, p, s[n - 8:n])
    return jnp.concatenate([s[:n - 8], bot], axis=0)


def _scan_fwd(a, u):
    n = a.shape[0]
    row = lax.broadcasted_iota(jnp.int32, a.shape, 0)
    k = 1
    while k < n:
        ok = row >= k
        a_s = jnp.where(ok, pltpu.roll(a, k, 0), 1.0)
        u_s = jnp.where(ok, pltpu.roll(u, k, 0), 0.0)
        u = a * u_s + u
        a = a * a_s
        k *= 2
    return a, u


def _scan_bwd(b, d):
    n = b.shape[0]
    row = lax.broadcasted_iota(jnp.int32, b.shape, 0)
    k = 1
    while k < n:
        ok = row < n - k
        b_s = jnp.where(ok, pltpu.roll(b, n - k, 0), 1.0)
        d_s = jnp.where(ok, pltpu.roll(d, n - k, 0), 0.0)
        d = b * d_s + d
        b = b * b_s
        k *= 2
    return b, d


def _inproj_fwd(x, g_pre, w_main, w_dt):
    T = x.shape[0]

    def body(x_ref, g_ref, wm_hbm, wd_hbm, h_ref, proj_ref, dtp_ref, wm, wd, sem):
        @pl.when(pl.program_id(0) == 0)
        def _():
            c1 = pltpu.make_async_copy(wm_hbm, wm, sem.at[0])
            c2 = pltpu.make_async_copy(wd_hbm, wd, sem.at[1])
            c1.start()
            c2.start()
            c1.wait()
            c2.wait()

        xv = x_ref[...]
        h = (xv * _rms(xv) * g_ref[...]).astype(BF16)
        h_ref[...] = h
        proj_ref[...] = _dot(h, wm[...])
        dtp_ref[...] = _dot(h, wd[...])

    return pl.pallas_call(
        body, name="inproj_fwd", grid=(T // TT,),
        in_specs=[pl.BlockSpec((TT, D_MODEL), lambda i: (i, 0)),
                  pl.BlockSpec((1, D_MODEL), lambda i: (0, 0)),
                  pl.BlockSpec(memory_space=pl.ANY), pl.BlockSpec(memory_space=pl.ANY)],
        out_specs=[pl.BlockSpec((TT, D_MODEL), lambda i: (i, 0)),
                   pl.BlockSpec((TT, IN_MAIN), lambda i: (i, 0)),
                   pl.BlockSpec((TT, LANE), lambda i: (i, 0))],
        out_shape=[jax.ShapeDtypeStruct((T, D_MODEL), BF16), jax.ShapeDtypeStruct((T, IN_MAIN), F32),
                   jax.ShapeDtypeStruct((T, LANE), F32)],
        scratch_shapes=[pltpu.VMEM((D_MODEL, IN_MAIN), BF16), pltpu.VMEM((D_MODEL, LANE), BF16),
                        pltpu.SemaphoreType.DMA((2,))],
        compiler_params=_params(1),
    )(x, g_pre, w_main, w_dt)


def _inproj_bwd(dp_lru, dp_ssd, ddtp, dx1, x, g_pre, w_main, w_dt):
    T = x.shape[0]

    def body(dl_ref, ds_ref, dd_ref, dx1_ref, x_ref, g_ref, wm_hbm, wd_hbm, gx_ref, dg_ref, wm, wd, sem):
        @pl.when(pl.program_id(0) == 0)
        def _():
            c1 = pltpu.make_async_copy(wm_hbm, wm, sem.at[0])
            c2 = pltpu.make_async_copy(wd_hbm, wd, sem.at[1])
            c1.start()
            c2.start()
            c1.wait()
            c2.wait()
            dg_ref[...] = jnp.zeros_like(dg_ref)

        dh = _dot(dl_ref[...], wm[:, 0:2048], NT)
        dh += _dot(ds_ref[...], wm[:, 2048:IN_MAIN], NT)
        dh += _dot(dd_ref[...], wd[...], NT)
        xv = x_ref[...]
        rn = _rms(xv)
        dg_ref[...] += jnp.sum(dh * xv * rn, axis=0, keepdims=True)
        gx_ref[...] = dx1_ref[...] + _rms_bwd(dh * g_ref[...], xv, rn)

    return pl.pallas_call(
        body, name="inproj_bwd", grid=(T // TT,),
        in_specs=[pl.BlockSpec((TT, 2048), lambda i: (i, 0)),
                  pl.BlockSpec((TT, 2560), lambda i: (i, 0)),
                  pl.BlockSpec((TT, LANE), lambda i: (i, 0)),
                  pl.BlockSpec((TT, D_MODEL), lambda i: (i, 0)),
                  pl.BlockSpec((TT, D_MODEL), lambda i: (i, 0)),
                  pl.BlockSpec((1, D_MODEL), lambda i: (0, 0)),
                  pl.BlockSpec(memory_space=pl.ANY), pl.BlockSpec(memory_space=pl.ANY)],
        out_specs=[pl.BlockSpec((TT, D_MODEL), lambda i: (i, 0)),
                   pl.BlockSpec((1, D_MODEL), lambda i: (0, 0))],
        out_shape=[jax.ShapeDtypeStruct((T, D_MODEL), F32), jax.ShapeDtypeStruct((1, D_MODEL), F32)],
        scratch_shapes=[pltpu.VMEM((D_MODEL, IN_MAIN), BF16), pltpu.VMEM((D_MODEL, LANE), BF16),
                        pltpu.SemaphoreType.DMA((2,))],
        compiler_params=_params(1),
    )(dp_lru, dp_ssd, ddtp, dx1, x, g_pre, w_main, w_dt)


def _lru_gates(lx, wa_ref, wx_ref, ba, bx, lam):
    lxb = lx.astype(BF16)
    pa = jnp.concatenate([_dot(lxb[:, 256 * k:256 * (k + 1)], wa_ref[k]) for k in range(4)], axis=1) + ba
    px = jnp.concatenate([_dot(lxb[:, 256 * k:256 * (k + 1)], wx_ref[k]) for k in range(4)], axis=1) + bx
    r = _sigmoid(pa)
    ig = _sigmoid(px)
    sp = _softplus(-lam)
    log_a = -8.0 * r * sp
    a = jnp.exp(log_a)
    mult = jnp.sqrt(_neg_expm1(2.0 * log_a))
    return r, ig, sp, a, mult


def _lru_fwd(proj, conv_w, conv_b, wa_bd, wx_bd, ba, bx, lam, g_lru):
    T = proj.shape[0]

    def body(cx_ref, gate_ref, cw_ref, cb_ref, wa_ref, wx_ref, ba_ref, bx_ref, lam_ref, g_ref,
             lx_ref, hl_ref, y_ref, tail, hcar):
        @pl.when(pl.program_id(0) == 0)
        def _():
            tail[...] = jnp.zeros_like(tail)
            hcar[...] = jnp.zeros_like(hcar)

        cx = cx_ref[...]
        prev8 = tail[...]
        lx = cb_ref[...] + cw_ref[3:4, :] * cx
        for j in range(1, 4):
            lx += cw_ref[3 - j:4 - j, :] * _shift_down(cx, prev8, j)
        tail[...] = cx[TT - 8:TT]
        lx_ref[...] = lx
        r, ig, sp, a, mult = _lru_gates(lx, wa_ref, wx_ref, ba_ref[...], bx_ref[...], lam_ref[...])
        acum, h0 = _scan_fwd(a, mult * (ig * lx))
        h = h0 + acum * hcar[...]
        hl_ref[...] = h
        hcar[...] = hl_ref[TT - 1:TT, :]
        ge, _ = _gelu(gate_ref[...])
        p = h * ge
        y_ref[...] = (p * _rms(p) * g_ref[...]).astype(BF16)

    vec = pl.BlockSpec((1, LRU_W), lambda i: (0, 0))
    bd = pl.BlockSpec((4, 256, 256), lambda i: (0, 0, 0))
    tile = pl.BlockSpec((TT, LRU_W), lambda i: (i, 0))
    return pl.pallas_call(
        body, name="lru_fwd", grid=(T // TT,),
        in_specs=[tile, pl.BlockSpec((TT, LRU_W), lambda i: (i, 1)),
                  pl.BlockSpec((4, LRU_W), lambda i: (0, 0)), vec, bd, bd, vec, vec, vec, vec],
        out_specs=[tile, tile, tile],
        out_shape=[jax.ShapeDtypeStruct((T, LRU_W), F32), jax.ShapeDtypeStruct((T, LRU_W), F32),
                   jax.ShapeDtypeStruct((T, LRU_W), BF16)],
        scratch_shapes=[pltpu.VMEM((8, LRU_W), F32), pltpu.VMEM((1, LRU_W), F32)],
        compiler_params=_params(1),
    )(proj, proj, conv_w, conv_b, wa_bd, wx_bd, ba, bx, lam, g_lru)


def _lru_bwd(dy, proj, lx, hl, conv_w, wa_bd, wx_bd, ba, bx, lam, g_lru):
    T = proj.shape[0]
    nt = T // TT

    def body(dy_ref, cx_ref, gate_ref, lx_ref, hl_ref, halo_ref, cw_ref, wa_ref, wx_ref, ba_ref, bx_ref,
             lam_ref, g_ref, dp_ref, dpa_ref, dpx_ref, dcw_ref, dcb_ref, dba_ref, dbx_ref, dlam_ref, dg_ref,
             gcar, acar, head):
        i = pl.program_id(0)

        @pl.when(i == 0)
        def _():
            gcar[...] = jnp.zeros_like(gcar)
            acar[...] = jnp.zeros_like(acar)
            head[...] = jnp.zeros_like(head)
            for ref in (dcw_ref, dcb_ref, dba_ref, dbx_ref, dlam_ref, dg_ref):
                ref[...] = jnp.zeros_like(ref)

        lx = lx_ref[...]
        h = hl_ref[...]
        gate = gate_ref[...]
        cx = cx_ref[...]
        lam = lam_ref[...]
        r, ig, sp, a, mult = _lru_gates(lx, wa_ref, wx_ref, ba_ref[...], bx_ref[...], lam)
        ge, th = _gelu(gate)
        p = h * ge
        rn = _rms(p)
        dyv = dy_ref[...]
        dg_ref[...] += jnp.sum(dyv * p * rn, axis=0, keepdims=True)
        dp = _rms_bwd(dyv * g_ref[...], p, rn)
        dp_ref[:, LRU_W:2 * LRU_W] = (dp * h * _gelu_grad(gate, th)).astype(BF16)
        dh = dp * ge
        row = lax.broadcasted_iota(jnp.int32, a.shape, 0)
        b = jnp.where(row == TT - 1, acar[...], pltpu.roll(a, TT - 1, 0))
        bcum, g0 = _scan_bwd(b, dh)
        g = g0 + bcum * gcar[...]
        gcar[...] = _row(g[0:8], 0)
        acar[...] = _row(a[0:8], 0)
        h_last_prev = halo_ref[7:8, :] * (i < nt - 1).astype(F32)
        hprev = jnp.where(row == 0, h_last_prev, pltpu.roll(h, 1, 0))
        da = g * hprev
        dm2 = (g * (ig * lx)) * 0.5 / mult
        dlog_a = da * a - 2.0 * a * a * dm2
        dlam_ref[...] += jnp.sum(-8.0 * r * dlog_a, axis=0, keepdims=True) * (-_sigmoid(-lam))
        dpa = (-8.0 * sp * dlog_a) * r * (1.0 - r)
        dpx = (g * mult * lx) * ig * (1.0 - ig)
        dba_ref[...] += jnp.sum(dpa, axis=0, keepdims=True)
        dbx_ref[...] += jnp.sum(dpx, axis=0, keepdims=True)
        dpab = dpa.astype(BF16)
        dpxb = dpx.astype(BF16)
        dpa_ref[...] = dpab
        dpx_ref[...] = dpxb
        dlx = g * mult * ig + jnp.concatenate(
            [_dot(dpab[:, 256 * k:256 * (k + 1)], wa_ref[k], NT) + _dot(dpxb[:, 256 * k:256 * (k + 1)], wx_ref[k], NT)
             for k in range(4)], axis=1)
        nxt = head[...]
        dcb_ref[...] += jnp.sum(dlx, axis=0, keepdims=True)
        dcx = cw_ref[3:4, :] * dlx
        dcw_ref[3:4, :] += jnp.sum(cx * dlx, axis=0, keepdims=True)
        for j in range(1, 4):
            sh = _shift_up(dlx, nxt, j)
            dcx += cw_ref[3 - j:4 - j, :] * sh
            dcw_ref[3 - j:4 - j, :] += jnp.sum(cx * sh, axis=0, keepdims=True)
        head[...] = dlx[0:8]
        dp_ref[:, 0:LRU_W] = dcx.astype(BF16)

    rev = lambda i: (nt - 1 - i, 0)
    vec = pl.BlockSpec((1, LRU_W), lambda i: (0, 0))
    bd = pl.BlockSpec((4, 256, 256), lambda i: (0, 0, 0))
    tile = pl.BlockSpec((TT, LRU_W), rev)
    halo = pl.BlockSpec((8, LRU_W), lambda i: (jnp.maximum((nt - 1 - i) * (TT // 8) - 1, 0), 0))
    cw = pl.BlockSpec((4, LRU_W), lambda i: (0, 0))
    return pl.pallas_call(
        body, name="lru_bwd", grid=(nt,),
        in_specs=[tile, tile, pl.BlockSpec((TT, LRU_W), lambda i: (nt - 1 - i, 1)), tile, tile, halo,
                  cw, bd, bd, vec, vec, vec, vec],
        out_specs=[pl.BlockSpec((TT, 2 * LRU_W), rev), tile, tile, cw, vec, vec, vec, vec, vec],
        out_shape=[jax.ShapeDtypeStruct((T, 2 * LRU_W), BF16), jax.ShapeDtypeStruct((T, LRU_W), BF16),
                   jax.ShapeDtypeStruct((T, LRU_W), BF16), jax.ShapeDtypeStruct((4, LRU_W), F32)]
                  + [jax.ShapeDtypeStruct((1, LRU_W), F32)] * 5,
        scratch_shapes=[pltpu.VMEM((1, LRU_W), F32), pltpu.VMEM((1, LRU_W), F32), pltpu.VMEM((8, LRU_W), F32)],
        compiler_params=_params(1),
    )(dy, proj, proj, lx, hl, hl, conv_w, wa_bd, wx_bd, ba, bx, lam, g_lru)


def _ssd_chunk_terms(xc, dtp, bias, alog, expand):
    sg = _sigmoid(xc)
    xbc = xc * sg
    pre = dtp + bias
    dt = _softplus(pre)
    A = -jnp.exp(alog)
    ri = lax.broadcasted_iota(jnp.int32, (CHUNK, CHUNK), 0)
    ci = lax.broadcasted_iota(jnp.int32, (CHUNK, CHUNK), 1)
    tril = (ri >= ci).astype(F32)
    cs = _dot(tril, dt * A, precision=HI)
    cs_last = _row(cs, CHUNK - 1)
    ecs = jnp.exp(cs)
    dec = jnp.exp(cs_last - cs)
    return dict(sg=sg, xbc=xbc, pre=pre, dt=dt, A=A, cs=cs, csT=cs.T, ecs=ecs, dec=dec, ri=ri, ci=ci,
                dt_e=_dot(dt, expand, precision=HI), ecs_e=_dot(ecs, expand, precision=HI),
                dec_e=_dot(dec, expand, precision=HI))


def _head_lambda(t, h):
    col = jnp.sum(jnp.where(t["ci"] == h, t["cs"], 0.0), axis=1, keepdims=True)
    rowv = jnp.sum(jnp.where(t["ri"] == h, t["csT"], 0.0), axis=0, keepdims=True)
    return jnp.exp(jnp.where(t["ri"] >= t["ci"], col - rowv, -1e30))


def _ssd_fwd(proj, dtp, conv_w, conv_b, dt_bias, a_log, d_e, g_ssd, expand):
    T = proj.shape[0]
    nc = T // CHUNK

    def body(z_ref, xp_ref, dtp_ref, cw_ref, cb_ref, bias_ref, alog_ref, de_ref, g_ref, ex_ref,
             xc_ref, y_ref, yn_ref, sprev_ref, tail, S):
        @pl.when(pl.program_id(0) == 0)
        def _():
            tail[...] = jnp.zeros_like(tail)
            S[...] = jnp.zeros_like(S)

        xp = xp_ref[...]
        prev8 = tail[...]
        xc = cb_ref[...] + cw_ref[3:4, :] * xp
        for j in range(1, 4):
            xc += cw_ref[3 - j:4 - j, :] * _shift_down(xp, prev8, j)
        tail[...] = xp[CHUNK - 8:CHUNK]
        xc_ref[...] = xc
        t = _ssd_chunk_terms(xc, dtp_ref[...], bias_ref[...], alog_ref[...], ex_ref[...])
        xbc = t["xbc"]
        sx = xbc[:, 0:SSD_INNER]
        Bb = xbc[:, SSD_INNER:SSD_INNER + 256].astype(BF16)
        Cb = xbc[:, SSD_INNER + 256:SSD_CONV_CH].astype(BF16)
        X = t["dt_e"] * sx
        lane = lax.broadcasted_iota(jnp.int32, (CHUNK, LANE), 1)
        G = [_dot(Cb[:, 128 * g:128 * (g + 1)], Bb[:, 128 * g:128 * (g + 1)], NT) for g in range(SSD_GROUPS)]
        for k in range(SSD_HEADS // 2):
            Xp = X[:, 128 * k:128 * (k + 1)]
            acc = jnp.zeros((CHUNK, LANE), F32)
            for half in range(2):
                M = (G[k // 4] * _head_lambda(t, 2 * k + half)).astype(BF16)
                Xh = jnp.where((lane >= 64) if half else (lane < 64), Xp, 0.0).astype(BF16)
                acc += _dot(M, Xh)
            y_ref[:, 128 * k:128 * (k + 1)] = acc
        sprev_ref[0] = S[...]
        eL_e = _row(t["ecs_e"], CHUNK - 1)
        Xd = (X * t["dec_e"]).astype(BF16)
        for g in range(SSD_GROUPS):
            sl = slice(512 * g, 512 * (g + 1))
            Sg = S[:, sl]
            y_ref[:, sl] += t["ecs_e"][:, sl] * _dot(Cb[:, 128 * g:128 * (g + 1)], Sg.astype(BF16))
            S[:, sl] = eL_e[:, sl] * Sg + _dot(Bb[:, 128 * g:128 * (g + 1)], Xd[:, sl], TN)
        y = y_ref[...] + de_ref[...] * sx
        y_ref[...] = y
        z = z_ref[...]
        q = y * (z * _sigmoid(z))
        yn_ref[...] = (q * _rms(q) * g_ref[...]).astype(BF16)

    c0 = lambda i: (0, 0)
    return pl.pallas_call(
        body, name="ssd_fwd", grid=(nc,),
        in_specs=[pl.BlockSpec((CHUNK, SSD_INNER), lambda i: (i, 2)),
                  pl.BlockSpec((CHUNK, SSD_CONV_CH), lambda i: (i, 2)),
                  pl.BlockSpec((CHUNK, LANE), lambda i: (i, 0)),
                  pl.BlockSpec((4, SSD_CONV_CH), c0), pl.BlockSpec((1, SSD_CONV_CH), c0),
                  pl.BlockSpec((1, LANE), c0), pl.BlockSpec((1, LANE), c0),
                  pl.BlockSpec((1, SSD_INNER), c0), pl.BlockSpec((1, SSD_INNER), c0),
                  pl.BlockSpec((LANE, SSD_INNER), c0)],
        out_specs=[pl.BlockSpec((CHUNK, SSD_CONV_CH), lambda i: (i, 0)),
                   pl.BlockSpec((CHUNK, SSD_INNER), lambda i: (i, 0)),
                   pl.BlockSpec((CHUNK, SSD_INNER), lambda i: (i, 0)),
                   pl.BlockSpec((1, SSD_STATE, SSD_INNER), lambda i: (i, 0, 0))],
        out_shape=[jax.ShapeDtypeStruct((T, SSD_CONV_CH), F32), jax.ShapeDtypeStruct((T, SSD_INNER), F32),
                   jax.ShapeDtypeStruct((T, SSD_INNER), BF16),
                   jax.ShapeDtypeStruct((nc, SSD_STATE, SSD_INNER), F32)],
        scratch_shapes=[pltpu.VMEM((8, SSD_CONV_CH), F32), pltpu.VMEM((SSD_STATE, SSD_INNER), F32)],
        compiler_params=_params(1),
    )(proj, proj, dtp, conv_w, conv_b, dt_bias, a_log, d_e, g_ssd, expand)


def _ssd_bwd(dyn, proj, dtp, xc, y, sprev, conv_w, dt_bias, a_log, d_e, g_ssd, expand):
    T = proj.shape[0]
    nc = T // CHUNK

    def body(dyn_ref, z_ref, xp_ref, dtp_ref, xc_ref, y_ref, sprev_ref, cw_ref, bias_ref, alog_ref, de_ref,
             g_ref, ex_ref, dp_ref, ddtp_ref, dcw_ref, dcb_ref, dbias_ref, dA_ref, dD_ref, dg_ref,
             dS, head, dX_s, dxbc_s):
        @pl.when(pl.program_id(0) == 0)
        def _():
            dS[...] = jnp.zeros_like(dS)
            head[...] = jnp.zeros_like(head)
            for ref in (dcw_ref, dcb_ref, dbias_ref, dA_ref, dD_ref, dg_ref):
                ref[...] = jnp.zeros_like(ref)

        ex = ex_ref[...]
        xc = xc_ref[...]
        t = _ssd_chunk_terms(xc, dtp_ref[...], bias_ref[...], alog_ref[...], ex)
        ri, ci = t["ri"], t["ci"]
        xbc = t["xbc"]
        sx = xbc[:, 0:SSD_INNER]
        Bb = xbc[:, SSD_INNER:SSD_INNER + 256].astype(BF16)
        Cb = xbc[:, SSD_INNER + 256:SSD_CONV_CH].astype(BF16)
        X = t["dt_e"] * sx
        z = z_ref[...]
        sz = _sigmoid(z)
        siluz = z * sz
        yv = y_ref[...]
        q = yv * siluz
        rn = _rms(q)
        dynv = dyn_ref[...]
        dg_ref[...] += jnp.sum(dynv * q * rn, axis=0, keepdims=True)
        dq = _rms_bwd(dynv * g_ref[...], q, rn)
        dp_ref[:, 0:SSD_INNER] = (dq * yv * (sz * (1.0 + z * (1.0 - sz)))).astype(BF16)
        dY = dq * siluz
        dD_ref[...] += jnp.sum(dY * sx, axis=0, keepdims=True)
        dYb = dY.astype(BF16)
        lane = lax.broadcasted_iota(jnp.int32, (CHUNK, LANE), 1)
        dcs = jnp.zeros((CHUNK, CHUNK), F32)
        dcsT = jnp.zeros((CHUNK, CHUNK), F32)
        Xb = X.astype(BF16)
        for g in range(SSD_GROUPS):
            Bg = Bb[:, 128 * g:128 * (g + 1)]
            Cg = Cb[:, 128 * g:128 * (g + 1)]
            G = _dot(Cg, Bg, NT)
            dGsum = jnp.zeros((CHUNK, CHUNK), F32)
            for k in range(4 * g, 4 * g + 4):
                Xp = Xb[:, 128 * k:128 * (k + 1)]
                dYp = dY[:, 128 * k:128 * (k + 1)]
                dXp = jnp.zeros((CHUNK, LANE), F32)
                for half in range(2):
                    h = 2 * k + half
                    lam = _head_lambda(t, h)
                    M = G * lam
                    dYh = jnp.where((lane >= 64) if half else (lane < 64), dYp, 0.0).astype(BF16)
                    dM = _dot(dYh, Xp, NT)
                    W = dM * M
                    dcs += jnp.where(ci == h, jnp.sum(W, axis=1, keepdims=True), 0.0)
                    dcsT += jnp.where(ri == h, jnp.sum(W, axis=0, keepdims=True), 0.0)
                    dGsum += dM * lam
                    dXp += _dot(M.astype(BF16), dYh, TN)
                dX_s[:, 128 * k:128 * (k + 1)] = dXp
            dGb = dGsum.astype(BF16)
            dxbc_s[:, SSD_INNER + 256 + 128 * g:SSD_INNER + 256 + 128 * (g + 1)] = _dot(dGb, Bg)
            dxbc_s[:, SSD_INNER + 128 * g:SSD_INNER + 128 * (g + 1)] = _dot(dGb, Cg, TN)
        dcs = dcs - dcsT.T
        Sp = sprev_ref[0]
        dSv = dS[...]
        ecs_e, dec_e = t["ecs_e"], t["dec_e"]
        eL_e = _row(ecs_e, CHUNK - 1)
        dYe = dY * ecs_e
        dYeb = dYe.astype(BF16)
        Xd = X * dec_e
        Xdb = Xd.astype(BF16)
        for g in range(SSD_GROUPS):
            sl = slice(512 * g, 512 * (g + 1))
            Bg = Bb[:, 128 * g:128 * (g + 1)]
            Cg = Cb[:, 128 * g:128 * (g + 1)]
            Spb = Sp[:, sl].astype(BF16)
            dSb = dSv[:, sl].astype(BF16)
            CS = _dot(Cg, Spb)
            BS = _dot(Bg, dSb)
            dxbc_s[:, SSD_INNER + 256 + 128 * g:SSD_INNER + 256 + 128 * (g + 1)] += _dot(dYeb[:, sl], Spb, NT)
            dxbc_s[:, SSD_INNER + 128 * g:SSD_INNER + 128 * (g + 1)] += _dot(Xdb[:, sl], dSb, NT)
            dS[:, sl] = eL_e[:, sl] * dSv[:, sl] + _dot(Cg, dYeb[:, sl], TN)
            dX_s[:, sl] += dec_e[:, sl] * BS
            dcs += _dot(dYe[:, sl] * CS, ex[:, sl], NT, precision=HI)
            tdec = _dot(X[:, sl] * BS, ex[:, sl], NT, precision=HI) * t["dec"]
            dcs -= tdec
            last = jnp.sum(tdec, axis=0, keepdims=True)
            last += jnp.sum(_dot(Sp[:, sl] * dSv[:, sl], ex[:, sl], NT, precision=HI), axis=0, keepdims=True) \
                * _row(t["ecs"], CHUNK - 1)
            dcs += jnp.where(ri == CHUNK - 1, last, 0.0)
        triu = (ci >= ri).astype(F32)
        da = _dot(triu, dcs, precision=HI)
        dX = dX_s[...]
        ddt = da * t["A"] + _dot(dX * sx, ex, NT, precision=HI)
        dA_ref[...] += jnp.sum(da * t["dt"], axis=0, keepdims=True)
        ddtp = ddt * _sigmoid(t["pre"])
        dbias_ref[...] += jnp.sum(ddtp, axis=0, keepdims=True)
        ddtp_ref[...] = ddtp.astype(BF16)
        dxbc_s[:, 0:SSD_INNER] = dX * t["dt_e"] + de_ref[...] * dY
        sg = t["sg"]
        dxc = dxbc_s[...] * (sg * (1.0 + xc * (1.0 - sg)))
        xp = xp_ref[...]
        nxt = head[...]
        dcb_ref[...] += jnp.sum(dxc, axis=0, keepdims=True)
        dpre = cw_ref[3:4, :] * dxc
        dcw_ref[3:4, :] += jnp.sum(xp * dxc, axis=0, keepdims=True)
        for j in range(1, 4):
            sh = _shift_up(dxc, nxt, j)
            dpre += cw_ref[3 - j:4 - j, :] * sh
            dcw_ref[3 - j:4 - j, :] += jnp.sum(xp * sh, axis=0, keepdims=True)
        head[...] = dxc[0:8]
        dp_ref[:, SSD_INNER:SSD_INNER + SSD_CONV_CH] = dpre.astype(BF16)

    c0 = lambda i: (0, 0)
    rev = lambda i: (nc - 1 - i, 0)
    return pl.pallas_call(
        body, name="ssd_bwd", grid=(nc,),
        in_specs=[pl.BlockSpec((CHUNK, SSD_INNER), rev),
                  pl.BlockSpec((CHUNK, SSD_INNER), lambda i: (nc - 1 - i, 2)),
                  pl.BlockSpec((CHUNK, SSD_CONV_CH), lambda i: (nc - 1 - i, 2)),
                  pl.BlockSpec((CHUNK, LANE), rev),
                  pl.BlockSpec((CHUNK, SSD_CONV_CH), rev),
                  pl.BlockSpec((CHUNK, SSD_INNER), rev),
                  pl.BlockSpec((1, SSD_STATE, SSD_INNER), lambda i: (nc - 1 - i, 0, 0)),
                  pl.BlockSpec((4, SSD_CONV_CH), c0), pl.BlockSpec((1, LANE), c0), pl.BlockSpec((1, LANE), c0),
                  pl.BlockSpec((1, SSD_INNER), c0), pl.BlockSpec((1, SSD_INNER), c0),
                  pl.BlockSpec((LANE, SSD_INNER), c0)],
        out_specs=[pl.BlockSpec((CHUNK, 2560), rev), pl.BlockSpec((CHUNK, LANE), rev),
                   pl.BlockSpec((4, SSD_CONV_CH), c0), pl.BlockSpec((1, SSD_CONV_CH), c0),
                   pl.BlockSpec((1, LANE), c0), pl.BlockSpec((1, LANE), c0),
                   pl.BlockSpec((1, SSD_INNER), c0), pl.BlockSpec((1, SSD_INNER), c0)],
        out_shape=[jax.ShapeDtypeStruct((T, 2560), BF16), jax.ShapeDtypeStruct((T, LANE), BF16),
                   jax.ShapeDtypeStruct((4, SSD_CONV_CH), F32), jax.ShapeDtypeStruct((1, SSD_CONV_CH), F32),
                   jax.ShapeDtypeStruct((1, LANE), F32), jax.ShapeDtypeStruct((1, LANE), F32),
                   jax.ShapeDtypeStruct((1, SSD_INNER), F32), jax.ShapeDtypeStruct((1, SSD_INNER), F32)],
        scratch_shapes=[pltpu.VMEM((SSD_STATE, SSD_INNER), F32), pltpu.VMEM((8, SSD_CONV_CH), F32),
                        pltpu.VMEM((CHUNK, SSD_INNER), F32), pltpu.VMEM((CHUNK, SSD_CONV_CH), F32)],
        compiler_params=_params(1),
    )(dyn, proj, proj, dtp, xc, y, sprev, conv_w, dt_bias, a_log, d_e, g_ssd, expand)


def _outproj_fwd(x, y_lru, y_ssd, w_out, g_pm, g_pf):
    T = x.shape[0]

    def body(x_ref, yl_ref, ys_ref, wo_ref, gpm_ref, gpf_ref, mix_ref, x1_ref, h2_ref):
        mix = _dot(yl_ref[...], wo_ref[0:LRU_W, :]) + _dot(ys_ref[...], wo_ref[LRU_W:2 * LRU_W, :])
        mix_ref[...] = mix
        x1 = x_ref[...] + mix * _rms(mix) * gpm_ref[...]
        x1_ref[...] = x1
        h2_ref[...] = (x1 * _rms(x1) * gpf_ref[...]).astype(BF16)

    tile = pl.BlockSpec((TT, D_MODEL), lambda i: (i, 0))
    vec = pl.BlockSpec((1, D_MODEL), lambda i: (0, 0))
    return pl.pallas_call(
        body, name="outproj_fwd", grid=(T // TT,),
        in_specs=[tile, tile, tile, pl.BlockSpec((2 * LRU_W, D_MODEL), lambda i: (0, 0)), vec, vec],
        out_specs=[tile, tile, tile],
        out_shape=[jax.ShapeDtypeStruct((T, D_MODEL), F32), jax.ShapeDtypeStruct((T, D_MODEL), F32),
                   jax.ShapeDtypeStruct((T, D_MODEL), BF16)],
        compiler_params=_params(1),
    )(x, y_lru, y_ssd, w_out, g_pm, g_pf)


def _ffn_fwd_bwd(x1, h2, target, w_gate, w_up, w_down, g_pf, g_ff):
    T = x1.shape[0]

    def body(x1_ref, h2_ref, tg_ref, wg_hbm, wu_hbm, wd_hbm, gpf_ref, gff_ref,
             dx1_ref, act_ref, df_ref, dgt_ref, dup_ref, dgpf_ref, dgff_ref, loss_ref,
             wg, wu, wd, gt_s, up_s, sem):
        @pl.when(pl.program_id(0) == 0)
        def _():
            cps = [pltpu.make_async_copy(s, d, sem.at[n]) for n, (s, d) in
                   enumerate(((wg_hbm, wg), (wu_hbm, wu), (wd_hbm, wd)))]
            for c in cps:
                c.start()
            for c in cps:
                c.wait()
            for ref in (dgpf_ref, dgff_ref, loss_ref):
                ref[...] = jnp.zeros_like(ref)

        h2 = h2_ref[...]
        f = jnp.zeros((TT, D_MODEL), F32)
        for c in range(2):
            sl = slice(FF_HALF * c, FF_HALF * (c + 1))
            gt = _dot(h2, wg[:, sl])
            up = _dot(h2, wu[:, sl])
            gt_s[:, sl] = gt
            up_s[:, sl] = up
            act = (gt * _sigmoid(gt) * up).astype(BF16)
            act_ref[:, sl] = act
            f += _dot(act, wd[sl, :])
        x1 = x1_ref[...]
        rnf = _rms(f)
        e = x1 + f * rnf * gff_ref[...] - tg_ref[...]
        part = 0.5 * jnp.sum(jnp.sum(e * e, axis=1, keepdims=True), axis=0, keepdims=True) * (1.0 / D_MODEL)
        lane = lax.broadcasted_iota(jnp.int32, (1, LANE), 1)
        loss_ref[...] += jnp.where(lane == 0, part, 0.0)
        dx2 = e * (1.0 / D_MODEL)
        dgff_ref[...] += jnp.sum(dx2 * f * rnf, axis=0, keepdims=True)
        df = _rms_bwd(dx2 * gff_ref[...], f, rnf).astype(BF16)
        df_ref[...] = df
        dh2 = jnp.zeros((TT, D_MODEL), F32)
        for c in range(2):
            sl = slice(FF_HALF * c, FF_HALF * (c + 1))
            dact = _dot(df, wd[sl, :], NT)
            gt = gt_s[:, sl]
            up = up_s[:, sl]
            sg = _sigmoid(gt)
            dgt = (dact * up * (sg * (1.0 + gt * (1.0 - sg)))).astype(BF16)
            dup = (dact * gt * sg).astype(BF16)
            dgt_ref[:, sl] = dgt
            dup_ref[:, sl] = dup
            dh2 += _dot(dgt, wg[:, sl], NT) + _dot(dup, wu[:, sl], NT)
        rn2 = _rms(x1)
        dgpf_ref[...] += jnp.sum(dh2 * x1 * rn2, axis=0, keepdims=True)
        dx1_ref[...] = dx2 + _rms_bwd(dh2 * gpf_ref[...], x1, rn2)

    tile = pl.BlockSpec((TT, D_MODEL), lambda i: (i, 0))
    wide = pl.BlockSpec((TT, D_FF), lambda i: (i, 0))
    vec = pl.BlockSpec((1, D_MODEL), lambda i: (0, 0))
    hbm = pl.BlockSpec(memory_space=pl.ANY)
    return pl.pallas_call(
        body, name="ffn_fwd_bwd", grid=(T // TT,),
        in_specs=[tile, tile, tile, hbm, hbm, hbm, vec, vec],
        out_specs=[tile, wide, tile, wide, wide, vec, vec, pl.BlockSpec((1, LANE), lambda i: (0, 0))],
        out_shape=[jax.ShapeDtypeStruct((T, D_MODEL), F32), jax.ShapeDtypeStruct((T, D_FF), BF16),
                   jax.ShapeDtypeStruct((T, D_MODEL), BF16), jax.ShapeDtypeStruct((T, D_FF), BF16),
                   jax.ShapeDtypeStruct((T, D_FF), BF16), jax.ShapeDtypeStruct((1, D_MODEL), F32),
                   jax.ShapeDtypeStruct((1, D_MODEL), F32), jax.ShapeDtypeStruct((1, LANE), F32)],
        scratch_shapes=[pltpu.VMEM((D_MODEL, D_FF), BF16), pltpu.VMEM((D_MODEL, D_FF), BF16),
                        pltpu.VMEM((D_FF, D_MODEL), BF16), pltpu.VMEM((TT, D_FF), F32),
                        pltpu.VMEM((TT, D_FF), F32), pltpu.SemaphoreType.DMA((3,))],
        compiler_params=_params(1),
    )(x1, h2, target, w_gate, w_up, w_down, g_pf, g_ff)


def _outproj_bwd(dx1, mix, w_out, g_pm):
    T = dx1.shape[0]

    def body(dx1_ref, mix_ref, wo_ref, gpm_ref, dyl_ref, dys_ref, dmix_ref, dg_ref):
        @pl.when(pl.program_id(0) == 0)
        def _():
            dg_ref[...] = jnp.zeros_like(dg_ref)

        mix = mix_ref[...]
        rn = _rms(mix)
        dx1v = dx1_ref[...]
        dg_ref[...] += jnp.sum(dx1v * mix * rn, axis=0, keepdims=True)
        dmix = _rms_bwd(dx1v * gpm_ref[...], mix, rn).astype(BF16)
        dmix_ref[...] = dmix
        dyl_ref[...] = _dot(dmix, wo_ref[0:LRU_W, :], NT)
        dys_ref[...] = _dot(dmix, wo_ref[LRU_W:2 * LRU_W, :], NT)

    tile = pl.BlockSpec((TT, D_MODEL), lambda i: (i, 0))
    vec = pl.BlockSpec((1, D_MODEL), lambda i: (0, 0))
    return pl.pallas_call(
        body, name="outproj_bwd", grid=(T // TT,),
        in_specs=[tile, tile, pl.BlockSpec((2 * LRU_W, D_MODEL), lambda i: (0, 0)), vec],
        out_specs=[tile, tile, tile, vec],
        out_shape=[jax.ShapeDtypeStruct((T, D_MODEL), F32), jax.ShapeDtypeStruct((T, D_MODEL), F32),
                   jax.ShapeDtypeStruct((T, D_MODEL), BF16), jax.ShapeDtypeStruct((1, D_MODEL), F32)],
        compiler_params=_params(1),
    )(dx1, mix, w_out, g_pm)


def _tn_matmul(a, bs, name, tk=512):
    T, M = a.shape
    nk = T // tk
    nb = len(bs)

    def body(*refs):
        a_ref, b_refs, o_refs, accs = refs[0], refs[1:1 + nb], refs[1 + nb:1 + 2 * nb], refs[1 + 2 * nb:]
        k = pl.program_id(0)

        @pl.when(k == 0)
        def _():
            for acc in accs:
                acc[...] = jnp.zeros_like(acc)

        av = a_ref[...].astype(BF16)
        for b_ref, acc in zip(b_refs, accs):
            acc[...] += _dot(av, b_ref[...], TN)

        @pl.when(k == nk - 1)
        def _():
            for o_ref, acc in zip(o_refs, accs):
                o_ref[...] = acc[...].astype(BF16)

    return pl.pallas_call(
        body, name=name, grid=(nk,),
        in_specs=[pl.BlockSpec((tk, M), lambda k: (k, 0))]
                 + [pl.BlockSpec((tk, b.shape[1]), lambda k: (k, 0)) for b in bs],
        out_specs=[pl.BlockSpec((M, b.shape[1]), lambda k: (0, 0)) for b in bs],
        out_shape=[jax.ShapeDtypeStruct((M, b.shape[1]), BF16) for b in bs],
        scratch_shapes=[pltpu.VMEM((M, b.shape[1]), F32) for b in bs],
        compiler_params=_params(1),
    )(a, *bs)


def _tn_blockdiag(a, b1, b2, name, tk=1024):
    T = a.shape[0]
    tk = min(tk, T)

    def body(a_ref, b1_ref, b2_ref, o1_ref, o2_ref):
        @pl.when(pl.program_id(0) == 0)
        def _():
            o1_ref[...] = jnp.zeros_like(o1_ref)
            o2_ref[...] = jnp.zeros_like(o2_ref)

        for j in range(4):
            sl = slice(256 * j, 256 * (j + 1))
            av = a_ref[:, sl].astype(BF16)
            o1_ref[j] += _dot(av, b1_ref[:, sl], TN)
            o2_ref[j] += _dot(av, b2_ref[:, sl], TN)

    blk = pl.BlockSpec((tk, LRU_W), lambda k: (k, 0))
    out = pl.BlockSpec((4, 256, 256), lambda k: (0, 0, 0))
    return pl.pallas_call(
        body, name=name, grid=(T // tk,),
        in_specs=[blk, blk, blk], out_specs=[out, out],
        out_shape=[jax.ShapeDtypeStruct((4, 256, 256), F32)] * 2,
        compiler_params=_params(1),
    )(a, b1, b2)


def _adamw(parts, w, m, v, name, tr):
    P, R, C = parts.shape

    def body(p_ref, w_ref, m_ref, v_ref, g_ref, d_ref, nm_ref, nv_ref):
        g = p_ref[0].astype(F32)
        for k in range(1, P):
            g = g + p_ref[k].astype(F32)
        g_ref[...] = g
        nm = ADAM_B1 * m_ref[...] + (1.0 - ADAM_B1) * g
        nv = ADAM_B2 * v_ref[...] + (1.0 - ADAM_B2) * (g * g)
        nm_ref[...] = nm
        nv_ref[...] = nv
        m_hat = nm / (1.0 - ADAM_B1 ** ADAM_STEP)
        v_hat = nv / (1.0 - ADAM_B2 ** ADAM_STEP)
        d_ref[...] = -ADAM_LR * (m_hat / (jnp.sqrt(v_hat) + ADAM_EPS) + ADAM_WD * w_ref[...])

    blk = pl.BlockSpec((tr, C), lambda i: (i, 0))
    return pl.pallas_call(
        body, name=name, grid=(R // tr,),
        in_specs=[pl.BlockSpec((P, tr, C), lambda i: (0, i, 0)), blk, blk, blk],
        out_specs=[blk, blk, blk, blk],
        out_shape=[jax.ShapeDtypeStruct((R, C), F32)] * 4,
        compiler_params=_params(1),
    )(parts, w, m, v)


def _peer(k):
    x, y, c = lax.axis_index("x"), lax.axis_index("y"), lax.axis_index("c")
    px = x ^ ((k >> 2) & 1)
    py = y ^ ((k >> 1) & 1)
    pc = c ^ (k & 1)
    return (px, py, pc), 4 * px + 2 * py + pc


def _my_block():
    return 4 * lax.axis_index("x") + 2 * lax.axis_index("y") + lax.axis_index("c")


def _all_gather(shards, name):
    n = len(shards)

    def body(*refs):
        ins, outs = refs[:n], refs[n:2 * n]
        send, recv, loc = refs[2 * n:]
        x, y, c = lax.axis_index("x"), lax.axis_index("y"), lax.axis_index("c")
        sibling = (x, y, 1 - c)
        chips = [(1 - x, y), (x, 1 - y), (1 - x, 1 - y)]
        slot = lambda px, py, pc: 4 * px + 2 * py + pc

        def copy(a, k, block, to, src=None):
            dst = outs[a].at[slot(*block)]
            return pltpu.make_async_remote_copy(
                src_ref=dst if src is None else src, dst_ref=dst, send_sem=send.at[a, k], recv_sem=recv.at[a, k],
                device_id=to, device_id_type=pl.DeviceIdType.MESH)

        mine = [pltpu.make_async_copy(ins[a], outs[a].at[slot(x, y, c)], loc.at[a]) for a in range(n)]
        for cp in mine:
            cp.start()
        first = []
        for a in range(n):
            first.append(copy(a, 0, (x, y, c), sibling, src=ins[a]))
            first += [copy(a, 1 + j, (x, y, c), (*chip, c), src=ins[a]) for j, chip in enumerate(chips)]
        for cp in first:
            cp.start()
        passed = []
        for j, chip in enumerate(chips):
            for a in range(n):
                copy(a, 1 + j, (*chip, c), (x, y, c)).wait_recv()
                fwd = copy(a, 4 + j, (*chip, c), sibling)
                fwd.start()
                passed.append(fwd)
        for a in range(n):
            copy(a, 0, sibling, (x, y, c)).wait_recv()
            for j, chip in enumerate(chips):
                copy(a, 4 + j, (*chip, 1 - c), (x, y, c)).wait_recv()
        for cp in first + passed:
            cp.wait_send()
        for cp in mine:
            cp.wait()

    hbm = pl.BlockSpec(memory_space=pl.ANY)
    return pl.pallas_call(
        body, name=name,
        in_specs=[hbm] * n, out_specs=[hbm] * n,
        out_shape=[jax.ShapeDtypeStruct((N_DEV,) + s.shape, s.dtype) for s in shards],
        scratch_shapes=[pltpu.SemaphoreType.DMA((n, N_DEV - 1)), pltpu.SemaphoreType.DMA((n, N_DEV - 1)),
                        pltpu.SemaphoreType.DMA((n,))],
    )(*shards)


def _exchange_grads(slabs, small, name):
    n = len(slabs)

    def body(*refs):
        ins, m_ref = refs[:n], refs[n]
        outs, mo_ref = refs[n + 1:2 * n + 1], refs[2 * n + 1]
        send, recv, loc = refs[2 * n + 2:]
        me = _my_block()
        local = [pltpu.make_async_copy(ins[a].at[me], outs[a].at[me], loc.at[a]) for a in range(n)]
        local.append(pltpu.make_async_copy(m_ref, mo_ref.at[me], loc.at[n]))
        for cp in local:
            cp.start()
        cps = []
        for k in range(1, N_DEV):
            to, blk = _peer(k)
            for a in range(n + 1):
                src, dst = (ins[a].at[blk], outs[a].at[me]) if a < n else (m_ref, mo_ref.at[me])
                cp = pltpu.make_async_remote_copy(src_ref=src, dst_ref=dst, send_sem=send.at[a, k - 1],
                                                  recv_sem=recv.at[a, k - 1], device_id=to,
                                                  device_id_type=pl.DeviceIdType.MESH)
                cp.start()
                cps.append(cp)
        for cp in cps:
            cp.wait()
        for cp in local:
            cp.wait()

    hbm = pl.BlockSpec(memory_space=pl.ANY)
    return pl.pallas_call(
        body, name=name,
        in_specs=[hbm] * (n + 1), out_specs=[hbm] * (n + 1),
        out_shape=[jax.ShapeDtypeStruct(s.shape, s.dtype) for s in slabs]
                  + [jax.ShapeDtypeStruct((N_DEV,) + small.shape, small.dtype)],
        scratch_shapes=[pltpu.SemaphoreType.DMA((n + 1, N_DEV - 1)), pltpu.SemaphoreType.DMA((n + 1, N_DEV - 1)),
                        pltpu.SemaphoreType.DMA((n + 1,))],
    )(*slabs, small)


BIG = ("w_in", "w_out", "w_gate", "w_up", "w_down")
BIG_SHARD = {"w_in": (1024, 578), "w_out": (256, 1024), "w_gate": (1024, 352), "w_up": (1024, 352),
             "w_down": (352, 1024)}
BIG_SHARD_AXIS = {"w_in": 1, "w_out": 0, "w_gate": 1, "w_up": 1, "w_down": 0}
BIG_ADAM_ROWS = {"w_in": 256, "w_out": 128, "w_gate": 256, "w_up": 256, "w_down": 176}


def _join(parts, axis):
    if axis == 0:
        return parts.reshape((-1,) + parts.shape[2:])
    return jnp.concatenate([parts[j] for j in range(N_DEV)], axis=1)


def _split(full, axis):
    if axis == 0:
        return full.reshape((N_DEV, full.shape[0] // N_DEV) + full.shape[1:])
    c = full.shape[1] // N_DEV
    return jnp.stack([full[:, c * j:c * (j + 1)] for j in range(N_DEV)])


SMALL = (("pre_mix_norm", 1024), ("lru_conv_w", 4096), ("lru_conv_b", 1024), ("lru_wa", 65536), ("lru_ba", 1024),
         ("lru_wx", 65536), ("lru_bx", 1024), ("lru_lambda", 1024), ("lru_out_norm", 1024), ("ssd_conv_w", 6144),
         ("ssd_conv_b", 1536), ("ssd_dt_bias", 16), ("ssd_a_log", 16), ("ssd_d", 16), ("ssd_out_norm", 1024),
         ("post_mix_norm", 1024), ("pre_ffn_norm", 1024), ("post_ffn_norm", 1024), ("loss", 1))
REPLICATED = tuple(n for n, _ in SMALL if n not in ("lru_conv_w", "ssd_conv_w", "loss"))


def _pack_small(d):
    flat = jnp.concatenate([d[n].astype(F32).reshape(-1) for n, _ in SMALL])
    return jnp.pad(flat, (0, SMALL_ROWS * 1024 - flat.shape[0])).reshape(SMALL_ROWS, 1024)


def _unpack_small(p):
    flat = p.reshape(-1)
    out, off = {}, 0
    for n, size in SMALL:
        out[n] = flat[off:off + size]
        off += size
    return out


def _blockdiag4(w):
    z = jnp.zeros((4, 4, 64, 4, 64), w.dtype)
    w4 = w.reshape(4, 4, 64, 64)
    for b in range(4):
        z = z.at[:, b, :, b, :].set(w4[:, b])
    return z.reshape(4, 256, 256)


def _diag_blocks(g):
    g5 = g.reshape(4, 4, 64, 4, 64)
    return jnp.stack([g5[:, b, :, b, :] for b in range(4)], axis=1).reshape(16, 64, 64)


def _local_step(x, target, W, P):
    w_in = W["w_in"]
    w_main = w_in[:, :IN_MAIN]
    w_dt = jnp.pad(w_in[:, IN_MAIN:], ((0, 0), (0, LANE - SSD_HEADS)))
    pad16 = lambda v: jnp.pad(v.reshape(1, SSD_HEADS), ((0, 0), (0, LANE - SSD_HEADS)))
    dt_bias, a_log = pad16(P["ssd_dt_bias"]), pad16(P["ssd_a_log"])
    d_e = jnp.repeat(P["ssd_d"].reshape(SSD_HEADS), SSD_HEAD_DIM).reshape(1, SSD_INNER)
    expand = (jnp.arange(LANE)[:, None] == (jnp.arange(SSD_INNER)[None, :] // SSD_HEAD_DIM)).astype(F32)
    wa_bd = _blockdiag4(P["lru_wa"].astype(BF16))
    wx_bd = _blockdiag4(P["lru_wx"].astype(BF16))
    vec = lambda n: P[n].reshape(1, -1)

    h, proj, dtp = _inproj_fwd(x, vec("pre_mix_norm"), w_main, w_dt)
    lx, hl, y_lru = _lru_fwd(proj, P["lru_conv_w"], vec("lru_conv_b"), wa_bd, wx_bd, vec("lru_ba"), vec("lru_bx"),
                             vec("lru_lambda"), vec("lru_out_norm"))
    xc, y, y_ssd, sprev = _ssd_fwd(proj, dtp, P["ssd_conv_w"], vec("ssd_conv_b"), dt_bias, a_log, d_e,
                                   vec("ssd_out_norm"), expand)
    mix, x1, h2 = _outproj_fwd(x, y_lru, y_ssd, W["w_out"], vec("post_mix_norm"), vec("pre_ffn_norm"))
    dx1, act, df, dgt, dup, dg_pf, dg_ff, loss = _ffn_fwd_bwd(
        x1, h2, target, W["w_gate"], W["w_up"], W["w_down"], vec("pre_ffn_norm"), vec("post_ffn_norm"))
    dy_lru, dy_ssd, dmix, dg_pm = _outproj_bwd(dx1, mix, W["w_out"], vec("post_mix_norm"))
    dp_ssd, ddtp, dcw_s, dcb_s, dbias, dA, dD_e, dg_ssd = _ssd_bwd(
        dy_ssd, proj, dtp, xc, y, sprev, P["ssd_conv_w"], dt_bias, a_log, d_e, vec("ssd_out_norm"), expand)
    dp_lru, dpa, dpx, dcw_l, dcb_l, dba, dbx, dlam, dg_lru = _lru_bwd(
        dy_lru, proj, lx, hl, P["lru_conv_w"], wa_bd, wx_bd, vec("lru_ba"), vec("lru_bx"), vec("lru_lambda"),
        vec("lru_out_norm"))
    grad_x, dg_pre = _inproj_bwd(dp_lru, dp_ssd, ddtp, dx1, x, vec("pre_mix_norm"), w_main, w_dt)

    dw_in = _tn_matmul(h, [dp_lru, dp_ssd, ddtp], "dw_in")
    dw_gate, dw_up = _tn_matmul(h2, [dgt], "dw_gate")[0], _tn_matmul(h2, [dup], "dw_up")[0]
    big = {
        "w_in": jnp.concatenate([dw_in[0], dw_in[1], dw_in[2][:, :SSD_HEADS]], axis=1),
        "w_out": jnp.concatenate([_tn_matmul(y_lru, [dmix], "dw_out_lru")[0],
                                  _tn_matmul(y_ssd, [dmix], "dw_out_ssd")[0]], axis=0),
        "w_gate": dw_gate, "w_up": dw_up,
        "w_down": _tn_matmul(act, [df], "dw_down")[0],
    }
    a_neg = -jnp.exp(P["ssd_a_log"].reshape(SSD_HEADS))
    dwa, dwx = _tn_blockdiag(lx, dpa, dpx, "dw_lru_gates")
    small = {
        "pre_mix_norm": dg_pre, "lru_conv_w": dcw_l, "lru_conv_b": dcb_l,
        "lru_wa": _diag_blocks(dwa), "lru_ba": dba,
        "lru_wx": _diag_blocks(dwx), "lru_bx": dbx,
        "lru_lambda": dlam, "lru_out_norm": dg_lru, "ssd_conv_w": dcw_s, "ssd_conv_b": dcb_s,
        "ssd_dt_bias": dbias[0, :SSD_HEADS], "ssd_a_log": dA[0, :SSD_HEADS] * a_neg,
        "ssd_d": jnp.sum(dD_e.reshape(SSD_HEADS, SSD_HEAD_DIM), axis=1), "ssd_out_norm": dg_ssd,
        "post_mix_norm": dg_pm, "pre_ffn_norm": dg_pf, "post_ffn_norm": dg_ff, "loss": loss[0, 0:1],
    }
    return grad_x, big, small


def kernel(x, pre_mix_norm, w_in, lru_conv_w, lru_conv_b, lru_wa, lru_ba, lru_wx, lru_bx, lru_lambda, lru_out_norm, ssd_conv_w, ssd_conv_b, ssd_dt_bias, ssd_a_log, ssd_d, ssd_out_norm, w_out, post_mix_norm, pre_ffn_norm, w_gate, w_up, w_down, post_ffn_norm, loss_target, m_pre_mix_norm, m_w_in, m_lru_conv_w, m_lru_conv_b, m_lru_wa, m_lru_ba, m_lru_wx, m_lru_bx, m_lru_lambda, m_lru_out_norm, m_ssd_conv_w, m_ssd_conv_b, m_ssd_dt_bias, m_ssd_a_log, m_ssd_d, m_ssd_out_norm, m_w_out, m_post_mix_norm, m_pre_ffn_norm, m_w_gate, m_w_up, m_w_down, m_post_ffn_norm, v_pre_mix_norm, v_w_in, v_lru_conv_w, v_lru_conv_b, v_lru_wa, v_lru_ba, v_lru_wx, v_lru_bx, v_lru_lambda, v_lru_out_norm, v_ssd_conv_w, v_ssd_conv_b, v_ssd_dt_bias, v_ssd_a_log, v_ssd_d, v_ssd_out_norm, v_w_out, v_post_mix_norm, v_pre_ffn_norm, v_w_gate, v_w_up, v_w_down, v_post_ffn_norm):
    a = dict(locals())
    names = [n for n, _ in SMALL if n != "loss"] + list(BIG)
    w = {n: a[n][0] for n in names}
    m = {n: a["m_" + n][0] for n in names}
    v = {n: a["v_" + n][0] for n in names}

    cpack = jnp.concatenate([w["lru_conv_w"], w["ssd_conv_w"], jnp.zeros((4, 64), F32)], axis=1)
    cpack = jnp.pad(cpack, ((0, 4), (0, 0)))
    gathered = _all_gather([w[n].astype(BF16) for n in BIG] + [cpack], "all_gather_weights")
    Wfull = {n: _join(p, BIG_SHARD_AXIS[n]) for n, p in zip(BIG, gathered)}
    cg = gathered[-1]
    P = {n: w[n] for n in REPLICATED}
    P["lru_conv_w"] = _join(cg[:, 0:4, 0:128], 1)
    P["ssd_conv_w"] = _join(cg[:, 0:4, 128:320], 1)

    grad_x, big, small = _local_step(x[0], loss_target[0], Wfull, P)

    got = _exchange_grads([_split(big[n], BIG_SHARD_AXIS[n]) for n in BIG], _pack_small(small), "exchange_grads")
    got_small = got[-1]

    outs = {}
    for n, parts in zip(BIG, got):
        res = _adamw(parts, w[n], m[n], v[n], "adamw_" + n, BIG_ADAM_ROWS[n])
        for kind, r in zip(("grad", "delta", "new_m", "new_v"), res):
            outs[kind + "_" + n] = r

    zero_extra = {"loss": jnp.zeros((1,), F32)}
    full_conv = {"lru_conv_w": jnp.zeros((4, LRU_W), F32), "ssd_conv_w": jnp.zeros((4, SSD_CONV_CH), F32)}
    rep = lambda d: _pack_small({**{n: d[n] for n in REPLICATED}, **full_conv, **zero_extra})
    res = _adamw(got_small, rep(w), rep(m), rep(v), "adamw_small", SMALL_ROWS)
    g_small = _unpack_small(res[0])
    for kind, r in zip(("grad", "delta", "new_m", "new_v"), res):
        for n, val in _unpack_small(r).items():
            if n in REPLICATED:
                outs[kind + "_" + n] = val.reshape(w[n].shape)

    me = _my_block()
    gl = lax.dynamic_slice(g_small["lru_conv_w"].reshape(4, LRU_W), (0, me * 128), (4, 128))
    gs = lax.dynamic_slice(g_small["ssd_conv_w"].reshape(4, SSD_CONV_CH), (0, me * 192), (4, 192))
    cat = lambda d: jnp.pad(jnp.concatenate([d["lru_conv_w"], d["ssd_conv_w"]], axis=1), ((0, 4), (0, 64)))
    res = _adamw(cat({"lru_conv_w": gl, "ssd_conv_w": gs})[None], cat(w), cat(m), cat(v), "adamw_conv", 8)
    for kind, r in zip(("grad", "delta", "new_m", "new_v"), res):
        outs[kind + "_lru_conv_w"] = r[0:4, 0:128]
        outs[kind + "_ssd_conv_w"] = r[0:4, 128:320]

    order = ["pre_mix_norm", "w_in", "lru_conv_w", "lru_conv_b", "lru_wa", "lru_ba", "lru_wx", "lru_bx", "lru_lambda",
             "lru_out_norm", "ssd_conv_w", "ssd_conv_b", "ssd_dt_bias", "ssd_a_log", "ssd_d", "ssd_out_norm", "w_out",
             "post_mix_norm", "pre_ffn_norm", "w_gate", "w_up", "w_down", "post_ffn_norm"]
    result = [g_small["loss"].reshape(()), grad_x[None]]
    for kind in ("grad", "delta", "new_m", "new_v"):
        result += [outs[kind + "_" + n][None] for n in order]
    return tuple(result)
```

```python
import functools

import jax
import jax.numpy as jnp
from jax import lax
from jax.experimental import pallas as pl
from jax.experimental.pallas import tpu as pltpu

F32 = jnp.float32
BF16 = jnp.bfloat16
HI = lax.Precision.HIGHEST
EPS = 1e-6
N_DEV = 8
D_MODEL = 1024
LRU_W = 1024
SSD_INNER = 1024
SSD_HEADS = 16
SSD_HEAD_DIM = 64
SSD_STATE = 128
SSD_GROUPS = 2
SSD_CONV_CH = 1536
CHUNK = 128
D_FF = 2816
FF_HALF = D_FF // 2
IN_MAIN = 4608
IN_COLS = 4624
LANE = 128
TT = 256
VMEM_LIMIT = 56 * 1024 * 1024
ADAM_LR, ADAM_B1, ADAM_B2, ADAM_EPS, ADAM_WD, ADAM_STEP = 0.001, 0.9, 0.999, 1e-08, 0.01, 10
PACK_ROWS = 1920
SMALL_ROWS = 160

NT = (((1,), (1,)), ((), ()))
TN = (((0,), (0,)), ((), ()))


def _params(n_grid):
    return pltpu.CompilerParams(dimension_semantics=("arbitrary",) * n_grid, vmem_limit_bytes=VMEM_LIMIT)


def _dot(a, b, dims=None, precision=None):
    if dims is None:
        return jnp.dot(a, b, preferred_element_type=F32, precision=precision)
    return lax.dot_general(a, b, dims, preferred_element_type=F32, precision=precision)


def _sigmoid(x):
    return 1.0 / (1.0 + jnp.exp(-x))


def _softplus(x):
    e = jnp.exp(-jnp.abs(x))
    l1p = jnp.where(e < 1e-3, e * (1.0 - e * (0.5 - e * (1.0 / 3.0))), jnp.log(1.0 + e))
    return jnp.maximum(x, 0.0) + l1p


def _neg_expm1(x):
    series = -x * (1.0 + x * (0.5 + x * (1.0 / 6.0 + x * (1.0 / 24.0))))
    return jnp.where(x > -0.01, series, 1.0 - jnp.exp(x))


_GELU_C = 0.7978845608028654


def _gelu(x):
    t = jnp.tanh(_GELU_C * (x + 0.044715 * x * x * x))
    return 0.5 * x * (1.0 + t), t


def _gelu_grad(x, t):
    return 0.5 * (1.0 + t) + 0.5 * x * (1.0 - t * t) * _GELU_C * (1.0 + 3.0 * 0.044715 * x * x)


def _rms(x):
    return lax.rsqrt(jnp.mean(x * x, axis=-1, keepdims=True) + EPS)


def _rms_bwd(dyn, x, rn):
    return rn * dyn - x * (rn * rn * rn) * jnp.mean(dyn * x, axis=-1, keepdims=True)


def _row(x, r):
    idx = lax.broadcasted_iota(jnp.int32, x.shape, 0)
    return jnp.sum(jnp.where(idx == r, x, 0.0), axis=0, keepdims=True)


def _shift_down(cur, prev8, j):
    s = pltpu.roll(cur, j, 0)
    p = pltpu.roll(prev8, j, 0)
    r8 = lax.broadcasted_iota(jnp.int32, prev8.shape, 0)
    top = jnp.where(r8 < j, p, s[0:8])
    return jnp.concatenate([top, s[8:]], axis=0)


def _shift_up(cur, next8, j):
    n = cur.shape[0]
    s = pltpu.roll(cur, n - j, 0)
    p = pltpu.roll(next8, 8 - j, 0)
    r8 = lax.broadcasted_iota(jnp.int32, next8.shape, 0)
    bot = jnp.where(r8 >= 8 - j, p, s[n - 8:n])
    return jnp.concatenate([s[:n - 8], bot], axis=0)


def _scan_fwd(a, u):
    n = a.shape[0]
    row = lax.broadcasted_iota(jnp.int32, a.shape, 0)
    k = 1
    while k < n:
        ok = row >= k
        a_s = jnp.where(ok, pltpu.roll(a, k, 0), 1.0)
        u_s = jnp.where(ok, pltpu.roll(u, k, 0), 0.0)
        u = a * u_s + u
        a = a * a_s
        k *= 2
    return a, u


def _scan_bwd(b, d):
    n = b.shape[0]
    row = lax.broadcasted_iota(jnp.int32, b.shape, 0)
    k = 1
    while k < n:
        ok = row < n - k
        b_s = jnp.where(ok, pltpu.roll(b, n - k, 0), 1.0)
        d_s = jnp.where(ok, pltpu.roll(d, n - k, 0), 0.0)
        d = b * d_s + d
        b = b * b_s
        k *= 2
    return b, d


def _inproj_fwd(x, g_pre, w_main, w_dt):
    T = x.shape[0]

    def body(x_ref, g_ref, wm_hbm, wd_hbm, h_ref, proj_ref, dtp_ref, wm, wd, sem):
        @pl.when(pl.program_id(0) == 0)
        def _():
            c1 = pltpu.make_async_copy(wm_hbm, wm, sem.at[0])
            c2 = pltpu.make_async_copy(wd_hbm, wd, sem.at[1])
            c1.start()
            c2.start()
            c1.wait()
            c2.wait()

        xv = x_ref[...]
        h = (xv * _rms(xv) * g_ref[...]).astype(BF16)
        h_ref[...] = h
        proj_ref[...] = _dot(h, wm[...])
        dtp_ref[...] = _dot(h, wd[...])

    return pl.pallas_call(
        body, name="inproj_fwd", grid=(T // TT,),
        in_specs=[pl.BlockSpec((TT, D_MODEL), lambda i: (i, 0)),
                  pl.BlockSpec((1, D_MODEL), lambda i: (0, 0)),
                  pl.BlockSpec(memory_space=pl.ANY), pl.BlockSpec(memory_space=pl.ANY)],
        out_specs=[pl.BlockSpec((TT, D_MODEL), lambda i: (i, 0)),
                   pl.BlockSpec((TT, IN_MAIN), lambda i: (i, 0)),
                   pl.BlockSpec((TT, LANE), lambda i: (i, 0))],
        out_shape=[jax.ShapeDtypeStruct((T, D_MODEL), BF16), jax.ShapeDtypeStruct((T, IN_MAIN), F32),
                   jax.ShapeDtypeStruct((T, LANE), F32)],
        scratch_shapes=[pltpu.VMEM((D_MODEL, IN_MAIN), BF16), pltpu.VMEM((D_MODEL, LANE), BF16),
                        pltpu.SemaphoreType.DMA((2,))],
        compiler_params=_params(1),
    )(x, g_pre, w_main, w_dt)


def _inproj_bwd(dp_lru, dp_ssd, ddtp, dx1, x, g_pre, w_main, w_dt):
    T = x.shape[0]

    def body(dl_ref, ds_ref, dd_ref, dx1_ref, x_ref, g_ref, wm_hbm, wd_hbm, gx_ref, dg_ref, wm, wd, sem):
        @pl.when(pl.program_id(0) == 0)
        def _():
            c1 = pltpu.make_async_copy(wm_hbm, wm, sem.at[0])
            c2 = pltpu.make_async_copy(wd_hbm, wd, sem.at[1])
            c1.start()
            c2.start()
            c1.wait()
            c2.wait()
            dg_ref[...] = jnp.zeros_like(dg_ref)

        dh = _dot(dl_ref[...], wm[:, 0:2048], NT)
        dh += _dot(ds_ref[...], wm[:, 2048:IN_MAIN], NT)
        dh += _dot(dd_ref[...], wd[...], NT)
        xv = x_ref[...]
        rn = _rms(xv)
        dg_ref[...] += jnp.sum(dh * xv * rn, axis=0, keepdims=True)
        gx_ref[...] = dx1_ref[...] + _rms_bwd(dh * g_ref[...], xv, rn)

    return pl.pallas_call(
        body, name="inproj_bwd", grid=(T // TT,),
        in_specs=[pl.BlockSpec((TT, 2048), lambda i: (i, 0)),
                  pl.BlockSpec((TT, 2560), lambda i: (i, 0)),
                  pl.BlockSpec((TT, LANE), lambda i: (i, 0)),
                  pl.BlockSpec((TT, D_MODEL), lambda i: (i, 0)),
                  pl.BlockSpec((TT, D_MODEL), lambda i: (i, 0)),
                  pl.BlockSpec((1, D_MODEL), lambda i: (0, 0)),
                  pl.BlockSpec(memory_space=pl.ANY), pl.BlockSpec(memory_space=pl.ANY)],
        out_specs=[pl.BlockSpec((TT, D_MODEL), lambda i: (i, 0)),
                   pl.BlockSpec((1, D_MODEL), lambda i: (0, 0))],
        out_shape=[jax.ShapeDtypeStruct((T, D_MODEL), F32), jax.ShapeDtypeStruct((1, D_MODEL), F32)],
        scratch_shapes=[pltpu.VMEM((D_MODEL, IN_MAIN), BF16), pltpu.VMEM((D_MODEL, LANE), BF16),
                        pltpu.SemaphoreType.DMA((2,))],
        compiler_params=_params(1),
    )(dp_lru, dp_ssd, ddtp, dx1, x, g_pre, w_main, w_dt)


def _lru_gates(lx, wa_ref, wx_ref, ba, bx, lam):
    lxb = lx.astype(BF16)
    pa = jnp.concatenate([_dot(lxb[:, 256 * k:256 * (k + 1)], wa_ref[k]) for k in range(4)], axis=1) + ba
    px = jnp.concatenate([_dot(lxb[:, 256 * k:256 * (k + 1)], wx_ref[k]) for k in range(4)], axis=1) + bx
    r = _sigmoid(pa)
    ig = _sigmoid(px)
    sp = _softplus(-lam)
    log_a = -8.0 * r * sp
    a = jnp.exp(log_a)
    mult = jnp.sqrt(_neg_expm1(2.0 * log_a))
    return r, ig, sp, a, mult


def _lru_fwd(proj, conv_w, conv_b, wa_bd, wx_bd, ba, bx, lam, g_lru):
    T = proj.shape[0]

    def body(cx_ref, gate_ref, cw_ref, cb_ref, wa_ref, wx_ref, ba_ref, bx_ref, lam_ref, g_ref,
             lx_ref, hl_ref, y_ref, tail, hcar):
        @pl.when(pl.program_id(0) == 0)
        def _():
            tail[...] = jnp.zeros_like(tail)
            hcar[...] = jnp.zeros_like(hcar)

        cx = cx_ref[...]
        prev8 = tail[...]
        lx = cb_ref[...] + cw_ref[3:4, :] * cx
        for j in range(1, 4):
            lx += cw_ref[3 - j:4 - j, :] * _shift_down(cx, prev8, j)
        tail[...] = cx[TT - 8:TT]
        lx_ref[...] = lx
        r, ig, sp, a, mult = _lru_gates(lx, wa_ref, wx_ref, ba_ref[...], bx_ref[...], lam_ref[...])
        acum, h0 = _scan_fwd(a, mult * (ig * lx))
        h = h0 + acum * hcar[...]
        hl_ref[...] = h
        hcar[...] = hl_ref[TT - 1:TT, :]
        ge, _ = _gelu(gate_ref[...])
        p = h * ge
        y_ref[...] = (p * _rms(p) * g_ref[...]).astype(BF16)

    vec = pl.BlockSpec((1, LRU_W), lambda i: (0, 0))
    bd = pl.BlockSpec((4, 256, 256), lambda i: (0, 0, 0))
    tile = pl.BlockSpec((TT, LRU_W), lambda i: (i, 0))
    return pl.pallas_call(
        body, name="lru_fwd", grid=(T // TT,),
        in_specs=[tile, pl.BlockSpec((TT, LRU_W), lambda i: (i, 1)),
                  pl.BlockSpec((4, LRU_W), lambda i: (0, 0)), vec, bd, bd, vec, vec, vec, vec],
        out_specs=[tile, tile, tile],
        out_shape=[jax.ShapeDtypeStruct((T, LRU_W), F32), jax.ShapeDtypeStruct((T, LRU_W), F32),
                   jax.ShapeDtypeStruct((T, LRU_W), BF16)],
        scratch_shapes=[pltpu.VMEM((8, LRU_W), F32), pltpu.VMEM((1, LRU_W), F32)],
        compiler_params=_params(1),
    )(proj, proj, conv_w, conv_b, wa_bd, wx_bd, ba, bx, lam, g_lru)


def _lru_bwd(dy, proj, lx, hl, conv_w, wa_bd, wx_bd, ba, bx, lam, g_lru):
    T = proj.shape[0]
    nt = T // TT

    def body(dy_ref, cx_ref, gate_ref, lx_ref, hl_ref, halo_ref, cw_ref, wa_ref, wx_ref, ba_ref, bx_ref,
             lam_ref, g_ref, dp_ref, dpa_ref, dpx_ref, dcw_ref, dcb_ref, dba_ref, dbx_ref, dlam_ref, dg_ref,
             gcar, acar, head):
        i = pl.program_id(0)

        @pl.when(i == 0)
        def _():
            gcar[...] = jnp.zeros_like(gcar)
            acar[...] = jnp.zeros_like(acar)
            head[...] = jnp.zeros_like(head)
            for ref in (dcw_ref, dcb_ref, dba_ref, dbx_ref, dlam_ref, dg_ref):
                ref[...] = jnp.zeros_like(ref)

        lx = lx_ref[...]
        h = hl_ref[...]
        gate = gate_ref[...]
        cx = cx_ref[...]
        lam = lam_ref[...]
        r, ig, sp, a, mult = _lru_gates(lx, wa_ref, wx_ref, ba_ref[...], bx_ref[...], lam)
        ge, th = _gelu(gate)
        p = h * ge
        rn = _rms(p)
        dyv = dy_ref[...]
        dg_ref[...] += jnp.sum(dyv * p * rn, axis=0, keepdims=True)
        dp = _rms_bwd(dyv * g_ref[...], p, rn)
        dp_ref[:, LRU_W:2 * LRU_W] = (dp * h * _gelu_grad(gate, th)).astype(BF16)
        dh = dp * ge
        row = lax.broadcasted_iota(jnp.int32, a.shape, 0)
        b = jnp.where(row == TT - 1, acar[...], pltpu.roll(a, TT - 1, 0))
        bcum, g0 = _scan_bwd(b, dh)
        g = g0 + bcum * gcar[...]
        gcar[...] = _row(g[0:8], 0)
        acar[...] = _row(a[0:8], 0)
        h_last_prev = halo_ref[7:8, :] * (i < nt - 1).astype(F32)
        hprev = jnp.where(row == 0, h_last_prev, pltpu.roll(h, 1, 0))
        da = g * hprev
        dm2 = (g * (ig * lx)) * 0.5 / mult
        dlog_a = da * a - 2.0 * a * a * dm2
        dlam_ref[...] += jnp.sum(-8.0 * r * dlog_a, axis=0, keepdims=True) * (-_sigmoid(-lam))
        dpa = (-8.0 * sp * dlog_a) * r * (1.0 - r)
        dpx = (g * mult * lx) * ig * (1.0 - ig)
        dba_ref[...] += jnp.sum(dpa, axis=0, keepdims=True)
        dbx_ref[...] += jnp.sum(dpx, axis=0, keepdims=True)
        dpab = dpa.astype(BF16)
        dpxb = dpx.astype(BF16)
        dpa_ref[...] = dpab
        dpx_ref[...] = dpxb
        dlx = g * mult * ig + jnp.concatenate(
            [_dot(dpab[:, 256 * k:256 * (k + 1)], wa_ref[k], NT) + _dot(dpxb[:, 256 * k:256 * (k + 1)], wx_ref[k], NT)
             for k in range(4)], axis=1)
        nxt = head[...]
        dcb_ref[...] += jnp.sum(dlx, axis=0, keepdims=True)
        dcx = cw_ref[3:4, :] * dlx
        dcw_ref[3:4, :] += jnp.sum(cx * dlx, axis=0, keepdims=True)
        for j in range(1, 4):
            sh = _shift_up(dlx, nxt, j)
            dcx += cw_ref[3 - j:4 - j, :] * sh
            dcw_ref[3 - j:4 - j, :] += jnp.sum(cx * sh, axis=0, keepdims=True)
        head[...] = dlx[0:8]
        dp_ref[:, 0:LRU_W] = dcx.astype(BF16)

    rev = lambda i: (nt - 1 - i, 0)
    vec = pl.BlockSpec((1, LRU_W), lambda i: (0, 0))
    bd = pl.BlockSpec((4, 256, 256), lambda i: (0, 0, 0))
    tile = pl.BlockSpec((TT, LRU_W), rev)
    halo = pl.BlockSpec((8, LRU_W), lambda i: (jnp.maximum((nt - 1 - i) * (TT // 8) - 1, 0), 0))
    cw = pl.BlockSpec((4, LRU_W), lambda i: (0, 0))
    return pl.pallas_call(
        body, name="lru_bwd", grid=(nt,),
        in_specs=[tile, tile, pl.BlockSpec((TT, LRU_W), lambda i: (nt - 1 - i, 1)), tile, tile, halo,
                  cw, bd, bd, vec, vec, vec, vec],
        out_specs=[pl.BlockSpec((TT, 2 * LRU_W), rev), tile, tile, cw, vec, vec, vec, vec, vec],
        out_shape=[jax.ShapeDtypeStruct((T, 2 * LRU_W), BF16), jax.ShapeDtypeStruct((T, LRU_W), BF16),
                   jax.ShapeDtypeStruct((T, LRU_W), BF16), jax.ShapeDtypeStruct((4, LRU_W), F32)]
                  + [jax.ShapeDtypeStruct((1, LRU_W), F32)] * 5,
        scratch_shapes=[pltpu.VMEM((1, LRU_W), F32), pltpu.VMEM((1, LRU_W), F32), pltpu.VMEM((8, LRU_W), F32)],
        compiler_params=_params(1),
    )(dy, proj, proj, lx, hl, hl, conv_w, wa_bd, wx_bd, ba, bx, lam, g_lru)


def _ssd_chunk_terms(xc, dtp, bias, alog, expand):
    sg = _sigmoid(xc)
    xbc = xc * sg
    pre = dtp + bias
    dt = _softplus(pre)
    A = -jnp.exp(alog)
    ri = lax.broadcasted_iota(jnp.int32, (CHUNK, CHUNK), 0)
    ci = lax.broadcasted_iota(jnp.int32, (CHUNK, CHUNK), 1)
    tril = (ri >= ci).astype(F32)
    cs = _dot(tril, dt * A, precision=HI)
    cs_last = _row(cs, CHUNK - 1)
    ecs = jnp.exp(cs)
    dec = jnp.exp(cs_last - cs)
    return dict(sg=sg, xbc=xbc, pre=pre, dt=dt, A=A, cs=cs, csT=cs.T, ecs=ecs, dec=dec, ri=ri, ci=ci,
                dt_e=_dot(dt, expand, precision=HI), ecs_e=_dot(ecs, expand, precision=HI),
                dec_e=_dot(dec, expand, precision=HI))


def _head_lambda(t, h):
    col = jnp.sum(jnp.where(t["ci"] == h, t["cs"], 0.0), axis=1, keepdims=True)
    rowv = jnp.sum(jnp.where(t["ri"] == h, t["csT"], 0.0), axis=0, keepdims=True)
    return jnp.exp(jnp.where(t["ri"] >= t["ci"], col - rowv, -1e30))


def _ssd_fwd(proj, dtp, conv_w, conv_b, dt_bias, a_log, d_e, g_ssd, expand):
    T = proj.shape[0]
    nc = T // CHUNK

    def body(z_ref, xp_ref, dtp_ref, cw_ref, cb_ref, bias_ref, alog_ref, de_ref, g_ref, ex_ref,
             xc_ref, y_ref, yn_ref, sprev_ref, tail, S):
        @pl.when(pl.program_id(0) == 0)
        def _():
            tail[...] = jnp.zeros_like(tail)
            S[...] = jnp.zeros_like(S)

        xp = xp_ref[...]
        prev8 = tail[...]
        xc = cb_ref[...] + cw_ref[3:4, :] * xp
        for j in range(1, 4):
            xc += cw_ref[3 - j:4 - j, :] * _shift_down(xp, prev8, j)
        tail[...] = xp[CHUNK - 8:CHUNK]
        xc_ref[...] = xc
        t = _ssd_chunk_terms(xc, dtp_ref[...], bias_ref[...], alog_ref[...], ex_ref[...])
        xbc = t["xbc"]
        sx = xbc[:, 0:SSD_INNER]
        Bb = xbc[:, SSD_INNER:SSD_INNER + 256].astype(BF16)
        Cb = xbc[:, SSD_INNER + 256:SSD_CONV_CH].astype(BF16)
        X = t["dt_e"] * sx
        lane = lax.broadcasted_iota(jnp.int32, (CHUNK, LANE), 1)
        G = [_dot(Cb[:, 128 * g:128 * (g + 1)], Bb[:, 128 * g:128 * (g + 1)], NT) for g in range(SSD_GROUPS)]
        for k in range(SSD_HEADS // 2):
            Xp = X[:, 128 * k:128 * (k + 1)]
            acc = jnp.zeros((CHUNK, LANE), F32)
            for half in range(2):
                M = (G[k // 4] * _head_lambda(t, 2 * k + half)).astype(BF16)
                Xh = jnp.where((lane >= 64) if half else (lane < 64), Xp, 0.0).astype(BF16)
                acc += _dot(M, Xh)
            y_ref[:, 128 * k:128 * (k + 1)] = acc
        sprev_ref[0] = S[...]
        eL_e = _row(t["ecs_e"], CHUNK - 1)
        Xd = (X * t["dec_e"]).astype(BF16)
        for g in range(SSD_GROUPS):
            sl = slice(512 * g, 512 * (g + 1))
            Sg = S[:, sl]
            y_ref[:, sl] += t["ecs_e"][:, sl] * _dot(Cb[:, 128 * g:128 * (g + 1)], Sg.astype(BF16))
            S[:, sl] = eL_e[:, sl] * Sg + _dot(Bb[:, 128 * g:128 * (g + 1)], Xd[:, sl], TN)
        y = y_ref[...] + de_ref[...] * sx
        y_ref[...] = y
        z = z_ref[...]
        q = y * (z * _sigmoid(z))
        yn_ref[...] = (q * _rms(q) * g_ref[...]).astype(BF16)

    c0 = lambda i: (0, 0)
    return pl.pallas_call(
        body, name="ssd_fwd", grid=(nc,),
        in_specs=[pl.BlockSpec((CHUNK, SSD_INNER), lambda i: (i, 2)),
                  pl.BlockSpec((CHUNK, SSD_CONV_CH), lambda i: (i, 2)),
                  pl.BlockSpec((CHUNK, LANE), lambda i: (i, 0)),
                  pl.BlockSpec((4, SSD_CONV_CH), c0), pl.BlockSpec((1, SSD_CONV_CH), c0),
                  pl.BlockSpec((1, LANE), c0), pl.BlockSpec((1, LANE), c0),
                  pl.BlockSpec((1, SSD_INNER), c0), pl.BlockSpec((1, SSD_INNER), c0),
                  pl.BlockSpec((LANE, SSD_INNER), c0)],
        out_specs=[pl.BlockSpec((CHUNK, SSD_CONV_CH), lambda i: (i, 0)),
                   pl.BlockSpec((CHUNK, SSD_INNER), lambda i: (i, 0)),
                   pl.BlockSpec((CHUNK, SSD_INNER), lambda i: (i, 0)),
                   pl.BlockSpec((1, SSD_STATE, SSD_INNER), lambda i: (i, 0, 0))],
        out_shape=[jax.ShapeDtypeStruct((T, SSD_CONV_CH), F32), jax.ShapeDtypeStruct((T, SSD_INNER), F32),
                   jax.ShapeDtypeStruct((T, SSD_INNER), BF16),
                   jax.ShapeDtypeStruct((nc, SSD_STATE, SSD_INNER), F32)],
        scratch_shapes=[pltpu.VMEM((8, SSD_CONV_CH), F32), pltpu.VMEM((SSD_STATE, SSD_INNER), F32)],
        compiler_params=_params(1),
    )(proj, proj, dtp, conv_w, conv_b, dt_bias, a_log, d_e, g_ssd, expand)


def _ssd_bwd(dyn, proj, dtp, xc, y, sprev, conv_w, dt_bias, a_log, d_e, g_ssd, expand):
    T = proj.shape[0]
    nc = T // CHUNK

    def body(dyn_ref, z_ref, xp_ref, dtp_ref, xc_ref, y_ref, sprev_ref, cw_ref, bias_ref, alog_ref, de_ref,
             g_ref, ex_ref, dp_ref, ddtp_ref, dcw_ref, dcb_ref, dbias_ref, dA_ref, dD_ref, dg_ref,
             dS, head, dX_s, dxbc_s):
        @pl.when(pl.program_id(0) == 0)
        def _():
            dS[...] = jnp.zeros_like(dS)
            head[...] = jnp.zeros_like(head)
            for ref in (dcw_ref, dcb_ref, dbias_ref, dA_ref, dD_ref, dg_ref):
                ref[...] = jnp.zeros_like(ref)

        ex = ex_ref[...]
        xc = xc_ref[...]
        t = _ssd_chunk_terms(xc, dtp_ref[...], bias_ref[...], alog_ref[...], ex)
        ri, ci = t["ri"], t["ci"]
        xbc = t["xbc"]
        sx = xbc[:, 0:SSD_INNER]
        Bb = xbc[:, SSD_INNER:SSD_INNER + 256].astype(BF16)
        Cb = xbc[:, SSD_INNER + 256:SSD_CONV_CH].astype(BF16)
        X = t["dt_e"] * sx
        z = z_ref[...]
        sz = _sigmoid(z)
        siluz = z * sz
        yv = y_ref[...]
        q = yv * siluz
        rn = _rms(q)
        dynv = dyn_ref[...]
        dg_ref[...] += jnp.sum(dynv * q * rn, axis=0, keepdims=True)
        dq = _rms_bwd(dynv * g_ref[...], q, rn)
        dp_ref[:, 0:SSD_INNER] = (dq * yv * (sz * (1.0 + z * (1.0 - sz)))).astype(BF16)
        dY = dq * siluz
        dD_ref[...] += jnp.sum(dY * sx, axis=0, keepdims=True)
        dYb = dY.astype(BF16)
        lane = lax.broadcasted_iota(jnp.int32, (CHUNK, LANE), 1)
        dcs = jnp.zeros((CHUNK, CHUNK), F32)
        dcsT = jnp.zeros((CHUNK, CHUNK), F32)
        Xb = X.astype(BF16)
        for g in range(SSD_GROUPS):
            Bg = Bb[:, 128 * g:128 * (g + 1)]
            Cg = Cb[:, 128 * g:128 * (g + 1)]
            G = _dot(Cg, Bg, NT)
            dGsum = jnp.zeros((CHUNK, CHUNK), F32)
            for k in range(4 * g, 4 * g + 4):
                Xp = Xb[:, 128 * k:128 * (k + 1)]
                dYp = dY[:, 128 * k:128 * (k + 1)]
                dXp = jnp.zeros((CHUNK, LANE), F32)
                for half in range(2):
                    h = 2 * k + half
                    lam = _head_lambda(t, h)
                    M = G * lam
                    dYh = jnp.where((lane >= 64) if half else (lane < 64), dYp, 0.0).astype(BF16)
                    dM = _dot(dYh, Xp, NT)
                    W = dM * M
                    dcs += jnp.where(ci == h, jnp.sum(W, axis=1, keepdims=True), 0.0)
                    dcsT += jnp.where(ri == h, jnp.sum(W, axis=0, keepdims=True), 0.0)
                    dGsum += dM * lam
                    dXp += _dot(M.astype(BF16), dYh, TN)
                dX_s[:, 128 * k:128 * (k + 1)] = dXp
            dGb = dGsum.astype(BF16)
            dxbc_s[:, SSD_INNER + 256 + 128 * g:SSD_INNER + 256 + 128 * (g + 1)] = _dot(dGb, Bg)
            dxbc_s[:, SSD_INNER + 128 * g:SSD_INNER + 128 * (g + 1)] = _dot(dGb, Cg, TN)
        dcs = dcs - dcsT.T
        Sp = sprev_ref[0]
        dSv = dS[...]
        ecs_e, dec_e = t["ecs_e"], t["dec_e"]
        eL_e = _row(ecs_e, CHUNK - 1)
        dYe = dY * ecs_e
        dYeb = dYe.astype(BF16)
        Xd = X * dec_e
        Xdb = Xd.astype(BF16)
        for g in range(SSD_GROUPS):
            sl = slice(512 * g, 512 * (g + 1))
            Bg = Bb[:, 128 * g:128 * (g + 1)]
            Cg = Cb[:, 128 * g:128 * (g + 1)]
            Spb = Sp[:, sl].astype(BF16)
            dSb = dSv[:, sl].astype(BF16)
            CS = _dot(Cg, Spb)
            BS = _dot(Bg, dSb)
            dxbc_s[:, SSD_INNER + 256 + 128 * g:SSD_INNER + 256 + 128 * (g + 1)] += _dot(dYeb[:, sl], Spb, NT)
            dxbc_s[:, SSD_INNER + 128 * g:SSD_INNER + 128 * (g + 1)] += _dot(Xdb[:, sl], dSb, NT)
            dS[:, sl] = eL_e[:, sl] * dSv[:, sl] + _dot(Cg, dYeb[:, sl], TN)
            dX_s[:, sl] += dec_e[:, sl] * BS
            dcs += _dot(dYe[:, sl] * CS, ex[:, sl], NT, precision=HI)
            tdec = _dot(X[:, sl] * BS, ex[:, sl], NT, precision=HI) * t["dec"]
            dcs -= tdec
            last = jnp.sum(tdec, axis=0, keepdims=True)
            last += jnp.sum(_dot(Sp[:, sl] * dSv[:, sl], ex[:, sl], NT, precision=HI), axis=0, keepdims=True) \
                * _row(t["ecs"], CHUNK - 1)
            dcs += jnp.where(ri == CHUNK - 1, last, 0.0)
        triu = (ci >= ri).astype(F32)
        da = _dot(triu, dcs, precision=HI)
        dX = dX_s[...]
        ddt = da * t["A"] + _dot(dX * sx, ex, NT, precision=HI)
        dA_ref[...] += jnp.sum(da * t["dt"], axis=0, keepdims=True)
        ddtp = ddt * _sigmoid(t["pre"])
        dbias_ref[...] += jnp.sum(ddtp, axis=0, keepdims=True)
        ddtp_ref[...] = ddtp.astype(BF16)
        dxbc_s[:, 0:SSD_INNER] = dX * t["dt_e"] + de_ref[...] * dY
        sg = t["sg"]
        dxc = dxbc_s[...] * (sg * (1.0 + xc * (1.0 - sg)))
        xp = xp_ref[...]
        nxt = head[...]
        dcb_ref[...] += jnp.sum(dxc, axis=0, keepdims=True)
        dpre = cw_ref[3:4, :] * dxc
        dcw_ref[3:4, :] += jnp.sum(xp * dxc, axis=0, keepdims=True)
        for j in range(1, 4):
            sh = _shift_up(dxc, nxt, j)
            dpre += cw_ref[3 - j:4 - j, :] * sh
            dcw_ref[3 - j:4 - j, :] += jnp.sum(xp * sh, axis=0, keepdims=True)
        head[...] = dxc[0:8]
        dp_ref[:, SSD_INNER:SSD_INNER + SSD_CONV_CH] = dpre.astype(BF16)

    c0 = lambda i: (0, 0)
    rev = lambda i: (nc - 1 - i, 0)
    return pl.pallas_call(
        body, name="ssd_bwd", grid=(nc,),
        in_specs=[pl.BlockSpec((CHUNK, SSD_INNER), rev),
                  pl.BlockSpec((CHUNK, SSD_INNER), lambda i: (nc - 1 - i, 2)),
                  pl.BlockSpec((CHUNK, SSD_CONV_CH), lambda i: (nc - 1 - i, 2)),
                  pl.BlockSpec((CHUNK, LANE), rev),
                  pl.BlockSpec((CHUNK, SSD_CONV_CH), rev),
                  pl.BlockSpec((CHUNK, SSD_INNER), rev),
                  pl.BlockSpec((1, SSD_STATE, SSD_INNER), lambda i: (nc - 1 - i, 0, 0)),
                  pl.BlockSpec((4, SSD_CONV_CH), c0), pl.BlockSpec((1, LANE), c0), pl.BlockSpec((1, LANE), c0),
                  pl.BlockSpec((1, SSD_INNER), c0), pl.BlockSpec((1, SSD_INNER), c0),
                  pl.BlockSpec((LANE, SSD_INNER), c0)],
        out_specs=[pl.BlockSpec((CHUNK, 2560), rev), pl.BlockSpec((CHUNK, LANE), rev),
                   pl.BlockSpec((4, SSD_CONV_CH), c0), pl.BlockSpec((1, SSD_CONV_CH), c0),
                   pl.BlockSpec((1, LANE), c0), pl.BlockSpec((1, LANE), c0),
                   pl.BlockSpec((1, SSD_INNER), c0), pl.BlockSpec((1, SSD_INNER), c0)],
        out_shape=[jax.ShapeDtypeStruct((T, 2560), BF16), jax.ShapeDtypeStruct((T, LANE), BF16),
                   jax.ShapeDtypeStruct((4, SSD_CONV_CH), F32), jax.ShapeDtypeStruct((1, SSD_CONV_CH), F32),
                   jax.ShapeDtypeStruct((1, LANE), F32), jax.ShapeDtypeStruct((1, LANE), F32),
                   jax.ShapeDtypeStruct((1, SSD_INNER), F32), jax.ShapeDtypeStruct((1, SSD_INNER), F32)],
        scratch_shapes=[pltpu.VMEM((SSD_STATE, SSD_INNER), F32), pltpu.VMEM((8, SSD_CONV_CH), F32),
                        pltpu.VMEM((CHUNK, SSD_INNER), F32), pltpu.VMEM((CHUNK, SSD_CONV_CH), F32)],
        compiler_params=_params(1),
    )(dyn, proj, proj, dtp, xc, y, sprev, conv_w, dt_bias, a_log, d_e, g_ssd, expand)


def _outproj_fwd(x, y_lru, y_ssd, w_out, g_pm, g_pf):
    T = x.shape[0]

    def body(x_ref, yl_ref, ys_ref, wo_ref, gpm_ref, gpf_ref, mix_ref, x1_ref, h2_ref):
        mix = _dot(yl_ref[...], wo_ref[0:LRU_W, :]) + _dot(ys_ref[...], wo_ref[LRU_W:2 * LRU_W, :])
        mix_ref[...] = mix
        x1 = x_ref[...] + mix * _rms(mix) * gpm_ref[...]
        x1_ref[...] = x1
        h2_ref[...] = (x1 * _rms(x1) * gpf_ref[...]).astype(BF16)

    tile = pl.BlockSpec((TT, D_MODEL), lambda i: (i, 0))
    vec = pl.BlockSpec((1, D_MODEL), lambda i: (0, 0))
    return pl.pallas_call(
        body, name="outproj_fwd", grid=(T // TT,),
        in_specs=[tile, tile, tile, pl.BlockSpec((2 * LRU_W, D_MODEL), lambda i: (0, 0)), vec, vec],
        out_specs=[tile, tile, tile],
        out_shape=[jax.ShapeDtypeStruct((T, D_MODEL), F32), jax.ShapeDtypeStruct((T, D_MODEL), F32),
                   jax.ShapeDtypeStruct((T, D_MODEL), BF16)],
        compiler_params=_params(1),
    )(x, y_lru, y_ssd, w_out, g_pm, g_pf)


def _ffn_fwd_bwd(x1, h2, target, w_gate, w_up, w_down, g_pf, g_ff):
    T = x1.shape[0]

    def body(x1_ref, h2_ref, tg_ref, wg_hbm, wu_hbm, wd_hbm, gpf_ref, gff_ref,
             dx1_ref, act_ref, df_ref, dgt_ref, dup_ref, dgpf_ref, dgff_ref, loss_ref,
             wg, wu, wd, gt_s, up_s, sem):
        @pl.when(pl.program_id(0) == 0)
        def _():
            cps = [pltpu.make_async_copy(s, d, sem.at[n]) for n, (s, d) in
                   enumerate(((wg_hbm, wg), (wu_hbm, wu), (wd_hbm, wd)))]
            for c in cps:
                c.start()
            for c in cps:
                c.wait()
            for ref in (dgpf_ref, dgff_ref, loss_ref):
                ref[...] = jnp.zeros_like(ref)

        h2 = h2_ref[...]
        f = jnp.zeros((TT, D_MODEL), F32)
        for c in range(2):
            sl = slice(FF_HALF * c, FF_HALF * (c + 1))
            gt = _dot(h2, wg[:, sl])
            up = _dot(h2, wu[:, sl])
            gt_s[:, sl] = gt
            up_s[:, sl] = up
            act = (gt * _sigmoid(gt) * up).astype(BF16)
            act_ref[:, sl] = act
            f += _dot(act, wd[sl, :])
        x1 = x1_ref[...]
        rnf = _rms(f)
        e = x1 + f * rnf * gff_ref[...] - tg_ref[...]
        part = 0.5 * jnp.sum(jnp.sum(e * e, axis=1, keepdims=True), axis=0, keepdims=True) * (1.0 / D_MODEL)
        lane = lax.broadcasted_iota(jnp.int32, (1, LANE), 1)
        loss_ref[...] += jnp.where(lane == 0, part, 0.0)
        dx2 = e * (1.0 / D_MODEL)
        dgff_ref[...] += jnp.sum(dx2 * f * rnf, axis=0, keepdims=True)
        df = _rms_bwd(dx2 * gff_ref[...], f, rnf).astype(BF16)
        df_ref[...] = df
        dh2 = jnp.zeros((TT, D_MODEL), F32)
        for c in range(2):
            sl = slice(FF_HALF * c, FF_HALF * (c + 1))
            dact = _dot(df, wd[sl, :], NT)
            gt = gt_s[:, sl]
            up = up_s[:, sl]
            sg = _sigmoid(gt)
            dgt = (dact * up * (sg * (1.0 + gt * (1.0 - sg)))).astype(BF16)
            dup = (dact * gt * sg).astype(BF16)
            dgt_ref[:, sl] = dgt
            dup_ref[:, sl] = dup
            dh2 += _dot(dgt, wg[:, sl], NT) + _dot(dup, wu[:, sl], NT)
        rn2 = _rms(x1)
        dgpf_ref[...] += jnp.sum(dh2 * x1 * rn2, axis=0, keepdims=True)
        dx1_ref[...] = dx2 + _rms_bwd(dh2 * gpf_ref[...], x1, rn2)

    tile = pl.BlockSpec((TT, D_MODEL), lambda i: (i, 0))
    wide = pl.BlockSpec((TT, D_FF), lambda i: (i, 0))
    vec = pl.BlockSpec((1, D_MODEL), lambda i: (0, 0))
    hbm = pl.BlockSpec(memory_space=pl.ANY)
    return pl.pallas_call(
        body, name="ffn_fwd_bwd", grid=(T // TT,),
        in_specs=[tile, tile, tile, hbm, hbm, hbm, vec, vec],
        out_specs=[tile, wide, tile, wide, wide, vec, vec, pl.BlockSpec((1, LANE), lambda i: (0, 0))],
        out_shape=[jax.ShapeDtypeStruct((T, D_MODEL), F32), jax.ShapeDtypeStruct((T, D_FF), BF16),
                   jax.ShapeDtypeStruct((T, D_MODEL), BF16), jax.ShapeDtypeStruct((T, D_FF), BF16),
                   jax.ShapeDtypeStruct((T, D_FF), BF16), jax.ShapeDtypeStruct((1, D_MODEL), F32),
                   jax.ShapeDtypeStruct((1, D_MODEL), F32), jax.ShapeDtypeStruct((1, LANE), F32)],
        scratch_shapes=[pltpu.VMEM((D_MODEL, D_FF), BF16), pltpu.VMEM((D_MODEL, D_FF), BF16),
                        pltpu.VMEM((D_FF, D_MODEL), BF16), pltpu.VMEM((TT, D_FF), F32),
                        pltpu.VMEM((TT, D_FF), F32), pltpu.SemaphoreType.DMA((3,))],
        compiler_params=_params(1),
    )(x1, h2, target, w_gate, w_up, w_down, g_pf, g_ff)


def _outproj_bwd(dx1, mix, w_out, g_pm):
    T = dx1.shape[0]

    def body(dx1_ref, mix_ref, wo_ref, gpm_ref, dyl_ref, dys_ref, dmix_ref, dg_ref):
        @pl.when(pl.program_id(0) == 0)
        def _():
            dg_ref[...] = jnp.zeros_like(dg_ref)

        mix = mix_ref[...]
        rn = _rms(mix)
        dx1v = dx1_ref[...]
        dg_ref[...] += jnp.sum(dx1v * mix * rn, axis=0, keepdims=True)
        dmix = _rms_bwd(dx1v * gpm_ref[...], mix, rn).astype(BF16)
        dmix_ref[...] = dmix
        dyl_ref[...] = _dot(dmix, wo_ref[0:LRU_W, :], NT)
        dys_ref[...] = _dot(dmix, wo_ref[LRU_W:2 * LRU_W, :], NT)

    tile = pl.BlockSpec((TT, D_MODEL), lambda i: (i, 0))
    vec = pl.BlockSpec((1, D_MODEL), lambda i: (0, 0))
    return pl.pallas_call(
        body, name="outproj_bwd", grid=(T // TT,),
        in_specs=[tile, tile, pl.BlockSpec((2 * LRU_W, D_MODEL), lambda i: (0, 0)), vec],
        out_specs=[tile, tile, tile, vec],
        out_shape=[jax.ShapeDtypeStruct((T, D_MODEL), F32), jax.ShapeDtypeStruct((T, D_MODEL), F32),
                   jax.ShapeDtypeStruct((T, D_MODEL), BF16), jax.ShapeDtypeStruct((1, D_MODEL), F32)],
        compiler_params=_params(1),
    )(dx1, mix, w_out, g_pm)


def _tn_matmul(a, bs, name, tk=512):
    T, M = a.shape
    nk = T // tk
    nb = len(bs)

    def body(*refs):
        a_ref, b_refs, o_refs, accs = refs[0], refs[1:1 + nb], refs[1 + nb:1 + 2 * nb], refs[1 + 2 * nb:]
        k = pl.program_id(0)

        @pl.when(k == 0)
        def _():
            for acc in accs:
                acc[...] = jnp.zeros_like(acc)

        av = a_ref[...].astype(BF16)
        for b_ref, acc in zip(b_refs, accs):
            acc[...] += _dot(av, b_ref[...], TN)

        @pl.when(k == nk - 1)
        def _():
            for o_ref, acc in zip(o_refs, accs):
                o_ref[...] = acc[...].astype(BF16)

    return pl.pallas_call(
        body, name=name, grid=(nk,),
        in_specs=[pl.BlockSpec((tk, M), lambda k: (k, 0))]
                 + [pl.BlockSpec((tk, b.shape[1]), lambda k: (k, 0)) for b in bs],
        out_specs=[pl.BlockSpec((M, b.shape[1]), lambda k: (0, 0)) for b in bs],
        out_shape=[jax.ShapeDtypeStruct((M, b.shape[1]), BF16) for b in bs],
        scratch_shapes=[pltpu.VMEM((M, b.shape[1]), F32) for b in bs],
        compiler_params=_params(1),
    )(a, *bs)


def _tn_blockdiag(a, b1, b2, name, tk=1024):
    T = a.shape[0]
    tk = min(tk, T)

    def body(a_ref, b1_ref, b2_ref, o1_ref, o2_ref):
        @pl.when(pl.program_id(0) == 0)
        def _():
            o1_ref[...] = jnp.zeros_like(o1_ref)
            o2_ref[...] = jnp.zeros_like(o2_ref)

        for j in range(4):
            sl = slice(256 * j, 256 * (j + 1))
            av = a_ref[:, sl].astype(BF16)
            o1_ref[j] += _dot(av, b1_ref[:, sl], TN)
            o2_ref[j] += _dot(av, b2_ref[:, sl], TN)

    blk = pl.BlockSpec((tk, LRU_W), lambda k: (k, 0))
    out = pl.BlockSpec((4, 256, 256), lambda k: (0, 0, 0))
    return pl.pallas_call(
        body, name=name, grid=(T // tk,),
        in_specs=[blk, blk, blk], out_specs=[out, out],
        out_shape=[jax.ShapeDtypeStruct((4, 256, 256), F32)] * 2,
        compiler_params=_params(1),
    )(a, b1, b2)


def _adamw(parts, w, m, v, name, tr):
    P, R, C = parts.shape

    def body(p_ref, w_ref, m_ref, v_ref, g_ref, d_ref, nm_ref, nv_ref):
        g = p_ref[0].astype(F32)
        for k in range(1, P):
            g = g + p_ref[k].astype(F32)
        g_ref[...] = g
        nm = ADAM_B1 * m_ref[...] + (1.0 - ADAM_B1) * g
        nv = ADAM_B2 * v_ref[...] + (1.0 - ADAM_B2) * (g * g)
        nm_ref[...] = nm
        nv_ref[...] = nv
        m_hat = nm / (1.0 - ADAM_B1 ** ADAM_STEP)
        v_hat = nv / (1.0 - ADAM_B2 ** ADAM_STEP)
        d_ref[...] = -ADAM_LR * (m_hat / (jnp.sqrt(v_hat) + ADAM_EPS) + ADAM_WD * w_ref[...])

    blk = pl.BlockSpec((tr, C), lambda i: (i, 0))
    return pl.pallas_call(
        body, name=name, grid=(R // tr,),
        in_specs=[pl.BlockSpec((P, tr, C), lambda i: (0, i, 0)), blk, blk, blk],
        out_specs=[blk, blk, blk, blk],
        out_shape=[jax.ShapeDtypeStruct((R, C), F32)] * 4,
        compiler_params=_params(1),
    )(parts, w, m, v)


def _peer(k):
    x, y, c = lax.axis_index("x"), lax.axis_index("y"), lax.axis_index("c")
    px = x ^ ((k >> 2) & 1)
    py = y ^ ((k >> 1) & 1)
    pc = c ^ (k & 1)
    return (px, py, pc), 4 * px + 2 * py + pc


def _my_block():
    return 4 * lax.axis_index("x") + 2 * lax.axis_index("y") + lax.axis_index("c")


def _all_gather(shards, name):
    n = len(shards)

    def body(*refs):
        ins, outs = refs[:n], refs[n:2 * n]
        send, recv, loc = refs[2 * n:]
        x, y, c = lax.axis_index("x"), lax.axis_index("y"), lax.axis_index("c")
        sibling = (x, y, 1 - c)
        chips = [(1 - x, y), (x, 1 - y), (1 - x, 1 - y)]
        slot = lambda px, py, pc: 4 * px + 2 * py + pc

        def copy(a, k, block, to, src=None):
            dst = outs[a].at[slot(*block)]
            return pltpu.make_async_remote_copy(
                src_ref=dst if src is None else src, dst_ref=dst, send_sem=send.at[a, k], recv_sem=recv.at[a, k],
                device_id=to, device_id_type=pl.DeviceIdType.MESH)

        mine = [pltpu.make_async_copy(ins[a], outs[a].at[slot(x, y, c)], loc.at[a]) for a in range(n)]
        for cp in mine:
            cp.start()
        first = []
        for a in range(n):
            first.append(copy(a, 0, (x, y, c), sibling, src=ins[a]))
            first += [copy(a, 1 + j, (x, y, c), (*chip, c), src=ins[a]) for j, chip in enumerate(chips)]
        for cp in first:
            cp.start()
        passed = []
        for j, chip in enumerate(chips):
            for a in range(n):
                copy(a, 1 + j, (*chip, c), (x, y, c)).wait_recv()
                fwd = copy(a, 4 + j, (*chip, c), sibling)
                fwd.start()
                passed.append(fwd)
        for a in range(n):
            copy(a, 0, sibling, (x, y, c)).wait_recv()
            for j, chip in enumerate(chips):
                copy(a, 4 + j, (*chip, 1 - c), (x, y, c)).wait_recv()
        for cp in first + passed:
            cp.wait_send()
        for cp in mine:
            cp.wait()

    hbm = pl.BlockSpec(memory_space=pl.ANY)
    return pl.pallas_call(
        body, name=name,
        in_specs=[hbm] * n, out_specs=[hbm] * n,
        out_shape=[jax.ShapeDtypeStruct((N_DEV,) + s.shape, s.dtype) for s in shards],
        scratch_shapes=[pltpu.SemaphoreType.DMA((n, N_DEV - 1)), pltpu.SemaphoreType.DMA((n, N_DEV - 1)),
                        pltpu.SemaphoreType.DMA((n,))],
    )(*shards)


_HBM = pl.BlockSpec(memory_space=pltpu.HBM)
_SEM = pl.BlockSpec(memory_space=pltpu.SEMAPHORE)
_EFFECT = pltpu.SideEffectType.DATAFLOW_SIDE_EFFECTING


def _direct_copies(srcs, lands, send, recv, slab_source):
    me = _my_block()
    cps = []
    for k in range(1, N_DEV):
        to, blk = _peer(k)
        for a, (src, land) in enumerate(zip(srcs, lands)):
            cps.append(pltpu.make_async_remote_copy(
                src_ref=src.at[blk] if slab_source else src, dst_ref=land.at[me],
                send_sem=send.at[a * (N_DEV - 1) + k - 1], recv_sem=recv.at[a * (N_DEV - 1) + k - 1],
                device_id=to, device_id_type=pl.DeviceIdType.MESH))
    return cps


def _exchange_start(srcs, name, slab_source):
    n = len(srcs)
    lands = [pltpu.with_memory_space_constraint(
        lax.empty(s.shape if slab_source else (N_DEV,) + s.shape, s.dtype), pltpu.HBM) for s in srcs]

    def body(*refs):
        ins, land_in = refs[:n], refs[n:2 * n]
        send, recv = refs[2 * n], refs[2 * n + 1]
        token, loc = refs[4 * n + 2], refs[4 * n + 3]
        me = _my_block()
        local = [pltpu.make_async_copy(ins[a].at[me] if slab_source else ins[a], land_in[a].at[me], loc.at[a])
                 for a in range(n)]
        for cp in local:
            cp.start()
        for cp in _direct_copies(ins, land_in, send, recv, slab_source):
            cp.start()
        for cp in local:
            cp.wait()
        token[...] = jnp.zeros_like(token)

    sems = pltpu.SemaphoreType.DMA((n * (N_DEV - 1),))
    res = pl.pallas_call(
        body, name=name,
        out_shape=(sems, sems, *[pltpu.HBM(s.shape, s.dtype) for s in srcs],
                   *[pltpu.HBM(l.shape, l.dtype) for l in lands], jax.ShapeDtypeStruct((8, LANE), F32)),
        in_specs=[_HBM] * (2 * n),
        out_specs=(_SEM, _SEM, *[_HBM] * (2 * n), pl.BlockSpec(memory_space=pltpu.VMEM)),
        input_output_aliases={i: 2 + i for i in range(2 * n)},
        scratch_shapes=[pltpu.SemaphoreType.DMA((n,))],
        compiler_params=pltpu.CompilerParams(has_side_effects=_EFFECT),
    )(*[pltpu.with_memory_space_constraint(s, pltpu.HBM) for s in srcs], *lands)
    return dict(send=res[0], recv=res[1], srcs=res[2:2 + n], lands=res[2 + n:2 + 2 * n], token=res[-1],
                slab_source=slab_source)


def _exchange_wait(ex, after, name):
    n = len(ex["srcs"])
    slab_source = ex["slab_source"]

    def body(*refs):
        ins, lands = refs[:n], refs[n:2 * n]
        send, recv = refs[2 * n], refs[2 * n + 1]
        for cp in _direct_copies(ins, lands, send, recv, slab_source):
            cp.wait_send()
            cp.wait_recv()

    res = pl.pallas_call(
        body, name=name,
        out_shape=tuple(pltpu.HBM(s.shape, s.dtype) for s in list(ex["srcs"]) + list(ex["lands"])),
        in_specs=[_HBM] * (2 * n) + [_SEM, _SEM, pl.BlockSpec(memory_space=pl.ANY)],
        out_specs=tuple([_HBM] * (2 * n)),
        input_output_aliases={i: i for i in range(2 * n)},
        compiler_params=pltpu.CompilerParams(has_side_effects=_EFFECT),
    )(*ex["srcs"], *ex["lands"], ex["send"], ex["recv"], after)
    return res[n:]


BIG = ("w_in", "w_out", "w_gate", "w_up", "w_down")
BIG_SHARD = {"w_in": (1024, 578), "w_out": (256, 1024), "w_gate": (1024, 352), "w_up": (1024, 352),
             "w_down": (352, 1024)}
BIG_SHARD_AXIS = {"w_in": 1, "w_out": 0, "w_gate": 1, "w_up": 1, "w_down": 0}
BIG_ADAM_ROWS = {"w_in": 256, "w_out": 128, "w_gate": 256, "w_up": 256, "w_down": 176}


def _join(parts, axis):
    if axis == 0:
        return parts.reshape((-1,) + parts.shape[2:])
    return jnp.concatenate([parts[j] for j in range(N_DEV)], axis=1)


def _split(full, axis):
    if axis == 0:
        return full.reshape((N_DEV, full.shape[0] // N_DEV) + full.shape[1:])
    c = full.shape[1] // N_DEV
    return jnp.stack([full[:, c * j:c * (j + 1)] for j in range(N_DEV)])


SMALL = (("pre_mix_norm", 1024), ("lru_conv_w", 4096), ("lru_conv_b", 1024), ("lru_wa", 65536), ("lru_ba", 1024),
         ("lru_wx", 65536), ("lru_bx", 1024), ("lru_lambda", 1024), ("lru_out_norm", 1024), ("ssd_conv_w", 6144),
         ("ssd_conv_b", 1536), ("ssd_dt_bias", 16), ("ssd_a_log", 16), ("ssd_d", 16), ("ssd_out_norm", 1024),
         ("post_mix_norm", 1024), ("pre_ffn_norm", 1024), ("post_ffn_norm", 1024), ("loss", 1))
REPLICATED = tuple(n for n, _ in SMALL if n not in ("lru_conv_w", "ssd_conv_w", "loss"))


def _pack_small(d):
    flat = jnp.concatenate([d[n].astype(F32).reshape(-1) for n, _ in SMALL])
    return jnp.pad(flat, (0, SMALL_ROWS * 1024 - flat.shape[0])).reshape(SMALL_ROWS, 1024)


def _unpack_small(p):
    flat = p.reshape(-1)
    out, off = {}, 0
    for n, size in SMALL:
        out[n] = flat[off:off + size]
        off += size
    return out


def _blockdiag4(w):
    z = jnp.zeros((4, 4, 64, 4, 64), w.dtype)
    w4 = w.reshape(4, 4, 64, 64)
    for b in range(4):
        z = z.at[:, b, :, b, :].set(w4[:, b])
    return z.reshape(4, 256, 256)


def _diag_blocks(g):
    g5 = g.reshape(4, 4, 64, 4, 64)
    return jnp.stack([g5[:, b, :, b, :] for b in range(4)], axis=1).reshape(16, 64, 64)


def _local_step(x, target, w_in, P, rest_weights, emit, start_token=None):
    w_main = w_in[:, :IN_MAIN]
    w_dt = jnp.pad(w_in[:, IN_MAIN:], ((0, 0), (0, LANE - SSD_HEADS)))
    pad16 = lambda v: jnp.pad(v.reshape(1, SSD_HEADS), ((0, 0), (0, LANE - SSD_HEADS)))
    dt_bias, a_log = pad16(P["ssd_dt_bias"]), pad16(P["ssd_a_log"])
    d_e = jnp.repeat(P["ssd_d"].reshape(SSD_HEADS), SSD_HEAD_DIM).reshape(1, SSD_INNER)
    expand = (jnp.arange(LANE)[:, None] == (jnp.arange(SSD_INNER)[None, :] // SSD_HEAD_DIM)).astype(F32)
    wa_bd = _blockdiag4(P["lru_wa"].astype(BF16))
    wx_bd = _blockdiag4(P["lru_wx"].astype(BF16))
    vec = lambda n: P[n].reshape(1, -1)

    after = lambda v, tok: v if tok is None else v + tok[0:1, 0:1]

    h, proj, dtp = _inproj_fwd(x, after(vec("pre_mix_norm"), start_token), w_main, w_dt)
    lx, hl, y_lru = _lru_fwd(proj, P["lru_conv_w"], vec("lru_conv_b"), wa_bd, wx_bd, vec("lru_ba"), vec("lru_bx"),
                             vec("lru_lambda"), vec("lru_out_norm"))
    xc, y, y_ssd, sprev = _ssd_fwd(proj, dtp, P["ssd_conv_w"], vec("ssd_conv_b"), dt_bias, a_log, d_e,
                                   vec("ssd_out_norm"), expand)
    W = rest_weights(y_ssd)
    mix, x1, h2 = _outproj_fwd(x, y_lru, y_ssd, W["w_out"], vec("post_mix_norm"), vec("pre_ffn_norm"))
    dx1, act, df, dgt, dup, dg_pf, dg_ff, loss = _ffn_fwd_bwd(
        x1, h2, target, W["w_gate"], W["w_up"], W["w_down"], vec("pre_ffn_norm"), vec("post_ffn_norm"))
    tok = emit("ffn", {"w_gate": _tn_matmul(h2, [dgt], "dw_gate")[0], "w_up": _tn_matmul(h2, [dup], "dw_up")[0],
                       "w_down": _tn_matmul(act, [df], "dw_down")[0]})
    dy_lru, dy_ssd, dmix, dg_pm = _outproj_bwd(dx1, mix, W["w_out"], after(vec("post_mix_norm"), tok))
    tok = emit("out", {"w_out": jnp.concatenate([_tn_matmul(y_lru, [dmix], "dw_out_lru")[0],
                                                 _tn_matmul(y_ssd, [dmix], "dw_out_ssd")[0]], axis=0)})
    dp_ssd, ddtp, dcw_s, dcb_s, dbias, dA, dD_e, dg_ssd = _ssd_bwd(
        dy_ssd, proj, dtp, xc, y, sprev, P["ssd_conv_w"], dt_bias, a_log, d_e,
        after(vec("ssd_out_norm"), tok), expand)
    dp_lru, dpa, dpx, dcw_l, dcb_l, dba, dbx, dlam, dg_lru = _lru_bwd(
        dy_lru, proj, lx, hl, P["lru_conv_w"], wa_bd, wx_bd, vec("lru_ba"), vec("lru_bx"), vec("lru_lambda"),
        vec("lru_out_norm"))
    dw_in = _tn_matmul(h, [dp_lru, dp_ssd, ddtp], "dw_in")
    tok = emit("in", {"w_in": jnp.concatenate([dw_in[0], dw_in[1], dw_in[2][:, :SSD_HEADS]], axis=1)})
    grad_x, dg_pre = _inproj_bwd(dp_lru, dp_ssd, ddtp, dx1, x, after(vec("pre_mix_norm"), tok), w_main, w_dt)

    a_neg = -jnp.exp(P["ssd_a_log"].reshape(SSD_HEADS))
    dwa, dwx = _tn_blockdiag(lx, dpa, dpx, "dw_lru_gates")
    small = {
        "pre_mix_norm": dg_pre, "lru_conv_w": dcw_l, "lru_conv_b": dcb_l,
        "lru_wa": _diag_blocks(dwa), "lru_ba": dba,
        "lru_wx": _diag_blocks(dwx), "lru_bx": dbx,
        "lru_lambda": dlam, "lru_out_norm": dg_lru, "ssd_conv_w": dcw_s, "ssd_conv_b": dcb_s,
        "ssd_dt_bias": dbias[0, :SSD_HEADS], "ssd_a_log": dA[0, :SSD_HEADS] * a_neg,
        "ssd_d": jnp.sum(dD_e.reshape(SSD_HEADS, SSD_HEAD_DIM), axis=1), "ssd_out_norm": dg_ssd,
        "post_mix_norm": dg_pm, "pre_ffn_norm": dg_pf, "post_ffn_norm": dg_ff, "loss": loss[0, 0:1],
    }
    return grad_x, small


def kernel(x, pre_mix_norm, w_in, lru_conv_w, lru_conv_b, lru_wa, lru_ba, lru_wx, lru_bx, lru_lambda, lru_out_norm, ssd_conv_w, ssd_conv_b, ssd_dt_bias, ssd_a_log, ssd_d, ssd_out_norm, w_out, post_mix_norm, pre_ffn_norm, w_gate, w_up, w_down, post_ffn_norm, loss_target, m_pre_mix_norm, m_w_in, m_lru_conv_w, m_lru_conv_b, m_lru_wa, m_lru_ba, m_lru_wx, m_lru_bx, m_lru_lambda, m_lru_out_norm, m_ssd_conv_w, m_ssd_conv_b, m_ssd_dt_bias, m_ssd_a_log, m_ssd_d, m_ssd_out_norm, m_w_out, m_post_mix_norm, m_pre_ffn_norm, m_w_gate, m_w_up, m_w_down, m_post_ffn_norm, v_pre_mix_norm, v_w_in, v_lru_conv_w, v_lru_conv_b, v_lru_wa, v_lru_ba, v_lru_wx, v_lru_bx, v_lru_lambda, v_lru_out_norm, v_ssd_conv_w, v_ssd_conv_b, v_ssd_dt_bias, v_ssd_a_log, v_ssd_d, v_ssd_out_norm, v_w_out, v_post_mix_norm, v_pre_ffn_norm, v_w_gate, v_w_up, v_w_down, v_post_ffn_norm):
    a = dict(locals())
    names = [n for n, _ in SMALL if n != "loss"] + list(BIG)
    w = {n: a[n][0] for n in names}
    m = {n: a["m_" + n][0] for n in names}
    v = {n: a["v_" + n][0] for n in names}

    cpack = jnp.concatenate([w["lru_conv_w"], w["ssd_conv_w"], jnp.zeros((4, 64), F32)], axis=1)
    cpack = jnp.pad(cpack, ((0, 4), (0, 0)))
    g_in, cg = _all_gather([w["w_in"].astype(BF16), cpack], "all_gather_w_in")
    P = {n: w[n] for n in REPLICATED}
    P["lru_conv_w"] = _join(cg[:, 0:4, 0:128], 1)
    P["ssd_conv_w"] = _join(cg[:, 0:4, 128:320], 1)

    rest = [n for n in BIG if n != "w_in"]
    zero = jnp.minimum(jnp.abs(cg[0, 0, 0]), 0.0)
    ex_w = _exchange_start([(w[n] + zero).astype(BF16) for n in rest], "weights_start", slab_source=False)

    def rest_weights(after):
        lands = _exchange_wait(ex_w, after, "weights_wait")
        return {n: _join(p, BIG_SHARD_AXIS[n]) for n, p in zip(rest, lands)}

    pending = []

    def emit(group, grads):
        ex = _exchange_start([_split(g, BIG_SHARD_AXIS[n]) for n, g in grads.items()], "grads_start_" + group,
                             slab_source=True)
        pending.append((group, list(grads), ex))
        return ex["token"]

    grad_x, small = _local_step(x[0], loss_target[0], _join(g_in, 1), P, rest_weights, emit, ex_w["token"])

    got_small = _all_gather([_pack_small(small)], "all_gather_small")[0]

    outs = {}
    done = got_small
    for group, group_names, ex in pending:
        for n, parts in zip(group_names, _exchange_wait(ex, done, "grads_wait_" + group)):
            res = _adamw(parts, w[n], m[n], v[n], "adamw_" + n, BIG_ADAM_ROWS[n])
            done = res[0]
            for kind, r in zip(("grad", "delta", "new_m", "new_v"), res):
                outs[kind + "_" + n] = r

    zero_extra = {"loss": jnp.zeros((1,), F32)}
    full_conv = {"lru_conv_w": jnp.zeros((4, LRU_W), F32), "ssd_conv_w": jnp.zeros((4, SSD_CONV_CH), F32)}
    rep = lambda d: _pack_small({**{n: d[n] for n in REPLICATED}, **full_conv, **zero_extra})
    res = _adamw(got_small, rep(w), rep(m), rep(v), "adamw_small", SMALL_ROWS)
    g_small = _unpack_small(res[0])
    for kind, r in zip(("grad", "delta", "new_m", "new_v"), res):
        for n, val in _unpack_small(r).items():
            if n in REPLICATED:
                outs[kind + "_" + n] = val.reshape(w[n].shape)

    me = _my_block()
    gl = lax.dynamic_slice(g_small["lru_conv_w"].reshape(4, LRU_W), (0, me * 128), (4, 128))
    gs = lax.dynamic_slice(g_small["ssd_conv_w"].reshape(4, SSD_CONV_CH), (0, me * 192), (4, 192))
    cat = lambda d: jnp.pad(jnp.concatenate([d["lru_conv_w"], d["ssd_conv_w"]], axis=1), ((0, 4), (0, 64)))
    res = _adamw(cat({"lru_conv_w": gl, "ssd_conv_w": gs})[None], cat(w), cat(m), cat(v), "adamw_conv", 8)
    for kind, r in zip(("grad", "delta", "new_m", "new_v"), res):
        outs[kind + "_lru_conv_w"] = r[0:4, 0:128]
        outs[kind + "_ssd_conv_w"] = r[0:4, 128:320]

    order = ["pre_mix_norm", "w_in", "lru_conv_w", "lru_conv_b", "lru_wa", "lru_ba", "lru_wx", "lru_bx", "lru_lambda",
             "lru_out_norm", "ssd_conv_w", "ssd_conv_b", "ssd_dt_bias", "ssd_a_log", "ssd_d", "ssd_out_norm", "w_out",
             "post_mix_norm", "pre_ffn_norm", "w_gate", "w_up", "w_down", "post_ffn_norm"]
    result = [g_small["loss"].reshape(()), grad_x[None]]
    for kind in ("grad", "delta", "new_m", "new_v"):
        result += [outs[kind + "_" + n][None] for n in order]
    return tuple(result)
```

```python
import functools

import jax
import jax.numpy as jnp
from jax import lax
from jax.experimental import pallas as pl
from jax.experimental.pallas import tpu as pltpu

F32 = jnp.float32
BF16 = jnp.bfloat16
HI = lax.Precision.HIGHEST
EPS = 1e-6
N_DEV = 8
D_MODEL = 1024
LRU_W = 1024
SSD_INNER = 1024
SSD_HEADS = 16
SSD_HEAD_DIM = 64
SSD_STATE = 128
SSD_GROUPS = 2
SSD_CONV_CH = 1536
CHUNK = 128
D_FF = 2816
FF_HALF = D_FF // 2
IN_MAIN = 4608
IN_COLS = 4624
LANE = 128
TT = 256
VMEM_LIMIT = 56 * 1024 * 1024
ADAM_LR, ADAM_B1, ADAM_B2, ADAM_EPS, ADAM_WD, ADAM_STEP = 0.001, 0.9, 0.999, 1e-08, 0.01, 10
PACK_ROWS = 1920
SMALL_ROWS = 160

NT = (((1,), (1,)), ((), ()))
TN = (((0,), (0,)), ((), ()))


def _params(n_grid):
    return pltpu.CompilerParams(dimension_semantics=("arbitrary",) * n_grid, vmem_limit_bytes=VMEM_LIMIT)


def _dot(a, b, dims=None, precision=None):
    if dims is None:
        return jnp.dot(a, b, preferred_element_type=F32, precision=precision)
    return lax.dot_general(a, b, dims, preferred_element_type=F32, precision=precision)


def _sigmoid(x):
    return 1.0 / (1.0 + jnp.exp(-x))


def _softplus(x):
    e = jnp.exp(-jnp.abs(x))
    l1p = jnp.where(e < 1e-3, e * (1.0 - e * (0.5 - e * (1.0 / 3.0))), jnp.log(1.0 + e))
    return jnp.maximum(x, 0.0) + l1p


def _neg_expm1(x):
    series = -x * (1.0 + x * (0.5 + x * (1.0 / 6.0 + x * (1.0 / 24.0))))
    return jnp.where(x > -0.01, series, 1.0 - jnp.exp(x))


_GELU_C = 0.7978845608028654


def _gelu(x):
    t = jnp.tanh(_GELU_C * (x + 0.044715 * x * x * x))
    return 0.5 * x * (1.0 + t), t


def _gelu_grad(x, t):
    return 0.5 * (1.0 + t) + 0.5 * x * (1.0 - t * t) * _GELU_C * (1.0 + 3.0 * 0.044715 * x * x)


def _rms(x):
    return lax.rsqrt(jnp.mean(x * x, axis=-1, keepdims=True) + EPS)


def _rms_bwd(dyn, x, rn):
    return rn * dyn - x * (rn * rn * rn) * jnp.mean(dyn * x, axis=-1, keepdims=True)


def _row(x, r):
    idx = lax.broadcasted_iota(jnp.int32, x.shape, 0)
    return jnp.sum(jnp.where(idx == r, x, 0.0), axis=0, keepdims=True)


def _shift_down(cur, prev8, j):
    s = pltpu.roll(cur, j, 0)
    p = pltpu.roll(prev8, j, 0)
    r8 = lax.broadcasted_iota(jnp.int32, prev8.shape, 0)
    top = jnp.where(r8 < j, p, s[0:8])
    return jnp.concatenate([top, s[8:]], axis=0)


def _shift_up(cur, next8, j):
    n = cur.shape[0]
    s = pltpu.roll(cur, n - j, 0)
    p = pltpu.roll(next8, 8 - j, 0)
    r8 = lax.broadcasted_iota(jnp.int32, next8.shape, 0)
    bot = jnp.where(r8 >= 8 - j, p, s[n - 8:n])
    return jnp.concatenate([s[:n - 8], bot], axis=0)


def _scan_fwd(a, u):
    n = a.shape[0]
    row = lax.broadcasted_iota(jnp.int32, a.shape, 0)
    k = 1
    while k < n:
        ok = row >= k
        a_s = jnp.where(ok, pltpu.roll(a, k, 0), 1.0)
        u_s = jnp.where(ok, pltpu.roll(u, k, 0), 0.0)
        u = a * u_s + u
        a = a * a_s
        k *= 2
    return a, u


def _scan_bwd(b, d):
    n = b.shape[0]
    row = lax.broadcasted_iota(jnp.int32, b.shape, 0)
    k = 1
    while k < n:
        ok = row < n - k
        b_s = jnp.where(ok, pltpu.roll(b, n - k, 0), 1.0)
        d_s = jnp.where(ok, pltpu.roll(d, n - k, 0), 0.0)
        d = b * d_s + d
        b = b * b_s
        k *= 2
    return b, d


def _inproj_fwd(x, g_pre, w_main, w_dt):
    T = x.shape[0]

    def body(x_ref, g_ref, wm_hbm, wd_hbm, h_ref, proj_ref, dtp_ref, wm, wd, sem):
        @pl.when(pl.program_id(0) == 0)
        def _():
            c1 = pltpu.make_async_copy(wm_hbm, wm, sem.at[0])
            c2 = pltpu.make_async_copy(wd_hbm, wd, sem.at[1])
            c1.start()
            c2.start()
            c1.wait()
            c2.wait()

        xv = x_ref[...]
        h = (xv * _rms(xv) * g_ref[...]).astype(BF16)
        h_ref[...] = h
        proj_ref[...] = _dot(h, wm[...])
        dtp_ref[...] = _dot(h, wd[...])

    return pl.pallas_call(
        body, name="inproj_fwd", grid=(T // TT,),
        in_specs=[pl.BlockSpec((TT, D_MODEL), lambda i: (i, 0)),
                  pl.BlockSpec((1, D_MODEL), lambda i: (0, 0)),
                  pl.BlockSpec(memory_space=pl.ANY), pl.BlockSpec(memory_space=pl.ANY)],
        out_specs=[pl.BlockSpec((TT, D_MODEL), lambda i: (i, 0)),
                   pl.BlockSpec((TT, IN_MAIN), lambda i: (i, 0)),
                   pl.BlockSpec((TT, LANE), lambda i: (i, 0))],
        out_shape=[jax.ShapeDtypeStruct((T, D_MODEL), BF16), jax.ShapeDtypeStruct((T, IN_MAIN), F32),
                   jax.ShapeDtypeStruct((T, LANE), F32)],
        scratch_shapes=[pltpu.VMEM((D_MODEL, IN_MAIN), BF16), pltpu.VMEM((D_MODEL, LANE), BF16),
                        pltpu.SemaphoreType.DMA((2,))],
        compiler_params=_params(1),
    )(x, g_pre, w_main, w_dt)


def _inproj_bwd(dp_lru, dp_ssd, ddtp, dx1, x, g_pre, w_main, w_dt):
    T = x.shape[0]

    def body(dl_ref, ds_ref, dd_ref, dx1_ref, x_ref, g_ref, wm_hbm, wd_hbm, gx_ref, dg_ref, wm, wd, sem):
        @pl.when(pl.program_id(0) == 0)
        def _():
            c1 = pltpu.make_async_copy(wm_hbm, wm, sem.at[0])
            c2 = pltpu.make_async_copy(wd_hbm, wd, sem.at[1])
            c1.start()
            c2.start()
            c1.wait()
            c2.wait()
            dg_ref[...] = jnp.zeros_like(dg_ref)

        dh = _dot(dl_ref[...], wm[:, 0:2048], NT)
        dh += _dot(ds_ref[...], wm[:, 2048:IN_MAIN], NT)
        dh += _dot(dd_ref[...], wd[...], NT)
        xv = x_ref[...]
        rn = _rms(xv)
        dg_ref[...] += jnp.sum(dh * xv * rn, axis=0, keepdims=True)
        gx_ref[...] = dx1_ref[...] + _rms_bwd(dh * g_ref[...], xv, rn)

    return pl.pallas_call(
        body, name="inproj_bwd", grid=(T // TT,),
        in_specs=[pl.BlockSpec((TT, 2048), lambda i: (i, 0)),
                  pl.BlockSpec((TT, 2560), lambda i: (i, 0)),
                  pl.BlockSpec((TT, LANE), lambda i: (i, 0)),
                  pl.BlockSpec((TT, D_MODEL), lambda i: (i, 0)),
                  pl.BlockSpec((TT, D_MODEL), lambda i: (i, 0)),
                  pl.BlockSpec((1, D_MODEL), lambda i: (0, 0)),
                  pl.BlockSpec(memory_space=pl.ANY), pl.BlockSpec(memory_space=pl.ANY)],
        out_specs=[pl.BlockSpec((TT, D_MODEL), lambda i: (i, 0)),
                   pl.BlockSpec((1, D_MODEL), lambda i: (0, 0))],
        out_shape=[jax.ShapeDtypeStruct((T, D_MODEL), F32), jax.ShapeDtypeStruct((1, D_MODEL), F32)],
        scratch_shapes=[pltpu.VMEM((D_MODEL, IN_MAIN), BF16), pltpu.VMEM((D_MODEL, LANE), BF16),
                        pltpu.SemaphoreType.DMA((2,))],
        compiler_params=_params(1),
    )(dp_lru, dp_ssd, ddtp, dx1, x, g_pre, w_main, w_dt)


def _lru_gates(lx, wa_ref, wx_ref, ba, bx, lam):
    lxb = lx.astype(BF16)
    pa = jnp.concatenate([_dot(lxb[:, 256 * k:256 * (k + 1)], wa_ref[k]) for k in range(4)], axis=1) + ba
    px = jnp.concatenate([_dot(lxb[:, 256 * k:256 * (k + 1)], wx_ref[k]) for k in range(4)], axis=1) + bx
    r = _sigmoid(pa)
    ig = _sigmoid(px)
    sp = _softplus(-lam)
    log_a = -8.0 * r * sp
    a = jnp.exp(log_a)
    mult = jnp.sqrt(_neg_expm1(2.0 * log_a))
    return r, ig, sp, a, mult


def _lru_fwd(proj, conv_w, conv_b, wa_bd, wx_bd, ba, bx, lam, g_lru):
    T = proj.shape[0]

    def body(cx_ref, gate_ref, cw_ref, cb_ref, wa_ref, wx_ref, ba_ref, bx_ref, lam_ref, g_ref,
             lx_ref, hl_ref, y_ref, tail, hcar):
        @pl.when(pl.program_id(0) == 0)
        def _():
            tail[...] = jnp.zeros_like(tail)
            hcar[...] = jnp.zeros_like(hcar)

        cx = cx_ref[...]
        prev8 = tail[...]
        lx = cb_ref[...] + cw_ref[3:4, :] * cx
        for j in range(1, 4):
            lx += cw_ref[3 - j:4 - j, :] * _shift_down(cx, prev8, j)
        tail[...] = cx[TT - 8:TT]
        lx_ref[...] = lx
        r, ig, sp, a, mult = _lru_gates(lx, wa_ref, wx_ref, ba_ref[...], bx_ref[...], lam_ref[...])
        acum, h0 = _scan_fwd(a, mult * (ig * lx))
        h = h0 + acum * hcar[...]
        hl_ref[...] = h
        hcar[...] = hl_ref[TT - 1:TT, :]
        ge, _ = _gelu(gate_ref[...])
        p = h * ge
        y_ref[...] = (p * _rms(p) * g_ref[...]).astype(BF16)

    vec = pl.BlockSpec((1, LRU_W), lambda i: (0, 0))
    bd = pl.BlockSpec((4, 256, 256), lambda i: (0, 0, 0))
    tile = pl.BlockSpec((TT, LRU_W), lambda i: (i, 0))
    return pl.pallas_call(
        body, name="lru_fwd", grid=(T // TT,),
        in_specs=[tile, pl.BlockSpec((TT, LRU_W), lambda i: (i, 1)),
                  pl.BlockSpec((4, LRU_W), lambda i: (0, 0)), vec, bd, bd, vec, vec, vec, vec],
        out_specs=[tile, tile, tile],
        out_shape=[jax.ShapeDtypeStruct((T, LRU_W), F32), jax.ShapeDtypeStruct((T, LRU_W), F32),
                   jax.ShapeDtypeStruct((T, LRU_W), BF16)],
        scratch_shapes=[pltpu.VMEM((8, LRU_W), F32), pltpu.VMEM((1, LRU_W), F32)],
        compiler_params=_params(1),
    )(proj, proj, conv_w, conv_b, wa_bd, wx_bd, ba, bx, lam, g_lru)


def _lru_bwd(dy, proj, lx, hl, conv_w, wa_bd, wx_bd, ba, bx, lam, g_lru):
    T = proj.shape[0]
    nt = T // TT

    def body(dy_ref, cx_ref, gate_ref, lx_ref, hl_ref, halo_ref, cw_ref, wa_ref, wx_ref, ba_ref, bx_ref,
             lam_ref, g_ref, dp_ref, dpa_ref, dpx_ref, dcw_ref, dcb_ref, dba_ref, dbx_ref, dlam_ref, dg_ref,
             gcar, acar, head):
        i = pl.program_id(0)

        @pl.when(i == 0)
        def _():
            gcar[...] = jnp.zeros_like(gcar)
            acar[...] = jnp.zeros_like(acar)
            head[...] = jnp.zeros_like(head)
            for ref in (dcw_ref, dcb_ref, dba_ref, dbx_ref, dlam_ref, dg_ref):
                ref[...] = jnp.zeros_like(ref)

        lx = lx_ref[...]
        h = hl_ref[...]
        gate = gate_ref[...]
        cx = cx_ref[...]
        lam = lam_ref[...]
        r, ig, sp, a, mult = _lru_gates(lx, wa_ref, wx_ref, ba_ref[...], bx_ref[...], lam)
        ge, th = _gelu(gate)
        p = h * ge
        rn = _rms(p)
        dyv = dy_ref[...]
        dg_ref[...] += jnp.sum(dyv * p * rn, axis=0, keepdims=True)
        dp = _rms_bwd(dyv * g_ref[...], p, rn)
        dp_ref[:, LRU_W:2 * LRU_W] = (dp * h * _gelu_grad(gate, th)).astype(BF16)
        dh = dp * ge
        row = lax.broadcasted_iota(jnp.int32, a.shape, 0)
        b = jnp.where(row == TT - 1, acar[...], pltpu.roll(a, TT - 1, 0))
        bcum, g0 = _scan_bwd(b, dh)
        g = g0 + bcum * gcar[...]
        gcar[...] = _row(g[0:8], 0)
        acar[...] = _row(a[0:8], 0)
        h_last_prev = halo_ref[7:8, :] * (i < nt - 1).astype(F32)
        hprev = jnp.where(row == 0, h_last_prev, pltpu.roll(h, 1, 0))
        da = g * hprev
        dm2 = (g * (ig * lx)) * 0.5 / mult
        dlog_a = da * a - 2.0 * a * a * dm2
        dlam_ref[...] += jnp.sum(-8.0 * r * dlog_a, axis=0, keepdims=True) * (-_sigmoid(-lam))
        dpa = (-8.0 * sp * dlog_a) * r * (1.0 - r)
        dpx = (g * mult * lx) * ig * (1.0 - ig)
        dba_ref[...] += jnp.sum(dpa, axis=0, keepdims=True)
        dbx_ref[...] += jnp.sum(dpx, axis=0, keepdims=True)
        dpab = dpa.astype(BF16)
        dpxb = dpx.astype(BF16)
        dpa_ref[...] = dpab
        dpx_ref[...] = dpxb
        dlx = g * mult * ig + jnp.concatenate(
            [_dot(dpab[:, 256 * k:256 * (k + 1)], wa_ref[k], NT) + _dot(dpxb[:, 256 * k:256 * (k + 1)], wx_ref[k], NT)
             for k in range(4)], axis=1)
        nxt = head[...]
        dcb_ref[...] += jnp.sum(dlx, axis=0, keepdims=True)
        dcx = cw_ref[3:4, :] * dlx
        dcw_ref[3:4, :] += jnp.sum(cx * dlx, axis=0, keepdims=True)
        for j in range(1, 4):
            sh = _shift_up(dlx, nxt, j)
            dcx += cw_ref[3 - j:4 - j, :] * sh
            dcw_ref[3 - j:4 - j, :] += jnp.sum(cx * sh, axis=0, keepdims=True)
        head[...] = dlx[0:8]
        dp_ref[:, 0:LRU_W] = dcx.astype(BF16)

    rev = lambda i: (nt - 1 - i, 0)
    vec = pl.BlockSpec((1, LRU_W), lambda i: (0, 0))
    bd = pl.BlockSpec((4, 256, 256), lambda i: (0, 0, 0))
    tile = pl.BlockSpec((TT, LRU_W), rev)
    halo = pl.BlockSpec((8, LRU_W), lambda i: (jnp.maximum((nt - 1 - i) * (TT // 8) - 1, 0), 0))
    cw = pl.BlockSpec((4, LRU_W), lambda i: (0, 0))
    return pl.pallas_call(
        body, name="lru_bwd", grid=(nt,),
        in_specs=[tile, tile, pl.BlockSpec((TT, LRU_W), lambda i: (nt - 1 - i, 1)), tile, tile, halo,
                  cw, bd, bd, vec, vec, vec, vec],
        out_specs=[pl.BlockSpec((TT, 2 * LRU_W), rev), tile, tile, cw, vec, vec, vec, vec, vec],
        out_shape=[jax.ShapeDtypeStruct((T, 2 * LRU_W), BF16), jax.ShapeDtypeStruct((T, LRU_W), BF16),
                   jax.ShapeDtypeStruct((T, LRU_W), BF16), jax.ShapeDtypeStruct((4, LRU_W), F32)]
                  + [jax.ShapeDtypeStruct((1, LRU_W), F32)] * 5,
        scratch_shapes=[pltpu.VMEM((1, LRU_W), F32), pltpu.VMEM((1, LRU_W), F32), pltpu.VMEM((8, LRU_W), F32)],
        compiler_params=_params(1),
    )(dy, proj, proj, lx, hl, hl, conv_w, wa_bd, wx_bd, ba, bx, lam, g_lru)


def _ssd_chunk_terms(xc, dtp, bias, alog, expand):
    sg = _sigmoid(xc)
    xbc = xc * sg
    pre = dtp + bias
    dt = _softplus(pre)
    A = -jnp.exp(alog)
    ri = lax.broadcasted_iota(jnp.int32, (CHUNK, CHUNK), 0)
    ci = lax.broadcasted_iota(jnp.int32, (CHUNK, CHUNK), 1)
    tril = (ri >= ci).astype(F32)
    cs = _dot(tril, dt * A, precision=HI)
    cs_last = _row(cs, CHUNK - 1)
    ecs = jnp.exp(cs)
    dec = jnp.exp(cs_last - cs)
    return dict(sg=sg, xbc=xbc, pre=pre, dt=dt, A=A, cs=cs, csT=cs.T, ecs=ecs, dec=dec, ri=ri, ci=ci,
                dt_e=_dot(dt, expand, precision=HI), ecs_e=_dot(ecs, expand, precision=HI),
                dec_e=_dot(dec, expand, precision=HI))


def _head_lambda(t, h):
    col = jnp.sum(jnp.where(t["ci"] == h, t["cs"], 0.0), axis=1, keepdims=True)
    rowv = jnp.sum(jnp.where(t["ri"] == h, t["csT"], 0.0), axis=0, keepdims=True)
    return jnp.exp(jnp.where(t["ri"] >= t["ci"], col - rowv, -1e30))


def _ssd_fwd(proj, dtp, conv_w, conv_b, dt_bias, a_log, d_e, g_ssd, expand):
    T = proj.shape[0]
    nc = T // CHUNK

    def body(z_ref, xp_ref, dtp_ref, cw_ref, cb_ref, bias_ref, alog_ref, de_ref, g_ref, ex_ref,
             xc_ref, y_ref, yn_ref, sprev_ref, tail, S):
        @pl.when(pl.program_id(0) == 0)
        def _():
            tail[...] = jnp.zeros_like(tail)
            S[...] = jnp.zeros_like(S)

        xp = xp_ref[...]
        prev8 = tail[...]
        xc = cb_ref[...] + cw_ref[3:4, :] * xp
        for j in range(1, 4):
            xc += cw_ref[3 - j:4 - j, :] * _shift_down(xp, prev8, j)
        tail[...] = xp[CHUNK - 8:CHUNK]
        xc_ref[...] = xc
        t = _ssd_chunk_terms(xc, dtp_ref[...], bias_ref[...], alog_ref[...], ex_ref[...])
        xbc = t["xbc"]
        sx = xbc[:, 0:SSD_INNER]
        Bb = xbc[:, SSD_INNER:SSD_INNER + 256].astype(BF16)
        Cb = xbc[:, SSD_INNER + 256:SSD_CONV_CH].astype(BF16)
        X = t["dt_e"] * sx
        lane = lax.broadcasted_iota(jnp.int32, (CHUNK, LANE), 1)
        G = [_dot(Cb[:, 128 * g:128 * (g + 1)], Bb[:, 128 * g:128 * (g + 1)], NT) for g in range(SSD_GROUPS)]
        for k in range(SSD_HEADS // 2):
            Xp = X[:, 128 * k:128 * (k + 1)]
            acc = jnp.zeros((CHUNK, LANE), F32)
            for half in range(2):
                M = (G[k // 4] * _head_lambda(t, 2 * k + half)).astype(BF16)
                Xh = jnp.where((lane >= 64) if half else (lane < 64), Xp, 0.0).astype(BF16)
                acc += _dot(M, Xh)
            y_ref[:, 128 * k:128 * (k + 1)] = acc
        sprev_ref[0] = S[...]
        eL_e = _row(t["ecs_e"], CHUNK - 1)
        Xd = (X * t["dec_e"]).astype(BF16)
        for g in range(SSD_GROUPS):
            sl = slice(512 * g, 512 * (g + 1))
            Sg = S[:, sl]
            y_ref[:, sl] += t["ecs_e"][:, sl] * _dot(Cb[:, 128 * g:128 * (g + 1)], Sg.astype(BF16))
            S[:, sl] = eL_e[:, sl] * Sg + _dot(Bb[:, 128 * g:128 * (g + 1)], Xd[:, sl], TN)
        y = y_ref[...] + de_ref[...] * sx
        y_ref[...] = y
        z = z_ref[...]
        q = y * (z * _sigmoid(z))
        yn_ref[...] = (q * _rms(q) * g_ref[...]).astype(BF16)

    c0 = lambda i: (0, 0)
    return pl.pallas_call(
        body, name="ssd_fwd", grid=(nc,),
        in_specs=[pl.BlockSpec((CHUNK, SSD_INNER), lambda i: (i, 2)),
                  pl.BlockSpec((CHUNK, SSD_CONV_CH), lambda i: (i, 2)),
                  pl.BlockSpec((CHUNK, LANE), lambda i: (i, 0)),
                  pl.BlockSpec((4, SSD_CONV_CH), c0), pl.BlockSpec((1, SSD_CONV_CH), c0),
                  pl.BlockSpec((1, LANE), c0), pl.BlockSpec((1, LANE), c0),
                  pl.BlockSpec((1, SSD_INNER), c0), pl.BlockSpec((1, SSD_INNER), c0),
                  pl.BlockSpec((LANE, SSD_INNER), c0)],
        out_specs=[pl.BlockSpec((CHUNK, SSD_CONV_CH), lambda i: (i, 0)),
                   pl.BlockSpec((CHUNK, SSD_INNER), lambda i: (i, 0)),
                   pl.BlockSpec((CHUNK, SSD_INNER), lambda i: (i, 0)),
                   pl.BlockSpec((1, SSD_STATE, SSD_INNER), lambda i: (i, 0, 0))],
        out_shape=[jax.ShapeDtypeStruct((T, SSD_CONV_CH), F32), jax.ShapeDtypeStruct((T, SSD_INNER), F32),
                   jax.ShapeDtypeStruct((T, SSD_INNER), BF16),
                   jax.ShapeDtypeStruct((nc, SSD_STATE, SSD_INNER), F32)],
        scratch_shapes=[pltpu.VMEM((8, SSD_CONV_CH), F32), pltpu.VMEM((SSD_STATE, SSD_INNER), F32)],
        compiler_params=_params(1),
    )(proj, proj, dtp, conv_w, conv_b, dt_bias, a_log, d_e, g_ssd, expand)


def _ssd_bwd(dyn, proj, dtp, xc, y, sprev, conv_w, dt_bias, a_log, d_e, g_ssd, expand):
    T = proj.shape[0]
    nc = T // CHUNK

    def body(dyn_ref, z_ref, xp_ref, dtp_ref, xc_ref, y_ref, sprev_ref, cw_ref, bias_ref, alog_ref, de_ref,
             g_ref, ex_ref, dp_ref, ddtp_ref, dcw_ref, dcb_ref, dbias_ref, dA_ref, dD_ref, dg_ref,
             dS, head, dX_s, dxbc_s):
        @pl.when(pl.program_id(0) == 0)
        def _():
            dS[...] = jnp.zeros_like(dS)
            head[...] = jnp.zeros_like(head)
            for ref in (dcw_ref, dcb_ref, dbias_ref, dA_ref, dD_ref, dg_ref):
                ref[...] = jnp.zeros_like(ref)

        ex = ex_ref[...]
        xc = xc_ref[...]
        t = _ssd_chunk_terms(xc, dtp_ref[...], bias_ref[...], alog_ref[...], ex)
        ri, ci = t["ri"], t["ci"]
        xbc = t["xbc"]
        sx = xbc[:, 0:SSD_INNER]
        Bb = xbc[:, SSD_INNER:SSD_INNER + 256].astype(BF16)
        Cb = xbc[:, SSD_INNER + 256:SSD_CONV_CH].astype(BF16)
        X = t["dt_e"] * sx
        z = z_ref[...]
        sz = _sigmoid(z)
        siluz = z * sz
        yv = y_ref[...]
        q = yv * siluz
        rn = _rms(q)
        dynv = dyn_ref[...]
        dg_ref[...] += jnp.sum(dynv * q * rn, axis=0, keepdims=True)
        dq = _rms_bwd(dynv * g_ref[...], q, rn)
        dp_ref[:, 0:SSD_INNER] = (dq * yv * (sz * (1.0 + z * (1.0 - sz)))).astype(BF16)
        dY = dq * siluz
        dD_ref[...] += jnp.sum(dY * sx, axis=0, keepdims=True)
        dYb = dY.astype(BF16)
        lane = lax.broadcasted_iota(jnp.int32, (CHUNK, LANE), 1)
        dcs = jnp.zeros((CHUNK, CHUNK), F32)
        dcsT = jnp.zeros((CHUNK, CHUNK), F32)
        Xb = X.astype(BF16)
        for g in range(SSD_GROUPS):
            Bg = Bb[:, 128 * g:128 * (g + 1)]
            Cg = Cb[:, 128 * g:128 * (g + 1)]
            G = _dot(Cg, Bg, NT)
            dGsum = jnp.zeros((CHUNK, CHUNK), F32)
            for k in range(4 * g, 4 * g + 4):
                Xp = Xb[:, 128 * k:128 * (k + 1)]
                dYp = dY[:, 128 * k:128 * (k + 1)]
                dXp = jnp.zeros((CHUNK, LANE), F32)
                for half in range(2):
                    h = 2 * k + half
                    lam = _head_lambda(t, h)
                    M = G * lam
                    dYh = jnp.where((lane >= 64) if half else (lane < 64), dYp, 0.0).astype(BF16)
                    dM = _dot(dYh, Xp, NT)
                    W = dM * M
                    dcs += jnp.where(ci == h, jnp.sum(W, axis=1, keepdims=True), 0.0)
                    dcsT += jnp.where(ri == h, jnp.sum(W, axis=0, keepdims=True), 0.0)
                    dGsum += dM * lam
                    dXp += _dot(M.astype(BF16), dYh, TN)
                dX_s[:, 128 * k:128 * (k + 1)] = dXp
            dGb = dGsum.astype(BF16)
            dxbc_s[:, SSD_INNER + 256 + 128 * g:SSD_INNER + 256 + 128 * (g + 1)] = _dot(dGb, Bg)
            dxbc_s[:, SSD_INNER + 128 * g:SSD_INNER + 128 * (g + 1)] = _dot(dGb, Cg, TN)
        dcs = dcs - dcsT.T
        Sp = sprev_ref[0]
        dSv = dS[...]
        ecs_e, dec_e = t["ecs_e"], t["dec_e"]
        eL_e = _row(ecs_e, CHUNK - 1)
        dYe = dY * ecs_e
        dYeb = dYe.astype(BF16)
        Xd = X * dec_e
        Xdb = Xd.astype(BF16)
        for g in range(SSD_GROUPS):
            sl = slice(512 * g, 512 * (g + 1))
            Bg = Bb[:, 128 * g:128 * (g + 1)]
            Cg = Cb[:, 128 * g:128 * (g + 1)]
            Spb = Sp[:, sl].astype(BF16)
            dSb = dSv[:, sl].astype(BF16)
            CS = _dot(Cg, Spb)
            BS = _dot(Bg, dSb)
            dxbc_s[:, SSD_INNER + 256 + 128 * g:SSD_INNER + 256 + 128 * (g + 1)] += _dot(dYeb[:, sl], Spb, NT)
            dxbc_s[:, SSD_INNER + 128 * g:SSD_INNER + 128 * (g + 1)] += _dot(Xdb[:, sl], dSb, NT)
            dS[:, sl] = eL_e[:, sl] * dSv[:, sl] + _dot(Cg, dYeb[:, sl], TN)
            dX_s[:, sl] += dec_e[:, sl] * BS
            dcs += _dot(dYe[:, sl] * CS, ex[:, sl], NT, precision=HI)
            tdec = _dot(X[:, sl] * BS, ex[:, sl], NT, precision=HI) * t["dec"]
            dcs -= tdec
            last = jnp.sum(tdec, axis=0, keepdims=True)
            last += jnp.sum(_dot(Sp[:, sl] * dSv[:, sl], ex[:, sl], NT, precision=HI), axis=0, keepdims=True) \
                * _row(t["ecs"], CHUNK - 1)
            dcs += jnp.where(ri == CHUNK - 1, last, 0.0)
        triu = (ci >= ri).astype(F32)
        da = _dot(triu, dcs, precision=HI)
        dX = dX_s[...]
        ddt = da * t["A"] + _dot(dX * sx, ex, NT, precision=HI)
        dA_ref[...] += jnp.sum(da * t["dt"], axis=0, keepdims=True)
        ddtp = ddt * _sigmoid(t["pre"])
        dbias_ref[...] += jnp.sum(ddtp, axis=0, keepdims=True)
        ddtp_ref[...] = ddtp.astype(BF16)
        dxbc_s[:, 0:SSD_INNER] = dX * t["dt_e"] + de_ref[...] * dY
        sg = t["sg"]
        dxc = dxbc_s[...] * (sg * (1.0 + xc * (1.0 - sg)))
        xp = xp_ref[...]
        nxt = head[...]
        dcb_ref[...] += jnp.sum(dxc, axis=0, keepdims=True)
        dpre = cw_ref[3:4, :] * dxc
        dcw_ref[3:4, :] += jnp.sum(xp * dxc, axis=0, keepdims=True)
        for j in range(1, 4):
            sh = _shift_up(dxc, nxt, j)
            dpre += cw_ref[3 - j:4 - j, :] * sh
            dcw_ref[3 - j:4 - j, :] += jnp.sum(xp * sh, axis=0, keepdims=True)
        head[...] = dxc[0:8]
        dp_ref[:, SSD_INNER:SSD_INNER + SSD_CONV_CH] = dpre.astype(BF16)

    c0 = lambda i: (0, 0)
    rev = lambda i: (nc - 1 - i, 0)
    return pl.pallas_call(
        body, name="ssd_bwd", grid=(nc,),
        in_specs=[pl.BlockSpec((CHUNK, SSD_INNER), rev),
                  pl.BlockSpec((CHUNK, SSD_INNER), lambda i: (nc - 1 - i, 2)),
                  pl.BlockSpec((CHUNK, SSD_CONV_CH), lambda i: (nc - 1 - i, 2)),
                  pl.BlockSpec((CHUNK, LANE), rev),
                  pl.BlockSpec((CHUNK, SSD_CONV_CH), rev),
                  pl.BlockSpec((CHUNK, SSD_INNER), rev),
                  pl.BlockSpec((1, SSD_STATE, SSD_INNER), lambda i: (nc - 1 - i, 0, 0)),
                  pl.BlockSpec((4, SSD_CONV_CH), c0), pl.BlockSpec((1, LANE), c0), pl.BlockSpec((1, LANE), c0),
                  pl.BlockSpec((1, SSD_INNER), c0), pl.BlockSpec((1, SSD_INNER), c0),
                  pl.BlockSpec((LANE, SSD_INNER), c0)],
        out_specs=[pl.BlockSpec((CHUNK, 2560), rev), pl.BlockSpec((CHUNK, LANE), rev),
                   pl.BlockSpec((4, SSD_CONV_CH), c0), pl.BlockSpec((1, SSD_CONV_CH), c0),
                   pl.BlockSpec((1, LANE), c0), pl.BlockSpec((1, LANE), c0),
                   pl.BlockSpec((1, SSD_INNER), c0), pl.BlockSpec((1, SSD_INNER), c0)],
        out_shape=[jax.ShapeDtypeStruct((T, 2560), BF16), jax.ShapeDtypeStruct((T, LANE), BF16),
                   jax.ShapeDtypeStruct((4, SSD_CONV_CH), F32), jax.ShapeDtypeStruct((1, SSD_CONV_CH), F32),
                   jax.ShapeDtypeStruct((1, LANE), F32), jax.ShapeDtypeStruct((1, LANE), F32),
                   jax.ShapeDtypeStruct((1, SSD_INNER), F32), jax.ShapeDtypeStruct((1, SSD_INNER), F32)],
        scratch_shapes=[pltpu.VMEM((SSD_STATE, SSD_INNER), F32), pltpu.VMEM((8, SSD_CONV_CH), F32),
                        pltpu.VMEM((CHUNK, SSD_INNER), F32), pltpu.VMEM((CHUNK, SSD_CONV_CH), F32)],
        compiler_params=_params(1),
    )(dyn, proj, proj, dtp, xc, y, sprev, conv_w, dt_bias, a_log, d_e, g_ssd, expand)


def _outproj_fwd(x, y_lru, y_ssd, w_out, g_pm, g_pf):
    T = x.shape[0]

    def body(x_ref, yl_ref, ys_ref, wo_ref, gpm_ref, gpf_ref, mix_ref, x1_ref, h2_ref):
        mix = _dot(yl_ref[...], wo_ref[0:LRU_W, :]) + _dot(ys_ref[...], wo_ref[LRU_W:2 * LRU_W, :])
        mix_ref[...] = mix
        x1 = x_ref[...] + mix * _rms(mix) * gpm_ref[...]
        x1_ref[...] = x1
        h2_ref[...] = (x1 * _rms(x1) * gpf_ref[...]).astype(BF16)

    tile = pl.BlockSpec((TT, D_MODEL), lambda i: (i, 0))
    vec = pl.BlockSpec((1, D_MODEL), lambda i: (0, 0))
    return pl.pallas_call(
        body, name="outproj_fwd", grid=(T // TT,),
        in_specs=[tile, tile, tile, pl.BlockSpec((2 * LRU_W, D_MODEL), lambda i: (0, 0)), vec, vec],
        out_specs=[tile, tile, tile],
        out_shape=[jax.ShapeDtypeStruct((T, D_MODEL), F32), jax.ShapeDtypeStruct((T, D_MODEL), F32),
                   jax.ShapeDtypeStruct((T, D_MODEL), BF16)],
        compiler_params=_params(1),
    )(x, y_lru, y_ssd, w_out, g_pm, g_pf)


def _ffn_fwd_bwd(x1, h2, target, w_gate, w_up, w_down, g_pf, g_ff):
    T = x1.shape[0]

    def body(x1_ref, h2_ref, tg_ref, wg_hbm, wu_hbm, wd_hbm, gpf_ref, gff_ref,
             dx1_ref, act_ref, df_ref, dgt_ref, dup_ref, dgpf_ref, dgff_ref, loss_ref,
             wg, wu, wd, gt_s, up_s, sem):
        @pl.when(pl.program_id(0) == 0)
        def _():
            cps = [pltpu.make_async_copy(s, d, sem.at[n]) for n, (s, d) in
                   enumerate(((wg_hbm, wg), (wu_hbm, wu), (wd_hbm, wd)))]
            for c in cps:
                c.start()
            for c in cps:
                c.wait()
            for ref in (dgpf_ref, dgff_ref, loss_ref):
                ref[...] = jnp.zeros_like(ref)

        h2 = h2_ref[...]
        f = jnp.zeros((TT, D_MODEL), F32)
        for c in range(2):
            sl = slice(FF_HALF * c, FF_HALF * (c + 1))
            gt = _dot(h2, wg[:, sl])
            up = _dot(h2, wu[:, sl])
            gt_s[:, sl] = gt
            up_s[:, sl] = up
            act = (gt * _sigmoid(gt) * up).astype(BF16)
            act_ref[:, sl] = act
            f += _dot(act, wd[sl, :])
        x1 = x1_ref[...]
        rnf = _rms(f)
        e = x1 + f * rnf * gff_ref[...] - tg_ref[...]
        part = 0.5 * jnp.sum(jnp.sum(e * e, axis=1, keepdims=True), axis=0, keepdims=True) * (1.0 / D_MODEL)
        lane = lax.broadcasted_iota(jnp.int32, (1, LANE), 1)
        loss_ref[...] += jnp.where(lane == 0, part, 0.0)
        dx2 = e * (1.0 / D_MODEL)
        dgff_ref[...] += jnp.sum(dx2 * f * rnf, axis=0, keepdims=True)
        df = _rms_bwd(dx2 * gff_ref[...], f, rnf).astype(BF16)
        df_ref[...] = df
        dh2 = jnp.zeros((TT, D_MODEL), F32)
        for c in range(2):
            sl = slice(FF_HALF * c, FF_HALF * (c + 1))
            dact = _dot(df, wd[sl, :], NT)
            gt = gt_s[:, sl]
            up = up_s[:, sl]
            sg = _sigmoid(gt)
            dgt = (dact * up * (sg * (1.0 + gt * (1.0 - sg)))).astype(BF16)
            dup = (dact * gt * sg).astype(BF16)
            dgt_ref[:, sl] = dgt
            dup_ref[:, sl] = dup
            dh2 += _dot(dgt, wg[:, sl], NT) + _dot(dup, wu[:, sl], NT)
        rn2 = _rms(x1)
        dgpf_ref[...] += jnp.sum(dh2 * x1 * rn2, axis=0, keepdims=True)
        dx1_ref[...] = dx2 + _rms_bwd(dh2 * gpf_ref[...], x1, rn2)

    tile = pl.BlockSpec((TT, D_MODEL), lambda i: (i, 0))
    wide = pl.BlockSpec((TT, D_FF), lambda i: (i, 0))
    vec = pl.BlockSpec((1, D_MODEL), lambda i: (0, 0))
    hbm = pl.BlockSpec(memory_space=pl.ANY)
    return pl.pallas_call(
        body, name="ffn_fwd_bwd", grid=(T // TT,),
        in_specs=[tile, tile, tile, hbm, hbm, hbm, vec, vec],
        out_specs=[tile, wide, tile, wide, wide, vec, vec, pl.BlockSpec((1, LANE), lambda i: (0, 0))],
        out_shape=[jax.ShapeDtypeStruct((T, D_MODEL), F32), jax.ShapeDtypeStruct((T, D_FF), BF16),
                   jax.ShapeDtypeStruct((T, D_MODEL), BF16), jax.ShapeDtypeStruct((T, D_FF), BF16),
                   jax.ShapeDtypeStruct((T, D_FF), BF16), jax.ShapeDtypeStruct((1, D_MODEL), F32),
                   jax.ShapeDtypeStruct((1, D_MODEL), F32), jax.ShapeDtypeStruct((1, LANE), F32)],
        scratch_shapes=[pltpu.VMEM((D_MODEL, D_FF), BF16), pltpu.VMEM((D_MODEL, D_FF), BF16),
                        pltpu.VMEM((D_FF, D_MODEL), BF16), pltpu.VMEM((TT, D_FF), F32),
                        pltpu.VMEM((TT, D_FF), F32), pltpu.SemaphoreType.DMA((3,))],
        compiler_params=_params(1),
    )(x1, h2, target, w_gate, w_up, w_down, g_pf, g_ff)


def _outproj_bwd(dx1, mix, w_out, g_pm):
    T = dx1.shape[0]

    def body(dx1_ref, mix_ref, wo_ref, gpm_ref, dyl_ref, dys_ref, dmix_ref, dg_ref):
        @pl.when(pl.program_id(0) == 0)
        def _():
            dg_ref[...] = jnp.zeros_like(dg_ref)

        mix = mix_ref[...]
        rn = _rms(mix)
        dx1v = dx1_ref[...]
        dg_ref[...] += jnp.sum(dx1v * mix * rn, axis=0, keepdims=True)
        dmix = _rms_bwd(dx1v * gpm_ref[...], mix, rn).astype(BF16)
        dmix_ref[...] = dmix
        dyl_ref[...] = _dot(dmix, wo_ref[0:LRU_W, :], NT)
        dys_ref[...] = _dot(dmix, wo_ref[LRU_W:2 * LRU_W, :], NT)

    tile = pl.BlockSpec((TT, D_MODEL), lambda i: (i, 0))
    vec = pl.BlockSpec((1, D_MODEL), lambda i: (0, 0))
    return pl.pallas_call(
        body, name="outproj_bwd", grid=(T // TT,),
        in_specs=[tile, tile, pl.BlockSpec((2 * LRU_W, D_MODEL), lambda i: (0, 0)), vec],
        out_specs=[tile, tile, tile, vec],
        out_shape=[jax.ShapeDtypeStruct((T, D_MODEL), F32), jax.ShapeDtypeStruct((T, D_MODEL), F32),
                   jax.ShapeDtypeStruct((T, D_MODEL), BF16), jax.ShapeDtypeStruct((1, D_MODEL), F32)],
        compiler_params=_params(1),
    )(dx1, mix, w_out, g_pm)


def _tn_matmul(a, bs, name, tk=512):
    T, M = a.shape
    nk = T // tk
    nb = len(bs)

    def body(*refs):
        a_ref, b_refs, o_refs, accs = refs[0], refs[1:1 + nb], refs[1 + nb:1 + 2 * nb], refs[1 + 2 * nb:]
        k = pl.program_id(0)

        @pl.when(k == 0)
        def _():
            for acc in accs:
                acc[...] = jnp.zeros_like(acc)

        av = a_ref[...].astype(BF16)
        for b_ref, acc in zip(b_refs, accs):
            acc[...] += _dot(av, b_ref[...], TN)

        @pl.when(k == nk - 1)
        def _():
            for o_ref, acc in zip(o_refs, accs):
                o_ref[...] = acc[...].astype(BF16)

    return pl.pallas_call(
        body, name=name, grid=(nk,),
        in_specs=[pl.BlockSpec((tk, M), lambda k: (k, 0))]
                 + [pl.BlockSpec((tk, b.shape[1]), lambda k: (k, 0)) for b in bs],
        out_specs=[pl.BlockSpec((M, b.shape[1]), lambda k: (0, 0)) for b in bs],
        out_shape=[jax.ShapeDtypeStruct((M, b.shape[1]), BF16) for b in bs],
        scratch_shapes=[pltpu.VMEM((M, b.shape[1]), F32) for b in bs],
        compiler_params=_params(1),
    )(a, *bs)


def _tn_blockdiag(a, b1, b2, name, tk=1024):
    T = a.shape[0]
    tk = min(tk, T)

    def body(a_ref, b1_ref, b2_ref, o1_ref, o2_ref):
        @pl.when(pl.program_id(0) == 0)
        def _():
            o1_ref[...] = jnp.zeros_like(o1_ref)
            o2_ref[...] = jnp.zeros_like(o2_ref)

        for j in range(4):
            sl = slice(256 * j, 256 * (j + 1))
            av = a_ref[:, sl].astype(BF16)
            o1_ref[j] += _dot(av, b1_ref[:, sl], TN)
            o2_ref[j] += _dot(av, b2_ref[:, sl], TN)

    blk = pl.BlockSpec((tk, LRU_W), lambda k: (k, 0))
    out = pl.BlockSpec((4, 256, 256), lambda k: (0, 0, 0))
    return pl.pallas_call(
        body, name=name, grid=(T // tk,),
        in_specs=[blk, blk, blk], out_specs=[out, out],
        out_shape=[jax.ShapeDtypeStruct((4, 256, 256), F32)] * 2,
        compiler_params=_params(1),
    )(a, b1, b2)


def _adamw(parts, w, m, v, name, tr):
    P, R, C = parts.shape

    def body(p_ref, w_ref, m_ref, v_ref, g_ref, d_ref, nm_ref, nv_ref):
        g = p_ref[0].astype(F32)
        for k in range(1, P):
            g = g + p_ref[k].astype(F32)
        g_ref[...] = g
        nm = ADAM_B1 * m_ref[...] + (1.0 - ADAM_B1) * g
        nv = ADAM_B2 * v_ref[...] + (1.0 - ADAM_B2) * (g * g)
        nm_ref[...] = nm
        nv_ref[...] = nv
        m_hat = nm / (1.0 - ADAM_B1 ** ADAM_STEP)
        v_hat = nv / (1.0 - ADAM_B2 ** ADAM_STEP)
        d_ref[...] = -ADAM_LR * (m_hat / (jnp.sqrt(v_hat) + ADAM_EPS) + ADAM_WD * w_ref[...])

    blk = pl.BlockSpec((tr, C), lambda i: (i, 0))
    return pl.pallas_call(
        body, name=name, grid=(R // tr,),
        in_specs=[pl.BlockSpec((P, tr, C), lambda i: (0, i, 0)), blk, blk, blk],
        out_specs=[blk, blk, blk, blk],
        out_shape=[jax.ShapeDtypeStruct((R, C), F32)] * 4,
        compiler_params=_params(1),
    )(parts, w, m, v)


def _peer(k):
    x, y, c = lax.axis_index("x"), lax.axis_index("y"), lax.axis_index("c")
    px = x ^ ((k >> 2) & 1)
    py = y ^ ((k >> 1) & 1)
    pc = c ^ (k & 1)
    return (px, py, pc), 4 * px + 2 * py + pc


def _my_block():
    return 4 * lax.axis_index("x") + 2 * lax.axis_index("y") + lax.axis_index("c")


def _all_gather(shards, name):
    n = len(shards)

    def body(*refs):
        ins, outs = refs[:n], refs[n:2 * n]
        send, recv, loc = refs[2 * n:]
        x, y, c = lax.axis_index("x"), lax.axis_index("y"), lax.axis_index("c")
        sibling = (x, y, 1 - c)
        chips = [(1 - x, y), (x, 1 - y), (1 - x, 1 - y)]
        slot = lambda px, py, pc: 4 * px + 2 * py + pc

        def copy(a, k, block, to, src=None):
            dst = outs[a].at[slot(*block)]
            return pltpu.make_async_remote_copy(
                src_ref=dst if src is None else src, dst_ref=dst, send_sem=send.at[a, k], recv_sem=recv.at[a, k],
                device_id=to, device_id_type=pl.DeviceIdType.MESH)

        mine = [pltpu.make_async_copy(ins[a], outs[a].at[slot(x, y, c)], loc.at[a]) for a in range(n)]
        for cp in mine:
            cp.start()
        first = []
        for a in range(n):
            first.append(copy(a, 0, (x, y, c), sibling, src=ins[a]))
            first += [copy(a, 1 + j, (x, y, c), (*chip, c), src=ins[a]) for j, chip in enumerate(chips)]
        for cp in first:
            cp.start()
        passed = []
        for j, chip in enumerate(chips):
            for a in range(n):
                copy(a, 1 + j, (*chip, c), (x, y, c)).wait_recv()
                fwd = copy(a, 4 + j, (*chip, c), sibling)
                fwd.start()
                passed.append(fwd)
        for a in range(n):
            copy(a, 0, sibling, (x, y, c)).wait_recv()
            for j, chip in enumerate(chips):
                copy(a, 4 + j, (*chip, 1 - c), (x, y, c)).wait_recv()
        for cp in first + passed:
            cp.wait_send()
        for cp in mine:
            cp.wait()

    hbm = pl.BlockSpec(memory_space=pl.ANY)
    return pl.pallas_call(
        body, name=name,
        in_specs=[hbm] * n, out_specs=[hbm] * n,
        out_shape=[jax.ShapeDtypeStruct((N_DEV,) + s.shape, s.dtype) for s in shards],
        scratch_shapes=[pltpu.SemaphoreType.DMA((n, N_DEV - 1)), pltpu.SemaphoreType.DMA((n, N_DEV - 1)),
                        pltpu.SemaphoreType.DMA((n,))],
    )(*shards)


_HBM = pl.BlockSpec(memory_space=pltpu.HBM)
_SEM = pl.BlockSpec(memory_space=pltpu.SEMAPHORE)
_EFFECT = pltpu.SideEffectType.DATAFLOW_SIDE_EFFECTING


def _direct_copies(srcs, lands, send, recv, slab_source):
    me = _my_block()
    cps = []
    for k in range(1, N_DEV):
        to, blk = _peer(k)
        for a, (src, land) in enumerate(zip(srcs, lands)):
            cps.append(pltpu.make_async_remote_copy(
                src_ref=src.at[blk] if slab_source else src, dst_ref=land.at[me],
                send_sem=send.at[a * (N_DEV - 1) + k - 1], recv_sem=recv.at[a * (N_DEV - 1) + k - 1],
                device_id=to, device_id_type=pl.DeviceIdType.MESH))
    return cps


def _exchange_start(srcs, name, slab_source):
    n = len(srcs)
    lands = [pltpu.with_memory_space_constraint(
        lax.empty(s.shape if slab_source else (N_DEV,) + s.shape, s.dtype), pltpu.HBM) for s in srcs]

    def body(*refs):
        ins, land_in = refs[:n], refs[n:2 * n]
        send, recv = refs[2 * n], refs[2 * n + 1]
        token = refs[4 * n + 2]
        for cp in _direct_copies(ins, land_in, send, recv, slab_source):
            cp.start()
        token[...] = jnp.zeros_like(token)

    sems = pltpu.SemaphoreType.DMA((n * (N_DEV - 1),))
    res = pl.pallas_call(
        body, name=name,
        out_shape=(sems, sems, *[pltpu.HBM(s.shape, s.dtype) for s in srcs],
                   *[pltpu.HBM(l.shape, l.dtype) for l in lands], jax.ShapeDtypeStruct((8, LANE), F32)),
        in_specs=[_HBM] * (2 * n),
        out_specs=(_SEM, _SEM, *[_HBM] * (2 * n), pl.BlockSpec(memory_space=pltpu.VMEM)),
        input_output_aliases={i: 2 + i for i in range(2 * n)},
        compiler_params=pltpu.CompilerParams(has_side_effects=_EFFECT),
    )(*[pltpu.with_memory_space_constraint(s, pltpu.HBM) for s in srcs], *lands)
    return dict(send=res[0], recv=res[1], srcs=res[2:2 + n], lands=res[2 + n:2 + 2 * n], token=res[-1],
                slab_source=slab_source)


def _exchange_wait(ex, after, name):
    n = len(ex["srcs"])
    slab_source = ex["slab_source"]

    def body(*refs):
        ins, lands = refs[:n], refs[n:2 * n]
        send, recv = refs[2 * n], refs[2 * n + 1]
        for cp in _direct_copies(ins, lands, send, recv, slab_source):
            cp.wait_send()
            cp.wait_recv()

    res = pl.pallas_call(
        body, name=name,
        out_shape=tuple(pltpu.HBM(s.shape, s.dtype) for s in list(ex["srcs"]) + list(ex["lands"])),
        in_specs=[_HBM] * (2 * n) + [_SEM, _SEM, pl.BlockSpec(memory_space=pl.ANY)],
        out_specs=tuple([_HBM] * (2 * n)),
        input_output_aliases={i: i for i in range(2 * n)},
        compiler_params=pltpu.CompilerParams(has_side_effects=_EFFECT),
    )(*ex["srcs"], *ex["lands"], ex["send"], ex["recv"], after)
    me = _my_block()
    out = []
    for src, land in zip(res[:n], res[n:]):
        own = lax.dynamic_index_in_dim(src, me, 0, keepdims=True) if slab_source else src[None]
        out.append(lax.dynamic_update_slice(land, own, (me,) + (0,) * (land.ndim - 1)))
    return out


BIG = ("w_in", "w_out", "w_gate", "w_up", "w_down")
BIG_SHARD = {"w_in": (1024, 578), "w_out": (256, 1024), "w_gate": (1024, 352), "w_up": (1024, 352),
             "w_down": (352, 1024)}
BIG_SHARD_AXIS = {"w_in": 1, "w_out": 0, "w_gate": 1, "w_up": 1, "w_down": 0}
BIG_ADAM_ROWS = {"w_in": 256, "w_out": 128, "w_gate": 256, "w_up": 256, "w_down": 176}


def _join(parts, axis):
    if axis == 0:
        return parts.reshape((-1,) + parts.shape[2:])
    return jnp.concatenate([parts[j] for j in range(N_DEV)], axis=1)


def _split(full, axis):
    if axis == 0:
        return full.reshape((N_DEV, full.shape[0] // N_DEV) + full.shape[1:])
    c = full.shape[1] // N_DEV
    return jnp.stack([full[:, c * j:c * (j + 1)] for j in range(N_DEV)])


SMALL = (("pre_mix_norm", 1024), ("lru_conv_w", 4096), ("lru_conv_b", 1024), ("lru_wa", 65536), ("lru_ba", 1024),
         ("lru_wx", 65536), ("lru_bx", 1024), ("lru_lambda", 1024), ("lru_out_norm", 1024), ("ssd_conv_w", 6144),
         ("ssd_conv_b", 1536), ("ssd_dt_bias", 16), ("ssd_a_log", 16), ("ssd_d", 16), ("ssd_out_norm", 1024),
         ("post_mix_norm", 1024), ("pre_ffn_norm", 1024), ("post_ffn_norm", 1024), ("loss", 1))
REPLICATED = tuple(n for n, _ in SMALL if n not in ("lru_conv_w", "ssd_conv_w", "loss"))


def _pack_small(d):
    flat = jnp.concatenate([d[n].astype(F32).reshape(-1) for n, _ in SMALL])
    return jnp.pad(flat, (0, SMALL_ROWS * 1024 - flat.shape[0])).reshape(SMALL_ROWS, 1024)


def _unpack_small(p):
    flat = p.reshape(-1)
    out, off = {}, 0
    for n, size in SMALL:
        out[n] = flat[off:off + size]
        off += size
    return out


def _blockdiag4(w):
    z = jnp.zeros((4, 4, 64, 4, 64), w.dtype)
    w4 = w.reshape(4, 4, 64, 64)
    for b in range(4):
        z = z.at[:, b, :, b, :].set(w4[:, b])
    return z.reshape(4, 256, 256)


def _diag_blocks(g):
    g5 = g.reshape(4, 4, 64, 4, 64)
    return jnp.stack([g5[:, b, :, b, :] for b in range(4)], axis=1).reshape(16, 64, 64)


def _local_step(x, target, w_in, P, rest_weights, emit, emit_small, start_token=None):
    w_main = w_in[:, :IN_MAIN]
    w_dt = jnp.pad(w_in[:, IN_MAIN:], ((0, 0), (0, LANE - SSD_HEADS)))
    pad16 = lambda v: jnp.pad(v.reshape(1, SSD_HEADS), ((0, 0), (0, LANE - SSD_HEADS)))
    dt_bias, a_log = pad16(P["ssd_dt_bias"]), pad16(P["ssd_a_log"])
    d_e = jnp.repeat(P["ssd_d"].reshape(SSD_HEADS), SSD_HEAD_DIM).reshape(1, SSD_INNER)
    expand = (jnp.arange(LANE)[:, None] == (jnp.arange(SSD_INNER)[None, :] // SSD_HEAD_DIM)).astype(F32)
    wa_bd = _blockdiag4(P["lru_wa"].astype(BF16))
    wx_bd = _blockdiag4(P["lru_wx"].astype(BF16))
    vec = lambda n: P[n].reshape(1, -1)

    after = lambda v, tok: v if tok is None else v + tok[0:1, 0:1]

    h, proj, dtp = _inproj_fwd(x, after(vec("pre_mix_norm"), start_token), w_main, w_dt)
    lx, hl, y_lru = _lru_fwd(proj, P["lru_conv_w"], vec("lru_conv_b"), wa_bd, wx_bd, vec("lru_ba"), vec("lru_bx"),
                             vec("lru_lambda"), vec("lru_out_norm"))
    xc, y, y_ssd, sprev = _ssd_fwd(proj, dtp, P["ssd_conv_w"], vec("ssd_conv_b"), dt_bias, a_log, d_e,
                                   vec("ssd_out_norm"), expand)
    W = rest_weights(y_ssd)
    mix, x1, h2 = _outproj_fwd(x, y_lru, y_ssd, W["w_out"], vec("post_mix_norm"), vec("pre_ffn_norm"))
    dx1, act, df, dgt, dup, dg_pf, dg_ff, loss = _ffn_fwd_bwd(
        x1, h2, target, W["w_gate"], W["w_up"], W["w_down"], vec("pre_ffn_norm"), vec("post_ffn_norm"))
    tok = emit("ffn", {"w_gate": _tn_matmul(h2, [dgt], "dw_gate")[0], "w_up": _tn_matmul(h2, [dup], "dw_up")[0],
                       "w_down": _tn_matmul(act, [df], "dw_down")[0]})
    dy_lru, dy_ssd, dmix, dg_pm = _outproj_bwd(dx1, mix, W["w_out"], after(vec("post_mix_norm"), tok))
    tok = emit("out", {"w_out": jnp.concatenate([_tn_matmul(y_lru, [dmix], "dw_out_lru")[0],
                                                 _tn_matmul(y_ssd, [dmix], "dw_out_ssd")[0]], axis=0)})
    dp_ssd, ddtp, dcw_s, dcb_s, dbias, dA, dD_e, dg_ssd = _ssd_bwd(
        dy_ssd, proj, dtp, xc, y, sprev, P["ssd_conv_w"], dt_bias, a_log, d_e,
        after(vec("ssd_out_norm"), tok), expand)
    dp_lru, dpa, dpx, dcw_l, dcb_l, dba, dbx, dlam, dg_lru = _lru_bwd(
        dy_lru, proj, lx, hl, P["lru_conv_w"], wa_bd, wx_bd, vec("lru_ba"), vec("lru_bx"), vec("lru_lambda"),
        vec("lru_out_norm"))
    dw_in = _tn_matmul(h, [dp_lru, dp_ssd, ddtp], "dw_in")
    tok = emit("in", {"w_in": jnp.concatenate([dw_in[0], dw_in[1], dw_in[2][:, :SSD_HEADS]], axis=1)})

    a_neg = -jnp.exp(P["ssd_a_log"].reshape(SSD_HEADS))
    dwa, dwx = _tn_blockdiag(lx, dpa, dpx, "dw_lru_gates")
    small = {
        "pre_mix_norm": jnp.zeros((1, D_MODEL), F32), "lru_conv_w": dcw_l, "lru_conv_b": dcb_l,
        "lru_wa": _diag_blocks(dwa), "lru_ba": dba,
        "lru_wx": _diag_blocks(dwx), "lru_bx": dbx,
        "lru_lambda": dlam, "lru_out_norm": dg_lru, "ssd_conv_w": dcw_s, "ssd_conv_b": dcb_s,
        "ssd_dt_bias": dbias[0, :SSD_HEADS], "ssd_a_log": dA[0, :SSD_HEADS] * a_neg,
        "ssd_d": jnp.sum(dD_e.reshape(SSD_HEADS, SSD_HEAD_DIM), axis=1), "ssd_out_norm": dg_ssd,
        "post_mix_norm": dg_pm, "pre_ffn_norm": dg_pf, "post_ffn_norm": dg_ff, "loss": loss[0, 0:1],
    }
    tok = after(after(vec("pre_mix_norm"), tok), emit_small(small))
    grad_x, dg_pre = _inproj_bwd(dp_lru, dp_ssd, ddtp, dx1, x, tok, w_main, w_dt)
    return grad_x, dg_pre


def kernel(x, pre_mix_norm, w_in, lru_conv_w, lru_conv_b, lru_wa, lru_ba, lru_wx, lru_bx, lru_lambda, lru_out_norm, ssd_conv_w, ssd_conv_b, ssd_dt_bias, ssd_a_log, ssd_d, ssd_out_norm, w_out, post_mix_norm, pre_ffn_norm, w_gate, w_up, w_down, post_ffn_norm, loss_target, m_pre_mix_norm, m_w_in, m_lru_conv_w, m_lru_conv_b, m_lru_wa, m_lru_ba, m_lru_wx, m_lru_bx, m_lru_lambda, m_lru_out_norm, m_ssd_conv_w, m_ssd_conv_b, m_ssd_dt_bias, m_ssd_a_log, m_ssd_d, m_ssd_out_norm, m_w_out, m_post_mix_norm, m_pre_ffn_norm, m_w_gate, m_w_up, m_w_down, m_post_ffn_norm, v_pre_mix_norm, v_w_in, v_lru_conv_w, v_lru_conv_b, v_lru_wa, v_lru_ba, v_lru_wx, v_lru_bx, v_lru_lambda, v_lru_out_norm, v_ssd_conv_w, v_ssd_conv_b, v_ssd_dt_bias, v_ssd_a_log, v_ssd_d, v_ssd_out_norm, v_w_out, v_post_mix_norm, v_pre_ffn_norm, v_w_gate, v_w_up, v_w_down, v_post_ffn_norm):
    a = dict(locals())
    names = [n for n, _ in SMALL if n != "loss"] + list(BIG)
    w = {n: a[n][0] for n in names}
    m = {n: a["m_" + n][0] for n in names}
    v = {n: a["v_" + n][0] for n in names}

    cpack = jnp.concatenate([w["lru_conv_w"], w["ssd_conv_w"], jnp.zeros((4, 64), F32)], axis=1)
    cpack = jnp.pad(cpack, ((0, 4), (0, 0)))
    g_in, cg = _all_gather([w["w_in"].astype(BF16), cpack], "all_gather_w_in")
    P = {n: w[n] for n in REPLICATED}
    P["lru_conv_w"] = _join(cg[:, 0:4, 0:128], 1)
    P["ssd_conv_w"] = _join(cg[:, 0:4, 128:320], 1)

    rest = [n for n in BIG if n != "w_in"]
    zero = jnp.minimum(jnp.abs(cg[0, 0, 0]), 0.0)
    ex_w = _exchange_start([(w[n] + zero).astype(BF16) for n in rest], "weights_start", slab_source=False)

    def rest_weights(after):
        lands = _exchange_wait(ex_w, after, "weights_wait")
        return {n: _join(p, BIG_SHARD_AXIS[n]) for n, p in zip(rest, lands)}

    pending = []

    def emit(group, grads):
        ex = _exchange_start([_split(g, BIG_SHARD_AXIS[n]) for n, g in grads.items()], "grads_start_" + group,
                             slab_source=True)
        pending.append((group, list(grads), ex))
        return ex["token"]

    def emit_small(small):
        ex = _exchange_start([_pack_small(small)], "small_start", slab_source=False)
        pending.append(("small", None, ex))
        return ex["token"]

    grad_x, dg_pre = _local_step(x[0], loss_target[0], _join(g_in, 1), P, rest_weights, emit, emit_small,
                                 ex_w["token"])

    got_pre = _all_gather([jnp.pad(dg_pre, ((0, 7), (0, 0)))], "all_gather_pre_mix_norm")[0]
    ex_small = pending.pop()[2]
    got_small = _exchange_wait(ex_small, got_pre, "small_wait")[0]
    got_small = got_small.at[:, 0:1, :].set(got_pre[:, 0:1, :])

    outs = {}
    done = got_small
    for group, group_names, ex in pending:
        for n, parts in zip(group_names, _exchange_wait(ex, done, "grads_wait_" + group)):
            res = _adamw(parts, w[n], m[n], v[n], "adamw_" + n, BIG_ADAM_ROWS[n])
            done = res[0]
            for kind, r in zip(("grad", "delta", "new_m", "new_v"), res):
                outs[kind + "_" + n] = r

    zero_extra = {"loss": jnp.zeros((1,), F32)}
    full_conv = {"lru_conv_w": jnp.zeros((4, LRU_W), F32), "ssd_conv_w": jnp.zeros((4, SSD_CONV_CH), F32)}
    rep = lambda d: _pack_small({**{n: d[n] for n in REPLICATED}, **full_conv, **zero_extra})
    res = _adamw(got_small, rep(w), rep(m), rep(v), "adamw_small", SMALL_ROWS)
    g_small = _unpack_small(res[0])
    for kind, r in zip(("grad", "delta", "new_m", "new_v"), res):
        for n, val in _unpack_small(r).items():
            if n in REPLICATED:
                outs[kind + "_" + n] = val.reshape(w[n].shape)

    me = _my_block()
    gl = lax.dynamic_slice(g_small["lru_conv_w"].reshape(4, LRU_W), (0, me * 128), (4, 128))
    gs = lax.dynamic_slice(g_small["ssd_conv_w"].reshape(4, SSD_CONV_CH), (0, me * 192), (4, 192))
    cat = lambda d: jnp.pad(jnp.concatenate([d["lru_conv_w"], d["ssd_conv_w"]], axis=1), ((0, 4), (0, 64)))
    res = _adamw(cat({"lru_conv_w": gl, "ssd_conv_w": gs})[None], cat(w), cat(m), cat(v), "adamw_conv", 8)
    for kind, r in zip(("grad", "delta", "new_m", "new_v"), res):
        outs[kind + "_lru_conv_w"] = r[0:4, 0:128]
        outs[kind + "_ssd_conv_w"] = r[0:4, 128:320]

    order = ["pre_mix_norm", "w_in", "lru_conv_w", "lru_conv_b", "lru_wa", "lru_ba", "lru_wx", "lru_bx", "lru_lambda",
             "lru_out_norm", "ssd_conv_w", "ssd_conv_b", "ssd_dt_bias", "ssd_a_log", "ssd_d", "ssd_out_norm", "w_out",
             "post_mix_norm", "pre_ffn_norm", "w_gate", "w_up", "w_down", "post_ffn_norm"]
    result = [g_small["loss"].reshape(()), grad_x[None]]
    for kind in ("grad", "delta", "new_m", "new_v"):
        result += [outs[kind + "_" + n][None] for n in order]
    return tuple(result)
```

```python
import functools

import jax
import jax.numpy as jnp
from jax import lax
from jax.experimental import pallas as pl
from jax.experimental.pallas import tpu as pltpu

F32 = jnp.float32
BF16 = jnp.bfloat16
HI = lax.Precision.HIGHEST
EPS = 1e-6
N_DEV = 8
D_MODEL = 1024
LRU_W = 1024
SSD_INNER = 1024
SSD_HEADS = 16
SSD_HEAD_DIM = 64
SSD_STATE = 128
SSD_GROUPS = 2
SSD_CONV_CH = 1536
CHUNK = 128
D_FF = 2816
FF_HALF = D_FF // 2
IN_MAIN = 4608
IN_COLS = 4624
LANE = 128
TT = 256
VMEM_LIMIT = 56 * 1024 * 1024
ADAM_LR, ADAM_B1, ADAM_B2, ADAM_EPS, ADAM_WD, ADAM_STEP = 0.001, 0.9, 0.999, 1e-08, 0.01, 10
PACK_ROWS = 1920
SMALL_ROWS = 160

NT = (((1,), (1,)), ((), ()))
TN = (((0,), (0,)), ((), ()))


def _params(n_grid):
    return pltpu.CompilerParams(dimension_semantics=("arbitrary",) * n_grid, vmem_limit_bytes=VMEM_LIMIT)


def _dot(a, b, dims=None, precision=None):
    if dims is None:
        return jnp.dot(a, b, preferred_element_type=F32, precision=precision)
    return lax.dot_general(a, b, dims, preferred_element_type=F32, precision=precision)


def _split_bf16(x, terms):
    out = []
    for _ in range(terms - 1):
        p = x.astype(BF16)
        out.append(p)
        x = x - p.astype(F32)
    return out + [x.astype(BF16)]


def _dot_sel(x, sel, dims=None, terms=2, sel_first=False):
    parts = [_dot(sel, p, dims) if sel_first else _dot(p, sel, dims) for p in _split_bf16(x, terms)]
    return functools.reduce(lambda a, b: a + b, parts)


def _sigmoid(x):
    return 0.5 * jnp.tanh(0.5 * x) + 0.5


def _softplus(x):
    e = jnp.exp(-jnp.abs(x))
    l1p = jnp.where(e < 1e-3, e * (1.0 - e * (0.5 - e * (1.0 / 3.0))), jnp.log(1.0 + e))
    return jnp.maximum(x, 0.0) + l1p


def _neg_expm1(x):
    series = -x * (1.0 + x * (0.5 + x * (1.0 / 6.0 + x * (1.0 / 24.0))))
    return jnp.where(x > -0.01, series, 1.0 - jnp.exp(x))


_GELU_C = 0.7978845608028654


def _gelu(x):
    t = jnp.tanh(_GELU_C * (x + 0.044715 * x * x * x))
    return 0.5 * x * (1.0 + t), t


def _gelu_grad(x, t):
    return 0.5 * (1.0 + t) + 0.5 * x * (1.0 - t * t) * _GELU_C * (1.0 + 3.0 * 0.044715 * x * x)


def _rms(x):
    return lax.rsqrt(jnp.mean(x * x, axis=-1, keepdims=True) + EPS)


def _rms_bwd(dyn, x, rn):
    return rn * dyn - x * (rn * rn * rn) * jnp.mean(dyn * x, axis=-1, keepdims=True)


def _row(x, r):
    idx = lax.broadcasted_iota(jnp.int32, x.shape, 0)
    return jnp.sum(jnp.where(idx == r, x, 0.0), axis=0, keepdims=True)


def _shift_down(cur, prev8, j):
    s = pltpu.roll(cur, j, 0)
    p = pltpu.roll(prev8, j, 0)
    r8 = lax.broadcasted_iota(jnp.int32, prev8.shape, 0)
    top = jnp.where(r8 < j, p, s[0:8])
    return jnp.concatenate([top, s[8:]], axis=0)


def _shift_up(cur, next8, j):
    n = cur.shape[0]
    s = pltpu.roll(cur, n - j, 0)
    p = pltpu.roll(next8, 8 - j, 0)
    r8 = lax.broadcasted_iota(jnp.int32, next8.shape, 0)
    bot = jnp.where(r8 >= 8 - j, p, s[n - 8:n])
    return jnp.concatenate([s[:n - 8], bot], axis=0)


def _scan_fwd(a, u):
    n = a.shape[0]
    row = lax.broadcasted_iota(jnp.int32, a.shape, 0)
    k = 1
    while k < n:
        ok = row >= k
        a_s = jnp.where(ok, pltpu.roll(a, k, 0), 1.0)
        u_s = jnp.where(ok, pltpu.roll(u, k, 0), 0.0)
        u = a * u_s + u
        a = a * a_s
        k *= 2
    return a, u


def _scan_bwd(b, d):
    n = b.shape[0]
    row = lax.broadcasted_iota(jnp.int32, b.shape, 0)
    k = 1
    while k < n:
        ok = row < n - k
        b_s = jnp.where(ok, pltpu.roll(b, n - k, 0), 1.0)
        d_s = jnp.where(ok, pltpu.roll(d, n - k, 0), 0.0)
        d = b * d_s + d
        b = b * b_s
        k *= 2
    return b, d


def _inproj_fwd(x, g_pre, w_main, w_dt):
    T = x.shape[0]

    def body(x_ref, g_ref, wm_hbm, wd_hbm, h_ref, proj_ref, dtp_ref, wm, wd, sem):
        @pl.when(pl.program_id(0) == 0)
        def _():
            c1 = pltpu.make_async_copy(wm_hbm, wm, sem.at[0])
            c2 = pltpu.make_async_copy(wd_hbm, wd, sem.at[1])
            c1.start()
            c2.start()
            c1.wait()
            c2.wait()

        xv = x_ref[...]
        h = (xv * _rms(xv) * g_ref[...]).astype(BF16)
        h_ref[...] = h
        proj_ref[...] = _dot(h, wm[...])
        dtp_ref[...] = _dot(h, wd[...])

    return pl.pallas_call(
        body, name="inproj_fwd", grid=(T // TT,),
        in_specs=[pl.BlockSpec((TT, D_MODEL), lambda i: (i, 0)),
                  pl.BlockSpec((1, D_MODEL), lambda i: (0, 0)),
                  pl.BlockSpec(memory_space=pl.ANY), pl.BlockSpec(memory_space=pl.ANY)],
        out_specs=[pl.BlockSpec((TT, D_MODEL), lambda i: (i, 0)),
                   pl.BlockSpec((TT, IN_MAIN), lambda i: (i, 0)),
                   pl.BlockSpec((TT, LANE), lambda i: (i, 0))],
        out_shape=[jax.ShapeDtypeStruct((T, D_MODEL), BF16), jax.ShapeDtypeStruct((T, IN_MAIN), F32),
                   jax.ShapeDtypeStruct((T, LANE), F32)],
        scratch_shapes=[pltpu.VMEM((D_MODEL, IN_MAIN), BF16), pltpu.VMEM((D_MODEL, LANE), BF16),
                        pltpu.SemaphoreType.DMA((2,))],
        compiler_params=_params(1),
    )(x, g_pre, w_main, w_dt)


def _inproj_bwd(dp_lru, dp_ssd, ddtp, dx1, x, g_pre, w_main, w_dt):
    T = x.shape[0]

    def body(dl_ref, ds_ref, dd_ref, dx1_ref, x_ref, g_ref, wm_hbm, wd_hbm, gx_ref, dg_ref, wm, wd, sem):
        @pl.when(pl.program_id(0) == 0)
        def _():
            c1 = pltpu.make_async_copy(wm_hbm, wm, sem.at[0])
            c2 = pltpu.make_async_copy(wd_hbm, wd, sem.at[1])
            c1.start()
            c2.start()
            c1.wait()
            c2.wait()
            dg_ref[...] = jnp.zeros_like(dg_ref)

        dh = _dot(dl_ref[...], wm[:, 0:2048], NT)
        dh += _dot(ds_ref[...], wm[:, 2048:IN_MAIN], NT)
        dh += _dot(dd_ref[...], wd[...], NT)
        xv = x_ref[...]
        rn = _rms(xv)
        dg_ref[...] += jnp.sum(dh * xv * rn, axis=0, keepdims=True)
        gx_ref[...] = dx1_ref[...] + _rms_bwd(dh * g_ref[...], xv, rn)

    return pl.pallas_call(
        body, name="inproj_bwd", grid=(T // TT,),
        in_specs=[pl.BlockSpec((TT, 2048), lambda i: (i, 0)),
                  pl.BlockSpec((TT, 2560), lambda i: (i, 0)),
                  pl.BlockSpec((TT, LANE), lambda i: (i, 0)),
                  pl.BlockSpec((TT, D_MODEL), lambda i: (i, 0)),
                  pl.BlockSpec((TT, D_MODEL), lambda i: (i, 0)),
                  pl.BlockSpec((1, D_MODEL), lambda i: (0, 0)),
                  pl.BlockSpec(memory_space=pl.ANY), pl.BlockSpec(memory_space=pl.ANY)],
        out_specs=[pl.BlockSpec((TT, D_MODEL), lambda i: (i, 0)),
                   pl.BlockSpec((1, D_MODEL), lambda i: (0, 0))],
        out_shape=[jax.ShapeDtypeStruct((T, D_MODEL), F32), jax.ShapeDtypeStruct((1, D_MODEL), F32)],
        scratch_shapes=[pltpu.VMEM((D_MODEL, IN_MAIN), BF16), pltpu.VMEM((D_MODEL, LANE), BF16),
                        pltpu.SemaphoreType.DMA((2,))],
        compiler_params=_params(1),
    )(dp_lru, dp_ssd, ddtp, dx1, x, g_pre, w_main, w_dt)


def _lru_gates(lx, wa_ref, wx_ref, ba, bx, lam):
    lxb = lx.astype(BF16)
    pa = jnp.concatenate([_dot(lxb[:, 256 * k:256 * (k + 1)], wa_ref[k]) for k in range(4)], axis=1) + ba
    px = jnp.concatenate([_dot(lxb[:, 256 * k:256 * (k + 1)], wx_ref[k]) for k in range(4)], axis=1) + bx
    r = _sigmoid(pa)
    ig = _sigmoid(px)
    sp = _softplus(-lam)
    log_a = -8.0 * r * sp
    a = jnp.exp(log_a)
    mult = jnp.sqrt(_neg_expm1(2.0 * log_a))
    return r, ig, sp, a, mult


def _lru_fwd(proj, conv_w, conv_b, wa_bd, wx_bd, ba, bx, lam, g_lru):
    T = proj.shape[0]

    def body(cx_ref, gate_ref, cw_ref, cb_ref, wa_ref, wx_ref, ba_ref, bx_ref, lam_ref, g_ref,
             lx_ref, hl_ref, y_ref, r_ref, ig_ref, a_ref, mult_ref, tail, hcar):
        @pl.when(pl.program_id(0) == 0)
        def _():
            tail[...] = jnp.zeros_like(tail)
            hcar[...] = jnp.zeros_like(hcar)

        cx = cx_ref[...]
        prev8 = tail[...]
        lx = cb_ref[...] + cw_ref[3:4, :] * cx
        for j in range(1, 4):
            lx += cw_ref[3 - j:4 - j, :] * _shift_down(cx, prev8, j)
        tail[...] = cx[TT - 8:TT]
        lx_ref[...] = lx
        r, ig, sp, a, mult = _lru_gates(lx, wa_ref, wx_ref, ba_ref[...], bx_ref[...], lam_ref[...])
        r_ref[...] = r
        ig_ref[...] = ig
        a_ref[...] = a
        mult_ref[...] = mult
        acum, h0 = _scan_fwd(a, mult * (ig * lx))
        h = h0 + acum * hcar[...]
        hl_ref[...] = h
        hcar[...] = hl_ref[TT - 1:TT, :]
        ge, _ = _gelu(gate_ref[...])
        p = h * ge
        y_ref[...] = (p * _rms(p) * g_ref[...]).astype(BF16)

    vec = pl.BlockSpec((1, LRU_W), lambda i: (0, 0))
    bd = pl.BlockSpec((4, 256, 256), lambda i: (0, 0, 0))
    tile = pl.BlockSpec((TT, LRU_W), lambda i: (i, 0))
    f32 = jax.ShapeDtypeStruct((T, LRU_W), F32)
    return pl.pallas_call(
        body, name="lru_fwd", grid=(T // TT,),
        in_specs=[tile, pl.BlockSpec((TT, LRU_W), lambda i: (i, 1)),
                  pl.BlockSpec((4, LRU_W), lambda i: (0, 0)), vec, bd, bd, vec, vec, vec, vec],
        out_specs=[tile] * 7,
        out_shape=[f32, f32, jax.ShapeDtypeStruct((T, LRU_W), BF16), f32, f32, f32, f32],
        scratch_shapes=[pltpu.VMEM((8, LRU_W), F32), pltpu.VMEM((1, LRU_W), F32)],
        compiler_params=_params(1),
    )(proj, proj, conv_w, conv_b, wa_bd, wx_bd, ba, bx, lam, g_lru)


def _lru_bwd(dy, proj, lx, hl, gates, conv_w, wa_bd, wx_bd, lam, g_lru):
    T = proj.shape[0]
    nt = T // TT

    def body(dy_ref, cx_ref, gate_ref, lx_ref, hl_ref, halo_ref, r_ref, ig_ref, a_ref, mult_ref, cw_ref, wa_ref,
             wx_ref, lam_ref, g_ref, dp_ref, dpa_ref, dpx_ref, dcw_ref, dcb_ref, dba_ref, dbx_ref, dlam_ref, dg_ref,
             gcar, acar, head):
        i = pl.program_id(0)

        @pl.when(i == 0)
        def _():
            gcar[...] = jnp.zeros_like(gcar)
            acar[...] = jnp.zeros_like(acar)
            head[...] = jnp.zeros_like(head)
            for ref in (dcw_ref, dcb_ref, dba_ref, dbx_ref, dlam_ref, dg_ref):
                ref[...] = jnp.zeros_like(ref)

        lx = lx_ref[...]
        h = hl_ref[...]
        gate = gate_ref[...]
        cx = cx_ref[...]
        lam = lam_ref[...]
        r, ig, a, mult = r_ref[...], ig_ref[...], a_ref[...], mult_ref[...]
        sp = _softplus(-lam)
        ge, th = _gelu(gate)
        p = h * ge
        rn = _rms(p)
        dyv = dy_ref[...]
        dg_ref[...] += jnp.sum(dyv * p * rn, axis=0, keepdims=True)
        dp = _rms_bwd(dyv * g_ref[...], p, rn)
        dp_ref[:, LRU_W:2 * LRU_W] = (dp * h * _gelu_grad(gate, th)).astype(BF16)
        dh = dp * ge
        row = lax.broadcasted_iota(jnp.int32, a.shape, 0)
        b = jnp.where(row == TT - 1, acar[...], pltpu.roll(a, TT - 1, 0))
        bcum, g0 = _scan_bwd(b, dh)
        g = g0 + bcum * gcar[...]
        gcar[...] = _row(g[0:8], 0)
        acar[...] = _row(a[0:8], 0)
        h_last_prev = halo_ref[7:8, :] * (i < nt - 1).astype(F32)
        hprev = jnp.where(row == 0, h_last_prev, pltpu.roll(h, 1, 0))
        da = g * hprev
        dm2 = (g * (ig * lx)) * 0.5 / mult
        dlog_a = da * a - 2.0 * a * a * dm2
        dlam_ref[...] += jnp.sum(-8.0 * r * dlog_a, axis=0, keepdims=True) * (-_sigmoid(-lam))
        dpa = (-8.0 * sp * dlog_a) * r * (1.0 - r)
        dpx = (g * mult * lx) * ig * (1.0 - ig)
        dba_ref[...] += jnp.sum(dpa, axis=0, keepdims=True)
        dbx_ref[...] += jnp.sum(dpx, axis=0, keepdims=True)
        dpab = dpa.astype(BF16)
        dpxb = dpx.astype(BF16)
        dpa_ref[...] = dpab
        dpx_ref[...] = dpxb
        dlx = g * mult * ig + jnp.concatenate(
            [_dot(dpab[:, 256 * k:256 * (k + 1)], wa_ref[k], NT) + _dot(dpxb[:, 256 * k:256 * (k + 1)], wx_ref[k], NT)
             for k in range(4)], axis=1)
        nxt = head[...]
        dcb_ref[...] += jnp.sum(dlx, axis=0, keepdims=True)
        dcx = cw_ref[3:4, :] * dlx
        dcw_ref[3:4, :] += jnp.sum(cx * dlx, axis=0, keepdims=True)
        for j in range(1, 4):
            sh = _shift_up(dlx, nxt, j)
            dcx += cw_ref[3 - j:4 - j, :] * sh
            dcw_ref[3 - j:4 - j, :] += jnp.sum(cx * sh, axis=0, keepdims=True)
        head[...] = dlx[0:8]
        dp_ref[:, 0:LRU_W] = dcx.astype(BF16)

    rev = lambda i: (nt - 1 - i, 0)
    vec = pl.BlockSpec((1, LRU_W), lambda i: (0, 0))
    bd = pl.BlockSpec((4, 256, 256), lambda i: (0, 0, 0))
    tile = pl.BlockSpec((TT, LRU_W), rev)
    halo = pl.BlockSpec((8, LRU_W), lambda i: (jnp.maximum((nt - 1 - i) * (TT // 8) - 1, 0), 0))
    cw = pl.BlockSpec((4, LRU_W), lambda i: (0, 0))
    return pl.pallas_call(
        body, name="lru_bwd", grid=(nt,),
        in_specs=[tile, tile, pl.BlockSpec((TT, LRU_W), lambda i: (nt - 1 - i, 1)), tile, tile, halo,
                  tile, tile, tile, tile, cw, bd, bd, vec, vec],
        out_specs=[pl.BlockSpec((TT, 2 * LRU_W), rev), tile, tile, cw, vec, vec, vec, vec, vec],
        out_shape=[jax.ShapeDtypeStruct((T, 2 * LRU_W), BF16), jax.ShapeDtypeStruct((T, LRU_W), BF16),
                   jax.ShapeDtypeStruct((T, LRU_W), BF16), jax.ShapeDtypeStruct((4, LRU_W), F32)]
                  + [jax.ShapeDtypeStruct((1, LRU_W), F32)] * 5,
        scratch_shapes=[pltpu.VMEM((1, LRU_W), F32), pltpu.VMEM((1, LRU_W), F32), pltpu.VMEM((8, LRU_W), F32)],
        compiler_params=_params(1),
    )(dy, proj, proj, lx, hl, hl, *gates, conv_w, wa_bd, wx_bd, lam, g_lru)


def _ssd_chunk_terms(xc, dtp, bias, alog, expand):
    sg = _sigmoid(xc)
    xbc = xc * sg
    pre = dtp + bias
    dt = _softplus(pre)
    A = -jnp.exp(alog)
    ri = lax.broadcasted_iota(jnp.int32, (CHUNK, CHUNK), 0)
    ci = lax.broadcasted_iota(jnp.int32, (CHUNK, CHUNK), 1)
    tril = (ri >= ci).astype(BF16)
    cs = _dot_sel(dt * A, tril, terms=3, sel_first=True)
    cs_last = _row(cs, CHUNK - 1)
    ecs = jnp.exp(cs)
    dec = jnp.exp(cs_last - cs)
    return dict(sg=sg, xbc=xbc, pre=pre, dt=dt, A=A, cs=cs, csT=cs.T, ecs=ecs, dec=dec, ri=ri, ci=ci,
                dt_e=_dot_sel(dt, expand), ecs_e=_dot_sel(ecs, expand), dec_e=_dot_sel(dec, expand))


def _head_lambda(t, h):
    col = jnp.sum(jnp.where(t["ci"] == h, t["cs"], 0.0), axis=1, keepdims=True)
    rowv = jnp.sum(jnp.where(t["ri"] == h, t["csT"], 0.0), axis=0, keepdims=True)
    return jnp.exp(jnp.where(t["ri"] >= t["ci"], col - rowv, -1e30))


def _ssd_fwd(proj, dtp, conv_w, conv_b, dt_bias, a_log, d_e, g_ssd, expand):
    T = proj.shape[0]
    nc = T // CHUNK

    def body(z_ref, xp_ref, dtp_ref, cw_ref, cb_ref, bias_ref, alog_ref, de_ref, g_ref, ex_ref,
             xc_ref, y_ref, yn_ref, sprev_ref, tail, S):
        @pl.when(pl.program_id(0) == 0)
        def _():
            tail[...] = jnp.zeros_like(tail)
            S[...] = jnp.zeros_like(S)

        xp = xp_ref[...]
        prev8 = tail[...]
        xc = cb_ref[...] + cw_ref[3:4, :] * xp
        for j in range(1, 4):
            xc += cw_ref[3 - j:4 - j, :] * _shift_down(xp, prev8, j)
        tail[...] = xp[CHUNK - 8:CHUNK]
        xc_ref[...] = xc
        t = _ssd_chunk_terms(xc, dtp_ref[...], bias_ref[...], alog_ref[...], ex_ref[...])
        xbc = t["xbc"]
        sx = xbc[:, 0:SSD_INNER]
        Bb = xbc[:, SSD_INNER:SSD_INNER + 256].astype(BF16)
        Cb = xbc[:, SSD_INNER + 256:SSD_CONV_CH].astype(BF16)
        X = t["dt_e"] * sx
        lane = lax.broadcasted_iota(jnp.int32, (CHUNK, LANE), 1)
        G = [_dot(Cb[:, 128 * g:128 * (g + 1)], Bb[:, 128 * g:128 * (g + 1)], NT) for g in range(SSD_GROUPS)]
        for k in range(SSD_HEADS // 2):
            Xp = X[:, 128 * k:128 * (k + 1)]
            acc = jnp.zeros((CHUNK, LANE), F32)
            for half in range(2):
                M = (G[k // 4] * _head_lambda(t, 2 * k + half)).astype(BF16)
                Xh = jnp.where((lane >= 64) if half else (lane < 64), Xp, 0.0).astype(BF16)
                acc += _dot(M, Xh)
            y_ref[:, 128 * k:128 * (k + 1)] = acc
        sprev_ref[0] = S[...]
        eL_e = _row(t["ecs_e"], CHUNK - 1)
        Xd = (X * t["dec_e"]).astype(BF16)
        for g in range(SSD_GROUPS):
            sl = slice(512 * g, 512 * (g + 1))
            Sg = S[:, sl]
            y_ref[:, sl] += t["ecs_e"][:, sl] * _dot(Cb[:, 128 * g:128 * (g + 1)], Sg.astype(BF16))
            S[:, sl] = eL_e[:, sl] * Sg + _dot(Bb[:, 128 * g:128 * (g + 1)], Xd[:, sl], TN)
        y = y_ref[...] + de_ref[...] * sx
        y_ref[...] = y
        z = z_ref[...]
        q = y * (z * _sigmoid(z))
        yn_ref[...] = (q * _rms(q) * g_ref[...]).astype(BF16)

    c0 = lambda i: (0, 0)
    return pl.pallas_call(
        body, name="ssd_fwd", grid=(nc,),
        in_specs=[pl.BlockSpec((CHUNK, SSD_INNER), lambda i: (i, 2)),
                  pl.BlockSpec((CHUNK, SSD_CONV_CH), lambda i: (i, 2)),
                  pl.BlockSpec((CHUNK, LANE), lambda i: (i, 0)),
                  pl.BlockSpec((4, SSD_CONV_CH), c0), pl.BlockSpec((1, SSD_CONV_CH), c0),
                  pl.BlockSpec((1, LANE), c0), pl.BlockSpec((1, LANE), c0),
                  pl.BlockSpec((1, SSD_INNER), c0), pl.BlockSpec((1, SSD_INNER), c0),
                  pl.BlockSpec((LANE, SSD_INNER), c0)],
        out_specs=[pl.BlockSpec((CHUNK, SSD_CONV_CH), lambda i: (i, 0)),
                   pl.BlockSpec((CHUNK, SSD_INNER), lambda i: (i, 0)),
                   pl.BlockSpec((CHUNK, SSD_INNER), lambda i: (i, 0)),
                   pl.BlockSpec((1, SSD_STATE, SSD_INNER), lambda i: (i, 0, 0))],
        out_shape=[jax.ShapeDtypeStruct((T, SSD_CONV_CH), F32), jax.ShapeDtypeStruct((T, SSD_INNER), F32),
                   jax.ShapeDtypeStruct((T, SSD_INNER), BF16),
                   jax.ShapeDtypeStruct((nc, SSD_STATE, SSD_INNER), F32)],
        scratch_shapes=[pltpu.VMEM((8, SSD_CONV_CH), F32), pltpu.VMEM((SSD_STATE, SSD_INNER), F32)],
        compiler_params=_params(1),
    )(proj, proj, dtp, conv_w, conv_b, dt_bias, a_log, d_e, g_ssd, expand)


def _ssd_bwd(dyn, proj, dtp, xc, y, sprev, conv_w, dt_bias, a_log, d_e, g_ssd, expand):
    T = proj.shape[0]
    nc = T // CHUNK

    def body(dyn_ref, z_ref, xp_ref, dtp_ref, xc_ref, y_ref, sprev_ref, cw_ref, bias_ref, alog_ref, de_ref,
             g_ref, ex_ref, dp_ref, ddtp_ref, dcw_ref, dcb_ref, dbias_ref, dA_ref, dD_ref, dg_ref,
             dS, head, dX_s, dxbc_s):
        @pl.when(pl.program_id(0) == 0)
        def _():
            dS[...] = jnp.zeros_like(dS)
            head[...] = jnp.zeros_like(head)
            for ref in (dcw_ref, dcb_ref, dbias_ref, dA_ref, dD_ref, dg_ref):
                ref[...] = jnp.zeros_like(ref)

        ex = ex_ref[...]
        xc = xc_ref[...]
        t = _ssd_chunk_terms(xc, dtp_ref[...], bias_ref[...], alog_ref[...], ex)
        ri, ci = t["ri"], t["ci"]
        xbc = t["xbc"]
        sx = xbc[:, 0:SSD_INNER]
        Bb = xbc[:, SSD_INNER:SSD_INNER + 256].astype(BF16)
        Cb = xbc[:, SSD_INNER + 256:SSD_CONV_CH].astype(BF16)
        X = t["dt_e"] * sx
        z = z_ref[...]
        sz = _sigmoid(z)
        siluz = z * sz
        yv = y_ref[...]
        q = yv * siluz
        rn = _rms(q)
        dynv = dyn_ref[...]
        dg_ref[...] += jnp.sum(dynv * q * rn, axis=0, keepdims=True)
        dq = _rms_bwd(dynv * g_ref[...], q, rn)
        dp_ref[:, 0:SSD_INNER] = (dq * yv * (sz * (1.0 + z * (1.0 - sz)))).astype(BF16)
        dY = dq * siluz
        dD_ref[...] += jnp.sum(dY * sx, axis=0, keepdims=True)
        dYb = dY.astype(BF16)
        lane = lax.broadcasted_iota(jnp.int32, (CHUNK, LANE), 1)
        dcs = jnp.zeros((CHUNK, CHUNK), F32)
        dcsT = jnp.zeros((CHUNK, CHUNK), F32)
        Xb = X.astype(BF16)
        for g in range(SSD_GROUPS):
            Bg = Bb[:, 128 * g:128 * (g + 1)]
            Cg = Cb[:, 128 * g:128 * (g + 1)]
            G = _dot(Cg, Bg, NT)
            dGsum = jnp.zeros((CHUNK, CHUNK), F32)
            for k in range(4 * g, 4 * g + 4):
                Xp = Xb[:, 128 * k:128 * (k + 1)]
                dYp = dY[:, 128 * k:128 * (k + 1)]
                dXp = jnp.zeros((CHUNK, LANE), F32)
                for half in range(2):
                    h = 2 * k + half
                    lam = _head_lambda(t, h)
                    M = G * lam
                    dYh = jnp.where((lane >= 64) if half else (lane < 64), dYp, 0.0).astype(BF16)
                    dM = _dot(dYh, Xp, NT)
                    W = dM * M
                    dcs += jnp.where(ci == h, jnp.sum(W, axis=1, keepdims=True), 0.0)
                    dcsT += jnp.where(ri == h, jnp.sum(W, axis=0, keepdims=True), 0.0)
                    dGsum += dM * lam
                    dXp += _dot(M.astype(BF16), dYh, TN)
                dX_s[:, 128 * k:128 * (k + 1)] = dXp
            dGb = dGsum.astype(BF16)
            dxbc_s[:, SSD_INNER + 256 + 128 * g:SSD_INNER + 256 + 128 * (g + 1)] = _dot(dGb, Bg)
            dxbc_s[:, SSD_INNER + 128 * g:SSD_INNER + 128 * (g + 1)] = _dot(dGb, Cg, TN)
        dcs = dcs - dcsT.T
        Sp = sprev_ref[0]
        dSv = dS[...]
        ecs_e, dec_e = t["ecs_e"], t["dec_e"]
        eL_e = _row(ecs_e, CHUNK - 1)
        dYe = dY * ecs_e
        dYeb = dYe.astype(BF16)
        Xd = X * dec_e
        Xdb = Xd.astype(BF16)
        for g in range(SSD_GROUPS):
            sl = slice(512 * g, 512 * (g + 1))
            Bg = Bb[:, 128 * g:128 * (g + 1)]
            Cg = Cb[:, 128 * g:128 * (g + 1)]
            Spb = Sp[:, sl].astype(BF16)
            dSb = dSv[:, sl].astype(BF16)
            CS = _dot(Cg, Spb)
            BS = _dot(Bg, dSb)
            dxbc_s[:, SSD_INNER + 256 + 128 * g:SSD_INNER + 256 + 128 * (g + 1)] += _dot(dYeb[:, sl], Spb, NT)
            dxbc_s[:, SSD_INNER + 128 * g:SSD_INNER + 128 * (g + 1)] += _dot(Xdb[:, sl], dSb, NT)
            dS[:, sl] = eL_e[:, sl] * dSv[:, sl] + _dot(Cg, dYeb[:, sl], TN)
            dX_s[:, sl] += dec_e[:, sl] * BS
            dcs += _dot_sel(dYe[:, sl] * CS, ex[:, sl], NT)
            tdec = _dot_sel(X[:, sl] * BS, ex[:, sl], NT) * t["dec"]
            dcs -= tdec
            last = jnp.sum(tdec, axis=0, keepdims=True)
            last += jnp.sum(_dot_sel(Sp[:, sl] * dSv[:, sl], ex[:, sl], NT), axis=0, keepdims=True) \
                * _row(t["ecs"], CHUNK - 1)
            dcs += jnp.where(ri == CHUNK - 1, last, 0.0)
        triu = (ci >= ri).astype(BF16)
        da = _dot_sel(dcs, triu, terms=3, sel_first=True)
        dX = dX_s[...]
        ddt = da * t["A"] + _dot_sel(dX * sx, ex, NT)
        dA_ref[...] += jnp.sum(da * t["dt"], axis=0, keepdims=True)
        ddtp = ddt * _sigmoid(t["pre"])
        dbias_ref[...] += jnp.sum(ddtp, axis=0, keepdims=True)
        ddtp_ref[...] = ddtp.astype(BF16)
        dxbc_s[:, 0:SSD_INNER] = dX * t["dt_e"] + de_ref[...] * dY
        sg = t["sg"]
        dxc = dxbc_s[...] * (sg * (1.0 + xc * (1.0 - sg)))
        xp = xp_ref[...]
        nxt = head[...]
        dcb_ref[...] += jnp.sum(dxc, axis=0, keepdims=True)
        dpre = cw_ref[3:4, :] * dxc
        dcw_ref[3:4, :] += jnp.sum(xp * dxc, axis=0, keepdims=True)
        for j in range(1, 4):
            sh = _shift_up(dxc, nxt, j)
            dpre += cw_ref[3 - j:4 - j, :] * sh
            dcw_ref[3 - j:4 - j, :] += jnp.sum(xp * sh, axis=0, keepdims=True)
        head[...] = dxc[0:8]
        dp_ref[:, SSD_INNER:SSD_INNER + SSD_CONV_CH] = dpre.astype(BF16)

    c0 = lambda i: (0, 0)
    rev = lambda i: (nc - 1 - i, 0)
    return pl.pallas_call(
        body, name="ssd_bwd", grid=(nc,),
        in_specs=[pl.BlockSpec((CHUNK, SSD_INNER), rev),
                  pl.BlockSpec((CHUNK, SSD_INNER), lambda i: (nc - 1 - i, 2)),
                  pl.BlockSpec((CHUNK, SSD_CONV_CH), lambda i: (nc - 1 - i, 2)),
                  pl.BlockSpec((CHUNK, LANE), rev),
                  pl.BlockSpec((CHUNK, SSD_CONV_CH), rev),
                  pl.BlockSpec((CHUNK, SSD_INNER), rev),
                  pl.BlockSpec((1, SSD_STATE, SSD_INNER), lambda i: (nc - 1 - i, 0, 0)),
                  pl.BlockSpec((4, SSD_CONV_CH), c0), pl.BlockSpec((1, LANE), c0), pl.BlockSpec((1, LANE), c0),
                  pl.BlockSpec((1, SSD_INNER), c0), pl.BlockSpec((1, SSD_INNER), c0),
                  pl.BlockSpec((LANE, SSD_INNER), c0)],
        out_specs=[pl.BlockSpec((CHUNK, 2560), rev), pl.BlockSpec((CHUNK, LANE), rev),
                   pl.BlockSpec((4, SSD_CONV_CH), c0), pl.BlockSpec((1, SSD_CONV_CH), c0),
                   pl.BlockSpec((1, LANE), c0), pl.BlockSpec((1, LANE), c0),
                   pl.BlockSpec((1, SSD_INNER), c0), pl.BlockSpec((1, SSD_INNER), c0)],
        out_shape=[jax.ShapeDtypeStruct((T, 2560), BF16), jax.ShapeDtypeStruct((T, LANE), BF16),
                   jax.ShapeDtypeStruct((4, SSD_CONV_CH), F32), jax.ShapeDtypeStruct((1, SSD_CONV_CH), F32),
                   jax.ShapeDtypeStruct((1, LANE), F32), jax.ShapeDtypeStruct((1, LANE), F32),
                   jax.ShapeDtypeStruct((1, SSD_INNER), F32), jax.ShapeDtypeStruct((1, SSD_INNER), F32)],
        scratch_shapes=[pltpu.VMEM((SSD_STATE, SSD_INNER), F32), pltpu.VMEM((8, SSD_CONV_CH), F32),
                        pltpu.VMEM((CHUNK, SSD_INNER), F32), pltpu.VMEM((CHUNK, SSD_CONV_CH), F32)],
        compiler_params=_params(1),
    )(dyn, proj, proj, dtp, xc, y, sprev, conv_w, dt_bias, a_log, d_e, g_ssd, expand)


def _outproj_fwd(x, y_lru, y_ssd, w_out, g_pm, g_pf):
    T = x.shape[0]

    def body(x_ref, yl_ref, ys_ref, wo_ref, gpm_ref, gpf_ref, mix_ref, x1_ref, h2_ref):
        mix = _dot(yl_ref[...], wo_ref[0:LRU_W, :]) + _dot(ys_ref[...], wo_ref[LRU_W:2 * LRU_W, :])
        mix_ref[...] = mix
        x1 = x_ref[...] + mix * _rms(mix) * gpm_ref[...]
        x1_ref[...] = x1
        h2_ref[...] = (x1 * _rms(x1) * gpf_ref[...]).astype(BF16)

    tile = pl.BlockSpec((TT, D_MODEL), lambda i: (i, 0))
    vec = pl.BlockSpec((1, D_MODEL), lambda i: (0, 0))
    return pl.pallas_call(
        body, name="outproj_fwd", grid=(T // TT,),
        in_specs=[tile, tile, tile, pl.BlockSpec((2 * LRU_W, D_MODEL), lambda i: (0, 0)), vec, vec],
        out_specs=[tile, tile, tile],
        out_shape=[jax.ShapeDtypeStruct((T, D_MODEL), F32), jax.ShapeDtypeStruct((T, D_MODEL), F32),
                   jax.ShapeDtypeStruct((T, D_MODEL), BF16)],
        compiler_params=_params(1),
    )(x, y_lru, y_ssd, w_out, g_pm, g_pf)


def _ffn_fwd_bwd(x1, h2, target, w_gate, w_up, w_down, g_pf, g_ff):
    T = x1.shape[0]

    def body(x1_ref, h2_ref, tg_ref, wg_hbm, wu_hbm, wd_hbm, gpf_ref, gff_ref,
             dx1_ref, act_ref, df_ref, dgt_ref, dup_ref, dgpf_ref, dgff_ref, loss_ref,
             wg, wu, wd, gt_s, up_s, sem):
        @pl.when(pl.program_id(0) == 0)
        def _():
            cps = [pltpu.make_async_copy(s, d, sem.at[n]) for n, (s, d) in
                   enumerate(((wg_hbm, wg), (wu_hbm, wu), (wd_hbm, wd)))]
            for c in cps:
                c.start()
            for c in cps:
                c.wait()
            for ref in (dgpf_ref, dgff_ref, loss_ref):
                ref[...] = jnp.zeros_like(ref)

        h2 = h2_ref[...]
        f = jnp.zeros((TT, D_MODEL), F32)
        for c in range(2):
            sl = slice(FF_HALF * c, FF_HALF * (c + 1))
            gt = _dot(h2, wg[:, sl])
            up = _dot(h2, wu[:, sl])
            gt_s[:, sl] = gt
            up_s[:, sl] = up
            act = (gt * _sigmoid(gt) * up).astype(BF16)
            act_ref[:, sl] = act
            f += _dot(act, wd[sl, :])
        x1 = x1_ref[...]
        rnf = _rms(f)
        e = x1 + f * rnf * gff_ref[...] - tg_ref[...]
        part = 0.5 * jnp.sum(jnp.sum(e * e, axis=1, keepdims=True), axis=0, keepdims=True) * (1.0 / D_MODEL)
        lane = lax.broadcasted_iota(jnp.int32, (1, LANE), 1)
        loss_ref[...] += jnp.where(lane == 0, part, 0.0)
        dx2 = e * (1.0 / D_MODEL)
        dgff_ref[...] += jnp.sum(dx2 * f * rnf, axis=0, keepdims=True)
        df = _rms_bwd(dx2 * gff_ref[...], f, rnf).astype(BF16)
        df_ref[...] = df
        dh2 = jnp.zeros((TT, D_MODEL), F32)
        for c in range(2):
            sl = slice(FF_HALF * c, FF_HALF * (c + 1))
            dact = _dot(df, wd[sl, :], NT)
            gt = gt_s[:, sl]
            up = up_s[:, sl]
            sg = _sigmoid(gt)
            dgt = (dact * up * (sg * (1.0 + gt * (1.0 - sg)))).astype(BF16)
            dup = (dact * gt * sg).astype(BF16)
            dgt_ref[:, sl] = dgt
            dup_ref[:, sl] = dup
            dh2 += _dot(dgt, wg[:, sl], NT) + _dot(dup, wu[:, sl], NT)
        rn2 = _rms(x1)
        dgpf_ref[...] += jnp.sum(dh2 * x1 * rn2, axis=0, keepdims=True)
        dx1_ref[...] = dx2 + _rms_bwd(dh2 * gpf_ref[...], x1, rn2)

    tile = pl.BlockSpec((TT, D_MODEL), lambda i: (i, 0))
    wide = pl.BlockSpec((TT, D_FF), lambda i: (i, 0))
    vec = pl.BlockSpec((1, D_MODEL), lambda i: (0, 0))
    hbm = pl.BlockSpec(memory_space=pl.ANY)
    return pl.pallas_call(
        body, name="ffn_fwd_bwd", grid=(T // TT,),
        in_specs=[tile, tile, tile, hbm, hbm, hbm, vec, vec],
        out_specs=[tile, wide, tile, wide, wide, vec, vec, pl.BlockSpec((1, LANE), lambda i: (0, 0))],
        out_shape=[jax.ShapeDtypeStruct((T, D_MODEL), F32), jax.ShapeDtypeStruct((T, D_FF), BF16),
                   jax.ShapeDtypeStruct((T, D_MODEL), BF16), jax.ShapeDtypeStruct((T, D_FF), BF16),
                   jax.ShapeDtypeStruct((T, D_FF), BF16), jax.ShapeDtypeStruct((1, D_MODEL), F32),
                   jax.ShapeDtypeStruct((1, D_MODEL), F32), jax.ShapeDtypeStruct((1, LANE), F32)],
        scratch_shapes=[pltpu.VMEM((D_MODEL, D_FF), BF16), pltpu.VMEM((D_MODEL, D_FF), BF16),
                        pltpu.VMEM((D_FF, D_MODEL), BF16), pltpu.VMEM((TT, D_FF), F32),
                        pltpu.VMEM((TT, D_FF), F32), pltpu.SemaphoreType.DMA((3,))],
        compiler_params=_params(1),
    )(x1, h2, target, w_gate, w_up, w_down, g_pf, g_ff)


def _outproj_bwd(dx1, mix, w_out, g_pm):
    T = dx1.shape[0]

    def body(dx1_ref, mix_ref, wo_ref, gpm_ref, dyl_ref, dys_ref, dmix_ref, dg_ref):
        @pl.when(pl.program_id(0) == 0)
        def _():
            dg_ref[...] = jnp.zeros_like(dg_ref)

        mix = mix_ref[...]
        rn = _rms(mix)
        dx1v = dx1_ref[...]
        dg_ref[...] += jnp.sum(dx1v * mix * rn, axis=0, keepdims=True)
        dmix = _rms_bwd(dx1v * gpm_ref[...], mix, rn).astype(BF16)
        dmix_ref[...] = dmix
        dyl_ref[...] = _dot(dmix, wo_ref[0:LRU_W, :], NT)
        dys_ref[...] = _dot(dmix, wo_ref[LRU_W:2 * LRU_W, :], NT)

    tile = pl.BlockSpec((TT, D_MODEL), lambda i: (i, 0))
    vec = pl.BlockSpec((1, D_MODEL), lambda i: (0, 0))
    return pl.pallas_call(
        body, name="outproj_bwd", grid=(T // TT,),
        in_specs=[tile, tile, pl.BlockSpec((2 * LRU_W, D_MODEL), lambda i: (0, 0)), vec],
        out_specs=[tile, tile, tile, vec],
        out_shape=[jax.ShapeDtypeStruct((T, D_MODEL), F32), jax.ShapeDtypeStruct((T, D_MODEL), F32),
                   jax.ShapeDtypeStruct((T, D_MODEL), BF16), jax.ShapeDtypeStruct((1, D_MODEL), F32)],
        compiler_params=_params(1),
    )(dx1, mix, w_out, g_pm)


def _tn_matmul(a, bs, name, tk=512):
    T, M = a.shape
    nk = T // tk
    nb = len(bs)

    def body(*refs):
        a_ref, b_refs, o_refs, accs = refs[0], refs[1:1 + nb], refs[1 + nb:1 + 2 * nb], refs[1 + 2 * nb:]
        k = pl.program_id(0)

        @pl.when(k == 0)
        def _():
            for acc in accs:
                acc[...] = jnp.zeros_like(acc)

        av = a_ref[...].astype(BF16)
        for b_ref, acc in zip(b_refs, accs):
            acc[...] += _dot(av, b_ref[...], TN)

        @pl.when(k == nk - 1)
        def _():
            for o_ref, acc in zip(o_refs, accs):
                o_ref[...] = acc[...].astype(BF16)

    return pl.pallas_call(
        body, name=name, grid=(nk,),
        in_specs=[pl.BlockSpec((tk, M), lambda k: (k, 0))]
                 + [pl.BlockSpec((tk, b.shape[1]), lambda k: (k, 0)) for b in bs],
        out_specs=[pl.BlockSpec((M, b.shape[1]), lambda k: (0, 0)) for b in bs],
        out_shape=[jax.ShapeDtypeStruct((M, b.shape[1]), BF16) for b in bs],
        scratch_shapes=[pltpu.VMEM((M, b.shape[1]), F32) for b in bs],
        compiler_params=_params(1),
    )(a, *bs)


def _tn_blockdiag(a, b1, b2, name, tk=1024):
    T = a.shape[0]
    tk = min(tk, T)

    def body(a_ref, b1_ref, b2_ref, o1_ref, o2_ref):
        @pl.when(pl.program_id(0) == 0)
        def _():
            o1_ref[...] = jnp.zeros_like(o1_ref)
            o2_ref[...] = jnp.zeros_like(o2_ref)

        for j in range(4):
            sl = slice(256 * j, 256 * (j + 1))
            av = a_ref[:, sl].astype(BF16)
            o1_ref[j] += _dot(av, b1_ref[:, sl], TN)
            o2_ref[j] += _dot(av, b2_ref[:, sl], TN)

    blk = pl.BlockSpec((tk, LRU_W), lambda k: (k, 0))
    out = pl.BlockSpec((4, 256, 256), lambda k: (0, 0, 0))
    return pl.pallas_call(
        body, name=name, grid=(T // tk,),
        in_specs=[blk, blk, blk], out_specs=[out, out],
        out_shape=[jax.ShapeDtypeStruct((4, 256, 256), F32)] * 2,
        compiler_params=_params(1),
    )(a, b1, b2)


def _adamw(parts, w, m, v, name, tr):
    P, R, C = parts.shape

    def body(p_ref, w_ref, m_ref, v_ref, g_ref, d_ref, nm_ref, nv_ref):
        g = p_ref[0].astype(F32)
        for k in range(1, P):
            g = g + p_ref[k].astype(F32)
        g_ref[...] = g
        nm = ADAM_B1 * m_ref[...] + (1.0 - ADAM_B1) * g
        nv = ADAM_B2 * v_ref[...] + (1.0 - ADAM_B2) * (g * g)
        nm_ref[...] = nm
        nv_ref[...] = nv
        m_hat = nm / (1.0 - ADAM_B1 ** ADAM_STEP)
        v_hat = nv / (1.0 - ADAM_B2 ** ADAM_STEP)
        d_ref[...] = -ADAM_LR * (m_hat / (jnp.sqrt(v_hat) + ADAM_EPS) + ADAM_WD * w_ref[...])

    blk = pl.BlockSpec((tr, C), lambda i: (i, 0))
    return pl.pallas_call(
        body, name=name, grid=(R // tr,),
        in_specs=[pl.BlockSpec((P, tr, C), lambda i: (0, i, 0)), blk, blk, blk],
        out_specs=[blk, blk, blk, blk],
        out_shape=[jax.ShapeDtypeStruct((R, C), F32)] * 4,
        compiler_params=_params(1),
    )(parts, w, m, v)


def _peer(k):
    x, y, c = lax.axis_index("x"), lax.axis_index("y"), lax.axis_index("c")
    px = x ^ ((k >> 2) & 1)
    py = y ^ ((k >> 1) & 1)
    pc = c ^ (k & 1)
    return (px, py, pc), 4 * px + 2 * py + pc


def _my_block():
    return 4 * lax.axis_index("x") + 2 * lax.axis_index("y") + lax.axis_index("c")


def _all_gather(shards, name):
    n = len(shards)

    def body(*refs):
        ins, outs = refs[:n], refs[n:2 * n]
        send, recv, loc = refs[2 * n:]
        x, y, c = lax.axis_index("x"), lax.axis_index("y"), lax.axis_index("c")
        sibling = (x, y, 1 - c)
        chips = [(1 - x, y), (x, 1 - y), (1 - x, 1 - y)]
        slot = lambda px, py, pc: 4 * px + 2 * py + pc

        def copy(a, k, block, to, src=None):
            dst = outs[a].at[slot(*block)]
            return pltpu.make_async_remote_copy(
                src_ref=dst if src is None else src, dst_ref=dst, send_sem=send.at[a, k], recv_sem=recv.at[a, k],
                device_id=to, device_id_type=pl.DeviceIdType.MESH)

        mine = [pltpu.make_async_copy(ins[a], outs[a].at[slot(x, y, c)], loc.at[a]) for a in range(n)]
        for cp in mine:
            cp.start()
        first = []
        for a in range(n):
            first.append(copy(a, 0, (x, y, c), sibling, src=ins[a]))
            first += [copy(a, 1 + j, (x, y, c), (*chip, c), src=ins[a]) for j, chip in enumerate(chips)]
        for cp in first:
            cp.start()
        passed = []
        for j, chip in enumerate(chips):
            for a in range(n):
                copy(a, 1 + j, (*chip, c), (x, y, c)).wait_recv()
                fwd = copy(a, 4 + j, (*chip, c), sibling)
                fwd.start()
                passed.append(fwd)
        for a in range(n):
            copy(a, 0, sibling, (x, y, c)).wait_recv()
            for j, chip in enumerate(chips):
                copy(a, 4 + j, (*chip, 1 - c), (x, y, c)).wait_recv()
        for cp in first + passed:
            cp.wait_send()
        for cp in mine:
            cp.wait()

    hbm = pl.BlockSpec(memory_space=pl.ANY)
    return pl.pallas_call(
        body, name=name,
        in_specs=[hbm] * n, out_specs=[hbm] * n,
        out_shape=[jax.ShapeDtypeStruct((N_DEV,) + s.shape, s.dtype) for s in shards],
        scratch_shapes=[pltpu.SemaphoreType.DMA((n, N_DEV - 1)), pltpu.SemaphoreType.DMA((n, N_DEV - 1)),
                        pltpu.SemaphoreType.DMA((n,))],
    )(*shards)


_HBM = pl.BlockSpec(memory_space=pltpu.HBM)
_SEM = pl.BlockSpec(memory_space=pltpu.SEMAPHORE)
_EFFECT = pltpu.SideEffectType.DATAFLOW_SIDE_EFFECTING


def _direct_copies(srcs, lands, send, recv, slab_source):
    me = _my_block()
    cps = []
    for k in range(1, N_DEV):
        to, blk = _peer(k)
        for a, (src, land) in enumerate(zip(srcs, lands)):
            cps.append(pltpu.make_async_remote_copy(
                src_ref=src.at[blk] if slab_source else src, dst_ref=land.at[me],
                send_sem=send.at[a * (N_DEV - 1) + k - 1], recv_sem=recv.at[a * (N_DEV - 1) + k - 1],
                device_id=to, device_id_type=pl.DeviceIdType.MESH))
    return cps


def _exchange_start(srcs, name, slab_source):
    n = len(srcs)
    lands = [pltpu.with_memory_space_constraint(
        lax.empty(s.shape if slab_source else (N_DEV,) + s.shape, s.dtype), pltpu.HBM) for s in srcs]

    def body(*refs):
        ins, land_in = refs[:n], refs[n:2 * n]
        send, recv = refs[2 * n], refs[2 * n + 1]
        token = refs[4 * n + 2]
        for cp in _direct_copies(ins, land_in, send, recv, slab_source):
            cp.start()
        token[...] = jnp.zeros_like(token)

    sems = pltpu.SemaphoreType.DMA((n * (N_DEV - 1),))
    res = pl.pallas_call(
        body, name=name,
        out_shape=(sems, sems, *[pltpu.HBM(s.shape, s.dtype) for s in srcs],
                   *[pltpu.HBM(l.shape, l.dtype) for l in lands], jax.ShapeDtypeStruct((8, LANE), F32)),
        in_specs=[_HBM] * (2 * n),
        out_specs=(_SEM, _SEM, *[_HBM] * (2 * n), pl.BlockSpec(memory_space=pltpu.VMEM)),
        input_output_aliases={i: 2 + i for i in range(2 * n)},
        compiler_params=pltpu.CompilerParams(has_side_effects=_EFFECT),
    )(*[pltpu.with_memory_space_constraint(s, pltpu.HBM) for s in srcs], *lands)
    return dict(send=res[0], recv=res[1], srcs=res[2:2 + n], lands=res[2 + n:2 + 2 * n], token=res[-1],
                slab_source=slab_source)


def _exchange_wait(ex, after, name):
    n = len(ex["srcs"])
    slab_source = ex["slab_source"]

    def body(*refs):
        ins, lands = refs[:n], refs[n:2 * n]
        send, recv = refs[2 * n], refs[2 * n + 1]
        for cp in _direct_copies(ins, lands, send, recv, slab_source):
            cp.wait_send()
            cp.wait_recv()

    res = pl.pallas_call(
        body, name=name,
        out_shape=tuple(pltpu.HBM(s.shape, s.dtype) for s in list(ex["srcs"]) + list(ex["lands"])),
        in_specs=[_HBM] * (2 * n) + [_SEM, _SEM, pl.BlockSpec(memory_space=pl.ANY)],
        out_specs=tuple([_HBM] * (2 * n)),
        input_output_aliases={i: i for i in range(2 * n)},
        compiler_params=pltpu.CompilerParams(has_side_effects=_EFFECT),
    )(*ex["srcs"], *ex["lands"], ex["send"], ex["recv"], after)
    me = _my_block()
    out = []
    for src, land in zip(res[:n], res[n:]):
        own = lax.dynamic_index_in_dim(src, me, 0, keepdims=True) if slab_source else src[None]
        out.append(lax.dynamic_update_slice(land, own, (me,) + (0,) * (land.ndim - 1)))
    return out


BIG = ("w_in", "w_out", "w_gate", "w_up", "w_down")
BIG_SHARD = {"w_in": (1024, 578), "w_out": (256, 1024), "w_gate": (1024, 352), "w_up": (1024, 352),
             "w_down": (352, 1024)}
BIG_SHARD_AXIS = {"w_in": 1, "w_out": 0, "w_gate": 1, "w_up": 1, "w_down": 0}
BIG_ADAM_ROWS = {"w_in": 256, "w_out": 128, "w_gate": 256, "w_up": 256, "w_down": 176}


def _join(parts, axis):
    if axis == 0:
        return parts.reshape((-1,) + parts.shape[2:])
    return jnp.concatenate([parts[j] for j in range(N_DEV)], axis=1)


def _split(full, axis):
    if axis == 0:
        return full.reshape((N_DEV, full.shape[0] // N_DEV) + full.shape[1:])
    c = full.shape[1] // N_DEV
    return jnp.stack([full[:, c * j:c * (j + 1)] for j in range(N_DEV)])


SMALL = (("pre_mix_norm", 1024), ("lru_conv_w", 4096), ("lru_conv_b", 1024), ("lru_wa", 65536), ("lru_ba", 1024),
         ("lru_wx", 65536), ("lru_bx", 1024), ("lru_lambda", 1024), ("lru_out_norm", 1024), ("ssd_conv_w", 6144),
         ("ssd_conv_b", 1536), ("ssd_dt_bias", 16), ("ssd_a_log", 16), ("ssd_d", 16), ("ssd_out_norm", 1024),
         ("post_mix_norm", 1024), ("pre_ffn_norm", 1024), ("post_ffn_norm", 1024), ("loss", 1))
REPLICATED = tuple(n for n, _ in SMALL if n not in ("lru_conv_w", "ssd_conv_w", "loss"))


def _pack_small(d):
    flat = jnp.concatenate([d[n].astype(F32).reshape(-1) for n, _ in SMALL])
    return jnp.pad(flat, (0, SMALL_ROWS * 1024 - flat.shape[0])).reshape(SMALL_ROWS, 1024)


def _unpack_small(p):
    flat = p.reshape(-1)
    out, off = {}, 0
    for n, size in SMALL:
        out[n] = flat[off:off + size]
        off += size
    return out


def _blockdiag4(w):
    z = jnp.zeros((4, 4, 64, 4, 64), w.dtype)
    w4 = w.reshape(4, 4, 64, 64)
    for b in range(4):
        z = z.at[:, b, :, b, :].set(w4[:, b])
    return z.reshape(4, 256, 256)


def _diag_blocks(g):
    g5 = g.reshape(4, 4, 64, 4, 64)
    return jnp.stack([g5[:, b, :, b, :] for b in range(4)], axis=1).reshape(16, 64, 64)


def _local_step(x, target, w_in, P, rest_weights, emit, emit_small, start_token=None):
    w_main = w_in[:, :IN_MAIN]
    w_dt = jnp.pad(w_in[:, IN_MAIN:], ((0, 0), (0, LANE - SSD_HEADS)))
    pad16 = lambda v: jnp.pad(v.reshape(1, SSD_HEADS), ((0, 0), (0, LANE - SSD_HEADS)))
    dt_bias, a_log = pad16(P["ssd_dt_bias"]), pad16(P["ssd_a_log"])
    d_e = jnp.repeat(P["ssd_d"].reshape(SSD_HEADS), SSD_HEAD_DIM).reshape(1, SSD_INNER)
    expand = (jnp.arange(LANE)[:, None] == (jnp.arange(SSD_INNER)[None, :] // SSD_HEAD_DIM)).astype(BF16)
    wa_bd = _blockdiag4(P["lru_wa"].astype(BF16))
    wx_bd = _blockdiag4(P["lru_wx"].astype(BF16))
    vec = lambda n: P[n].reshape(1, -1)

    after = lambda v, tok: v if tok is None else v + tok[0:1, 0:1]

    h, proj, dtp = _inproj_fwd(x, after(vec("pre_mix_norm"), start_token), w_main, w_dt)
    lx, hl, y_lru, *gates = _lru_fwd(proj, P["lru_conv_w"], vec("lru_conv_b"), wa_bd, wx_bd, vec("lru_ba"),
                                     vec("lru_bx"), vec("lru_lambda"), vec("lru_out_norm"))
    xc, y, y_ssd, sprev = _ssd_fwd(proj, dtp, P["ssd_conv_w"], vec("ssd_conv_b"), dt_bias, a_log, d_e,
                                   vec("ssd_out_norm"), expand)
    W = rest_weights(y_ssd)
    mix, x1, h2 = _outproj_fwd(x, y_lru, y_ssd, W["w_out"], vec("post_mix_norm"), vec("pre_ffn_norm"))
    dx1, act, df, dgt, dup, dg_pf, dg_ff, loss = _ffn_fwd_bwd(
        x1, h2, target, W["w_gate"], W["w_up"], W["w_down"], vec("pre_ffn_norm"), vec("post_ffn_norm"))
    tok = emit("ffn", {"w_gate": _tn_matmul(h2, [dgt], "dw_gate")[0], "w_up": _tn_matmul(h2, [dup], "dw_up")[0],
                       "w_down": _tn_matmul(act, [df], "dw_down")[0]})
    dy_lru, dy_ssd, dmix, dg_pm = _outproj_bwd(dx1, mix, W["w_out"], after(vec("post_mix_norm"), tok))
    tok = emit("out", {"w_out": jnp.concatenate([_tn_matmul(y_lru, [dmix], "dw_out_lru")[0],
                                                 _tn_matmul(y_ssd, [dmix], "dw_out_ssd")[0]], axis=0)})
    dp_ssd, ddtp, dcw_s, dcb_s, dbias, dA, dD_e, dg_ssd = _ssd_bwd(
        dy_ssd, proj, dtp, xc, y, sprev, P["ssd_conv_w"], dt_bias, a_log, d_e,
        after(vec("ssd_out_norm"), tok), expand)
    dp_lru, dpa, dpx, dcw_l, dcb_l, dba, dbx, dlam, dg_lru = _lru_bwd(
        dy_lru, proj, lx, hl, gates, P["lru_conv_w"], wa_bd, wx_bd, vec("lru_lambda"), vec("lru_out_norm"))
    dw_in = _tn_matmul(h, [dp_lru, dp_ssd, ddtp], "dw_in")
    tok = emit("in", {"w_in": jnp.concatenate([dw_in[0], dw_in[1], dw_in[2][:, :SSD_HEADS]], axis=1)})

    a_neg = -jnp.exp(P["ssd_a_log"].reshape(SSD_HEADS))
    dwa, dwx = _tn_blockdiag(lx, dpa, dpx, "dw_lru_gates")
    small = {
        "pre_mix_norm": jnp.zeros((1, D_MODEL), F32), "lru_conv_w": dcw_l, "lru_conv_b": dcb_l,
        "lru_wa": _diag_blocks(dwa), "lru_ba": dba,
        "lru_wx": _diag_blocks(dwx), "lru_bx": dbx,
        "lru_lambda": dlam, "lru_out_norm": dg_lru, "ssd_conv_w": dcw_s, "ssd_conv_b": dcb_s,
        "ssd_dt_bias": dbias[0, :SSD_HEADS], "ssd_a_log": dA[0, :SSD_HEADS] * a_neg,
        "ssd_d": jnp.sum(dD_e.reshape(SSD_HEADS, SSD_HEAD_DIM), axis=1), "ssd_out_norm": dg_ssd,
        "post_mix_norm": dg_pm, "pre_ffn_norm": dg_pf, "post_ffn_norm": dg_ff, "loss": loss[0, 0:1],
    }
    tok = after(after(vec("pre_mix_norm"), tok), emit_small(small))
    grad_x, dg_pre = _inproj_bwd(dp_lru, dp_ssd, ddtp, dx1, x, tok, w_main, w_dt)
    return grad_x, dg_pre


def kernel(x, pre_mix_norm, w_in, lru_conv_w, lru_conv_b, lru_wa, lru_ba, lru_wx, lru_bx, lru_lambda, lru_out_norm, ssd_conv_w, ssd_conv_b, ssd_dt_bias, ssd_a_log, ssd_d, ssd_out_norm, w_out, post_mix_norm, pre_ffn_norm, w_gate, w_up, w_down, post_ffn_norm, loss_target, m_pre_mix_norm, m_w_in, m_lru_conv_w, m_lru_conv_b, m_lru_wa, m_lru_ba, m_lru_wx, m_lru_bx, m_lru_lambda, m_lru_out_norm, m_ssd_conv_w, m_ssd_conv_b, m_ssd_dt_bias, m_ssd_a_log, m_ssd_d, m_ssd_out_norm, m_w_out, m_post_mix_norm, m_pre_ffn_norm, m_w_gate, m_w_up, m_w_down, m_post_ffn_norm, v_pre_mix_norm, v_w_in, v_lru_conv_w, v_lru_conv_b, v_lru_wa, v_lru_ba, v_lru_wx, v_lru_bx, v_lru_lambda, v_lru_out_norm, v_ssd_conv_w, v_ssd_conv_b, v_ssd_dt_bias, v_ssd_a_log, v_ssd_d, v_ssd_out_norm, v_w_out, v_post_mix_norm, v_pre_ffn_norm, v_w_gate, v_w_up, v_w_down, v_post_ffn_norm):
    a = dict(locals())
    names = [n for n, _ in SMALL if n != "loss"] + list(BIG)
    w = {n: a[n][0] for n in names}
    m = {n: a["m_" + n][0] for n in names}
    v = {n: a["v_" + n][0] for n in names}

    cpack = jnp.concatenate([w["lru_conv_w"], w["ssd_conv_w"], jnp.zeros((4, 64), F32)], axis=1)
    cpack = jnp.pad(cpack, ((0, 4), (0, 0)))
    g_in, cg = _all_gather([w["w_in"].astype(BF16), cpack], "all_gather_w_in")
    P = {n: w[n] for n in REPLICATED}
    P["lru_conv_w"] = _join(cg[:, 0:4, 0:128], 1)
    P["ssd_conv_w"] = _join(cg[:, 0:4, 128:320], 1)

    rest = [n for n in BIG if n != "w_in"]
    zero = jnp.minimum(jnp.abs(cg[0, 0, 0]), 0.0)
    ex_w = _exchange_start([(w[n] + zero).astype(BF16) for n in rest], "weights_start", slab_source=False)

    def rest_weights(after):
        lands = _exchange_wait(ex_w, after, "weights_wait")
        return {n: _join(p, BIG_SHARD_AXIS[n]) for n, p in zip(rest, lands)}

    pending = []

    def emit(group, grads):
        ex = _exchange_start([_split(g, BIG_SHARD_AXIS[n]) for n, g in grads.items()], "grads_start_" + group,
                             slab_source=True)
        pending.append((group, list(grads), ex))
        return ex["token"]

    def emit_small(small):
        ex = _exchange_start([_pack_small(small)], "small_start", slab_source=False)
        pending.append(("small", None, ex))
        return ex["token"]

    grad_x, dg_pre = _local_step(x[0], loss_target[0], _join(g_in, 1), P, rest_weights, emit, emit_small,
                                 ex_w["token"])

    got_pre = _all_gather([jnp.pad(dg_pre, ((0, 7), (0, 0)))], "all_gather_pre_mix_norm")[0]
    ex_small = pending.pop()[2]
    got_small = _exchange_wait(ex_small, got_pre, "small_wait")[0]
    got_small = got_small.at[:, 0:1, :].set(got_pre[:, 0:1, :])

    outs = {}
    done = got_small
    for group, group_names, ex in pending:
        for n, parts in zip(group_names, _exchange_wait(ex, done, "grads_wait_" + group)):
            res = _adamw(parts, w[n], m[n], v[n], "adamw_" + n, BIG_ADAM_ROWS[n])
            done = res[0]
            for kind, r in zip(("grad", "delta", "new_m", "new_v"), res):
                outs[kind + "_" + n] = r

    zero_extra = {"loss": jnp.zeros((1,), F32)}
    full_conv = {"lru_conv_w": jnp.zeros((4, LRU_W), F32), "ssd_conv_w": jnp.zeros((4, SSD_CONV_CH), F32)}
    rep = lambda d: _pack_small({**{n: d[n] for n in REPLICATED}, **full_conv, **zero_extra})
    res = _adamw(got_small, rep(w), rep(m), rep(v), "adamw_small", SMALL_ROWS)
    g_small = _unpack_small(res[0])
    for kind, r in zip(("grad", "delta", "new_m", "new_v"), res):
        for n, val in _unpack_small(r).items():
            if n in REPLICATED:
                outs[kind + "_" + n] = val.reshape(w[n].shape)

    me = _my_block()
    gl = lax.dynamic_slice(g_small["lru_conv_w"].reshape(4, LRU_W), (0, me * 128), (4, 128))
    gs = lax.dynamic_slice(g_small["ssd_conv_w"].reshape(4, SSD_CONV_CH), (0, me * 192), (4, 192))
    cat = lambda d: jnp.pad(jnp.concatenate([d["lru_conv_w"], d["ssd_conv_w"]], axis=1), ((0, 4), (0, 64)))
    res = _adamw(cat({"lru_conv_w": gl, "ssd_conv_w": gs})[None], cat(w), cat(m), cat(v), "adamw_conv", 8)
    for kind, r in zip(("grad", "delta", "new_m", "new_v"), res):
        outs[kind + "_lru_conv_w"] = r[0:4, 0:128]
        outs[kind + "_ssd_conv_w"] = r[0:4, 128:320]

    order = ["pre_mix_norm", "w_in", "lru_conv_w", "lru_conv_b", "lru_wa", "lru_ba", "lru_wx", "lru_bx", "lru_lambda",
             "lru_out_norm", "ssd_conv_w", "ssd_conv_b", "ssd_dt_bias", "ssd_a_log", "ssd_d", "ssd_out_norm", "w_out",
             "post_mix_norm", "pre_ffn_norm", "w_gate", "w_up", "w_down", "post_ffn_norm"]
    result = [g_small["loss"].reshape(()), grad_x[None]]
    for kind in ("grad", "delta", "new_m", "new_v"):
        result += [outs[kind + "_" + n][None] for n in order]
    return tuple(result)
```

```python
import functools

import jax
import jax.numpy as jnp
from jax import lax
from jax.experimental import pallas as pl
from jax.experimental.pallas import tpu as pltpu

F32 = jnp.float32
BF16 = jnp.bfloat16
HI = lax.Precision.HIGHEST
EPS = 1e-6
N_DEV = 8
D_MODEL = 1024
LRU_W = 1024
SSD_INNER = 1024
SSD_HEADS = 16
SSD_HEAD_DIM = 64
SSD_STATE = 128
SSD_GROUPS = 2
SSD_CONV_CH = 1536
CHUNK = 128
D_FF = 2816
FF_HALF = D_FF // 2
IN_MAIN = 4608
IN_COLS = 4624
LANE = 128
TT = 256
VMEM_LIMIT = 56 * 1024 * 1024
ADAM_LR, ADAM_B1, ADAM_B2, ADAM_EPS, ADAM_WD, ADAM_STEP = 0.001, 0.9, 0.999, 1e-08, 0.01, 10
PACK_ROWS = 1920
SMALL_ROWS = 160

NT = (((1,), (1,)), ((), ()))
TN = (((0,), (0,)), ((), ()))


def _params(n_grid):
    return pltpu.CompilerParams(dimension_semantics=("arbitrary",) * n_grid, vmem_limit_bytes=VMEM_LIMIT)


def _dot(a, b, dims=None, precision=None):
    if dims is None:
        return jnp.dot(a, b, preferred_element_type=F32, precision=precision)
    return lax.dot_general(a, b, dims, preferred_element_type=F32, precision=precision)


def _split_bf16(x, terms):
    out = []
    for _ in range(terms - 1):
        p = x.astype(BF16)
        out.append(p)
        x = x - p.astype(F32)
    return out + [x.astype(BF16)]


def _dot_sel(x, sel, dims=None, terms=2, sel_first=False):
    parts = [_dot(sel, p, dims) if sel_first else _dot(p, sel, dims) for p in _split_bf16(x, terms)]
    return functools.reduce(lambda a, b: a + b, parts)


def _sigmoid(x):
    return 0.5 * jnp.tanh(0.5 * x) + 0.5


def _softplus(x):
    e = jnp.exp(-jnp.abs(x))
    l1p = jnp.where(e < 1e-3, e * (1.0 - e * (0.5 - e * (1.0 / 3.0))), jnp.log(1.0 + e))
    return jnp.maximum(x, 0.0) + l1p


def _neg_expm1(x):
    series = -x * (1.0 + x * (0.5 + x * (1.0 / 6.0 + x * (1.0 / 24.0))))
    return jnp.where(x > -0.01, series, 1.0 - jnp.exp(x))


_GELU_C = 0.7978845608028654


def _gelu(x):
    t = jnp.tanh(_GELU_C * (x + 0.044715 * x * x * x))
    return 0.5 * x * (1.0 + t), t


def _gelu_grad(x, t):
    return 0.5 * (1.0 + t) + 0.5 * x * (1.0 - t * t) * _GELU_C * (1.0 + 3.0 * 0.044715 * x * x)


def _rms(x):
    return lax.rsqrt(jnp.mean(x * x, axis=-1, keepdims=True) + EPS)


def _rms_bwd(dyn, x, rn):
    return rn * dyn - x * (rn * rn * rn) * jnp.mean(dyn * x, axis=-1, keepdims=True)


def _row(x, r):
    idx = lax.broadcasted_iota(jnp.int32, x.shape, 0)
    return jnp.sum(jnp.where(idx == r, x, 0.0), axis=0, keepdims=True)


def _shift_down(cur, prev8, j):
    s = pltpu.roll(cur, j, 0)
    p = pltpu.roll(prev8, j, 0)
    r8 = lax.broadcasted_iota(jnp.int32, prev8.shape, 0)
    top = jnp.where(r8 < j, p, s[0:8])
    return jnp.concatenate([top, s[8:]], axis=0)


def _shift_up(cur, next8, j):
    n = cur.shape[0]
    s = pltpu.roll(cur, n - j, 0)
    p = pltpu.roll(next8, 8 - j, 0)
    r8 = lax.broadcasted_iota(jnp.int32, next8.shape, 0)
    bot = jnp.where(r8 >= 8 - j, p, s[n - 8:n])
    return jnp.concatenate([s[:n - 8], bot], axis=0)


def _scan_fwd(a, u):
    n = a.shape[0]
    row = lax.broadcasted_iota(jnp.int32, a.shape, 0)
    k = 1
    while k < n:
        ok = row >= k
        a_s = jnp.where(ok, pltpu.roll(a, k, 0), 1.0)
        u_s = jnp.where(ok, pltpu.roll(u, k, 0), 0.0)
        u = a * u_s + u
        a = a * a_s
        k *= 2
    return a, u


def _scan_bwd(b, d):
    n = b.shape[0]
    row = lax.broadcasted_iota(jnp.int32, b.shape, 0)
    k = 1
    while k < n:
        ok = row < n - k
        b_s = jnp.where(ok, pltpu.roll(b, n - k, 0), 1.0)
        d_s = jnp.where(ok, pltpu.roll(d, n - k, 0), 0.0)
        d = b * d_s + d
        b = b * b_s
        k *= 2
    return b, d


def _scan_tile(a, u, carry, a_s, u_s, reverse):
    n, c = a.shape
    groups = n // 8
    r8 = lax.broadcasted_iota(jnp.int32, a.shape, 0) & 7
    in_group = lambda x, k: pltpu.roll(x.reshape(groups, 8, c), k, 1).reshape(n, c)
    for k in (1, 2, 4):
        ok = (r8 < 8 - k) if reverse else (r8 >= k)
        shift = 8 - k if reverse else k
        a_n = jnp.where(ok, in_group(a, shift), 1.0)
        u_n = jnp.where(ok, in_group(u, shift), 0.0)
        u = a * u_n + u
        a = a * a_n
    nl = c // LANE
    for j in range(nl):
        a_s[j] = a[:, LANE * j:LANE * (j + 1)]
        u_s[j] = u[:, LANE * j:LANE * (j + 1)]
    end = 0 if reverse else 7
    ends = lambda ref, j: ref[pl.ds(j, 1), pl.ds(end, groups, stride=8), :].reshape(groups, LANE)
    ga = jnp.concatenate([ends(a_s, j) for j in range(nl)], axis=1)
    gu = jnp.concatenate([ends(u_s, j) for j in range(nl)], axis=1)
    gacc, gh = (_scan_bwd if reverse else _scan_fwd)(ga, gu)
    gh = gh + gacc * carry
    grow = lax.broadcasted_iota(jnp.int32, gh.shape, 0)
    if reverse:
        cin = jnp.where(grow == groups - 1, carry, pltpu.roll(gh, groups - 1, 0))
    else:
        cin = jnp.where(grow == 0, carry, pltpu.roll(gh, 1, 0))
    spread = ((lax.broadcasted_iota(jnp.int32, (n, LANE), 0) >> 3)
              == lax.broadcasted_iota(jnp.int32, (n, LANE), 1)).astype(BF16)
    cin = jnp.concatenate([cin, jnp.zeros((LANE - groups, c), F32)], axis=0)
    return u + a * _dot_sel(cin, spread, terms=3, sel_first=True)


def _inproj_fwd(x, g_pre, w_main, w_dt):
    T = x.shape[0]

    def body(x_ref, g_ref, wm_hbm, wd_hbm, h_ref, proj_ref, dtp_ref, wm, wd, sem):
        @pl.when(pl.program_id(0) == 0)
        def _():
            c1 = pltpu.make_async_copy(wm_hbm, wm, sem.at[0])
            c2 = pltpu.make_async_copy(wd_hbm, wd, sem.at[1])
            c1.start()
            c2.start()
            c1.wait()
            c2.wait()

        xv = x_ref[...]
        h = (xv * _rms(xv) * g_ref[...]).astype(BF16)
        h_ref[...] = h
        proj_ref[...] = _dot(h, wm[...])
        dtp_ref[...] = _dot(h, wd[...])

    return pl.pallas_call(
        body, name="inproj_fwd", grid=(T // TT,),
        in_specs=[pl.BlockSpec((TT, D_MODEL), lambda i: (i, 0)),
                  pl.BlockSpec((1, D_MODEL), lambda i: (0, 0)),
                  pl.BlockSpec(memory_space=pl.ANY), pl.BlockSpec(memory_space=pl.ANY)],
        out_specs=[pl.BlockSpec((TT, D_MODEL), lambda i: (i, 0)),
                   pl.BlockSpec((TT, IN_MAIN), lambda i: (i, 0)),
                   pl.BlockSpec((TT, LANE), lambda i: (i, 0))],
        out_shape=[jax.ShapeDtypeStruct((T, D_MODEL), BF16), jax.ShapeDtypeStruct((T, IN_MAIN), F32),
                   jax.ShapeDtypeStruct((T, LANE), F32)],
        scratch_shapes=[pltpu.VMEM((D_MODEL, IN_MAIN), BF16), pltpu.VMEM((D_MODEL, LANE), BF16),
                        pltpu.SemaphoreType.DMA((2,))],
        compiler_params=_params(1),
    )(x, g_pre, w_main, w_dt)


def _inproj_bwd(dp_lru, dp_ssd, ddtp, dx1, x, g_pre, w_main, w_dt):
    T = x.shape[0]

    def body(dl_ref, ds_ref, dd_ref, dx1_ref, x_ref, g_ref, wm_hbm, wd_hbm, gx_ref, dg_ref, wm, wd, sem):
        @pl.when(pl.program_id(0) == 0)
        def _():
            c1 = pltpu.make_async_copy(wm_hbm, wm, sem.at[0])
            c2 = pltpu.make_async_copy(wd_hbm, wd, sem.at[1])
            c1.start()
            c2.start()
            c1.wait()
            c2.wait()
            dg_ref[...] = jnp.zeros_like(dg_ref)

        dh = _dot(dl_ref[...], wm[:, 0:2048], NT)
        dh += _dot(ds_ref[...], wm[:, 2048:IN_MAIN], NT)
        dh += _dot(dd_ref[...], wd[...], NT)
        xv = x_ref[...]
        rn = _rms(xv)
        dg_ref[...] += jnp.sum(dh * xv * rn, axis=0, keepdims=True)
        gx_ref[...] = dx1_ref[...] + _rms_bwd(dh * g_ref[...], xv, rn)

    return pl.pallas_call(
        body, name="inproj_bwd", grid=(T // TT,),
        in_specs=[pl.BlockSpec((TT, 2048), lambda i: (i, 0)),
                  pl.BlockSpec((TT, 2560), lambda i: (i, 0)),
                  pl.BlockSpec((TT, LANE), lambda i: (i, 0)),
                  pl.BlockSpec((TT, D_MODEL), lambda i: (i, 0)),
                  pl.BlockSpec((TT, D_MODEL), lambda i: (i, 0)),
                  pl.BlockSpec((1, D_MODEL), lambda i: (0, 0)),
                  pl.BlockSpec(memory_space=pl.ANY), pl.BlockSpec(memory_space=pl.ANY)],
        out_specs=[pl.BlockSpec((TT, D_MODEL), lambda i: (i, 0)),
                   pl.BlockSpec((1, D_MODEL), lambda i: (0, 0))],
        out_shape=[jax.ShapeDtypeStruct((T, D_MODEL), F32), jax.ShapeDtypeStruct((1, D_MODEL), F32)],
        scratch_shapes=[pltpu.VMEM((D_MODEL, IN_MAIN), BF16), pltpu.VMEM((D_MODEL, LANE), BF16),
                        pltpu.SemaphoreType.DMA((2,))],
        compiler_params=_params(1),
    )(dp_lru, dp_ssd, ddtp, dx1, x, g_pre, w_main, w_dt)


def _lru_gates(lx, wa_ref, wx_ref, ba, bx, lam):
    lxb = lx.astype(BF16)
    pa = jnp.concatenate([_dot(lxb[:, 256 * k:256 * (k + 1)], wa_ref[k]) for k in range(4)], axis=1) + ba
    px = jnp.concatenate([_dot(lxb[:, 256 * k:256 * (k + 1)], wx_ref[k]) for k in range(4)], axis=1) + bx
    r = _sigmoid(pa)
    ig = _sigmoid(px)
    sp = _softplus(-lam)
    log_a = -8.0 * r * sp
    a = jnp.exp(log_a)
    mult = jnp.sqrt(_neg_expm1(2.0 * log_a))
    return r, ig, sp, a, mult


def _lru_fwd(proj, conv_w, conv_b, wa_bd, wx_bd, ba, bx, lam, g_lru):
    T = proj.shape[0]

    def body(cx_ref, gate_ref, cw_ref, cb_ref, wa_ref, wx_ref, ba_ref, bx_ref, lam_ref, g_ref,
             lx_ref, hl_ref, y_ref, r_ref, ig_ref, a_ref, mult_ref, tail, hcar, sa, su):
        @pl.when(pl.program_id(0) == 0)
        def _():
            tail[...] = jnp.zeros_like(tail)
            hcar[...] = jnp.zeros_like(hcar)

        cx = cx_ref[...]
        prev8 = tail[...]
        lx = cb_ref[...] + cw_ref[3:4, :] * cx
        for j in range(1, 4):
            lx += cw_ref[3 - j:4 - j, :] * _shift_down(cx, prev8, j)
        tail[...] = cx[TT - 8:TT]
        lx_ref[...] = lx
        r, ig, sp, a, mult = _lru_gates(lx, wa_ref, wx_ref, ba_ref[...], bx_ref[...], lam_ref[...])
        r_ref[...] = r
        ig_ref[...] = ig
        a_ref[...] = a
        mult_ref[...] = mult
        h = _scan_tile(a, mult * (ig * lx), hcar[...], sa, su, reverse=False)
        hl_ref[...] = h
        hcar[...] = hl_ref[TT - 1:TT, :]
        ge, _ = _gelu(gate_ref[...])
        p = h * ge
        y_ref[...] = (p * _rms(p) * g_ref[...]).astype(BF16)

    vec = pl.BlockSpec((1, LRU_W), lambda i: (0, 0))
    bd = pl.BlockSpec((4, 256, 256), lambda i: (0, 0, 0))
    tile = pl.BlockSpec((TT, LRU_W), lambda i: (i, 0))
    f32 = jax.ShapeDtypeStruct((T, LRU_W), F32)
    return pl.pallas_call(
        body, name="lru_fwd", grid=(T // TT,),
        in_specs=[tile, pl.BlockSpec((TT, LRU_W), lambda i: (i, 1)),
                  pl.BlockSpec((4, LRU_W), lambda i: (0, 0)), vec, bd, bd, vec, vec, vec, vec],
        out_specs=[tile] * 7,
        out_shape=[f32, f32, jax.ShapeDtypeStruct((T, LRU_W), BF16), f32, f32, f32, f32],
        scratch_shapes=[pltpu.VMEM((8, LRU_W), F32), pltpu.VMEM((1, LRU_W), F32)]
                       + [pltpu.VMEM((LRU_W // LANE, TT, LANE), F32)] * 2,
        compiler_params=_params(1),
    )(proj, proj, conv_w, conv_b, wa_bd, wx_bd, ba, bx, lam, g_lru)


def _lru_bwd(dy, proj, lx, hl, gates, conv_w, wa_bd, wx_bd, lam, g_lru):
    T = proj.shape[0]
    nt = T // TT

    def body(dy_ref, cx_ref, gate_ref, lx_ref, hl_ref, halo_ref, r_ref, ig_ref, a_ref, mult_ref, cw_ref, wa_ref,
             wx_ref, lam_ref, g_ref, dp_ref, dpa_ref, dpx_ref, dcw_ref, dcb_ref, dba_ref, dbx_ref, dlam_ref, dg_ref,
             gcar, acar, head, sa, su):
        i = pl.program_id(0)

        @pl.when(i == 0)
        def _():
            gcar[...] = jnp.zeros_like(gcar)
            acar[...] = jnp.zeros_like(acar)
            head[...] = jnp.zeros_like(head)
            for ref in (dcw_ref, dcb_ref, dba_ref, dbx_ref, dlam_ref, dg_ref):
                ref[...] = jnp.zeros_like(ref)

        lx = lx_ref[...]
        h = hl_ref[...]
        gate = gate_ref[...]
        cx = cx_ref[...]
        lam = lam_ref[...]
        r, ig, a, mult = r_ref[...], ig_ref[...], a_ref[...], mult_ref[...]
        sp = _softplus(-lam)
        ge, th = _gelu(gate)
        p = h * ge
        rn = _rms(p)
        dyv = dy_ref[...]
        dg_ref[...] += jnp.sum(dyv * p * rn, axis=0, keepdims=True)
        dp = _rms_bwd(dyv * g_ref[...], p, rn)
        dp_ref[:, LRU_W:2 * LRU_W] = (dp * h * _gelu_grad(gate, th)).astype(BF16)
        dh = dp * ge
        row = lax.broadcasted_iota(jnp.int32, a.shape, 0)
        b = jnp.where(row == TT - 1, acar[...], pltpu.roll(a, TT - 1, 0))
        g = _scan_tile(b, dh, gcar[...], sa, su, reverse=True)
        gcar[...] = _row(g[0:8], 0)
        acar[...] = _row(a[0:8], 0)
        h_last_prev = halo_ref[7:8, :] * (i < nt - 1).astype(F32)
        hprev = jnp.where(row == 0, h_last_prev, pltpu.roll(h, 1, 0))
        da = g * hprev
        dm2 = (g * (ig * lx)) * 0.5 / mult
        dlog_a = da * a - 2.0 * a * a * dm2
        dlam_ref[...] += jnp.sum(-8.0 * r * dlog_a, axis=0, keepdims=True) * (-_sigmoid(-lam))
        dpa = (-8.0 * sp * dlog_a) * r * (1.0 - r)
        dpx = (g * mult * lx) * ig * (1.0 - ig)
        dba_ref[...] += jnp.sum(dpa, axis=0, keepdims=True)
        dbx_ref[...] += jnp.sum(dpx, axis=0, keepdims=True)
        dpab = dpa.astype(BF16)
        dpxb = dpx.astype(BF16)
        dpa_ref[...] = dpab
        dpx_ref[...] = dpxb
        dlx = g * mult * ig + jnp.concatenate(
            [_dot(dpab[:, 256 * k:256 * (k + 1)], wa_ref[k], NT) + _dot(dpxb[:, 256 * k:256 * (k + 1)], wx_ref[k], NT)
             for k in range(4)], axis=1)
        nxt = head[...]
        dcb_ref[...] += jnp.sum(dlx, axis=0, keepdims=True)
        dcx = cw_ref[3:4, :] * dlx
        dcw_ref[3:4, :] += jnp.sum(cx * dlx, axis=0, keepdims=True)
        for j in range(1, 4):
            sh = _shift_up(dlx, nxt, j)
            dcx += cw_ref[3 - j:4 - j, :] * sh
            dcw_ref[3 - j:4 - j, :] += jnp.sum(cx * sh, axis=0, keepdims=True)
        head[...] = dlx[0:8]
        dp_ref[:, 0:LRU_W] = dcx.astype(BF16)

    rev = lambda i: (nt - 1 - i, 0)
    vec = pl.BlockSpec((1, LRU_W), lambda i: (0, 0))
    bd = pl.BlockSpec((4, 256, 256), lambda i: (0, 0, 0))
    tile = pl.BlockSpec((TT, LRU_W), rev)
    halo = pl.BlockSpec((8, LRU_W), lambda i: (jnp.maximum((nt - 1 - i) * (TT // 8) - 1, 0), 0))
    cw = pl.BlockSpec((4, LRU_W), lambda i: (0, 0))
    return pl.pallas_call(
        body, name="lru_bwd", grid=(nt,),
        in_specs=[tile, tile, pl.BlockSpec((TT, LRU_W), lambda i: (nt - 1 - i, 1)), tile, tile, halo,
                  tile, tile, tile, tile, cw, bd, bd, vec, vec],
        out_specs=[pl.BlockSpec((TT, 2 * LRU_W), rev), tile, tile, cw, vec, vec, vec, vec, vec],
        out_shape=[jax.ShapeDtypeStruct((T, 2 * LRU_W), BF16), jax.ShapeDtypeStruct((T, LRU_W), BF16),
                   jax.ShapeDtypeStruct((T, LRU_W), BF16), jax.ShapeDtypeStruct((4, LRU_W), F32)]
                  + [jax.ShapeDtypeStruct((1, LRU_W), F32)] * 5,
        scratch_shapes=[pltpu.VMEM((1, LRU_W), F32), pltpu.VMEM((1, LRU_W), F32), pltpu.VMEM((8, LRU_W), F32)]
                       + [pltpu.VMEM((LRU_W // LANE, TT, LANE), F32)] * 2,
        compiler_params=_params(1),
    )(dy, proj, proj, lx, hl, hl, *gates, conv_w, wa_bd, wx_bd, lam, g_lru)


def _ssd_chunk_terms(xc, dtp, bias, alog, expand):
    sg = _sigmoid(xc)
    xbc = xc * sg
    pre = dtp + bias
    dt = _softplus(pre)
    A = -jnp.exp(alog)
    ri = lax.broadcasted_iota(jnp.int32, (CHUNK, CHUNK), 0)
    ci = lax.broadcasted_iota(jnp.int32, (CHUNK, CHUNK), 1)
    tril = (ri >= ci).astype(BF16)
    cs = _dot_sel(dt * A, tril, terms=3, sel_first=True)
    cs_last = _row(cs, CHUNK - 1)
    ecs = jnp.exp(cs)
    dec = jnp.exp(cs_last - cs)
    return dict(sg=sg, xbc=xbc, pre=pre, dt=dt, A=A, cs=cs, csT=cs.T, ecs=ecs, dec=dec, ri=ri, ci=ci,
                dt_e=_dot_sel(dt, expand), ecs_e=_dot_sel(ecs, expand), dec_e=_dot_sel(dec, expand))


def _head_lambda(t, h):
    col = jnp.sum(jnp.where(t["ci"] == h, t["cs"], 0.0), axis=1, keepdims=True)
    rowv = jnp.sum(jnp.where(t["ri"] == h, t["csT"], 0.0), axis=0, keepdims=True)
    return jnp.exp(jnp.where(t["ri"] >= t["ci"], col - rowv, -1e30))


def _ssd_fwd(proj, dtp, conv_w, conv_b, dt_bias, a_log, d_e, g_ssd, expand):
    T = proj.shape[0]
    nc = T // CHUNK

    def body(z_ref, xp_ref, dtp_ref, cw_ref, cb_ref, bias_ref, alog_ref, de_ref, g_ref, ex_ref,
             xc_ref, y_ref, yn_ref, sprev_ref, tail, S):
        @pl.when(pl.program_id(0) == 0)
        def _():
            tail[...] = jnp.zeros_like(tail)
            S[...] = jnp.zeros_like(S)

        xp = xp_ref[...]
        prev8 = tail[...]
        xc = cb_ref[...] + cw_ref[3:4, :] * xp
        for j in range(1, 4):
            xc += cw_ref[3 - j:4 - j, :] * _shift_down(xp, prev8, j)
        tail[...] = xp[CHUNK - 8:CHUNK]
        xc_ref[...] = xc
        t = _ssd_chunk_terms(xc, dtp_ref[...], bias_ref[...], alog_ref[...], ex_ref[...])
        xbc = t["xbc"]
        sx = xbc[:, 0:SSD_INNER]
        Bb = xbc[:, SSD_INNER:SSD_INNER + 256].astype(BF16)
        Cb = xbc[:, SSD_INNER + 256:SSD_CONV_CH].astype(BF16)
        X = t["dt_e"] * sx
        lane = lax.broadcasted_iota(jnp.int32, (CHUNK, LANE), 1)
        G = [_dot(Cb[:, 128 * g:128 * (g + 1)], Bb[:, 128 * g:128 * (g + 1)], NT) for g in range(SSD_GROUPS)]
        for k in range(SSD_HEADS // 2):
            Xp = X[:, 128 * k:128 * (k + 1)]
            acc = jnp.zeros((CHUNK, LANE), F32)
            for half in range(2):
                M = (G[k // 4] * _head_lambda(t, 2 * k + half)).astype(BF16)
                Xh = jnp.where((lane >= 64) if half else (lane < 64), Xp, 0.0).astype(BF16)
                acc += _dot(M, Xh)
            y_ref[:, 128 * k:128 * (k + 1)] = acc
        sprev_ref[0] = S[...]
        eL_e = _row(t["ecs_e"], CHUNK - 1)
        Xd = (X * t["dec_e"]).astype(BF16)
        for g in range(SSD_GROUPS):
            sl = slice(512 * g, 512 * (g + 1))
            Sg = S[:, sl]
            y_ref[:, sl] += t["ecs_e"][:, sl] * _dot(Cb[:, 128 * g:128 * (g + 1)], Sg.astype(BF16))
            S[:, sl] = eL_e[:, sl] * Sg + _dot(Bb[:, 128 * g:128 * (g + 1)], Xd[:, sl], TN)
        y = y_ref[...] + de_ref[...] * sx
        y_ref[...] = y
        z = z_ref[...]
        q = y * (z * _sigmoid(z))
        yn_ref[...] = (q * _rms(q) * g_ref[...]).astype(BF16)

    c0 = lambda i: (0, 0)
    return pl.pallas_call(
        body, name="ssd_fwd", grid=(nc,),
        in_specs=[pl.BlockSpec((CHUNK, SSD_INNER), lambda i: (i, 2)),
                  pl.BlockSpec((CHUNK, SSD_CONV_CH), lambda i: (i, 2)),
                  pl.BlockSpec((CHUNK, LANE), lambda i: (i, 0)),
                  pl.BlockSpec((4, SSD_CONV_CH), c0), pl.BlockSpec((1, SSD_CONV_CH), c0),
                  pl.BlockSpec((1, LANE), c0), pl.BlockSpec((1, LANE), c0),
                  pl.BlockSpec((1, SSD_INNER), c0), pl.BlockSpec((1, SSD_INNER), c0),
                  pl.BlockSpec((LANE, SSD_INNER), c0)],
        out_specs=[pl.BlockSpec((CHUNK, SSD_CONV_CH), lambda i: (i, 0)),
                   pl.BlockSpec((CHUNK, SSD_INNER), lambda i: (i, 0)),
                   pl.BlockSpec((CHUNK, SSD_INNER), lambda i: (i, 0)),
                   pl.BlockSpec((1, SSD_STATE, SSD_INNER), lambda i: (i, 0, 0))],
        out_shape=[jax.ShapeDtypeStruct((T, SSD_CONV_CH), F32), jax.ShapeDtypeStruct((T, SSD_INNER), F32),
                   jax.ShapeDtypeStruct((T, SSD_INNER), BF16),
                   jax.ShapeDtypeStruct((nc, SSD_STATE, SSD_INNER), F32)],
        scratch_shapes=[pltpu.VMEM((8, SSD_CONV_CH), F32), pltpu.VMEM((SSD_STATE, SSD_INNER), F32)],
        compiler_params=_params(1),
    )(proj, proj, dtp, conv_w, conv_b, dt_bias, a_log, d_e, g_ssd, expand)


def _ssd_bwd(dyn, proj, dtp, xc, y, sprev, conv_w, dt_bias, a_log, d_e, g_ssd, expand):
    T = proj.shape[0]
    nc = T // CHUNK

    def body(dyn_ref, z_ref, xp_ref, dtp_ref, xc_ref, y_ref, sprev_ref, cw_ref, bias_ref, alog_ref, de_ref,
             g_ref, ex_ref, dp_ref, ddtp_ref, dcw_ref, dcb_ref, dbias_ref, dA_ref, dD_ref, dg_ref,
             dS, head, dX_s, dxbc_s):
        @pl.when(pl.program_id(0) == 0)
        def _():
            dS[...] = jnp.zeros_like(dS)
            head[...] = jnp.zeros_like(head)
            for ref in (dcw_ref, dcb_ref, dbias_ref, dA_ref, dD_ref, dg_ref):
                ref[...] = jnp.zeros_like(ref)

        ex = ex_ref[...]
        xc = xc_ref[...]
        t = _ssd_chunk_terms(xc, dtp_ref[...], bias_ref[...], alog_ref[...], ex)
        ri, ci = t["ri"], t["ci"]
        xbc = t["xbc"]
        sx = xbc[:, 0:SSD_INNER]
        Bb = xbc[:, SSD_INNER:SSD_INNER + 256].astype(BF16)
        Cb = xbc[:, SSD_INNER + 256:SSD_CONV_CH].astype(BF16)
        X = t["dt_e"] * sx
        z = z_ref[...]
        sz = _sigmoid(z)
        siluz = z * sz
        yv = y_ref[...]
        q = yv * siluz
        rn = _rms(q)
        dynv = dyn_ref[...]
        dg_ref[...] += jnp.sum(dynv * q * rn, axis=0, keepdims=True)
        dq = _rms_bwd(dynv * g_ref[...], q, rn)
        dp_ref[:, 0:SSD_INNER] = (dq * yv * (sz * (1.0 + z * (1.0 - sz)))).astype(BF16)
        dY = dq * siluz
        dD_ref[...] += jnp.sum(dY * sx, axis=0, keepdims=True)
        dYb = dY.astype(BF16)
        lane = lax.broadcasted_iota(jnp.int32, (CHUNK, LANE), 1)
        dcs = jnp.zeros((CHUNK, CHUNK), F32)
        dcsT = jnp.zeros((CHUNK, CHUNK), F32)
        Xb = X.astype(BF16)
        for g in range(SSD_GROUPS):
            Bg = Bb[:, 128 * g:128 * (g + 1)]
            Cg = Cb[:, 128 * g:128 * (g + 1)]
            G = _dot(Cg, Bg, NT)
            dGsum = jnp.zeros((CHUNK, CHUNK), F32)
            for k in range(4 * g, 4 * g + 4):
                Xp = Xb[:, 128 * k:128 * (k + 1)]
                dYp = dY[:, 128 * k:128 * (k + 1)]
                dXp = jnp.zeros((CHUNK, LANE), F32)
                for half in range(2):
                    h = 2 * k + half
                    lam = _head_lambda(t, h)
                    M = G * lam
                    dYh = jnp.where((lane >= 64) if half else (lane < 64), dYp, 0.0).astype(BF16)
                    dM = _dot(dYh, Xp, NT)
                    W = dM * M
                    dcs += jnp.where(ci == h, jnp.sum(W, axis=1, keepdims=True), 0.0)
                    dcsT += jnp.where(ri == h, jnp.sum(W, axis=0, keepdims=True), 0.0)
                    dGsum += dM * lam
                    dXp += _dot(M.astype(BF16), dYh, TN)
                dX_s[:, 128 * k:128 * (k + 1)] = dXp
            dGb = dGsum.astype(BF16)
            dxbc_s[:, SSD_INNER + 256 + 128 * g:SSD_INNER + 256 + 128 * (g + 1)] = _dot(dGb, Bg)
            dxbc_s[:, SSD_INNER + 128 * g:SSD_INNER + 128 * (g + 1)] = _dot(dGb, Cg, TN)
        dcs = dcs - dcsT.T
        Sp = sprev_ref[0]
        dSv = dS[...]
        ecs_e, dec_e = t["ecs_e"], t["dec_e"]
        eL_e = _row(ecs_e, CHUNK - 1)
        dYe = dY * ecs_e
        dYeb = dYe.astype(BF16)
        Xd = X * dec_e
        Xdb = Xd.astype(BF16)
        for g in range(SSD_GROUPS):
            sl = slice(512 * g, 512 * (g + 1))
            Bg = Bb[:, 128 * g:128 * (g + 1)]
            Cg = Cb[:, 128 * g:128 * (g + 1)]
            Spb = Sp[:, sl].astype(BF16)
            dSb = dSv[:, sl].astype(BF16)
            CS = _dot(Cg, Spb)
            BS = _dot(Bg, dSb)
            dxbc_s[:, SSD_INNER + 256 + 128 * g:SSD_INNER + 256 + 128 * (g + 1)] += _dot(dYeb[:, sl], Spb, NT)
            dxbc_s[:, SSD_INNER + 128 * g:SSD_INNER + 128 * (g + 1)] += _dot(Xdb[:, sl], dSb, NT)
            dS[:, sl] = eL_e[:, sl] * dSv[:, sl] + _dot(Cg, dYeb[:, sl], TN)
            dX_s[:, sl] += dec_e[:, sl] * BS
            dcs += _dot_sel(dYe[:, sl] * CS, ex[:, sl], NT)
            tdec = _dot_sel(X[:, sl] * BS, ex[:, sl], NT) * t["dec"]
            dcs -= tdec
            last = jnp.sum(tdec, axis=0, keepdims=True)
            last += jnp.sum(_dot_sel(Sp[:, sl] * dSv[:, sl], ex[:, sl], NT), axis=0, keepdims=True) \
                * _row(t["ecs"], CHUNK - 1)
            dcs += jnp.where(ri == CHUNK - 1, last, 0.0)
        triu = (ci >= ri).astype(BF16)
        da = _dot_sel(dcs, triu, terms=3, sel_first=True)
        dX = dX_s[...]
        ddt = da * t["A"] + _dot_sel(dX * sx, ex, NT)
        dA_ref[...] += jnp.sum(da * t["dt"], axis=0, keepdims=True)
        ddtp = ddt * _sigmoid(t["pre"])
        dbias_ref[...] += jnp.sum(ddtp, axis=0, keepdims=True)
        ddtp_ref[...] = ddtp.astype(BF16)
        dxbc_s[:, 0:SSD_INNER] = dX * t["dt_e"] + de_ref[...] * dY
        sg = t["sg"]
        dxc = dxbc_s[...] * (sg * (1.0 + xc * (1.0 - sg)))
        xp = xp_ref[...]
        nxt = head[...]
        dcb_ref[...] += jnp.sum(dxc, axis=0, keepdims=True)
        dpre = cw_ref[3:4, :] * dxc
        dcw_ref[3:4, :] += jnp.sum(xp * dxc, axis=0, keepdims=True)
        for j in range(1, 4):
            sh = _shift_up(dxc, nxt, j)
            dpre += cw_ref[3 - j:4 - j, :] * sh
            dcw_ref[3 - j:4 - j, :] += jnp.sum(xp * sh, axis=0, keepdims=True)
        head[...] = dxc[0:8]
        dp_ref[:, SSD_INNER:SSD_INNER + SSD_CONV_CH] = dpre.astype(BF16)

    c0 = lambda i: (0, 0)
    rev = lambda i: (nc - 1 - i, 0)
    return pl.pallas_call(
        body, name="ssd_bwd", grid=(nc,),
        in_specs=[pl.BlockSpec((CHUNK, SSD_INNER), rev),
                  pl.BlockSpec((CHUNK, SSD_INNER), lambda i: (nc - 1 - i, 2)),
                  pl.BlockSpec((CHUNK, SSD_CONV_CH), lambda i: (nc - 1 - i, 2)),
                  pl.BlockSpec((CHUNK, LANE), rev),
                  pl.BlockSpec((CHUNK, SSD_CONV_CH), rev),
                  pl.BlockSpec((CHUNK, SSD_INNER), rev),
                  pl.BlockSpec((1, SSD_STATE, SSD_INNER), lambda i: (nc - 1 - i, 0, 0)),
                  pl.BlockSpec((4, SSD_CONV_CH), c0), pl.BlockSpec((1, LANE), c0), pl.BlockSpec((1, LANE), c0),
                  pl.BlockSpec((1, SSD_INNER), c0), pl.BlockSpec((1, SSD_INNER), c0),
                  pl.BlockSpec((LANE, SSD_INNER), c0)],
        out_specs=[pl.BlockSpec((CHUNK, 2560), rev), pl.BlockSpec((CHUNK, LANE), rev),
                   pl.BlockSpec((4, SSD_CONV_CH), c0), pl.BlockSpec((1, SSD_CONV_CH), c0),
                   pl.BlockSpec((1, LANE), c0), pl.BlockSpec((1, LANE), c0),
                   pl.BlockSpec((1, SSD_INNER), c0), pl.BlockSpec((1, SSD_INNER), c0)],
        out_shape=[jax.ShapeDtypeStruct((T, 2560), BF16), jax.ShapeDtypeStruct((T, LANE), BF16),
                   jax.ShapeDtypeStruct((4, SSD_CONV_CH), F32), jax.ShapeDtypeStruct((1, SSD_CONV_CH), F32),
                   jax.ShapeDtypeStruct((1, LANE), F32), jax.ShapeDtypeStruct((1, LANE), F32),
                   jax.ShapeDtypeStruct((1, SSD_INNER), F32), jax.ShapeDtypeStruct((1, SSD_INNER), F32)],
        scratch_shapes=[pltpu.VMEM((SSD_STATE, SSD_INNER), F32), pltpu.VMEM((8, SSD_CONV_CH), F32),
                        pltpu.VMEM((CHUNK, SSD_INNER), F32), pltpu.VMEM((CHUNK, SSD_CONV_CH), F32)],
        compiler_params=_params(1),
    )(dyn, proj, proj, dtp, xc, y, sprev, conv_w, dt_bias, a_log, d_e, g_ssd, expand)


def _outproj_fwd(x, y_lru, y_ssd, w_out, g_pm, g_pf):
    T = x.shape[0]

    def body(x_ref, yl_ref, ys_ref, wo_ref, gpm_ref, gpf_ref, mix_ref, x1_ref, h2_ref):
        mix = _dot(yl_ref[...], wo_ref[0:LRU_W, :]) + _dot(ys_ref[...], wo_ref[LRU_W:2 * LRU_W, :])
        mix_ref[...] = mix
        x1 = x_ref[...] + mix * _rms(mix) * gpm_ref[...]
        x1_ref[...] = x1
        h2_ref[...] = (x1 * _rms(x1) * gpf_ref[...]).astype(BF16)

    tile = pl.BlockSpec((TT, D_MODEL), lambda i: (i, 0))
    vec = pl.BlockSpec((1, D_MODEL), lambda i: (0, 0))
    return pl.pallas_call(
        body, name="outproj_fwd", grid=(T // TT,),
        in_specs=[tile, tile, tile, pl.BlockSpec((2 * LRU_W, D_MODEL), lambda i: (0, 0)), vec, vec],
        out_specs=[tile, tile, tile],
        out_shape=[jax.ShapeDtypeStruct((T, D_MODEL), F32), jax.ShapeDtypeStruct((T, D_MODEL), F32),
                   jax.ShapeDtypeStruct((T, D_MODEL), BF16)],
        compiler_params=_params(1),
    )(x, y_lru, y_ssd, w_out, g_pm, g_pf)


def _ffn_fwd_bwd(x1, h2, target, w_gate, w_up, w_down, g_pf, g_ff):
    T = x1.shape[0]

    def body(x1_ref, h2_ref, tg_ref, wg_hbm, wu_hbm, wd_hbm, gpf_ref, gff_ref,
             dx1_ref, act_ref, df_ref, dgt_ref, dup_ref, dgpf_ref, dgff_ref, loss_ref,
             wg, wu, wd, gt_s, up_s, sem):
        @pl.when(pl.program_id(0) == 0)
        def _():
            cps = [pltpu.make_async_copy(s, d, sem.at[n]) for n, (s, d) in
                   enumerate(((wg_hbm, wg), (wu_hbm, wu), (wd_hbm, wd)))]
            for c in cps:
                c.start()
            for c in cps:
                c.wait()
            for ref in (dgpf_ref, dgff_ref, loss_ref):
                ref[...] = jnp.zeros_like(ref)

        h2 = h2_ref[...]
        f = jnp.zeros((TT, D_MODEL), F32)
        for c in range(2):
            sl = slice(FF_HALF * c, FF_HALF * (c + 1))
            gt = _dot(h2, wg[:, sl])
            up = _dot(h2, wu[:, sl])
            gt_s[:, sl] = gt
            up_s[:, sl] = up
            act = (gt * _sigmoid(gt) * up).astype(BF16)
            act_ref[:, sl] = act
            f += _dot(act, wd[sl, :])
        x1 = x1_ref[...]
        rnf = _rms(f)
        e = x1 + f * rnf * gff_ref[...] - tg_ref[...]
        part = 0.5 * jnp.sum(jnp.sum(e * e, axis=1, keepdims=True), axis=0, keepdims=True) * (1.0 / D_MODEL)
        lane = lax.broadcasted_iota(jnp.int32, (1, LANE), 1)
        loss_ref[...] += jnp.where(lane == 0, part, 0.0)
        dx2 = e * (1.0 / D_MODEL)
        dgff_ref[...] += jnp.sum(dx2 * f * rnf, axis=0, keepdims=True)
        df = _rms_bwd(dx2 * gff_ref[...], f, rnf).astype(BF16)
        df_ref[...] = df
        dh2 = jnp.zeros((TT, D_MODEL), F32)
        for c in range(2):
            sl = slice(FF_HALF * c, FF_HALF * (c + 1))
            dact = _dot(df, wd[sl, :], NT)
            gt = gt_s[:, sl]
            up = up_s[:, sl]
            sg = _sigmoid(gt)
            dgt = (dact * up * (sg * (1.0 + gt * (1.0 - sg)))).astype(BF16)
            dup = (dact * gt * sg).astype(BF16)
            dgt_ref[:, sl] = dgt
            dup_ref[:, sl] = dup
            dh2 += _dot(dgt, wg[:, sl], NT) + _dot(dup, wu[:, sl], NT)
        rn2 = _rms(x1)
        dgpf_ref[...] += jnp.sum(dh2 * x1 * rn2, axis=0, keepdims=True)
        dx1_ref[...] = dx2 + _rms_bwd(dh2 * gpf_ref[...], x1, rn2)

    tile = pl.BlockSpec((TT, D_MODEL), lambda i: (i, 0))
    wide = pl.BlockSpec((TT, D_FF), lambda i: (i, 0))
    vec = pl.BlockSpec((1, D_MODEL), lambda i: (0, 0))
    hbm = pl.BlockSpec(memory_space=pl.ANY)
    return pl.pallas_call(
        body, name="ffn_fwd_bwd", grid=(T // TT,),
        in_specs=[tile, tile, tile, hbm, hbm, hbm, vec, vec],
        out_specs=[tile, wide, tile, wide, wide, vec, vec, pl.BlockSpec((1, LANE), lambda i: (0, 0))],
        out_shape=[jax.ShapeDtypeStruct((T, D_MODEL), F32), jax.ShapeDtypeStruct((T, D_FF), BF16),
                   jax.ShapeDtypeStruct((T, D_MODEL), BF16), jax.ShapeDtypeStruct((T, D_FF), BF16),
                   jax.ShapeDtypeStruct((T, D_FF), BF16), jax.ShapeDtypeStruct((1, D_MODEL), F32),
                   jax.ShapeDtypeStruct((1, D_MODEL), F32), jax.ShapeDtypeStruct((1, LANE), F32)],
        scratch_shapes=[pltpu.VMEM((D_MODEL, D_FF), BF16), pltpu.VMEM((D_MODEL, D_FF), BF16),
                        pltpu.VMEM((D_FF, D_MODEL), BF16), pltpu.VMEM((TT, D_FF), F32),
                        pltpu.VMEM((TT, D_FF), F32), pltpu.SemaphoreType.DMA((3,))],
        compiler_params=_params(1),
    )(x1, h2, target, w_gate, w_up, w_down, g_pf, g_ff)


def _outproj_bwd(dx1, mix, w_out, g_pm):
    T = dx1.shape[0]

    def body(dx1_ref, mix_ref, wo_ref, gpm_ref, dyl_ref, dys_ref, dmix_ref, dg_ref):
        @pl.when(pl.program_id(0) == 0)
        def _():
            dg_ref[...] = jnp.zeros_like(dg_ref)

        mix = mix_ref[...]
        rn = _rms(mix)
        dx1v = dx1_ref[...]
        dg_ref[...] += jnp.sum(dx1v * mix * rn, axis=0, keepdims=True)
        dmix = _rms_bwd(dx1v * gpm_ref[...], mix, rn).astype(BF16)
        dmix_ref[...] = dmix
        dyl_ref[...] = _dot(dmix, wo_ref[0:LRU_W, :], NT)
        dys_ref[...] = _dot(dmix, wo_ref[LRU_W:2 * LRU_W, :], NT)

    tile = pl.BlockSpec((TT, D_MODEL), lambda i: (i, 0))
    vec = pl.BlockSpec((1, D_MODEL), lambda i: (0, 0))
    return pl.pallas_call(
        body, name="outproj_bwd", grid=(T // TT,),
        in_specs=[tile, tile, pl.BlockSpec((2 * LRU_W, D_MODEL), lambda i: (0, 0)), vec],
        out_specs=[tile, tile, tile, vec],
        out_shape=[jax.ShapeDtypeStruct((T, D_MODEL), F32), jax.ShapeDtypeStruct((T, D_MODEL), F32),
                   jax.ShapeDtypeStruct((T, D_MODEL), BF16), jax.ShapeDtypeStruct((1, D_MODEL), F32)],
        compiler_params=_params(1),
    )(dx1, mix, w_out, g_pm)


def _tn_matmul(a, bs, name, tk=512):
    T, M = a.shape
    nk = T // tk
    nb = len(bs)

    def body(*refs):
        a_ref, b_refs, o_refs, accs = refs[0], refs[1:1 + nb], refs[1 + nb:1 + 2 * nb], refs[1 + 2 * nb:]
        k = pl.program_id(0)

        @pl.when(k == 0)
        def _():
            for acc in accs:
                acc[...] = jnp.zeros_like(acc)

        av = a_ref[...].astype(BF16)
        for b_ref, acc in zip(b_refs, accs):
            acc[...] += _dot(av, b_ref[...], TN)

        @pl.when(k == nk - 1)
        def _():
            for o_ref, acc in zip(o_refs, accs):
                o_ref[...] = acc[...].astype(BF16)

    return pl.pallas_call(
        body, name=name, grid=(nk,),
        in_specs=[pl.BlockSpec((tk, M), lambda k: (k, 0))]
                 + [pl.BlockSpec((tk, b.shape[1]), lambda k: (k, 0)) for b in bs],
        out_specs=[pl.BlockSpec((M, b.shape[1]), lambda k: (0, 0)) for b in bs],
        out_shape=[jax.ShapeDtypeStruct((M, b.shape[1]), BF16) for b in bs],
        scratch_shapes=[pltpu.VMEM((M, b.shape[1]), F32) for b in bs],
        compiler_params=_params(1),
    )(a, *bs)


def _tn_blockdiag(a, b1, b2, name, tk=1024):
    T = a.shape[0]
    tk = min(tk, T)

    def body(a_ref, b1_ref, b2_ref, o1_ref, o2_ref):
        @pl.when(pl.program_id(0) == 0)
        def _():
            o1_ref[...] = jnp.zeros_like(o1_ref)
            o2_ref[...] = jnp.zeros_like(o2_ref)

        for j in range(4):
            sl = slice(256 * j, 256 * (j + 1))
            av = a_ref[:, sl].astype(BF16)
            o1_ref[j] += _dot(av, b1_ref[:, sl], TN)
            o2_ref[j] += _dot(av, b2_ref[:, sl], TN)

    blk = pl.BlockSpec((tk, LRU_W), lambda k: (k, 0))
    out = pl.BlockSpec((4, 256, 256), lambda k: (0, 0, 0))
    return pl.pallas_call(
        body, name=name, grid=(T // tk,),
        in_specs=[blk, blk, blk], out_specs=[out, out],
        out_shape=[jax.ShapeDtypeStruct((4, 256, 256), F32)] * 2,
        compiler_params=_params(1),
    )(a, b1, b2)


def _adamw(parts, w, m, v, name, tr):
    P, R, C = parts.shape

    def body(p_ref, w_ref, m_ref, v_ref, g_ref, d_ref, nm_ref, nv_ref):
        g = p_ref[0].astype(F32)
        for k in range(1, P):
            g = g + p_ref[k].astype(F32)
        g_ref[...] = g
        nm = ADAM_B1 * m_ref[...] + (1.0 - ADAM_B1) * g
        nv = ADAM_B2 * v_ref[...] + (1.0 - ADAM_B2) * (g * g)
        nm_ref[...] = nm
        nv_ref[...] = nv
        m_hat = nm / (1.0 - ADAM_B1 ** ADAM_STEP)
        v_hat = nv / (1.0 - ADAM_B2 ** ADAM_STEP)
        d_ref[...] = -ADAM_LR * (m_hat / (jnp.sqrt(v_hat) + ADAM_EPS) + ADAM_WD * w_ref[...])

    blk = pl.BlockSpec((tr, C), lambda i: (i, 0))
    return pl.pallas_call(
        body, name=name, grid=(R // tr,),
        in_specs=[pl.BlockSpec((P, tr, C), lambda i: (0, i, 0)), blk, blk, blk],
        out_specs=[blk, blk, blk, blk],
        out_shape=[jax.ShapeDtypeStruct((R, C), F32)] * 4,
        compiler_params=_params(1),
    )(parts, w, m, v)


def _peer(k):
    x, y, c = lax.axis_index("x"), lax.axis_index("y"), lax.axis_index("c")
    px = x ^ ((k >> 2) & 1)
    py = y ^ ((k >> 1) & 1)
    pc = c ^ (k & 1)
    return (px, py, pc), 4 * px + 2 * py + pc


def _my_block():
    return 4 * lax.axis_index("x") + 2 * lax.axis_index("y") + lax.axis_index("c")


def _all_gather(shards, name):
    n = len(shards)

    def body(*refs):
        ins, outs = refs[:n], refs[n:2 * n]
        send, recv, loc = refs[2 * n:]
        x, y, c = lax.axis_index("x"), lax.axis_index("y"), lax.axis_index("c")
        sibling = (x, y, 1 - c)
        chips = [(1 - x, y), (x, 1 - y), (1 - x, 1 - y)]
        slot = lambda px, py, pc: 4 * px + 2 * py + pc

        def copy(a, k, block, to, src=None):
            dst = outs[a].at[slot(*block)]
            return pltpu.make_async_remote_copy(
                src_ref=dst if src is None else src, dst_ref=dst, send_sem=send.at[a, k], recv_sem=recv.at[a, k],
                device_id=to, device_id_type=pl.DeviceIdType.MESH)

        mine = [pltpu.make_async_copy(ins[a], outs[a].at[slot(x, y, c)], loc.at[a]) for a in range(n)]
        for cp in mine:
            cp.start()
        first = []
        for a in range(n):
            first.append(copy(a, 0, (x, y, c), sibling, src=ins[a]))
            first += [copy(a, 1 + j, (x, y, c), (*chip, c), src=ins[a]) for j, chip in enumerate(chips)]
        for cp in first:
            cp.start()
        passed = []
        for j, chip in enumerate(chips):
            for a in range(n):
                copy(a, 1 + j, (*chip, c), (x, y, c)).wait_recv()
                fwd = copy(a, 4 + j, (*chip, c), sibling)
                fwd.start()
                passed.append(fwd)
        for a in range(n):
            copy(a, 0, sibling, (x, y, c)).wait_recv()
            for j, chip in enumerate(chips):
                copy(a, 4 + j, (*chip, 1 - c), (x, y, c)).wait_recv()
        for cp in first + passed:
            cp.wait_send()
        for cp in mine:
            cp.wait()

    hbm = pl.BlockSpec(memory_space=pl.ANY)
    return pl.pallas_call(
        body, name=name,
        in_specs=[hbm] * n, out_specs=[hbm] * n,
        out_shape=[jax.ShapeDtypeStruct((N_DEV,) + s.shape, s.dtype) for s in shards],
        scratch_shapes=[pltpu.SemaphoreType.DMA((n, N_DEV - 1)), pltpu.SemaphoreType.DMA((n, N_DEV - 1)),
                        pltpu.SemaphoreType.DMA((n,))],
    )(*shards)


_HBM = pl.BlockSpec(memory_space=pltpu.HBM)
_SEM = pl.BlockSpec(memory_space=pltpu.SEMAPHORE)
_EFFECT = pltpu.SideEffectType.DATAFLOW_SIDE_EFFECTING


def _direct_copies(srcs, lands, send, recv, slab_source):
    me = _my_block()
    cps = []
    for k in range(1, N_DEV):
        to, blk = _peer(k)
        for a, (src, land) in enumerate(zip(srcs, lands)):
            if slab_source:
                s, d = src.at[blk], land.at[me]
            elif land.ndim == 3:
                s, d = src, land.at[me]
            else:
                s, d = src, land.at[pl.ds(pl.multiple_of(me * src.shape[0], 16), src.shape[0]), :]
            cps.append(pltpu.make_async_remote_copy(
                src_ref=s, dst_ref=d,
                send_sem=send.at[a * (N_DEV - 1) + k - 1], recv_sem=recv.at[a * (N_DEV - 1) + k - 1],
                device_id=to, device_id_type=pl.DeviceIdType.MESH))
    return cps


def _exchange_start(srcs, name, slab_source, axes=None):
    n = len(srcs)
    if slab_source:
        shapes = [s.shape for s in srcs]
    else:
        shapes = [(N_DEV,) + s.shape if ax == 1 else (N_DEV * s.shape[0], s.shape[1]) for s, ax in zip(srcs, axes)]
    lands = [pltpu.with_memory_space_constraint(lax.empty(sh, s.dtype), pltpu.HBM) for sh, s in zip(shapes, srcs)]

    def body(*refs):
        ins, land_in = refs[:n], refs[n:2 * n]
        send, recv = refs[2 * n], refs[2 * n + 1]
        token = refs[4 * n + 2]
        for cp in _direct_copies(ins, land_in, send, recv, slab_source):
            cp.start()
        token[...] = jnp.zeros_like(token)

    sems = pltpu.SemaphoreType.DMA((n * (N_DEV - 1),))
    res = pl.pallas_call(
        body, name=name,
        out_shape=(sems, sems, *[pltpu.HBM(s.shape, s.dtype) for s in srcs],
                   *[pltpu.HBM(l.shape, l.dtype) for l in lands], jax.ShapeDtypeStruct((8, LANE), F32)),
        in_specs=[_HBM] * (2 * n),
        out_specs=(_SEM, _SEM, *[_HBM] * (2 * n), pl.BlockSpec(memory_space=pltpu.VMEM)),
        input_output_aliases={i: 2 + i for i in range(2 * n)},
        compiler_params=pltpu.CompilerParams(has_side_effects=_EFFECT),
    )(*[pltpu.with_memory_space_constraint(s, pltpu.HBM) for s in srcs], *lands)
    return dict(send=res[0], recv=res[1], srcs=res[2:2 + n], lands=res[2 + n:2 + 2 * n], token=res[-1],
                slab_source=slab_source)


def _exchange_wait(ex, after, name):
    n = len(ex["srcs"])
    slab_source = ex["slab_source"]

    def body(*refs):
        ins, lands = refs[:n], refs[n:2 * n]
        send, recv = refs[2 * n], refs[2 * n + 1]
        for cp in _direct_copies(ins, lands, send, recv, slab_source):
            cp.wait_send()
            cp.wait_recv()

    res = pl.pallas_call(
        body, name=name,
        out_shape=tuple(pltpu.HBM(s.shape, s.dtype) for s in list(ex["srcs"]) + list(ex["lands"])),
        in_specs=[_HBM] * (2 * n) + [_SEM, _SEM, pl.BlockSpec(memory_space=pl.ANY)],
        out_specs=tuple([_HBM] * (2 * n)),
        input_output_aliases={i: i for i in range(2 * n)},
        compiler_params=pltpu.CompilerParams(has_side_effects=_EFFECT),
    )(*ex["srcs"], *ex["lands"], ex["send"], ex["recv"], after)
    me = _my_block()
    out = []
    for src, land in zip(res[:n], res[n:]):
        if slab_source:
            own, at = lax.dynamic_index_in_dim(src, me, 0, keepdims=True), (me, 0, 0)
        elif land.ndim == 3:
            own, at = src[None], (me, 0, 0)
        else:
            own, at = src, (me * src.shape[0], 0)
        out.append(lax.dynamic_update_slice(land, own, at))
    return out


BIG = ("w_in", "w_out", "w_gate", "w_up", "w_down")
BIG_SHARD = {"w_in": (1024, 578), "w_out": (256, 1024), "w_gate": (1024, 352), "w_up": (1024, 352),
             "w_down": (352, 1024)}
BIG_SHARD_AXIS = {"w_in": 1, "w_out": 0, "w_gate": 1, "w_up": 1, "w_down": 0}
BIG_ADAM_ROWS = {"w_in": 256, "w_out": 128, "w_gate": 256, "w_up": 256, "w_down": 176}


def _join(parts, axis):
    if axis == 0:
        return parts.reshape((-1,) + parts.shape[2:])
    return jnp.concatenate([parts[j] for j in range(N_DEV)], axis=1)


def _split(full, axis):
    if axis == 0:
        return full.reshape((N_DEV, full.shape[0] // N_DEV) + full.shape[1:])
    c = full.shape[1] // N_DEV
    return jnp.stack([full[:, c * j:c * (j + 1)] for j in range(N_DEV)])


SMALL = (("pre_mix_norm", 1024), ("lru_conv_w", 4096), ("lru_conv_b", 1024), ("lru_wa", 65536), ("lru_ba", 1024),
         ("lru_wx", 65536), ("lru_bx", 1024), ("lru_lambda", 1024), ("lru_out_norm", 1024), ("ssd_conv_w", 6144),
         ("ssd_conv_b", 1536), ("ssd_dt_bias", 16), ("ssd_a_log", 16), ("ssd_d", 16), ("ssd_out_norm", 1024),
         ("post_mix_norm", 1024), ("pre_ffn_norm", 1024), ("post_ffn_norm", 1024), ("loss", 1))
REPLICATED = tuple(n for n, _ in SMALL if n not in ("lru_conv_w", "ssd_conv_w", "loss"))


def _pack_small(d):
    flat = jnp.concatenate([d[n].astype(F32).reshape(-1) for n, _ in SMALL])
    return jnp.pad(flat, (0, SMALL_ROWS * 1024 - flat.shape[0])).reshape(SMALL_ROWS, 1024)


def _unpack_small(p):
    flat = p.reshape(-1)
    out, off = {}, 0
    for n, size in SMALL:
        out[n] = flat[off:off + size]
        off += size
    return out


def _blockdiag4(w):
    z = jnp.zeros((4, 4, 64, 4, 64), w.dtype)
    w4 = w.reshape(4, 4, 64, 64)
    for b in range(4):
        z = z.at[:, b, :, b, :].set(w4[:, b])
    return z.reshape(4, 256, 256)


def _diag_blocks(g):
    g5 = g.reshape(4, 4, 64, 4, 64)
    return jnp.stack([g5[:, b, :, b, :] for b in range(4)], axis=1).reshape(16, 64, 64)


def _local_step(x, target, w_in, P, rest_weights, emit, emit_small, start_token=None):
    w_main = w_in[:, :IN_MAIN]
    w_dt = jnp.pad(w_in[:, IN_MAIN:], ((0, 0), (0, LANE - SSD_HEADS)))
    pad16 = lambda v: jnp.pad(v.reshape(1, SSD_HEADS), ((0, 0), (0, LANE - SSD_HEADS)))
    dt_bias, a_log = pad16(P["ssd_dt_bias"]), pad16(P["ssd_a_log"])
    d_e = jnp.repeat(P["ssd_d"].reshape(SSD_HEADS), SSD_HEAD_DIM).reshape(1, SSD_INNER)
    expand = (jnp.arange(LANE)[:, None] == (jnp.arange(SSD_INNER)[None, :] // SSD_HEAD_DIM)).astype(BF16)
    wa_bd = _blockdiag4(P["lru_wa"].astype(BF16))
    wx_bd = _blockdiag4(P["lru_wx"].astype(BF16))
    vec = lambda n: P[n].reshape(1, -1)

    after = lambda v, tok: v if tok is None else v + tok[0:1, 0:1]

    h, proj, dtp = _inproj_fwd(x, after(vec("pre_mix_norm"), start_token), w_main, w_dt)
    lx, hl, y_lru, *gates = _lru_fwd(proj, P["lru_conv_w"], vec("lru_conv_b"), wa_bd, wx_bd, vec("lru_ba"),
                                     vec("lru_bx"), vec("lru_lambda"), vec("lru_out_norm"))
    xc, y, y_ssd, sprev = _ssd_fwd(proj, dtp, P["ssd_conv_w"], vec("ssd_conv_b"), dt_bias, a_log, d_e,
                                   vec("ssd_out_norm"), expand)
    W = rest_weights(y_ssd)
    mix, x1, h2 = _outproj_fwd(x, y_lru, y_ssd, W["w_out"], vec("post_mix_norm"), vec("pre_ffn_norm"))
    dx1, act, df, dgt, dup, dg_pf, dg_ff, loss = _ffn_fwd_bwd(
        x1, h2, target, W["w_gate"], W["w_up"], W["w_down"], vec("pre_ffn_norm"), vec("post_ffn_norm"))
    tok = emit("ffn", {"w_gate": _tn_matmul(h2, [dgt], "dw_gate")[0], "w_up": _tn_matmul(h2, [dup], "dw_up")[0],
                       "w_down": _tn_matmul(act, [df], "dw_down")[0]})
    dy_lru, dy_ssd, dmix, dg_pm = _outproj_bwd(dx1, mix, W["w_out"], after(vec("post_mix_norm"), tok))
    tok = emit("out", {"w_out": jnp.concatenate([_tn_matmul(y_lru, [dmix], "dw_out_lru")[0],
                                                 _tn_matmul(y_ssd, [dmix], "dw_out_ssd")[0]], axis=0)})
    dp_ssd, ddtp, dcw_s, dcb_s, dbias, dA, dD_e, dg_ssd = _ssd_bwd(
        dy_ssd, proj, dtp, xc, y, sprev, P["ssd_conv_w"], dt_bias, a_log, d_e,
        after(vec("ssd_out_norm"), tok), expand)
    dp_lru, dpa, dpx, dcw_l, dcb_l, dba, dbx, dlam, dg_lru = _lru_bwd(
        dy_lru, proj, lx, hl, gates, P["lru_conv_w"], wa_bd, wx_bd, vec("lru_lambda"), vec("lru_out_norm"))
    dw_in = _tn_matmul(h, [dp_lru, dp_ssd, ddtp], "dw_in")
    tok = emit("in", {"w_in": jnp.concatenate([dw_in[0], dw_in[1], dw_in[2][:, :SSD_HEADS]], axis=1)})

    a_neg = -jnp.exp(P["ssd_a_log"].reshape(SSD_HEADS))
    dwa, dwx = _tn_blockdiag(lx, dpa, dpx, "dw_lru_gates")
    small = {
        "pre_mix_norm": jnp.zeros((1, D_MODEL), F32), "lru_conv_w": dcw_l, "lru_conv_b": dcb_l,
        "lru_wa": _diag_blocks(dwa), "lru_ba": dba,
        "lru_wx": _diag_blocks(dwx), "lru_bx": dbx,
        "lru_lambda": dlam, "lru_out_norm": dg_lru, "ssd_conv_w": dcw_s, "ssd_conv_b": dcb_s,
        "ssd_dt_bias": dbias[0, :SSD_HEADS], "ssd_a_log": dA[0, :SSD_HEADS] * a_neg,
        "ssd_d": jnp.sum(dD_e.reshape(SSD_HEADS, SSD_HEAD_DIM), axis=1), "ssd_out_norm": dg_ssd,
        "post_mix_norm": dg_pm, "pre_ffn_norm": dg_pf, "post_ffn_norm": dg_ff, "loss": loss[0, 0:1],
    }
    tok = after(after(vec("pre_mix_norm"), tok), emit_small(small))
    grad_x, dg_pre = _inproj_bwd(dp_lru, dp_ssd, ddtp, dx1, x, tok, w_main, w_dt)
    return grad_x, dg_pre


def kernel(x, pre_mix_norm, w_in, lru_conv_w, lru_conv_b, lru_wa, lru_ba, lru_wx, lru_bx, lru_lambda, lru_out_norm, ssd_conv_w, ssd_conv_b, ssd_dt_bias, ssd_a_log, ssd_d, ssd_out_norm, w_out, post_mix_norm, pre_ffn_norm, w_gate, w_up, w_down, post_ffn_norm, loss_target, m_pre_mix_norm, m_w_in, m_lru_conv_w, m_lru_conv_b, m_lru_wa, m_lru_ba, m_lru_wx, m_lru_bx, m_lru_lambda, m_lru_out_norm, m_ssd_conv_w, m_ssd_conv_b, m_ssd_dt_bias, m_ssd_a_log, m_ssd_d, m_ssd_out_norm, m_w_out, m_post_mix_norm, m_pre_ffn_norm, m_w_gate, m_w_up, m_w_down, m_post_ffn_norm, v_pre_mix_norm, v_w_in, v_lru_conv_w, v_lru_conv_b, v_lru_wa, v_lru_ba, v_lru_wx, v_lru_bx, v_lru_lambda, v_lru_out_norm, v_ssd_conv_w, v_ssd_conv_b, v_ssd_dt_bias, v_ssd_a_log, v_ssd_d, v_ssd_out_norm, v_w_out, v_post_mix_norm, v_pre_ffn_norm, v_w_gate, v_w_up, v_w_down, v_post_ffn_norm):
    a = dict(locals())
    names = [n for n, _ in SMALL if n != "loss"] + list(BIG)
    w = {n: a[n][0] for n in names}
    m = {n: a["m_" + n][0] for n in names}
    v = {n: a["v_" + n][0] for n in names}

    cpack = jnp.concatenate([w["lru_conv_w"], w["ssd_conv_w"], jnp.zeros((4, 64), F32)], axis=1)
    cpack = jnp.pad(cpack, ((0, 4), (0, 0)))
    g_in, cg = _all_gather([w["w_in"].astype(BF16), cpack], "all_gather_w_in")
    P = {n: w[n] for n in REPLICATED}
    P["lru_conv_w"] = _join(cg[:, 0:4, 0:128], 1)
    P["ssd_conv_w"] = _join(cg[:, 0:4, 128:320], 1)

    rest = [n for n in BIG if n != "w_in"]
    zero = jnp.minimum(jnp.abs(cg[0, 0, 0]), 0.0)
    ex_w = _exchange_start([(w[n] + zero).astype(BF16) for n in rest], "weights_start", slab_source=False,
                           axes=[BIG_SHARD_AXIS[n] for n in rest])

    def rest_weights(after):
        lands = _exchange_wait(ex_w, after, "weights_wait")
        return {n: _join(p, 1) if p.ndim == 3 else p for n, p in zip(rest, lands)}

    pending = []

    def emit(group, grads):
        ex = _exchange_start([_split(g, BIG_SHARD_AXIS[n]) for n, g in grads.items()], "grads_start_" + group,
                             slab_source=True)
        pending.append((group, list(grads), ex))
        return ex["token"]

    def emit_small(small):
        ex = _exchange_start([_pack_small(small)], "small_start", slab_source=False, axes=[0])
        pending.append(("small", None, ex))
        return ex["token"]

    grad_x, dg_pre = _local_step(x[0], loss_target[0], _join(g_in, 1), P, rest_weights, emit, emit_small,
                                 ex_w["token"])

    got_pre = _all_gather([jnp.pad(dg_pre, ((0, 7), (0, 0)))], "all_gather_pre_mix_norm")[0]
    ex_small = pending.pop()[2]
    got_small = _exchange_wait(ex_small, got_pre, "small_wait")[0].reshape(N_DEV, SMALL_ROWS, 1024)
    got_small = got_small.at[:, 0:1, :].set(got_pre[:, 0:1, :])

    outs = {}
    done = got_small
    for group, group_names, ex in pending:
        for n, parts in zip(group_names, _exchange_wait(ex, done, "grads_wait_" + group)):
            res = _adamw(parts, w[n], m[n], v[n], "adamw_" + n, BIG_ADAM_ROWS[n])
            done = res[0]
            for kind, r in zip(("grad", "delta", "new_m", "new_v"), res):
                outs[kind + "_" + n] = r

    zero_extra = {"loss": jnp.zeros((1,), F32)}
    full_conv = {"lru_conv_w": jnp.zeros((4, LRU_W), F32), "ssd_conv_w": jnp.zeros((4, SSD_CONV_CH), F32)}
    rep = lambda d: _pack_small({**{n: d[n] for n in REPLICATED}, **full_conv, **zero_extra})
    res = _adamw(got_small, rep(w), rep(m), rep(v), "adamw_small", SMALL_ROWS)
    g_small = _unpack_small(res[0])
    for kind, r in zip(("grad", "delta", "new_m", "new_v"), res):
        for n, val in _unpack_small(r).items():
            if n in REPLICATED:
                outs[kind + "_" + n] = val.reshape(w[n].shape)

    me = _my_block()
    gl = lax.dynamic_slice(g_small["lru_conv_w"].reshape(4, LRU_W), (0, me * 128), (4, 128))
    gs = lax.dynamic_slice(g_small["ssd_conv_w"].reshape(4, SSD_CONV_CH), (0, me * 192), (4, 192))
    cat = lambda d: jnp.pad(jnp.concatenate([d["lru_conv_w"], d["ssd_conv_w"]], axis=1), ((0, 4), (0, 64)))
    res = _adamw(cat({"lru_conv_w": gl, "ssd_conv_w": gs})[None], cat(w), cat(m), cat(v), "adamw_conv", 8)
    for kind, r in zip(("grad", "delta", "new_m", "new_v"), res):
        outs[kind + "_lru_conv_w"] = r[0:4, 0:128]
        outs[kind + "_ssd_conv_w"] = r[0:4, 128:320]

    order = ["pre_mix_norm", "w_in", "lru_conv_w", "lru_conv_b", "lru_wa", "lru_ba", "lru_wx", "lru_bx", "lru_lambda",
             "lru_out_norm", "ssd_conv_w", "ssd_conv_b", "ssd_dt_bias", "ssd_a_log", "ssd_d", "ssd_out_norm", "w_out",
             "post_mix_norm", "pre_ffn_norm", "w_gate", "w_up", "w_down", "post_ffn_norm"]
    result = [g_small["loss"].reshape(()), grad_x[None]]
    for kind in ("grad", "delta", "new_m", "new_v"):
        result += [outs[kind + "_" + n][None] for n in order]
    return tuple(result)
```

```python
import functools

import jax
import jax.numpy as jnp
from jax import lax
from jax.experimental import pallas as pl
from jax.experimental.pallas import tpu as pltpu

F32 = jnp.float32
BF16 = jnp.bfloat16
HI = lax.Precision.HIGHEST
EPS = 1e-6
N_DEV = 8
D_MODEL = 1024
LRU_W = 1024
SSD_INNER = 1024
SSD_HEADS = 16
SSD_HEAD_DIM = 64
SSD_STATE = 128
SSD_GROUPS = 2
SSD_CONV_CH = 1536
CHUNK = 128
D_FF = 2816
FF_CHUNK = 2816
IN_MAIN = 4608
IN_COLS = 4624
LANE = 128
TT = 256
TP = 512
VMEM_LIMIT = 56 * 1024 * 1024
ADAM_LR, ADAM_B1, ADAM_B2, ADAM_EPS, ADAM_WD, ADAM_STEP = 0.001, 0.9, 0.999, 1e-08, 0.01, 10
PACK_ROWS = 1920
SMALL_ROWS = 160

NT = (((1,), (1,)), ((), ()))
TN = (((0,), (0,)), ((), ()))


def _params(n_grid):
    return pltpu.CompilerParams(dimension_semantics=("arbitrary",) * n_grid, vmem_limit_bytes=VMEM_LIMIT)


def _dot(a, b, dims=None, precision=None):
    if dims is None:
        return jnp.dot(a, b, preferred_element_type=F32, precision=precision)
    return lax.dot_general(a, b, dims, preferred_element_type=F32, precision=precision)


def _split_bf16(x, terms):
    out = []
    for _ in range(terms - 1):
        p = x.astype(BF16)
        out.append(p)
        x = x - p.astype(F32)
    return out + [x.astype(BF16)]


def _dot_sel(x, sel, dims=None, terms=2, sel_first=False):
    parts = [_dot(sel, p, dims) if sel_first else _dot(p, sel, dims) for p in _split_bf16(x, terms)]
    return functools.reduce(lambda a, b: a + b, parts)


def _sigmoid(x):
    return 0.5 * jnp.tanh(0.5 * x) + 0.5


def _softplus(x):
    e = jnp.exp(-jnp.abs(x))
    l1p = jnp.where(e < 1e-3, e * (1.0 - e * (0.5 - e * (1.0 / 3.0))), jnp.log(1.0 + e))
    return jnp.maximum(x, 0.0) + l1p


def _neg_expm1(x):
    series = -x * (1.0 + x * (0.5 + x * (1.0 / 6.0 + x * (1.0 / 24.0))))
    return jnp.where(x > -0.01, series, 1.0 - jnp.exp(x))


_GELU_C = 0.7978845608028654


def _gelu(x):
    t = jnp.tanh(_GELU_C * (x + 0.044715 * x * x * x))
    return 0.5 * x * (1.0 + t), t


def _gelu_grad(x, t):
    return 0.5 * (1.0 + t) + 0.5 * x * (1.0 - t * t) * _GELU_C * (1.0 + 3.0 * 0.044715 * x * x)


def _rms(x):
    return lax.rsqrt(jnp.mean(x * x, axis=-1, keepdims=True) + EPS)


def _rms_bwd(dyn, x, rn):
    return rn * dyn - x * (rn * rn * rn) * jnp.mean(dyn * x, axis=-1, keepdims=True)


def _row(x, r):
    idx = lax.broadcasted_iota(jnp.int32, x.shape, 0)
    return jnp.sum(jnp.where(idx == r, x, 0.0), axis=0, keepdims=True)


def _shift_down(cur, prev8, j):
    s = pltpu.roll(cur, j, 0)
    p = pltpu.roll(prev8, j, 0)
    r8 = lax.broadcasted_iota(jnp.int32, prev8.shape, 0)
    top = jnp.where(r8 < j, p, s[0:8])
    return jnp.concatenate([top, s[8:]], axis=0)


def _shift_up(cur, next8, j):
    n = cur.shape[0]
    s = pltpu.roll(cur, n - j, 0)
    p = pltpu.roll(next8, 8 - j, 0)
    r8 = lax.broadcasted_iota(jnp.int32, next8.shape, 0)
    bot = jnp.where(r8 >= 8 - j, p, s[n - 8:n])
    return jnp.concatenate([s[:n - 8], bot], axis=0)


def _scan_fwd(a, u):
    n = a.shape[0]
    row = lax.broadcasted_iota(jnp.int32, a.shape, 0)
    k = 1
    while k < n:
        ok = row >= k
        a_s = jnp.where(ok, pltpu.roll(a, k, 0), 1.0)
        u_s = jnp.where(ok, pltpu.roll(u, k, 0), 0.0)
        u = a * u_s + u
        a = a * a_s
        k *= 2
    return a, u


def _scan_bwd(b, d):
    n = b.shape[0]
    row = lax.broadcasted_iota(jnp.int32, b.shape, 0)
    k = 1
    while k < n:
        ok = row < n - k
        b_s = jnp.where(ok, pltpu.roll(b, n - k, 0), 1.0)
        d_s = jnp.where(ok, pltpu.roll(d, n - k, 0), 0.0)
        d = b * d_s + d
        b = b * b_s
        k *= 2
    return b, d


def _scan_tile(a, u, carry, a_s, u_s, reverse):
    n, c = a.shape
    groups = n // 8
    r8 = lax.broadcasted_iota(jnp.int32, a.shape, 0) & 7
    in_group = lambda x, k: pltpu.roll(x.reshape(groups, 8, c), k, 1).reshape(n, c)
    for k in (1, 2, 4):
        ok = (r8 < 8 - k) if reverse else (r8 >= k)
        shift = 8 - k if reverse else k
        a_n = jnp.where(ok, in_group(a, shift), 1.0)
        u_n = jnp.where(ok, in_group(u, shift), 0.0)
        u = a * u_n + u
        a = a * a_n
    nl = c // LANE
    for j in range(nl):
        a_s[j] = a[:, LANE * j:LANE * (j + 1)]
        u_s[j] = u[:, LANE * j:LANE * (j + 1)]
    end = 0 if reverse else 7
    ends = lambda ref, j: ref[pl.ds(j, 1), pl.ds(end, groups, stride=8), :].reshape(groups, LANE)
    ga = jnp.concatenate([ends(a_s, j) for j in range(nl)], axis=1)
    gu = jnp.concatenate([ends(u_s, j) for j in range(nl)], axis=1)
    gacc, gh = (_scan_bwd if reverse else _scan_fwd)(ga, gu)
    gh = gh + gacc * carry
    grow = lax.broadcasted_iota(jnp.int32, gh.shape, 0)
    if reverse:
        cin = jnp.where(grow == groups - 1, carry, pltpu.roll(gh, groups - 1, 0))
    else:
        cin = jnp.where(grow == 0, carry, pltpu.roll(gh, 1, 0))
    spread = ((lax.broadcasted_iota(jnp.int32, (n, LANE), 0) >> 3)
              == lax.broadcasted_iota(jnp.int32, (n, LANE), 1)).astype(BF16)
    cin = jnp.concatenate([cin, jnp.zeros((LANE - groups, c), F32)], axis=0)
    return u + a * _dot_sel(cin, spread, terms=3, sel_first=True)


def _inproj_fwd(x, g_pre, w_main, w_dt):
    T = x.shape[0]
    TT = TP

    def body(x_ref, g_ref, wm_hbm, wd_hbm, h_ref, proj_ref, dtp_ref, wm, wd, sem):
        @pl.when(pl.program_id(0) == 0)
        def _():
            c1 = pltpu.make_async_copy(wm_hbm, wm, sem.at[0])
            c2 = pltpu.make_async_copy(wd_hbm, wd, sem.at[1])
            c1.start()
            c2.start()
            c1.wait()
            c2.wait()

        xv = x_ref[...]
        h = (xv * _rms(xv) * g_ref[...]).astype(BF16)
        h_ref[...] = h
        proj_ref[...] = _dot(h, wm[...])
        dtp_ref[...] = _dot(h, wd[...])

    return pl.pallas_call(
        body, name="inproj_fwd", grid=(T // TT,),
        in_specs=[pl.BlockSpec((TT, D_MODEL), lambda i: (i, 0)),
                  pl.BlockSpec((1, D_MODEL), lambda i: (0, 0)),
                  pl.BlockSpec(memory_space=pl.ANY), pl.BlockSpec(memory_space=pl.ANY)],
        out_specs=[pl.BlockSpec((TT, D_MODEL), lambda i: (i, 0)),
                   pl.BlockSpec((TT, IN_MAIN), lambda i: (i, 0)),
                   pl.BlockSpec((TT, LANE), lambda i: (i, 0))],
        out_shape=[jax.ShapeDtypeStruct((T, D_MODEL), BF16), jax.ShapeDtypeStruct((T, IN_MAIN), F32),
                   jax.ShapeDtypeStruct((T, LANE), F32)],
        scratch_shapes=[pltpu.VMEM((D_MODEL, IN_MAIN), BF16), pltpu.VMEM((D_MODEL, LANE), BF16),
                        pltpu.SemaphoreType.DMA((2,))],
        compiler_params=_params(1),
    )(x, g_pre, w_main, w_dt)


def _inproj_bwd(dp_lru, dp_ssd, ddtp, dx1, x, g_pre, w_main, w_dt):
    T = x.shape[0]
    TT = TP

    def body(dl_ref, ds_ref, dd_ref, dx1_ref, x_ref, g_ref, wm_hbm, wd_hbm, gx_ref, dg_ref, wm, wd, sem):
        @pl.when(pl.program_id(0) == 0)
        def _():
            c1 = pltpu.make_async_copy(wm_hbm, wm, sem.at[0])
            c2 = pltpu.make_async_copy(wd_hbm, wd, sem.at[1])
            c1.start()
            c2.start()
            c1.wait()
            c2.wait()
            dg_ref[...] = jnp.zeros_like(dg_ref)

        dh = _dot(dl_ref[...], wm[:, 0:2048], NT)
        dh += _dot(ds_ref[...], wm[:, 2048:IN_MAIN], NT)
        dh += _dot(dd_ref[...], wd[...], NT)
        xv = x_ref[...]
        rn = _rms(xv)
        dg_ref[...] += jnp.sum(dh * xv * rn, axis=0, keepdims=True)
        gx_ref[...] = dx1_ref[...] + _rms_bwd(dh * g_ref[...], xv, rn)

    return pl.pallas_call(
        body, name="inproj_bwd", grid=(T // TT,),
        in_specs=[pl.BlockSpec((TT, 2048), lambda i: (i, 0)),
                  pl.BlockSpec((TT, 2560), lambda i: (i, 0)),
                  pl.BlockSpec((TT, LANE), lambda i: (i, 0)),
                  pl.BlockSpec((TT, D_MODEL), lambda i: (i, 0)),
                  pl.BlockSpec((TT, D_MODEL), lambda i: (i, 0)),
                  pl.BlockSpec((1, D_MODEL), lambda i: (0, 0)),
                  pl.BlockSpec(memory_space=pl.ANY), pl.BlockSpec(memory_space=pl.ANY)],
        out_specs=[pl.BlockSpec((TT, D_MODEL), lambda i: (i, 0)),
                   pl.BlockSpec((1, D_MODEL), lambda i: (0, 0))],
        out_shape=[jax.ShapeDtypeStruct((T, D_MODEL), F32), jax.ShapeDtypeStruct((1, D_MODEL), F32)],
        scratch_shapes=[pltpu.VMEM((D_MODEL, IN_MAIN), BF16), pltpu.VMEM((D_MODEL, LANE), BF16),
                        pltpu.SemaphoreType.DMA((2,))],
        compiler_params=_params(1),
    )(dp_lru, dp_ssd, ddtp, dx1, x, g_pre, w_main, w_dt)


def _lru_gates(lx, wa_ref, wx_ref, ba, bx, lam):
    lxb = lx.astype(BF16)
    pa = jnp.concatenate([_dot(lxb[:, 256 * k:256 * (k + 1)], wa_ref[k]) for k in range(4)], axis=1) + ba
    px = jnp.concatenate([_dot(lxb[:, 256 * k:256 * (k + 1)], wx_ref[k]) for k in range(4)], axis=1) + bx
    r = _sigmoid(pa)
    ig = _sigmoid(px)
    sp = _softplus(-lam)
    log_a = -8.0 * r * sp
    a = jnp.exp(log_a)
    mult = jnp.sqrt(_neg_expm1(2.0 * log_a))
    return r, ig, sp, a, mult


def _lru_fwd(proj, conv_w, conv_b, wa_bd, wx_bd, ba, bx, lam, g_lru):
    T = proj.shape[0]

    def body(cx_ref, gate_ref, cw_ref, cb_ref, wa_ref, wx_ref, ba_ref, bx_ref, lam_ref, g_ref,
             lx_ref, hl_ref, y_ref, r_ref, ig_ref, a_ref, mult_ref, tail, hcar, sa, su):
        @pl.when(pl.program_id(0) == 0)
        def _():
            tail[...] = jnp.zeros_like(tail)
            hcar[...] = jnp.zeros_like(hcar)

        cx = cx_ref[...]
        prev8 = tail[...]
        lx = cb_ref[...] + cw_ref[3:4, :] * cx
        for j in range(1, 4):
            lx += cw_ref[3 - j:4 - j, :] * _shift_down(cx, prev8, j)
        tail[...] = cx[TT - 8:TT]
        lx_ref[...] = lx
        r, ig, sp, a, mult = _lru_gates(lx, wa_ref, wx_ref, ba_ref[...], bx_ref[...], lam_ref[...])
        r_ref[...] = r
        ig_ref[...] = ig
        a_ref[...] = a
        mult_ref[...] = mult
        h = _scan_tile(a, mult * (ig * lx), hcar[...], sa, su, reverse=False)
        hl_ref[...] = h
        hcar[...] = hl_ref[TT - 1:TT, :]
        ge, _ = _gelu(gate_ref[...])
        p = h * ge
        y_ref[...] = (p * _rms(p) * g_ref[...]).astype(BF16)

    vec = pl.BlockSpec((1, LRU_W), lambda i: (0, 0))
    bd = pl.BlockSpec((4, 256, 256), lambda i: (0, 0, 0))
    tile = pl.BlockSpec((TT, LRU_W), lambda i: (i, 0))
    f32 = jax.ShapeDtypeStruct((T, LRU_W), F32)
    return pl.pallas_call(
        body, name="lru_fwd", grid=(T // TT,),
        in_specs=[tile, pl.BlockSpec((TT, LRU_W), lambda i: (i, 1)),
                  pl.BlockSpec((4, LRU_W), lambda i: (0, 0)), vec, bd, bd, vec, vec, vec, vec],
        out_specs=[tile] * 7,
        out_shape=[f32, f32, jax.ShapeDtypeStruct((T, LRU_W), BF16), f32, f32, f32, f32],
        scratch_shapes=[pltpu.VMEM((8, LRU_W), F32), pltpu.VMEM((1, LRU_W), F32)]
                       + [pltpu.VMEM((LRU_W // LANE, TT, LANE), F32)] * 2,
        compiler_params=_params(1),
    )(proj, proj, conv_w, conv_b, wa_bd, wx_bd, ba, bx, lam, g_lru)


def _lru_bwd(dy, proj, lx, hl, gates, conv_w, wa_bd, wx_bd, lam, g_lru):
    T = proj.shape[0]
    nt = T // TT

    def body(dy_ref, cx_ref, gate_ref, lx_ref, hl_ref, halo_ref, r_ref, ig_ref, a_ref, mult_ref, cw_ref, wa_ref,
             wx_ref, lam_ref, g_ref, dp_ref, dpa_ref, dpx_ref, dcw_ref, dcb_ref, dba_ref, dbx_ref, dlam_ref, dg_ref,
             gcar, acar, head, sa, su):
        i = pl.program_id(0)

        @pl.when(i == 0)
        def _():
            gcar[...] = jnp.zeros_like(gcar)
            acar[...] = jnp.zeros_like(acar)
            head[...] = jnp.zeros_like(head)
            for ref in (dcw_ref, dcb_ref, dba_ref, dbx_ref, dlam_ref, dg_ref):
                ref[...] = jnp.zeros_like(ref)

        lx = lx_ref[...]
        h = hl_ref[...]
        gate = gate_ref[...]
        cx = cx_ref[...]
        lam = lam_ref[...]
        r, ig, a, mult = r_ref[...], ig_ref[...], a_ref[...], mult_ref[...]
        sp = _softplus(-lam)
        ge, th = _gelu(gate)
        p = h * ge
        rn = _rms(p)
        dyv = dy_ref[...]
        dg_ref[...] += jnp.sum(dyv * p * rn, axis=0, keepdims=True)
        dp = _rms_bwd(dyv * g_ref[...], p, rn)
        dp_ref[:, LRU_W:2 * LRU_W] = (dp * h * _gelu_grad(gate, th)).astype(BF16)
        dh = dp * ge
        row = lax.broadcasted_iota(jnp.int32, a.shape, 0)
        b = jnp.where(row == TT - 1, acar[...], pltpu.roll(a, TT - 1, 0))
        g = _scan_tile(b, dh, gcar[...], sa, su, reverse=True)
        gcar[...] = _row(g[0:8], 0)
        acar[...] = _row(a[0:8], 0)
        h_last_prev = halo_ref[7:8, :] * (i < nt - 1).astype(F32)
        hprev = jnp.where(row == 0, h_last_prev, pltpu.roll(h, 1, 0))
        da = g * hprev
        dm2 = (g * (ig * lx)) * 0.5 / mult
        dlog_a = da * a - 2.0 * a * a * dm2
        dlam_ref[...] += jnp.sum(-8.0 * r * dlog_a, axis=0, keepdims=True) * (-_sigmoid(-lam))
        dpa = (-8.0 * sp * dlog_a) * r * (1.0 - r)
        dpx = (g * mult * lx) * ig * (1.0 - ig)
        dba_ref[...] += jnp.sum(dpa, axis=0, keepdims=True)
        dbx_ref[...] += jnp.sum(dpx, axis=0, keepdims=True)
        dpab = dpa.astype(BF16)
        dpxb = dpx.astype(BF16)
        dpa_ref[...] = dpab
        dpx_ref[...] = dpxb
        dlx = g * mult * ig + jnp.concatenate(
            [_dot(dpab[:, 256 * k:256 * (k + 1)], wa_ref[k], NT) + _dot(dpxb[:, 256 * k:256 * (k + 1)], wx_ref[k], NT)
             for k in range(4)], axis=1)
        nxt = head[...]
        dcb_ref[...] += jnp.sum(dlx, axis=0, keepdims=True)
        dcx = cw_ref[3:4, :] * dlx
        dcw_ref[3:4, :] += jnp.sum(cx * dlx, axis=0, keepdims=True)
        for j in range(1, 4):
            sh = _shift_up(dlx, nxt, j)
            dcx += cw_ref[3 - j:4 - j, :] * sh
            dcw_ref[3 - j:4 - j, :] += jnp.sum(cx * sh, axis=0, keepdims=True)
        head[...] = dlx[0:8]
        dp_ref[:, 0:LRU_W] = dcx.astype(BF16)

    rev = lambda i: (nt - 1 - i, 0)
    vec = pl.BlockSpec((1, LRU_W), lambda i: (0, 0))
    bd = pl.BlockSpec((4, 256, 256), lambda i: (0, 0, 0))
    tile = pl.BlockSpec((TT, LRU_W), rev)
    halo = pl.BlockSpec((8, LRU_W), lambda i: (jnp.maximum((nt - 1 - i) * (TT // 8) - 1, 0), 0))
    cw = pl.BlockSpec((4, LRU_W), lambda i: (0, 0))
    return pl.pallas_call(
        body, name="lru_bwd", grid=(nt,),
        in_specs=[tile, tile, pl.BlockSpec((TT, LRU_W), lambda i: (nt - 1 - i, 1)), tile, tile, halo,
                  tile, tile, tile, tile, cw, bd, bd, vec, vec],
        out_specs=[pl.BlockSpec((TT, 2 * LRU_W), rev), tile, tile, cw, vec, vec, vec, vec, vec],
        out_shape=[jax.ShapeDtypeStruct((T, 2 * LRU_W), BF16), jax.ShapeDtypeStruct((T, LRU_W), BF16),
                   jax.ShapeDtypeStruct((T, LRU_W), BF16), jax.ShapeDtypeStruct((4, LRU_W), F32)]
                  + [jax.ShapeDtypeStruct((1, LRU_W), F32)] * 5,
        scratch_shapes=[pltpu.VMEM((1, LRU_W), F32), pltpu.VMEM((1, LRU_W), F32), pltpu.VMEM((8, LRU_W), F32)]
                       + [pltpu.VMEM((LRU_W // LANE, TT, LANE), F32)] * 2,
        compiler_params=_params(1),
    )(dy, proj, proj, lx, hl, hl, *gates, conv_w, wa_bd, wx_bd, lam, g_lru)


def _ssd_chunk_terms(xc, dtp, bias, alog, expand):
    sg = _sigmoid(xc)
    xbc = xc * sg
    pre = dtp + bias
    dt = _softplus(pre)
    A = -jnp.exp(alog)
    ri = lax.broadcasted_iota(jnp.int32, (CHUNK, CHUNK), 0)
    ci = lax.broadcasted_iota(jnp.int32, (CHUNK, CHUNK), 1)
    tril = (ri >= ci).astype(BF16)
    cs = _dot_sel(dt * A, tril, terms=3, sel_first=True)
    cs_last = _row(cs, CHUNK - 1)
    ecs = jnp.exp(cs)
    dec = jnp.exp(cs_last - cs)
    return dict(sg=sg, xbc=xbc, pre=pre, dt=dt, A=A, cs=cs, csT=cs.T, ecs=ecs, dec=dec, ri=ri, ci=ci,
                dt_e=_dot_sel(dt, expand), ecs_e=_dot_sel(ecs, expand), dec_e=_dot_sel(dec, expand))


def _head_lambda(t, h):
    col = jnp.sum(jnp.where(t["ci"] == h, t["cs"], 0.0), axis=1, keepdims=True)
    rowv = jnp.sum(jnp.where(t["ri"] == h, t["csT"], 0.0), axis=0, keepdims=True)
    return jnp.exp(jnp.where(t["ri"] >= t["ci"], col - rowv, -1e30))


def _ssd_fwd(proj, dtp, conv_w, conv_b, dt_bias, a_log, d_e, g_ssd, expand):
    T = proj.shape[0]
    nc = T // CHUNK

    def body(z_ref, xp_ref, dtp_ref, cw_ref, cb_ref, bias_ref, alog_ref, de_ref, g_ref, ex_ref,
             xc_ref, y_ref, yn_ref, sprev_ref, tail, S):
        @pl.when(pl.program_id(0) == 0)
        def _():
            tail[...] = jnp.zeros_like(tail)
            S[...] = jnp.zeros_like(S)

        xp = xp_ref[...]
        prev8 = tail[...]
        xc = cb_ref[...] + cw_ref[3:4, :] * xp
        for j in range(1, 4):
            xc += cw_ref[3 - j:4 - j, :] * _shift_down(xp, prev8, j)
        tail[...] = xp[CHUNK - 8:CHUNK]
        xc_ref[...] = xc
        t = _ssd_chunk_terms(xc, dtp_ref[...], bias_ref[...], alog_ref[...], ex_ref[...])
        xbc = t["xbc"]
        sx = xbc[:, 0:SSD_INNER]
        Bb = xbc[:, SSD_INNER:SSD_INNER + 256].astype(BF16)
        Cb = xbc[:, SSD_INNER + 256:SSD_CONV_CH].astype(BF16)
        X = t["dt_e"] * sx
        lane = lax.broadcasted_iota(jnp.int32, (CHUNK, LANE), 1)
        G = [_dot(Cb[:, 128 * g:128 * (g + 1)], Bb[:, 128 * g:128 * (g + 1)], NT) for g in range(SSD_GROUPS)]
        for k in range(SSD_HEADS // 2):
            Xp = X[:, 128 * k:128 * (k + 1)]
            acc = jnp.zeros((CHUNK, LANE), F32)
            for half in range(2):
                M = (G[k // 4] * _head_lambda(t, 2 * k + half)).astype(BF16)
                Xh = jnp.where((lane >= 64) if half else (lane < 64), Xp, 0.0).astype(BF16)
                acc += _dot(M, Xh)
            y_ref[:, 128 * k:128 * (k + 1)] = acc
        sprev_ref[0] = S[...]
        eL_e = _row(t["ecs_e"], CHUNK - 1)
        Xd = (X * t["dec_e"]).astype(BF16)
        for g in range(SSD_GROUPS):
            sl = slice(512 * g, 512 * (g + 1))
            Sg = S[:, sl]
            y_ref[:, sl] += t["ecs_e"][:, sl] * _dot(Cb[:, 128 * g:128 * (g + 1)], Sg.astype(BF16))
            S[:, sl] = eL_e[:, sl] * Sg + _dot(Bb[:, 128 * g:128 * (g + 1)], Xd[:, sl], TN)
        y = y_ref[...] + de_ref[...] * sx
        y_ref[...] = y
        z = z_ref[...]
        q = y * (z * _sigmoid(z))
        yn_ref[...] = (q * _rms(q) * g_ref[...]).astype(BF16)

    c0 = lambda i: (0, 0)
    return pl.pallas_call(
        body, name="ssd_fwd", grid=(nc,),
        in_specs=[pl.BlockSpec((CHUNK, SSD_INNER), lambda i: (i, 2)),
                  pl.BlockSpec((CHUNK, SSD_CONV_CH), lambda i: (i, 2)),
                  pl.BlockSpec((CHUNK, LANE), lambda i: (i, 0)),
                  pl.BlockSpec((4, SSD_CONV_CH), c0), pl.BlockSpec((1, SSD_CONV_CH), c0),
                  pl.BlockSpec((1, LANE), c0), pl.BlockSpec((1, LANE), c0),
                  pl.BlockSpec((1, SSD_INNER), c0), pl.BlockSpec((1, SSD_INNER), c0),
                  pl.BlockSpec((LANE, SSD_INNER), c0)],
        out_specs=[pl.BlockSpec((CHUNK, SSD_CONV_CH), lambda i: (i, 0)),
                   pl.BlockSpec((CHUNK, SSD_INNER), lambda i: (i, 0)),
                   pl.BlockSpec((CHUNK, SSD_INNER), lambda i: (i, 0)),
                   pl.BlockSpec((1, SSD_STATE, SSD_INNER), lambda i: (i, 0, 0))],
        out_shape=[jax.ShapeDtypeStruct((T, SSD_CONV_CH), F32), jax.ShapeDtypeStruct((T, SSD_INNER), F32),
                   jax.ShapeDtypeStruct((T, SSD_INNER), BF16),
                   jax.ShapeDtypeStruct((nc, SSD_STATE, SSD_INNER), F32)],
        scratch_shapes=[pltpu.VMEM((8, SSD_CONV_CH), F32), pltpu.VMEM((SSD_STATE, SSD_INNER), F32)],
        compiler_params=_params(1),
    )(proj, proj, dtp, conv_w, conv_b, dt_bias, a_log, d_e, g_ssd, expand)


def _ssd_bwd(dyn, proj, dtp, xc, y, sprev, conv_w, dt_bias, a_log, d_e, g_ssd, expand):
    T = proj.shape[0]
    nc = T // CHUNK

    def body(dyn_ref, z_ref, xp_ref, dtp_ref, xc_ref, y_ref, sprev_ref, cw_ref, bias_ref, alog_ref, de_ref,
             g_ref, ex_ref, dp_ref, ddtp_ref, dcw_ref, dcb_ref, dbias_ref, dA_ref, dD_ref, dg_ref,
             dS, head, dX_s, dxbc_s):
        @pl.when(pl.program_id(0) == 0)
        def _():
            dS[...] = jnp.zeros_like(dS)
            head[...] = jnp.zeros_like(head)
            for ref in (dcw_ref, dcb_ref, dbias_ref, dA_ref, dD_ref, dg_ref):
                ref[...] = jnp.zeros_like(ref)

        ex = ex_ref[...]
        xc = xc_ref[...]
        t = _ssd_chunk_terms(xc, dtp_ref[...], bias_ref[...], alog_ref[...], ex)
        ri, ci = t["ri"], t["ci"]
        xbc = t["xbc"]
        sx = xbc[:, 0:SSD_INNER]
        Bb = xbc[:, SSD_INNER:SSD_INNER + 256].astype(BF16)
        Cb = xbc[:, SSD_INNER + 256:SSD_CONV_CH].astype(BF16)
        X = t["dt_e"] * sx
        z = z_ref[...]
        sz = _sigmoid(z)
        siluz = z * sz
        yv = y_ref[...]
        q = yv * siluz
        rn = _rms(q)
        dynv = dyn_ref[...]
        dg_ref[...] += jnp.sum(dynv * q * rn, axis=0, keepdims=True)
        dq = _rms_bwd(dynv * g_ref[...], q, rn)
        dp_ref[:, 0:SSD_INNER] = (dq * yv * (sz * (1.0 + z * (1.0 - sz)))).astype(BF16)
        dY = dq * siluz
        dD_ref[...] += jnp.sum(dY * sx, axis=0, keepdims=True)
        dYb = dY.astype(BF16)
        lane = lax.broadcasted_iota(jnp.int32, (CHUNK, LANE), 1)
        dcs = jnp.zeros((CHUNK, CHUNK), F32)
        dcsT = jnp.zeros((CHUNK, CHUNK), F32)
        Xb = X.astype(BF16)
        for g in range(SSD_GROUPS):
            Bg = Bb[:, 128 * g:128 * (g + 1)]
            Cg = Cb[:, 128 * g:128 * (g + 1)]
            G = _dot(Cg, Bg, NT)
            dGsum = jnp.zeros((CHUNK, CHUNK), F32)
            for k in range(4 * g, 4 * g + 4):
                Xp = Xb[:, 128 * k:128 * (k + 1)]
                dYp = dY[:, 128 * k:128 * (k + 1)]
                dXp = jnp.zeros((CHUNK, LANE), F32)
                for half in range(2):
                    h = 2 * k + half
                    lam = _head_lambda(t, h)
                    M = G * lam
                    dYh = jnp.where((lane >= 64) if half else (lane < 64), dYp, 0.0).astype(BF16)
                    dM = _dot(dYh, Xp, NT)
                    W = dM * M
                    dcs += jnp.where(ci == h, jnp.sum(W, axis=1, keepdims=True), 0.0)
                    dcsT += jnp.where(ri == h, jnp.sum(W, axis=0, keepdims=True), 0.0)
                    dGsum += dM * lam
                    dXp += _dot(M.astype(BF16), dYh, TN)
                dX_s[:, 128 * k:128 * (k + 1)] = dXp
            dGb = dGsum.astype(BF16)
            dxbc_s[:, SSD_INNER + 256 + 128 * g:SSD_INNER + 256 + 128 * (g + 1)] = _dot(dGb, Bg)
            dxbc_s[:, SSD_INNER + 128 * g:SSD_INNER + 128 * (g + 1)] = _dot(dGb, Cg, TN)
        dcs = dcs - dcsT.T
        Sp = sprev_ref[0]
        dSv = dS[...]
        ecs_e, dec_e = t["ecs_e"], t["dec_e"]
        eL_e = _row(ecs_e, CHUNK - 1)
        dYe = dY * ecs_e
        dYeb = dYe.astype(BF16)
        Xd = X * dec_e
        Xdb = Xd.astype(BF16)
        for g in range(SSD_GROUPS):
            sl = slice(512 * g, 512 * (g + 1))
            Bg = Bb[:, 128 * g:128 * (g + 1)]
            Cg = Cb[:, 128 * g:128 * (g + 1)]
            Spb = Sp[:, sl].astype(BF16)
            dSb = dSv[:, sl].astype(BF16)
            CS = _dot(Cg, Spb)
            BS = _dot(Bg, dSb)
            dxbc_s[:, SSD_INNER + 256 + 128 * g:SSD_INNER + 256 + 128 * (g + 1)] += _dot(dYeb[:, sl], Spb, NT)
            dxbc_s[:, SSD_INNER + 128 * g:SSD_INNER + 128 * (g + 1)] += _dot(Xdb[:, sl], dSb, NT)
            dS[:, sl] = eL_e[:, sl] * dSv[:, sl] + _dot(Cg, dYeb[:, sl], TN)
            dX_s[:, sl] += dec_e[:, sl] * BS
            dcs += _dot_sel(dYe[:, sl] * CS, ex[:, sl], NT)
            tdec = _dot_sel(X[:, sl] * BS, ex[:, sl], NT) * t["dec"]
            dcs -= tdec
            last = jnp.sum(tdec, axis=0, keepdims=True)
            last += jnp.sum(_dot_sel(Sp[:, sl] * dSv[:, sl], ex[:, sl], NT), axis=0, keepdims=True) \
                * _row(t["ecs"], CHUNK - 1)
            dcs += jnp.where(ri == CHUNK - 1, last, 0.0)
        triu = (ci >= ri).astype(BF16)
        da = _dot_sel(dcs, triu, terms=3, sel_first=True)
        dX = dX_s[...]
        ddt = da * t["A"] + _dot_sel(dX * sx, ex, NT)
        dA_ref[...] += jnp.sum(da * t["dt"], axis=0, keepdims=True)
        ddtp = ddt * _sigmoid(t["pre"])
        dbias_ref[...] += jnp.sum(ddtp, axis=0, keepdims=True)
        ddtp_ref[...] = ddtp.astype(BF16)
        dxbc_s[:, 0:SSD_INNER] = dX * t["dt_e"] + de_ref[...] * dY
        sg = t["sg"]
        dxc = dxbc_s[...] * (sg * (1.0 + xc * (1.0 - sg)))
        xp = xp_ref[...]
        nxt = head[...]
        dcb_ref[...] += jnp.sum(dxc, axis=0, keepdims=True)
        dpre = cw_ref[3:4, :] * dxc
        dcw_ref[3:4, :] += jnp.sum(xp * dxc, axis=0, keepdims=True)
        for j in range(1, 4):
            sh = _shift_up(dxc, nxt, j)
            dpre += cw_ref[3 - j:4 - j, :] * sh
            dcw_ref[3 - j:4 - j, :] += jnp.sum(xp * sh, axis=0, keepdims=True)
        head[...] = dxc[0:8]
        dp_ref[:, SSD_INNER:SSD_INNER + SSD_CONV_CH] = dpre.astype(BF16)

    c0 = lambda i: (0, 0)
    rev = lambda i: (nc - 1 - i, 0)
    return pl.pallas_call(
        body, name="ssd_bwd", grid=(nc,),
        in_specs=[pl.BlockSpec((CHUNK, SSD_INNER), rev),
                  pl.BlockSpec((CHUNK, SSD_INNER), lambda i: (nc - 1 - i, 2)),
                  pl.BlockSpec((CHUNK, SSD_CONV_CH), lambda i: (nc - 1 - i, 2)),
                  pl.BlockSpec((CHUNK, LANE), rev),
                  pl.BlockSpec((CHUNK, SSD_CONV_CH), rev),
                  pl.BlockSpec((CHUNK, SSD_INNER), rev),
                  pl.BlockSpec((1, SSD_STATE, SSD_INNER), lambda i: (nc - 1 - i, 0, 0)),
                  pl.BlockSpec((4, SSD_CONV_CH), c0), pl.BlockSpec((1, LANE), c0), pl.BlockSpec((1, LANE), c0),
                  pl.BlockSpec((1, SSD_INNER), c0), pl.BlockSpec((1, SSD_INNER), c0),
                  pl.BlockSpec((LANE, SSD_INNER), c0)],
        out_specs=[pl.BlockSpec((CHUNK, 2560), rev), pl.BlockSpec((CHUNK, LANE), rev),
                   pl.BlockSpec((4, SSD_CONV_CH), c0), pl.BlockSpec((1, SSD_CONV_CH), c0),
                   pl.BlockSpec((1, LANE), c0), pl.BlockSpec((1, LANE), c0),
                   pl.BlockSpec((1, SSD_INNER), c0), pl.BlockSpec((1, SSD_INNER), c0)],
        out_shape=[jax.ShapeDtypeStruct((T, 2560), BF16), jax.ShapeDtypeStruct((T, LANE), BF16),
                   jax.ShapeDtypeStruct((4, SSD_CONV_CH), F32), jax.ShapeDtypeStruct((1, SSD_CONV_CH), F32),
                   jax.ShapeDtypeStruct((1, LANE), F32), jax.ShapeDtypeStruct((1, LANE), F32),
                   jax.ShapeDtypeStruct((1, SSD_INNER), F32), jax.ShapeDtypeStruct((1, SSD_INNER), F32)],
        scratch_shapes=[pltpu.VMEM((SSD_STATE, SSD_INNER), F32), pltpu.VMEM((8, SSD_CONV_CH), F32),
                        pltpu.VMEM((CHUNK, SSD_INNER), F32), pltpu.VMEM((CHUNK, SSD_CONV_CH), F32)],
        compiler_params=_params(1),
    )(dyn, proj, proj, dtp, xc, y, sprev, conv_w, dt_bias, a_log, d_e, g_ssd, expand)


def _outproj_fwd(x, y_lru, y_ssd, w_out, g_pm, g_pf):
    T = x.shape[0]
    TT = TP

    def body(x_ref, yl_ref, ys_ref, wo_ref, gpm_ref, gpf_ref, mix_ref, x1_ref, h2_ref):
        mix = _dot(yl_ref[...], wo_ref[0:LRU_W, :]) + _dot(ys_ref[...], wo_ref[LRU_W:2 * LRU_W, :])
        mix_ref[...] = mix
        x1 = x_ref[...] + mix * _rms(mix) * gpm_ref[...]
        x1_ref[...] = x1
        h2_ref[...] = (x1 * _rms(x1) * gpf_ref[...]).astype(BF16)

    tile = pl.BlockSpec((TT, D_MODEL), lambda i: (i, 0))
    vec = pl.BlockSpec((1, D_MODEL), lambda i: (0, 0))
    return pl.pallas_call(
        body, name="outproj_fwd", grid=(T // TT,),
        in_specs=[tile, tile, tile, pl.BlockSpec((2 * LRU_W, D_MODEL), lambda i: (0, 0)), vec, vec],
        out_specs=[tile, tile, tile],
        out_shape=[jax.ShapeDtypeStruct((T, D_MODEL), F32), jax.ShapeDtypeStruct((T, D_MODEL), F32),
                   jax.ShapeDtypeStruct((T, D_MODEL), BF16)],
        compiler_params=_params(1),
    )(x, y_lru, y_ssd, w_out, g_pm, g_pf)


def _ffn_fwd_bwd(x1, h2, target, w_gate, w_up, w_down, g_pf, g_ff):
    T = x1.shape[0]

    def body(x1_ref, h2_ref, tg_ref, wg_hbm, wu_hbm, wd_hbm, gpf_ref, gff_ref,
             dx1_ref, act_ref, df_ref, dgt_ref, dup_ref, dgpf_ref, dgff_ref, loss_ref,
             wg, wu, wd, gt_s, up_s, sem):
        @pl.when(pl.program_id(0) == 0)
        def _():
            cps = [pltpu.make_async_copy(s, d, sem.at[n]) for n, (s, d) in
                   enumerate(((wg_hbm, wg), (wu_hbm, wu), (wd_hbm, wd)))]
            for c in cps:
                c.start()
            for c in cps:
                c.wait()
            for ref in (dgpf_ref, dgff_ref, loss_ref):
                ref[...] = jnp.zeros_like(ref)

        h2 = h2_ref[...]
        f = jnp.zeros((TT, D_MODEL), F32)
        for c in range(D_FF // FF_CHUNK):
            sl = slice(FF_CHUNK * c, FF_CHUNK * (c + 1))
            gt = _dot(h2, wg[:, sl])
            up = _dot(h2, wu[:, sl])
            gt_s[:, sl] = gt
            up_s[:, sl] = up
            act = (gt * _sigmoid(gt) * up).astype(BF16)
            act_ref[:, sl] = act
            f += _dot(act, wd[sl, :])
        x1 = x1_ref[...]
        rnf = _rms(f)
        e = x1 + f * rnf * gff_ref[...] - tg_ref[...]
        part = 0.5 * jnp.sum(jnp.sum(e * e, axis=1, keepdims=True), axis=0, keepdims=True) * (1.0 / D_MODEL)
        lane = lax.broadcasted_iota(jnp.int32, (1, LANE), 1)
        loss_ref[...] += jnp.where(lane == 0, part, 0.0)
        dx2 = e * (1.0 / D_MODEL)
        dgff_ref[...] += jnp.sum(dx2 * f * rnf, axis=0, keepdims=True)
        df = _rms_bwd(dx2 * gff_ref[...], f, rnf).astype(BF16)
        df_ref[...] = df
        dh2 = jnp.zeros((TT, D_MODEL), F32)
        for c in range(D_FF // FF_CHUNK):
            sl = slice(FF_CHUNK * c, FF_CHUNK * (c + 1))
            dact = _dot(df, wd[sl, :], NT)
            gt = gt_s[:, sl]
            up = up_s[:, sl]
            sg = _sigmoid(gt)
            dgt = (dact * up * (sg * (1.0 + gt * (1.0 - sg)))).astype(BF16)
            dup = (dact * gt * sg).astype(BF16)
            dgt_ref[:, sl] = dgt
            dup_ref[:, sl] = dup
            dh2 += _dot(dgt, wg[:, sl], NT) + _dot(dup, wu[:, sl], NT)
        rn2 = _rms(x1)
        dgpf_ref[...] += jnp.sum(dh2 * x1 * rn2, axis=0, keepdims=True)
        dx1_ref[...] = dx2 + _rms_bwd(dh2 * gpf_ref[...], x1, rn2)

    tile = pl.BlockSpec((TT, D_MODEL), lambda i: (i, 0))
    wide = pl.BlockSpec((TT, D_FF), lambda i: (i, 0))
    vec = pl.BlockSpec((1, D_MODEL), lambda i: (0, 0))
    hbm = pl.BlockSpec(memory_space=pl.ANY)
    return pl.pallas_call(
        body, name="ffn_fwd_bwd", grid=(T // TT,),
        in_specs=[tile, tile, tile, hbm, hbm, hbm, vec, vec],
        out_specs=[tile, wide, tile, wide, wide, vec, vec, pl.BlockSpec((1, LANE), lambda i: (0, 0))],
        out_shape=[jax.ShapeDtypeStruct((T, D_MODEL), F32), jax.ShapeDtypeStruct((T, D_FF), BF16),
                   jax.ShapeDtypeStruct((T, D_MODEL), BF16), jax.ShapeDtypeStruct((T, D_FF), BF16),
                   jax.ShapeDtypeStruct((T, D_FF), BF16), jax.ShapeDtypeStruct((1, D_MODEL), F32),
                   jax.ShapeDtypeStruct((1, D_MODEL), F32), jax.ShapeDtypeStruct((1, LANE), F32)],
        scratch_shapes=[pltpu.VMEM((D_MODEL, D_FF), BF16), pltpu.VMEM((D_MODEL, D_FF), BF16),
                        pltpu.VMEM((D_FF, D_MODEL), BF16), pltpu.VMEM((TT, D_FF), F32),
                        pltpu.VMEM((TT, D_FF), F32), pltpu.SemaphoreType.DMA((3,))],
        compiler_params=_params(1),
    )(x1, h2, target, w_gate, w_up, w_down, g_pf, g_ff)


def _outproj_bwd(dx1, mix, w_out, g_pm):
    T = dx1.shape[0]
    TT = TP

    def body(dx1_ref, mix_ref, wo_ref, gpm_ref, dyl_ref, dys_ref, dmix_ref, dg_ref):
        @pl.when(pl.program_id(0) == 0)
        def _():
            dg_ref[...] = jnp.zeros_like(dg_ref)

        mix = mix_ref[...]
        rn = _rms(mix)
        dx1v = dx1_ref[...]
        dg_ref[...] += jnp.sum(dx1v * mix * rn, axis=0, keepdims=True)
        dmix = _rms_bwd(dx1v * gpm_ref[...], mix, rn).astype(BF16)
        dmix_ref[...] = dmix
        dyl_ref[...] = _dot(dmix, wo_ref[0:LRU_W, :], NT)
        dys_ref[...] = _dot(dmix, wo_ref[LRU_W:2 * LRU_W, :], NT)

    tile = pl.BlockSpec((TT, D_MODEL), lambda i: (i, 0))
    vec = pl.BlockSpec((1, D_MODEL), lambda i: (0, 0))
    return pl.pallas_call(
        body, name="outproj_bwd", grid=(T // TT,),
        in_specs=[tile, tile, pl.BlockSpec((2 * LRU_W, D_MODEL), lambda i: (0, 0)), vec],
        out_specs=[tile, tile, tile, vec],
        out_shape=[jax.ShapeDtypeStruct((T, D_MODEL), F32), jax.ShapeDtypeStruct((T, D_MODEL), F32),
                   jax.ShapeDtypeStruct((T, D_MODEL), BF16), jax.ShapeDtypeStruct((1, D_MODEL), F32)],
        compiler_params=_params(1),
    )(dx1, mix, w_out, g_pm)


def _tn_matmul(a, bs, name, tk=512):
    T, M = a.shape
    nk = T // tk
    nb = len(bs)

    def body(*refs):
        a_ref, b_refs, o_refs, accs = refs[0], refs[1:1 + nb], refs[1 + nb:1 + 2 * nb], refs[1 + 2 * nb:]
        k = pl.program_id(0)

        @pl.when(k == 0)
        def _():
            for acc in accs:
                acc[...] = jnp.zeros_like(acc)

        av = a_ref[...].astype(BF16)
        for b_ref, acc in zip(b_refs, accs):
            acc[...] += _dot(av, b_ref[...], TN)

        @pl.when(k == nk - 1)
        def _():
            for o_ref, acc in zip(o_refs, accs):
                o_ref[...] = acc[...].astype(BF16)

    return pl.pallas_call(
        body, name=name, grid=(nk,),
        in_specs=[pl.BlockSpec((tk, M), lambda k: (k, 0))]
                 + [pl.BlockSpec((tk, b.shape[1]), lambda k: (k, 0)) for b in bs],
        out_specs=[pl.BlockSpec((M, b.shape[1]), lambda k: (0, 0)) for b in bs],
        out_shape=[jax.ShapeDtypeStruct((M, b.shape[1]), BF16) for b in bs],
        scratch_shapes=[pltpu.VMEM((M, b.shape[1]), F32) for b in bs],
        compiler_params=_params(1),
    )(a, *bs)


def _tn_blockdiag(a, b1, b2, name, tk=1024):
    T = a.shape[0]
    tk = min(tk, T)

    def body(a_ref, b1_ref, b2_ref, o1_ref, o2_ref):
        @pl.when(pl.program_id(0) == 0)
        def _():
            o1_ref[...] = jnp.zeros_like(o1_ref)
            o2_ref[...] = jnp.zeros_like(o2_ref)

        for j in range(4):
            sl = slice(256 * j, 256 * (j + 1))
            av = a_ref[:, sl].astype(BF16)
            o1_ref[j] += _dot(av, b1_ref[:, sl], TN)
            o2_ref[j] += _dot(av, b2_ref[:, sl], TN)

    blk = pl.BlockSpec((tk, LRU_W), lambda k: (k, 0))
    out = pl.BlockSpec((4, 256, 256), lambda k: (0, 0, 0))
    return pl.pallas_call(
        body, name=name, grid=(T // tk,),
        in_specs=[blk, blk, blk], out_specs=[out, out],
        out_shape=[jax.ShapeDtypeStruct((4, 256, 256), F32)] * 2,
        compiler_params=_params(1),
    )(a, b1, b2)


def _adamw_update(g, w, m, v):
    nm = ADAM_B1 * m + (1.0 - ADAM_B1) * g
    nv = ADAM_B2 * v + (1.0 - ADAM_B2) * (g * g)
    m_hat = nm / (1.0 - ADAM_B1 ** ADAM_STEP)
    v_hat = nv / (1.0 - ADAM_B2 ** ADAM_STEP)
    return -ADAM_LR * (m_hat / (jnp.sqrt(v_hat) + ADAM_EPS) + ADAM_WD * w), nm, nv


def _adamw_small(parts, params):
    n = len(params)

    def body(*refs):
        p_ref, wmv = refs[0], refs[1:1 + 3 * n]
        gs_ref, outs = refs[1 + 3 * n], refs[2 + 3 * n:]
        g = p_ref[0]
        for k in range(1, N_DEV):
            g = g + p_ref[k]
        gs_ref[...] = g
        for i, (row, w, _, _) in enumerate(params):
            rows, width = w.shape
            if width <= 1024:
                gp = gs_ref[row:row + rows, 0:width]
            else:
                gp = jnp.concatenate([gs_ref[row:row + 1, :], gs_ref[row + 1:row + 2, 0:width - 1024]], axis=1)
            d, nm, nv = _adamw_update(gp, wmv[3 * i][...], wmv[3 * i + 1][...], wmv[3 * i + 2][...])
            for ref, val in zip(outs[4 * i:4 * i + 4], (gp, d, nm, nv)):
                ref[...] = val

    full = lambda s: pl.BlockSpec(s, lambda: (0,) * len(s))
    shapes = [w.shape for _, w, _, _ in params]
    return pl.pallas_call(
        body, name="adamw_small",
        in_specs=[full(parts.shape)] + [full(s) for s in shapes for _ in range(3)],
        out_specs=[full(parts.shape[1:])] + [full(s) for s in shapes for _ in range(4)],
        out_shape=[jax.ShapeDtypeStruct(parts.shape[1:], F32)]
                  + [jax.ShapeDtypeStruct(s, F32) for s in shapes for _ in range(4)],
        compiler_params=pltpu.CompilerParams(vmem_limit_bytes=VMEM_LIMIT),
    )(parts, *[a for _, w, m, v in params for a in (w, m, v)])


def _adamw(parts, w, m, v, name, tr):
    P, R, C = parts.shape

    def body(p_ref, w_ref, m_ref, v_ref, g_ref, d_ref, nm_ref, nv_ref):
        g = p_ref[0].astype(F32)
        for k in range(1, P):
            g = g + p_ref[k].astype(F32)
        g_ref[...] = g
        d_ref[...], nm_ref[...], nv_ref[...] = _adamw_update(g, w_ref[...], m_ref[...], v_ref[...])

    blk = pl.BlockSpec((tr, C), lambda i: (i, 0))
    return pl.pallas_call(
        body, name=name, grid=(R // tr,),
        in_specs=[pl.BlockSpec((P, tr, C), lambda i: (0, i, 0)), blk, blk, blk],
        out_specs=[blk, blk, blk, blk],
        out_shape=[jax.ShapeDtypeStruct((R, C), F32)] * 4,
        compiler_params=_params(1),
    )(parts, w, m, v)


def _peer(k):
    x, y, c = lax.axis_index("x"), lax.axis_index("y"), lax.axis_index("c")
    px = x ^ ((k >> 2) & 1)
    py = y ^ ((k >> 1) & 1)
    pc = c ^ (k & 1)
    return (px, py, pc), 4 * px + 2 * py + pc


def _my_block():
    return 4 * lax.axis_index("x") + 2 * lax.axis_index("y") + lax.axis_index("c")


def _all_gather(shards, name):
    n = len(shards)

    def body(*refs):
        ins, outs = refs[:n], refs[n:2 * n]
        send, recv, loc = refs[2 * n:]
        x, y, c = lax.axis_index("x"), lax.axis_index("y"), lax.axis_index("c")
        sibling = (x, y, 1 - c)
        chips = [(1 - x, y), (x, 1 - y), (1 - x, 1 - y)]
        slot = lambda px, py, pc: 4 * px + 2 * py + pc

        def copy(a, k, block, to, src=None):
            dst = outs[a].at[slot(*block)]
            return pltpu.make_async_remote_copy(
                src_ref=dst if src is None else src, dst_ref=dst, send_sem=send.at[a, k], recv_sem=recv.at[a, k],
                device_id=to, device_id_type=pl.DeviceIdType.MESH)

        mine = [pltpu.make_async_copy(ins[a], outs[a].at[slot(x, y, c)], loc.at[a]) for a in range(n)]
        for cp in mine:
            cp.start()
        first = []
        for a in range(n):
            first.append(copy(a, 0, (x, y, c), sibling, src=ins[a]))
            first += [copy(a, 1 + j, (x, y, c), (*chip, c), src=ins[a]) for j, chip in enumerate(chips)]
        for cp in first:
            cp.start()
        passed = []
        for j, chip in enumerate(chips):
            for a in range(n):
                copy(a, 1 + j, (*chip, c), (x, y, c)).wait_recv()
                fwd = copy(a, 4 + j, (*chip, c), sibling)
                fwd.start()
                passed.append(fwd)
        for a in range(n):
            copy(a, 0, sibling, (x, y, c)).wait_recv()
            for j, chip in enumerate(chips):
                copy(a, 4 + j, (*chip, 1 - c), (x, y, c)).wait_recv()
        for cp in first + passed:
            cp.wait_send()
        for cp in mine:
            cp.wait()

    hbm = pl.BlockSpec(memory_space=pl.ANY)
    return pl.pallas_call(
        body, name=name,
        in_specs=[hbm] * n, out_specs=[hbm] * n,
        out_shape=[jax.ShapeDtypeStruct((N_DEV,) + s.shape, s.dtype) for s in shards],
        scratch_shapes=[pltpu.SemaphoreType.DMA((n, N_DEV - 1)), pltpu.SemaphoreType.DMA((n, N_DEV - 1)),
                        pltpu.SemaphoreType.DMA((n,))],
    )(*shards)


_HBM = pl.BlockSpec(memory_space=pltpu.HBM)
_SEM = pl.BlockSpec(memory_space=pltpu.SEMAPHORE)
_EFFECT = pltpu.SideEffectType.DATAFLOW_SIDE_EFFECTING


def _direct_copies(srcs, lands, send, recv, slab_source):
    me = _my_block()
    cps = []
    for k in range(1, N_DEV):
        to, blk = _peer(k)
        for a, (src, land) in enumerate(zip(srcs, lands)):
            if slab_source:
                s, d = src.at[blk], land.at[me]
            elif land.ndim == 3:
                s, d = src, land.at[me]
            else:
                s, d = src, land.at[pl.ds(pl.multiple_of(me * src.shape[0], 16), src.shape[0]), :]
            cps.append(pltpu.make_async_remote_copy(
                src_ref=s, dst_ref=d,
                send_sem=send.at[a * (N_DEV - 1) + k - 1], recv_sem=recv.at[a * (N_DEV - 1) + k - 1],
                device_id=to, device_id_type=pl.DeviceIdType.MESH))
    return cps


def _exchange_start(srcs, name, slab_source, axes=None):
    n = len(srcs)
    if slab_source:
        shapes = [s.shape for s in srcs]
    else:
        shapes = [(N_DEV,) + s.shape if ax == 1 else (N_DEV * s.shape[0], s.shape[1]) for s, ax in zip(srcs, axes)]
    lands = [pltpu.with_memory_space_constraint(lax.empty(sh, s.dtype), pltpu.HBM) for sh, s in zip(shapes, srcs)]

    def body(*refs):
        ins, land_in = refs[:n], refs[n:2 * n]
        send, recv = refs[2 * n], refs[2 * n + 1]
        token = refs[4 * n + 2]
        for cp in _direct_copies(ins, land_in, send, recv, slab_source):
            cp.start()
        token[...] = jnp.zeros_like(token)

    sems = pltpu.SemaphoreType.DMA((n * (N_DEV - 1),))
    res = pl.pallas_call(
        body, name=name,
        out_shape=(sems, sems, *[pltpu.HBM(s.shape, s.dtype) for s in srcs],
                   *[pltpu.HBM(l.shape, l.dtype) for l in lands], jax.ShapeDtypeStruct((8, LANE), F32)),
        in_specs=[_HBM] * (2 * n),
        out_specs=(_SEM, _SEM, *[_HBM] * (2 * n), pl.BlockSpec(memory_space=pltpu.VMEM)),
        input_output_aliases={i: 2 + i for i in range(2 * n)},
        compiler_params=pltpu.CompilerParams(has_side_effects=_EFFECT),
    )(*[pltpu.with_memory_space_constraint(s, pltpu.HBM) for s in srcs], *lands)
    return dict(send=res[0], recv=res[1], srcs=res[2:2 + n], lands=res[2 + n:2 + 2 * n], token=res[-1],
                slab_source=slab_source)


def _exchange_wait(ex, after, name):
    n = len(ex["srcs"])
    slab_source = ex["slab_source"]

    def body(*refs):
        ins, lands = refs[:n], refs[n:2 * n]
        send, recv = refs[2 * n], refs[2 * n + 1]
        for cp in _direct_copies(ins, lands, send, recv, slab_source):
            cp.wait_send()
            cp.wait_recv()

    res = pl.pallas_call(
        body, name=name,
        out_shape=tuple(pltpu.HBM(s.shape, s.dtype) for s in list(ex["srcs"]) + list(ex["lands"])),
        in_specs=[_HBM] * (2 * n) + [_SEM, _SEM, pl.BlockSpec(memory_space=pl.ANY)],
        out_specs=tuple([_HBM] * (2 * n)),
        input_output_aliases={i: i for i in range(2 * n)},
        compiler_params=pltpu.CompilerParams(has_side_effects=_EFFECT),
    )(*ex["srcs"], *ex["lands"], ex["send"], ex["recv"], after)
    me = _my_block()
    out = []
    for src, land in zip(res[:n], res[n:]):
        if slab_source:
            own, at = lax.dynamic_index_in_dim(src, me, 0, keepdims=True), (me, 0, 0)
        elif land.ndim == 3:
            own, at = src[None], (me, 0, 0)
        else:
            own, at = src, (me * src.shape[0], 0)
        out.append(lax.dynamic_update_slice(land, own, at))
    return out


BIG = ("w_in", "w_out", "w_gate", "w_up", "w_down")
BIG_SHARD = {"w_in": (1024, 578), "w_out": (256, 1024), "w_gate": (1024, 352), "w_up": (1024, 352),
             "w_down": (352, 1024)}
BIG_SHARD_AXIS = {"w_in": 1, "w_out": 0, "w_gate": 1, "w_up": 1, "w_down": 0}
BIG_ADAM_ROWS = {"w_in": 256, "w_out": 128, "w_gate": 256, "w_up": 256, "w_down": 176}


def _join(parts, axis):
    if axis == 0:
        return parts.reshape((-1,) + parts.shape[2:])
    return jnp.concatenate([parts[j] for j in range(N_DEV)], axis=1)


def _split(full, axis):
    if axis == 0:
        return full.reshape((N_DEV, full.shape[0] // N_DEV) + full.shape[1:])
    c = full.shape[1] // N_DEV
    return jnp.stack([full[:, c * j:c * (j + 1)] for j in range(N_DEV)])


SMALL = (("lru_wa", 65536), ("lru_wx", 65536), ("pre_mix_norm", 1024), ("lru_conv_w", 4096), ("lru_conv_b", 1024),
         ("lru_ba", 1024), ("lru_bx", 1024), ("lru_lambda", 1024), ("lru_out_norm", 1024), ("ssd_conv_w", 6144),
         ("ssd_conv_b", 1536), ("ssd_dt_bias", 16), ("ssd_a_log", 16), ("ssd_d", 16), ("ssd_out_norm", 1024),
         ("post_mix_norm", 1024), ("pre_ffn_norm", 1024), ("post_ffn_norm", 1024), ("loss", 1))
REPLICATED = tuple(n for n, _ in SMALL if n not in ("lru_conv_w", "ssd_conv_w", "loss"))
SMALL_ROW = {}
for _name, _size in SMALL:
    SMALL_ROW[_name] = (sum(-(-s // 1024) for n, s in SMALL[:len(SMALL_ROW)]), -(-_size // 1024))


def _pack_small(d):
    rows = [jnp.pad(d[n].astype(F32).reshape(-1), (0, SMALL_ROW[n][1] * 1024 - s)).reshape(-1, 1024) for n, s in SMALL]
    used = sum(r.shape[0] for r in rows)
    return jnp.concatenate(rows + [jnp.zeros((SMALL_ROWS - used, 1024), F32)], axis=0)


def _small_entry(p, name):
    row, rows = SMALL_ROW[name]
    return p[row:row + rows].reshape(-1)[:dict(SMALL)[name]]


def _blockdiag4(w):
    on_diag = jnp.eye(4, dtype=w.dtype)[None, :, None, :, None]
    return (w.reshape(4, 4, 64, 1, 64) * on_diag).reshape(4, 256, 256)


def _diag_blocks(g):
    on_diag = jnp.eye(4, dtype=g.dtype)[None, :, None, :, None]
    return jnp.sum(g.reshape(4, 4, 64, 4, 64) * on_diag, axis=3).reshape(16, 64, 64)


def _local_step(x, target, w_in, P, rest_weights, emit, emit_small, start_token=None):
    w_main = w_in[:, :IN_MAIN]
    w_dt = jnp.pad(w_in[:, IN_MAIN:], ((0, 0), (0, LANE - SSD_HEADS)))
    pad16 = lambda v: jnp.pad(v.reshape(1, SSD_HEADS), ((0, 0), (0, LANE - SSD_HEADS)))
    dt_bias, a_log = pad16(P["ssd_dt_bias"]), pad16(P["ssd_a_log"])
    d_e = jnp.repeat(P["ssd_d"].reshape(SSD_HEADS), SSD_HEAD_DIM).reshape(1, SSD_INNER)
    expand = (jnp.arange(LANE)[:, None] == (jnp.arange(SSD_INNER)[None, :] // SSD_HEAD_DIM)).astype(BF16)
    wa_bd = _blockdiag4(P["lru_wa"].astype(BF16))
    wx_bd = _blockdiag4(P["lru_wx"].astype(BF16))
    vec = lambda n: P[n].reshape(1, -1)

    after = lambda v, tok: v if tok is None else v + tok[0:1, 0:1]

    h, proj, dtp = _inproj_fwd(x, after(vec("pre_mix_norm"), start_token), w_main, w_dt)
    lx, hl, y_lru, *gates = _lru_fwd(proj, P["lru_conv_w"], vec("lru_conv_b"), wa_bd, wx_bd, vec("lru_ba"),
                                     vec("lru_bx"), vec("lru_lambda"), vec("lru_out_norm"))
    xc, y, y_ssd, sprev = _ssd_fwd(proj, dtp, P["ssd_conv_w"], vec("ssd_conv_b"), dt_bias, a_log, d_e,
                                   vec("ssd_out_norm"), expand)
    W = rest_weights(y_ssd)
    mix, x1, h2 = _outproj_fwd(x, y_lru, y_ssd, W["w_out"], vec("post_mix_norm"), vec("pre_ffn_norm"))
    dx1, act, df, dgt, dup, dg_pf, dg_ff, loss = _ffn_fwd_bwd(
        x1, h2, target, W["w_gate"], W["w_up"], W["w_down"], vec("pre_ffn_norm"), vec("post_ffn_norm"))
    tok = emit("ffn", {"w_gate": _tn_matmul(h2, [dgt], "dw_gate")[0], "w_up": _tn_matmul(h2, [dup], "dw_up")[0],
                       "w_down": _tn_matmul(act, [df], "dw_down")[0]})
    dy_lru, dy_ssd, dmix, dg_pm = _outproj_bwd(dx1, mix, W["w_out"], after(vec("post_mix_norm"), tok))
    tok = emit("out", {"w_out": jnp.concatenate([_tn_matmul(y_lru, [dmix], "dw_out_lru")[0],
                                                 _tn_matmul(y_ssd, [dmix], "dw_out_ssd")[0]], axis=0)})
    dp_ssd, ddtp, dcw_s, dcb_s, dbias, dA, dD_e, dg_ssd = _ssd_bwd(
        dy_ssd, proj, dtp, xc, y, sprev, P["ssd_conv_w"], dt_bias, a_log, d_e,
        after(vec("ssd_out_norm"), tok), expand)
    dp_lru, dpa, dpx, dcw_l, dcb_l, dba, dbx, dlam, dg_lru = _lru_bwd(
        dy_lru, proj, lx, hl, gates, P["lru_conv_w"], wa_bd, wx_bd, vec("lru_lambda"), vec("lru_out_norm"))
    dw_in = _tn_matmul(h, [dp_lru, dp_ssd, ddtp], "dw_in")
    tok = emit("in", {"w_in": jnp.concatenate([dw_in[0], dw_in[1], dw_in[2][:, :SSD_HEADS]], axis=1)})

    a_neg = -jnp.exp(P["ssd_a_log"].reshape(SSD_HEADS))
    dwa, dwx = _tn_blockdiag(lx, dpa, dpx, "dw_lru_gates")
    small = {
        "pre_mix_norm": jnp.zeros((1, D_MODEL), F32), "lru_conv_w": dcw_l, "lru_conv_b": dcb_l,
        "lru_wa": _diag_blocks(dwa), "lru_ba": dba,
        "lru_wx": _diag_blocks(dwx), "lru_bx": dbx,
        "lru_lambda": dlam, "lru_out_norm": dg_lru, "ssd_conv_w": dcw_s, "ssd_conv_b": dcb_s,
        "ssd_dt_bias": dbias[0, :SSD_HEADS], "ssd_a_log": dA[0, :SSD_HEADS] * a_neg,
        "ssd_d": jnp.sum(dD_e.reshape(SSD_HEADS, SSD_HEAD_DIM), axis=1), "ssd_out_norm": dg_ssd,
        "post_mix_norm": dg_pm, "pre_ffn_norm": dg_pf, "post_ffn_norm": dg_ff, "loss": loss[0, 0:1],
    }
    tok = after(after(vec("pre_mix_norm"), tok), emit_small(small))
    grad_x, dg_pre = _inproj_bwd(dp_lru, dp_ssd, ddtp, dx1, x, tok, w_main, w_dt)
    return grad_x, dg_pre


def kernel(x, pre_mix_norm, w_in, lru_conv_w, lru_conv_b, lru_wa, lru_ba, lru_wx, lru_bx, lru_lambda, lru_out_norm, ssd_conv_w, ssd_conv_b, ssd_dt_bias, ssd_a_log, ssd_d, ssd_out_norm, w_out, post_mix_norm, pre_ffn_norm, w_gate, w_up, w_down, post_ffn_norm, loss_target, m_pre_mix_norm, m_w_in, m_lru_conv_w, m_lru_conv_b, m_lru_wa, m_lru_ba, m_lru_wx, m_lru_bx, m_lru_lambda, m_lru_out_norm, m_ssd_conv_w, m_ssd_conv_b, m_ssd_dt_bias, m_ssd_a_log, m_ssd_d, m_ssd_out_norm, m_w_out, m_post_mix_norm, m_pre_ffn_norm, m_w_gate, m_w_up, m_w_down, m_post_ffn_norm, v_pre_mix_norm, v_w_in, v_lru_conv_w, v_lru_conv_b, v_lru_wa, v_lru_ba, v_lru_wx, v_lru_bx, v_lru_lambda, v_lru_out_norm, v_ssd_conv_w, v_ssd_conv_b, v_ssd_dt_bias, v_ssd_a_log, v_ssd_d, v_ssd_out_norm, v_w_out, v_post_mix_norm, v_pre_ffn_norm, v_w_gate, v_w_up, v_w_down, v_post_ffn_norm):
    a = dict(locals())
    names = [n for n, _ in SMALL if n != "loss"] + list(BIG)
    w = {n: a[n][0] for n in names}
    m = {n: a["m_" + n][0] for n in names}
    v = {n: a["v_" + n][0] for n in names}

    cpack = jnp.concatenate([w["lru_conv_w"], w["ssd_conv_w"], jnp.zeros((4, 64), F32)], axis=1)
    cpack = jnp.pad(cpack, ((0, 4), (0, 0)))
    g_in, cg = _all_gather([w["w_in"].astype(BF16), cpack], "all_gather_w_in")
    P = {n: w[n] for n in REPLICATED}
    P["lru_conv_w"] = _join(cg[:, 0:4, 0:128], 1)
    P["ssd_conv_w"] = _join(cg[:, 0:4, 128:320], 1)

    rest = [n for n in BIG if n != "w_in"]
    zero = jnp.minimum(jnp.abs(cg[0, 0, 0]), 0.0)
    ex_w = _exchange_start([(w[n] + zero).astype(BF16) for n in rest], "weights_start", slab_source=False,
                           axes=[BIG_SHARD_AXIS[n] for n in rest])

    def rest_weights(after):
        lands = _exchange_wait(ex_w, after, "weights_wait")
        return {n: _join(p, 1) if p.ndim == 3 else p for n, p in zip(rest, lands)}

    pending = []

    def emit(group, grads):
        ex = _exchange_start([_split(g, BIG_SHARD_AXIS[n]) for n, g in grads.items()], "grads_start_" + group,
                             slab_source=True)
        pending.append((group, list(grads), ex))
        return ex["token"]

    def emit_small(small):
        ex = _exchange_start([_pack_small(small)], "small_start", slab_source=False, axes=[0])
        pending.append(("small", None, ex))
        return ex["token"]

    grad_x, dg_pre = _local_step(x[0], loss_target[0], _join(g_in, 1), P, rest_weights, emit, emit_small,
                                 ex_w["token"])

    got_pre = _all_gather([jnp.pad(dg_pre, ((0, 7), (0, 0)))], "all_gather_pre_mix_norm")[0]
    ex_small = pending.pop()[2]
    got_small = _exchange_wait(ex_small, got_pre, "small_wait")[0].reshape(N_DEV, SMALL_ROWS, 1024)
    row = SMALL_ROW["pre_mix_norm"][0]
    got_small = got_small.at[:, row:row + 1, :].set(got_pre[:, 0:1, :])

    outs = {}
    done = got_small
    for group, group_names, ex in pending:
        for n, parts in zip(group_names, _exchange_wait(ex, done, "grads_wait_" + group)):
            res = _adamw(parts, w[n], m[n], v[n], "adamw_" + n, BIG_ADAM_ROWS[n])
            done = res[0]
            for kind, r in zip(("grad", "delta", "new_m", "new_v"), res):
                outs[kind + "_" + n] = r

    two_d = lambda t: t.reshape(-1, 1024) if t.ndim == 3 else t.reshape(1, -1)
    res = _adamw_small(got_small, [(SMALL_ROW[n][0], two_d(w[n]), two_d(m[n]), two_d(v[n])) for n in REPLICATED])
    g_pack = res[0]
    for i, n in enumerate(REPLICATED):
        for kind, r in zip(("grad", "delta", "new_m", "new_v"), res[1 + 4 * i:5 + 4 * i]):
            outs[kind + "_" + n] = r.reshape(w[n].shape)

    me = _my_block()
    gl = lax.dynamic_slice(_small_entry(g_pack, "lru_conv_w").reshape(4, LRU_W), (0, me * 128), (4, 128))
    gs = lax.dynamic_slice(_small_entry(g_pack, "ssd_conv_w").reshape(4, SSD_CONV_CH), (0, me * 192), (4, 192))
    cat = lambda d: jnp.pad(jnp.concatenate([d["lru_conv_w"], d["ssd_conv_w"]], axis=1), ((0, 4), (0, 64)))
    res = _adamw(cat({"lru_conv_w": gl, "ssd_conv_w": gs})[None], cat(w), cat(m), cat(v), "adamw_conv", 8)
    for kind, r in zip(("grad", "delta", "new_m", "new_v"), res):
        outs[kind + "_lru_conv_w"] = r[0:4, 0:128]
        outs[kind + "_ssd_conv_w"] = r[0:4, 128:320]

    order = ["pre_mix_norm", "w_in", "lru_conv_w", "lru_conv_b", "lru_wa", "lru_ba", "lru_wx", "lru_bx", "lru_lambda",
             "lru_out_norm", "ssd_conv_w", "ssd_conv_b", "ssd_dt_bias", "ssd_a_log", "ssd_d", "ssd_out_norm", "w_out",
             "post_mix_norm", "pre_ffn_norm", "w_gate", "w_up", "w_down", "post_ffn_norm"]
    result = [_small_entry(g_pack, "loss").reshape(()), grad_x[None]]
    for kind in ("grad", "delta", "new_m", "new_v"):
        result += [outs[kind + "_" + n][None] for n in order]
    return tuple(result)
```

```python
import functools

import jax
import jax.numpy as jnp
from jax import lax
from jax.experimental import pallas as pl
from jax.experimental.pallas import tpu as pltpu

F32 = jnp.float32
BF16 = jnp.bfloat16
HI = lax.Precision.HIGHEST
EPS = 1e-6
N_DEV = 8
D_MODEL = 1024
LRU_W = 1024
SSD_INNER = 1024
SSD_HEADS = 16
SSD_HEAD_DIM = 64
SSD_STATE = 128
SSD_GROUPS = 2
SSD_CONV_CH = 1536
CHUNK = 128
D_FF = 2816
FF_CHUNK = 2816
IN_MAIN = 4608
IN_COLS = 4624
LANE = 128
TT = 256
TP = 512
VMEM_LIMIT = 56 * 1024 * 1024
ADAM_LR, ADAM_B1, ADAM_B2, ADAM_EPS, ADAM_WD, ADAM_STEP = 0.001, 0.9, 0.999, 1e-08, 0.01, 10
PACK_ROWS = 1920
SMALL_ROWS = 160

NT = (((1,), (1,)), ((), ()))
TN = (((0,), (0,)), ((), ()))


def _params(n_grid):
    return pltpu.CompilerParams(dimension_semantics=("arbitrary",) * n_grid, vmem_limit_bytes=VMEM_LIMIT)


def _dot(a, b, dims=None, precision=None):
    if dims is None:
        return jnp.dot(a, b, preferred_element_type=F32, precision=precision)
    return lax.dot_general(a, b, dims, preferred_element_type=F32, precision=precision)


def _split_bf16(x, terms):
    out = []
    for _ in range(terms - 1):
        p = x.astype(BF16)
        out.append(p)
        x = x - p.astype(F32)
    return out + [x.astype(BF16)]


def _dot_sel(x, sel, dims=None, terms=2, sel_first=False):
    parts = [_dot(sel, p, dims) if sel_first else _dot(p, sel, dims) for p in _split_bf16(x, terms)]
    return functools.reduce(lambda a, b: a + b, parts)


def _sigmoid(x):
    return 0.5 * jnp.tanh(0.5 * x) + 0.5


def _softplus(x):
    e = jnp.exp(-jnp.abs(x))
    l1p = jnp.where(e < 1e-3, e * (1.0 - e * (0.5 - e * (1.0 / 3.0))), jnp.log(1.0 + e))
    return jnp.maximum(x, 0.0) + l1p


def _neg_expm1(x):
    series = -x * (1.0 + x * (0.5 + x * (1.0 / 6.0 + x * (1.0 / 24.0))))
    return jnp.where(x > -0.01, series, 1.0 - jnp.exp(x))


_GELU_C = 0.7978845608028654


def _gelu(x):
    t = jnp.tanh(_GELU_C * (x + 0.044715 * x * x * x))
    return 0.5 * x * (1.0 + t), t


def _gelu_grad(x, t):
    return 0.5 * (1.0 + t) + 0.5 * x * (1.0 - t * t) * _GELU_C * (1.0 + 3.0 * 0.044715 * x * x)


def _rms(x):
    return lax.rsqrt(jnp.mean(x * x, axis=-1, keepdims=True) + EPS)


def _rms_bwd(dyn, x, rn):
    return rn * dyn - x * (rn * rn * rn) * jnp.mean(dyn * x, axis=-1, keepdims=True)


def _row(x, r):
    idx = lax.broadcasted_iota(jnp.int32, x.shape, 0)
    return jnp.sum(jnp.where(idx == r, x, 0.0), axis=0, keepdims=True)


def _shift_down(cur, prev8, j):
    s = pltpu.roll(cur, j, 0)
    p = pltpu.roll(prev8, j, 0)
    r8 = lax.broadcasted_iota(jnp.int32, prev8.shape, 0)
    top = jnp.where(r8 < j, p, s[0:8])
    return jnp.concatenate([top, s[8:]], axis=0)


def _shift_up(cur, next8, j):
    n = cur.shape[0]
    s = pltpu.roll(cur, n - j, 0)
    p = pltpu.roll(next8, 8 - j, 0)
    r8 = lax.broadcasted_iota(jnp.int32, next8.shape, 0)
    bot = jnp.where(r8 >= 8 - j, p, s[n - 8:n])
    return jnp.concatenate([s[:n - 8], bot], axis=0)


def _scan_fwd(a, u):
    n = a.shape[0]
    row = lax.broadcasted_iota(jnp.int32, a.shape, 0)
    k = 1
    while k < n:
        ok = row >= k
        a_s = jnp.where(ok, pltpu.roll(a, k, 0), 1.0)
        u_s = jnp.where(ok, pltpu.roll(u, k, 0), 0.0)
        u = a * u_s + u
        a = a * a_s
        k *= 2
    return a, u


def _scan_bwd(b, d):
    n = b.shape[0]
    row = lax.broadcasted_iota(jnp.int32, b.shape, 0)
    k = 1
    while k < n:
        ok = row < n - k
        b_s = jnp.where(ok, pltpu.roll(b, n - k, 0), 1.0)
        d_s = jnp.where(ok, pltpu.roll(d, n - k, 0), 0.0)
        d = b * d_s + d
        b = b * b_s
        k *= 2
    return b, d


def _scan_tile(a, u, carry, a_s, u_s, reverse):
    n, c = a.shape
    groups = n // 8
    r8 = lax.broadcasted_iota(jnp.int32, a.shape, 0) & 7
    in_group = lambda x, k: pltpu.roll(x.reshape(groups, 8, c), k, 1).reshape(n, c)
    for k in (1, 2, 4):
        ok = (r8 < 8 - k) if reverse else (r8 >= k)
        shift = 8 - k if reverse else k
        a_n = jnp.where(ok, in_group(a, shift), 1.0)
        u_n = jnp.where(ok, in_group(u, shift), 0.0)
        u = a * u_n + u
        a = a * a_n
    nl = c // LANE
    for j in range(nl):
        a_s[j] = a[:, LANE * j:LANE * (j + 1)]
        u_s[j] = u[:, LANE * j:LANE * (j + 1)]
    end = 0 if reverse else 7
    ends = lambda ref, j: ref[pl.ds(j, 1), pl.ds(end, groups, stride=8), :].reshape(groups, LANE)
    ga = jnp.concatenate([ends(a_s, j) for j in range(nl)], axis=1)
    gu = jnp.concatenate([ends(u_s, j) for j in range(nl)], axis=1)
    gacc, gh = (_scan_bwd if reverse else _scan_fwd)(ga, gu)
    gh = gh + gacc * carry
    grow = lax.broadcasted_iota(jnp.int32, gh.shape, 0)
    if reverse:
        cin = jnp.where(grow == groups - 1, carry, pltpu.roll(gh, groups - 1, 0))
    else:
        cin = jnp.where(grow == 0, carry, pltpu.roll(gh, 1, 0))
    spread = ((lax.broadcasted_iota(jnp.int32, (n, LANE), 0) >> 3)
              == lax.broadcasted_iota(jnp.int32, (n, LANE), 1)).astype(BF16)
    cin = jnp.concatenate([cin, jnp.zeros((LANE - groups, c), F32)], axis=0)
    return u + a * _dot_sel(cin, spread, terms=3, sel_first=True)


def _inproj_fwd(x, g_pre, w_main, w_dt):
    T = x.shape[0]
    TT = TP

    def body(x_ref, g_ref, wm_hbm, wd_hbm, h_ref, proj_ref, dtp_ref, wm, wd, sem):
        @pl.when(pl.program_id(0) == 0)
        def _():
            c1 = pltpu.make_async_copy(wm_hbm, wm, sem.at[0])
            c2 = pltpu.make_async_copy(wd_hbm, wd, sem.at[1])
            c1.start()
            c2.start()
            c1.wait()
            c2.wait()

        xv = x_ref[...]
        h = (xv * _rms(xv) * g_ref[...]).astype(BF16)
        h_ref[...] = h
        proj_ref[...] = _dot(h, wm[...])
        dtp_ref[...] = _dot(h, wd[...])

    return pl.pallas_call(
        body, name="inproj_fwd", grid=(T // TT,),
        in_specs=[pl.BlockSpec((TT, D_MODEL), lambda i: (i, 0)),
                  pl.BlockSpec((1, D_MODEL), lambda i: (0, 0)),
                  pl.BlockSpec(memory_space=pl.ANY), pl.BlockSpec(memory_space=pl.ANY)],
        out_specs=[pl.BlockSpec((TT, D_MODEL), lambda i: (i, 0)),
                   pl.BlockSpec((TT, IN_MAIN), lambda i: (i, 0)),
                   pl.BlockSpec((TT, LANE), lambda i: (i, 0))],
        out_shape=[jax.ShapeDtypeStruct((T, D_MODEL), BF16), jax.ShapeDtypeStruct((T, IN_MAIN), F32),
                   jax.ShapeDtypeStruct((T, LANE), F32)],
        scratch_shapes=[pltpu.VMEM((D_MODEL, IN_MAIN), BF16), pltpu.VMEM((D_MODEL, LANE), BF16),
                        pltpu.SemaphoreType.DMA((2,))],
        compiler_params=_params(1),
    )(x, g_pre, w_main, w_dt)


def _inproj_bwd(dp_lru, dp_ssd, ddtp, dx1, x, g_pre, w_main, w_dt):
    T = x.shape[0]
    TT = TP

    def body(dl_ref, ds_ref, dd_ref, dx1_ref, x_ref, g_ref, wm_hbm, wd_hbm, gx_ref, dg_ref, wm, wd, sem):
        @pl.when(pl.program_id(0) == 0)
        def _():
            c1 = pltpu.make_async_copy(wm_hbm, wm, sem.at[0])
            c2 = pltpu.make_async_copy(wd_hbm, wd, sem.at[1])
            c1.start()
            c2.start()
            c1.wait()
            c2.wait()
            dg_ref[...] = jnp.zeros_like(dg_ref)

        dh = _dot(dl_ref[...], wm[:, 0:2048], NT)
        dh += _dot(ds_ref[...], wm[:, 2048:IN_MAIN], NT)
        dh += _dot(dd_ref[...], wd[...], NT)
        xv = x_ref[...]
        rn = _rms(xv)
        dg_ref[...] += jnp.sum(dh * xv * rn, axis=0, keepdims=True)
        gx_ref[...] = dx1_ref[...] + _rms_bwd(dh * g_ref[...], xv, rn)

    return pl.pallas_call(
        body, name="inproj_bwd", grid=(T // TT,),
        in_specs=[pl.BlockSpec((TT, 2048), lambda i: (i, 0)),
                  pl.BlockSpec((TT, 2560), lambda i: (i, 0)),
                  pl.BlockSpec((TT, LANE), lambda i: (i, 0)),
                  pl.BlockSpec((TT, D_MODEL), lambda i: (i, 0)),
                  pl.BlockSpec((TT, D_MODEL), lambda i: (i, 0)),
                  pl.BlockSpec((1, D_MODEL), lambda i: (0, 0)),
                  pl.BlockSpec(memory_space=pl.ANY), pl.BlockSpec(memory_space=pl.ANY)],
        out_specs=[pl.BlockSpec((TT, D_MODEL), lambda i: (i, 0)),
                   pl.BlockSpec((1, D_MODEL), lambda i: (0, 0))],
        out_shape=[jax.ShapeDtypeStruct((T, D_MODEL), F32), jax.ShapeDtypeStruct((1, D_MODEL), F32)],
        scratch_shapes=[pltpu.VMEM((D_MODEL, IN_MAIN), BF16), pltpu.VMEM((D_MODEL, LANE), BF16),
                        pltpu.SemaphoreType.DMA((2,))],
        compiler_params=_params(1),
    )(dp_lru, dp_ssd, ddtp, dx1, x, g_pre, w_main, w_dt)


def _lru_gates(lx, wa_ref, wx_ref, ba, bx, lam):
    lxb = lx.astype(BF16)
    pa = jnp.concatenate([_dot(lxb[:, 256 * k:256 * (k + 1)], wa_ref[k]) for k in range(4)], axis=1) + ba
    px = jnp.concatenate([_dot(lxb[:, 256 * k:256 * (k + 1)], wx_ref[k]) for k in range(4)], axis=1) + bx
    r = _sigmoid(pa)
    ig = _sigmoid(px)
    sp = _softplus(-lam)
    log_a = -8.0 * r * sp
    a = jnp.exp(log_a)
    mult = jnp.sqrt(_neg_expm1(2.0 * log_a))
    return r, ig, sp, a, mult


def _lru_fwd(proj, conv_w, conv_b, wa_bd, wx_bd, ba, bx, lam, g_lru):
    T = proj.shape[0]

    def body(cx_ref, gate_ref, cw_ref, cb_ref, wa_ref, wx_ref, ba_ref, bx_ref, lam_ref, g_ref,
             lx_ref, hl_ref, y_ref, r_ref, ig_ref, a_ref, mult_ref, tail, hcar, sa, su):
        @pl.when(pl.program_id(0) == 0)
        def _():
            tail[...] = jnp.zeros_like(tail)
            hcar[...] = jnp.zeros_like(hcar)

        cx = cx_ref[...]
        prev8 = tail[...]
        lx = cb_ref[...] + cw_ref[3:4, :] * cx
        for j in range(1, 4):
            lx += cw_ref[3 - j:4 - j, :] * _shift_down(cx, prev8, j)
        tail[...] = cx[TT - 8:TT]
        lx_ref[...] = lx
        r, ig, sp, a, mult = _lru_gates(lx, wa_ref, wx_ref, ba_ref[...], bx_ref[...], lam_ref[...])
        r_ref[...] = r
        ig_ref[...] = ig
        a_ref[...] = a
        mult_ref[...] = mult
        h = _scan_tile(a, mult * (ig * lx), hcar[...], sa, su, reverse=False)
        hl_ref[...] = h
        hcar[...] = hl_ref[TT - 1:TT, :]
        ge, _ = _gelu(gate_ref[...])
        p = h * ge
        y_ref[...] = (p * _rms(p) * g_ref[...]).astype(BF16)

    vec = pl.BlockSpec((1, LRU_W), lambda i: (0, 0))
    bd = pl.BlockSpec((4, 256, 256), lambda i: (0, 0, 0))
    tile = pl.BlockSpec((TT, LRU_W), lambda i: (i, 0))
    f32 = jax.ShapeDtypeStruct((T, LRU_W), F32)
    return pl.pallas_call(
        body, name="lru_fwd", grid=(T // TT,),
        in_specs=[tile, pl.BlockSpec((TT, LRU_W), lambda i: (i, 1)),
                  pl.BlockSpec((4, LRU_W), lambda i: (0, 0)), vec, bd, bd, vec, vec, vec, vec],
        out_specs=[tile] * 7,
        out_shape=[f32, f32, jax.ShapeDtypeStruct((T, LRU_W), BF16), f32, f32, f32, f32],
        scratch_shapes=[pltpu.VMEM((8, LRU_W), F32), pltpu.VMEM((1, LRU_W), F32)]
                       + [pltpu.VMEM((LRU_W // LANE, TT, LANE), F32)] * 2,
        compiler_params=_params(1),
    )(proj, proj, conv_w, conv_b, wa_bd, wx_bd, ba, bx, lam, g_lru)


def _lru_bwd(dy, proj, lx, hl, gates, conv_w, wa_bd, wx_bd, lam, g_lru):
    T = proj.shape[0]
    nt = T // TT

    def body(dy_ref, cx_ref, gate_ref, lx_ref, hl_ref, halo_ref, r_ref, ig_ref, a_ref, mult_ref, cw_ref, wa_ref,
             wx_ref, lam_ref, g_ref, dp_ref, dpa_ref, dpx_ref, dcw_ref, dcb_ref, dba_ref, dbx_ref, dlam_ref, dg_ref,
             gcar, acar, head, sa, su):
        i = pl.program_id(0)

        @pl.when(i == 0)
        def _():
            gcar[...] = jnp.zeros_like(gcar)
            acar[...] = jnp.zeros_like(acar)
            head[...] = jnp.zeros_like(head)
            for ref in (dcw_ref, dcb_ref, dba_ref, dbx_ref, dlam_ref, dg_ref):
                ref[...] = jnp.zeros_like(ref)

        lx = lx_ref[...]
        h = hl_ref[...]
        gate = gate_ref[...]
        cx = cx_ref[...]
        lam = lam_ref[...]
        r, ig, a, mult = r_ref[...], ig_ref[...], a_ref[...], mult_ref[...]
        sp = _softplus(-lam)
        ge, th = _gelu(gate)
        p = h * ge
        rn = _rms(p)
        dyv = dy_ref[...]
        dg_ref[...] += jnp.sum(dyv * p * rn, axis=0, keepdims=True)
        dp = _rms_bwd(dyv * g_ref[...], p, rn)
        dp_ref[:, LRU_W:2 * LRU_W] = (dp * h * _gelu_grad(gate, th)).astype(BF16)
        dh = dp * ge
        row = lax.broadcasted_iota(jnp.int32, a.shape, 0)
        b = jnp.where(row == TT - 1, acar[...], pltpu.roll(a, TT - 1, 0))
        g = _scan_tile(b, dh, gcar[...], sa, su, reverse=True)
        gcar[...] = _row(g[0:8], 0)
        acar[...] = _row(a[0:8], 0)
        h_last_prev = halo_ref[7:8, :] * (i < nt - 1).astype(F32)
        hprev = jnp.where(row == 0, h_last_prev, pltpu.roll(h, 1, 0))
        da = g * hprev
        dm2 = (g * (ig * lx)) * 0.5 / mult
        dlog_a = da * a - 2.0 * a * a * dm2
        dlam_ref[...] += jnp.sum(-8.0 * r * dlog_a, axis=0, keepdims=True) * (-_sigmoid(-lam))
        dpa = (-8.0 * sp * dlog_a) * r * (1.0 - r)
        dpx = (g * mult * lx) * ig * (1.0 - ig)
        dba_ref[...] += jnp.sum(dpa, axis=0, keepdims=True)
        dbx_ref[...] += jnp.sum(dpx, axis=0, keepdims=True)
        dpab = dpa.astype(BF16)
        dpxb = dpx.astype(BF16)
        dpa_ref[...] = dpab
        dpx_ref[...] = dpxb
        dlx = g * mult * ig + jnp.concatenate(
            [_dot(dpab[:, 256 * k:256 * (k + 1)], wa_ref[k], NT) + _dot(dpxb[:, 256 * k:256 * (k + 1)], wx_ref[k], NT)
             for k in range(4)], axis=1)
        nxt = head[...]
        dcb_ref[...] += jnp.sum(dlx, axis=0, keepdims=True)
        dcx = cw_ref[3:4, :] * dlx
        dcw_ref[3:4, :] += jnp.sum(cx * dlx, axis=0, keepdims=True)
        for j in range(1, 4):
            sh = _shift_up(dlx, nxt, j)
            dcx += cw_ref[3 - j:4 - j, :] * sh
            dcw_ref[3 - j:4 - j, :] += jnp.sum(cx * sh, axis=0, keepdims=True)
        head[...] = dlx[0:8]
        dp_ref[:, 0:LRU_W] = dcx.astype(BF16)

    rev = lambda i: (nt - 1 - i, 0)
    vec = pl.BlockSpec((1, LRU_W), lambda i: (0, 0))
    bd = pl.BlockSpec((4, 256, 256), lambda i: (0, 0, 0))
    tile = pl.BlockSpec((TT, LRU_W), rev)
    halo = pl.BlockSpec((8, LRU_W), lambda i: (jnp.maximum((nt - 1 - i) * (TT // 8) - 1, 0), 0))
    cw = pl.BlockSpec((4, LRU_W), lambda i: (0, 0))
    return pl.pallas_call(
        body, name="lru_bwd", grid=(nt,),
        in_specs=[tile, tile, pl.BlockSpec((TT, LRU_W), lambda i: (nt - 1 - i, 1)), tile, tile, halo,
                  tile, tile, tile, tile, cw, bd, bd, vec, vec],
        out_specs=[pl.BlockSpec((TT, 2 * LRU_W), rev), tile, tile, cw, vec, vec, vec, vec, vec],
        out_shape=[jax.ShapeDtypeStruct((T, 2 * LRU_W), BF16), jax.ShapeDtypeStruct((T, LRU_W), BF16),
                   jax.ShapeDtypeStruct((T, LRU_W), BF16), jax.ShapeDtypeStruct((4, LRU_W), F32)]
                  + [jax.ShapeDtypeStruct((1, LRU_W), F32)] * 5,
        scratch_shapes=[pltpu.VMEM((1, LRU_W), F32), pltpu.VMEM((1, LRU_W), F32), pltpu.VMEM((8, LRU_W), F32)]
                       + [pltpu.VMEM((LRU_W // LANE, TT, LANE), F32)] * 2,
        compiler_params=_params(1),
    )(dy, proj, proj, lx, hl, hl, *gates, conv_w, wa_bd, wx_bd, lam, g_lru)


def _ssd_chunk_terms(xc, dtp, bias, alog, expand):
    sg = _sigmoid(xc)
    xbc = xc * sg
    pre = dtp + bias
    dt = _softplus(pre)
    A = -jnp.exp(alog)
    ri = lax.broadcasted_iota(jnp.int32, (CHUNK, CHUNK), 0)
    ci = lax.broadcasted_iota(jnp.int32, (CHUNK, CHUNK), 1)
    tril = (ri >= ci).astype(BF16)
    cs = _dot_sel(dt * A, tril, terms=3, sel_first=True)
    cs_last = _row(cs, CHUNK - 1)
    ecs = jnp.exp(cs)
    dec = jnp.exp(cs_last - cs)
    return dict(sg=sg, xbc=xbc, pre=pre, dt=dt, A=A, cs=cs, csT=cs.T, ecs=ecs, dec=dec, ri=ri, ci=ci,
                dt_e=_dot_sel(dt, expand), ecs_e=_dot_sel(ecs, expand), dec_e=_dot_sel(dec, expand))


def _head_lambda(t, h):
    col = jnp.sum(jnp.where(t["ci"] == h, t["cs"], 0.0), axis=1, keepdims=True)
    rowv = jnp.sum(jnp.where(t["ri"] == h, t["csT"], 0.0), axis=0, keepdims=True)
    return jnp.exp(jnp.where(t["ri"] >= t["ci"], col - rowv, -1e30))


def _ssd_fwd(proj, dtp, conv_w, conv_b, dt_bias, a_log, d_e, g_ssd, expand):
    T = proj.shape[0]
    nc = T // CHUNK

    def body(z_ref, xp_ref, dtp_ref, cw_ref, cb_ref, bias_ref, alog_ref, de_ref, g_ref, ex_ref,
             xc_ref, y_ref, yn_ref, sprev_ref, tail, S):
        @pl.when(pl.program_id(0) == 0)
        def _():
            tail[...] = jnp.zeros_like(tail)
            S[...] = jnp.zeros_like(S)

        xp = xp_ref[...]
        prev8 = tail[...]
        xc = cb_ref[...] + cw_ref[3:4, :] * xp
        for j in range(1, 4):
            xc += cw_ref[3 - j:4 - j, :] * _shift_down(xp, prev8, j)
        tail[...] = xp[CHUNK - 8:CHUNK]
        xc_ref[...] = xc
        t = _ssd_chunk_terms(xc, dtp_ref[...], bias_ref[...], alog_ref[...], ex_ref[...])
        xbc = t["xbc"]
        sx = xbc[:, 0:SSD_INNER]
        Bb = xbc[:, SSD_INNER:SSD_INNER + 256].astype(BF16)
        Cb = xbc[:, SSD_INNER + 256:SSD_CONV_CH].astype(BF16)
        X = t["dt_e"] * sx
        lane = lax.broadcasted_iota(jnp.int32, (CHUNK, LANE), 1)
        G = [_dot(Cb[:, 128 * g:128 * (g + 1)], Bb[:, 128 * g:128 * (g + 1)], NT) for g in range(SSD_GROUPS)]
        for k in range(SSD_HEADS // 2):
            Xp = X[:, 128 * k:128 * (k + 1)]
            acc = jnp.zeros((CHUNK, LANE), F32)
            for half in range(2):
                M = (G[k // 4] * _head_lambda(t, 2 * k + half)).astype(BF16)
                Xh = jnp.where((lane >= 64) if half else (lane < 64), Xp, 0.0).astype(BF16)
                acc += _dot(M, Xh)
            y_ref[:, 128 * k:128 * (k + 1)] = acc
        sprev_ref[0] = S[...]
        eL_e = _row(t["ecs_e"], CHUNK - 1)
        Xd = (X * t["dec_e"]).astype(BF16)
        for g in range(SSD_GROUPS):
            sl = slice(512 * g, 512 * (g + 1))
            Sg = S[:, sl]
            y_ref[:, sl] += t["ecs_e"][:, sl] * _dot(Cb[:, 128 * g:128 * (g + 1)], Sg.astype(BF16))
            S[:, sl] = eL_e[:, sl] * Sg + _dot(Bb[:, 128 * g:128 * (g + 1)], Xd[:, sl], TN)
        y = y_ref[...] + de_ref[...] * sx
        y_ref[...] = y
        z = z_ref[...]
        q = y * (z * _sigmoid(z))
        yn_ref[...] = (q * _rms(q) * g_ref[...]).astype(BF16)

    c0 = lambda i: (0, 0)
    return pl.pallas_call(
        body, name="ssd_fwd", grid=(nc,),
        in_specs=[pl.BlockSpec((CHUNK, SSD_INNER), lambda i: (i, 2)),
                  pl.BlockSpec((CHUNK, SSD_CONV_CH), lambda i: (i, 2)),
                  pl.BlockSpec((CHUNK, LANE), lambda i: (i, 0)),
                  pl.BlockSpec((4, SSD_CONV_CH), c0), pl.BlockSpec((1, SSD_CONV_CH), c0),
                  pl.BlockSpec((1, LANE), c0), pl.BlockSpec((1, LANE), c0),
                  pl.BlockSpec((1, SSD_INNER), c0), pl.BlockSpec((1, SSD_INNER), c0),
                  pl.BlockSpec((LANE, SSD_INNER), c0)],
        out_specs=[pl.BlockSpec((CHUNK, SSD_CONV_CH), lambda i: (i, 0)),
                   pl.BlockSpec((CHUNK, SSD_INNER), lambda i: (i, 0)),
                   pl.BlockSpec((CHUNK, SSD_INNER), lambda i: (i, 0)),
                   pl.BlockSpec((1, SSD_STATE, SSD_INNER), lambda i: (i, 0, 0))],
        out_shape=[jax.ShapeDtypeStruct((T, SSD_CONV_CH), F32), jax.ShapeDtypeStruct((T, SSD_INNER), F32),
                   jax.ShapeDtypeStruct((T, SSD_INNER), BF16),
                   jax.ShapeDtypeStruct((nc, SSD_STATE, SSD_INNER), F32)],
        scratch_shapes=[pltpu.VMEM((8, SSD_CONV_CH), F32), pltpu.VMEM((SSD_STATE, SSD_INNER), F32)],
        compiler_params=_params(1),
    )(proj, proj, dtp, conv_w, conv_b, dt_bias, a_log, d_e, g_ssd, expand)


def _ssd_bwd(dyn, proj, dtp, xc, y, sprev, conv_w, dt_bias, a_log, d_e, g_ssd, expand):
    T = proj.shape[0]
    nc = T // CHUNK

    def body(dyn_ref, z_ref, xp_ref, dtp_ref, xc_ref, y_ref, sprev_ref, cw_ref, bias_ref, alog_ref, de_ref,
             g_ref, ex_ref, dp_ref, ddtp_ref, dcw_ref, dcb_ref, dbias_ref, dA_ref, dD_ref, dg_ref,
             dS, head, dX_s, dxbc_s):
        @pl.when(pl.program_id(0) == 0)
        def _():
            dS[...] = jnp.zeros_like(dS)
            head[...] = jnp.zeros_like(head)
            for ref in (dcw_ref, dcb_ref, dbias_ref, dA_ref, dD_ref, dg_ref):
                ref[...] = jnp.zeros_like(ref)

        ex = ex_ref[...]
        xc = xc_ref[...]
        t = _ssd_chunk_terms(xc, dtp_ref[...], bias_ref[...], alog_ref[...], ex)
        ri, ci = t["ri"], t["ci"]
        xbc = t["xbc"]
        sx = xbc[:, 0:SSD_INNER]
        Bb = xbc[:, SSD_INNER:SSD_INNER + 256].astype(BF16)
        Cb = xbc[:, SSD_INNER + 256:SSD_CONV_CH].astype(BF16)
        X = t["dt_e"] * sx
        z = z_ref[...]
        sz = _sigmoid(z)
        siluz = z * sz
        yv = y_ref[...]
        q = yv * siluz
        rn = _rms(q)
        dynv = dyn_ref[...]
        dg_ref[...] += jnp.sum(dynv * q * rn, axis=0, keepdims=True)
        dq = _rms_bwd(dynv * g_ref[...], q, rn)
        dp_ref[:, 0:SSD_INNER] = (dq * yv * (sz * (1.0 + z * (1.0 - sz)))).astype(BF16)
        dY = dq * siluz
        dD_ref[...] += jnp.sum(dY * sx, axis=0, keepdims=True)
        dYb = dY.astype(BF16)
        lane = lax.broadcasted_iota(jnp.int32, (CHUNK, LANE), 1)
        dcs = jnp.zeros((CHUNK, CHUNK), F32)
        dcsT = jnp.zeros((CHUNK, CHUNK), F32)
        Xb = X.astype(BF16)
        for g in range(SSD_GROUPS):
            Bg = Bb[:, 128 * g:128 * (g + 1)]
            Cg = Cb[:, 128 * g:128 * (g + 1)]
            G = _dot(Cg, Bg, NT)
            dGsum = jnp.zeros((CHUNK, CHUNK), F32)
            for k in range(4 * g, 4 * g + 4):
                Xp = Xb[:, 128 * k:128 * (k + 1)]
                dYp = dY[:, 128 * k:128 * (k + 1)]
                dXp = jnp.zeros((CHUNK, LANE), F32)
                for half in range(2):
                    h = 2 * k + half
                    lam = _head_lambda(t, h)
                    M = G * lam
                    dYh = jnp.where((lane >= 64) if half else (lane < 64), dYp, 0.0).astype(BF16)
                    dM = _dot(dYh, Xp, NT)
                    W = dM * M
                    dcs += jnp.where(ci == h, jnp.sum(W, axis=1, keepdims=True), 0.0)
                    dcsT += jnp.where(ri == h, jnp.sum(W, axis=0, keepdims=True), 0.0)
                    dGsum += dM * lam
                    dXp += _dot(M.astype(BF16), dYh, TN)
                dX_s[:, 128 * k:128 * (k + 1)] = dXp
            dGb = dGsum.astype(BF16)
            dxbc_s[:, SSD_INNER + 256 + 128 * g:SSD_INNER + 256 + 128 * (g + 1)] = _dot(dGb, Bg)
            dxbc_s[:, SSD_INNER + 128 * g:SSD_INNER + 128 * (g + 1)] = _dot(dGb, Cg, TN)
        dcs = dcs - dcsT.T
        Sp = sprev_ref[0]
        dSv = dS[...]
        ecs_e, dec_e = t["ecs_e"], t["dec_e"]
        eL_e = _row(ecs_e, CHUNK - 1)
        dYe = dY * ecs_e
        dYeb = dYe.astype(BF16)
        Xd = X * dec_e
        Xdb = Xd.astype(BF16)
        for g in range(SSD_GROUPS):
            sl = slice(512 * g, 512 * (g + 1))
            Bg = Bb[:, 128 * g:128 * (g + 1)]
            Cg = Cb[:, 128 * g:128 * (g + 1)]
            Spb = Sp[:, sl].astype(BF16)
            dSb = dSv[:, sl].astype(BF16)
            CS = _dot(Cg, Spb)
            BS = _dot(Bg, dSb)
            dxbc_s[:, SSD_INNER + 256 + 128 * g:SSD_INNER + 256 + 128 * (g + 1)] += _dot(dYeb[:, sl], Spb, NT)
            dxbc_s[:, SSD_INNER + 128 * g:SSD_INNER + 128 * (g + 1)] += _dot(Xdb[:, sl], dSb, NT)
            dS[:, sl] = eL_e[:, sl] * dSv[:, sl] + _dot(Cg, dYeb[:, sl], TN)
            dX_s[:, sl] += dec_e[:, sl] * BS
            dcs += _dot_sel(dYe[:, sl] * CS, ex[:, sl], NT)
            tdec = _dot_sel(X[:, sl] * BS, ex[:, sl], NT) * t["dec"]
            dcs -= tdec
            last = jnp.sum(tdec, axis=0, keepdims=True)
            last += jnp.sum(_dot_sel(Sp[:, sl] * dSv[:, sl], ex[:, sl], NT), axis=0, keepdims=True) \
                * _row(t["ecs"], CHUNK - 1)
            dcs += jnp.where(ri == CHUNK - 1, last, 0.0)
        triu = (ci >= ri).astype(BF16)
        da = _dot_sel(dcs, triu, terms=3, sel_first=True)
        dX = dX_s[...]
        ddt = da * t["A"] + _dot_sel(dX * sx, ex, NT)
        dA_ref[...] += jnp.sum(da * t["dt"], axis=0, keepdims=True)
        ddtp = ddt * _sigmoid(t["pre"])
        dbias_ref[...] += jnp.sum(ddtp, axis=0, keepdims=True)
        ddtp_ref[...] = ddtp.astype(BF16)
        dxbc_s[:, 0:SSD_INNER] = dX * t["dt_e"] + de_ref[...] * dY
        sg = t["sg"]
        dxc = dxbc_s[...] * (sg * (1.0 + xc * (1.0 - sg)))
        xp = xp_ref[...]
        nxt = head[...]
        dcb_ref[...] += jnp.sum(dxc, axis=0, keepdims=True)
        dpre = cw_ref[3:4, :] * dxc
        dcw_ref[3:4, :] += jnp.sum(xp * dxc, axis=0, keepdims=True)
        for j in range(1, 4):
            sh = _shift_up(dxc, nxt, j)
            dpre += cw_ref[3 - j:4 - j, :] * sh
            dcw_ref[3 - j:4 - j, :] += jnp.sum(xp * sh, axis=0, keepdims=True)
        head[...] = dxc[0:8]
        dp_ref[:, SSD_INNER:SSD_INNER + SSD_CONV_CH] = dpre.astype(BF16)

    c0 = lambda i: (0, 0)
    rev = lambda i: (nc - 1 - i, 0)
    return pl.pallas_call(
        body, name="ssd_bwd", grid=(nc,),
        in_specs=[pl.BlockSpec((CHUNK, SSD_INNER), rev),
                  pl.BlockSpec((CHUNK, SSD_INNER), lambda i: (nc - 1 - i, 2)),
                  pl.BlockSpec((CHUNK, SSD_CONV_CH), lambda i: (nc - 1 - i, 2)),
                  pl.BlockSpec((CHUNK, LANE), rev),
                  pl.BlockSpec((CHUNK, SSD_CONV_CH), rev),
                  pl.BlockSpec((CHUNK, SSD_INNER), rev),
                  pl.BlockSpec((1, SSD_STATE, SSD_INNER), lambda i: (nc - 1 - i, 0, 0)),
                  pl.BlockSpec((4, SSD_CONV_CH), c0), pl.BlockSpec((1, LANE), c0), pl.BlockSpec((1, LANE), c0),
                  pl.BlockSpec((1, SSD_INNER), c0), pl.BlockSpec((1, SSD_INNER), c0),
                  pl.BlockSpec((LANE, SSD_INNER), c0)],
        out_specs=[pl.BlockSpec((CHUNK, 2560), rev), pl.BlockSpec((CHUNK, LANE), rev),
                   pl.BlockSpec((4, SSD_CONV_CH), c0), pl.BlockSpec((1, SSD_CONV_CH), c0),
                   pl.BlockSpec((1, LANE), c0), pl.BlockSpec((1, LANE), c0),
                   pl.BlockSpec((1, SSD_INNER), c0), pl.BlockSpec((1, SSD_INNER), c0)],
        out_shape=[jax.ShapeDtypeStruct((T, 2560), BF16), jax.ShapeDtypeStruct((T, LANE), BF16),
                   jax.ShapeDtypeStruct((4, SSD_CONV_CH), F32), jax.ShapeDtypeStruct((1, SSD_CONV_CH), F32),
                   jax.ShapeDtypeStruct((1, LANE), F32), jax.ShapeDtypeStruct((1, LANE), F32),
                   jax.ShapeDtypeStruct((1, SSD_INNER), F32), jax.ShapeDtypeStruct((1, SSD_INNER), F32)],
        scratch_shapes=[pltpu.VMEM((SSD_STATE, SSD_INNER), F32), pltpu.VMEM((8, SSD_CONV_CH), F32),
                        pltpu.VMEM((CHUNK, SSD_INNER), F32), pltpu.VMEM((CHUNK, SSD_CONV_CH), F32)],
        compiler_params=_params(1),
    )(dyn, proj, proj, dtp, xc, y, sprev, conv_w, dt_bias, a_log, d_e, g_ssd, expand)


def _outproj_fwd(x, y_lru, y_ssd, w_out, g_pm, g_pf):
    T = x.shape[0]
    TT = TP

    def body(x_ref, yl_ref, ys_ref, wo_ref, gpm_ref, gpf_ref, mix_ref, x1_ref, h2_ref):
        mix = _dot(yl_ref[...], wo_ref[0:LRU_W, :]) + _dot(ys_ref[...], wo_ref[LRU_W:2 * LRU_W, :])
        mix_ref[...] = mix
        x1 = x_ref[...] + mix * _rms(mix) * gpm_ref[...]
        x1_ref[...] = x1
        h2_ref[...] = (x1 * _rms(x1) * gpf_ref[...]).astype(BF16)

    tile = pl.BlockSpec((TT, D_MODEL), lambda i: (i, 0))
    vec = pl.BlockSpec((1, D_MODEL), lambda i: (0, 0))
    return pl.pallas_call(
        body, name="outproj_fwd", grid=(T // TT,),
        in_specs=[tile, tile, tile, pl.BlockSpec((2 * LRU_W, D_MODEL), lambda i: (0, 0)), vec, vec],
        out_specs=[tile, tile, tile],
        out_shape=[jax.ShapeDtypeStruct((T, D_MODEL), F32), jax.ShapeDtypeStruct((T, D_MODEL), F32),
                   jax.ShapeDtypeStruct((T, D_MODEL), BF16)],
        compiler_params=_params(1),
    )(x, y_lru, y_ssd, w_out, g_pm, g_pf)


def _ffn_fwd_bwd(x1, h2, target, w_gate, w_up, w_down, g_pf, g_ff):
    T = x1.shape[0]

    def body(x1_ref, h2_ref, tg_ref, wg_hbm, wu_hbm, wd_hbm, gpf_ref, gff_ref,
             dx1_ref, act_ref, df_ref, dgt_ref, dup_ref, dgpf_ref, dgff_ref, loss_ref,
             wg, wu, wd, gt_s, up_s, sem):
        @pl.when(pl.program_id(0) == 0)
        def _():
            cps = [pltpu.make_async_copy(s, d, sem.at[n]) for n, (s, d) in
                   enumerate(((wg_hbm, wg), (wu_hbm, wu), (wd_hbm, wd)))]
            for c in cps:
                c.start()
            for c in cps:
                c.wait()
            for ref in (dgpf_ref, dgff_ref, loss_ref):
                ref[...] = jnp.zeros_like(ref)

        h2 = h2_ref[...]
        f = jnp.zeros((TT, D_MODEL), F32)
        for c in range(D_FF // FF_CHUNK):
            sl = slice(FF_CHUNK * c, FF_CHUNK * (c + 1))
            gt = _dot(h2, wg[:, sl])
            up = _dot(h2, wu[:, sl])
            gt_s[:, sl] = gt
            up_s[:, sl] = up
            act = (gt * _sigmoid(gt) * up).astype(BF16)
            act_ref[:, sl] = act
            f += _dot(act, wd[sl, :])
        x1 = x1_ref[...]
        rnf = _rms(f)
        e = x1 + f * rnf * gff_ref[...] - tg_ref[...]
        part = 0.5 * jnp.sum(jnp.sum(e * e, axis=1, keepdims=True), axis=0, keepdims=True) * (1.0 / D_MODEL)
        lane = lax.broadcasted_iota(jnp.int32, (1, LANE), 1)
        loss_ref[...] += jnp.where(lane == 0, part, 0.0)
        dx2 = e * (1.0 / D_MODEL)
        dgff_ref[...] += jnp.sum(dx2 * f * rnf, axis=0, keepdims=True)
        df = _rms_bwd(dx2 * gff_ref[...], f, rnf).astype(BF16)
        df_ref[...] = df
        dh2 = jnp.zeros((TT, D_MODEL), F32)
        for c in range(D_FF // FF_CHUNK):
            sl = slice(FF_CHUNK * c, FF_CHUNK * (c + 1))
            dact = _dot(df, wd[sl, :], NT)
            gt = gt_s[:, sl]
            up = up_s[:, sl]
            sg = _sigmoid(gt)
            dgt = (dact * up * (sg * (1.0 + gt * (1.0 - sg)))).astype(BF16)
            dup = (dact * gt * sg).astype(BF16)
            dgt_ref[:, sl] = dgt
            dup_ref[:, sl] = dup
            dh2 += _dot(dgt, wg[:, sl], NT) + _dot(dup, wu[:, sl], NT)
        rn2 = _rms(x1)
        dgpf_ref[...] += jnp.sum(dh2 * x1 * rn2, axis=0, keepdims=True)
        dx1_ref[...] = dx2 + _rms_bwd(dh2 * gpf_ref[...], x1, rn2)

    tile = pl.BlockSpec((TT, D_MODEL), lambda i: (i, 0))
    wide = pl.BlockSpec((TT, D_FF), lambda i: (i, 0))
    vec = pl.BlockSpec((1, D_MODEL), lambda i: (0, 0))
    hbm = pl.BlockSpec(memory_space=pl.ANY)
    return pl.pallas_call(
        body, name="ffn_fwd_bwd", grid=(T // TT,),
        in_specs=[tile, tile, tile, hbm, hbm, hbm, vec, vec],
        out_specs=[tile, wide, tile, wide, wide, vec, vec, pl.BlockSpec((1, LANE), lambda i: (0, 0))],
        out_shape=[jax.ShapeDtypeStruct((T, D_MODEL), F32), jax.ShapeDtypeStruct((T, D_FF), BF16),
                   jax.ShapeDtypeStruct((T, D_MODEL), BF16), jax.ShapeDtypeStruct((T, D_FF), BF16),
                   jax.ShapeDtypeStruct((T, D_FF), BF16), jax.ShapeDtypeStruct((1, D_MODEL), F32),
                   jax.ShapeDtypeStruct((1, D_MODEL), F32), jax.ShapeDtypeStruct((1, LANE), F32)],
        scratch_shapes=[pltpu.VMEM((D_MODEL, D_FF), BF16), pltpu.VMEM((D_MODEL, D_FF), BF16),
                        pltpu.VMEM((D_FF, D_MODEL), BF16), pltpu.VMEM((TT, D_FF), F32),
                        pltpu.VMEM((TT, D_FF), F32), pltpu.SemaphoreType.DMA((3,))],
        compiler_params=_params(1),
    )(x1, h2, target, w_gate, w_up, w_down, g_pf, g_ff)


def _outproj_bwd(dx1, mix, w_out, g_pm):
    T = dx1.shape[0]
    TT = TP

    def body(dx1_ref, mix_ref, wo_ref, gpm_ref, dyl_ref, dys_ref, dmix_ref, dg_ref):
        @pl.when(pl.program_id(0) == 0)
        def _():
            dg_ref[...] = jnp.zeros_like(dg_ref)

        mix = mix_ref[...]
        rn = _rms(mix)
        dx1v = dx1_ref[...]
        dg_ref[...] += jnp.sum(dx1v * mix * rn, axis=0, keepdims=True)
        dmix = _rms_bwd(dx1v * gpm_ref[...], mix, rn).astype(BF16)
        dmix_ref[...] = dmix
        dyl_ref[...] = _dot(dmix, wo_ref[0:LRU_W, :], NT)
        dys_ref[...] = _dot(dmix, wo_ref[LRU_W:2 * LRU_W, :], NT)

    tile = pl.BlockSpec((TT, D_MODEL), lambda i: (i, 0))
    vec = pl.BlockSpec((1, D_MODEL), lambda i: (0, 0))
    return pl.pallas_call(
        body, name="outproj_bwd", grid=(T // TT,),
        in_specs=[tile, tile, pl.BlockSpec((2 * LRU_W, D_MODEL), lambda i: (0, 0)), vec],
        out_specs=[tile, tile, tile, vec],
        out_shape=[jax.ShapeDtypeStruct((T, D_MODEL), F32), jax.ShapeDtypeStruct((T, D_MODEL), F32),
                   jax.ShapeDtypeStruct((T, D_MODEL), BF16), jax.ShapeDtypeStruct((1, D_MODEL), F32)],
        compiler_params=_params(1),
    )(dx1, mix, w_out, g_pm)


def _tn_matmul(a, bs, name, tk=512):
    T, M = a.shape
    nk = T // tk
    nb = len(bs)

    def body(*refs):
        a_ref, b_refs, o_refs, accs = refs[0], refs[1:1 + nb], refs[1 + nb:1 + 2 * nb], refs[1 + 2 * nb:]
        k = pl.program_id(0)

        @pl.when(k == 0)
        def _():
            for acc in accs:
                acc[...] = jnp.zeros_like(acc)

        av = a_ref[...].astype(BF16)
        for b_ref, acc in zip(b_refs, accs):
            acc[...] += _dot(av, b_ref[...], TN)

        @pl.when(k == nk - 1)
        def _():
            for o_ref, acc in zip(o_refs, accs):
                o_ref[...] = acc[...].astype(BF16)

    return pl.pallas_call(
        body, name=name, grid=(nk,),
        in_specs=[pl.BlockSpec((tk, M), lambda k: (k, 0))]
                 + [pl.BlockSpec((tk, b.shape[1]), lambda k: (k, 0)) for b in bs],
        out_specs=[pl.BlockSpec((M, b.shape[1]), lambda k: (0, 0)) for b in bs],
        out_shape=[jax.ShapeDtypeStruct((M, b.shape[1]), BF16) for b in bs],
        scratch_shapes=[pltpu.VMEM((M, b.shape[1]), F32) for b in bs],
        compiler_params=_params(1),
    )(a, *bs)


def _tn_matmul_slabs(a, bs, name, slab, tk=512):
    T, M = a.shape
    nk = T // tk
    nb = len(bs)
    offs = [sum(b.shape[1] for b in bs[:i]) for i in range(nb + 1)]

    def body(*refs):
        a_ref, b_refs, o_ref, acc = refs[0], refs[1:1 + nb], refs[1 + nb], refs[2 + nb]
        k = pl.program_id(0)

        @pl.when(k == 0)
        def _():
            acc[...] = jnp.zeros_like(acc)

        av = a_ref[...].astype(BF16)
        for i, b_ref in enumerate(b_refs):
            acc[:, offs[i]:offs[i + 1]] += _dot(av, b_ref[...], TN)

        @pl.when(k == nk - 1)
        def _():
            for j in range(N_DEV):
                o_ref[j] = acc[:, slab * j:slab * (j + 1)].astype(BF16)

    return pl.pallas_call(
        body, name=name, grid=(nk,),
        in_specs=[pl.BlockSpec((tk, M), lambda k: (k, 0))]
                 + [pl.BlockSpec((tk, b.shape[1]), lambda k: (k, 0)) for b in bs],
        out_specs=pl.BlockSpec((N_DEV, M, slab), lambda k: (0, 0, 0)),
        out_shape=jax.ShapeDtypeStruct((N_DEV, M, slab), BF16),
        scratch_shapes=[pltpu.VMEM((M, offs[-1]), F32)],
        compiler_params=_params(1),
    )(a, *bs)


def _tn_blockdiag(a, b1, b2, name, tk=1024):
    T = a.shape[0]
    tk = min(tk, T)

    def body(a_ref, b1_ref, b2_ref, o1_ref, o2_ref):
        @pl.when(pl.program_id(0) == 0)
        def _():
            o1_ref[...] = jnp.zeros_like(o1_ref)
            o2_ref[...] = jnp.zeros_like(o2_ref)

        for j in range(4):
            sl = slice(256 * j, 256 * (j + 1))
            av = a_ref[:, sl].astype(BF16)
            o1_ref[j] += _dot(av, b1_ref[:, sl], TN)
            o2_ref[j] += _dot(av, b2_ref[:, sl], TN)

    blk = pl.BlockSpec((tk, LRU_W), lambda k: (k, 0))
    out = pl.BlockSpec((4, 256, 256), lambda k: (0, 0, 0))
    return pl.pallas_call(
        body, name=name, grid=(T // tk,),
        in_specs=[blk, blk, blk], out_specs=[out, out],
        out_shape=[jax.ShapeDtypeStruct((4, 256, 256), F32)] * 2,
        compiler_params=_params(1),
    )(a, b1, b2)


def _adamw_update(g, w, m, v):
    nm = ADAM_B1 * m + (1.0 - ADAM_B1) * g
    nv = ADAM_B2 * v + (1.0 - ADAM_B2) * (g * g)
    m_hat = nm / (1.0 - ADAM_B1 ** ADAM_STEP)
    v_hat = nv / (1.0 - ADAM_B2 ** ADAM_STEP)
    return -ADAM_LR * (m_hat / (jnp.sqrt(v_hat) + ADAM_EPS) + ADAM_WD * w), nm, nv


def _adamw_small(parts, params):
    n = len(params)

    def body(*refs):
        p_ref, wmv = refs[0], refs[1:1 + 3 * n]
        gs_ref, outs = refs[1 + 3 * n], refs[2 + 3 * n:]
        g = p_ref[0]
        for k in range(1, N_DEV):
            g = g + p_ref[k]
        gs_ref[...] = g
        for i, (row, w, _, _) in enumerate(params):
            rows, width = w.shape
            if width <= 1024:
                gp = gs_ref[row:row + rows, 0:width]
            else:
                gp = jnp.concatenate([gs_ref[row:row + 1, :], gs_ref[row + 1:row + 2, 0:width - 1024]], axis=1)
            d, nm, nv = _adamw_update(gp, wmv[3 * i][...], wmv[3 * i + 1][...], wmv[3 * i + 2][...])
            for ref, val in zip(outs[4 * i:4 * i + 4], (gp, d, nm, nv)):
                ref[...] = val

    full = lambda s: pl.BlockSpec(s, lambda: (0,) * len(s))
    shapes = [w.shape for _, w, _, _ in params]
    return pl.pallas_call(
        body, name="adamw_small",
        in_specs=[full(parts.shape)] + [full(s) for s in shapes for _ in range(3)],
        out_specs=[full(parts.shape[1:])] + [full(s) for s in shapes for _ in range(4)],
        out_shape=[jax.ShapeDtypeStruct(parts.shape[1:], F32)]
                  + [jax.ShapeDtypeStruct(s, F32) for s in shapes for _ in range(4)],
        compiler_params=pltpu.CompilerParams(vmem_limit_bytes=VMEM_LIMIT),
    )(parts, *[a for _, w, m, v in params for a in (w, m, v)])


def _adamw(parts, w, m, v, name, tr):
    P, R, C = parts.shape

    def body(p_ref, w_ref, m_ref, v_ref, g_ref, d_ref, nm_ref, nv_ref):
        g = p_ref[0].astype(F32)
        for k in range(1, P):
            g = g + p_ref[k].astype(F32)
        g_ref[...] = g
        d_ref[...], nm_ref[...], nv_ref[...] = _adamw_update(g, w_ref[...], m_ref[...], v_ref[...])

    blk = pl.BlockSpec((tr, C), lambda i: (i, 0))
    return pl.pallas_call(
        body, name=name, grid=(R // tr,),
        in_specs=[pl.BlockSpec((P, tr, C), lambda i: (0, i, 0)), blk, blk, blk],
        out_specs=[blk, blk, blk, blk],
        out_shape=[jax.ShapeDtypeStruct((R, C), F32)] * 4,
        compiler_params=_params(1),
    )(parts, w, m, v)


def _peer(k):
    x, y, c = lax.axis_index("x"), lax.axis_index("y"), lax.axis_index("c")
    px = x ^ ((k >> 2) & 1)
    py = y ^ ((k >> 1) & 1)
    pc = c ^ (k & 1)
    return (px, py, pc), 4 * px + 2 * py + pc


def _my_block():
    return 4 * lax.axis_index("x") + 2 * lax.axis_index("y") + lax.axis_index("c")


def _all_gather(shards, name):
    n = len(shards)

    def body(*refs):
        ins, outs = refs[:n], refs[n:2 * n]
        send, recv, loc = refs[2 * n:]
        x, y, c = lax.axis_index("x"), lax.axis_index("y"), lax.axis_index("c")
        sibling = (x, y, 1 - c)
        chips = [(1 - x, y), (x, 1 - y), (1 - x, 1 - y)]
        slot = lambda px, py, pc: 4 * px + 2 * py + pc

        def copy(a, k, block, to, src=None):
            dst = outs[a].at[slot(*block)]
            return pltpu.make_async_remote_copy(
                src_ref=dst if src is None else src, dst_ref=dst, send_sem=send.at[a, k], recv_sem=recv.at[a, k],
                device_id=to, device_id_type=pl.DeviceIdType.MESH)

        mine = [pltpu.make_async_copy(ins[a], outs[a].at[slot(x, y, c)], loc.at[a]) for a in range(n)]
        for cp in mine:
            cp.start()
        first = []
        for a in range(n):
            first.append(copy(a, 0, (x, y, c), sibling, src=ins[a]))
            first += [copy(a, 1 + j, (x, y, c), (*chip, c), src=ins[a]) for j, chip in enumerate(chips)]
        for cp in first:
            cp.start()
        passed = []
        for j, chip in enumerate(chips):
            for a in range(n):
                copy(a, 1 + j, (*chip, c), (x, y, c)).wait_recv()
                fwd = copy(a, 4 + j, (*chip, c), sibling)
                fwd.start()
                passed.append(fwd)
        for a in range(n):
            copy(a, 0, sibling, (x, y, c)).wait_recv()
            for j, chip in enumerate(chips):
                copy(a, 4 + j, (*chip, 1 - c), (x, y, c)).wait_recv()
        for cp in first + passed:
            cp.wait_send()
        for cp in mine:
            cp.wait()

    hbm = pl.BlockSpec(memory_space=pl.ANY)
    return pl.pallas_call(
        body, name=name,
        in_specs=[hbm] * n, out_specs=[hbm] * n,
        out_shape=[jax.ShapeDtypeStruct((N_DEV,) + s.shape, s.dtype) for s in shards],
        scratch_shapes=[pltpu.SemaphoreType.DMA((n, N_DEV - 1)), pltpu.SemaphoreType.DMA((n, N_DEV - 1)),
                        pltpu.SemaphoreType.DMA((n,))],
    )(*shards)


_HBM = pl.BlockSpec(memory_space=pltpu.HBM)
_SEM = pl.BlockSpec(memory_space=pltpu.SEMAPHORE)
_EFFECT = pltpu.SideEffectType.DATAFLOW_SIDE_EFFECTING


def _direct_copies(srcs, lands, send, recv, slab_source):
    me = _my_block()
    cps = []
    for k in range(1, N_DEV):
        to, blk = _peer(k)
        for a, (src, land) in enumerate(zip(srcs, lands)):
            if slab_source:
                s, d = src.at[blk], land.at[me]
            elif land.ndim == 3:
                s, d = src, land.at[me]
            else:
                s, d = src, land.at[pl.ds(pl.multiple_of(me * src.shape[0], 16), src.shape[0]), :]
            cps.append(pltpu.make_async_remote_copy(
                src_ref=s, dst_ref=d,
                send_sem=send.at[a * (N_DEV - 1) + k - 1], recv_sem=recv.at[a * (N_DEV - 1) + k - 1],
                device_id=to, device_id_type=pl.DeviceIdType.MESH))
    return cps


def _exchange_start(srcs, name, slab_source, axes=None):
    n = len(srcs)
    if slab_source:
        shapes = [s.shape for s in srcs]
    else:
        shapes = [(N_DEV,) + s.shape if ax == 1 else (N_DEV * s.shape[0], s.shape[1]) for s, ax in zip(srcs, axes)]
    lands = [pltpu.with_memory_space_constraint(lax.empty(sh, s.dtype), pltpu.HBM) for sh, s in zip(shapes, srcs)]

    def body(*refs):
        ins, land_in = refs[:n], refs[n:2 * n]
        send, recv = refs[2 * n], refs[2 * n + 1]
        token = refs[4 * n + 2]
        for cp in _direct_copies(ins, land_in, send, recv, slab_source):
            cp.start()
        token[...] = jnp.zeros_like(token)

    sems = pltpu.SemaphoreType.DMA((n * (N_DEV - 1),))
    res = pl.pallas_call(
        body, name=name,
        out_shape=(sems, sems, *[pltpu.HBM(s.shape, s.dtype) for s in srcs],
                   *[pltpu.HBM(l.shape, l.dtype) for l in lands], jax.ShapeDtypeStruct((8, LANE), F32)),
        in_specs=[_HBM] * (2 * n),
        out_specs=(_SEM, _SEM, *[_HBM] * (2 * n), pl.BlockSpec(memory_space=pltpu.VMEM)),
        input_output_aliases={i: 2 + i for i in range(2 * n)},
        compiler_params=pltpu.CompilerParams(has_side_effects=_EFFECT),
    )(*[pltpu.with_memory_space_constraint(s, pltpu.HBM) for s in srcs], *lands)
    return dict(send=res[0], recv=res[1], srcs=res[2:2 + n], lands=res[2 + n:2 + 2 * n], token=res[-1],
                slab_source=slab_source)


def _exchange_wait(ex, after, name):
    n = len(ex["srcs"])
    slab_source = ex["slab_source"]

    def body(*refs):
        ins, lands = refs[:n], refs[n:2 * n]
        send, recv = refs[2 * n], refs[2 * n + 1]
        for cp in _direct_copies(ins, lands, send, recv, slab_source):
            cp.wait_send()
            cp.wait_recv()

    res = pl.pallas_call(
        body, name=name,
        out_shape=tuple(pltpu.HBM(s.shape, s.dtype) for s in list(ex["srcs"]) + list(ex["lands"])),
        in_specs=[_HBM] * (2 * n) + [_SEM, _SEM, pl.BlockSpec(memory_space=pl.ANY)],
        out_specs=tuple([_HBM] * (2 * n)),
        input_output_aliases={i: i for i in range(2 * n)},
        compiler_params=pltpu.CompilerParams(has_side_effects=_EFFECT),
    )(*ex["srcs"], *ex["lands"], ex["send"], ex["recv"], after)
    me = _my_block()
    out = []
    for src, land in zip(res[:n], res[n:]):
        if slab_source:
            own, at = lax.dynamic_index_in_dim(src, me, 0, keepdims=True), (me, 0, 0)
        elif land.ndim == 3:
            own, at = src[None], (me, 0, 0)
        else:
            own, at = src, (me * src.shape[0], 0)
        out.append(lax.dynamic_update_slice(land, own, at))
    return out


BIG = ("w_in", "w_out", "w_gate", "w_up", "w_down")
BIG_SHARD = {"w_in": (1024, 578), "w_out": (256, 1024), "w_gate": (1024, 352), "w_up": (1024, 352),
             "w_down": (352, 1024)}
BIG_SHARD_AXIS = {"w_in": 1, "w_out": 0, "w_gate": 1, "w_up": 1, "w_down": 0}
BIG_ADAM_ROWS = {"w_in": 256, "w_out": 128, "w_gate": 256, "w_up": 256, "w_down": 176}


def _join(parts, axis):
    if axis == 0:
        return parts.reshape((-1,) + parts.shape[2:])
    return jnp.concatenate([parts[j] for j in range(N_DEV)], axis=1)


def _split(full, axis):
    if axis == 0:
        return full.reshape((N_DEV, full.shape[0] // N_DEV) + full.shape[1:])
    c = full.shape[1] // N_DEV
    return jnp.stack([full[:, c * j:c * (j + 1)] for j in range(N_DEV)])


SMALL = (("lru_wa", 65536), ("lru_wx", 65536), ("pre_mix_norm", 1024), ("lru_conv_w", 4096), ("lru_conv_b", 1024),
         ("lru_ba", 1024), ("lru_bx", 1024), ("lru_lambda", 1024), ("lru_out_norm", 1024), ("ssd_conv_w", 6144),
         ("ssd_conv_b", 1536), ("ssd_dt_bias", 16), ("ssd_a_log", 16), ("ssd_d", 16), ("ssd_out_norm", 1024),
         ("post_mix_norm", 1024), ("pre_ffn_norm", 1024), ("post_ffn_norm", 1024), ("loss", 1))
REPLICATED = tuple(n for n, _ in SMALL if n not in ("lru_conv_w", "ssd_conv_w", "loss"))
SMALL_ROW = {}
for _name, _size in SMALL:
    SMALL_ROW[_name] = (sum(-(-s // 1024) for n, s in SMALL[:len(SMALL_ROW)]), -(-_size // 1024))


def _pack_small(d):
    rows = [jnp.pad(d[n].astype(F32).reshape(-1), (0, SMALL_ROW[n][1] * 1024 - s)).reshape(-1, 1024) for n, s in SMALL]
    used = sum(r.shape[0] for r in rows)
    return jnp.concatenate(rows + [jnp.zeros((SMALL_ROWS - used, 1024), F32)], axis=0)


def _small_entry(p, name):
    row, rows = SMALL_ROW[name]
    return p[row:row + rows].reshape(-1)[:dict(SMALL)[name]]


def _blockdiag4(w):
    on_diag = jnp.eye(4, dtype=w.dtype)[None, :, None, :, None]
    return (w.reshape(4, 4, 64, 1, 64) * on_diag).reshape(4, 256, 256)


def _diag_blocks(g):
    on_diag = jnp.eye(4, dtype=g.dtype)[None, :, None, :, None]
    return jnp.sum(g.reshape(4, 4, 64, 4, 64) * on_diag, axis=3).reshape(16, 64, 64)


def _local_step(x, target, w_in, P, rest_weights, emit, emit_small, start_token=None):
    cut = IN_MAIN - (N_DEV - 1) * (IN_COLS // N_DEV)
    w_main = jnp.concatenate([w_in[j] for j in range(N_DEV - 1)] + [w_in[N_DEV - 1][:, :cut]], axis=1)
    w_dt = jnp.pad(w_in[N_DEV - 1][:, cut:], ((0, 0), (0, LANE - SSD_HEADS)))
    pad16 = lambda v: jnp.pad(v.reshape(1, SSD_HEADS), ((0, 0), (0, LANE - SSD_HEADS)))
    dt_bias, a_log = pad16(P["ssd_dt_bias"]), pad16(P["ssd_a_log"])
    d_e = jnp.repeat(P["ssd_d"].reshape(SSD_HEADS), SSD_HEAD_DIM).reshape(1, SSD_INNER)
    expand = (jnp.arange(LANE)[:, None] == (jnp.arange(SSD_INNER)[None, :] // SSD_HEAD_DIM)).astype(BF16)
    wa_bd = _blockdiag4(P["lru_wa"].astype(BF16))
    wx_bd = _blockdiag4(P["lru_wx"].astype(BF16))
    vec = lambda n: P[n].reshape(1, -1)

    after = lambda v, tok: v if tok is None else v + tok[0:1, 0:1]

    h, proj, dtp = _inproj_fwd(x, after(vec("pre_mix_norm"), start_token), w_main, w_dt)
    lx, hl, y_lru, *gates = _lru_fwd(proj, P["lru_conv_w"], vec("lru_conv_b"), wa_bd, wx_bd, vec("lru_ba"),
                                     vec("lru_bx"), vec("lru_lambda"), vec("lru_out_norm"))
    xc, y, y_ssd, sprev = _ssd_fwd(proj, dtp, P["ssd_conv_w"], vec("ssd_conv_b"), dt_bias, a_log, d_e,
                                   vec("ssd_out_norm"), expand)
    W = rest_weights(y_ssd)
    mix, x1, h2 = _outproj_fwd(x, y_lru, y_ssd, W["w_out"], vec("post_mix_norm"), vec("pre_ffn_norm"))
    dx1, act, df, dgt, dup, dg_pf, dg_ff, loss = _ffn_fwd_bwd(
        x1, h2, target, W["w_gate"], W["w_up"], W["w_down"], vec("pre_ffn_norm"), vec("post_ffn_norm"))
    tok = emit("ffn", {"w_gate": _tn_matmul_slabs(h2, [dgt], "dw_gate", D_FF // N_DEV),
                       "w_up": _tn_matmul_slabs(h2, [dup], "dw_up", D_FF // N_DEV),
                       "w_down": _tn_matmul(act, [df], "dw_down")[0]})
    dy_lru, dy_ssd, dmix, dg_pm = _outproj_bwd(dx1, mix, W["w_out"], after(vec("post_mix_norm"), tok))
    tok = emit("out", {"w_out": jnp.concatenate([_tn_matmul(y_lru, [dmix], "dw_out_lru")[0],
                                                 _tn_matmul(y_ssd, [dmix], "dw_out_ssd")[0]], axis=0)})
    dp_ssd, ddtp, dcw_s, dcb_s, dbias, dA, dD_e, dg_ssd = _ssd_bwd(
        dy_ssd, proj, dtp, xc, y, sprev, P["ssd_conv_w"], dt_bias, a_log, d_e,
        after(vec("ssd_out_norm"), tok), expand)
    dp_lru, dpa, dpx, dcw_l, dcb_l, dba, dbx, dlam, dg_lru = _lru_bwd(
        dy_lru, proj, lx, hl, gates, P["lru_conv_w"], wa_bd, wx_bd, vec("lru_lambda"), vec("lru_out_norm"))
    tok = emit("in", {"w_in": _tn_matmul_slabs(h, [dp_lru, dp_ssd, ddtp], "dw_in", IN_COLS // N_DEV, tk=256)})

    a_neg = -jnp.exp(P["ssd_a_log"].reshape(SSD_HEADS))
    dwa, dwx = _tn_blockdiag(lx, dpa, dpx, "dw_lru_gates")
    small = {
        "pre_mix_norm": jnp.zeros((1, D_MODEL), F32), "lru_conv_w": dcw_l, "lru_conv_b": dcb_l,
        "lru_wa": _diag_blocks(dwa), "lru_ba": dba,
        "lru_wx": _diag_blocks(dwx), "lru_bx": dbx,
        "lru_lambda": dlam, "lru_out_norm": dg_lru, "ssd_conv_w": dcw_s, "ssd_conv_b": dcb_s,
        "ssd_dt_bias": dbias[0, :SSD_HEADS], "ssd_a_log": dA[0, :SSD_HEADS] * a_neg,
        "ssd_d": jnp.sum(dD_e.reshape(SSD_HEADS, SSD_HEAD_DIM), axis=1), "ssd_out_norm": dg_ssd,
        "post_mix_norm": dg_pm, "pre_ffn_norm": dg_pf, "post_ffn_norm": dg_ff, "loss": loss[0, 0:1],
    }
    tok = after(after(vec("pre_mix_norm"), tok), emit_small(small))
    grad_x, dg_pre = _inproj_bwd(dp_lru, dp_ssd, ddtp, dx1, x, tok, w_main, w_dt)
    return grad_x, dg_pre


def kernel(x, pre_mix_norm, w_in, lru_conv_w, lru_conv_b, lru_wa, lru_ba, lru_wx, lru_bx, lru_lambda, lru_out_norm, ssd_conv_w, ssd_conv_b, ssd_dt_bias, ssd_a_log, ssd_d, ssd_out_norm, w_out, post_mix_norm, pre_ffn_norm, w_gate, w_up, w_down, post_ffn_norm, loss_target, m_pre_mix_norm, m_w_in, m_lru_conv_w, m_lru_conv_b, m_lru_wa, m_lru_ba, m_lru_wx, m_lru_bx, m_lru_lambda, m_lru_out_norm, m_ssd_conv_w, m_ssd_conv_b, m_ssd_dt_bias, m_ssd_a_log, m_ssd_d, m_ssd_out_norm, m_w_out, m_post_mix_norm, m_pre_ffn_norm, m_w_gate, m_w_up, m_w_down, m_post_ffn_norm, v_pre_mix_norm, v_w_in, v_lru_conv_w, v_lru_conv_b, v_lru_wa, v_lru_ba, v_lru_wx, v_lru_bx, v_lru_lambda, v_lru_out_norm, v_ssd_conv_w, v_ssd_conv_b, v_ssd_dt_bias, v_ssd_a_log, v_ssd_d, v_ssd_out_norm, v_w_out, v_post_mix_norm, v_pre_ffn_norm, v_w_gate, v_w_up, v_w_down, v_post_ffn_norm):
    a = dict(locals())
    names = [n for n, _ in SMALL if n != "loss"] + list(BIG)
    w = {n: a[n][0] for n in names}
    m = {n: a["m_" + n][0] for n in names}
    v = {n: a["v_" + n][0] for n in names}

    cpack = jnp.concatenate([w["lru_conv_w"], w["ssd_conv_w"], jnp.zeros((4, 64), F32)], axis=1)
    cpack = jnp.pad(cpack, ((0, 4), (0, 0)))
    g_in, cg = _all_gather([w["w_in"].astype(BF16), cpack], "all_gather_w_in")
    P = {n: w[n] for n in REPLICATED}
    P["lru_conv_w"] = _join(cg[:, 0:4, 0:128], 1)
    P["ssd_conv_w"] = _join(cg[:, 0:4, 128:320], 1)

    rest = [n for n in BIG if n != "w_in"]
    zero = jnp.minimum(jnp.abs(cg[0, 0, 0]), 0.0)
    ex_w = _exchange_start([(w[n] + zero).astype(BF16) for n in rest], "weights_start", slab_source=False,
                           axes=[BIG_SHARD_AXIS[n] for n in rest])

    def rest_weights(after):
        lands = _exchange_wait(ex_w, after, "weights_wait")
        return {n: _join(p, 1) if p.ndim == 3 else p for n, p in zip(rest, lands)}

    pending = []

    def emit(group, grads):
        ex = _exchange_start([g if g.ndim == 3 else _split(g, BIG_SHARD_AXIS[n]) for n, g in grads.items()],
                             "grads_start_" + group, slab_source=True)
        pending.append((group, list(grads), ex))
        return ex["token"]

    def emit_small(small):
        ex = _exchange_start([_pack_small(small)], "small_start", slab_source=False, axes=[0])
        pending.append(("small", None, ex))
        return ex["token"]

    grad_x, dg_pre = _local_step(x[0], loss_target[0], g_in, P, rest_weights, emit, emit_small, ex_w["token"])

    got_pre = _all_gather([jnp.pad(dg_pre, ((0, 7), (0, 0)))], "all_gather_pre_mix_norm")[0]
    ex_small = pending.pop()[2]
    got_small = _exchange_wait(ex_small, got_pre, "small_wait")[0].reshape(N_DEV, SMALL_ROWS, 1024)
    row = SMALL_ROW["pre_mix_norm"][0]
    got_small = got_small.at[:, row:row + 1, :].set(got_pre[:, 0:1, :])

    outs = {}
    done = got_small
    for group, group_names, ex in pending:
        for n, parts in zip(group_names, _exchange_wait(ex, done, "grads_wait_" + group)):
            res = _adamw(parts, w[n], m[n], v[n], "adamw_" + n, BIG_ADAM_ROWS[n])
            done = res[0]
            for kind, r in zip(("grad", "delta", "new_m", "new_v"), res):
                outs[kind + "_" + n] = r

    two_d = lambda t: t.reshape(-1, 1024) if t.ndim == 3 else t.reshape(1, -1)
    res = _adamw_small(got_small, [(SMALL_ROW[n][0], two_d(w[n]), two_d(m[n]), two_d(v[n])) for n in REPLICATED])
    g_pack = res[0]
    for i, n in enumerate(REPLICATED):
        for kind, r in zip(("grad", "delta", "new_m", "new_v"), res[1 + 4 * i:5 + 4 * i]):
            outs[kind + "_" + n] = r.reshape(w[n].shape)

    me = _my_block()
    gl = lax.dynamic_slice(_small_entry(g_pack, "lru_conv_w").reshape(4, LRU_W), (0, me * 128), (4, 128))
    gs = lax.dynamic_slice(_small_entry(g_pack, "ssd_conv_w").reshape(4, SSD_CONV_CH), (0, me * 192), (4, 192))
    cat = lambda d: jnp.pad(jnp.concatenate([d["lru_conv_w"], d["ssd_conv_w"]], axis=1), ((0, 4), (0, 64)))
    res = _adamw(cat({"lru_conv_w": gl, "ssd_conv_w": gs})[None], cat(w), cat(m), cat(v), "adamw_conv", 8)
    for kind, r in zip(("grad", "delta", "new_m", "new_v"), res):
        outs[kind + "_lru_conv_w"] = r[0:4, 0:128]
        outs[kind + "_ssd_conv_w"] = r[0:4, 128:320]

    order = ["pre_mix_norm", "w_in", "lru_conv_w", "lru_conv_b", "lru_wa", "lru_ba", "lru_wx", "lru_bx", "lru_lambda",
             "lru_out_norm", "ssd_conv_w", "ssd_conv_b", "ssd_dt_bias", "ssd_a_log", "ssd_d", "ssd_out_norm", "w_out",
             "post_mix_norm", "pre_ffn_norm", "w_gate", "w_up", "w_down", "post_ffn_norm"]
    result = [_small_entry(g_pack, "loss").reshape(()), grad_x[None]]
    for kind in ("grad", "delta", "new_m", "new_v"):
        result += [outs[kind + "_" + n][None] for n in order]
    return tuple(result)
```

```python
import functools

import jax
import jax.numpy as jnp
from jax import lax
from jax.experimental import pallas as pl
from jax.experimental.pallas import tpu as pltpu

F32 = jnp.float32
BF16 = jnp.bfloat16
HI = lax.Precision.HIGHEST
EPS = 1e-6
N_DEV = 8
D_MODEL = 1024
LRU_W = 1024
SSD_INNER = 1024
SSD_HEADS = 16
SSD_HEAD_DIM = 64
SSD_STATE = 128
SSD_GROUPS = 2
SSD_CONV_CH = 1536
CHUNK = 128
D_FF = 2816
FF_CHUNK = 2816
IN_MAIN = 4608
IN_COLS = 4624
LANE = 128
TT = 256
TP = 512
VMEM_LIMIT = 56 * 1024 * 1024
ADAM_LR, ADAM_B1, ADAM_B2, ADAM_EPS, ADAM_WD, ADAM_STEP = 0.001, 0.9, 0.999, 1e-08, 0.01, 10
PACK_ROWS = 1920
SMALL_ROWS = 160

NT = (((1,), (1,)), ((), ()))
TN = (((0,), (0,)), ((), ()))


def _params(n_grid):
    return pltpu.CompilerParams(dimension_semantics=("arbitrary",) * n_grid, vmem_limit_bytes=VMEM_LIMIT)


def _dot(a, b, dims=None, precision=None):
    if dims is None:
        return jnp.dot(a, b, preferred_element_type=F32, precision=precision)
    return lax.dot_general(a, b, dims, preferred_element_type=F32, precision=precision)


def _split_bf16(x, terms):
    out = []
    for _ in range(terms - 1):
        p = x.astype(BF16)
        out.append(p)
        x = x - p.astype(F32)
    return out + [x.astype(BF16)]


def _dot_sel(x, sel, dims=None, terms=2, sel_first=False):
    parts = [_dot(sel, p, dims) if sel_first else _dot(p, sel, dims) for p in _split_bf16(x, terms)]
    return functools.reduce(lambda a, b: a + b, parts)


def _sigmoid(x):
    return 0.5 * jnp.tanh(0.5 * x) + 0.5


def _softplus(x):
    e = jnp.exp(-jnp.abs(x))
    l1p = jnp.where(e < 1e-3, e * (1.0 - e * (0.5 - e * (1.0 / 3.0))), jnp.log(1.0 + e))
    return jnp.maximum(x, 0.0) + l1p


def _neg_expm1(x):
    series = -x * (1.0 + x * (0.5 + x * (1.0 / 6.0 + x * (1.0 / 24.0))))
    return jnp.where(x > -0.01, series, 1.0 - jnp.exp(x))


_GELU_C = 0.7978845608028654


def _gelu(x):
    t = jnp.tanh(_GELU_C * (x + 0.044715 * x * x * x))
    return 0.5 * x * (1.0 + t), t


def _gelu_grad(x, t):
    return 0.5 * (1.0 + t) + 0.5 * x * (1.0 - t * t) * _GELU_C * (1.0 + 3.0 * 0.044715 * x * x)


def _rms(x):
    return lax.rsqrt(jnp.mean(x * x, axis=-1, keepdims=True) + EPS)


def _rms_bwd(dyn, x, rn):
    return rn * dyn - x * (rn * rn * rn) * jnp.mean(dyn * x, axis=-1, keepdims=True)


def _row(x, r):
    idx = lax.broadcasted_iota(jnp.int32, x.shape, 0)
    return jnp.sum(jnp.where(idx == r, x, 0.0), axis=0, keepdims=True)


def _shift_down(cur, prev8, j):
    s = pltpu.roll(cur, j, 0)
    p = pltpu.roll(prev8, j, 0)
    r8 = lax.broadcasted_iota(jnp.int32, prev8.shape, 0)
    top = jnp.where(r8 < j, p, s[0:8])
    return jnp.concatenate([top, s[8:]], axis=0)


def _shift_up(cur, next8, j):
    n = cur.shape[0]
    s = pltpu.roll(cur, n - j, 0)
    p = pltpu.roll(next8, 8 - j, 0)
    r8 = lax.broadcasted_iota(jnp.int32, next8.shape, 0)
    bot = jnp.where(r8 >= 8 - j, p, s[n - 8:n])
    return jnp.concatenate([s[:n - 8], bot], axis=0)


def _scan_fwd(a, u):
    n = a.shape[0]
    row = lax.broadcasted_iota(jnp.int32, a.shape, 0)
    k = 1
    while k < n:
        ok = row >= k
        a_s = jnp.where(ok, pltpu.roll(a, k, 0), 1.0)
        u_s = jnp.where(ok, pltpu.roll(u, k, 0), 0.0)
        u = a * u_s + u
        a = a * a_s
        k *= 2
    return a, u


def _scan_bwd(b, d):
    n = b.shape[0]
    row = lax.broadcasted_iota(jnp.int32, b.shape, 0)
    k = 1
    while k < n:
        ok = row < n - k
        b_s = jnp.where(ok, pltpu.roll(b, n - k, 0), 1.0)
        d_s = jnp.where(ok, pltpu.roll(d, n - k, 0), 0.0)
        d = b * d_s + d
        b = b * b_s
        k *= 2
    return b, d


def _scan_tile(a, u, carry, a_s, u_s, reverse):
    n, c = a.shape
    groups = n // 8
    r8 = lax.broadcasted_iota(jnp.int32, a.shape, 0) & 7
    in_group = lambda x, k: pltpu.roll(x.reshape(groups, 8, c), k, 1).reshape(n, c)
    for k in (1, 2, 4):
        ok = (r8 < 8 - k) if reverse else (r8 >= k)
        shift = 8 - k if reverse else k
        a_n = jnp.where(ok, in_group(a, shift), 1.0)
        u_n = jnp.where(ok, in_group(u, shift), 0.0)
        u = a * u_n + u
        a = a * a_n
    nl = c // LANE
    for j in range(nl):
        a_s[j] = a[:, LANE * j:LANE * (j + 1)]
        u_s[j] = u[:, LANE * j:LANE * (j + 1)]
    end = 0 if reverse else 7
    ends = lambda ref, j: ref[pl.ds(j, 1), pl.ds(end, groups, stride=8), :].reshape(groups, LANE)
    ga = jnp.concatenate([ends(a_s, j) for j in range(nl)], axis=1)
    gu = jnp.concatenate([ends(u_s, j) for j in range(nl)], axis=1)
    gacc, gh = (_scan_bwd if reverse else _scan_fwd)(ga, gu)
    gh = gh + gacc * carry
    grow = lax.broadcasted_iota(jnp.int32, gh.shape, 0)
    if reverse:
        cin = jnp.where(grow == groups - 1, carry, pltpu.roll(gh, groups - 1, 0))
    else:
        cin = jnp.where(grow == 0, carry, pltpu.roll(gh, 1, 0))
    spread = ((lax.broadcasted_iota(jnp.int32, (n, LANE), 0) >> 3)
              == lax.broadcasted_iota(jnp.int32, (n, LANE), 1)).astype(BF16)
    cin = jnp.concatenate([cin, jnp.zeros((LANE - groups, c), F32)], axis=0)
    return u + a * _dot_sel(cin, spread, terms=3, sel_first=True)


def _inproj_fwd(x, g_pre, w_main, w_dt):
    T = x.shape[0]
    TT = TP

    def body(x_ref, g_ref, wm_hbm, wd_hbm, h_ref, proj_ref, dtp_ref, wm, wd, sem):
        @pl.when(pl.program_id(0) == 0)
        def _():
            c1 = pltpu.make_async_copy(wm_hbm, wm, sem.at[0])
            c2 = pltpu.make_async_copy(wd_hbm, wd, sem.at[1])
            c1.start()
            c2.start()
            c1.wait()
            c2.wait()

        xv = x_ref[...]
        h = (xv * _rms(xv) * g_ref[...]).astype(BF16)
        h_ref[...] = h
        proj_ref[...] = _dot(h, wm[...])
        dtp_ref[...] = _dot(h, wd[...])

    return pl.pallas_call(
        body, name="inproj_fwd", grid=(T // TT,),
        in_specs=[pl.BlockSpec((TT, D_MODEL), lambda i: (i, 0)),
                  pl.BlockSpec((1, D_MODEL), lambda i: (0, 0)),
                  pl.BlockSpec(memory_space=pl.ANY), pl.BlockSpec(memory_space=pl.ANY)],
        out_specs=[pl.BlockSpec((TT, D_MODEL), lambda i: (i, 0)),
                   pl.BlockSpec((TT, IN_MAIN), lambda i: (i, 0)),
                   pl.BlockSpec((TT, LANE), lambda i: (i, 0))],
        out_shape=[jax.ShapeDtypeStruct((T, D_MODEL), BF16), jax.ShapeDtypeStruct((T, IN_MAIN), F32),
                   jax.ShapeDtypeStruct((T, LANE), F32)],
        scratch_shapes=[pltpu.VMEM((D_MODEL, IN_MAIN), BF16), pltpu.VMEM((D_MODEL, LANE), BF16),
                        pltpu.SemaphoreType.DMA((2,))],
        compiler_params=_params(1),
    )(x, g_pre, w_main, w_dt)


def _inproj_bwd(dp_lru, dp_ssd, ddtp, dx1, x, g_pre, w_main, w_dt):
    T = x.shape[0]
    TT = TP

    def body(dl_ref, ds_ref, dd_ref, dx1_ref, x_ref, g_ref, wm_hbm, wd_hbm, gx_ref, dg_ref, wm, wd, sem):
        @pl.when(pl.program_id(0) == 0)
        def _():
            c1 = pltpu.make_async_copy(wm_hbm, wm, sem.at[0])
            c2 = pltpu.make_async_copy(wd_hbm, wd, sem.at[1])
            c1.start()
            c2.start()
            c1.wait()
            c2.wait()
            dg_ref[...] = jnp.zeros_like(dg_ref)

        dh = _dot(dl_ref[...], wm[:, 0:2048], NT)
        dh += _dot(ds_ref[...], wm[:, 2048:IN_MAIN], NT)
        dh += _dot(dd_ref[...], wd[...], NT)
        xv = x_ref[...]
        rn = _rms(xv)
        dg_ref[...] += jnp.sum(dh * xv * rn, axis=0, keepdims=True)
        gx_ref[...] = dx1_ref[...] + _rms_bwd(dh * g_ref[...], xv, rn)

    return pl.pallas_call(
        body, name="inproj_bwd", grid=(T // TT,),
        in_specs=[pl.BlockSpec((TT, 2048), lambda i: (i, 0)),
                  pl.BlockSpec((TT, 2560), lambda i: (i, 0)),
                  pl.BlockSpec((TT, LANE), lambda i: (i, 0)),
                  pl.BlockSpec((TT, D_MODEL), lambda i: (i, 0)),
                  pl.BlockSpec((TT, D_MODEL), lambda i: (i, 0)),
                  pl.BlockSpec((1, D_MODEL), lambda i: (0, 0)),
                  pl.BlockSpec(memory_space=pl.ANY), pl.BlockSpec(memory_space=pl.ANY)],
        out_specs=[pl.BlockSpec((TT, D_MODEL), lambda i: (i, 0)),
                   pl.BlockSpec((1, D_MODEL), lambda i: (0, 0))],
        out_shape=[jax.ShapeDtypeStruct((T, D_MODEL), F32), jax.ShapeDtypeStruct((1, D_MODEL), F32)],
        scratch_shapes=[pltpu.VMEM((D_MODEL, IN_MAIN), BF16), pltpu.VMEM((D_MODEL, LANE), BF16),
                        pltpu.SemaphoreType.DMA((2,))],
        compiler_params=_params(1),
    )(dp_lru, dp_ssd, ddtp, dx1, x, g_pre, w_main, w_dt)


def _lru_gates(lx, wa_ref, wx_ref, ba, bx, lam):
    lxb = lx.astype(BF16)
    pa = jnp.concatenate([_dot(lxb[:, 256 * k:256 * (k + 1)], wa_ref[k]) for k in range(4)], axis=1) + ba
    px = jnp.concatenate([_dot(lxb[:, 256 * k:256 * (k + 1)], wx_ref[k]) for k in range(4)], axis=1) + bx
    r = _sigmoid(pa)
    ig = _sigmoid(px)
    sp = _softplus(-lam)
    log_a = -8.0 * r * sp
    a = jnp.exp(log_a)
    mult = jnp.sqrt(_neg_expm1(2.0 * log_a))
    return r, ig, sp, a, mult


def _lru_fwd(proj, conv_w, conv_b, wa_bd, wx_bd, ba, bx, lam, g_lru):
    T = proj.shape[0]

    def body(cx_ref, gate_ref, cw_ref, cb_ref, wa_ref, wx_ref, ba_ref, bx_ref, lam_ref, g_ref,
             lx_ref, hl_ref, y_ref, r_ref, ig_ref, a_ref, mult_ref, tail, hcar, sa, su):
        @pl.when(pl.program_id(0) == 0)
        def _():
            tail[...] = jnp.zeros_like(tail)
            hcar[...] = jnp.zeros_like(hcar)

        cx = cx_ref[...]
        prev8 = tail[...]
        lx = cb_ref[...] + cw_ref[3:4, :] * cx
        for j in range(1, 4):
            lx += cw_ref[3 - j:4 - j, :] * _shift_down(cx, prev8, j)
        tail[...] = cx[TT - 8:TT]
        lx_ref[...] = lx
        r, ig, sp, a, mult = _lru_gates(lx, wa_ref, wx_ref, ba_ref[...], bx_ref[...], lam_ref[...])
        r_ref[...] = r
        ig_ref[...] = ig
        a_ref[...] = a
        mult_ref[...] = mult
        h = _scan_tile(a, mult * (ig * lx), hcar[...], sa, su, reverse=False)
        hl_ref[...] = h
        hcar[...] = hl_ref[TT - 1:TT, :]
        ge, _ = _gelu(gate_ref[...])
        p = h * ge
        y_ref[...] = (p * _rms(p) * g_ref[...]).astype(BF16)

    vec = pl.BlockSpec((1, LRU_W), lambda i: (0, 0))
    bd = pl.BlockSpec((4, 256, 256), lambda i: (0, 0, 0))
    tile = pl.BlockSpec((TT, LRU_W), lambda i: (i, 0))
    f32 = jax.ShapeDtypeStruct((T, LRU_W), F32)
    return pl.pallas_call(
        body, name="lru_fwd", grid=(T // TT,),
        in_specs=[tile, pl.BlockSpec((TT, LRU_W), lambda i: (i, 1)),
                  pl.BlockSpec((4, LRU_W), lambda i: (0, 0)), vec, bd, bd, vec, vec, vec, vec],
        out_specs=[tile] * 7,
        out_shape=[f32, f32, jax.ShapeDtypeStruct((T, LRU_W), BF16), f32, f32, f32, f32],
        scratch_shapes=[pltpu.VMEM((8, LRU_W), F32), pltpu.VMEM((1, LRU_W), F32)]
                       + [pltpu.VMEM((LRU_W // LANE, TT, LANE), F32)] * 2,
        compiler_params=_params(1),
    )(proj, proj, conv_w, conv_b, wa_bd, wx_bd, ba, bx, lam, g_lru)


def _lru_bwd(dy, proj, lx, hl, gates, conv_w, wa_bd, wx_bd, lam, g_lru):
    T = proj.shape[0]
    nt = T // TT

    def body(dy_ref, cx_ref, gate_ref, lx_ref, hl_ref, halo_ref, r_ref, ig_ref, a_ref, mult_ref, cw_ref, wa_ref,
             wx_ref, lam_ref, g_ref, dp_ref, dpa_ref, dpx_ref, dcw_ref, dcb_ref, dba_ref, dbx_ref, dlam_ref, dg_ref,
             gcar, acar, head, sa, su):
        i = pl.program_id(0)

        @pl.when(i == 0)
        def _():
            gcar[...] = jnp.zeros_like(gcar)
            acar[...] = jnp.zeros_like(acar)
            head[...] = jnp.zeros_like(head)
            for ref in (dcw_ref, dcb_ref, dba_ref, dbx_ref, dlam_ref, dg_ref):
                ref[...] = jnp.zeros_like(ref)

        lx = lx_ref[...]
        h = hl_ref[...]
        gate = gate_ref[...]
        cx = cx_ref[...]
        lam = lam_ref[...]
        r, ig, a, mult = r_ref[...], ig_ref[...], a_ref[...], mult_ref[...]
        sp = _softplus(-lam)
        ge, th = _gelu(gate)
        p = h * ge
        rn = _rms(p)
        dyv = dy_ref[...]
        dg_ref[...] += jnp.sum(dyv * p * rn, axis=0, keepdims=True)
        dp = _rms_bwd(dyv * g_ref[...], p, rn)
        dp_ref[:, LRU_W:2 * LRU_W] = (dp * h * _gelu_grad(gate, th)).astype(BF16)
        dh = dp * ge
        row = lax.broadcasted_iota(jnp.int32, a.shape, 0)
        b = jnp.where(row == TT - 1, acar[...], pltpu.roll(a, TT - 1, 0))
        g = _scan_tile(b, dh, gcar[...], sa, su, reverse=True)
        gcar[...] = _row(g[0:8], 0)
        acar[...] = _row(a[0:8], 0)
        h_last_prev = halo_ref[7:8, :] * (i < nt - 1).astype(F32)
        hprev = jnp.where(row == 0, h_last_prev, pltpu.roll(h, 1, 0))
        da = g * hprev
        dm2 = (g * (ig * lx)) * 0.5 / mult
        dlog_a = da * a - 2.0 * a * a * dm2
        dlam_ref[...] += jnp.sum(-8.0 * r * dlog_a, axis=0, keepdims=True) * (-_sigmoid(-lam))
        dpa = (-8.0 * sp * dlog_a) * r * (1.0 - r)
        dpx = (g * mult * lx) * ig * (1.0 - ig)
        dba_ref[...] += jnp.sum(dpa, axis=0, keepdims=True)
        dbx_ref[...] += jnp.sum(dpx, axis=0, keepdims=True)
        dpab = dpa.astype(BF16)
        dpxb = dpx.astype(BF16)
        dpa_ref[...] = dpab
        dpx_ref[...] = dpxb
        dlx = g * mult * ig + jnp.concatenate(
            [_dot(dpab[:, 256 * k:256 * (k + 1)], wa_ref[k], NT) + _dot(dpxb[:, 256 * k:256 * (k + 1)], wx_ref[k], NT)
             for k in range(4)], axis=1)
        nxt = head[...]
        dcb_ref[...] += jnp.sum(dlx, axis=0, keepdims=True)
        dcx = cw_ref[3:4, :] * dlx
        dcw_ref[3:4, :] += jnp.sum(cx * dlx, axis=0, keepdims=True)
        for j in range(1, 4):
            sh = _shift_up(dlx, nxt, j)
            dcx += cw_ref[3 - j:4 - j, :] * sh
            dcw_ref[3 - j:4 - j, :] += jnp.sum(cx * sh, axis=0, keepdims=True)
        head[...] = dlx[0:8]
        dp_ref[:, 0:LRU_W] = dcx.astype(BF16)

    rev = lambda i: (nt - 1 - i, 0)
    vec = pl.BlockSpec((1, LRU_W), lambda i: (0, 0))
    bd = pl.BlockSpec((4, 256, 256), lambda i: (0, 0, 0))
    tile = pl.BlockSpec((TT, LRU_W), rev)
    halo = pl.BlockSpec((8, LRU_W), lambda i: (jnp.maximum((nt - 1 - i) * (TT // 8) - 1, 0), 0))
    cw = pl.BlockSpec((4, LRU_W), lambda i: (0, 0))
    return pl.pallas_call(
        body, name="lru_bwd", grid=(nt,),
        in_specs=[tile, tile, pl.BlockSpec((TT, LRU_W), lambda i: (nt - 1 - i, 1)), tile, tile, halo,
                  tile, tile, tile, tile, cw, bd, bd, vec, vec],
        out_specs=[pl.BlockSpec((TT, 2 * LRU_W), rev), tile, tile, cw, vec, vec, vec, vec, vec],
        out_shape=[jax.ShapeDtypeStruct((T, 2 * LRU_W), BF16), jax.ShapeDtypeStruct((T, LRU_W), BF16),
                   jax.ShapeDtypeStruct((T, LRU_W), BF16), jax.ShapeDtypeStruct((4, LRU_W), F32)]
                  + [jax.ShapeDtypeStruct((1, LRU_W), F32)] * 5,
        scratch_shapes=[pltpu.VMEM((1, LRU_W), F32), pltpu.VMEM((1, LRU_W), F32), pltpu.VMEM((8, LRU_W), F32)]
                       + [pltpu.VMEM((LRU_W // LANE, TT, LANE), F32)] * 2,
        compiler_params=_params(1),
    )(dy, proj, proj, lx, hl, hl, *gates, conv_w, wa_bd, wx_bd, lam, g_lru)


def _ssd_chunk_terms(xc, dtp, bias, alog, expand):
    sg = _sigmoid(xc)
    xbc = xc * sg
    pre = dtp + bias
    dt = _softplus(pre)
    A = -jnp.exp(alog)
    ri = lax.broadcasted_iota(jnp.int32, (CHUNK, CHUNK), 0)
    ci = lax.broadcasted_iota(jnp.int32, (CHUNK, CHUNK), 1)
    tril = (ri >= ci).astype(BF16)
    cs = _dot_sel(dt * A, tril, terms=3, sel_first=True)
    cs_last = _row(cs, CHUNK - 1)
    ecs = jnp.exp(cs)
    dec = jnp.exp(cs_last - cs)
    return dict(sg=sg, xbc=xbc, pre=pre, dt=dt, A=A, cs=cs, csT=cs.T, ecs=ecs, dec=dec, ri=ri, ci=ci,
                dt_e=_dot_sel(dt, expand), ecs_e=_dot_sel(ecs, expand), dec_e=_dot_sel(dec, expand))


def _head_lambda(t, h):
    col = jnp.sum(jnp.where(t["ci"] == h, t["cs"], 0.0), axis=1, keepdims=True)
    rowv = jnp.sum(jnp.where(t["ri"] == h, t["csT"], 0.0), axis=0, keepdims=True)
    return jnp.exp(jnp.where(t["ri"] >= t["ci"], col - rowv, -1e30))


def _ssd_fwd(proj, dtp, conv_w, conv_b, dt_bias, a_log, d_e, g_ssd, expand):
    T = proj.shape[0]
    nc = T // CHUNK

    def body(z_ref, xp_ref, dtp_ref, cw_ref, cb_ref, bias_ref, alog_ref, de_ref, g_ref, ex_ref,
             xc_ref, y_ref, yn_ref, sprev_ref, tail, S):
        @pl.when(pl.program_id(0) == 0)
        def _():
            tail[...] = jnp.zeros_like(tail)
            S[...] = jnp.zeros_like(S)

        xp = xp_ref[...]
        prev8 = tail[...]
        xc = cb_ref[...] + cw_ref[3:4, :] * xp
        for j in range(1, 4):
            xc += cw_ref[3 - j:4 - j, :] * _shift_down(xp, prev8, j)
        tail[...] = xp[CHUNK - 8:CHUNK]
        xc_ref[...] = xc
        t = _ssd_chunk_terms(xc, dtp_ref[...], bias_ref[...], alog_ref[...], ex_ref[...])
        xbc = t["xbc"]
        sx = xbc[:, 0:SSD_INNER]
        Bb = xbc[:, SSD_INNER:SSD_INNER + 256].astype(BF16)
        Cb = xbc[:, SSD_INNER + 256:SSD_CONV_CH].astype(BF16)
        X = t["dt_e"] * sx
        lane = lax.broadcasted_iota(jnp.int32, (CHUNK, LANE), 1)
        G = [_dot(Cb[:, 128 * g:128 * (g + 1)], Bb[:, 128 * g:128 * (g + 1)], NT) for g in range(SSD_GROUPS)]
        for k in range(SSD_HEADS // 2):
            Xp = X[:, 128 * k:128 * (k + 1)]
            acc = jnp.zeros((CHUNK, LANE), F32)
            for half in range(2):
                M = (G[k // 4] * _head_lambda(t, 2 * k + half)).astype(BF16)
                Xh = jnp.where((lane >= 64) if half else (lane < 64), Xp, 0.0).astype(BF16)
                acc += _dot(M, Xh)
            y_ref[:, 128 * k:128 * (k + 1)] = acc
        sprev_ref[0] = S[...]
        eL_e = _row(t["ecs_e"], CHUNK - 1)
        Xd = (X * t["dec_e"]).astype(BF16)
        for g in range(SSD_GROUPS):
            sl = slice(512 * g, 512 * (g + 1))
            Sg = S[:, sl]
            y_ref[:, sl] += t["ecs_e"][:, sl] * _dot(Cb[:, 128 * g:128 * (g + 1)], Sg.astype(BF16))
            S[:, sl] = eL_e[:, sl] * Sg + _dot(Bb[:, 128 * g:128 * (g + 1)], Xd[:, sl], TN)
        y = y_ref[...] + de_ref[...] * sx
        y_ref[...] = y
        z = z_ref[...]
        q = y * (z * _sigmoid(z))
        yn_ref[...] = (q * _rms(q) * g_ref[...]).astype(BF16)

    c0 = lambda i: (0, 0)
    return pl.pallas_call(
        body, name="ssd_fwd", grid=(nc,),
        in_specs=[pl.BlockSpec((CHUNK, SSD_INNER), lambda i: (i, 2)),
                  pl.BlockSpec((CHUNK, SSD_CONV_CH), lambda i: (i, 2)),
                  pl.BlockSpec((CHUNK, LANE), lambda i: (i, 0)),
                  pl.BlockSpec((4, SSD_CONV_CH), c0), pl.BlockSpec((1, SSD_CONV_CH), c0),
                  pl.BlockSpec((1, LANE), c0), pl.BlockSpec((1, LANE), c0),
                  pl.BlockSpec((1, SSD_INNER), c0), pl.BlockSpec((1, SSD_INNER), c0),
                  pl.BlockSpec((LANE, SSD_INNER), c0)],
        out_specs=[pl.BlockSpec((CHUNK, SSD_CONV_CH), lambda i: (i, 0)),
                   pl.BlockSpec((CHUNK, SSD_INNER), lambda i: (i, 0)),
                   pl.BlockSpec((CHUNK, SSD_INNER), lambda i: (i, 0)),
                   pl.BlockSpec((1, SSD_STATE, SSD_INNER), lambda i: (i, 0, 0))],
        out_shape=[jax.ShapeDtypeStruct((T, SSD_CONV_CH), F32), jax.ShapeDtypeStruct((T, SSD_INNER), F32),
                   jax.ShapeDtypeStruct((T, SSD_INNER), BF16),
                   jax.ShapeDtypeStruct((nc, SSD_STATE, SSD_INNER), F32)],
        scratch_shapes=[pltpu.VMEM((8, SSD_CONV_CH), F32), pltpu.VMEM((SSD_STATE, SSD_INNER), F32)],
        compiler_params=_params(1),
    )(proj, proj, dtp, conv_w, conv_b, dt_bias, a_log, d_e, g_ssd, expand)


def _ssd_bwd(dyn, proj, dtp, xc, y, sprev, conv_w, dt_bias, a_log, d_e, g_ssd, expand):
    T = proj.shape[0]
    nc = T // CHUNK

    def body(dyn_ref, z_ref, xp_ref, dtp_ref, xc_ref, y_ref, sprev_ref, cw_ref, bias_ref, alog_ref, de_ref,
             g_ref, ex_ref, dp_ref, ddtp_ref, dcw_ref, dcb_ref, dbias_ref, dA_ref, dD_ref, dg_ref,
             dS, head, dX_s, dxbc_s):
        @pl.when(pl.program_id(0) == 0)
        def _():
            dS[...] = jnp.zeros_like(dS)
            head[...] = jnp.zeros_like(head)
            for ref in (dcw_ref, dcb_ref, dbias_ref, dA_ref, dD_ref, dg_ref):
                ref[...] = jnp.zeros_like(ref)

        ex = ex_ref[...]
        xc = xc_ref[...]
        t = _ssd_chunk_terms(xc, dtp_ref[...], bias_ref[...], alog_ref[...], ex)
        ri, ci = t["ri"], t["ci"]
        xbc = t["xbc"]
        sx = xbc[:, 0:SSD_INNER]
        Bb = xbc[:, SSD_INNER:SSD_INNER + 256].astype(BF16)
        Cb = xbc[:, SSD_INNER + 256:SSD_CONV_CH].astype(BF16)
        X = t["dt_e"] * sx
        z = z_ref[...]
        sz = _sigmoid(z)
        siluz = z * sz
        yv = y_ref[...]
        q = yv * siluz
        rn = _rms(q)
        dynv = dyn_ref[...]
        dg_ref[...] += jnp.sum(dynv * q * rn, axis=0, keepdims=True)
        dq = _rms_bwd(dynv * g_ref[...], q, rn)
        dp_ref[:, 0:SSD_INNER] = (dq * yv * (sz * (1.0 + z * (1.0 - sz)))).astype(BF16)
        dY = dq * siluz
        dD_ref[...] += jnp.sum(dY * sx, axis=0, keepdims=True)
        dYb = dY.astype(BF16)
        lane = lax.broadcasted_iota(jnp.int32, (CHUNK, LANE), 1)
        dcs = jnp.zeros((CHUNK, CHUNK), F32)
        dcsT = jnp.zeros((CHUNK, CHUNK), F32)
        Xb = X.astype(BF16)
        for g in range(SSD_GROUPS):
            Bg = Bb[:, 128 * g:128 * (g + 1)]
            Cg = Cb[:, 128 * g:128 * (g + 1)]
            G = _dot(Cg, Bg, NT)
            dGsum = jnp.zeros((CHUNK, CHUNK), F32)
            for k in range(4 * g, 4 * g + 4):
                Xp = Xb[:, 128 * k:128 * (k + 1)]
                dYp = dY[:, 128 * k:128 * (k + 1)]
                dXp = jnp.zeros((CHUNK, LANE), F32)
                for half in range(2):
                    h = 2 * k + half
                    lam = _head_lambda(t, h)
                    M = G * lam
                    dYh = jnp.where((lane >= 64) if half else (lane < 64), dYp, 0.0).astype(BF16)
                    dM = _dot(dYh, Xp, NT)
                    W = dM * M
                    dcs += jnp.where(ci == h, jnp.sum(W, axis=1, keepdims=True), 0.0)
                    dcsT += jnp.where(ri == h, jnp.sum(W, axis=0, keepdims=True), 0.0)
                    dGsum += dM * lam
                    dXp += _dot(M.astype(BF16), dYh, TN)
                dX_s[:, 128 * k:128 * (k + 1)] = dXp
            dGb = dGsum.astype(BF16)
            dxbc_s[:, SSD_INNER + 256 + 128 * g:SSD_INNER + 256 + 128 * (g + 1)] = _dot(dGb, Bg)
            dxbc_s[:, SSD_INNER + 128 * g:SSD_INNER + 128 * (g + 1)] = _dot(dGb, Cg, TN)
        dcs = dcs - dcsT.T
        Sp = sprev_ref[0]
        dSv = dS[...]
        ecs_e, dec_e = t["ecs_e"], t["dec_e"]
        eL_e = _row(ecs_e, CHUNK - 1)
        dYe = dY * ecs_e
        dYeb = dYe.astype(BF16)
        Xd = X * dec_e
        Xdb = Xd.astype(BF16)
        for g in range(SSD_GROUPS):
            sl = slice(512 * g, 512 * (g + 1))
            Bg = Bb[:, 128 * g:128 * (g + 1)]
            Cg = Cb[:, 128 * g:128 * (g + 1)]
            Spb = Sp[:, sl].astype(BF16)
            dSb = dSv[:, sl].astype(BF16)
            CS = _dot(Cg, Spb)
            BS = _dot(Bg, dSb)
            dxbc_s[:, SSD_INNER + 256 + 128 * g:SSD_INNER + 256 + 128 * (g + 1)] += _dot(dYeb[:, sl], Spb, NT)
            dxbc_s[:, SSD_INNER + 128 * g:SSD_INNER + 128 * (g + 1)] += _dot(Xdb[:, sl], dSb, NT)
            dS[:, sl] = eL_e[:, sl] * dSv[:, sl] + _dot(Cg, dYeb[:, sl], TN)
            dX_s[:, sl] += dec_e[:, sl] * BS
            dcs += _dot_sel(dYe[:, sl] * CS, ex[:, sl], NT, terms=1)
            tdec = _dot_sel(X[:, sl] * BS, ex[:, sl], NT, terms=1) * t["dec"]
            dcs -= tdec
            last = jnp.sum(tdec, axis=0, keepdims=True)
            last += jnp.sum(_dot_sel(Sp[:, sl] * dSv[:, sl], ex[:, sl], NT, terms=1), axis=0, keepdims=True) \
                * _row(t["ecs"], CHUNK - 1)
            dcs += jnp.where(ri == CHUNK - 1, last, 0.0)
        triu = (ci >= ri).astype(BF16)
        da = _dot_sel(dcs, triu, terms=3, sel_first=True)
        dX = dX_s[...]
        ddt = da * t["A"] + _dot_sel(dX * sx, ex, NT, terms=1)
        dA_ref[...] += jnp.sum(da * t["dt"], axis=0, keepdims=True)
        ddtp = ddt * _sigmoid(t["pre"])
        dbias_ref[...] += jnp.sum(ddtp, axis=0, keepdims=True)
        ddtp_ref[...] = ddtp.astype(BF16)
        dxbc_s[:, 0:SSD_INNER] = dX * t["dt_e"] + de_ref[...] * dY
        sg = t["sg"]
        dxc = dxbc_s[...] * (sg * (1.0 + xc * (1.0 - sg)))
        xp = xp_ref[...]
        nxt = head[...]
        dcb_ref[...] += jnp.sum(dxc, axis=0, keepdims=True)
        dpre = cw_ref[3:4, :] * dxc
        dcw_ref[3:4, :] += jnp.sum(xp * dxc, axis=0, keepdims=True)
        for j in range(1, 4):
            sh = _shift_up(dxc, nxt, j)
            dpre += cw_ref[3 - j:4 - j, :] * sh
            dcw_ref[3 - j:4 - j, :] += jnp.sum(xp * sh, axis=0, keepdims=True)
        head[...] = dxc[0:8]
        dp_ref[:, SSD_INNER:SSD_INNER + SSD_CONV_CH] = dpre.astype(BF16)

    c0 = lambda i: (0, 0)
    rev = lambda i: (nc - 1 - i, 0)
    return pl.pallas_call(
        body, name="ssd_bwd", grid=(nc,),
        in_specs=[pl.BlockSpec((CHUNK, SSD_INNER), rev),
                  pl.BlockSpec((CHUNK, SSD_INNER), lambda i: (nc - 1 - i, 2)),
                  pl.BlockSpec((CHUNK, SSD_CONV_CH), lambda i: (nc - 1 - i, 2)),
                  pl.BlockSpec((CHUNK, LANE), rev),
                  pl.BlockSpec((CHUNK, SSD_CONV_CH), rev),
                  pl.BlockSpec((CHUNK, SSD_INNER), rev),
                  pl.BlockSpec((1, SSD_STATE, SSD_INNER), lambda i: (nc - 1 - i, 0, 0)),
                  pl.BlockSpec((4, SSD_CONV_CH), c0), pl.BlockSpec((1, LANE), c0), pl.BlockSpec((1, LANE), c0),
                  pl.BlockSpec((1, SSD_INNER), c0), pl.BlockSpec((1, SSD_INNER), c0),
                  pl.BlockSpec((LANE, SSD_INNER), c0)],
        out_specs=[pl.BlockSpec((CHUNK, 2560), rev), pl.BlockSpec((CHUNK, LANE), rev),
                   pl.BlockSpec((4, SSD_CONV_CH), c0), pl.BlockSpec((1, SSD_CONV_CH), c0),
                   pl.BlockSpec((1, LANE), c0), pl.BlockSpec((1, LANE), c0),
                   pl.BlockSpec((1, SSD_INNER), c0), pl.BlockSpec((1, SSD_INNER), c0)],
        out_shape=[jax.ShapeDtypeStruct((T, 2560), BF16), jax.ShapeDtypeStruct((T, LANE), BF16),
                   jax.ShapeDtypeStruct((4, SSD_CONV_CH), F32), jax.ShapeDtypeStruct((1, SSD_CONV_CH), F32),
                   jax.ShapeDtypeStruct((1, LANE), F32), jax.ShapeDtypeStruct((1, LANE), F32),
                   jax.ShapeDtypeStruct((1, SSD_INNER), F32), jax.ShapeDtypeStruct((1, SSD_INNER), F32)],
        scratch_shapes=[pltpu.VMEM((SSD_STATE, SSD_INNER), F32), pltpu.VMEM((8, SSD_CONV_CH), F32),
                        pltpu.VMEM((CHUNK, SSD_INNER), F32), pltpu.VMEM((CHUNK, SSD_CONV_CH), F32)],
        compiler_params=_params(1),
    )(dyn, proj, proj, dtp, xc, y, sprev, conv_w, dt_bias, a_log, d_e, g_ssd, expand)


def _outproj_fwd(x, y_lru, y_ssd, w_out, g_pm, g_pf):
    T = x.shape[0]
    TT = TP

    def body(x_ref, yl_ref, ys_ref, wo_ref, gpm_ref, gpf_ref, mix_ref, x1_ref, h2_ref):
        mix = _dot(yl_ref[...], wo_ref[0:LRU_W, :]) + _dot(ys_ref[...], wo_ref[LRU_W:2 * LRU_W, :])
        mix_ref[...] = mix
        x1 = x_ref[...] + mix * _rms(mix) * gpm_ref[...]
        x1_ref[...] = x1
        h2_ref[...] = (x1 * _rms(x1) * gpf_ref[...]).astype(BF16)

    tile = pl.BlockSpec((TT, D_MODEL), lambda i: (i, 0))
    vec = pl.BlockSpec((1, D_MODEL), lambda i: (0, 0))
    return pl.pallas_call(
        body, name="outproj_fwd", grid=(T // TT,),
        in_specs=[tile, tile, tile, pl.BlockSpec((2 * LRU_W, D_MODEL), lambda i: (0, 0)), vec, vec],
        out_specs=[tile, tile, tile],
        out_shape=[jax.ShapeDtypeStruct((T, D_MODEL), F32), jax.ShapeDtypeStruct((T, D_MODEL), F32),
                   jax.ShapeDtypeStruct((T, D_MODEL), BF16)],
        compiler_params=_params(1),
    )(x, y_lru, y_ssd, w_out, g_pm, g_pf)


def _ffn_fwd_bwd(x1, h2, target, w_gate, w_up, w_down, g_pf, g_ff):
    T = x1.shape[0]

    def body(x1_ref, h2_ref, tg_ref, wg_hbm, wu_hbm, wd_hbm, gpf_ref, gff_ref,
             dx1_ref, act_ref, df_ref, dgt_ref, dup_ref, dgpf_ref, dgff_ref, loss_ref,
             wg, wu, wd, gt_s, up_s, sem):
        @pl.when(pl.program_id(0) == 0)
        def _():
            cps = [pltpu.make_async_copy(s, d, sem.at[n]) for n, (s, d) in
                   enumerate(((wg_hbm, wg), (wu_hbm, wu), (wd_hbm, wd)))]
            for c in cps:
                c.start()
            for c in cps:
                c.wait()
            for ref in (dgpf_ref, dgff_ref, loss_ref):
                ref[...] = jnp.zeros_like(ref)

        h2 = h2_ref[...]
        f = jnp.zeros((TT, D_MODEL), F32)
        for c in range(D_FF // FF_CHUNK):
            sl = slice(FF_CHUNK * c, FF_CHUNK * (c + 1))
            gt = _dot(h2, wg[:, sl])
            up = _dot(h2, wu[:, sl])
            gt_s[:, sl] = gt
            up_s[:, sl] = up
            act = (gt * _sigmoid(gt) * up).astype(BF16)
            act_ref[:, sl] = act
            f += _dot(act, wd[sl, :])
        x1 = x1_ref[...]
        rnf = _rms(f)
        e = x1 + f * rnf * gff_ref[...] - tg_ref[...]
        part = 0.5 * jnp.sum(jnp.sum(e * e, axis=1, keepdims=True), axis=0, keepdims=True) * (1.0 / D_MODEL)
        lane = lax.broadcasted_iota(jnp.int32, (1, LANE), 1)
        loss_ref[...] += jnp.where(lane == 0, part, 0.0)
        dx2 = e * (1.0 / D_MODEL)
        dgff_ref[...] += jnp.sum(dx2 * f * rnf, axis=0, keepdims=True)
        df = _rms_bwd(dx2 * gff_ref[...], f, rnf).astype(BF16)
        df_ref[...] = df
        dh2 = jnp.zeros((TT, D_MODEL), F32)
        for c in range(D_FF // FF_CHUNK):
            sl = slice(FF_CHUNK * c, FF_CHUNK * (c + 1))
            dact = _dot(df, wd[sl, :], NT)
            gt = gt_s[:, sl]
            up = up_s[:, sl]
            sg = _sigmoid(gt)
            dgt = (dact * up * (sg * (1.0 + gt * (1.0 - sg)))).astype(BF16)
            dup = (dact * gt * sg).astype(BF16)
            dgt_ref[:, sl] = dgt
            dup_ref[:, sl] = dup
            dh2 += _dot(dgt, wg[:, sl], NT) + _dot(dup, wu[:, sl], NT)
        rn2 = _rms(x1)
        dgpf_ref[...] += jnp.sum(dh2 * x1 * rn2, axis=0, keepdims=True)
        dx1_ref[...] = dx2 + _rms_bwd(dh2 * gpf_ref[...], x1, rn2)

    tile = pl.BlockSpec((TT, D_MODEL), lambda i: (i, 0))
    wide = pl.BlockSpec((TT, D_FF), lambda i: (i, 0))
    vec = pl.BlockSpec((1, D_MODEL), lambda i: (0, 0))
    hbm = pl.BlockSpec(memory_space=pl.ANY)
    return pl.pallas_call(
        body, name="ffn_fwd_bwd", grid=(T // TT,),
        in_specs=[tile, tile, tile, hbm, hbm, hbm, vec, vec],
        out_specs=[tile, wide, tile, wide, wide, vec, vec, pl.BlockSpec((1, LANE), lambda i: (0, 0))],
        out_shape=[jax.ShapeDtypeStruct((T, D_MODEL), F32), jax.ShapeDtypeStruct((T, D_FF), BF16),
                   jax.ShapeDtypeStruct((T, D_MODEL), BF16), jax.ShapeDtypeStruct((T, D_FF), BF16),
                   jax.ShapeDtypeStruct((T, D_FF), BF16), jax.ShapeDtypeStruct((1, D_MODEL), F32),
                   jax.ShapeDtypeStruct((1, D_MODEL), F32), jax.ShapeDtypeStruct((1, LANE), F32)],
        scratch_shapes=[pltpu.VMEM((D_MODEL, D_FF), BF16), pltpu.VMEM((D_MODEL, D_FF), BF16),
                        pltpu.VMEM((D_FF, D_MODEL), BF16), pltpu.VMEM((TT, D_FF), F32),
                        pltpu.VMEM((TT, D_FF), F32), pltpu.SemaphoreType.DMA((3,))],
        compiler_params=_params(1),
    )(x1, h2, target, w_gate, w_up, w_down, g_pf, g_ff)


def _outproj_bwd(dx1, mix, w_out, g_pm):
    T = dx1.shape[0]
    TT = TP

    def body(dx1_ref, mix_ref, wo_ref, gpm_ref, dyl_ref, dys_ref, dmix_ref, dg_ref):
        @pl.when(pl.program_id(0) == 0)
        def _():
            dg_ref[...] = jnp.zeros_like(dg_ref)

        mix = mix_ref[...]
        rn = _rms(mix)
        dx1v = dx1_ref[...]
        dg_ref[...] += jnp.sum(dx1v * mix * rn, axis=0, keepdims=True)
        dmix = _rms_bwd(dx1v * gpm_ref[...], mix, rn).astype(BF16)
        dmix_ref[...] = dmix
        dyl_ref[...] = _dot(dmix, wo_ref[0:LRU_W, :], NT)
        dys_ref[...] = _dot(dmix, wo_ref[LRU_W:2 * LRU_W, :], NT)

    tile = pl.BlockSpec((TT, D_MODEL), lambda i: (i, 0))
    vec = pl.BlockSpec((1, D_MODEL), lambda i: (0, 0))
    return pl.pallas_call(
        body, name="outproj_bwd", grid=(T // TT,),
        in_specs=[tile, tile, pl.BlockSpec((2 * LRU_W, D_MODEL), lambda i: (0, 0)), vec],
        out_specs=[tile, tile, tile, vec],
        out_shape=[jax.ShapeDtypeStruct((T, D_MODEL), F32), jax.ShapeDtypeStruct((T, D_MODEL), F32),
                   jax.ShapeDtypeStruct((T, D_MODEL), BF16), jax.ShapeDtypeStruct((1, D_MODEL), F32)],
        compiler_params=_params(1),
    )(dx1, mix, w_out, g_pm)


def _tn_matmul(a, bs, name, tk=512):
    T, M = a.shape
    nk = T // tk
    nb = len(bs)

    def body(*refs):
        a_ref, b_refs, o_refs, accs = refs[0], refs[1:1 + nb], refs[1 + nb:1 + 2 * nb], refs[1 + 2 * nb:]
        k = pl.program_id(0)

        @pl.when(k == 0)
        def _():
            for acc in accs:
                acc[...] = jnp.zeros_like(acc)

        av = a_ref[...].astype(BF16)
        for b_ref, acc in zip(b_refs, accs):
            acc[...] += _dot(av, b_ref[...], TN)

        @pl.when(k == nk - 1)
        def _():
            for o_ref, acc in zip(o_refs, accs):
                o_ref[...] = acc[...].astype(BF16)

    return pl.pallas_call(
        body, name=name, grid=(nk,),
        in_specs=[pl.BlockSpec((tk, M), lambda k: (k, 0))]
                 + [pl.BlockSpec((tk, b.shape[1]), lambda k: (k, 0)) for b in bs],
        out_specs=[pl.BlockSpec((M, b.shape[1]), lambda k: (0, 0)) for b in bs],
        out_shape=[jax.ShapeDtypeStruct((M, b.shape[1]), BF16) for b in bs],
        scratch_shapes=[pltpu.VMEM((M, b.shape[1]), F32) for b in bs],
        compiler_params=_params(1),
    )(a, *bs)


def _tn_matmul_slabs(a, bs, name, slab, tk=512):
    T, M = a.shape
    nk = T // tk
    nb = len(bs)
    offs = [sum(b.shape[1] for b in bs[:i]) for i in range(nb + 1)]

    def body(*refs):
        a_ref, b_refs, o_ref, acc = refs[0], refs[1:1 + nb], refs[1 + nb], refs[2 + nb]
        k = pl.program_id(0)

        @pl.when(k == 0)
        def _():
            acc[...] = jnp.zeros_like(acc)

        av = a_ref[...].astype(BF16)
        for i, b_ref in enumerate(b_refs):
            acc[:, offs[i]:offs[i + 1]] += _dot(av, b_ref[...], TN)

        @pl.when(k == nk - 1)
        def _():
            for j in range(N_DEV):
                o_ref[j] = acc[:, slab * j:slab * (j + 1)].astype(BF16)

    return pl.pallas_call(
        body, name=name, grid=(nk,),
        in_specs=[pl.BlockSpec((tk, M), lambda k: (k, 0))]
                 + [pl.BlockSpec((tk, b.shape[1]), lambda k: (k, 0)) for b in bs],
        out_specs=pl.BlockSpec((N_DEV, M, slab), lambda k: (0, 0, 0)),
        out_shape=jax.ShapeDtypeStruct((N_DEV, M, slab), BF16),
        scratch_shapes=[pltpu.VMEM((M, offs[-1]), F32)],
        compiler_params=_params(1),
    )(a, *bs)


def _tn_blockdiag(a, b1, b2, name, tk=1024):
    T = a.shape[0]
    tk = min(tk, T)

    def body(a_ref, b1_ref, b2_ref, o1_ref, o2_ref):
        @pl.when(pl.program_id(0) == 0)
        def _():
            o1_ref[...] = jnp.zeros_like(o1_ref)
            o2_ref[...] = jnp.zeros_like(o2_ref)

        for j in range(4):
            sl = slice(256 * j, 256 * (j + 1))
            av = a_ref[:, sl].astype(BF16)
            o1_ref[j] += _dot(av, b1_ref[:, sl], TN)
            o2_ref[j] += _dot(av, b2_ref[:, sl], TN)

    blk = pl.BlockSpec((tk, LRU_W), lambda k: (k, 0))
    out = pl.BlockSpec((4, 256, 256), lambda k: (0, 0, 0))
    return pl.pallas_call(
        body, name=name, grid=(T // tk,),
        in_specs=[blk, blk, blk], out_specs=[out, out],
        out_shape=[jax.ShapeDtypeStruct((4, 256, 256), F32)] * 2,
        compiler_params=_params(1),
    )(a, b1, b2)


def _adamw_update(g, w, m, v):
    nm = ADAM_B1 * m + (1.0 - ADAM_B1) * g
    nv = ADAM_B2 * v + (1.0 - ADAM_B2) * (g * g)
    m_hat = nm / (1.0 - ADAM_B1 ** ADAM_STEP)
    v_hat = nv / (1.0 - ADAM_B2 ** ADAM_STEP)
    return -ADAM_LR * (m_hat / (jnp.sqrt(v_hat) + ADAM_EPS) + ADAM_WD * w), nm, nv


def _adamw_small(parts, params):
    n = len(params)

    def body(*refs):
        p_ref, wmv = refs[0], refs[1:1 + 3 * n]
        gs_ref, outs = refs[1 + 3 * n], refs[2 + 3 * n:]
        g = p_ref[0]
        for k in range(1, N_DEV):
            g = g + p_ref[k]
        gs_ref[...] = g
        for i, (row, w, _, _) in enumerate(params):
            rows, width = w.shape
            if width <= 1024:
                gp = gs_ref[row:row + rows, 0:width]
            else:
                gp = jnp.concatenate([gs_ref[row:row + 1, :], gs_ref[row + 1:row + 2, 0:width - 1024]], axis=1)
            d, nm, nv = _adamw_update(gp, wmv[3 * i][...], wmv[3 * i + 1][...], wmv[3 * i + 2][...])
            for ref, val in zip(outs[4 * i:4 * i + 4], (gp, d, nm, nv)):
                ref[...] = val

    full = lambda s: pl.BlockSpec(s, lambda: (0,) * len(s))
    shapes = [w.shape for _, w, _, _ in params]
    return pl.pallas_call(
        body, name="adamw_small",
        in_specs=[full(parts.shape)] + [full(s) for s in shapes for _ in range(3)],
        out_specs=[full(parts.shape[1:])] + [full(s) for s in shapes for _ in range(4)],
        out_shape=[jax.ShapeDtypeStruct(parts.shape[1:], F32)]
                  + [jax.ShapeDtypeStruct(s, F32) for s in shapes for _ in range(4)],
        compiler_params=pltpu.CompilerParams(vmem_limit_bytes=VMEM_LIMIT),
    )(parts, *[a for _, w, m, v in params for a in (w, m, v)])


def _adamw(parts, w, m, v, name, tr):
    P, R, C = parts.shape

    def body(p_ref, w_ref, m_ref, v_ref, g_ref, d_ref, nm_ref, nv_ref):
        g = p_ref[0].astype(F32)
        for k in range(1, P):
            g = g + p_ref[k].astype(F32)
        g_ref[...] = g
        d_ref[...], nm_ref[...], nv_ref[...] = _adamw_update(g, w_ref[...], m_ref[...], v_ref[...])

    blk = pl.BlockSpec((tr, C), lambda i: (i, 0))
    return pl.pallas_call(
        body, name=name, grid=(R // tr,),
        in_specs=[pl.BlockSpec((P, tr, C), lambda i: (0, i, 0)), blk, blk, blk],
        out_specs=[blk, blk, blk, blk],
        out_shape=[jax.ShapeDtypeStruct((R, C), F32)] * 4,
        compiler_params=_params(1),
    )(parts, w, m, v)


def _peer(k):
    x, y, c = lax.axis_index("x"), lax.axis_index("y"), lax.axis_index("c")
    px = x ^ ((k >> 2) & 1)
    py = y ^ ((k >> 1) & 1)
    pc = c ^ (k & 1)
    return (px, py, pc), 4 * px + 2 * py + pc


def _my_block():
    return 4 * lax.axis_index("x") + 2 * lax.axis_index("y") + lax.axis_index("c")


def _all_gather(shards, name):
    n = len(shards)

    def body(*refs):
        ins, outs = refs[:n], refs[n:2 * n]
        send, recv, loc = refs[2 * n:]
        x, y, c = lax.axis_index("x"), lax.axis_index("y"), lax.axis_index("c")
        sibling = (x, y, 1 - c)
        chips = [(1 - x, y), (x, 1 - y), (1 - x, 1 - y)]
        slot = lambda px, py, pc: 4 * px + 2 * py + pc

        def copy(a, k, block, to, src=None):
            dst = outs[a].at[slot(*block)]
            return pltpu.make_async_remote_copy(
                src_ref=dst if src is None else src, dst_ref=dst, send_sem=send.at[a, k], recv_sem=recv.at[a, k],
                device_id=to, device_id_type=pl.DeviceIdType.MESH)

        mine = [pltpu.make_async_copy(ins[a], outs[a].at[slot(x, y, c)], loc.at[a]) for a in range(n)]
        for cp in mine:
            cp.start()
        first = []
        for a in range(n):
            first.append(copy(a, 0, (x, y, c), sibling, src=ins[a]))
            first += [copy(a, 1 + j, (x, y, c), (*chip, c), src=ins[a]) for j, chip in enumerate(chips)]
        for cp in first:
            cp.start()
        passed = []
        for j, chip in enumerate(chips):
            for a in range(n):
                copy(a, 1 + j, (*chip, c), (x, y, c)).wait_recv()
                fwd = copy(a, 4 + j, (*chip, c), sibling)
                fwd.start()
                passed.append(fwd)
        for a in range(n):
            copy(a, 0, sibling, (x, y, c)).wait_recv()
            for j, chip in enumerate(chips):
                copy(a, 4 + j, (*chip, 1 - c), (x, y, c)).wait_recv()
        for cp in first + passed:
            cp.wait_send()
        for cp in mine:
            cp.wait()

    hbm = pl.BlockSpec(memory_space=pl.ANY)
    return pl.pallas_call(
        body, name=name,
        in_specs=[hbm] * n, out_specs=[hbm] * n,
        out_shape=[jax.ShapeDtypeStruct((N_DEV,) + s.shape, s.dtype) for s in shards],
        scratch_shapes=[pltpu.SemaphoreType.DMA((n, N_DEV - 1)), pltpu.SemaphoreType.DMA((n, N_DEV - 1)),
                        pltpu.SemaphoreType.DMA((n,))],
    )(*shards)


_HBM = pl.BlockSpec(memory_space=pltpu.HBM)
_SEM = pl.BlockSpec(memory_space=pltpu.SEMAPHORE)
_EFFECT = pltpu.SideEffectType.DATAFLOW_SIDE_EFFECTING


def _direct_copies(srcs, lands, send, recv, slab_source):
    me = _my_block()
    cps = []
    for k in range(1, N_DEV):
        to, blk = _peer(k)
        for a, (src, land) in enumerate(zip(srcs, lands)):
            if slab_source:
                s, d = src.at[blk], land.at[me]
            elif land.ndim == 3:
                s, d = src, land.at[me]
            else:
                s, d = src, land.at[pl.ds(pl.multiple_of(me * src.shape[0], 16), src.shape[0]), :]
            cps.append(pltpu.make_async_remote_copy(
                src_ref=s, dst_ref=d,
                send_sem=send.at[a * (N_DEV - 1) + k - 1], recv_sem=recv.at[a * (N_DEV - 1) + k - 1],
                device_id=to, device_id_type=pl.DeviceIdType.MESH))
    return cps


def _exchange_start(srcs, name, slab_source, axes=None):
    n = len(srcs)
    if slab_source:
        shapes = [s.shape for s in srcs]
    else:
        shapes = [(N_DEV,) + s.shape if ax == 1 else (N_DEV * s.shape[0], s.shape[1]) for s, ax in zip(srcs, axes)]
    lands = [pltpu.with_memory_space_constraint(lax.empty(sh, s.dtype), pltpu.HBM) for sh, s in zip(shapes, srcs)]

    def body(*refs):
        ins, land_in = refs[:n], refs[n:2 * n]
        send, recv = refs[2 * n], refs[2 * n + 1]
        token = refs[4 * n + 2]
        for cp in _direct_copies(ins, land_in, send, recv, slab_source):
            cp.start()
        token[...] = jnp.zeros_like(token)

    sems = pltpu.SemaphoreType.DMA((n * (N_DEV - 1),))
    res = pl.pallas_call(
        body, name=name,
        out_shape=(sems, sems, *[pltpu.HBM(s.shape, s.dtype) for s in srcs],
                   *[pltpu.HBM(l.shape, l.dtype) for l in lands], jax.ShapeDtypeStruct((8, LANE), F32)),
        in_specs=[_HBM] * (2 * n),
        out_specs=(_SEM, _SEM, *[_HBM] * (2 * n), pl.BlockSpec(memory_space=pltpu.VMEM)),
        input_output_aliases={i: 2 + i for i in range(2 * n)},
        compiler_params=pltpu.CompilerParams(has_side_effects=_EFFECT),
    )(*[pltpu.with_memory_space_constraint(s, pltpu.HBM) for s in srcs], *lands)
    return dict(send=res[0], recv=res[1], srcs=res[2:2 + n], lands=res[2 + n:2 + 2 * n], token=res[-1],
                slab_source=slab_source)


def _exchange_wait(ex, after, name):
    n = len(ex["srcs"])
    slab_source = ex["slab_source"]

    def body(*refs):
        ins, lands = refs[:n], refs[n:2 * n]
        send, recv = refs[2 * n], refs[2 * n + 1]
        for cp in _direct_copies(ins, lands, send, recv, slab_source):
            cp.wait_send()
            cp.wait_recv()

    res = pl.pallas_call(
        body, name=name,
        out_shape=tuple(pltpu.HBM(s.shape, s.dtype) for s in list(ex["srcs"]) + list(ex["lands"])),
        in_specs=[_HBM] * (2 * n) + [_SEM, _SEM, pl.BlockSpec(memory_space=pl.ANY)],
        out_specs=tuple([_HBM] * (2 * n)),
        input_output_aliases={i: i for i in range(2 * n)},
        compiler_params=pltpu.CompilerParams(has_side_effects=_EFFECT),
    )(*ex["srcs"], *ex["lands"], ex["send"], ex["recv"], after)
    me = _my_block()
    out = []
    for src, land in zip(res[:n], res[n:]):
        if slab_source:
            own, at = lax.dynamic_index_in_dim(src, me, 0, keepdims=True), (me, 0, 0)
        elif land.ndim == 3:
            own, at = src[None], (me, 0, 0)
        else:
            own, at = src, (me * src.shape[0], 0)
        out.append(lax.dynamic_update_slice(land, own, at))
    return out


BIG = ("w_in", "w_out", "w_gate", "w_up", "w_down")
BIG_SHARD = {"w_in": (1024, 578), "w_out": (256, 1024), "w_gate": (1024, 352), "w_up": (1024, 352),
             "w_down": (352, 1024)}
BIG_SHARD_AXIS = {"w_in": 1, "w_out": 0, "w_gate": 1, "w_up": 1, "w_down": 0}
BIG_ADAM_ROWS = {"w_in": 256, "w_out": 128, "w_gate": 256, "w_up": 256, "w_down": 176}


def _join(parts, axis):
    if axis == 0:
        return parts.reshape((-1,) + parts.shape[2:])
    return jnp.concatenate([parts[j] for j in range(N_DEV)], axis=1)


def _split(full, axis):
    if axis == 0:
        return full.reshape((N_DEV, full.shape[0] // N_DEV) + full.shape[1:])
    c = full.shape[1] // N_DEV
    return jnp.stack([full[:, c * j:c * (j + 1)] for j in range(N_DEV)])


SMALL = (("lru_wa", 65536), ("lru_wx", 65536), ("pre_mix_norm", 1024), ("lru_conv_w", 4096), ("lru_conv_b", 1024),
         ("lru_ba", 1024), ("lru_bx", 1024), ("lru_lambda", 1024), ("lru_out_norm", 1024), ("ssd_conv_w", 6144),
         ("ssd_conv_b", 1536), ("ssd_dt_bias", 16), ("ssd_a_log", 16), ("ssd_d", 16), ("ssd_out_norm", 1024),
         ("post_mix_norm", 1024), ("pre_ffn_norm", 1024), ("post_ffn_norm", 1024), ("loss", 1))
REPLICATED = tuple(n for n, _ in SMALL if n not in ("lru_conv_w", "ssd_conv_w", "loss"))
SMALL_ROW = {}
for _name, _size in SMALL:
    SMALL_ROW[_name] = (sum(-(-s // 1024) for n, s in SMALL[:len(SMALL_ROW)]), -(-_size // 1024))


def _pack_small(d):
    rows = [jnp.pad(d[n].astype(F32).reshape(-1), (0, SMALL_ROW[n][1] * 1024 - s)).reshape(-1, 1024) for n, s in SMALL]
    used = sum(r.shape[0] for r in rows)
    return jnp.concatenate(rows + [jnp.zeros((SMALL_ROWS - used, 1024), F32)], axis=0)


def _small_entry(p, name):
    row, rows = SMALL_ROW[name]
    return p[row:row + rows].reshape(-1)[:dict(SMALL)[name]]


def _blockdiag4(w):
    on_diag = jnp.eye(4, dtype=w.dtype)[None, :, None, :, None]
    return (w.reshape(4, 4, 64, 1, 64) * on_diag).reshape(4, 256, 256)


def _diag_blocks(g):
    on_diag = jnp.eye(4, dtype=g.dtype)[None, :, None, :, None]
    return jnp.sum(g.reshape(4, 4, 64, 4, 64) * on_diag, axis=3).reshape(16, 64, 64)


def _local_step(x, target, w_in, P, rest_weights, emit, emit_small, start_token=None):
    cut = IN_MAIN - (N_DEV - 1) * (IN_COLS // N_DEV)
    w_main = jnp.concatenate([w_in[j] for j in range(N_DEV - 1)] + [w_in[N_DEV - 1][:, :cut]], axis=1)
    w_dt = jnp.pad(w_in[N_DEV - 1][:, cut:], ((0, 0), (0, LANE - SSD_HEADS)))
    pad16 = lambda v: jnp.pad(v.reshape(1, SSD_HEADS), ((0, 0), (0, LANE - SSD_HEADS)))
    dt_bias, a_log = pad16(P["ssd_dt_bias"]), pad16(P["ssd_a_log"])
    d_e = jnp.repeat(P["ssd_d"].reshape(SSD_HEADS), SSD_HEAD_DIM).reshape(1, SSD_INNER)
    expand = (jnp.arange(LANE)[:, None] == (jnp.arange(SSD_INNER)[None, :] // SSD_HEAD_DIM)).astype(BF16)
    wa_bd = _blockdiag4(P["lru_wa"].astype(BF16))
    wx_bd = _blockdiag4(P["lru_wx"].astype(BF16))
    vec = lambda n: P[n].reshape(1, -1)

    after = lambda v, tok: v if tok is None else v + tok[0:1, 0:1]

    h, proj, dtp = _inproj_fwd(x, after(vec("pre_mix_norm"), start_token), w_main, w_dt)
    lx, hl, y_lru, *gates = _lru_fwd(proj, P["lru_conv_w"], vec("lru_conv_b"), wa_bd, wx_bd, vec("lru_ba"),
                                     vec("lru_bx"), vec("lru_lambda"), vec("lru_out_norm"))
    xc, y, y_ssd, sprev = _ssd_fwd(proj, dtp, P["ssd_conv_w"], vec("ssd_conv_b"), dt_bias, a_log, d_e,
                                   vec("ssd_out_norm"), expand)
    W = rest_weights(y_ssd)
    mix, x1, h2 = _outproj_fwd(x, y_lru, y_ssd, W["w_out"], vec("post_mix_norm"), vec("pre_ffn_norm"))
    dx1, act, df, dgt, dup, dg_pf, dg_ff, loss = _ffn_fwd_bwd(
        x1, h2, target, W["w_gate"], W["w_up"], W["w_down"], vec("pre_ffn_norm"), vec("post_ffn_norm"))
    tok = emit("ffn", {"w_gate": _tn_matmul_slabs(h2, [dgt], "dw_gate", D_FF // N_DEV),
                       "w_up": _tn_matmul_slabs(h2, [dup], "dw_up", D_FF // N_DEV),
                       "w_down": _tn_matmul(act, [df], "dw_down")[0]})
    dy_lru, dy_ssd, dmix, dg_pm = _outproj_bwd(dx1, mix, W["w_out"], after(vec("post_mix_norm"), tok))
    tok = emit("out", {"w_out": jnp.concatenate([_tn_matmul(y_lru, [dmix], "dw_out_lru")[0],
                                                 _tn_matmul(y_ssd, [dmix], "dw_out_ssd")[0]], axis=0)})
    dp_ssd, ddtp, dcw_s, dcb_s, dbias, dA, dD_e, dg_ssd = _ssd_bwd(
        dy_ssd, proj, dtp, xc, y, sprev, P["ssd_conv_w"], dt_bias, a_log, d_e,
        after(vec("ssd_out_norm"), tok), expand)
    dp_lru, dpa, dpx, dcw_l, dcb_l, dba, dbx, dlam, dg_lru = _lru_bwd(
        dy_lru, proj, lx, hl, gates, P["lru_conv_w"], wa_bd, wx_bd, vec("lru_lambda"), vec("lru_out_norm"))
    tok = emit("in", {"w_in": _tn_matmul_slabs(h, [dp_lru, dp_ssd, ddtp], "dw_in", IN_COLS // N_DEV, tk=256)})

    a_neg = -jnp.exp(P["ssd_a_log"].reshape(SSD_HEADS))
    dwa, dwx = _tn_blockdiag(lx, dpa, dpx, "dw_lru_gates")
    small = {
        "pre_mix_norm": jnp.zeros((1, D_MODEL), F32), "lru_conv_w": dcw_l, "lru_conv_b": dcb_l,
        "lru_wa": _diag_blocks(dwa), "lru_ba": dba,
        "lru_wx": _diag_blocks(dwx), "lru_bx": dbx,
        "lru_lambda": dlam, "lru_out_norm": dg_lru, "ssd_conv_w": dcw_s, "ssd_conv_b": dcb_s,
        "ssd_dt_bias": dbias[0, :SSD_HEADS], "ssd_a_log": dA[0, :SSD_HEADS] * a_neg,
        "ssd_d": jnp.sum(dD_e.reshape(SSD_HEADS, SSD_HEAD_DIM), axis=1), "ssd_out_norm": dg_ssd,
        "post_mix_norm": dg_pm, "pre_ffn_norm": dg_pf, "post_ffn_norm": dg_ff, "loss": loss[0, 0:1],
    }
    tok = after(after(vec("pre_mix_norm"), tok), emit_small(small))
    grad_x, dg_pre = _inproj_bwd(dp_lru, dp_ssd, ddtp, dx1, x, tok, w_main, w_dt)
    return grad_x, dg_pre


def kernel(x, pre_mix_norm, w_in, lru_conv_w, lru_conv_b, lru_wa, lru_ba, lru_wx, lru_bx, lru_lambda, lru_out_norm, ssd_conv_w, ssd_conv_b, ssd_dt_bias, ssd_a_log, ssd_d, ssd_out_norm, w_out, post_mix_norm, pre_ffn_norm, w_gate, w_up, w_down, post_ffn_norm, loss_target, m_pre_mix_norm, m_w_in, m_lru_conv_w, m_lru_conv_b, m_lru_wa, m_lru_ba, m_lru_wx, m_lru_bx, m_lru_lambda, m_lru_out_norm, m_ssd_conv_w, m_ssd_conv_b, m_ssd_dt_bias, m_ssd_a_log, m_ssd_d, m_ssd_out_norm, m_w_out, m_post_mix_norm, m_pre_ffn_norm, m_w_gate, m_w_up, m_w_down, m_post_ffn_norm, v_pre_mix_norm, v_w_in, v_lru_conv_w, v_lru_conv_b, v_lru_wa, v_lru_ba, v_lru_wx, v_lru_bx, v_lru_lambda, v_lru_out_norm, v_ssd_conv_w, v_ssd_conv_b, v_ssd_dt_bias, v_ssd_a_log, v_ssd_d, v_ssd_out_norm, v_w_out, v_post_mix_norm, v_pre_ffn_norm, v_w_gate, v_w_up, v_w_down, v_post_ffn_norm):
    a = dict(locals())
    names = [n for n, _ in SMALL if n != "loss"] + list(BIG)
    w = {n: a[n][0] for n in names}
    m = {n: a["m_" + n][0] for n in names}
    v = {n: a["v_" + n][0] for n in names}

    cpack = jnp.concatenate([w["lru_conv_w"], w["ssd_conv_w"], jnp.zeros((4, 64), F32)], axis=1)
    cpack = jnp.pad(cpack, ((0, 4), (0, 0)))
    g_in, cg = _all_gather([w["w_in"].astype(BF16), cpack], "all_gather_w_in")
    P = {n: w[n] for n in REPLICATED}
    P["lru_conv_w"] = _join(cg[:, 0:4, 0:128], 1)
    P["ssd_conv_w"] = _join(cg[:, 0:4, 128:320], 1)

    rest = [n for n in BIG if n != "w_in"]
    zero = jnp.minimum(jnp.abs(cg[0, 0, 0]), 0.0)
    ex_w = _exchange_start([(w[n] + zero).astype(BF16) for n in rest], "weights_start", slab_source=False,
                           axes=[BIG_SHARD_AXIS[n] for n in rest])

    def rest_weights(after):
        lands = _exchange_wait(ex_w, after, "weights_wait")
        return {n: _join(p, 1) if p.ndim == 3 else p for n, p in zip(rest, lands)}

    pending = []

    def emit(group, grads):
        ex = _exchange_start([g if g.ndim == 3 else _split(g, BIG_SHARD_AXIS[n]) for n, g in grads.items()],
                             "grads_start_" + group, slab_source=True)
        pending.append((group, list(grads), ex))
        return ex["token"]

    def emit_small(small):
        ex = _exchange_start([_pack_small(small)], "small_start", slab_source=False, axes=[0])
        pending.append(("small", None, ex))
        return ex["token"]

    grad_x, dg_pre = _local_step(x[0], loss_target[0], g_in, P, rest_weights, emit, emit_small, ex_w["token"])

    got_pre = _all_gather([jnp.pad(dg_pre, ((0, 7), (0, 0)))], "all_gather_pre_mix_norm")[0]
    ex_small = pending.pop()[2]
    got_small = _exchange_wait(ex_small, got_pre, "small_wait")[0].reshape(N_DEV, SMALL_ROWS, 1024)
    row = SMALL_ROW["pre_mix_norm"][0]
    got_small = got_small.at[:, row:row + 1, :].set(got_pre[:, 0:1, :])

    outs = {}
    done = got_small
    for group, group_names, ex in pending:
        for n, parts in zip(group_names, _exchange_wait(ex, done, "grads_wait_" + group)):
            res = _adamw(parts, w[n], m[n], v[n], "adamw_" + n, BIG_ADAM_ROWS[n])
            done = res[0]
            for kind, r in zip(("grad", "delta", "new_m", "new_v"), res):
                outs[kind + "_" + n] = r

    two_d = lambda t: t.reshape(-1, 1024) if t.ndim == 3 else t.reshape(1, -1)
    res = _adamw_small(got_small, [(SMALL_ROW[n][0], two_d(w[n]), two_d(m[n]), two_d(v[n])) for n in REPLICATED])
    g_pack = res[0]
    for i, n in enumerate(REPLICATED):
        for kind, r in zip(("grad", "delta", "new_m", "new_v"), res[1 + 4 * i:5 + 4 * i]):
            outs[kind + "_" + n] = r.reshape(w[n].shape)

    me = _my_block()
    gl = lax.dynamic_slice(_small_entry(g_pack, "lru_conv_w").reshape(4, LRU_W), (0, me * 128), (4, 128))
    gs = lax.dynamic_slice(_small_entry(g_pack, "ssd_conv_w").reshape(4, SSD_CONV_CH), (0, me * 192), (4, 192))
    cat = lambda d: jnp.pad(jnp.concatenate([d["lru_conv_w"], d["ssd_conv_w"]], axis=1), ((0, 4), (0, 64)))
    res = _adamw(cat({"lru_conv_w": gl, "ssd_conv_w": gs})[None], cat(w), cat(m), cat(v), "adamw_conv", 8)
    for kind, r in zip(("grad", "delta", "new_m", "new_v"), res):
        outs[kind + "_lru_conv_w"] = r[0:4, 0:128]
        outs[kind + "_ssd_conv_w"] = r[0:4, 128:320]

    order = ["pre_mix_norm", "w_in", "lru_conv_w", "lru_conv_b", "lru_wa", "lru_ba", "lru_wx", "lru_bx", "lru_lambda",
             "lru_out_norm", "ssd_conv_w", "ssd_conv_b", "ssd_dt_bias", "ssd_a_log", "ssd_d", "ssd_out_norm", "w_out",
             "post_mix_norm", "pre_ffn_norm", "w_gate", "w_up", "w_down", "post_ffn_norm"]
    result = [_small_entry(g_pack, "loss").reshape(()), grad_x[None]]
    for kind in ("grad", "delta", "new_m", "new_v"):
        result += [outs[kind + "_" + n][None] for n in order]
    return tuple(result)
```

```python
import functools

import jax
import jax.numpy as jnp
from jax import lax
from jax.experimental import pallas as pl
from jax.experimental.pallas import tpu as pltpu

F32 = jnp.float32
BF16 = jnp.bfloat16
EPS = 1e-6
N_DEV = 8
D_MODEL = 1024
LRU_W = 1024
SSD_INNER = 1024
SSD_HEADS = 16
SSD_HEAD_DIM = 64
SSD_STATE = 128
SSD_GROUPS = 2
SSD_CONV_CH = 1536
CHUNK = 128
D_FF = 2816
FF_CHUNK = 2816
IN_MAIN = 4608
IN_COLS = 4624
LANE = 128
TT = 256
TP = 512
VMEM_LIMIT = 56 * 1024 * 1024
ADAM_LR, ADAM_B1, ADAM_B2, ADAM_EPS, ADAM_WD, ADAM_STEP = 0.001, 0.9, 0.999, 1e-08, 0.01, 10
SMALL_ROWS = 160

NT = (((1,), (1,)), ((), ()))
TN = (((0,), (0,)), ((), ()))


def _params(n_grid):
    return pltpu.CompilerParams(dimension_semantics=("arbitrary",) * n_grid, vmem_limit_bytes=VMEM_LIMIT)


def _dot(a, b, dims=None, precision=None):
    if dims is None:
        return jnp.dot(a, b, preferred_element_type=F32, precision=precision)
    return lax.dot_general(a, b, dims, preferred_element_type=F32, precision=precision)


def _split_bf16(x, terms):
    out = []
    for _ in range(terms - 1):
        p = x.astype(BF16)
        out.append(p)
        x = x - p.astype(F32)
    return out + [x.astype(BF16)]


def _dot_sel(x, sel, dims=None, terms=2, sel_first=False):
    parts = [_dot(sel, p, dims) if sel_first else _dot(p, sel, dims) for p in _split_bf16(x, terms)]
    return functools.reduce(lambda a, b: a + b, parts)


def _sigmoid(x):
    return 0.5 * jnp.tanh(0.5 * x) + 0.5


def _softplus(x):
    e = jnp.exp(-jnp.abs(x))
    l1p = jnp.where(e < 1e-3, e * (1.0 - e * (0.5 - e * (1.0 / 3.0))), jnp.log(1.0 + e))
    return jnp.maximum(x, 0.0) + l1p


def _neg_expm1(x):
    series = -x * (1.0 + x * (0.5 + x * (1.0 / 6.0 + x * (1.0 / 24.0))))
    return jnp.where(x > -0.01, series, 1.0 - jnp.exp(x))


_GELU_C = 0.7978845608028654


def _gelu(x):
    t = jnp.tanh(_GELU_C * (x + 0.044715 * x * x * x))
    return 0.5 * x * (1.0 + t), t


def _gelu_grad(x, t):
    return 0.5 * (1.0 + t) + 0.5 * x * (1.0 - t * t) * _GELU_C * (1.0 + 3.0 * 0.044715 * x * x)


def _rms(x):
    return lax.rsqrt(jnp.mean(x * x, axis=-1, keepdims=True) + EPS)


def _rms_bwd(dyn, x, rn):
    return rn * dyn - x * (rn * rn * rn) * jnp.mean(dyn * x, axis=-1, keepdims=True)


def _row(x, r):
    idx = lax.broadcasted_iota(jnp.int32, x.shape, 0)
    return jnp.sum(jnp.where(idx == r, x, 0.0), axis=0, keepdims=True)


def _shift_down(cur, prev8, j):
    s = pltpu.roll(cur, j, 0)
    p = pltpu.roll(prev8, j, 0)
    r8 = lax.broadcasted_iota(jnp.int32, prev8.shape, 0)
    top = jnp.where(r8 < j, p, s[0:8])
    return jnp.concatenate([top, s[8:]], axis=0)


def _shift_up(cur, next8, j):
    n = cur.shape[0]
    s = pltpu.roll(cur, n - j, 0)
    p = pltpu.roll(next8, 8 - j, 0)
    r8 = lax.broadcasted_iota(jnp.int32, next8.shape, 0)
    bot = jnp.where(r8 >= 8 - j, p, s[n - 8:n])
    return jnp.concatenate([s[:n - 8], bot], axis=0)


def _scan_fwd(a, u):
    n = a.shape[0]
    row = lax.broadcasted_iota(jnp.int32, a.shape, 0)
    k = 1
    while k < n:
        ok = row >= k
        a_s = jnp.where(ok, pltpu.roll(a, k, 0), 1.0)
        u_s = jnp.where(ok, pltpu.roll(u, k, 0), 0.0)
        u = a * u_s + u
        a = a * a_s
        k *= 2
    return a, u


def _scan_bwd(b, d):
    n = b.shape[0]
    row = lax.broadcasted_iota(jnp.int32, b.shape, 0)
    k = 1
    while k < n:
        ok = row < n - k
        b_s = jnp.where(ok, pltpu.roll(b, n - k, 0), 1.0)
        d_s = jnp.where(ok, pltpu.roll(d, n - k, 0), 0.0)
        d = b * d_s + d
        b = b * b_s
        k *= 2
    return b, d


def _scan_tile(a, u, carry, a_s, u_s, reverse):
    n, c = a.shape
    groups = n // 8
    r8 = lax.broadcasted_iota(jnp.int32, a.shape, 0) & 7
    in_group = lambda x, k: pltpu.roll(x.reshape(groups, 8, c), k, 1).reshape(n, c)
    for k in (1, 2, 4):
        ok = (r8 < 8 - k) if reverse else (r8 >= k)
        shift = 8 - k if reverse else k
        a_n = jnp.where(ok, in_group(a, shift), 1.0)
        u_n = jnp.where(ok, in_group(u, shift), 0.0)
        u = a * u_n + u
        a = a * a_n
    nl = c // LANE
    for j in range(nl):
        a_s[j] = a[:, LANE * j:LANE * (j + 1)]
        u_s[j] = u[:, LANE * j:LANE * (j + 1)]
    end = 0 if reverse else 7
    ends = lambda ref, j: ref[pl.ds(j, 1), pl.ds(end, groups, stride=8), :].reshape(groups, LANE)
    ga = jnp.concatenate([ends(a_s, j) for j in range(nl)], axis=1)
    gu = jnp.concatenate([ends(u_s, j) for j in range(nl)], axis=1)
    gacc, gh = (_scan_bwd if reverse else _scan_fwd)(ga, gu)
    gh = gh + gacc * carry
    grow = lax.broadcasted_iota(jnp.int32, gh.shape, 0)
    if reverse:
        cin = jnp.where(grow == groups - 1, carry, pltpu.roll(gh, groups - 1, 0))
    else:
        cin = jnp.where(grow == 0, carry, pltpu.roll(gh, 1, 0))
    spread = ((lax.broadcasted_iota(jnp.int32, (n, LANE), 0) >> 3)
              == lax.broadcasted_iota(jnp.int32, (n, LANE), 1)).astype(BF16)
    cin = jnp.concatenate([cin, jnp.zeros((LANE - groups, c), F32)], axis=0)
    return u + a * _dot_sel(cin, spread, terms=3, sel_first=True)


def _inproj_fwd(x, g_pre, w_main, w_dt):
    T = x.shape[0]
    TT = TP

    def body(x_ref, g_ref, wm_hbm, wd_hbm, h_ref, proj_ref, dtp_ref, wm, wd, sem):
        @pl.when(pl.program_id(0) == 0)
        def _():
            c1 = pltpu.make_async_copy(wm_hbm, wm, sem.at[0])
            c2 = pltpu.make_async_copy(wd_hbm, wd, sem.at[1])
            c1.start()
            c2.start()
            c1.wait()
            c2.wait()

        xv = x_ref[...]
        h = (xv * _rms(xv) * g_ref[...]).astype(BF16)
        h_ref[...] = h
        proj_ref[...] = _dot(h, wm[...])
        dtp_ref[...] = _dot(h, wd[...])

    return pl.pallas_call(
        body, name="inproj_fwd", grid=(T // TT,),
        in_specs=[pl.BlockSpec((TT, D_MODEL), lambda i: (i, 0)),
                  pl.BlockSpec((1, D_MODEL), lambda i: (0, 0)),
                  pl.BlockSpec(memory_space=pl.ANY), pl.BlockSpec(memory_space=pl.ANY)],
        out_specs=[pl.BlockSpec((TT, D_MODEL), lambda i: (i, 0)),
                   pl.BlockSpec((TT, IN_MAIN), lambda i: (i, 0)),
                   pl.BlockSpec((TT, LANE), lambda i: (i, 0))],
        out_shape=[jax.ShapeDtypeStruct((T, D_MODEL), BF16), jax.ShapeDtypeStruct((T, IN_MAIN), F32),
                   jax.ShapeDtypeStruct((T, LANE), F32)],
        scratch_shapes=[pltpu.VMEM((D_MODEL, IN_MAIN), BF16), pltpu.VMEM((D_MODEL, LANE), BF16),
                        pltpu.SemaphoreType.DMA((2,))],
        compiler_params=_params(1),
    )(x, g_pre, w_main, w_dt)


def _inproj_bwd(dp_lru, dp_ssd, ddtp, dx1, x, g_pre, w_main, w_dt):
    T = x.shape[0]
    TT = TP

    def body(dl_ref, ds_ref, dd_ref, dx1_ref, x_ref, g_ref, wm_hbm, wd_hbm, gx_ref, dg_ref, wm, wd, sem):
        @pl.when(pl.program_id(0) == 0)
        def _():
            c1 = pltpu.make_async_copy(wm_hbm, wm, sem.at[0])
            c2 = pltpu.make_async_copy(wd_hbm, wd, sem.at[1])
            c1.start()
            c2.start()
            c1.wait()
            c2.wait()
            dg_ref[...] = jnp.zeros_like(dg_ref)

        dh = _dot(dl_ref[...], wm[:, 0:2048], NT)
        dh += _dot(ds_ref[...], wm[:, 2048:IN_MAIN], NT)
        dh += _dot(dd_ref[...], wd[...], NT)
        xv = x_ref[...]
        rn = _rms(xv)
        dg_ref[...] += jnp.sum(dh * xv * rn, axis=0, keepdims=True)
        gx_ref[...] = dx1_ref[...] + _rms_bwd(dh * g_ref[...], xv, rn)

    return pl.pallas_call(
        body, name="inproj_bwd", grid=(T // TT,),
        in_specs=[pl.BlockSpec((TT, 2048), lambda i: (i, 0)),
                  pl.BlockSpec((TT, 2560), lambda i: (i, 0)),
                  pl.BlockSpec((TT, LANE), lambda i: (i, 0)),
                  pl.BlockSpec((TT, D_MODEL), lambda i: (i, 0)),
                  pl.BlockSpec((TT, D_MODEL), lambda i: (i, 0)),
                  pl.BlockSpec((1, D_MODEL), lambda i: (0, 0)),
                  pl.BlockSpec(memory_space=pl.ANY), pl.BlockSpec(memory_space=pl.ANY)],
        out_specs=[pl.BlockSpec((TT, D_MODEL), lambda i: (i, 0)),
                   pl.BlockSpec((1, D_MODEL), lambda i: (0, 0))],
        out_shape=[jax.ShapeDtypeStruct((T, D_MODEL), F32), jax.ShapeDtypeStruct((1, D_MODEL), F32)],
        scratch_shapes=[pltpu.VMEM((D_MODEL, IN_MAIN), BF16), pltpu.VMEM((D_MODEL, LANE), BF16),
                        pltpu.SemaphoreType.DMA((2,))],
        compiler_params=_params(1),
    )(dp_lru, dp_ssd, ddtp, dx1, x, g_pre, w_main, w_dt)


def _lru_gates(lx, wa_ref, wx_ref, ba, bx, lam):
    lxb = lx.astype(BF16)
    pa = jnp.concatenate([_dot(lxb[:, 256 * k:256 * (k + 1)], wa_ref[k]) for k in range(4)], axis=1) + ba
    px = jnp.concatenate([_dot(lxb[:, 256 * k:256 * (k + 1)], wx_ref[k]) for k in range(4)], axis=1) + bx
    r = _sigmoid(pa)
    ig = _sigmoid(px)
    sp = _softplus(-lam)
    log_a = -8.0 * r * sp
    a = jnp.exp(log_a)
    mult = jnp.sqrt(_neg_expm1(2.0 * log_a))
    return r, ig, sp, a, mult


def _lru_fwd(proj, conv_w, conv_b, wa_bd, wx_bd, ba, bx, lam, g_lru):
    T = proj.shape[0]

    def body(cx_ref, gate_ref, cw_ref, cb_ref, wa_ref, wx_ref, ba_ref, bx_ref, lam_ref, g_ref,
             lx_ref, hl_ref, y_ref, r_ref, ig_ref, a_ref, mult_ref, tail, hcar, sa, su):
        @pl.when(pl.program_id(0) == 0)
        def _():
            tail[...] = jnp.zeros_like(tail)
            hcar[...] = jnp.zeros_like(hcar)

        cx = cx_ref[...]
        prev8 = tail[...]
        lx = cb_ref[...] + cw_ref[3:4, :] * cx
        for j in range(1, 4):
            lx += cw_ref[3 - j:4 - j, :] * _shift_down(cx, prev8, j)
        tail[...] = cx[TT - 8:TT]
        lx_ref[...] = lx
        r, ig, sp, a, mult = _lru_gates(lx, wa_ref, wx_ref, ba_ref[...], bx_ref[...], lam_ref[...])
        r_ref[...] = r
        ig_ref[...] = ig
        a_ref[...] = a
        mult_ref[...] = mult
        h = _scan_tile(a, mult * (ig * lx), hcar[...], sa, su, reverse=False)
        hl_ref[...] = h
        hcar[...] = hl_ref[TT - 1:TT, :]
        ge, _ = _gelu(gate_ref[...])
        p = h * ge
        y_ref[...] = (p * _rms(p) * g_ref[...]).astype(BF16)

    vec = pl.BlockSpec((1, LRU_W), lambda i: (0, 0))
    bd = pl.BlockSpec((4, 256, 256), lambda i: (0, 0, 0))
    tile = pl.BlockSpec((TT, LRU_W), lambda i: (i, 0))
    f32 = jax.ShapeDtypeStruct((T, LRU_W), F32)
    return pl.pallas_call(
        body, name="lru_fwd", grid=(T // TT,),
        in_specs=[tile, pl.BlockSpec((TT, LRU_W), lambda i: (i, 1)),
                  pl.BlockSpec((4, LRU_W), lambda i: (0, 0)), vec, bd, bd, vec, vec, vec, vec],
        out_specs=[tile] * 7,
        out_shape=[f32, f32, jax.ShapeDtypeStruct((T, LRU_W), BF16), f32, f32, f32, f32],
        scratch_shapes=[pltpu.VMEM((8, LRU_W), F32), pltpu.VMEM((1, LRU_W), F32)]
                       + [pltpu.VMEM((LRU_W // LANE, TT, LANE), F32)] * 2,
        compiler_params=_params(1),
    )(proj, proj, conv_w, conv_b, wa_bd, wx_bd, ba, bx, lam, g_lru)


def _lru_bwd(dy, proj, lx, hl, gates, conv_w, wa_bd, wx_bd, lam, g_lru):
    T = proj.shape[0]
    nt = T // TT

    def body(dy_ref, cx_ref, gate_ref, lx_ref, hl_ref, halo_ref, r_ref, ig_ref, a_ref, mult_ref, cw_ref, wa_ref,
             wx_ref, lam_ref, g_ref, dp_ref, dpa_ref, dpx_ref, dcw_ref, dcb_ref, dba_ref, dbx_ref, dlam_ref, dg_ref,
             gcar, acar, head, sa, su):
        i = pl.program_id(0)

        @pl.when(i == 0)
        def _():
            gcar[...] = jnp.zeros_like(gcar)
            acar[...] = jnp.zeros_like(acar)
            head[...] = jnp.zeros_like(head)
            for ref in (dcw_ref, dcb_ref, dba_ref, dbx_ref, dlam_ref, dg_ref):
                ref[...] = jnp.zeros_like(ref)

        lx = lx_ref[...]
        h = hl_ref[...]
        gate = gate_ref[...]
        cx = cx_ref[...]
        lam = lam_ref[...]
        r, ig, a, mult = r_ref[...], ig_ref[...], a_ref[...], mult_ref[...]
        sp = _softplus(-lam)
        ge, th = _gelu(gate)
        p = h * ge
        rn = _rms(p)
        dyv = dy_ref[...]
        dg_ref[...] += jnp.sum(dyv * p * rn, axis=0, keepdims=True)
        dp = _rms_bwd(dyv * g_ref[...], p, rn)
        dp_ref[:, LRU_W:2 * LRU_W] = (dp * h * _gelu_grad(gate, th)).astype(BF16)
        dh = dp * ge
        row = lax.broadcasted_iota(jnp.int32, a.shape, 0)
        b = jnp.where(row == TT - 1, acar[...], pltpu.roll(a, TT - 1, 0))
        g = _scan_tile(b, dh, gcar[...], sa, su, reverse=True)
        gcar[...] = _row(g[0:8], 0)
        acar[...] = _row(a[0:8], 0)
        h_last_prev = halo_ref[7:8, :] * (i < nt - 1).astype(F32)
        hprev = jnp.where(row == 0, h_last_prev, pltpu.roll(h, 1, 0))
        da = g * hprev
        dm2 = (g * (ig * lx)) * 0.5 / mult
        dlog_a = da * a - 2.0 * a * a * dm2
        dlam_ref[...] += jnp.sum(-8.0 * r * dlog_a, axis=0, keepdims=True) * (-_sigmoid(-lam))
        dpa = (-8.0 * sp * dlog_a) * r * (1.0 - r)
        dpx = (g * mult * lx) * ig * (1.0 - ig)
        dba_ref[...] += jnp.sum(dpa, axis=0, keepdims=True)
        dbx_ref[...] += jnp.sum(dpx, axis=0, keepdims=True)
        dpab = dpa.astype(BF16)
        dpxb = dpx.astype(BF16)
        dpa_ref[...] = dpab
        dpx_ref[...] = dpxb
        dlx = g * mult * ig + jnp.concatenate(
            [_dot(dpab[:, 256 * k:256 * (k + 1)], wa_ref[k], NT) + _dot(dpxb[:, 256 * k:256 * (k + 1)], wx_ref[k], NT)
             for k in range(4)], axis=1)
        nxt = head[...]
        dcb_ref[...] += jnp.sum(dlx, axis=0, keepdims=True)
        dcx = cw_ref[3:4, :] * dlx
        dcw_ref[3:4, :] += jnp.sum(cx * dlx, axis=0, keepdims=True)
        for j in range(1, 4):
            sh = _shift_up(dlx, nxt, j)
            dcx += cw_ref[3 - j:4 - j, :] * sh
            dcw_ref[3 - j:4 - j, :] += jnp.sum(cx * sh, axis=0, keepdims=True)
        head[...] = dlx[0:8]
        dp_ref[:, 0:LRU_W] = dcx.astype(BF16)

    rev = lambda i: (nt - 1 - i, 0)
    vec = pl.BlockSpec((1, LRU_W), lambda i: (0, 0))
    bd = pl.BlockSpec((4, 256, 256), lambda i: (0, 0, 0))
    tile = pl.BlockSpec((TT, LRU_W), rev)
    halo = pl.BlockSpec((8, LRU_W), lambda i: (jnp.maximum((nt - 1 - i) * (TT // 8) - 1, 0), 0))
    cw = pl.BlockSpec((4, LRU_W), lambda i: (0, 0))
    return pl.pallas_call(
        body, name="lru_bwd", grid=(nt,),
        in_specs=[tile, tile, pl.BlockSpec((TT, LRU_W), lambda i: (nt - 1 - i, 1)), tile, tile, halo,
                  tile, tile, tile, tile, cw, bd, bd, vec, vec],
        out_specs=[pl.BlockSpec((TT, 2 * LRU_W), rev), tile, tile, cw, vec, vec, vec, vec, vec],
        out_shape=[jax.ShapeDtypeStruct((T, 2 * LRU_W), BF16), jax.ShapeDtypeStruct((T, LRU_W), BF16),
                   jax.ShapeDtypeStruct((T, LRU_W), BF16), jax.ShapeDtypeStruct((4, LRU_W), F32)]
                  + [jax.ShapeDtypeStruct((1, LRU_W), F32)] * 5,
        scratch_shapes=[pltpu.VMEM((1, LRU_W), F32), pltpu.VMEM((1, LRU_W), F32), pltpu.VMEM((8, LRU_W), F32)]
                       + [pltpu.VMEM((LRU_W // LANE, TT, LANE), F32)] * 2,
        compiler_params=_params(1),
    )(dy, proj, proj, lx, hl, hl, *gates, conv_w, wa_bd, wx_bd, lam, g_lru)


def _ssd_chunk_terms(xc, dtp, bias, alog, expand):
    sg = _sigmoid(xc)
    xbc = xc * sg
    pre = dtp + bias
    dt = _softplus(pre)
    A = -jnp.exp(alog)
    ri = lax.broadcasted_iota(jnp.int32, (CHUNK, CHUNK), 0)
    ci = lax.broadcasted_iota(jnp.int32, (CHUNK, CHUNK), 1)
    tril = (ri >= ci).astype(BF16)
    cs = _dot_sel(dt * A, tril, terms=3, sel_first=True)
    cs_last = _row(cs, CHUNK - 1)
    ecs = jnp.exp(cs)
    dec = jnp.exp(cs_last - cs)
    return dict(sg=sg, xbc=xbc, pre=pre, dt=dt, A=A, cs=cs, csT=cs.T, ecs=ecs, dec=dec, ri=ri, ci=ci,
                dt_e=_dot_sel(dt, expand), ecs_e=_dot_sel(ecs, expand), dec_e=_dot_sel(dec, expand))


def _head_lambda(t, h):
    col = jnp.sum(jnp.where(t["ci"] == h, t["cs"], 0.0), axis=1, keepdims=True)
    rowv = jnp.sum(jnp.where(t["ri"] == h, t["csT"], 0.0), axis=0, keepdims=True)
    return jnp.exp(jnp.where(t["ri"] >= t["ci"], col - rowv, -1e30))


def _ssd_fwd(proj, dtp, conv_w, conv_b, dt_bias, a_log, d_e, g_ssd, expand):
    T = proj.shape[0]
    nc = T // CHUNK

    def body(z_ref, xp_ref, dtp_ref, cw_ref, cb_ref, bias_ref, alog_ref, de_ref, g_ref, ex_ref,
             xc_ref, y_ref, yn_ref, sprev_ref, tail, S):
        @pl.when(pl.program_id(0) == 0)
        def _():
            tail[...] = jnp.zeros_like(tail)
            S[...] = jnp.zeros_like(S)

        xp = xp_ref[...]
        prev8 = tail[...]
        xc = cb_ref[...] + cw_ref[3:4, :] * xp
        for j in range(1, 4):
            xc += cw_ref[3 - j:4 - j, :] * _shift_down(xp, prev8, j)
        tail[...] = xp[CHUNK - 8:CHUNK]
        xc_ref[...] = xc
        t = _ssd_chunk_terms(xc, dtp_ref[...], bias_ref[...], alog_ref[...], ex_ref[...])
        xbc = t["xbc"]
        sx = xbc[:, 0:SSD_INNER]
        Bb = xbc[:, SSD_INNER:SSD_INNER + 256].astype(BF16)
        Cb = xbc[:, SSD_INNER + 256:SSD_CONV_CH].astype(BF16)
        X = t["dt_e"] * sx
        lane = lax.broadcasted_iota(jnp.int32, (CHUNK, LANE), 1)
        G = [_dot(Cb[:, 128 * g:128 * (g + 1)], Bb[:, 128 * g:128 * (g + 1)], NT) for g in range(SSD_GROUPS)]
        for k in range(SSD_HEADS // 2):
            Xp = X[:, 128 * k:128 * (k + 1)]
            acc = jnp.zeros((CHUNK, LANE), F32)
            for half in range(2):
                M = (G[k // 4] * _head_lambda(t, 2 * k + half)).astype(BF16)
                Xh = jnp.where((lane >= 64) if half else (lane < 64), Xp, 0.0).astype(BF16)
                acc += _dot(M, Xh)
            y_ref[:, 128 * k:128 * (k + 1)] = acc
        sprev_ref[0] = S[...]
        eL_e = _row(t["ecs_e"], CHUNK - 1)
        Xd = (X * t["dec_e"]).astype(BF16)
        for g in range(SSD_GROUPS):
            sl = slice(512 * g, 512 * (g + 1))
            Sg = S[:, sl]
            y_ref[:, sl] += t["ecs_e"][:, sl] * _dot(Cb[:, 128 * g:128 * (g + 1)], Sg.astype(BF16))
            S[:, sl] = eL_e[:, sl] * Sg + _dot(Bb[:, 128 * g:128 * (g + 1)], Xd[:, sl], TN)
        y = y_ref[...] + de_ref[...] * sx
        y_ref[...] = y
        z = z_ref[...]
        q = y * (z * _sigmoid(z))
        yn_ref[...] = (q * _rms(q) * g_ref[...]).astype(BF16)

    c0 = lambda i: (0, 0)
    return pl.pallas_call(
        body, name="ssd_fwd", grid=(nc,),
        in_specs=[pl.BlockSpec((CHUNK, SSD_INNER), lambda i: (i, 2)),
                  pl.BlockSpec((CHUNK, SSD_CONV_CH), lambda i: (i, 2)),
                  pl.BlockSpec((CHUNK, LANE), lambda i: (i, 0)),
                  pl.BlockSpec((4, SSD_CONV_CH), c0), pl.BlockSpec((1, SSD_CONV_CH), c0),
                  pl.BlockSpec((1, LANE), c0), pl.BlockSpec((1, LANE), c0),
                  pl.BlockSpec((1, SSD_INNER), c0), pl.BlockSpec((1, SSD_INNER), c0),
                  pl.BlockSpec((LANE, SSD_INNER), c0)],
        out_specs=[pl.BlockSpec((CHUNK, SSD_CONV_CH), lambda i: (i, 0)),
                   pl.BlockSpec((CHUNK, SSD_INNER), lambda i: (i, 0)),
                   pl.BlockSpec((CHUNK, SSD_INNER), lambda i: (i, 0)),
                   pl.BlockSpec((1, SSD_STATE, SSD_INNER), lambda i: (i, 0, 0))],
        out_shape=[jax.ShapeDtypeStruct((T, SSD_CONV_CH), F32), jax.ShapeDtypeStruct((T, SSD_INNER), F32),
                   jax.ShapeDtypeStruct((T, SSD_INNER), BF16),
                   jax.ShapeDtypeStruct((nc, SSD_STATE, SSD_INNER), F32)],
        scratch_shapes=[pltpu.VMEM((8, SSD_CONV_CH), F32), pltpu.VMEM((SSD_STATE, SSD_INNER), F32)],
        compiler_params=_params(1),
    )(proj, proj, dtp, conv_w, conv_b, dt_bias, a_log, d_e, g_ssd, expand)


def _ssd_bwd(dyn, proj, dtp, xc, y, sprev, conv_w, dt_bias, a_log, d_e, g_ssd, expand):
    T = proj.shape[0]
    nc = T // CHUNK

    def body(dyn_ref, z_ref, xp_ref, dtp_ref, xc_ref, y_ref, sprev_ref, cw_ref, bias_ref, alog_ref, de_ref,
             g_ref, ex_ref, dp_ref, ddtp_ref, dcw_ref, dcb_ref, dbias_ref, dA_ref, dD_ref, dg_ref,
             dS, head, dX_s, dxbc_s):
        @pl.when(pl.program_id(0) == 0)
        def _():
            dS[...] = jnp.zeros_like(dS)
            head[...] = jnp.zeros_like(head)
            for ref in (dcw_ref, dcb_ref, dbias_ref, dA_ref, dD_ref, dg_ref):
                ref[...] = jnp.zeros_like(ref)

        ex = ex_ref[...]
        xc = xc_ref[...]
        t = _ssd_chunk_terms(xc, dtp_ref[...], bias_ref[...], alog_ref[...], ex)
        ri, ci = t["ri"], t["ci"]
        xbc = t["xbc"]
        sx = xbc[:, 0:SSD_INNER]
        Bb = xbc[:, SSD_INNER:SSD_INNER + 256].astype(BF16)
        Cb = xbc[:, SSD_INNER + 256:SSD_CONV_CH].astype(BF16)
        X = t["dt_e"] * sx
        z = z_ref[...]
        sz = _sigmoid(z)
        siluz = z * sz
        yv = y_ref[...]
        q = yv * siluz
        rn = _rms(q)
        dynv = dyn_ref[...]
        dg_ref[...] += jnp.sum(dynv * q * rn, axis=0, keepdims=True)
        dq = _rms_bwd(dynv * g_ref[...], q, rn)
        dp_ref[:, 0:SSD_INNER] = (dq * yv * (sz * (1.0 + z * (1.0 - sz)))).astype(BF16)
        dY = dq * siluz
        dD_ref[...] += jnp.sum(dY * sx, axis=0, keepdims=True)
        dYb = dY.astype(BF16)
        lane = lax.broadcasted_iota(jnp.int32, (CHUNK, LANE), 1)
        dcs = jnp.zeros((CHUNK, CHUNK), F32)
        dcsT = jnp.zeros((CHUNK, CHUNK), F32)
        Xb = X.astype(BF16)
        for g in range(SSD_GROUPS):
            Bg = Bb[:, 128 * g:128 * (g + 1)]
            Cg = Cb[:, 128 * g:128 * (g + 1)]
            G = _dot(Cg, Bg, NT)
            dGsum = jnp.zeros((CHUNK, CHUNK), F32)
            for k in range(4 * g, 4 * g + 4):
                Xp = Xb[:, 128 * k:128 * (k + 1)]
                dYp = dY[:, 128 * k:128 * (k + 1)]
                dXp = jnp.zeros((CHUNK, LANE), F32)
                for half in range(2):
                    h = 2 * k + half
                    lam = _head_lambda(t, h)
                    M = G * lam
                    dYh = jnp.where((lane >= 64) if half else (lane < 64), dYp, 0.0).astype(BF16)
                    dM = _dot(dYh, Xp, NT)
                    W = dM * M
                    dcs += jnp.where(ci == h, jnp.sum(W, axis=1, keepdims=True), 0.0)
                    dcsT += jnp.where(ri == h, jnp.sum(W, axis=0, keepdims=True), 0.0)
                    dGsum += dM * lam
                    dXp += _dot(M.astype(BF16), dYh, TN)
                dX_s[:, 128 * k:128 * (k + 1)] = dXp
            dGb = dGsum.astype(BF16)
            dxbc_s[:, SSD_INNER + 256 + 128 * g:SSD_INNER + 256 + 128 * (g + 1)] = _dot(dGb, Bg)
            dxbc_s[:, SSD_INNER + 128 * g:SSD_INNER + 128 * (g + 1)] = _dot(dGb, Cg, TN)
        dcs = dcs - dcsT.T
        Sp = sprev_ref[0]
        dSv = dS[...]
        ecs_e, dec_e = t["ecs_e"], t["dec_e"]
        eL_e = _row(ecs_e, CHUNK - 1)
        dYe = dY * ecs_e
        dYeb = dYe.astype(BF16)
        Xd = X * dec_e
        Xdb = Xd.astype(BF16)
        for g in range(SSD_GROUPS):
            sl = slice(512 * g, 512 * (g + 1))
            Bg = Bb[:, 128 * g:128 * (g + 1)]
            Cg = Cb[:, 128 * g:128 * (g + 1)]
            Spb = Sp[:, sl].astype(BF16)
            dSb = dSv[:, sl].astype(BF16)
            CS = _dot(Cg, Spb)
            BS = _dot(Bg, dSb)
            dxbc_s[:, SSD_INNER + 256 + 128 * g:SSD_INNER + 256 + 128 * (g + 1)] += _dot(dYeb[:, sl], Spb, NT)
            dxbc_s[:, SSD_INNER + 128 * g:SSD_INNER + 128 * (g + 1)] += _dot(Xdb[:, sl], dSb, NT)
            dS[:, sl] = eL_e[:, sl] * dSv[:, sl] + _dot(Cg, dYeb[:, sl], TN)
            dX_s[:, sl] += dec_e[:, sl] * BS
            dcs += _dot_sel(dYe[:, sl] * CS, ex[:, sl], NT, terms=1)
            tdec = _dot_sel(X[:, sl] * BS, ex[:, sl], NT, terms=1) * t["dec"]
            dcs -= tdec
            last = jnp.sum(tdec, axis=0, keepdims=True)
            last += jnp.sum(_dot_sel(Sp[:, sl] * dSv[:, sl], ex[:, sl], NT, terms=1), axis=0, keepdims=True) \
                * _row(t["ecs"], CHUNK - 1)
            dcs += jnp.where(ri == CHUNK - 1, last, 0.0)
        triu = (ci >= ri).astype(BF16)
        da = _dot_sel(dcs, triu, terms=3, sel_first=True)
        dX = dX_s[...]
        ddt = da * t["A"] + _dot_sel(dX * sx, ex, NT, terms=1)
        dA_ref[...] += jnp.sum(da * t["dt"], axis=0, keepdims=True)
        ddtp = ddt * _sigmoid(t["pre"])
        dbias_ref[...] += jnp.sum(ddtp, axis=0, keepdims=True)
        ddtp_ref[...] = ddtp.astype(BF16)
        dxbc_s[:, 0:SSD_INNER] = dX * t["dt_e"] + de_ref[...] * dY
        sg = t["sg"]
        dxc = dxbc_s[...] * (sg * (1.0 + xc * (1.0 - sg)))
        xp = xp_ref[...]
        nxt = head[...]
        dcb_ref[...] += jnp.sum(dxc, axis=0, keepdims=True)
        dpre = cw_ref[3:4, :] * dxc
        dcw_ref[3:4, :] += jnp.sum(xp * dxc, axis=0, keepdims=True)
        for j in range(1, 4):
            sh = _shift_up(dxc, nxt, j)
            dpre += cw_ref[3 - j:4 - j, :] * sh
            dcw_ref[3 - j:4 - j, :] += jnp.sum(xp * sh, axis=0, keepdims=True)
        head[...] = dxc[0:8]
        dp_ref[:, SSD_INNER:SSD_INNER + SSD_CONV_CH] = dpre.astype(BF16)

    c0 = lambda i: (0, 0)
    rev = lambda i: (nc - 1 - i, 0)
    return pl.pallas_call(
        body, name="ssd_bwd", grid=(nc,),
        in_specs=[pl.BlockSpec((CHUNK, SSD_INNER), rev),
                  pl.BlockSpec((CHUNK, SSD_INNER), lambda i: (nc - 1 - i, 2)),
                  pl.BlockSpec((CHUNK, SSD_CONV_CH), lambda i: (nc - 1 - i, 2)),
                  pl.BlockSpec((CHUNK, LANE), rev),
                  pl.BlockSpec((CHUNK, SSD_CONV_CH), rev),
                  pl.BlockSpec((CHUNK, SSD_INNER), rev),
                  pl.BlockSpec((1, SSD_STATE, SSD_INNER), lambda i: (nc - 1 - i, 0, 0)),
                  pl.BlockSpec((4, SSD_CONV_CH), c0), pl.BlockSpec((1, LANE), c0), pl.BlockSpec((1, LANE), c0),
                  pl.BlockSpec((1, SSD_INNER), c0), pl.BlockSpec((1, SSD_INNER), c0),
                  pl.BlockSpec((LANE, SSD_INNER), c0)],
        out_specs=[pl.BlockSpec((CHUNK, 2560), rev), pl.BlockSpec((CHUNK, LANE), rev),
                   pl.BlockSpec((4, SSD_CONV_CH), c0), pl.BlockSpec((1, SSD_CONV_CH), c0),
                   pl.BlockSpec((1, LANE), c0), pl.BlockSpec((1, LANE), c0),
                   pl.BlockSpec((1, SSD_INNER), c0), pl.BlockSpec((1, SSD_INNER), c0)],
        out_shape=[jax.ShapeDtypeStruct((T, 2560), BF16), jax.ShapeDtypeStruct((T, LANE), BF16),
                   jax.ShapeDtypeStruct((4, SSD_CONV_CH), F32), jax.ShapeDtypeStruct((1, SSD_CONV_CH), F32),
                   jax.ShapeDtypeStruct((1, LANE), F32), jax.ShapeDtypeStruct((1, LANE), F32),
                   jax.ShapeDtypeStruct((1, SSD_INNER), F32), jax.ShapeDtypeStruct((1, SSD_INNER), F32)],
        scratch_shapes=[pltpu.VMEM((SSD_STATE, SSD_INNER), F32), pltpu.VMEM((8, SSD_CONV_CH), F32),
                        pltpu.VMEM((CHUNK, SSD_INNER), F32), pltpu.VMEM((CHUNK, SSD_CONV_CH), F32)],
        compiler_params=_params(1),
    )(dyn, proj, proj, dtp, xc, y, sprev, conv_w, dt_bias, a_log, d_e, g_ssd, expand)


def _outproj_fwd(x, y_lru, y_ssd, w_out, g_pm, g_pf):
    T = x.shape[0]
    TT = TP

    def body(x_ref, yl_ref, ys_ref, wo_ref, gpm_ref, gpf_ref, mix_ref, x1_ref, h2_ref):
        mix = _dot(yl_ref[...], wo_ref[0:LRU_W, :]) + _dot(ys_ref[...], wo_ref[LRU_W:2 * LRU_W, :])
        mix_ref[...] = mix
        x1 = x_ref[...] + mix * _rms(mix) * gpm_ref[...]
        x1_ref[...] = x1
        h2_ref[...] = (x1 * _rms(x1) * gpf_ref[...]).astype(BF16)

    tile = pl.BlockSpec((TT, D_MODEL), lambda i: (i, 0))
    vec = pl.BlockSpec((1, D_MODEL), lambda i: (0, 0))
    return pl.pallas_call(
        body, name="outproj_fwd", grid=(T // TT,),
        in_specs=[tile, tile, tile, pl.BlockSpec((2 * LRU_W, D_MODEL), lambda i: (0, 0)), vec, vec],
        out_specs=[tile, tile, tile],
        out_shape=[jax.ShapeDtypeStruct((T, D_MODEL), F32), jax.ShapeDtypeStruct((T, D_MODEL), F32),
                   jax.ShapeDtypeStruct((T, D_MODEL), BF16)],
        compiler_params=_params(1),
    )(x, y_lru, y_ssd, w_out, g_pm, g_pf)


def _ffn_fwd_bwd(x1, h2, target, w_gate, w_up, w_down, g_pf, g_ff):
    T = x1.shape[0]

    def body(x1_ref, h2_ref, tg_ref, wg_hbm, wu_hbm, wd_hbm, gpf_ref, gff_ref,
             dx1_ref, act_ref, df_ref, dgt_ref, dup_ref, dgpf_ref, dgff_ref, loss_ref,
             wg, wu, wd, gt_s, up_s, sem):
        @pl.when(pl.program_id(0) == 0)
        def _():
            cps = [pltpu.make_async_copy(s, d, sem.at[n]) for n, (s, d) in
                   enumerate(((wg_hbm, wg), (wu_hbm, wu), (wd_hbm, wd)))]
            for c in cps:
                c.start()
            for c in cps:
                c.wait()
            for ref in (dgpf_ref, dgff_ref, loss_ref):
                ref[...] = jnp.zeros_like(ref)

        h2 = h2_ref[...]
        f = jnp.zeros((TT, D_MODEL), F32)
        for c in range(D_FF // FF_CHUNK):
            sl = slice(FF_CHUNK * c, FF_CHUNK * (c + 1))
            gt = _dot(h2, wg[:, sl])
            up = _dot(h2, wu[:, sl])
            gt_s[:, sl] = gt
            up_s[:, sl] = up
            act = (gt * _sigmoid(gt) * up).astype(BF16)
            act_ref[:, sl] = act
            f += _dot(act, wd[sl, :])
        x1 = x1_ref[...]
        rnf = _rms(f)
        e = x1 + f * rnf * gff_ref[...] - tg_ref[...]
        part = 0.5 * jnp.sum(jnp.sum(e * e, axis=1, keepdims=True), axis=0, keepdims=True) * (1.0 / D_MODEL)
        lane = lax.broadcasted_iota(jnp.int32, (1, LANE), 1)
        loss_ref[...] += jnp.where(lane == 0, part, 0.0)
        dx2 = e * (1.0 / D_MODEL)
        dgff_ref[...] += jnp.sum(dx2 * f * rnf, axis=0, keepdims=True)
        df = _rms_bwd(dx2 * gff_ref[...], f, rnf).astype(BF16)
        df_ref[...] = df
        dh2 = jnp.zeros((TT, D_MODEL), F32)
        for c in range(D_FF // FF_CHUNK):
            sl = slice(FF_CHUNK * c, FF_CHUNK * (c + 1))
            dact = _dot(df, wd[sl, :], NT)
            gt = gt_s[:, sl]
            up = up_s[:, sl]
            sg = _sigmoid(gt)
            dgt = (dact * up * (sg * (1.0 + gt * (1.0 - sg)))).astype(BF16)
            dup = (dact * gt * sg).astype(BF16)
            dgt_ref[:, sl] = dgt
            dup_ref[:, sl] = dup
            dh2 += _dot(dgt, wg[:, sl], NT) + _dot(dup, wu[:, sl], NT)
        rn2 = _rms(x1)
        dgpf_ref[...] += jnp.sum(dh2 * x1 * rn2, axis=0, keepdims=True)
        dx1_ref[...] = dx2 + _rms_bwd(dh2 * gpf_ref[...], x1, rn2)

    tile = pl.BlockSpec((TT, D_MODEL), lambda i: (i, 0))
    wide = pl.BlockSpec((TT, D_FF), lambda i: (i, 0))
    vec = pl.BlockSpec((1, D_MODEL), lambda i: (0, 0))
    hbm = pl.BlockSpec(memory_space=pl.ANY)
    return pl.pallas_call(
        body, name="ffn_fwd_bwd", grid=(T // TT,),
        in_specs=[tile, tile, tile, hbm, hbm, hbm, vec, vec],
        out_specs=[tile, wide, tile, wide, wide, vec, vec, pl.BlockSpec((1, LANE), lambda i: (0, 0))],
        out_shape=[jax.ShapeDtypeStruct((T, D_MODEL), F32), jax.ShapeDtypeStruct((T, D_FF), BF16),
                   jax.ShapeDtypeStruct((T, D_MODEL), BF16), jax.ShapeDtypeStruct((T, D_FF), BF16),
                   jax.ShapeDtypeStruct((T, D_FF), BF16), jax.ShapeDtypeStruct((1, D_MODEL), F32),
                   jax.ShapeDtypeStruct((1, D_MODEL), F32), jax.ShapeDtypeStruct((1, LANE), F32)],
        scratch_shapes=[pltpu.VMEM((D_MODEL, D_FF), BF16), pltpu.VMEM((D_MODEL, D_FF), BF16),
                        pltpu.VMEM((D_FF, D_MODEL), BF16), pltpu.VMEM((TT, D_FF), F32),
                        pltpu.VMEM((TT, D_FF), F32), pltpu.SemaphoreType.DMA((3,))],
        compiler_params=_params(1),
    )(x1, h2, target, w_gate, w_up, w_down, g_pf, g_ff)


def _outproj_bwd(dx1, mix, w_out, g_pm):
    T = dx1.shape[0]
    TT = TP

    def body(dx1_ref, mix_ref, wo_ref, gpm_ref, dyl_ref, dys_ref, dmix_ref, dg_ref):
        @pl.when(pl.program_id(0) == 0)
        def _():
            dg_ref[...] = jnp.zeros_like(dg_ref)

        mix = mix_ref[...]
        rn = _rms(mix)
        dx1v = dx1_ref[...]
        dg_ref[...] += jnp.sum(dx1v * mix * rn, axis=0, keepdims=True)
        dmix = _rms_bwd(dx1v * gpm_ref[...], mix, rn).astype(BF16)
        dmix_ref[...] = dmix
        dyl_ref[...] = _dot(dmix, wo_ref[0:LRU_W, :], NT)
        dys_ref[...] = _dot(dmix, wo_ref[LRU_W:2 * LRU_W, :], NT)

    tile = pl.BlockSpec((TT, D_MODEL), lambda i: (i, 0))
    vec = pl.BlockSpec((1, D_MODEL), lambda i: (0, 0))
    return pl.pallas_call(
        body, name="outproj_bwd", grid=(T // TT,),
        in_specs=[tile, tile, pl.BlockSpec((2 * LRU_W, D_MODEL), lambda i: (0, 0)), vec],
        out_specs=[tile, tile, tile, vec],
        out_shape=[jax.ShapeDtypeStruct((T, D_MODEL), F32), jax.ShapeDtypeStruct((T, D_MODEL), F32),
                   jax.ShapeDtypeStruct((T, D_MODEL), BF16), jax.ShapeDtypeStruct((1, D_MODEL), F32)],
        compiler_params=_params(1),
    )(dx1, mix, w_out, g_pm)


def _tn_matmul(a, bs, name, tk=512):
    T, M = a.shape
    nk = T // tk
    nb = len(bs)

    def body(*refs):
        a_ref, b_refs, o_refs, accs = refs[0], refs[1:1 + nb], refs[1 + nb:1 + 2 * nb], refs[1 + 2 * nb:]
        k = pl.program_id(0)

        @pl.when(k == 0)
        def _():
            for acc in accs:
                acc[...] = jnp.zeros_like(acc)

        av = a_ref[...].astype(BF16)
        for b_ref, acc in zip(b_refs, accs):
            acc[...] += _dot(av, b_ref[...], TN)

        @pl.when(k == nk - 1)
        def _():
            for o_ref, acc in zip(o_refs, accs):
                o_ref[...] = acc[...].astype(BF16)

    return pl.pallas_call(
        body, name=name, grid=(nk,),
        in_specs=[pl.BlockSpec((tk, M), lambda k: (k, 0))]
                 + [pl.BlockSpec((tk, b.shape[1]), lambda k: (k, 0)) for b in bs],
        out_specs=[pl.BlockSpec((M, b.shape[1]), lambda k: (0, 0)) for b in bs],
        out_shape=[jax.ShapeDtypeStruct((M, b.shape[1]), BF16) for b in bs],
        scratch_shapes=[pltpu.VMEM((M, b.shape[1]), F32) for b in bs],
        compiler_params=_params(1),
    )(a, *bs)


def _tn_matmul_slabs(a, bs, name, slab, tk=512):
    T, M = a.shape
    nk = T // tk
    nb = len(bs)
    offs = [sum(b.shape[1] for b in bs[:i]) for i in range(nb + 1)]

    def body(*refs):
        a_ref, b_refs, o_ref, acc = refs[0], refs[1:1 + nb], refs[1 + nb], refs[2 + nb]
        k = pl.program_id(0)

        @pl.when(k == 0)
        def _():
            acc[...] = jnp.zeros_like(acc)

        av = a_ref[...].astype(BF16)
        for i, b_ref in enumerate(b_refs):
            acc[:, offs[i]:offs[i + 1]] += _dot(av, b_ref[...], TN)

        @pl.when(k == nk - 1)
        def _():
            for j in range(N_DEV):
                o_ref[j] = acc[:, slab * j:slab * (j + 1)].astype(BF16)

    return pl.pallas_call(
        body, name=name, grid=(nk,),
        in_specs=[pl.BlockSpec((tk, M), lambda k: (k, 0))]
                 + [pl.BlockSpec((tk, b.shape[1]), lambda k: (k, 0)) for b in bs],
        out_specs=pl.BlockSpec((N_DEV, M, slab), lambda k: (0, 0, 0)),
        out_shape=jax.ShapeDtypeStruct((N_DEV, M, slab), BF16),
        scratch_shapes=[pltpu.VMEM((M, offs[-1]), F32)],
        compiler_params=_params(1),
    )(a, *bs)


def _tn_blockdiag(a, b1, b2, name, tk=1024):
    T = a.shape[0]
    tk = min(tk, T)

    def body(a_ref, b1_ref, b2_ref, o1_ref, o2_ref):
        @pl.when(pl.program_id(0) == 0)
        def _():
            o1_ref[...] = jnp.zeros_like(o1_ref)
            o2_ref[...] = jnp.zeros_like(o2_ref)

        for j in range(4):
            sl = slice(256 * j, 256 * (j + 1))
            av = a_ref[:, sl].astype(BF16)
            o1_ref[j] += _dot(av, b1_ref[:, sl], TN)
            o2_ref[j] += _dot(av, b2_ref[:, sl], TN)

    blk = pl.BlockSpec((tk, LRU_W), lambda k: (k, 0))
    out = pl.BlockSpec((4, 256, 256), lambda k: (0, 0, 0))
    return pl.pallas_call(
        body, name=name, grid=(T // tk,),
        in_specs=[blk, blk, blk], out_specs=[out, out],
        out_shape=[jax.ShapeDtypeStruct((4, 256, 256), F32)] * 2,
        compiler_params=_params(1),
    )(a, b1, b2)


def _adamw_update(g, w, m, v):
    nm = ADAM_B1 * m + (1.0 - ADAM_B1) * g
    nv = ADAM_B2 * v + (1.0 - ADAM_B2) * (g * g)
    m_hat = nm / (1.0 - ADAM_B1 ** ADAM_STEP)
    v_hat = nv / (1.0 - ADAM_B2 ** ADAM_STEP)
    return -ADAM_LR * (m_hat / (jnp.sqrt(v_hat) + ADAM_EPS) + ADAM_WD * w), nm, nv


def _adamw_small(parts, params):
    n = len(params)

    def body(*refs):
        p_ref, wmv = refs[0], refs[1:1 + 3 * n]
        gs_ref, outs = refs[1 + 3 * n], refs[2 + 3 * n:]
        g = p_ref[0]
        for k in range(1, N_DEV):
            g = g + p_ref[k]
        gs_ref[...] = g
        for i, (row, w, _, _) in enumerate(params):
            rows, width = w.shape
            if width <= 1024:
                gp = gs_ref[row:row + rows, 0:width]
            else:
                gp = jnp.concatenate([gs_ref[row:row + 1, :], gs_ref[row + 1:row + 2, 0:width - 1024]], axis=1)
            d, nm, nv = _adamw_update(gp, wmv[3 * i][...], wmv[3 * i + 1][...], wmv[3 * i + 2][...])
            for ref, val in zip(outs[4 * i:4 * i + 4], (gp, d, nm, nv)):
                ref[...] = val

    full = lambda s: pl.BlockSpec(s, lambda: (0,) * len(s))
    shapes = [w.shape for _, w, _, _ in params]
    return pl.pallas_call(
        body, name="adamw_small",
        in_specs=[full(parts.shape)] + [full(s) for s in shapes for _ in range(3)],
        out_specs=[full(parts.shape[1:])] + [full(s) for s in shapes for _ in range(4)],
        out_shape=[jax.ShapeDtypeStruct(parts.shape[1:], F32)]
                  + [jax.ShapeDtypeStruct(s, F32) for s in shapes for _ in range(4)],
        compiler_params=pltpu.CompilerParams(vmem_limit_bytes=VMEM_LIMIT),
    )(parts, *[a for _, w, m, v in params for a in (w, m, v)])


def _adamw(parts, w, m, v, name, tr):
    P, R, C = parts.shape

    def body(p_ref, w_ref, m_ref, v_ref, g_ref, d_ref, nm_ref, nv_ref):
        g = p_ref[0].astype(F32)
        for k in range(1, P):
            g = g + p_ref[k].astype(F32)
        g_ref[...] = g
        d_ref[...], nm_ref[...], nv_ref[...] = _adamw_update(g, w_ref[...], m_ref[...], v_ref[...])

    blk = pl.BlockSpec((tr, C), lambda i: (i, 0))
    return pl.pallas_call(
        body, name=name, grid=(R // tr,),
        in_specs=[pl.BlockSpec((P, tr, C), lambda i: (0, i, 0)), blk, blk, blk],
        out_specs=[blk, blk, blk, blk],
        out_shape=[jax.ShapeDtypeStruct((R, C), F32)] * 4,
        compiler_params=_params(1),
    )(parts, w, m, v)


def _peer(k):
    x, y, c = lax.axis_index("x"), lax.axis_index("y"), lax.axis_index("c")
    px = x ^ ((k >> 2) & 1)
    py = y ^ ((k >> 1) & 1)
    pc = c ^ (k & 1)
    return (px, py, pc), 4 * px + 2 * py + pc


def _my_block():
    return 4 * lax.axis_index("x") + 2 * lax.axis_index("y") + lax.axis_index("c")


def _all_gather_direct(block, name):
    def body(x_ref, o_ref, send, recv, loc):
        me = _my_block()
        mine = pltpu.make_async_copy(x_ref, o_ref.at[me], loc)
        mine.start()
        cps = []
        for k in range(1, N_DEV):
            to, _ = _peer(k)
            cps.append(pltpu.make_async_remote_copy(
                src_ref=x_ref, dst_ref=o_ref.at[me], send_sem=send.at[k - 1], recv_sem=recv.at[k - 1],
                device_id=to, device_id_type=pl.DeviceIdType.MESH))
            cps[-1].start()
        for cp in cps:
            cp.wait()
        mine.wait()

    hbm = pl.BlockSpec(memory_space=pl.ANY)
    return pl.pallas_call(
        body, name=name, in_specs=[hbm], out_specs=hbm,
        out_shape=jax.ShapeDtypeStruct((N_DEV,) + block.shape, block.dtype),
        scratch_shapes=[pltpu.SemaphoreType.DMA((N_DEV - 1,)), pltpu.SemaphoreType.DMA((N_DEV - 1,)),
                        pltpu.SemaphoreType.DMA],
    )(block)


def _all_gather(shards, name):
    n = len(shards)

    def body(*refs):
        ins, outs = refs[:n], refs[n:2 * n]
        send, recv, loc = refs[2 * n:]
        x, y, c = lax.axis_index("x"), lax.axis_index("y"), lax.axis_index("c")
        sibling = (x, y, 1 - c)
        chips = [(1 - x, y), (x, 1 - y), (1 - x, 1 - y)]
        slot = lambda px, py, pc: 4 * px + 2 * py + pc

        def copy(a, k, block, to, src=None):
            dst = outs[a].at[slot(*block)]
            return pltpu.make_async_remote_copy(
                src_ref=dst if src is None else src, dst_ref=dst, send_sem=send.at[a, k], recv_sem=recv.at[a, k],
                device_id=to, device_id_type=pl.DeviceIdType.MESH)

        mine = [pltpu.make_async_copy(ins[a], outs[a].at[slot(x, y, c)], loc.at[a]) for a in range(n)]
        for cp in mine:
            cp.start()
        first = []
        for a in range(n):
            first.append(copy(a, 0, (x, y, c), sibling, src=ins[a]))
            first += [copy(a, 1 + j, (x, y, c), (*chip, c), src=ins[a]) for j, chip in enumerate(chips)]
        for cp in first:
            cp.start()
        passed = []
        for j, chip in enumerate(chips):
            for a in range(n):
                copy(a, 1 + j, (*chip, c), (x, y, c)).wait_recv()
                fwd = copy(a, 4 + j, (*chip, c), sibling)
                fwd.start()
                passed.append(fwd)
        for a in range(n):
            copy(a, 0, sibling, (x, y, c)).wait_recv()
            for j, chip in enumerate(chips):
                copy(a, 4 + j, (*chip, 1 - c), (x, y, c)).wait_recv()
        for cp in first + passed:
            cp.wait_send()
        for cp in mine:
            cp.wait()

    hbm = pl.BlockSpec(memory_space=pl.ANY)
    return pl.pallas_call(
        body, name=name,
        in_specs=[hbm] * n, out_specs=[hbm] * n,
        out_shape=[jax.ShapeDtypeStruct((N_DEV,) + s.shape, s.dtype) for s in shards],
        scratch_shapes=[pltpu.SemaphoreType.DMA((n, N_DEV - 1)), pltpu.SemaphoreType.DMA((n, N_DEV - 1)),
                        pltpu.SemaphoreType.DMA((n,))],
    )(*shards)


_HBM = pl.BlockSpec(memory_space=pltpu.HBM)
_SEM = pl.BlockSpec(memory_space=pltpu.SEMAPHORE)
_EFFECT = pltpu.SideEffectType.DATAFLOW_SIDE_EFFECTING


def _direct_copies(srcs, lands, send, recv, slab_source):
    me = _my_block()
    cps = []
    for k in range(1, N_DEV):
        to, blk = _peer(k)
        for a, (src, land) in enumerate(zip(srcs, lands)):
            if slab_source:
                s, d = src.at[blk], land.at[me]
            elif land.ndim == 3:
                s, d = src, land.at[me]
            else:
                s, d = src, land.at[pl.ds(pl.multiple_of(me * src.shape[0], 16), src.shape[0]), :]
            cps.append(pltpu.make_async_remote_copy(
                src_ref=s, dst_ref=d,
                send_sem=send.at[a * (N_DEV - 1) + k - 1], recv_sem=recv.at[a * (N_DEV - 1) + k - 1],
                device_id=to, device_id_type=pl.DeviceIdType.MESH))
    return cps


def _exchange_start(srcs, name, slab_source, axes=None):
    n = len(srcs)
    if slab_source:
        shapes = [s.shape for s in srcs]
    else:
        shapes = [(N_DEV,) + s.shape if ax == 1 else (N_DEV * s.shape[0], s.shape[1]) for s, ax in zip(srcs, axes)]
    lands = [pltpu.with_memory_space_constraint(lax.empty(sh, s.dtype), pltpu.HBM) for sh, s in zip(shapes, srcs)]

    def body(*refs):
        ins, land_in = refs[:n], refs[n:2 * n]
        send, recv = refs[2 * n], refs[2 * n + 1]
        token = refs[4 * n + 2]
        for cp in _direct_copies(ins, land_in, send, recv, slab_source):
            cp.start()
        token[...] = jnp.zeros_like(token)

    sems = pltpu.SemaphoreType.DMA((n * (N_DEV - 1),))
    res = pl.pallas_call(
        body, name=name,
        out_shape=(sems, sems, *[pltpu.HBM(s.shape, s.dtype) for s in srcs],
                   *[pltpu.HBM(l.shape, l.dtype) for l in lands], jax.ShapeDtypeStruct((8, LANE), F32)),
        in_specs=[_HBM] * (2 * n),
        out_specs=(_SEM, _SEM, *[_HBM] * (2 * n), pl.BlockSpec(memory_space=pltpu.VMEM)),
        input_output_aliases={i: 2 + i for i in range(2 * n)},
        compiler_params=pltpu.CompilerParams(has_side_effects=_EFFECT),
    )(*[pltpu.with_memory_space_constraint(s, pltpu.HBM) for s in srcs], *lands)
    return dict(send=res[0], recv=res[1], srcs=res[2:2 + n], lands=res[2 + n:2 + 2 * n], token=res[-1],
                slab_source=slab_source)


def _exchange_wait(ex, after, name):
    n = len(ex["srcs"])
    slab_source = ex["slab_source"]

    def body(*refs):
        ins, lands = refs[:n], refs[n:2 * n]
        send, recv = refs[2 * n], refs[2 * n + 1]
        for cp in _direct_copies(ins, lands, send, recv, slab_source):
            cp.wait_send()
            cp.wait_recv()

    res = pl.pallas_call(
        body, name=name,
        out_shape=tuple(pltpu.HBM(s.shape, s.dtype) for s in list(ex["srcs"]) + list(ex["lands"])),
        in_specs=[_HBM] * (2 * n) + [_SEM, _SEM, pl.BlockSpec(memory_space=pl.ANY)],
        out_specs=tuple([_HBM] * (2 * n)),
        input_output_aliases={i: i for i in range(2 * n)},
        compiler_params=pltpu.CompilerParams(has_side_effects=_EFFECT),
    )(*ex["srcs"], *ex["lands"], ex["send"], ex["recv"], after)
    me = _my_block()
    out = []
    for src, land in zip(res[:n], res[n:]):
        if slab_source:
            own, at = lax.dynamic_index_in_dim(src, me, 0, keepdims=True), (me, 0, 0)
        elif land.ndim == 3:
            own, at = src[None], (me, 0, 0)
        else:
            own, at = src, (me * src.shape[0], 0)
        out.append(lax.dynamic_update_slice(land, own, at))
    return out


BIG = ("w_in", "w_out", "w_gate", "w_up", "w_down")
BIG_SHARD = {"w_in": (1024, 578), "w_out": (256, 1024), "w_gate": (1024, 352), "w_up": (1024, 352),
             "w_down": (352, 1024)}
BIG_SHARD_AXIS = {"w_in": 1, "w_out": 0, "w_gate": 1, "w_up": 1, "w_down": 0}
BIG_ADAM_ROWS = {"w_in": 256, "w_out": 128, "w_gate": 256, "w_up": 256, "w_down": 176}


def _join(parts, axis):
    if axis == 0:
        return parts.reshape((-1,) + parts.shape[2:])
    return jnp.concatenate([parts[j] for j in range(N_DEV)], axis=1)


def _split(full, axis):
    if axis == 0:
        return full.reshape((N_DEV, full.shape[0] // N_DEV) + full.shape[1:])
    c = full.shape[1] // N_DEV
    return jnp.stack([full[:, c * j:c * (j + 1)] for j in range(N_DEV)])


SMALL = (("lru_wa", 65536), ("lru_wx", 65536), ("pre_mix_norm", 1024), ("lru_conv_w", 4096), ("lru_conv_b", 1024),
         ("lru_ba", 1024), ("lru_bx", 1024), ("lru_lambda", 1024), ("lru_out_norm", 1024), ("ssd_conv_w", 6144),
         ("ssd_conv_b", 1536), ("ssd_dt_bias", 16), ("ssd_a_log", 16), ("ssd_d", 16), ("ssd_out_norm", 1024),
         ("post_mix_norm", 1024), ("pre_ffn_norm", 1024), ("post_ffn_norm", 1024), ("loss", 1))
REPLICATED = tuple(n for n, _ in SMALL if n not in ("lru_conv_w", "ssd_conv_w", "loss"))
SMALL_ROW = {}
for _name, _size in SMALL:
    SMALL_ROW[_name] = (sum(-(-s // 1024) for n, s in SMALL[:len(SMALL_ROW)]), -(-_size // 1024))


def _pack_small(d):
    rows = [jnp.pad(d[n].astype(F32).reshape(-1), (0, SMALL_ROW[n][1] * 1024 - s)).reshape(-1, 1024) for n, s in SMALL]
    used = sum(r.shape[0] for r in rows)
    return jnp.concatenate(rows + [jnp.zeros((SMALL_ROWS - used, 1024), F32)], axis=0)


def _small_entry(p, name):
    row, rows = SMALL_ROW[name]
    return p[row:row + rows].reshape(-1)[:dict(SMALL)[name]]


def _blockdiag4(w):
    on_diag = jnp.eye(4, dtype=w.dtype)[None, :, None, :, None]
    return (w.reshape(4, 4, 64, 1, 64) * on_diag).reshape(4, 256, 256)


def _diag_blocks(g):
    on_diag = jnp.eye(4, dtype=g.dtype)[None, :, None, :, None]
    return jnp.sum(g.reshape(4, 4, 64, 4, 64) * on_diag, axis=3).reshape(16, 64, 64)


def _local_step(x, target, w_in, P, rest_weights, emit, emit_small, start_token=None):
    cut = IN_MAIN - (N_DEV - 1) * (IN_COLS // N_DEV)
    w_main = jnp.concatenate([w_in[j] for j in range(N_DEV - 1)] + [w_in[N_DEV - 1][:, :cut]], axis=1)
    w_dt = jnp.pad(w_in[N_DEV - 1][:, cut:], ((0, 0), (0, LANE - SSD_HEADS)))
    pad16 = lambda v: jnp.pad(v.reshape(1, SSD_HEADS), ((0, 0), (0, LANE - SSD_HEADS)))
    dt_bias, a_log = pad16(P["ssd_dt_bias"]), pad16(P["ssd_a_log"])
    d_e = jnp.repeat(P["ssd_d"].reshape(SSD_HEADS), SSD_HEAD_DIM).reshape(1, SSD_INNER)
    expand = (jnp.arange(LANE)[:, None] == (jnp.arange(SSD_INNER)[None, :] // SSD_HEAD_DIM)).astype(BF16)
    wa_bd = _blockdiag4(P["lru_wa"].astype(BF16))
    wx_bd = _blockdiag4(P["lru_wx"].astype(BF16))
    vec = lambda n: P[n].reshape(1, -1)

    after = lambda v, tok: v if tok is None else v + tok[0:1, 0:1]

    h, proj, dtp = _inproj_fwd(x, after(vec("pre_mix_norm"), start_token), w_main, w_dt)
    lx, hl, y_lru, *gates = _lru_fwd(proj, P["lru_conv_w"], vec("lru_conv_b"), wa_bd, wx_bd, vec("lru_ba"),
                                     vec("lru_bx"), vec("lru_lambda"), vec("lru_out_norm"))
    xc, y, y_ssd, sprev = _ssd_fwd(proj, dtp, P["ssd_conv_w"], vec("ssd_conv_b"), dt_bias, a_log, d_e,
                                   vec("ssd_out_norm"), expand)
    W = rest_weights(y_ssd)
    mix, x1, h2 = _outproj_fwd(x, y_lru, y_ssd, W["w_out"], vec("post_mix_norm"), vec("pre_ffn_norm"))
    dx1, act, df, dgt, dup, dg_pf, dg_ff, loss = _ffn_fwd_bwd(
        x1, h2, target, W["w_gate"], W["w_up"], W["w_down"], vec("pre_ffn_norm"), vec("post_ffn_norm"))
    tok = emit("ffn", {"w_gate": _tn_matmul_slabs(h2, [dgt], "dw_gate", D_FF // N_DEV),
                       "w_up": _tn_matmul_slabs(h2, [dup], "dw_up", D_FF // N_DEV),
                       "w_down": _tn_matmul(act, [df], "dw_down")[0]})
    dy_lru, dy_ssd, dmix, dg_pm = _outproj_bwd(dx1, mix, W["w_out"], after(vec("post_mix_norm"), tok))
    tok = emit("out", {"w_out": jnp.concatenate([_tn_matmul(y_lru, [dmix], "dw_out_lru")[0],
                                                 _tn_matmul(y_ssd, [dmix], "dw_out_ssd")[0]], axis=0)})
    dp_ssd, ddtp, dcw_s, dcb_s, dbias, dA, dD_e, dg_ssd = _ssd_bwd(
        dy_ssd, proj, dtp, xc, y, sprev, P["ssd_conv_w"], dt_bias, a_log, d_e,
        after(vec("ssd_out_norm"), tok), expand)
    dp_lru, dpa, dpx, dcw_l, dcb_l, dba, dbx, dlam, dg_lru = _lru_bwd(
        dy_lru, proj, lx, hl, gates, P["lru_conv_w"], wa_bd, wx_bd, vec("lru_lambda"), vec("lru_out_norm"))
    tok = emit("in", {"w_in": _tn_matmul_slabs(h, [dp_lru, dp_ssd, ddtp], "dw_in", IN_COLS // N_DEV)})

    a_neg = -jnp.exp(P["ssd_a_log"].reshape(SSD_HEADS))
    dwa, dwx = _tn_blockdiag(lx, dpa, dpx, "dw_lru_gates")
    small = {
        "pre_mix_norm": jnp.zeros((1, D_MODEL), F32), "lru_conv_w": dcw_l, "lru_conv_b": dcb_l,
        "lru_wa": _diag_blocks(dwa), "lru_ba": dba,
        "lru_wx": _diag_blocks(dwx), "lru_bx": dbx,
        "lru_lambda": dlam, "lru_out_norm": dg_lru, "ssd_conv_w": dcw_s, "ssd_conv_b": dcb_s,
        "ssd_dt_bias": dbias[0, :SSD_HEADS], "ssd_a_log": dA[0, :SSD_HEADS] * a_neg,
        "ssd_d": jnp.sum(dD_e.reshape(SSD_HEADS, SSD_HEAD_DIM), axis=1), "ssd_out_norm": dg_ssd,
        "post_mix_norm": dg_pm, "pre_ffn_norm": dg_pf, "post_ffn_norm": dg_ff, "loss": loss[0, 0:1],
    }
    tok = after(after(vec("pre_mix_norm"), tok), emit_small(small))
    grad_x, dg_pre = _inproj_bwd(dp_lru, dp_ssd, ddtp, dx1, x, tok, w_main, w_dt)
    return grad_x, dg_pre


def kernel(x, pre_mix_norm, w_in, lru_conv_w, lru_conv_b, lru_wa, lru_ba, lru_wx, lru_bx, lru_lambda, lru_out_norm, ssd_conv_w, ssd_conv_b, ssd_dt_bias, ssd_a_log, ssd_d, ssd_out_norm, w_out, post_mix_norm, pre_ffn_norm, w_gate, w_up, w_down, post_ffn_norm, loss_target, m_pre_mix_norm, m_w_in, m_lru_conv_w, m_lru_conv_b, m_lru_wa, m_lru_ba, m_lru_wx, m_lru_bx, m_lru_lambda, m_lru_out_norm, m_ssd_conv_w, m_ssd_conv_b, m_ssd_dt_bias, m_ssd_a_log, m_ssd_d, m_ssd_out_norm, m_w_out, m_post_mix_norm, m_pre_ffn_norm, m_w_gate, m_w_up, m_w_down, m_post_ffn_norm, v_pre_mix_norm, v_w_in, v_lru_conv_w, v_lru_conv_b, v_lru_wa, v_lru_ba, v_lru_wx, v_lru_bx, v_lru_lambda, v_lru_out_norm, v_ssd_conv_w, v_ssd_conv_b, v_ssd_dt_bias, v_ssd_a_log, v_ssd_d, v_ssd_out_norm, v_w_out, v_post_mix_norm, v_pre_ffn_norm, v_w_gate, v_w_up, v_w_down, v_post_ffn_norm):
    a = dict(locals())
    names = [n for n, _ in SMALL if n != "loss"] + list(BIG)
    w = {n: a[n][0] for n in names}
    m = {n: a["m_" + n][0] for n in names}
    v = {n: a["v_" + n][0] for n in names}

    cpack = jnp.concatenate([w["lru_conv_w"], w["ssd_conv_w"], jnp.zeros((4, 64), F32)], axis=1)
    cpack = jnp.pad(cpack, ((0, 4), (0, 0)))
    g_in, cg = _all_gather([w["w_in"].astype(BF16), cpack], "all_gather_w_in")
    P = {n: w[n] for n in REPLICATED}
    P["lru_conv_w"] = _join(cg[:, 0:4, 0:128], 1)
    P["ssd_conv_w"] = _join(cg[:, 0:4, 128:320], 1)

    rest = [n for n in BIG if n != "w_in"]
    zero = jnp.minimum(jnp.abs(cg[0, 0, 0]), 0.0)
    ex_w = _exchange_start([(w[n] + zero).astype(BF16) for n in rest], "weights_start", slab_source=False,
                           axes=[BIG_SHARD_AXIS[n] for n in rest])

    def rest_weights(after):
        lands = _exchange_wait(ex_w, after, "weights_wait")
        return {n: _join(p, 1) if p.ndim == 3 else p for n, p in zip(rest, lands)}

    pending = []

    def emit(group, grads):
        ex = _exchange_start([g if g.ndim == 3 else _split(g, BIG_SHARD_AXIS[n]) for n, g in grads.items()],
                             "grads_start_" + group, slab_source=True)
        pending.append((group, list(grads), ex))
        return ex["token"]

    def emit_small(small):
        ex = _exchange_start([_pack_small(small)], "small_start", slab_source=False, axes=[0])
        pending.append(("small", None, ex))
        return ex["token"]

    grad_x, dg_pre = _local_step(x[0], loss_target[0], g_in, P, rest_weights, emit, emit_small, ex_w["token"])

    got_pre = _all_gather_direct(jnp.pad(dg_pre, ((0, 7), (0, 0))), "all_gather_pre_mix_norm")
    ex_small = pending.pop()[2]
    got_small = _exchange_wait(ex_small, got_pre, "small_wait")[0].reshape(N_DEV, SMALL_ROWS, 1024)
    row = SMALL_ROW["pre_mix_norm"][0]
    got_small = got_small.at[:, row:row + 1, :].set(got_pre[:, 0:1, :])

    outs = {}
    done = got_small
    for group, group_names, ex in pending:
        for n, parts in zip(group_names, _exchange_wait(ex, done, "grads_wait_" + group)):
            res = _adamw(parts, w[n], m[n], v[n], "adamw_" + n, BIG_ADAM_ROWS[n])
            done = res[0]
            for kind, r in zip(("grad", "delta", "new_m", "new_v"), res):
                outs[kind + "_" + n] = r

    two_d = lambda t: t.reshape(-1, 1024) if t.ndim == 3 else t.reshape(1, -1)
    res = _adamw_small(got_small, [(SMALL_ROW[n][0], two_d(w[n]), two_d(m[n]), two_d(v[n])) for n in REPLICATED])
    g_pack = res[0]
    for i, n in enumerate(REPLICATED):
        for kind, r in zip(("grad", "delta", "new_m", "new_v"), res[1 + 4 * i:5 + 4 * i]):
            outs[kind + "_" + n] = r.reshape(w[n].shape)

    me = _my_block()
    gl = lax.dynamic_slice(_small_entry(g_pack, "lru_conv_w").reshape(4, LRU_W), (0, me * 128), (4, 128))
    gs = lax.dynamic_slice(_small_entry(g_pack, "ssd_conv_w").reshape(4, SSD_CONV_CH), (0, me * 192), (4, 192))
    cat = lambda d: jnp.pad(jnp.concatenate([d["lru_conv_w"], d["ssd_conv_w"]], axis=1), ((0, 4), (0, 64)))
    res = _adamw(cat({"lru_conv_w": gl, "ssd_conv_w": gs})[None], cat(w), cat(m), cat(v), "adamw_conv", 8)
    for kind, r in zip(("grad", "delta", "new_m", "new_v"), res):
        outs[kind + "_lru_conv_w"] = r[0:4, 0:128]
        outs[kind + "_ssd_conv_w"] = r[0:4, 128:320]

    order = ["pre_mix_norm", "w_in", "lru_conv_w", "lru_conv_b", "lru_wa", "lru_ba", "lru_wx", "lru_bx", "lru_lambda",
             "lru_out_norm", "ssd_conv_w", "ssd_conv_b", "ssd_dt_bias", "ssd_a_log", "ssd_d", "ssd_out_norm", "w_out",
             "post_mix_norm", "pre_ffn_norm", "w_gate", "w_up", "w_down", "post_ffn_norm"]
    result = [_small_entry(g_pack, "loss").reshape(()), grad_x[None]]
    for kind in ("grad", "delta", "new_m", "new_v"):
        result += [outs[kind + "_" + n][None] for n in order]
    return tuple(result)
```

```python
import functools

import jax
import jax.numpy as jnp
from jax import lax
from jax.experimental import pallas as pl
from jax.experimental.pallas import tpu as pltpu

F32 = jnp.float32
BF16 = jnp.bfloat16
EPS = 1e-6
N_DEV = 8
D_MODEL = 1024
LRU_W = 1024
SSD_INNER = 1024
SSD_HEADS = 16
SSD_HEAD_DIM = 64
SSD_STATE = 128
SSD_GROUPS = 2
SSD_CONV_CH = 1536
CHUNK = 128
D_FF = 2816
FF_CHUNK = 2816
IN_MAIN = 4608
IN_COLS = 4624
LANE = 128
TT = 256
TP = 512
VMEM_LIMIT = 56 * 1024 * 1024
ADAM_LR, ADAM_B1, ADAM_B2, ADAM_EPS, ADAM_WD, ADAM_STEP = 0.001, 0.9, 0.999, 1e-08, 0.01, 10
SMALL_ROWS = 160

NT = (((1,), (1,)), ((), ()))
TN = (((0,), (0,)), ((), ()))


def _params(n_grid):
    return pltpu.CompilerParams(dimension_semantics=("arbitrary",) * n_grid, vmem_limit_bytes=VMEM_LIMIT)


def _dot(a, b, dims=None, precision=None):
    if dims is None:
        return jnp.dot(a, b, preferred_element_type=F32, precision=precision)
    return lax.dot_general(a, b, dims, preferred_element_type=F32, precision=precision)


def _split_bf16(x, terms):
    out = []
    for _ in range(terms - 1):
        p = x.astype(BF16)
        out.append(p)
        x = x - p.astype(F32)
    return out + [x.astype(BF16)]


def _dot_sel(x, sel, dims=None, terms=2, sel_first=False):
    parts = [_dot(sel, p, dims) if sel_first else _dot(p, sel, dims) for p in _split_bf16(x, terms)]
    return functools.reduce(lambda a, b: a + b, parts)


def _sigmoid(x):
    return 0.5 * jnp.tanh(0.5 * x) + 0.5


def _softplus(x):
    e = jnp.exp(-jnp.abs(x))
    l1p = jnp.where(e < 1e-3, e * (1.0 - e * (0.5 - e * (1.0 / 3.0))), jnp.log(1.0 + e))
    return jnp.maximum(x, 0.0) + l1p


def _neg_expm1(x):
    series = -x * (1.0 + x * (0.5 + x * (1.0 / 6.0 + x * (1.0 / 24.0))))
    return jnp.where(x > -0.01, series, 1.0 - jnp.exp(x))


_GELU_C = 0.7978845608028654


def _gelu(x):
    t = jnp.tanh(_GELU_C * (x + 0.044715 * x * x * x))
    return 0.5 * x * (1.0 + t), t


def _gelu_grad(x, t):
    return 0.5 * (1.0 + t) + 0.5 * x * (1.0 - t * t) * _GELU_C * (1.0 + 3.0 * 0.044715 * x * x)


def _rms(x):
    return lax.rsqrt(jnp.mean(x * x, axis=-1, keepdims=True) + EPS)


def _rms_bwd(dyn, x, rn):
    return rn * dyn - x * (rn * rn * rn) * jnp.mean(dyn * x, axis=-1, keepdims=True)


def _row(x, r):
    idx = lax.broadcasted_iota(jnp.int32, x.shape, 0)
    return jnp.sum(jnp.where(idx == r, x, 0.0), axis=0, keepdims=True)


def _shift_down(cur, prev8, j):
    s = pltpu.roll(cur, j, 0)
    p = pltpu.roll(prev8, j, 0)
    r8 = lax.broadcasted_iota(jnp.int32, prev8.shape, 0)
    top = jnp.where(r8 < j, p, s[0:8])
    return jnp.concatenate([top, s[8:]], axis=0)


def _shift_up(cur, next8, j):
    n = cur.shape[0]
    s = pltpu.roll(cur, n - j, 0)
    p = pltpu.roll(next8, 8 - j, 0)
    r8 = lax.broadcasted_iota(jnp.int32, next8.shape, 0)
    bot = jnp.where(r8 >= 8 - j, p, s[n - 8:n])
    return jnp.concatenate([s[:n - 8], bot], axis=0)


def _scan_fwd(a, u):
    n = a.shape[0]
    row = lax.broadcasted_iota(jnp.int32, a.shape, 0)
    k = 1
    while k < n:
        ok = row >= k
        a_s = jnp.where(ok, pltpu.roll(a, k, 0), 1.0)
        u_s = jnp.where(ok, pltpu.roll(u, k, 0), 0.0)
        u = a * u_s + u
        a = a * a_s
        k *= 2
    return a, u


def _scan_bwd(b, d):
    n = b.shape[0]
    row = lax.broadcasted_iota(jnp.int32, b.shape, 0)
    k = 1
    while k < n:
        ok = row < n - k
        b_s = jnp.where(ok, pltpu.roll(b, n - k, 0), 1.0)
        d_s = jnp.where(ok, pltpu.roll(d, n - k, 0), 0.0)
        d = b * d_s + d
        b = b * b_s
        k *= 2
    return b, d


def _scan_tile(a, u, carry, a_s, u_s, reverse):
    n, c = a.shape
    groups = n // 8
    r8 = lax.broadcasted_iota(jnp.int32, a.shape, 0) & 7
    in_group = lambda x, k: pltpu.roll(x.reshape(groups, 8, c), k, 1).reshape(n, c)
    for k in (1, 2, 4):
        ok = (r8 < 8 - k) if reverse else (r8 >= k)
        shift = 8 - k if reverse else k
        a_n = jnp.where(ok, in_group(a, shift), 1.0)
        u_n = jnp.where(ok, in_group(u, shift), 0.0)
        u = a * u_n + u
        a = a * a_n
    nl = c // LANE
    for j in range(nl):
        a_s[j] = a[:, LANE * j:LANE * (j + 1)]
        u_s[j] = u[:, LANE * j:LANE * (j + 1)]
    end = 0 if reverse else 7
    ends = lambda ref, j: ref[pl.ds(j, 1), pl.ds(end, groups, stride=8), :].reshape(groups, LANE)
    ga = jnp.concatenate([ends(a_s, j) for j in range(nl)], axis=1)
    gu = jnp.concatenate([ends(u_s, j) for j in range(nl)], axis=1)
    gacc, gh = (_scan_bwd if reverse else _scan_fwd)(ga, gu)
    gh = gh + gacc * carry
    grow = lax.broadcasted_iota(jnp.int32, gh.shape, 0)
    if reverse:
        cin = jnp.where(grow == groups - 1, carry, pltpu.roll(gh, groups - 1, 0))
    else:
        cin = jnp.where(grow == 0, carry, pltpu.roll(gh, 1, 0))
    spread = ((lax.broadcasted_iota(jnp.int32, (n, LANE), 0) >> 3)
              == lax.broadcasted_iota(jnp.int32, (n, LANE), 1)).astype(BF16)
    cin = jnp.concatenate([cin, jnp.zeros((LANE - groups, c), F32)], axis=0)
    return u + a * _dot_sel(cin, spread, terms=3, sel_first=True)


def _inproj_fwd(x, g_pre, w_main, w_dt):
    T = x.shape[0]
    TT = TP

    def body(x_ref, g_ref, wm_hbm, wd_hbm, h_ref, proj_ref, dtp_ref, wm, wd, sem):
        @pl.when(pl.program_id(0) == 0)
        def _():
            c1 = pltpu.make_async_copy(wm_hbm, wm, sem.at[0])
            c2 = pltpu.make_async_copy(wd_hbm, wd, sem.at[1])
            c1.start()
            c2.start()
            c1.wait()
            c2.wait()

        xv = x_ref[...]
        h = (xv * _rms(xv) * g_ref[...]).astype(BF16)
        h_ref[...] = h
        proj_ref[...] = _dot(h, wm[...])
        dtp_ref[...] = _dot(h, wd[...])

    return pl.pallas_call(
        body, name="inproj_fwd", grid=(T // TT,),
        in_specs=[pl.BlockSpec((TT, D_MODEL), lambda i: (i, 0)),
                  pl.BlockSpec((1, D_MODEL), lambda i: (0, 0)),
                  pl.BlockSpec(memory_space=pl.ANY), pl.BlockSpec(memory_space=pl.ANY)],
        out_specs=[pl.BlockSpec((TT, D_MODEL), lambda i: (i, 0)),
                   pl.BlockSpec((TT, IN_MAIN), lambda i: (i, 0)),
                   pl.BlockSpec((TT, LANE), lambda i: (i, 0))],
        out_shape=[jax.ShapeDtypeStruct((T, D_MODEL), BF16), jax.ShapeDtypeStruct((T, IN_MAIN), F32),
                   jax.ShapeDtypeStruct((T, LANE), F32)],
        scratch_shapes=[pltpu.VMEM((D_MODEL, IN_MAIN), BF16), pltpu.VMEM((D_MODEL, LANE), BF16),
                        pltpu.SemaphoreType.DMA((2,))],
        compiler_params=_params(1),
    )(x, g_pre, w_main, w_dt)


def _inproj_bwd(dp_lru, dp_ssd, ddtp, dx1, x, g_pre, w_main, w_dt):
    T = x.shape[0]
    TT = TP

    def body(dl_ref, ds_ref, dd_ref, dx1_ref, x_ref, g_ref, wm_hbm, wd_hbm, gx_ref, dg_ref, wm, wd, sem):
        @pl.when(pl.program_id(0) == 0)
        def _():
            c1 = pltpu.make_async_copy(wm_hbm, wm, sem.at[0])
            c2 = pltpu.make_async_copy(wd_hbm, wd, sem.at[1])
            c1.start()
            c2.start()
            c1.wait()
            c2.wait()
            dg_ref[...] = jnp.zeros_like(dg_ref)

        dh = _dot(dl_ref[...], wm[:, 0:2048], NT)
        dh += _dot(ds_ref[...], wm[:, 2048:IN_MAIN], NT)
        dh += _dot(dd_ref[...], wd[...], NT)
        xv = x_ref[...]
        rn = _rms(xv)
        dg_ref[...] += jnp.sum(dh * xv * rn, axis=0, keepdims=True)
        gx_ref[...] = dx1_ref[...] + _rms_bwd(dh * g_ref[...], xv, rn)

    return pl.pallas_call(
        body, name="inproj_bwd", grid=(T // TT,),
        in_specs=[pl.BlockSpec((TT, 2048), lambda i: (i, 0)),
                  pl.BlockSpec((TT, 2560), lambda i: (i, 0)),
                  pl.BlockSpec((TT, LANE), lambda i: (i, 0)),
                  pl.BlockSpec((TT, D_MODEL), lambda i: (i, 0)),
                  pl.BlockSpec((TT, D_MODEL), lambda i: (i, 0)),
                  pl.BlockSpec((1, D_MODEL), lambda i: (0, 0)),
                  pl.BlockSpec(memory_space=pl.ANY), pl.BlockSpec(memory_space=pl.ANY)],
        out_specs=[pl.BlockSpec((TT, D_MODEL), lambda i: (i, 0)),
                   pl.BlockSpec((1, D_MODEL), lambda i: (0, 0))],
        out_shape=[jax.ShapeDtypeStruct((T, D_MODEL), F32), jax.ShapeDtypeStruct((1, D_MODEL), F32)],
        scratch_shapes=[pltpu.VMEM((D_MODEL, IN_MAIN), BF16), pltpu.VMEM((D_MODEL, LANE), BF16),
                        pltpu.SemaphoreType.DMA((2,))],
        compiler_params=_params(1),
    )(dp_lru, dp_ssd, ddtp, dx1, x, g_pre, w_main, w_dt)


def _lru_gates(lx, wa_ref, wx_ref, ba, bx, lam):
    lxb = lx.astype(BF16)
    pa = jnp.concatenate([_dot(lxb[:, 256 * k:256 * (k + 1)], wa_ref[k]) for k in range(4)], axis=1) + ba
    px = jnp.concatenate([_dot(lxb[:, 256 * k:256 * (k + 1)], wx_ref[k]) for k in range(4)], axis=1) + bx
    r = _sigmoid(pa)
    ig = _sigmoid(px)
    sp = _softplus(-lam)
    log_a = -8.0 * r * sp
    a = jnp.exp(log_a)
    mult = jnp.sqrt(_neg_expm1(2.0 * log_a))
    return r, ig, sp, a, mult


def _lru_fwd(proj, conv_w, conv_b, wa_bd, wx_bd, ba, bx, lam, g_lru):
    T = proj.shape[0]

    def body(cx_ref, gate_ref, cw_ref, cb_ref, wa_ref, wx_ref, ba_ref, bx_ref, lam_ref, g_ref,
             lx_ref, hl_ref, y_ref, r_ref, ig_ref, a_ref, mult_ref, tail, hcar, sa, su):
        @pl.when(pl.program_id(0) == 0)
        def _():
            tail[...] = jnp.zeros_like(tail)
            hcar[...] = jnp.zeros_like(hcar)

        cx = cx_ref[...]
        prev8 = tail[...]
        lx = cb_ref[...] + cw_ref[3:4, :] * cx
        for j in range(1, 4):
            lx += cw_ref[3 - j:4 - j, :] * _shift_down(cx, prev8, j)
        tail[...] = cx[TT - 8:TT]
        lx_ref[...] = lx
        r, ig, sp, a, mult = _lru_gates(lx, wa_ref, wx_ref, ba_ref[...], bx_ref[...], lam_ref[...])
        r_ref[...] = r
        ig_ref[...] = ig
        a_ref[...] = a
        mult_ref[...] = mult
        h = _scan_tile(a, mult * (ig * lx), hcar[...], sa, su, reverse=False)
        hl_ref[...] = h
        hcar[...] = hl_ref[TT - 1:TT, :]
        ge, _ = _gelu(gate_ref[...])
        p = h * ge
        y_ref[...] = (p * _rms(p) * g_ref[...]).astype(BF16)

    vec = pl.BlockSpec((1, LRU_W), lambda i: (0, 0))
    bd = pl.BlockSpec((4, 256, 256), lambda i: (0, 0, 0))
    tile = pl.BlockSpec((TT, LRU_W), lambda i: (i, 0))
    f32 = jax.ShapeDtypeStruct((T, LRU_W), F32)
    return pl.pallas_call(
        body, name="lru_fwd", grid=(T // TT,),
        in_specs=[tile, pl.BlockSpec((TT, LRU_W), lambda i: (i, 1)),
                  pl.BlockSpec((4, LRU_W), lambda i: (0, 0)), vec, bd, bd, vec, vec, vec, vec],
        out_specs=[tile] * 7,
        out_shape=[f32, f32, jax.ShapeDtypeStruct((T, LRU_W), BF16), f32, f32, f32, f32],
        scratch_shapes=[pltpu.VMEM((8, LRU_W), F32), pltpu.VMEM((1, LRU_W), F32)]
                       + [pltpu.VMEM((LRU_W // LANE, TT, LANE), F32)] * 2,
        compiler_params=_params(1),
    )(proj, proj, conv_w, conv_b, wa_bd, wx_bd, ba, bx, lam, g_lru)


def _lru_bwd(dy, proj, lx, hl, gates, conv_w, wa_bd, wx_bd, lam, g_lru):
    T = proj.shape[0]
    nt = T // TT

    def body(dy_ref, cx_ref, gate_ref, lx_ref, hl_ref, halo_ref, r_ref, ig_ref, a_ref, mult_ref, cw_ref, wa_ref,
             wx_ref, lam_ref, g_ref, dp_ref, dpa_ref, dpx_ref, dcw_ref, dcb_ref, dba_ref, dbx_ref, dlam_ref, dg_ref,
             gcar, acar, head, sa, su):
        i = pl.program_id(0)

        @pl.when(i == 0)
        def _():
            gcar[...] = jnp.zeros_like(gcar)
            acar[...] = jnp.zeros_like(acar)
            head[...] = jnp.zeros_like(head)
            for ref in (dcw_ref, dcb_ref, dba_ref, dbx_ref, dlam_ref, dg_ref):
                ref[...] = jnp.zeros_like(ref)

        lx = lx_ref[...]
        h = hl_ref[...]
        gate = gate_ref[...]
        cx = cx_ref[...]
        lam = lam_ref[...]
        r, ig, a, mult = r_ref[...], ig_ref[...], a_ref[...], mult_ref[...]
        sp = _softplus(-lam)
        ge, th = _gelu(gate)
        p = h * ge
        rn = _rms(p)
        dyv = dy_ref[...]
        dg_ref[...] += jnp.sum(dyv * p * rn, axis=0, keepdims=True)
        dp = _rms_bwd(dyv * g_ref[...], p, rn)
        dp_ref[:, LRU_W:2 * LRU_W] = (dp * h * _gelu_grad(gate, th)).astype(BF16)
        dh = dp * ge
        row = lax.broadcasted_iota(jnp.int32, a.shape, 0)
        b = jnp.where(row == TT - 1, acar[...], pltpu.roll(a, TT - 1, 0))
        g = _scan_tile(b, dh, gcar[...], sa, su, reverse=True)
        gcar[...] = _row(g[0:8], 0)
        acar[...] = _row(a[0:8], 0)
        h_last_prev = halo_ref[7:8, :] * (i < nt - 1).astype(F32)
        hprev = jnp.where(row == 0, h_last_prev, pltpu.roll(h, 1, 0))
        da = g * hprev
        dm2 = (g * (ig * lx)) * 0.5 / mult
        dlog_a = da * a - 2.0 * a * a * dm2
        dlam_ref[...] += jnp.sum(-8.0 * r * dlog_a, axis=0, keepdims=True) * (-_sigmoid(-lam))
        dpa = (-8.0 * sp * dlog_a) * r * (1.0 - r)
        dpx = (g * mult * lx) * ig * (1.0 - ig)
        dba_ref[...] += jnp.sum(dpa, axis=0, keepdims=True)
        dbx_ref[...] += jnp.sum(dpx, axis=0, keepdims=True)
        dpab = dpa.astype(BF16)
        dpxb = dpx.astype(BF16)
        dpa_ref[...] = dpab
        dpx_ref[...] = dpxb
        dlx = g * mult * ig + jnp.concatenate(
            [_dot(dpab[:, 256 * k:256 * (k + 1)], wa_ref[k], NT) + _dot(dpxb[:, 256 * k:256 * (k + 1)], wx_ref[k], NT)
             for k in range(4)], axis=1)
        nxt = head[...]
        dcb_ref[...] += jnp.sum(dlx, axis=0, keepdims=True)
        dcx = cw_ref[3:4, :] * dlx
        dcw_ref[3:4, :] += jnp.sum(cx * dlx, axis=0, keepdims=True)
        for j in range(1, 4):
            sh = _shift_up(dlx, nxt, j)
            dcx += cw_ref[3 - j:4 - j, :] * sh
            dcw_ref[3 - j:4 - j, :] += jnp.sum(cx * sh, axis=0, keepdims=True)
        head[...] = dlx[0:8]
        dp_ref[:, 0:LRU_W] = dcx.astype(BF16)

    rev = lambda i: (nt - 1 - i, 0)
    vec = pl.BlockSpec((1, LRU_W), lambda i: (0, 0))
    bd = pl.BlockSpec((4, 256, 256), lambda i: (0, 0, 0))
    tile = pl.BlockSpec((TT, LRU_W), rev)
    halo = pl.BlockSpec((8, LRU_W), lambda i: (jnp.maximum((nt - 1 - i) * (TT // 8) - 1, 0), 0))
    cw = pl.BlockSpec((4, LRU_W), lambda i: (0, 0))
    return pl.pallas_call(
        body, name="lru_bwd", grid=(nt,),
        in_specs=[tile, tile, pl.BlockSpec((TT, LRU_W), lambda i: (nt - 1 - i, 1)), tile, tile, halo,
                  tile, tile, tile, tile, cw, bd, bd, vec, vec],
        out_specs=[pl.BlockSpec((TT, 2 * LRU_W), rev), tile, tile, cw, vec, vec, vec, vec, vec],
        out_shape=[jax.ShapeDtypeStruct((T, 2 * LRU_W), BF16), jax.ShapeDtypeStruct((T, LRU_W), BF16),
                   jax.ShapeDtypeStruct((T, LRU_W), BF16), jax.ShapeDtypeStruct((4, LRU_W), F32)]
                  + [jax.ShapeDtypeStruct((1, LRU_W), F32)] * 5,
        scratch_shapes=[pltpu.VMEM((1, LRU_W), F32), pltpu.VMEM((1, LRU_W), F32), pltpu.VMEM((8, LRU_W), F32)]
                       + [pltpu.VMEM((LRU_W // LANE, TT, LANE), F32)] * 2,
        compiler_params=_params(1),
    )(dy, proj, proj, lx, hl, hl, *gates, conv_w, wa_bd, wx_bd, lam, g_lru)


def _ssd_chunk_terms(xc, dtp, bias, alog, expand):
    sg = _sigmoid(xc)
    xbc = xc * sg
    pre = dtp + bias
    dt = _softplus(pre)
    A = -jnp.exp(alog)
    ri = lax.broadcasted_iota(jnp.int32, (CHUNK, CHUNK), 0)
    ci = lax.broadcasted_iota(jnp.int32, (CHUNK, CHUNK), 1)
    tril = (ri >= ci).astype(BF16)
    cs = _dot_sel(dt * A, tril, terms=3, sel_first=True)
    cs_last = _row(cs, CHUNK - 1)
    ecs = jnp.exp(cs)
    dec = jnp.exp(cs_last - cs)
    return dict(sg=sg, xbc=xbc, pre=pre, dt=dt, A=A, cs=cs, csT=cs.T, ecs=ecs, dec=dec, ri=ri, ci=ci,
                dt_e=_dot_sel(dt, expand), ecs_e=_dot_sel(ecs, expand), dec_e=_dot_sel(dec, expand))


def _head_lambda(t, h):
    col = jnp.sum(jnp.where(t["ci"] == h, t["cs"], 0.0), axis=1, keepdims=True)
    rowv = jnp.sum(jnp.where(t["ri"] == h, t["csT"], 0.0), axis=0, keepdims=True)
    return jnp.exp(jnp.where(t["ri"] >= t["ci"], col - rowv, -1e30))


def _ssd_fwd(proj, dtp, conv_w, conv_b, dt_bias, a_log, d_e, g_ssd, expand):
    T = proj.shape[0]
    nc = T // CHUNK

    def body(z_ref, xp_ref, dtp_ref, cw_ref, cb_ref, bias_ref, alog_ref, de_ref, g_ref, ex_ref,
             xc_ref, y_ref, yn_ref, sprev_ref, tail, S):
        @pl.when(pl.program_id(0) == 0)
        def _():
            tail[...] = jnp.zeros_like(tail)
            S[...] = jnp.zeros_like(S)

        xp = xp_ref[...]
        prev8 = tail[...]
        xc = cb_ref[...] + cw_ref[3:4, :] * xp
        for j in range(1, 4):
            xc += cw_ref[3 - j:4 - j, :] * _shift_down(xp, prev8, j)
        tail[...] = xp[CHUNK - 8:CHUNK]
        xc_ref[...] = xc
        t = _ssd_chunk_terms(xc, dtp_ref[...], bias_ref[...], alog_ref[...], ex_ref[...])
        xbc = t["xbc"]
        sx = xbc[:, 0:SSD_INNER]
        Bb = xbc[:, SSD_INNER:SSD_INNER + 256].astype(BF16)
        Cb = xbc[:, SSD_INNER + 256:SSD_CONV_CH].astype(BF16)
        X = t["dt_e"] * sx
        lane = lax.broadcasted_iota(jnp.int32, (CHUNK, LANE), 1)
        G = [_dot(Cb[:, 128 * g:128 * (g + 1)], Bb[:, 128 * g:128 * (g + 1)], NT) for g in range(SSD_GROUPS)]
        for k in range(SSD_HEADS // 2):
            Xp = X[:, 128 * k:128 * (k + 1)]
            acc = jnp.zeros((CHUNK, LANE), F32)
            for half in range(2):
                M = (G[k // 4] * _head_lambda(t, 2 * k + half)).astype(BF16)
                Xh = jnp.where((lane >= 64) if half else (lane < 64), Xp, 0.0).astype(BF16)
                acc += _dot(M, Xh)
            y_ref[:, 128 * k:128 * (k + 1)] = acc
        sprev_ref[0] = S[...]
        eL_e = _row(t["ecs_e"], CHUNK - 1)
        Xd = (X * t["dec_e"]).astype(BF16)
        for g in range(SSD_GROUPS):
            sl = slice(512 * g, 512 * (g + 1))
            Sg = S[:, sl]
            y_ref[:, sl] += t["ecs_e"][:, sl] * _dot(Cb[:, 128 * g:128 * (g + 1)], Sg.astype(BF16))
            S[:, sl] = eL_e[:, sl] * Sg + _dot(Bb[:, 128 * g:128 * (g + 1)], Xd[:, sl], TN)
        y = y_ref[...] + de_ref[...] * sx
        y_ref[...] = y
        z = z_ref[...]
        q = y * (z * _sigmoid(z))
        yn_ref[...] = (q * _rms(q) * g_ref[...]).astype(BF16)

    c0 = lambda i: (0, 0)
    return pl.pallas_call(
        body, name="ssd_fwd", grid=(nc,),
        in_specs=[pl.BlockSpec((CHUNK, SSD_INNER), lambda i: (i, 2)),
                  pl.BlockSpec((CHUNK, SSD_CONV_CH), lambda i: (i, 2)),
                  pl.BlockSpec((CHUNK, LANE), lambda i: (i, 0)),
                  pl.BlockSpec((4, SSD_CONV_CH), c0), pl.BlockSpec((1, SSD_CONV_CH), c0),
                  pl.BlockSpec((1, LANE), c0), pl.BlockSpec((1, LANE), c0),
                  pl.BlockSpec((1, SSD_INNER), c0), pl.BlockSpec((1, SSD_INNER), c0),
                  pl.BlockSpec((LANE, SSD_INNER), c0)],
        out_specs=[pl.BlockSpec((CHUNK, SSD_CONV_CH), lambda i: (i, 0)),
                   pl.BlockSpec((CHUNK, SSD_INNER), lambda i: (i, 0)),
                   pl.BlockSpec((CHUNK, SSD_INNER), lambda i: (i, 0)),
                   pl.BlockSpec((1, SSD_STATE, SSD_INNER), lambda i: (i, 0, 0))],
        out_shape=[jax.ShapeDtypeStruct((T, SSD_CONV_CH), F32), jax.ShapeDtypeStruct((T, SSD_INNER), F32),
                   jax.ShapeDtypeStruct((T, SSD_INNER), BF16),
                   jax.ShapeDtypeStruct((nc, SSD_STATE, SSD_INNER), F32)],
        scratch_shapes=[pltpu.VMEM((8, SSD_CONV_CH), F32), pltpu.VMEM((SSD_STATE, SSD_INNER), F32)],
        compiler_params=_params(1),
    )(proj, proj, dtp, conv_w, conv_b, dt_bias, a_log, d_e, g_ssd, expand)


def _ssd_bwd(dyn, proj, dtp, xc, y, sprev, conv_w, dt_bias, a_log, d_e, g_ssd, expand):
    T = proj.shape[0]
    nc = T // CHUNK

    def body(dyn_ref, z_ref, xp_ref, dtp_ref, xc_ref, y_ref, sprev_ref, cw_ref, bias_ref, alog_ref, de_ref,
             g_ref, ex_ref, dp_ref, ddtp_ref, dcw_ref, dcb_ref, dbias_ref, dA_ref, dD_ref, dg_ref,
             dS, head, dX_s, dxbc_s):
        @pl.when(pl.program_id(0) == 0)
        def _():
            dS[...] = jnp.zeros_like(dS)
            head[...] = jnp.zeros_like(head)
            for ref in (dcw_ref, dcb_ref, dbias_ref, dA_ref, dD_ref, dg_ref):
                ref[...] = jnp.zeros_like(ref)

        ex = ex_ref[...]
        xc = xc_ref[...]
        t = _ssd_chunk_terms(xc, dtp_ref[...], bias_ref[...], alog_ref[...], ex)
        ri, ci = t["ri"], t["ci"]
        xbc = t["xbc"]
        sx = xbc[:, 0:SSD_INNER]
        Bb = xbc[:, SSD_INNER:SSD_INNER + 256].astype(BF16)
        Cb = xbc[:, SSD_INNER + 256:SSD_CONV_CH].astype(BF16)
        X = t["dt_e"] * sx
        z = z_ref[...]
        sz = _sigmoid(z)
        siluz = z * sz
        yv = y_ref[...]
        q = yv * siluz
        rn = _rms(q)
        dynv = dyn_ref[...]
        dg_ref[...] += jnp.sum(dynv * q * rn, axis=0, keepdims=True)
        dq = _rms_bwd(dynv * g_ref[...], q, rn)
        dp_ref[:, 0:SSD_INNER] = (dq * yv * (sz * (1.0 + z * (1.0 - sz)))).astype(BF16)
        dY = dq * siluz
        dD_ref[...] += jnp.sum(dY * sx, axis=0, keepdims=True)
        dYb = dY.astype(BF16)
        lane = lax.broadcasted_iota(jnp.int32, (CHUNK, LANE), 1)
        dcs = jnp.zeros((CHUNK, CHUNK), F32)
        dcsT = jnp.zeros((CHUNK, CHUNK), F32)
        Xb = X.astype(BF16)
        for g in range(SSD_GROUPS):
            Bg = Bb[:, 128 * g:128 * (g + 1)]
            Cg = Cb[:, 128 * g:128 * (g + 1)]
            G = _dot(Cg, Bg, NT)
            dGsum = jnp.zeros((CHUNK, CHUNK), F32)
            for k in range(4 * g, 4 * g + 4):
                Xp = Xb[:, 128 * k:128 * (k + 1)]
                dYp = dY[:, 128 * k:128 * (k + 1)]
                dXp = jnp.zeros((CHUNK, LANE), F32)
                for half in range(2):
                    h = 2 * k + half
                    lam = _head_lambda(t, h)
                    M = G * lam
                    dYh = jnp.where((lane >= 64) if half else (lane < 64), dYp, 0.0).astype(BF16)
                    dM = _dot(dYh, Xp, NT)
                    W = dM * M
                    dcs += jnp.where(ci == h, jnp.sum(W, axis=1, keepdims=True), 0.0)
                    dcsT += jnp.where(ri == h, jnp.sum(W, axis=0, keepdims=True), 0.0)
                    dGsum += dM * lam
                    dXp += _dot(M.astype(BF16), dYh, TN)
                dX_s[:, 128 * k:128 * (k + 1)] = dXp
            dGb = dGsum.astype(BF16)
            dxbc_s[:, SSD_INNER + 256 + 128 * g:SSD_INNER + 256 + 128 * (g + 1)] = _dot(dGb, Bg)
            dxbc_s[:, SSD_INNER + 128 * g:SSD_INNER + 128 * (g + 1)] = _dot(dGb, Cg, TN)
        dcs = dcs - dcsT.T
        Sp = sprev_ref[0]
        dSv = dS[...]
        ecs_e, dec_e = t["ecs_e"], t["dec_e"]
        eL_e = _row(ecs_e, CHUNK - 1)
        dYe = dY * ecs_e
        dYeb = dYe.astype(BF16)
        Xd = X * dec_e
        Xdb = Xd.astype(BF16)
        for g in range(SSD_GROUPS):
            sl = slice(512 * g, 512 * (g + 1))
            Bg = Bb[:, 128 * g:128 * (g + 1)]
            Cg = Cb[:, 128 * g:128 * (g + 1)]
            Spb = Sp[:, sl].astype(BF16)
            dSb = dSv[:, sl].astype(BF16)
            CS = _dot(Cg, Spb)
            BS = _dot(Bg, dSb)
            dxbc_s[:, SSD_INNER + 256 + 128 * g:SSD_INNER + 256 + 128 * (g + 1)] += _dot(dYeb[:, sl], Spb, NT)
            dxbc_s[:, SSD_INNER + 128 * g:SSD_INNER + 128 * (g + 1)] += _dot(Xdb[:, sl], dSb, NT)
            dS[:, sl] = eL_e[:, sl] * dSv[:, sl] + _dot(Cg, dYeb[:, sl], TN)
            dX_s[:, sl] += dec_e[:, sl] * BS
            dcs += _dot_sel(dYe[:, sl] * CS, ex[:, sl], NT, terms=1)
            tdec = _dot_sel(X[:, sl] * BS, ex[:, sl], NT, terms=1) * t["dec"]
            dcs -= tdec
            last = jnp.sum(tdec, axis=0, keepdims=True)
            last += jnp.sum(_dot_sel(Sp[:, sl] * dSv[:, sl], ex[:, sl], NT, terms=1), axis=0, keepdims=True) \
                * _row(t["ecs"], CHUNK - 1)
            dcs += jnp.where(ri == CHUNK - 1, last, 0.0)
        triu = (ci >= ri).astype(BF16)
        da = _dot_sel(dcs, triu, terms=3, sel_first=True)
        dX = dX_s[...]
        ddt = da * t["A"] + _dot_sel(dX * sx, ex, NT, terms=1)
        dA_ref[...] += jnp.sum(da * t["dt"], axis=0, keepdims=True)
        ddtp = ddt * _sigmoid(t["pre"])
        dbias_ref[...] += jnp.sum(ddtp, axis=0, keepdims=True)
        ddtp_ref[...] = ddtp.astype(BF16)
        dxbc_s[:, 0:SSD_INNER] = dX * t["dt_e"] + de_ref[...] * dY
        sg = t["sg"]
        dxc = dxbc_s[...] * (sg * (1.0 + xc * (1.0 - sg)))
        xp = xp_ref[...]
        nxt = head[...]
        dcb_ref[...] += jnp.sum(dxc, axis=0, keepdims=True)
        dpre = cw_ref[3:4, :] * dxc
        dcw_ref[3:4, :] += jnp.sum(xp * dxc, axis=0, keepdims=True)
        for j in range(1, 4):
            sh = _shift_up(dxc, nxt, j)
            dpre += cw_ref[3 - j:4 - j, :] * sh
            dcw_ref[3 - j:4 - j, :] += jnp.sum(xp * sh, axis=0, keepdims=True)
        head[...] = dxc[0:8]
        dp_ref[:, SSD_INNER:SSD_INNER + SSD_CONV_CH] = dpre.astype(BF16)

    c0 = lambda i: (0, 0)
    rev = lambda i: (nc - 1 - i, 0)
    return pl.pallas_call(
        body, name="ssd_bwd", grid=(nc,),
        in_specs=[pl.BlockSpec((CHUNK, SSD_INNER), rev),
                  pl.BlockSpec((CHUNK, SSD_INNER), lambda i: (nc - 1 - i, 2)),
                  pl.BlockSpec((CHUNK, SSD_CONV_CH), lambda i: (nc - 1 - i, 2)),
                  pl.BlockSpec((CHUNK, LANE), rev),
                  pl.BlockSpec((CHUNK, SSD_CONV_CH), rev),
                  pl.BlockSpec((CHUNK, SSD_INNER), rev),
                  pl.BlockSpec((1, SSD_STATE, SSD_INNER), lambda i: (nc - 1 - i, 0, 0)),
                  pl.BlockSpec((4, SSD_CONV_CH), c0), pl.BlockSpec((1, LANE), c0), pl.BlockSpec((1, LANE), c0),
                  pl.BlockSpec((1, SSD_INNER), c0), pl.BlockSpec((1, SSD_INNER), c0),
                  pl.BlockSpec((LANE, SSD_INNER), c0)],
        out_specs=[pl.BlockSpec((CHUNK, 2560), rev), pl.BlockSpec((CHUNK, LANE), rev),
                   pl.BlockSpec((4, SSD_CONV_CH), c0), pl.BlockSpec((1, SSD_CONV_CH), c0),
                   pl.BlockSpec((1, LANE), c0), pl.BlockSpec((1, LANE), c0),
                   pl.BlockSpec((1, SSD_INNER), c0), pl.BlockSpec((1, SSD_INNER), c0)],
        out_shape=[jax.ShapeDtypeStruct((T, 2560), BF16), jax.ShapeDtypeStruct((T, LANE), BF16),
                   jax.ShapeDtypeStruct((4, SSD_CONV_CH), F32), jax.ShapeDtypeStruct((1, SSD_CONV_CH), F32),
                   jax.ShapeDtypeStruct((1, LANE), F32), jax.ShapeDtypeStruct((1, LANE), F32),
                   jax.ShapeDtypeStruct((1, SSD_INNER), F32), jax.ShapeDtypeStruct((1, SSD_INNER), F32)],
        scratch_shapes=[pltpu.VMEM((SSD_STATE, SSD_INNER), F32), pltpu.VMEM((8, SSD_CONV_CH), F32),
                        pltpu.VMEM((CHUNK, SSD_INNER), F32), pltpu.VMEM((CHUNK, SSD_CONV_CH), F32)],
        compiler_params=_params(1),
    )(dyn, proj, proj, dtp, xc, y, sprev, conv_w, dt_bias, a_log, d_e, g_ssd, expand)


def _outproj_fwd(x, y_lru, y_ssd, w_out, g_pm, g_pf):
    T = x.shape[0]
    TT = TP

    def body(x_ref, yl_ref, ys_ref, wo_ref, gpm_ref, gpf_ref, mix_ref, x1_ref, h2_ref):
        mix = _dot(yl_ref[...], wo_ref[0:LRU_W, :]) + _dot(ys_ref[...], wo_ref[LRU_W:2 * LRU_W, :])
        mix_ref[...] = mix
        x1 = x_ref[...] + mix * _rms(mix) * gpm_ref[...]
        x1_ref[...] = x1
        h2_ref[...] = (x1 * _rms(x1) * gpf_ref[...]).astype(BF16)

    tile = pl.BlockSpec((TT, D_MODEL), lambda i: (i, 0))
    vec = pl.BlockSpec((1, D_MODEL), lambda i: (0, 0))
    return pl.pallas_call(
        body, name="outproj_fwd", grid=(T // TT,),
        in_specs=[tile, tile, tile, pl.BlockSpec((2 * LRU_W, D_MODEL), lambda i: (0, 0)), vec, vec],
        out_specs=[tile, tile, tile],
        out_shape=[jax.ShapeDtypeStruct((T, D_MODEL), F32), jax.ShapeDtypeStruct((T, D_MODEL), F32),
                   jax.ShapeDtypeStruct((T, D_MODEL), BF16)],
        compiler_params=_params(1),
    )(x, y_lru, y_ssd, w_out, g_pm, g_pf)


def _ffn_fwd_bwd(x1, h2, target, w_gate, w_up, w_down, g_pf, g_ff):
    T = x1.shape[0]

    def body(x1_ref, h2_ref, tg_ref, wg_hbm, wu_hbm, wd_hbm, gpf_ref, gff_ref,
             dx1_ref, act_ref, df_ref, dgt_ref, dup_ref, dgpf_ref, dgff_ref, loss_ref,
             wg, wu, wd, gt_s, up_s, sem):
        @pl.when(pl.program_id(0) == 0)
        def _():
            cps = [pltpu.make_async_copy(s, d, sem.at[n]) for n, (s, d) in
                   enumerate(((wg_hbm, wg), (wu_hbm, wu), (wd_hbm, wd)))]
            for c in cps:
                c.start()
            for c in cps:
                c.wait()
            for ref in (dgpf_ref, dgff_ref, loss_ref):
                ref[...] = jnp.zeros_like(ref)

        h2 = h2_ref[...]
        f = jnp.zeros((TT, D_MODEL), F32)
        for c in range(D_FF // FF_CHUNK):
            sl = slice(FF_CHUNK * c, FF_CHUNK * (c + 1))
            gt = _dot(h2, wg[:, sl])
            up = _dot(h2, wu[:, sl])
            gt_s[:, sl] = gt
            up_s[:, sl] = up
            act = (gt * _sigmoid(gt) * up).astype(BF16)
            act_ref[:, sl] = act
            f += _dot(act, wd[sl, :])
        x1 = x1_ref[...]
        rnf = _rms(f)
        e = x1 + f * rnf * gff_ref[...] - tg_ref[...]
        part = 0.5 * jnp.sum(jnp.sum(e * e, axis=1, keepdims=True), axis=0, keepdims=True) * (1.0 / D_MODEL)
        lane = lax.broadcasted_iota(jnp.int32, (1, LANE), 1)
        loss_ref[...] += jnp.where(lane == 0, part, 0.0)
        dx2 = e * (1.0 / D_MODEL)
        dgff_ref[...] += jnp.sum(dx2 * f * rnf, axis=0, keepdims=True)
        df = _rms_bwd(dx2 * gff_ref[...], f, rnf).astype(BF16)
        df_ref[...] = df
        dh2 = jnp.zeros((TT, D_MODEL), F32)
        for c in range(D_FF // FF_CHUNK):
            sl = slice(FF_CHUNK * c, FF_CHUNK * (c + 1))
            dact = _dot(df, wd[sl, :], NT)
            gt = gt_s[:, sl]
            up = up_s[:, sl]
            sg = _sigmoid(gt)
            dgt = (dact * up * (sg * (1.0 + gt * (1.0 - sg)))).astype(BF16)
            dup = (dact * gt * sg).astype(BF16)
            dgt_ref[:, sl] = dgt
            dup_ref[:, sl] = dup
            dh2 += _dot(dgt, wg[:, sl], NT) + _dot(dup, wu[:, sl], NT)
        rn2 = _rms(x1)
        dgpf_ref[...] += jnp.sum(dh2 * x1 * rn2, axis=0, keepdims=True)
        dx1_ref[...] = dx2 + _rms_bwd(dh2 * gpf_ref[...], x1, rn2)

    tile = pl.BlockSpec((TT, D_MODEL), lambda i: (i, 0))
    wide = pl.BlockSpec((TT, D_FF), lambda i: (i, 0))
    vec = pl.BlockSpec((1, D_MODEL), lambda i: (0, 0))
    hbm = pl.BlockSpec(memory_space=pl.ANY)
    return pl.pallas_call(
        body, name="ffn_fwd_bwd", grid=(T // TT,),
        in_specs=[tile, tile, tile, hbm, hbm, hbm, vec, vec],
        out_specs=[tile, wide, tile, wide, wide, vec, vec, pl.BlockSpec((1, LANE), lambda i: (0, 0))],
        out_shape=[jax.ShapeDtypeStruct((T, D_MODEL), F32), jax.ShapeDtypeStruct((T, D_FF), BF16),
                   jax.ShapeDtypeStruct((T, D_MODEL), BF16), jax.ShapeDtypeStruct((T, D_FF), BF16),
                   jax.ShapeDtypeStruct((T, D_FF), BF16), jax.ShapeDtypeStruct((1, D_MODEL), F32),
                   jax.ShapeDtypeStruct((1, D_MODEL), F32), jax.ShapeDtypeStruct((1, LANE), F32)],
        scratch_shapes=[pltpu.VMEM((D_MODEL, D_FF), BF16), pltpu.VMEM((D_MODEL, D_FF), BF16),
                        pltpu.VMEM((D_FF, D_MODEL), BF16), pltpu.VMEM((TT, D_FF), F32),
                        pltpu.VMEM((TT, D_FF), F32), pltpu.SemaphoreType.DMA((3,))],
        compiler_params=_params(1),
    )(x1, h2, target, w_gate, w_up, w_down, g_pf, g_ff)


def _outproj_bwd(dx1, mix, w_out, g_pm):
    T = dx1.shape[0]
    TT = TP

    def body(dx1_ref, mix_ref, wo_ref, gpm_ref, dyl_ref, dys_ref, dmix_ref, dg_ref):
        @pl.when(pl.program_id(0) == 0)
        def _():
            dg_ref[...] = jnp.zeros_like(dg_ref)

        mix = mix_ref[...]
        rn = _rms(mix)
        dx1v = dx1_ref[...]
        dg_ref[...] += jnp.sum(dx1v * mix * rn, axis=0, keepdims=True)
        dmix = _rms_bwd(dx1v * gpm_ref[...], mix, rn).astype(BF16)
        dmix_ref[...] = dmix
        dyl_ref[...] = _dot(dmix, wo_ref[0:LRU_W, :], NT)
        dys_ref[...] = _dot(dmix, wo_ref[LRU_W:2 * LRU_W, :], NT)

    tile = pl.BlockSpec((TT, D_MODEL), lambda i: (i, 0))
    vec = pl.BlockSpec((1, D_MODEL), lambda i: (0, 0))
    return pl.pallas_call(
        body, name="outproj_bwd", grid=(T // TT,),
        in_specs=[tile, tile, pl.BlockSpec((2 * LRU_W, D_MODEL), lambda i: (0, 0)), vec],
        out_specs=[tile, tile, tile, vec],
        out_shape=[jax.ShapeDtypeStruct((T, D_MODEL), F32), jax.ShapeDtypeStruct((T, D_MODEL), F32),
                   jax.ShapeDtypeStruct((T, D_MODEL), BF16), jax.ShapeDtypeStruct((1, D_MODEL), F32)],
        compiler_params=_params(1),
    )(dx1, mix, w_out, g_pm)


def _tn_matmul(a, bs, name, tk=512):
    T, M = a.shape
    nk = T // tk
    nb = len(bs)

    def body(*refs):
        a_ref, b_refs, o_refs, accs = refs[0], refs[1:1 + nb], refs[1 + nb:1 + 2 * nb], refs[1 + 2 * nb:]
        k = pl.program_id(0)

        @pl.when(k == 0)
        def _():
            for acc in accs:
                acc[...] = jnp.zeros_like(acc)

        av = a_ref[...].astype(BF16)
        for b_ref, acc in zip(b_refs, accs):
            acc[...] += _dot(av, b_ref[...], TN)

        @pl.when(k == nk - 1)
        def _():
            for o_ref, acc in zip(o_refs, accs):
                o_ref[...] = acc[...].astype(BF16)

    return pl.pallas_call(
        body, name=name, grid=(nk,),
        in_specs=[pl.BlockSpec((tk, M), lambda k: (k, 0))]
                 + [pl.BlockSpec((tk, b.shape[1]), lambda k: (k, 0)) for b in bs],
        out_specs=[pl.BlockSpec((M, b.shape[1]), lambda k: (0, 0)) for b in bs],
        out_shape=[jax.ShapeDtypeStruct((M, b.shape[1]), BF16) for b in bs],
        scratch_shapes=[pltpu.VMEM((M, b.shape[1]), F32) for b in bs],
        compiler_params=_params(1),
    )(a, *bs)


def _tn_matmul_slabs(a, bs, name, slab, tk=512):
    T, M = a.shape
    nk = T // tk
    nb = len(bs)
    offs = [sum(b.shape[1] for b in bs[:i]) for i in range(nb + 1)]

    def body(*refs):
        a_ref, b_refs, o_ref, acc = refs[0], refs[1:1 + nb], refs[1 + nb], refs[2 + nb]
        k = pl.program_id(0)

        @pl.when(k == 0)
        def _():
            acc[...] = jnp.zeros_like(acc)

        av = a_ref[...].astype(BF16)
        for i, b_ref in enumerate(b_refs):
            acc[:, offs[i]:offs[i + 1]] += _dot(av, b_ref[...], TN)

        @pl.when(k == nk - 1)
        def _():
            for j in range(N_DEV):
                o_ref[j] = acc[:, slab * j:slab * (j + 1)].astype(BF16)

    return pl.pallas_call(
        body, name=name, grid=(nk,),
        in_specs=[pl.BlockSpec((tk, M), lambda k: (k, 0))]
                 + [pl.BlockSpec((tk, b.shape[1]), lambda k: (k, 0)) for b in bs],
        out_specs=pl.BlockSpec((N_DEV, M, slab), lambda k: (0, 0, 0)),
        out_shape=jax.ShapeDtypeStruct((N_DEV, M, slab), BF16),
        scratch_shapes=[pltpu.VMEM((M, offs[-1]), F32)],
        compiler_params=_params(1),
    )(a, *bs)


def _tn_blockdiag(a, b1, b2, name, tk=1024):
    T = a.shape[0]
    tk = min(tk, T)

    def body(a_ref, b1_ref, b2_ref, o1_ref, o2_ref):
        @pl.when(pl.program_id(0) == 0)
        def _():
            o1_ref[...] = jnp.zeros_like(o1_ref)
            o2_ref[...] = jnp.zeros_like(o2_ref)

        for j in range(4):
            sl = slice(256 * j, 256 * (j + 1))
            av = a_ref[:, sl].astype(BF16)
            o1_ref[j] += _dot(av, b1_ref[:, sl], TN)
            o2_ref[j] += _dot(av, b2_ref[:, sl], TN)

    blk = pl.BlockSpec((tk, LRU_W), lambda k: (k, 0))
    out = pl.BlockSpec((4, 256, 256), lambda k: (0, 0, 0))
    return pl.pallas_call(
        body, name=name, grid=(T // tk,),
        in_specs=[blk, blk, blk], out_specs=[out, out],
        out_shape=[jax.ShapeDtypeStruct((4, 256, 256), F32)] * 2,
        compiler_params=_params(1),
    )(a, b1, b2)


def _adamw_update(g, w, m, v):
    nm = ADAM_B1 * m + (1.0 - ADAM_B1) * g
    nv = ADAM_B2 * v + (1.0 - ADAM_B2) * (g * g)
    m_hat = nm / (1.0 - ADAM_B1 ** ADAM_STEP)
    v_hat = nv / (1.0 - ADAM_B2 ** ADAM_STEP)
    return -ADAM_LR * (m_hat / (jnp.sqrt(v_hat) + ADAM_EPS) + ADAM_WD * w), nm, nv


def _adamw_small(parts, params):
    n = len(params)

    def body(*refs):
        p_ref, wmv = refs[0], refs[1:1 + 3 * n]
        gs_ref, outs = refs[1 + 3 * n], refs[2 + 3 * n:]
        g = p_ref[0]
        for k in range(1, N_DEV):
            g = g + p_ref[k]
        gs_ref[...] = g
        for i, (row, w, _, _) in enumerate(params):
            rows, width = w.shape
            if width <= 1024:
                gp = gs_ref[row:row + rows, 0:width]
            else:
                gp = jnp.concatenate([gs_ref[row:row + 1, :], gs_ref[row + 1:row + 2, 0:width - 1024]], axis=1)
            d, nm, nv = _adamw_update(gp, wmv[3 * i][...], wmv[3 * i + 1][...], wmv[3 * i + 2][...])
            for ref, val in zip(outs[4 * i:4 * i + 4], (gp, d, nm, nv)):
                ref[...] = val

    full = lambda s: pl.BlockSpec(s, lambda: (0,) * len(s))
    shapes = [w.shape for _, w, _, _ in params]
    return pl.pallas_call(
        body, name="adamw_small",
        in_specs=[full(parts.shape)] + [full(s) for s in shapes for _ in range(3)],
        out_specs=[full(parts.shape[1:])] + [full(s) for s in shapes for _ in range(4)],
        out_shape=[jax.ShapeDtypeStruct(parts.shape[1:], F32)]
                  + [jax.ShapeDtypeStruct(s, F32) for s in shapes for _ in range(4)],
        compiler_params=pltpu.CompilerParams(vmem_limit_bytes=VMEM_LIMIT),
    )(parts, *[a for _, w, m, v in params for a in (w, m, v)])


def _adamw(parts, w, m, v, name, tr):
    P, R, C = parts.shape

    def body(p_ref, w_ref, m_ref, v_ref, g_ref, d_ref, nm_ref, nv_ref):
        g = p_ref[0].astype(F32)
        for k in range(1, P):
            g = g + p_ref[k].astype(F32)
        g_ref[...] = g
        d_ref[...], nm_ref[...], nv_ref[...] = _adamw_update(g, w_ref[...], m_ref[...], v_ref[...])

    blk = pl.BlockSpec((tr, C), lambda i: (i, 0))
    return pl.pallas_call(
        body, name=name, grid=(R // tr,),
        in_specs=[pl.BlockSpec((P, tr, C), lambda i: (0, i, 0)), blk, blk, blk],
        out_specs=[blk, blk, blk, blk],
        out_shape=[jax.ShapeDtypeStruct((R, C), F32)] * 4,
        compiler_params=_params(1),
    )(parts, w, m, v)


def _peer(k):
    x, y, c = lax.axis_index("x"), lax.axis_index("y"), lax.axis_index("c")
    px = x ^ ((k >> 2) & 1)
    py = y ^ ((k >> 1) & 1)
    pc = c ^ (k & 1)
    return (px, py, pc), 4 * px + 2 * py + pc


def _my_block():
    return 4 * lax.axis_index("x") + 2 * lax.axis_index("y") + lax.axis_index("c")


def _all_gather(shards, name):
    n = len(shards)

    def body(*refs):
        ins, outs = refs[:n], refs[n:2 * n]
        send, recv, loc = refs[2 * n:]
        x, y, c = lax.axis_index("x"), lax.axis_index("y"), lax.axis_index("c")
        sibling = (x, y, 1 - c)
        chips = [(1 - x, y), (x, 1 - y), (1 - x, 1 - y)]
        slot = lambda px, py, pc: 4 * px + 2 * py + pc

        def copy(a, k, block, to, src=None):
            dst = outs[a].at[slot(*block)]
            return pltpu.make_async_remote_copy(
                src_ref=dst if src is None else src, dst_ref=dst, send_sem=send.at[a, k], recv_sem=recv.at[a, k],
                device_id=to, device_id_type=pl.DeviceIdType.MESH)

        mine = [pltpu.make_async_copy(ins[a], outs[a].at[slot(x, y, c)], loc.at[a]) for a in range(n)]
        for cp in mine:
            cp.start()
        first = []
        for a in range(n):
            first.append(copy(a, 0, (x, y, c), sibling, src=ins[a]))
            first += [copy(a, 1 + j, (x, y, c), (*chip, c), src=ins[a]) for j, chip in enumerate(chips)]
        for cp in first:
            cp.start()
        passed = []
        for j, chip in enumerate(chips):
            for a in range(n):
                copy(a, 1 + j, (*chip, c), (x, y, c)).wait_recv()
                fwd = copy(a, 4 + j, (*chip, c), sibling)
                fwd.start()
                passed.append(fwd)
        for a in range(n):
            copy(a, 0, sibling, (x, y, c)).wait_recv()
            for j, chip in enumerate(chips):
                copy(a, 4 + j, (*chip, 1 - c), (x, y, c)).wait_recv()
        for cp in first + passed:
            cp.wait_send()
        for cp in mine:
            cp.wait()

    hbm = pl.BlockSpec(memory_space=pl.ANY)
    return pl.pallas_call(
        body, name=name,
        in_specs=[hbm] * n, out_specs=[hbm] * n,
        out_shape=[jax.ShapeDtypeStruct((N_DEV,) + s.shape, s.dtype) for s in shards],
        scratch_shapes=[pltpu.SemaphoreType.DMA((n, N_DEV - 1)), pltpu.SemaphoreType.DMA((n, N_DEV - 1)),
                        pltpu.SemaphoreType.DMA((n,))],
    )(*shards)


_HBM = pl.BlockSpec(memory_space=pltpu.HBM)
_SEM = pl.BlockSpec(memory_space=pltpu.SEMAPHORE)
_EFFECT = pltpu.SideEffectType.DATAFLOW_SIDE_EFFECTING


def _direct_copies(srcs, lands, send, recv, slab_source):
    me = _my_block()
    cps = []
    for k in range(1, N_DEV):
        to, blk = _peer(k)
        for a, (src, land) in enumerate(zip(srcs, lands)):
            if slab_source:
                s, d = src.at[blk], land.at[me]
            elif land.ndim == 3:
                s, d = src, land.at[me]
            else:
                s, d = src, land.at[pl.ds(pl.multiple_of(me * src.shape[0], 16), src.shape[0]), :]
            cps.append(pltpu.make_async_remote_copy(
                src_ref=s, dst_ref=d,
                send_sem=send.at[a * (N_DEV - 1) + k - 1], recv_sem=recv.at[a * (N_DEV - 1) + k - 1],
                device_id=to, device_id_type=pl.DeviceIdType.MESH))
    return cps


def _exchange_start(srcs, name, slab_source, axes=None):
    n = len(srcs)
    if slab_source:
        shapes = [s.shape for s in srcs]
    else:
        shapes = [(N_DEV,) + s.shape if ax == 1 else (N_DEV * s.shape[0], s.shape[1]) for s, ax in zip(srcs, axes)]
    lands = [pltpu.with_memory_space_constraint(lax.empty(sh, s.dtype), pltpu.HBM) for sh, s in zip(shapes, srcs)]

    def body(*refs):
        ins, land_in = refs[:n], refs[n:2 * n]
        send, recv = refs[2 * n], refs[2 * n + 1]
        token = refs[4 * n + 2]
        for cp in _direct_copies(ins, land_in, send, recv, slab_source):
            cp.start()
        token[...] = jnp.zeros_like(token)

    sems = pltpu.SemaphoreType.DMA((n * (N_DEV - 1),))
    res = pl.pallas_call(
        body, name=name,
        out_shape=(sems, sems, *[pltpu.HBM(s.shape, s.dtype) for s in srcs],
                   *[pltpu.HBM(l.shape, l.dtype) for l in lands], jax.ShapeDtypeStruct((8, LANE), F32)),
        in_specs=[_HBM] * (2 * n),
        out_specs=(_SEM, _SEM, *[_HBM] * (2 * n), pl.BlockSpec(memory_space=pltpu.VMEM)),
        input_output_aliases={i: 2 + i for i in range(2 * n)},
        compiler_params=pltpu.CompilerParams(has_side_effects=_EFFECT),
    )(*[pltpu.with_memory_space_constraint(s, pltpu.HBM) for s in srcs], *lands)
    return dict(send=res[0], recv=res[1], srcs=res[2:2 + n], lands=res[2 + n:2 + 2 * n], token=res[-1],
                slab_source=slab_source)


def _exchange_wait(ex, after, name):
    n = len(ex["srcs"])
    slab_source = ex["slab_source"]

    def body(*refs):
        ins, lands = refs[:n], refs[n:2 * n]
        send, recv = refs[2 * n], refs[2 * n + 1]
        for cp in _direct_copies(ins, lands, send, recv, slab_source):
            cp.wait_send()
            cp.wait_recv()

    res = pl.pallas_call(
        body, name=name,
        out_shape=tuple(pltpu.HBM(s.shape, s.dtype) for s in list(ex["srcs"]) + list(ex["lands"])),
        in_specs=[_HBM] * (2 * n) + [_SEM, _SEM, pl.BlockSpec(memory_space=pl.ANY)],
        out_specs=tuple([_HBM] * (2 * n)),
        input_output_aliases={i: i for i in range(2 * n)},
        compiler_params=pltpu.CompilerParams(has_side_effects=_EFFECT),
    )(*ex["srcs"], *ex["lands"], ex["send"], ex["recv"], after)
    me = _my_block()
    out = []
    for src, land in zip(res[:n], res[n:]):
        if slab_source:
            own, at = lax.dynamic_index_in_dim(src, me, 0, keepdims=True), (me, 0, 0)
        elif land.ndim == 3:
            own, at = src[None], (me, 0, 0)
        else:
            own, at = src, (me * src.shape[0], 0)
        out.append(lax.dynamic_update_slice(land, own, at))
    return out


BIG = ("w_in", "w_out", "w_gate", "w_up", "w_down")
BIG_SHARD = {"w_in": (1024, 578), "w_out": (256, 1024), "w_gate": (1024, 352), "w_up": (1024, 352),
             "w_down": (352, 1024)}
BIG_SHARD_AXIS = {"w_in": 1, "w_out": 0, "w_gate": 1, "w_up": 1, "w_down": 0}
BIG_ADAM_ROWS = {"w_in": 256, "w_out": 128, "w_gate": 256, "w_up": 256, "w_down": 176}


def _join(parts, axis):
    if axis == 0:
        return parts.reshape((-1,) + parts.shape[2:])
    return jnp.concatenate([parts[j] for j in range(N_DEV)], axis=1)


def _split(full, axis):
    if axis == 0:
        return full.reshape((N_DEV, full.shape[0] // N_DEV) + full.shape[1:])
    c = full.shape[1] // N_DEV
    return jnp.stack([full[:, c * j:c * (j + 1)] for j in range(N_DEV)])


SMALL = (("lru_wa", 65536), ("lru_wx", 65536), ("pre_mix_norm", 1024), ("lru_conv_w", 4096), ("lru_conv_b", 1024),
         ("lru_ba", 1024), ("lru_bx", 1024), ("lru_lambda", 1024), ("lru_out_norm", 1024), ("ssd_conv_w", 6144),
         ("ssd_conv_b", 1536), ("ssd_dt_bias", 16), ("ssd_a_log", 16), ("ssd_d", 16), ("ssd_out_norm", 1024),
         ("post_mix_norm", 1024), ("pre_ffn_norm", 1024), ("post_ffn_norm", 1024), ("loss", 1))
REPLICATED = tuple(n for n, _ in SMALL if n not in ("lru_conv_w", "ssd_conv_w", "loss"))
SMALL_ROW = {}
for _name, _size in SMALL:
    SMALL_ROW[_name] = (sum(-(-s // 1024) for n, s in SMALL[:len(SMALL_ROW)]), -(-_size // 1024))


def _pack_small(d):
    rows = [jnp.pad(d[n].astype(F32).reshape(-1), (0, SMALL_ROW[n][1] * 1024 - s)).reshape(-1, 1024) for n, s in SMALL]
    used = sum(r.shape[0] for r in rows)
    return jnp.concatenate(rows + [jnp.zeros((SMALL_ROWS - used, 1024), F32)], axis=0)


def _small_entry(p, name):
    row, rows = SMALL_ROW[name]
    return p[row:row + rows].reshape(-1)[:dict(SMALL)[name]]


def _blockdiag4(w):
    on_diag = jnp.eye(4, dtype=w.dtype)[None, :, None, :, None]
    return (w.reshape(4, 4, 64, 1, 64) * on_diag).reshape(4, 256, 256)


def _diag_blocks(g):
    on_diag = jnp.eye(4, dtype=g.dtype)[None, :, None, :, None]
    return jnp.sum(g.reshape(4, 4, 64, 4, 64) * on_diag, axis=3).reshape(16, 64, 64)


def _local_step(x, target, w_in, P, rest_weights, emit, emit_small, start_token=None):
    cut = IN_MAIN - (N_DEV - 1) * (IN_COLS // N_DEV)
    w_main = jnp.concatenate([w_in[j] for j in range(N_DEV - 1)] + [w_in[N_DEV - 1][:, :cut]], axis=1)
    w_dt = jnp.pad(w_in[N_DEV - 1][:, cut:], ((0, 0), (0, LANE - SSD_HEADS)))
    pad16 = lambda v: jnp.pad(v.reshape(1, SSD_HEADS), ((0, 0), (0, LANE - SSD_HEADS)))
    dt_bias, a_log = pad16(P["ssd_dt_bias"]), pad16(P["ssd_a_log"])
    d_e = jnp.repeat(P["ssd_d"].reshape(SSD_HEADS), SSD_HEAD_DIM).reshape(1, SSD_INNER)
    expand = (jnp.arange(LANE)[:, None] == (jnp.arange(SSD_INNER)[None, :] // SSD_HEAD_DIM)).astype(BF16)
    wa_bd = _blockdiag4(P["lru_wa"].astype(BF16))
    wx_bd = _blockdiag4(P["lru_wx"].astype(BF16))
    vec = lambda n: P[n].reshape(1, -1)

    after = lambda v, tok: v if tok is None else v + tok[0:1, 0:1]

    h, proj, dtp = _inproj_fwd(x, after(vec("pre_mix_norm"), start_token), w_main, w_dt)
    lx, hl, y_lru, *gates = _lru_fwd(proj, P["lru_conv_w"], vec("lru_conv_b"), wa_bd, wx_bd, vec("lru_ba"),
                                     vec("lru_bx"), vec("lru_lambda"), vec("lru_out_norm"))
    xc, y, y_ssd, sprev = _ssd_fwd(proj, dtp, P["ssd_conv_w"], vec("ssd_conv_b"), dt_bias, a_log, d_e,
                                   vec("ssd_out_norm"), expand)
    W = rest_weights(y_ssd)
    mix, x1, h2 = _outproj_fwd(x, y_lru, y_ssd, W["w_out"], vec("post_mix_norm"), vec("pre_ffn_norm"))
    dx1, act, df, dgt, dup, dg_pf, dg_ff, loss = _ffn_fwd_bwd(
        x1, h2, target, W["w_gate"], W["w_up"], W["w_down"], vec("pre_ffn_norm"), vec("post_ffn_norm"))
    tok = emit("ffn", {"w_gate": _tn_matmul_slabs(h2, [dgt], "dw_gate", D_FF // N_DEV),
                       "w_up": _tn_matmul_slabs(h2, [dup], "dw_up", D_FF // N_DEV),
                       "w_down": _tn_matmul(act, [df], "dw_down")[0]})
    dy_lru, dy_ssd, dmix, dg_pm = _outproj_bwd(dx1, mix, W["w_out"], after(vec("post_mix_norm"), tok))
    tok = emit("out", {"w_out": jnp.concatenate([_tn_matmul(y_lru, [dmix], "dw_out_lru")[0],
                                                 _tn_matmul(y_ssd, [dmix], "dw_out_ssd")[0]], axis=0)})
    dp_ssd, ddtp, dcw_s, dcb_s, dbias, dA, dD_e, dg_ssd = _ssd_bwd(
        dy_ssd, proj, dtp, xc, y, sprev, P["ssd_conv_w"], dt_bias, a_log, d_e,
        after(vec("ssd_out_norm"), tok), expand)
    dp_lru, dpa, dpx, dcw_l, dcb_l, dba, dbx, dlam, dg_lru = _lru_bwd(
        dy_lru, proj, lx, hl, gates, P["lru_conv_w"], wa_bd, wx_bd, vec("lru_lambda"), vec("lru_out_norm"))
    tok = emit("in", {"w_in": _tn_matmul_slabs(h, [dp_lru, dp_ssd, ddtp], "dw_in", IN_COLS // N_DEV)})

    a_neg = -jnp.exp(P["ssd_a_log"].reshape(SSD_HEADS))
    dwa, dwx = _tn_blockdiag(lx, dpa, dpx, "dw_lru_gates")
    small = {
        "pre_mix_norm": jnp.zeros((1, D_MODEL), F32), "lru_conv_w": dcw_l, "lru_conv_b": dcb_l,
        "lru_wa": _diag_blocks(dwa), "lru_ba": dba,
        "lru_wx": _diag_blocks(dwx), "lru_bx": dbx,
        "lru_lambda": dlam, "lru_out_norm": dg_lru, "ssd_conv_w": dcw_s, "ssd_conv_b": dcb_s,
        "ssd_dt_bias": dbias[0, :SSD_HEADS], "ssd_a_log": dA[0, :SSD_HEADS] * a_neg,
        "ssd_d": jnp.sum(dD_e.reshape(SSD_HEADS, SSD_HEAD_DIM), axis=1), "ssd_out_norm": dg_ssd,
        "post_mix_norm": dg_pm, "pre_ffn_norm": dg_pf, "post_ffn_norm": dg_ff, "loss": loss[0, 0:1],
    }
    tok = after(after(vec("pre_mix_norm"), tok), emit_small(small))
    grad_x, dg_pre = _inproj_bwd(dp_lru, dp_ssd, ddtp, dx1, x, tok, w_main, w_dt)
    return grad_x, dg_pre


def kernel(x, pre_mix_norm, w_in, lru_conv_w, lru_conv_b, lru_wa, lru_ba, lru_wx, lru_bx, lru_lambda, lru_out_norm, ssd_conv_w, ssd_conv_b, ssd_dt_bias, ssd_a_log, ssd_d, ssd_out_norm, w_out, post_mix_norm, pre_ffn_norm, w_gate, w_up, w_down, post_ffn_norm, loss_target, m_pre_mix_norm, m_w_in, m_lru_conv_w, m_lru_conv_b, m_lru_wa, m_lru_ba, m_lru_wx, m_lru_bx, m_lru_lambda, m_lru_out_norm, m_ssd_conv_w, m_ssd_conv_b, m_ssd_dt_bias, m_ssd_a_log, m_ssd_d, m_ssd_out_norm, m_w_out, m_post_mix_norm, m_pre_ffn_norm, m_w_gate, m_w_up, m_w_down, m_post_ffn_norm, v_pre_mix_norm, v_w_in, v_lru_conv_w, v_lru_conv_b, v_lru_wa, v_lru_ba, v_lru_wx, v_lru_bx, v_lru_lambda, v_lru_out_norm, v_ssd_conv_w, v_ssd_conv_b, v_ssd_dt_bias, v_ssd_a_log, v_ssd_d, v_ssd_out_norm, v_w_out, v_post_mix_norm, v_pre_ffn_norm, v_w_gate, v_w_up, v_w_down, v_post_ffn_norm):
    a = dict(locals())
    names = [n for n, _ in SMALL if n != "loss"] + list(BIG)
    w = {n: a[n][0] for n in names}
    m = {n: a["m_" + n][0] for n in names}
    v = {n: a["v_" + n][0] for n in names}

    cpack = jnp.concatenate([w["lru_conv_w"], w["ssd_conv_w"], jnp.zeros((4, 64), F32)], axis=1)
    cpack = jnp.pad(cpack, ((0, 4), (0, 0)))
    g_in, cg = _all_gather([w["w_in"].astype(BF16), cpack], "all_gather_w_in")
    P = {n: w[n] for n in REPLICATED}
    P["lru_conv_w"] = _join(cg[:, 0:4, 0:128], 1)
    P["ssd_conv_w"] = _join(cg[:, 0:4, 128:320], 1)

    rest = [n for n in BIG if n != "w_in"]
    zero = jnp.minimum(jnp.abs(cg[0, 0, 0]), 0.0)
    ex_w = _exchange_start([(w[n] + zero).astype(BF16) for n in rest], "weights_start", slab_source=False,
                           axes=[BIG_SHARD_AXIS[n] for n in rest])

    def rest_weights(after):
        lands = _exchange_wait(ex_w, after, "weights_wait")
        return {n: _join(p, 1) if p.ndim == 3 else p for n, p in zip(rest, lands)}

    pending = []

    def emit(group, grads):
        ex = _exchange_start([g if g.ndim == 3 else _split(g, BIG_SHARD_AXIS[n]) for n, g in grads.items()],
                             "grads_start_" + group, slab_source=True)
        pending.append((group, list(grads), ex))
        return ex["token"]

    def emit_small(small):
        ex = _exchange_start([_pack_small(small)], "small_start", slab_source=False, axes=[0])
        pending.append(("small", None, ex))
        return ex["token"]

    grad_x, dg_pre = _local_step(x[0], loss_target[0], g_in, P, rest_weights, emit, emit_small, ex_w["token"])

    ex_small = pending.pop()[2]
    ex_pre = _exchange_start([jnp.pad(dg_pre, ((0, 7), (0, 0)))], "pre_mix_norm_start", slab_source=False, axes=[0])

    outs = {}
    done = ex_pre["token"]
    for group, group_names, ex in pending:
        for n, parts in zip(group_names, _exchange_wait(ex, done, "grads_wait_" + group)):
            res = _adamw(parts, w[n], m[n], v[n], "adamw_" + n, BIG_ADAM_ROWS[n])
            done = res[0]
            for kind, r in zip(("grad", "delta", "new_m", "new_v"), res):
                outs[kind + "_" + n] = r

    got_pre = _exchange_wait(ex_pre, done, "pre_mix_norm_wait")[0].reshape(N_DEV, 8, 1024)
    got_small = _exchange_wait(ex_small, got_pre, "small_wait")[0].reshape(N_DEV, SMALL_ROWS, 1024)
    row = SMALL_ROW["pre_mix_norm"][0]
    got_small = got_small.at[:, row:row + 1, :].set(got_pre[:, 0:1, :])

    two_d = lambda t: t.reshape(-1, 1024) if t.ndim == 3 else t.reshape(1, -1)
    res = _adamw_small(got_small, [(SMALL_ROW[n][0], two_d(w[n]), two_d(m[n]), two_d(v[n])) for n in REPLICATED])
    g_pack = res[0]
    for i, n in enumerate(REPLICATED):
        for kind, r in zip(("grad", "delta", "new_m", "new_v"), res[1 + 4 * i:5 + 4 * i]):
            outs[kind + "_" + n] = r.reshape(w[n].shape)

    me = _my_block()
    gl = lax.dynamic_slice(_small_entry(g_pack, "lru_conv_w").reshape(4, LRU_W), (0, me * 128), (4, 128))
    gs = lax.dynamic_slice(_small_entry(g_pack, "ssd_conv_w").reshape(4, SSD_CONV_CH), (0, me * 192), (4, 192))
    cat = lambda d: jnp.pad(jnp.concatenate([d["lru_conv_w"], d["ssd_conv_w"]], axis=1), ((0, 4), (0, 64)))
    res = _adamw(cat({"lru_conv_w": gl, "ssd_conv_w": gs})[None], cat(w), cat(m), cat(v), "adamw_conv", 8)
    for kind, r in zip(("grad", "delta", "new_m", "new_v"), res):
        outs[kind + "_lru_conv_w"] = r[0:4, 0:128]
        outs[kind + "_ssd_conv_w"] = r[0:4, 128:320]

    order = ["pre_mix_norm", "w_in", "lru_conv_w", "lru_conv_b", "lru_wa", "lru_ba", "lru_wx", "lru_bx", "lru_lambda",
             "lru_out_norm", "ssd_conv_w", "ssd_conv_b", "ssd_dt_bias", "ssd_a_log", "ssd_d", "ssd_out_norm", "w_out",
             "post_mix_norm", "pre_ffn_norm", "w_gate", "w_up", "w_down", "post_ffn_norm"]
    result = [_small_entry(g_pack, "loss").reshape(()), grad_x[None]]
    for kind in ("grad", "delta", "new_m", "new_v"):
        result += [outs[kind + "_" + n][None] for n in order]
    return tuple(result)
```

```python
import functools

import jax
import jax.numpy as jnp
from jax import lax
from jax.experimental import pallas as pl
from jax.experimental.pallas import tpu as pltpu

F32 = jnp.float32
BF16 = jnp.bfloat16
EPS = 1e-6
N_DEV = 8
D_MODEL = 1024
LRU_W = 1024
SSD_INNER = 1024
SSD_HEADS = 16
SSD_HEAD_DIM = 64
SSD_STATE = 128
SSD_GROUPS = 2
SSD_CONV_CH = 1536
CHUNK = 128
D_FF = 2816
FF_CHUNK = 2816
IN_MAIN = 4608
IN_COLS = 4624
LANE = 128
TT = 256
TP = 512
VMEM_LIMIT = 56 * 1024 * 1024
ADAM_LR, ADAM_B1, ADAM_B2, ADAM_EPS, ADAM_WD, ADAM_STEP = 0.001, 0.9, 0.999, 1e-08, 0.01, 10
SMALL_ROWS = 160

NT = (((1,), (1,)), ((), ()))
TN = (((0,), (0,)), ((), ()))


def _params(n_grid):
    return pltpu.CompilerParams(dimension_semantics=("arbitrary",) * n_grid, vmem_limit_bytes=VMEM_LIMIT)


def _dot(a, b, dims=None, precision=None):
    if dims is None:
        return jnp.dot(a, b, preferred_element_type=F32, precision=precision)
    return lax.dot_general(a, b, dims, preferred_element_type=F32, precision=precision)


def _split_bf16(x, terms):
    out = []
    for _ in range(terms - 1):
        p = x.astype(BF16)
        out.append(p)
        x = x - p.astype(F32)
    return out + [x.astype(BF16)]


def _dot_sel(x, sel, dims=None, terms=2, sel_first=False):
    parts = [_dot(sel, p, dims) if sel_first else _dot(p, sel, dims) for p in _split_bf16(x, terms)]
    return functools.reduce(lambda a, b: a + b, parts)


def _sigmoid(x):
    return 0.5 * jnp.tanh(0.5 * x) + 0.5


def _softplus(x):
    e = jnp.exp(-jnp.abs(x))
    l1p = jnp.where(e < 1e-3, e * (1.0 - e * (0.5 - e * (1.0 / 3.0))), jnp.log(1.0 + e))
    return jnp.maximum(x, 0.0) + l1p


def _neg_expm1(x):
    series = -x * (1.0 + x * (0.5 + x * (1.0 / 6.0 + x * (1.0 / 24.0))))
    return jnp.where(x > -0.01, series, 1.0 - jnp.exp(x))


_GELU_C = 0.7978845608028654


def _gelu(x):
    t = jnp.tanh(_GELU_C * (x + 0.044715 * x * x * x))
    return 0.5 * x * (1.0 + t), t


def _gelu_grad(x, t):
    return 0.5 * (1.0 + t) + 0.5 * x * (1.0 - t * t) * _GELU_C * (1.0 + 3.0 * 0.044715 * x * x)


def _rms(x):
    return lax.rsqrt(jnp.mean(x * x, axis=-1, keepdims=True) + EPS)


def _rms_bwd(dyn, x, rn):
    return rn * dyn - x * (rn * rn * rn) * jnp.mean(dyn * x, axis=-1, keepdims=True)


def _row(x, r):
    idx = lax.broadcasted_iota(jnp.int32, x.shape, 0)
    return jnp.sum(jnp.where(idx == r, x, 0.0), axis=0, keepdims=True)


def _shift_down(cur, prev8, j):
    s = pltpu.roll(cur, j, 0)
    p = pltpu.roll(prev8, j, 0)
    r8 = lax.broadcasted_iota(jnp.int32, prev8.shape, 0)
    top = jnp.where(r8 < j, p, s[0:8])
    return jnp.concatenate([top, s[8:]], axis=0)


def _shift_up(cur, next8, j):
    n = cur.shape[0]
    s = pltpu.roll(cur, n - j, 0)
    p = pltpu.roll(next8, 8 - j, 0)
    r8 = lax.broadcasted_iota(jnp.int32, next8.shape, 0)
    bot = jnp.where(r8 >= 8 - j, p, s[n - 8:n])
    return jnp.concatenate([s[:n - 8], bot], axis=0)


def _scan_fwd(a, u):
    n = a.shape[0]
    row = lax.broadcasted_iota(jnp.int32, a.shape, 0)
    k = 1
    while k < n:
        ok = row >= k
        a_s = jnp.where(ok, pltpu.roll(a, k, 0), 1.0)
        u_s = jnp.where(ok, pltpu.roll(u, k, 0), 0.0)
        u = a * u_s + u
        a = a * a_s
        k *= 2
    return a, u


def _scan_bwd(b, d):
    n = b.shape[0]
    row = lax.broadcasted_iota(jnp.int32, b.shape, 0)
    k = 1
    while k < n:
        ok = row < n - k
        b_s = jnp.where(ok, pltpu.roll(b, n - k, 0), 1.0)
        d_s = jnp.where(ok, pltpu.roll(d, n - k, 0), 0.0)
        d = b * d_s + d
        b = b * b_s
        k *= 2
    return b, d


def _scan_tile(a, u, carry, a_s, u_s, reverse):
    n, c = a.shape
    groups = n // 8
    r8 = lax.broadcasted_iota(jnp.int32, a.shape, 0) & 7
    in_group = lambda x, k: pltpu.roll(x.reshape(groups, 8, c), k, 1).reshape(n, c)
    for k in (1, 2, 4):
        ok = (r8 < 8 - k) if reverse else (r8 >= k)
        shift = 8 - k if reverse else k
        a_n = jnp.where(ok, in_group(a, shift), 1.0)
        u_n = jnp.where(ok, in_group(u, shift), 0.0)
        u = a * u_n + u
        a = a * a_n
    nl = c // LANE
    for j in range(nl):
        a_s[j] = a[:, LANE * j:LANE * (j + 1)]
        u_s[j] = u[:, LANE * j:LANE * (j + 1)]
    end = 0 if reverse else 7
    ends = lambda ref, j: ref[pl.ds(j, 1), pl.ds(end, groups, stride=8), :].reshape(groups, LANE)
    ga = jnp.concatenate([ends(a_s, j) for j in range(nl)], axis=1)
    gu = jnp.concatenate([ends(u_s, j) for j in range(nl)], axis=1)
    gacc, gh = (_scan_bwd if reverse else _scan_fwd)(ga, gu)
    gh = gh + gacc * carry
    grow = lax.broadcasted_iota(jnp.int32, gh.shape, 0)
    if reverse:
        cin = jnp.where(grow == groups - 1, carry, pltpu.roll(gh, groups - 1, 0))
    else:
        cin = jnp.where(grow == 0, carry, pltpu.roll(gh, 1, 0))
    spread = ((lax.broadcasted_iota(jnp.int32, (n, LANE), 0) >> 3)
              == lax.broadcasted_iota(jnp.int32, (n, LANE), 1)).astype(BF16)
    cin = jnp.concatenate([cin, jnp.zeros((LANE - groups, c), F32)], axis=0)
    return u + a * _dot_sel(cin, spread, terms=3, sel_first=True)


def _inproj_fwd(x, g_pre, w_main, w_dt):
    T = x.shape[0]
    TT = TP

    def body(x_ref, g_ref, wm_hbm, wd_hbm, h_ref, proj_ref, dtp_ref, wm, wd, sem):
        @pl.when(pl.program_id(0) == 0)
        def _():
            c1 = pltpu.make_async_copy(wm_hbm, wm, sem.at[0])
            c2 = pltpu.make_async_copy(wd_hbm, wd, sem.at[1])
            c1.start()
            c2.start()
            c1.wait()
            c2.wait()

        xv = x_ref[...]
        h = (xv * _rms(xv) * g_ref[...]).astype(BF16)
        h_ref[...] = h
        proj_ref[...] = _dot(h, wm[...])
        dtp_ref[...] = _dot(h, wd[...])

    return pl.pallas_call(
        body, name="inproj_fwd", grid=(T // TT,),
        in_specs=[pl.BlockSpec((TT, D_MODEL), lambda i: (i, 0)),
                  pl.BlockSpec((1, D_MODEL), lambda i: (0, 0)),
                  pl.BlockSpec(memory_space=pl.ANY), pl.BlockSpec(memory_space=pl.ANY)],
        out_specs=[pl.BlockSpec((TT, D_MODEL), lambda i: (i, 0)),
                   pl.BlockSpec((TT, IN_MAIN), lambda i: (i, 0)),
                   pl.BlockSpec((TT, LANE), lambda i: (i, 0))],
        out_shape=[jax.ShapeDtypeStruct((T, D_MODEL), BF16), jax.ShapeDtypeStruct((T, IN_MAIN), F32),
                   jax.ShapeDtypeStruct((T, LANE), F32)],
        scratch_shapes=[pltpu.VMEM((D_MODEL, IN_MAIN), BF16), pltpu.VMEM((D_MODEL, LANE), BF16),
                        pltpu.SemaphoreType.DMA((2,))],
        compiler_params=_params(1),
    )(x, g_pre, w_main, w_dt)


def _inproj_bwd(dp_lru, dp_ssd, ddtp, dx1, x, g_pre, w_main, w_dt):
    T = x.shape[0]
    TT = TP

    def body(dl_ref, ds_ref, dd_ref, dx1_ref, x_ref, g_ref, wm_hbm, wd_hbm, gx_ref, dg_ref, wm, wd, sem):
        @pl.when(pl.program_id(0) == 0)
        def _():
            c1 = pltpu.make_async_copy(wm_hbm, wm, sem.at[0])
            c2 = pltpu.make_async_copy(wd_hbm, wd, sem.at[1])
            c1.start()
            c2.start()
            c1.wait()
            c2.wait()
            dg_ref[...] = jnp.zeros_like(dg_ref)

        dh = _dot(dl_ref[...], wm[:, 0:2048], NT)
        dh += _dot(ds_ref[...], wm[:, 2048:IN_MAIN], NT)
        dh += _dot(dd_ref[...], wd[...], NT)
        xv = x_ref[...]
        rn = _rms(xv)
        dg_ref[...] += jnp.sum(dh * xv * rn, axis=0, keepdims=True)
        gx_ref[...] = dx1_ref[...] + _rms_bwd(dh * g_ref[...], xv, rn)

    return pl.pallas_call(
        body, name="inproj_bwd", grid=(T // TT,),
        in_specs=[pl.BlockSpec((TT, 2048), lambda i: (i, 0)),
                  pl.BlockSpec((TT, 2560), lambda i: (i, 0)),
                  pl.BlockSpec((TT, LANE), lambda i: (i, 0)),
                  pl.BlockSpec((TT, D_MODEL), lambda i: (i, 0)),
                  pl.BlockSpec((TT, D_MODEL), lambda i: (i, 0)),
                  pl.BlockSpec((1, D_MODEL), lambda i: (0, 0)),
                  pl.BlockSpec(memory_space=pl.ANY), pl.BlockSpec(memory_space=pl.ANY)],
        out_specs=[pl.BlockSpec((TT, D_MODEL), lambda i: (i, 0)),
                   pl.BlockSpec((1, D_MODEL), lambda i: (0, 0))],
        out_shape=[jax.ShapeDtypeStruct((T, D_MODEL), F32), jax.ShapeDtypeStruct((1, D_MODEL), F32)],
        scratch_shapes=[pltpu.VMEM((D_MODEL, IN_MAIN), BF16), pltpu.VMEM((D_MODEL, LANE), BF16),
                        pltpu.SemaphoreType.DMA((2,))],
        compiler_params=_params(1),
    )(dp_lru, dp_ssd, ddtp, dx1, x, g_pre, w_main, w_dt)


def _lru_gates(lx, wa_ref, wx_ref, ba, bx, lam):
    lxb = lx.astype(BF16)
    pa = jnp.concatenate([_dot(lxb[:, 256 * k:256 * (k + 1)], wa_ref[k]) for k in range(4)], axis=1) + ba
    px = jnp.concatenate([_dot(lxb[:, 256 * k:256 * (k + 1)], wx_ref[k]) for k in range(4)], axis=1) + bx
    r = _sigmoid(pa)
    ig = _sigmoid(px)
    sp = _softplus(-lam)
    log_a = -8.0 * r * sp
    a = jnp.exp(log_a)
    mult = jnp.sqrt(_neg_expm1(2.0 * log_a))
    return r, ig, sp, a, mult


def _lru_fwd(proj, conv_w, conv_b, wa_bd, wx_bd, ba, bx, lam, g_lru):
    T = proj.shape[0]

    def body(cx_ref, gate_ref, cw_ref, cb_ref, wa_ref, wx_ref, ba_ref, bx_ref, lam_ref, g_ref,
             lx_ref, hl_ref, y_ref, r_ref, ig_ref, a_ref, mult_ref, tail, hcar, sa, su):
        @pl.when(pl.program_id(0) == 0)
        def _():
            tail[...] = jnp.zeros_like(tail)
            hcar[...] = jnp.zeros_like(hcar)

        cx = cx_ref[...]
        prev8 = tail[...]
        lx = cb_ref[...] + cw_ref[3:4, :] * cx
        for j in range(1, 4):
            lx += cw_ref[3 - j:4 - j, :] * _shift_down(cx, prev8, j)
        tail[...] = cx[TT - 8:TT]
        lx_ref[...] = lx
        r, ig, sp, a, mult = _lru_gates(lx, wa_ref, wx_ref, ba_ref[...], bx_ref[...], lam_ref[...])
        r_ref[...] = r
        ig_ref[...] = ig
        a_ref[...] = a
        mult_ref[...] = mult
        h = _scan_tile(a, mult * (ig * lx), hcar[...], sa, su, reverse=False)
        hl_ref[...] = h
        hcar[...] = hl_ref[TT - 1:TT, :]
        ge, _ = _gelu(gate_ref[...])
        p = h * ge
        y_ref[...] = (p * _rms(p) * g_ref[...]).astype(BF16)

    vec = pl.BlockSpec((1, LRU_W), lambda i: (0, 0))
    bd = pl.BlockSpec((4, 256, 256), lambda i: (0, 0, 0))
    tile = pl.BlockSpec((TT, LRU_W), lambda i: (i, 0))
    f32 = jax.ShapeDtypeStruct((T, LRU_W), F32)
    return pl.pallas_call(
        body, name="lru_fwd", grid=(T // TT,),
        in_specs=[tile, pl.BlockSpec((TT, LRU_W), lambda i: (i, 1)),
                  pl.BlockSpec((4, LRU_W), lambda i: (0, 0)), vec, bd, bd, vec, vec, vec, vec],
        out_specs=[tile] * 7,
        out_shape=[f32, f32, jax.ShapeDtypeStruct((T, LRU_W), BF16), f32, f32, f32, f32],
        scratch_shapes=[pltpu.VMEM((8, LRU_W), F32), pltpu.VMEM((1, LRU_W), F32)]
                       + [pltpu.VMEM((LRU_W // LANE, TT, LANE), F32)] * 2,
        compiler_params=_params(1),
    )(proj, proj, conv_w, conv_b, wa_bd, wx_bd, ba, bx, lam, g_lru)


def _lru_bwd(dy, proj, lx, hl, gates, conv_w, wa_bd, wx_bd, lam, g_lru):
    T = proj.shape[0]
    nt = T // TT

    def body(dy_ref, cx_ref, gate_ref, lx_ref, hl_ref, halo_ref, r_ref, ig_ref, a_ref, mult_ref, cw_ref, wa_ref,
             wx_ref, lam_ref, g_ref, dp_ref, dpa_ref, dpx_ref, dcw_ref, dcb_ref, dba_ref, dbx_ref, dlam_ref, dg_ref,
             gcar, acar, head, sa, su):
        i = pl.program_id(0)

        @pl.when(i == 0)
        def _():
            gcar[...] = jnp.zeros_like(gcar)
            acar[...] = jnp.zeros_like(acar)
            head[...] = jnp.zeros_like(head)
            for ref in (dcw_ref, dcb_ref, dba_ref, dbx_ref, dlam_ref, dg_ref):
                ref[...] = jnp.zeros_like(ref)

        lx = lx_ref[...]
        h = hl_ref[...]
        gate = gate_ref[...]
        cx = cx_ref[...]
        lam = lam_ref[...]
        r, ig, a, mult = r_ref[...], ig_ref[...], a_ref[...], mult_ref[...]
        sp = _softplus(-lam)
        ge, th = _gelu(gate)
        p = h * ge
        rn = _rms(p)
        dyv = dy_ref[...]
        dg_ref[...] += jnp.sum(dyv * p * rn, axis=0, keepdims=True)
        dp = _rms_bwd(dyv * g_ref[...], p, rn)
        dp_ref[:, LRU_W:2 * LRU_W] = (dp * h * _gelu_grad(gate, th)).astype(BF16)
        dh = dp * ge
        row = lax.broadcasted_iota(jnp.int32, a.shape, 0)
        b = jnp.where(row == TT - 1, acar[...], pltpu.roll(a, TT - 1, 0))
        g = _scan_tile(b, dh, gcar[...], sa, su, reverse=True)
        gcar[...] = _row(g[0:8], 0)
        acar[...] = _row(a[0:8], 0)
        h_last_prev = halo_ref[7:8, :] * (i < nt - 1).astype(F32)
        hprev = jnp.where(row == 0, h_last_prev, pltpu.roll(h, 1, 0))
        da = g * hprev
        dm2 = (g * (ig * lx)) * 0.5 / mult
        dlog_a = da * a - 2.0 * a * a * dm2
        dlam_ref[...] += jnp.sum(-8.0 * r * dlog_a, axis=0, keepdims=True) * (-_sigmoid(-lam))
        dpa = (-8.0 * sp * dlog_a) * r * (1.0 - r)
        dpx = (g * mult * lx) * ig * (1.0 - ig)
        dba_ref[...] += jnp.sum(dpa, axis=0, keepdims=True)
        dbx_ref[...] += jnp.sum(dpx, axis=0, keepdims=True)
        dpab = dpa.astype(BF16)
        dpxb = dpx.astype(BF16)
        dpa_ref[...] = dpab
        dpx_ref[...] = dpxb
        dlx = g * mult * ig + jnp.concatenate(
            [_dot(dpab[:, 256 * k:256 * (k + 1)], wa_ref[k], NT) + _dot(dpxb[:, 256 * k:256 * (k + 1)], wx_ref[k], NT)
             for k in range(4)], axis=1)
        nxt = head[...]
        dcb_ref[...] += jnp.sum(dlx, axis=0, keepdims=True)
        dcx = cw_ref[3:4, :] * dlx
        dcw_ref[3:4, :] += jnp.sum(cx * dlx, axis=0, keepdims=True)
        for j in range(1, 4):
            sh = _shift_up(dlx, nxt, j)
            dcx += cw_ref[3 - j:4 - j, :] * sh
            dcw_ref[3 - j:4 - j, :] += jnp.sum(cx * sh, axis=0, keepdims=True)
        head[...] = dlx[0:8]
        dp_ref[:, 0:LRU_W] = dcx.astype(BF16)

    rev = lambda i: (nt - 1 - i, 0)
    vec = pl.BlockSpec((1, LRU_W), lambda i: (0, 0))
    bd = pl.BlockSpec((4, 256, 256), lambda i: (0, 0, 0))
    tile = pl.BlockSpec((TT, LRU_W), rev)
    halo = pl.BlockSpec((8, LRU_W), lambda i: (jnp.maximum((nt - 1 - i) * (TT // 8) - 1, 0), 0))
    cw = pl.BlockSpec((4, LRU_W), lambda i: (0, 0))
    return pl.pallas_call(
        body, name="lru_bwd", grid=(nt,),
        in_specs=[tile, tile, pl.BlockSpec((TT, LRU_W), lambda i: (nt - 1 - i, 1)), tile, tile, halo,
                  tile, tile, tile, tile, cw, bd, bd, vec, vec],
        out_specs=[pl.BlockSpec((TT, 2 * LRU_W), rev), tile, tile, cw, vec, vec, vec, vec, vec],
        out_shape=[jax.ShapeDtypeStruct((T, 2 * LRU_W), BF16), jax.ShapeDtypeStruct((T, LRU_W), BF16),
                   jax.ShapeDtypeStruct((T, LRU_W), BF16), jax.ShapeDtypeStruct((4, LRU_W), F32)]
                  + [jax.ShapeDtypeStruct((1, LRU_W), F32)] * 5,
        scratch_shapes=[pltpu.VMEM((1, LRU_W), F32), pltpu.VMEM((1, LRU_W), F32), pltpu.VMEM((8, LRU_W), F32)]
                       + [pltpu.VMEM((LRU_W // LANE, TT, LANE), F32)] * 2,
        compiler_params=_params(1),
    )(dy, proj, proj, lx, hl, hl, *gates, conv_w, wa_bd, wx_bd, lam, g_lru)


def _ssd_chunk_terms(xc, dtp, bias, alog, expand):
    sg = _sigmoid(xc)
    xbc = xc * sg
    pre = dtp + bias
    dt = _softplus(pre)
    A = -jnp.exp(alog)
    ri = lax.broadcasted_iota(jnp.int32, (CHUNK, CHUNK), 0)
    ci = lax.broadcasted_iota(jnp.int32, (CHUNK, CHUNK), 1)
    tril = (ri >= ci).astype(BF16)
    cs = _dot_sel(dt * A, tril, terms=3, sel_first=True)
    cs_last = _row(cs, CHUNK - 1)
    ecs = jnp.exp(cs)
    dec = jnp.exp(cs_last - cs)
    return dict(sg=sg, xbc=xbc, pre=pre, dt=dt, A=A, cs=cs, csT=cs.T, ecs=ecs, dec=dec, ri=ri, ci=ci,
                dt_e=_dot_sel(dt, expand), ecs_e=_dot_sel(ecs, expand), dec_e=_dot_sel(dec, expand))


def _head_lambda(t, h):
    col = jnp.sum(jnp.where(t["ci"] == h, t["cs"], 0.0), axis=1, keepdims=True)
    rowv = jnp.sum(jnp.where(t["ri"] == h, t["csT"], 0.0), axis=0, keepdims=True)
    return jnp.exp(jnp.where(t["ri"] >= t["ci"], col - rowv, -1e30))


def _ssd_fwd(proj, dtp, conv_w, conv_b, dt_bias, a_log, d_e, g_ssd, expand):
    T = proj.shape[0]
    nc = T // CHUNK

    def body(z_ref, xp_ref, dtp_ref, cw_ref, cb_ref, bias_ref, alog_ref, de_ref, g_ref, ex_ref,
             xc_ref, y_ref, yn_ref, sprev_ref, tail, S):
        @pl.when(pl.program_id(0) == 0)
        def _():
            tail[...] = jnp.zeros_like(tail)
            S[...] = jnp.zeros_like(S)

        xp = xp_ref[...]
        prev8 = tail[...]
        xc = cb_ref[...] + cw_ref[3:4, :] * xp
        for j in range(1, 4):
            xc += cw_ref[3 - j:4 - j, :] * _shift_down(xp, prev8, j)
        tail[...] = xp[CHUNK - 8:CHUNK]
        xc_ref[...] = xc
        t = _ssd_chunk_terms(xc, dtp_ref[...], bias_ref[...], alog_ref[...], ex_ref[...])
        xbc = t["xbc"]
        sx = xbc[:, 0:SSD_INNER]
        Bb = xbc[:, SSD_INNER:SSD_INNER + 256].astype(BF16)
        Cb = xbc[:, SSD_INNER + 256:SSD_CONV_CH].astype(BF16)
        X = t["dt_e"] * sx
        lane = lax.broadcasted_iota(jnp.int32, (CHUNK, LANE), 1)
        G = [_dot(Cb[:, 128 * g:128 * (g + 1)], Bb[:, 128 * g:128 * (g + 1)], NT) for g in range(SSD_GROUPS)]
        for k in range(SSD_HEADS // 2):
            Xp = X[:, 128 * k:128 * (k + 1)]
            acc = jnp.zeros((CHUNK, LANE), F32)
            for half in range(2):
                M = (G[k // 4] * _head_lambda(t, 2 * k + half)).astype(BF16)
                Xh = jnp.where((lane >= 64) if half else (lane < 64), Xp, 0.0).astype(BF16)
                acc += _dot(M, Xh)
            y_ref[:, 128 * k:128 * (k + 1)] = acc
        sprev_ref[0] = S[...]
        eL_e = _row(t["ecs_e"], CHUNK - 1)
        Xd = (X * t["dec_e"]).astype(BF16)
        for g in range(SSD_GROUPS):
            sl = slice(512 * g, 512 * (g + 1))
            Sg = S[:, sl]
            y_ref[:, sl] += t["ecs_e"][:, sl] * _dot(Cb[:, 128 * g:128 * (g + 1)], Sg.astype(BF16))
            S[:, sl] = eL_e[:, sl] * Sg + _dot(Bb[:, 128 * g:128 * (g + 1)], Xd[:, sl], TN)
        y = y_ref[...] + de_ref[...] * sx
        y_ref[...] = y
        z = z_ref[...]
        q = y * (z * _sigmoid(z))
        yn_ref[...] = (q * _rms(q) * g_ref[...]).astype(BF16)

    c0 = lambda i: (0, 0)
    return pl.pallas_call(
        body, name="ssd_fwd", grid=(nc,),
        in_specs=[pl.BlockSpec((CHUNK, SSD_INNER), lambda i: (i, 2)),
                  pl.BlockSpec((CHUNK, SSD_CONV_CH), lambda i: (i, 2)),
                  pl.BlockSpec((CHUNK, LANE), lambda i: (i, 0)),
                  pl.BlockSpec((4, SSD_CONV_CH), c0), pl.BlockSpec((1, SSD_CONV_CH), c0),
                  pl.BlockSpec((1, LANE), c0), pl.BlockSpec((1, LANE), c0),
                  pl.BlockSpec((1, SSD_INNER), c0), pl.BlockSpec((1, SSD_INNER), c0),
                  pl.BlockSpec((LANE, SSD_INNER), c0)],
        out_specs=[pl.BlockSpec((CHUNK, SSD_CONV_CH), lambda i: (i, 0)),
                   pl.BlockSpec((CHUNK, SSD_INNER), lambda i: (i, 0)),
                   pl.BlockSpec((CHUNK, SSD_INNER), lambda i: (i, 0)),
                   pl.BlockSpec((1, SSD_STATE, SSD_INNER), lambda i: (i, 0, 0))],
        out_shape=[jax.ShapeDtypeStruct((T, SSD_CONV_CH), F32), jax.ShapeDtypeStruct((T, SSD_INNER), F32),
                   jax.ShapeDtypeStruct((T, SSD_INNER), BF16),
                   jax.ShapeDtypeStruct((nc, SSD_STATE, SSD_INNER), F32)],
        scratch_shapes=[pltpu.VMEM((8, SSD_CONV_CH), F32), pltpu.VMEM((SSD_STATE, SSD_INNER), F32)],
        compiler_params=_params(1),
    )(proj, proj, dtp, conv_w, conv_b, dt_bias, a_log, d_e, g_ssd, expand)


def _ssd_bwd(dyn, proj, dtp, xc, y, sprev, conv_w, dt_bias, a_log, d_e, g_ssd, expand):
    T = proj.shape[0]
    nc = T // CHUNK

    def body(dyn_ref, z_ref, xp_ref, dtp_ref, xc_ref, y_ref, sprev_ref, cw_ref, bias_ref, alog_ref, de_ref,
             g_ref, ex_ref, dp_ref, ddtp_ref, dcw_ref, dcb_ref, dbias_ref, dA_ref, dD_ref, dg_ref,
             dS, head, dX_s, dxbc_s):
        @pl.when(pl.program_id(0) == 0)
        def _():
            dS[...] = jnp.zeros_like(dS)
            head[...] = jnp.zeros_like(head)
            for ref in (dcw_ref, dcb_ref, dbias_ref, dA_ref, dD_ref, dg_ref):
                ref[...] = jnp.zeros_like(ref)

        ex = ex_ref[...]
        xc = xc_ref[...]
        t = _ssd_chunk_terms(xc, dtp_ref[...], bias_ref[...], alog_ref[...], ex)
        ri, ci = t["ri"], t["ci"]
        xbc = t["xbc"]
        sx = xbc[:, 0:SSD_INNER]
        Bb = xbc[:, SSD_INNER:SSD_INNER + 256].astype(BF16)
        Cb = xbc[:, SSD_INNER + 256:SSD_CONV_CH].astype(BF16)
        X = t["dt_e"] * sx
        z = z_ref[...]
        sz = _sigmoid(z)
        siluz = z * sz
        yv = y_ref[...]
        q = yv * siluz
        rn = _rms(q)
        dynv = dyn_ref[...]
        dg_ref[...] += jnp.sum(dynv * q * rn, axis=0, keepdims=True)
        dq = _rms_bwd(dynv * g_ref[...], q, rn)
        dp_ref[:, 0:SSD_INNER] = (dq * yv * (sz * (1.0 + z * (1.0 - sz)))).astype(BF16)
        dY = dq * siluz
        dD_ref[...] += jnp.sum(dY * sx, axis=0, keepdims=True)
        dYb = dY.astype(BF16)
        lane = lax.broadcasted_iota(jnp.int32, (CHUNK, LANE), 1)
        dcs = jnp.zeros((CHUNK, CHUNK), F32)
        dcsT = jnp.zeros((CHUNK, CHUNK), F32)
        Xb = X.astype(BF16)
        for g in range(SSD_GROUPS):
            Bg = Bb[:, 128 * g:128 * (g + 1)]
            Cg = Cb[:, 128 * g:128 * (g + 1)]
            G = _dot(Cg, Bg, NT)
            dGsum = jnp.zeros((CHUNK, CHUNK), F32)
            for k in range(4 * g, 4 * g + 4):
                Xp = Xb[:, 128 * k:128 * (k + 1)]
                dYp = dY[:, 128 * k:128 * (k + 1)]
                dXp = jnp.zeros((CHUNK, LANE), F32)
                for half in range(2):
                    h = 2 * k + half
                    lam = _head_lambda(t, h)
                    M = G * lam
                    dYh = jnp.where((lane >= 64) if half else (lane < 64), dYp, 0.0).astype(BF16)
                    dM = _dot(dYh, Xp, NT)
                    W = dM * M
                    dcs += jnp.where(ci == h, jnp.sum(W, axis=1, keepdims=True), 0.0)
                    dcsT += jnp.where(ri == h, jnp.sum(W, axis=0, keepdims=True), 0.0)
                    dGsum += dM * lam
                    dXp += _dot(M.astype(BF16), dYh, TN)
                dX_s[:, 128 * k:128 * (k + 1)] = dXp
            dGb = dGsum.astype(BF16)
            dxbc_s[:, SSD_INNER + 256 + 128 * g:SSD_INNER + 256 + 128 * (g + 1)] = _dot(dGb, Bg)
            dxbc_s[:, SSD_INNER + 128 * g:SSD_INNER + 128 * (g + 1)] = _dot(dGb, Cg, TN)
        dcs = dcs - dcsT.T
        Sp = sprev_ref[0]
        dSv = dS[...]
        ecs_e, dec_e = t["ecs_e"], t["dec_e"]
        eL_e = _row(ecs_e, CHUNK - 1)
        dYe = dY * ecs_e
        dYeb = dYe.astype(BF16)
        Xd = X * dec_e
        Xdb = Xd.astype(BF16)
        for g in range(SSD_GROUPS):
            sl = slice(512 * g, 512 * (g + 1))
            Bg = Bb[:, 128 * g:128 * (g + 1)]
            Cg = Cb[:, 128 * g:128 * (g + 1)]
            Spb = Sp[:, sl].astype(BF16)
            dSb = dSv[:, sl].astype(BF16)
            CS = _dot(Cg, Spb)
            BS = _dot(Bg, dSb)
            dxbc_s[:, SSD_INNER + 256 + 128 * g:SSD_INNER + 256 + 128 * (g + 1)] += _dot(dYeb[:, sl], Spb, NT)
            dxbc_s[:, SSD_INNER + 128 * g:SSD_INNER + 128 * (g + 1)] += _dot(Xdb[:, sl], dSb, NT)
            dS[:, sl] = eL_e[:, sl] * dSv[:, sl] + _dot(Cg, dYeb[:, sl], TN)
            dX_s[:, sl] += dec_e[:, sl] * BS
            dcs += _dot_sel(dYe[:, sl] * CS, ex[:, sl], NT, terms=1)
            tdec = _dot_sel(X[:, sl] * BS, ex[:, sl], NT, terms=1) * t["dec"]
            dcs -= tdec
            last = jnp.sum(tdec, axis=0, keepdims=True)
            last += jnp.sum(_dot_sel(Sp[:, sl] * dSv[:, sl], ex[:, sl], NT, terms=1), axis=0, keepdims=True) \
                * _row(t["ecs"], CHUNK - 1)
            dcs += jnp.where(ri == CHUNK - 1, last, 0.0)
        triu = (ci >= ri).astype(BF16)
        da = _dot_sel(dcs, triu, terms=3, sel_first=True)
        dX = dX_s[...]
        ddt = da * t["A"] + _dot_sel(dX * sx, ex, NT, terms=1)
        dA_ref[...] += jnp.sum(da * t["dt"], axis=0, keepdims=True)
        ddtp = ddt * _sigmoid(t["pre"])
        dbias_ref[...] += jnp.sum(ddtp, axis=0, keepdims=True)
        ddtp_ref[...] = ddtp.astype(BF16)
        dxbc_s[:, 0:SSD_INNER] = dX * t["dt_e"] + de_ref[...] * dY
        sg = t["sg"]
        dxc = dxbc_s[...] * (sg * (1.0 + xc * (1.0 - sg)))
        xp = xp_ref[...]
        nxt = head[...]
        dcb_ref[...] += jnp.sum(dxc, axis=0, keepdims=True)
        dpre = cw_ref[3:4, :] * dxc
        dcw_ref[3:4, :] += jnp.sum(xp * dxc, axis=0, keepdims=True)
        for j in range(1, 4):
            sh = _shift_up(dxc, nxt, j)
            dpre += cw_ref[3 - j:4 - j, :] * sh
            dcw_ref[3 - j:4 - j, :] += jnp.sum(xp * sh, axis=0, keepdims=True)
        head[...] = dxc[0:8]
        dp_ref[:, SSD_INNER:SSD_INNER + SSD_CONV_CH] = dpre.astype(BF16)

    c0 = lambda i: (0, 0)
    rev = lambda i: (nc - 1 - i, 0)
    return pl.pallas_call(
        body, name="ssd_bwd", grid=(nc,),
        in_specs=[pl.BlockSpec((CHUNK, SSD_INNER), rev),
                  pl.BlockSpec((CHUNK, SSD_INNER), lambda i: (nc - 1 - i, 2)),
                  pl.BlockSpec((CHUNK, SSD_CONV_CH), lambda i: (nc - 1 - i, 2)),
                  pl.BlockSpec((CHUNK, LANE), rev),
                  pl.BlockSpec((CHUNK, SSD_CONV_CH), rev),
                  pl.BlockSpec((CHUNK, SSD_INNER), rev),
                  pl.BlockSpec((1, SSD_STATE, SSD_INNER), lambda i: (nc - 1 - i, 0, 0)),
                  pl.BlockSpec((4, SSD_CONV_CH), c0), pl.BlockSpec((1, LANE), c0), pl.BlockSpec((1, LANE), c0),
                  pl.BlockSpec((1, SSD_INNER), c0), pl.BlockSpec((1, SSD_INNER), c0),
                  pl.BlockSpec((LANE, SSD_INNER), c0)],
        out_specs=[pl.BlockSpec((CHUNK, 2560), rev), pl.BlockSpec((CHUNK, LANE), rev),
                   pl.BlockSpec((4, SSD_CONV_CH), c0), pl.BlockSpec((1, SSD_CONV_CH), c0),
                   pl.BlockSpec((1, LANE), c0), pl.BlockSpec((1, LANE), c0),
                   pl.BlockSpec((1, SSD_INNER), c0), pl.BlockSpec((1, SSD_INNER), c0)],
        out_shape=[jax.ShapeDtypeStruct((T, 2560), BF16), jax.ShapeDtypeStruct((T, LANE), BF16),
                   jax.ShapeDtypeStruct((4, SSD_CONV_CH), F32), jax.ShapeDtypeStruct((1, SSD_CONV_CH), F32),
                   jax.ShapeDtypeStruct((1, LANE), F32), jax.ShapeDtypeStruct((1, LANE), F32),
                   jax.ShapeDtypeStruct((1, SSD_INNER), F32), jax.ShapeDtypeStruct((1, SSD_INNER), F32)],
        scratch_shapes=[pltpu.VMEM((SSD_STATE, SSD_INNER), F32), pltpu.VMEM((8, SSD_CONV_CH), F32),
                        pltpu.VMEM((CHUNK, SSD_INNER), F32), pltpu.VMEM((CHUNK, SSD_CONV_CH), F32)],
        compiler_params=_params(1),
    )(dyn, proj, proj, dtp, xc, y, sprev, conv_w, dt_bias, a_log, d_e, g_ssd, expand)


def _outproj_fwd(x, y_lru, y_ssd, w_out, g_pm, g_pf):
    T = x.shape[0]
    TT = TP

    def body(x_ref, yl_ref, ys_ref, wo_ref, gpm_ref, gpf_ref, mix_ref, x1_ref, h2_ref):
        mix = _dot(yl_ref[...], wo_ref[0:LRU_W, :]) + _dot(ys_ref[...], wo_ref[LRU_W:2 * LRU_W, :])
        mix_ref[...] = mix
        x1 = x_ref[...] + mix * _rms(mix) * gpm_ref[...]
        x1_ref[...] = x1
        h2_ref[...] = (x1 * _rms(x1) * gpf_ref[...]).astype(BF16)

    tile = pl.BlockSpec((TT, D_MODEL), lambda i: (i, 0))
    vec = pl.BlockSpec((1, D_MODEL), lambda i: (0, 0))
    return pl.pallas_call(
        body, name="outproj_fwd", grid=(T // TT,),
        in_specs=[tile, tile, tile, pl.BlockSpec((2 * LRU_W, D_MODEL), lambda i: (0, 0)), vec, vec],
        out_specs=[tile, tile, tile],
        out_shape=[jax.ShapeDtypeStruct((T, D_MODEL), F32), jax.ShapeDtypeStruct((T, D_MODEL), F32),
                   jax.ShapeDtypeStruct((T, D_MODEL), BF16)],
        compiler_params=_params(1),
    )(x, y_lru, y_ssd, w_out, g_pm, g_pf)


def _ffn_fwd_bwd(x1, h2, target, w_gate, w_up, w_down, g_pf, g_ff):
    T = x1.shape[0]

    def body(x1_ref, h2_ref, tg_ref, wg_hbm, wu_hbm, wd_hbm, gpf_ref, gff_ref,
             dx1_ref, act_ref, df_ref, dgt_ref, dup_ref, dgpf_ref, dgff_ref, loss_ref,
             wg, wu, wd, gt_s, up_s, sem):
        @pl.when(pl.program_id(0) == 0)
        def _():
            cps = [pltpu.make_async_copy(s, d, sem.at[n]) for n, (s, d) in
                   enumerate(((wg_hbm, wg), (wu_hbm, wu), (wd_hbm, wd)))]
            for c in cps:
                c.start()
            for c in cps:
                c.wait()
            for ref in (dgpf_ref, dgff_ref, loss_ref):
                ref[...] = jnp.zeros_like(ref)

        h2 = h2_ref[...]
        f = jnp.zeros((TT, D_MODEL), F32)
        for c in range(D_FF // FF_CHUNK):
            sl = slice(FF_CHUNK * c, FF_CHUNK * (c + 1))
            gt = _dot(h2, wg[:, sl])
            up = _dot(h2, wu[:, sl])
            gt_s[:, sl] = gt
            up_s[:, sl] = up
            act = (gt * _sigmoid(gt) * up).astype(BF16)
            act_ref[:, sl] = act
            f += _dot(act, wd[sl, :])
        x1 = x1_ref[...]
        rnf = _rms(f)
        e = x1 + f * rnf * gff_ref[...] - tg_ref[...]
        part = 0.5 * jnp.sum(jnp.sum(e * e, axis=1, keepdims=True), axis=0, keepdims=True) * (1.0 / D_MODEL)
        lane = lax.broadcasted_iota(jnp.int32, (1, LANE), 1)
        loss_ref[...] += jnp.where(lane == 0, part, 0.0)
        dx2 = e * (1.0 / D_MODEL)
        dgff_ref[...] += jnp.sum(dx2 * f * rnf, axis=0, keepdims=True)
        df = _rms_bwd(dx2 * gff_ref[...], f, rnf).astype(BF16)
        df_ref[...] = df
        dh2 = jnp.zeros((TT, D_MODEL), F32)
        for c in range(D_FF // FF_CHUNK):
            sl = slice(FF_CHUNK * c, FF_CHUNK * (c + 1))
            dact = _dot(df, wd[sl, :], NT)
            gt = gt_s[:, sl]
            up = up_s[:, sl]
            sg = _sigmoid(gt)
            dgt = (dact * up * (sg * (1.0 + gt * (1.0 - sg)))).astype(BF16)
            dup = (dact * gt * sg).astype(BF16)
            dgt_ref[:, sl] = dgt
            dup_ref[:, sl] = dup
            dh2 += _dot(dgt, wg[:, sl], NT) + _dot(dup, wu[:, sl], NT)
        rn2 = _rms(x1)
        dgpf_ref[...] += jnp.sum(dh2 * x1 * rn2, axis=0, keepdims=True)
        dx1_ref[...] = dx2 + _rms_bwd(dh2 * gpf_ref[...], x1, rn2)

    tile = pl.BlockSpec((TT, D_MODEL), lambda i: (i, 0))
    wide = pl.BlockSpec((TT, D_FF), lambda i: (i, 0))
    vec = pl.BlockSpec((1, D_MODEL), lambda i: (0, 0))
    hbm = pl.BlockSpec(memory_space=pl.ANY)
    return pl.pallas_call(
        body, name="ffn_fwd_bwd", grid=(T // TT,),
        in_specs=[tile, tile, tile, hbm, hbm, hbm, vec, vec],
        out_specs=[tile, wide, tile, wide, wide, vec, vec, pl.BlockSpec((1, LANE), lambda i: (0, 0))],
        out_shape=[jax.ShapeDtypeStruct((T, D_MODEL), F32), jax.ShapeDtypeStruct((T, D_FF), BF16),
                   jax.ShapeDtypeStruct((T, D_MODEL), BF16), jax.ShapeDtypeStruct((T, D_FF), BF16),
                   jax.ShapeDtypeStruct((T, D_FF), BF16), jax.ShapeDtypeStruct((1, D_MODEL), F32),
                   jax.ShapeDtypeStruct((1, D_MODEL), F32), jax.ShapeDtypeStruct((1, LANE), F32)],
        scratch_shapes=[pltpu.VMEM((D_MODEL, D_FF), BF16), pltpu.VMEM((D_MODEL, D_FF), BF16),
                        pltpu.VMEM((D_FF, D_MODEL), BF16), pltpu.VMEM((TT, D_FF), F32),
                        pltpu.VMEM((TT, D_FF), F32), pltpu.SemaphoreType.DMA((3,))],
        compiler_params=_params(1),
    )(x1, h2, target, w_gate, w_up, w_down, g_pf, g_ff)


def _outproj_bwd(dx1, mix, w_out, g_pm):
    T = dx1.shape[0]
    TT = TP

    def body(dx1_ref, mix_ref, wo_ref, gpm_ref, dyl_ref, dys_ref, dmix_ref, dg_ref):
        @pl.when(pl.program_id(0) == 0)
        def _():
            dg_ref[...] = jnp.zeros_like(dg_ref)

        mix = mix_ref[...]
        rn = _rms(mix)
        dx1v = dx1_ref[...]
        dg_ref[...] += jnp.sum(dx1v * mix * rn, axis=0, keepdims=True)
        dmix = _rms_bwd(dx1v * gpm_ref[...], mix, rn).astype(BF16)
        dmix_ref[...] = dmix
        dyl_ref[...] = _dot(dmix, wo_ref[0:LRU_W, :], NT)
        dys_ref[...] = _dot(dmix, wo_ref[LRU_W:2 * LRU_W, :], NT)

    tile = pl.BlockSpec((TT, D_MODEL), lambda i: (i, 0))
    vec = pl.BlockSpec((1, D_MODEL), lambda i: (0, 0))
    return pl.pallas_call(
        body, name="outproj_bwd", grid=(T // TT,),
        in_specs=[tile, tile, pl.BlockSpec((2 * LRU_W, D_MODEL), lambda i: (0, 0)), vec],
        out_specs=[tile, tile, tile, vec],
        out_shape=[jax.ShapeDtypeStruct((T, D_MODEL), F32), jax.ShapeDtypeStruct((T, D_MODEL), F32),
                   jax.ShapeDtypeStruct((T, D_MODEL), BF16), jax.ShapeDtypeStruct((1, D_MODEL), F32)],
        compiler_params=_params(1),
    )(dx1, mix, w_out, g_pm)


def _tn_matmul(a, bs, name, tk=512):
    T, M = a.shape
    nk = T // tk
    nb = len(bs)

    def body(*refs):
        a_ref, b_refs, o_refs, accs = refs[0], refs[1:1 + nb], refs[1 + nb:1 + 2 * nb], refs[1 + 2 * nb:]
        k = pl.program_id(0)

        @pl.when(k == 0)
        def _():
            for acc in accs:
                acc[...] = jnp.zeros_like(acc)

        av = a_ref[...].astype(BF16)
        for b_ref, acc in zip(b_refs, accs):
            acc[...] += _dot(av, b_ref[...], TN)

        @pl.when(k == nk - 1)
        def _():
            for o_ref, acc in zip(o_refs, accs):
                o_ref[...] = acc[...].astype(BF16)

    return pl.pallas_call(
        body, name=name, grid=(nk,),
        in_specs=[pl.BlockSpec((tk, M), lambda k: (k, 0))]
                 + [pl.BlockSpec((tk, b.shape[1]), lambda k: (k, 0)) for b in bs],
        out_specs=[pl.BlockSpec((M, b.shape[1]), lambda k: (0, 0)) for b in bs],
        out_shape=[jax.ShapeDtypeStruct((M, b.shape[1]), BF16) for b in bs],
        scratch_shapes=[pltpu.VMEM((M, b.shape[1]), F32) for b in bs],
        compiler_params=_params(1),
    )(a, *bs)


def _tn_matmul_slabs(a, bs, name, slab, tk=512):
    T, M = a.shape
    nk = T // tk
    nb = len(bs)
    offs = [sum(b.shape[1] for b in bs[:i]) for i in range(nb + 1)]

    def body(*refs):
        a_ref, b_refs, o_ref, acc = refs[0], refs[1:1 + nb], refs[1 + nb], refs[2 + nb]
        k = pl.program_id(0)

        @pl.when(k == 0)
        def _():
            acc[...] = jnp.zeros_like(acc)

        av = a_ref[...].astype(BF16)
        for i, b_ref in enumerate(b_refs):
            acc[:, offs[i]:offs[i + 1]] += _dot(av, b_ref[...], TN)

        @pl.when(k == nk - 1)
        def _():
            for j in range(N_DEV):
                o_ref[j] = acc[:, slab * j:slab * (j + 1)].astype(BF16)

    return pl.pallas_call(
        body, name=name, grid=(nk,),
        in_specs=[pl.BlockSpec((tk, M), lambda k: (k, 0))]
                 + [pl.BlockSpec((tk, b.shape[1]), lambda k: (k, 0)) for b in bs],
        out_specs=pl.BlockSpec((N_DEV, M, slab), lambda k: (0, 0, 0)),
        out_shape=jax.ShapeDtypeStruct((N_DEV, M, slab), BF16),
        scratch_shapes=[pltpu.VMEM((M, offs[-1]), F32)],
        compiler_params=_params(1),
    )(a, *bs)


def _tn_blockdiag(a, b1, b2, name, tk=1024):
    T = a.shape[0]
    tk = min(tk, T)

    def body(a_ref, b1_ref, b2_ref, o1_ref, o2_ref):
        @pl.when(pl.program_id(0) == 0)
        def _():
            o1_ref[...] = jnp.zeros_like(o1_ref)
            o2_ref[...] = jnp.zeros_like(o2_ref)

        for j in range(4):
            sl = slice(256 * j, 256 * (j + 1))
            av = a_ref[:, sl].astype(BF16)
            o1_ref[j] += _dot(av, b1_ref[:, sl], TN)
            o2_ref[j] += _dot(av, b2_ref[:, sl], TN)

    blk = pl.BlockSpec((tk, LRU_W), lambda k: (k, 0))
    out = pl.BlockSpec((4, 256, 256), lambda k: (0, 0, 0))
    return pl.pallas_call(
        body, name=name, grid=(T // tk,),
        in_specs=[blk, blk, blk], out_specs=[out, out],
        out_shape=[jax.ShapeDtypeStruct((4, 256, 256), F32)] * 2,
        compiler_params=_params(1),
    )(a, b1, b2)


def _adamw_update(g, w, m, v):
    nm = ADAM_B1 * m + (1.0 - ADAM_B1) * g
    nv = ADAM_B2 * v + (1.0 - ADAM_B2) * (g * g)
    m_hat = nm / (1.0 - ADAM_B1 ** ADAM_STEP)
    v_hat = nv / (1.0 - ADAM_B2 ** ADAM_STEP)
    return -ADAM_LR * (m_hat / (jnp.sqrt(v_hat) + ADAM_EPS) + ADAM_WD * w), nm, nv


def _adamw_small(parts, params):
    n = len(params)

    def body(*refs):
        p_ref, wmv = refs[0], refs[1:1 + 3 * n]
        gs_ref, outs = refs[1 + 3 * n], refs[2 + 3 * n:]
        g = p_ref[0]
        for k in range(1, N_DEV):
            g = g + p_ref[k]
        gs_ref[...] = g
        for i, (row, w, _, _) in enumerate(params):
            rows, width = w.shape
            if width <= 1024:
                gp = gs_ref[row:row + rows, 0:width]
            else:
                gp = jnp.concatenate([gs_ref[row:row + 1, :], gs_ref[row + 1:row + 2, 0:width - 1024]], axis=1)
            d, nm, nv = _adamw_update(gp, wmv[3 * i][...], wmv[3 * i + 1][...], wmv[3 * i + 2][...])
            for ref, val in zip(outs[4 * i:4 * i + 4], (gp, d, nm, nv)):
                ref[...] = val

    full = lambda s: pl.BlockSpec(s, lambda: (0,) * len(s))
    shapes = [w.shape for _, w, _, _ in params]
    return pl.pallas_call(
        body, name="adamw_small",
        in_specs=[full(parts.shape)] + [full(s) for s in shapes for _ in range(3)],
        out_specs=[full(parts.shape[1:])] + [full(s) for s in shapes for _ in range(4)],
        out_shape=[jax.ShapeDtypeStruct(parts.shape[1:], F32)]
                  + [jax.ShapeDtypeStruct(s, F32) for s in shapes for _ in range(4)],
        compiler_params=pltpu.CompilerParams(vmem_limit_bytes=VMEM_LIMIT),
    )(parts, *[a for _, w, m, v in params for a in (w, m, v)])


def _adamw_shard(lands, slabs, me, w, m, v, name, tr):
    P, R, C = lands.shape

    def body(me_ref, p_ref, own_ref, w_ref, m_ref, v_ref, g_ref, d_ref, nm_ref, nv_ref):
        g = None
        for k in range(P):
            part = jnp.where(me_ref[0] == k, own_ref[0], p_ref[k]).astype(F32)
            g = part if g is None else g + part
        g_ref[...] = g
        d_ref[...], nm_ref[...], nv_ref[...] = _adamw_update(g, w_ref[...], m_ref[...], v_ref[...])

    blk = pl.BlockSpec((tr, C), lambda i, me_ref: (i, 0))
    return pl.pallas_call(
        body, name=name,
        grid_spec=pltpu.PrefetchScalarGridSpec(
            num_scalar_prefetch=1, grid=(R // tr,),
            in_specs=[pl.BlockSpec((P, tr, C), lambda i, me_ref: (0, i, 0)),
                      pl.BlockSpec((1, tr, C), lambda i, me_ref: (me_ref[0], i, 0)), blk, blk, blk],
            out_specs=[blk, blk, blk, blk]),
        out_shape=[jax.ShapeDtypeStruct((R, C), F32)] * 4,
        compiler_params=_params(1),
    )(me, lands, slabs, w, m, v)


def _adamw(parts, w, m, v, name, tr):
    P, R, C = parts.shape

    def body(p_ref, w_ref, m_ref, v_ref, g_ref, d_ref, nm_ref, nv_ref):
        g = p_ref[0].astype(F32)
        for k in range(1, P):
            g = g + p_ref[k].astype(F32)
        g_ref[...] = g
        d_ref[...], nm_ref[...], nv_ref[...] = _adamw_update(g, w_ref[...], m_ref[...], v_ref[...])

    blk = pl.BlockSpec((tr, C), lambda i: (i, 0))
    return pl.pallas_call(
        body, name=name, grid=(R // tr,),
        in_specs=[pl.BlockSpec((P, tr, C), lambda i: (0, i, 0)), blk, blk, blk],
        out_specs=[blk, blk, blk, blk],
        out_shape=[jax.ShapeDtypeStruct((R, C), F32)] * 4,
        compiler_params=_params(1),
    )(parts, w, m, v)


def _peer(k):
    x, y, c = lax.axis_index("x"), lax.axis_index("y"), lax.axis_index("c")
    px = x ^ ((k >> 2) & 1)
    py = y ^ ((k >> 1) & 1)
    pc = c ^ (k & 1)
    return (px, py, pc), 4 * px + 2 * py + pc


def _my_block():
    return 4 * lax.axis_index("x") + 2 * lax.axis_index("y") + lax.axis_index("c")


def _all_gather(shards, name):
    n = len(shards)

    def body(*refs):
        ins, outs = refs[:n], refs[n:2 * n]
        send, recv, loc = refs[2 * n:]
        x, y, c = lax.axis_index("x"), lax.axis_index("y"), lax.axis_index("c")
        sibling = (x, y, 1 - c)
        chips = [(1 - x, y), (x, 1 - y), (1 - x, 1 - y)]
        slot = lambda px, py, pc: 4 * px + 2 * py + pc

        def copy(a, k, block, to, src=None):
            dst = outs[a].at[slot(*block)]
            return pltpu.make_async_remote_copy(
                src_ref=dst if src is None else src, dst_ref=dst, send_sem=send.at[a, k], recv_sem=recv.at[a, k],
                device_id=to, device_id_type=pl.DeviceIdType.MESH)

        mine = [pltpu.make_async_copy(ins[a], outs[a].at[slot(x, y, c)], loc.at[a]) for a in range(n)]
        for cp in mine:
            cp.start()
        first = []
        for a in range(n):
            first.append(copy(a, 0, (x, y, c), sibling, src=ins[a]))
            first += [copy(a, 1 + j, (x, y, c), (*chip, c), src=ins[a]) for j, chip in enumerate(chips)]
        for cp in first:
            cp.start()
        passed = []
        for j, chip in enumerate(chips):
            for a in range(n):
                copy(a, 1 + j, (*chip, c), (x, y, c)).wait_recv()
                fwd = copy(a, 4 + j, (*chip, c), sibling)
                fwd.start()
                passed.append(fwd)
        for a in range(n):
            copy(a, 0, sibling, (x, y, c)).wait_recv()
            for j, chip in enumerate(chips):
                copy(a, 4 + j, (*chip, 1 - c), (x, y, c)).wait_recv()
        for cp in first + passed:
            cp.wait_send()
        for cp in mine:
            cp.wait()

    hbm = pl.BlockSpec(memory_space=pl.ANY)
    return pl.pallas_call(
        body, name=name,
        in_specs=[hbm] * n, out_specs=[hbm] * n,
        out_shape=[jax.ShapeDtypeStruct((N_DEV,) + s.shape, s.dtype) for s in shards],
        scratch_shapes=[pltpu.SemaphoreType.DMA((n, N_DEV - 1)), pltpu.SemaphoreType.DMA((n, N_DEV - 1)),
                        pltpu.SemaphoreType.DMA((n,))],
    )(*shards)


_HBM = pl.BlockSpec(memory_space=pltpu.HBM)
_SEM = pl.BlockSpec(memory_space=pltpu.SEMAPHORE)
_EFFECT = pltpu.SideEffectType.DATAFLOW_SIDE_EFFECTING


def _direct_copies(srcs, lands, send, recv, slab_source):
    me = _my_block()
    cps = []
    for k in range(1, N_DEV):
        to, blk = _peer(k)
        for a, (src, land) in enumerate(zip(srcs, lands)):
            if slab_source:
                s, d = src.at[blk], land.at[me]
            elif land.ndim == 3:
                s, d = src, land.at[me]
            else:
                s, d = src, land.at[pl.ds(pl.multiple_of(me * src.shape[0], 16), src.shape[0]), :]
            cps.append(pltpu.make_async_remote_copy(
                src_ref=s, dst_ref=d,
                send_sem=send.at[a * (N_DEV - 1) + k - 1], recv_sem=recv.at[a * (N_DEV - 1) + k - 1],
                device_id=to, device_id_type=pl.DeviceIdType.MESH))
    return cps


def _exchange_start(srcs, name, slab_source, axes=None):
    n = len(srcs)
    if slab_source:
        shapes = [s.shape for s in srcs]
    else:
        shapes = [(N_DEV,) + s.shape if ax == 1 else (N_DEV * s.shape[0], s.shape[1]) for s, ax in zip(srcs, axes)]
    lands = [pltpu.with_memory_space_constraint(lax.empty(sh, s.dtype), pltpu.HBM) for sh, s in zip(shapes, srcs)]

    def body(*refs):
        ins, land_in = refs[:n], refs[n:2 * n]
        send, recv = refs[2 * n], refs[2 * n + 1]
        token = refs[4 * n + 2]
        for cp in _direct_copies(ins, land_in, send, recv, slab_source):
            cp.start()
        token[...] = jnp.zeros_like(token)

    sems = pltpu.SemaphoreType.DMA((n * (N_DEV - 1),))
    res = pl.pallas_call(
        body, name=name,
        out_shape=(sems, sems, *[pltpu.HBM(s.shape, s.dtype) for s in srcs],
                   *[pltpu.HBM(l.shape, l.dtype) for l in lands], jax.ShapeDtypeStruct((8, LANE), F32)),
        in_specs=[_HBM] * (2 * n),
        out_specs=(_SEM, _SEM, *[_HBM] * (2 * n), pl.BlockSpec(memory_space=pltpu.VMEM)),
        input_output_aliases={i: 2 + i for i in range(2 * n)},
        compiler_params=pltpu.CompilerParams(has_side_effects=_EFFECT),
    )(*[pltpu.with_memory_space_constraint(s, pltpu.HBM) for s in srcs], *lands)
    return dict(send=res[0], recv=res[1], srcs=res[2:2 + n], lands=res[2 + n:2 + 2 * n], token=res[-1],
                slab_source=slab_source)


def _exchange_wait(ex, after, name):
    n = len(ex["srcs"])
    slab_source = ex["slab_source"]

    def body(*refs):
        ins, lands = refs[:n], refs[n:2 * n]
        send, recv = refs[2 * n], refs[2 * n + 1]
        for cp in _direct_copies(ins, lands, send, recv, slab_source):
            cp.wait_send()
            cp.wait_recv()

    res = pl.pallas_call(
        body, name=name,
        out_shape=tuple(pltpu.HBM(s.shape, s.dtype) for s in list(ex["srcs"]) + list(ex["lands"])),
        in_specs=[_HBM] * (2 * n) + [_SEM, _SEM, pl.BlockSpec(memory_space=pl.ANY)],
        out_specs=tuple([_HBM] * (2 * n)),
        input_output_aliases={i: i for i in range(2 * n)},
        compiler_params=pltpu.CompilerParams(has_side_effects=_EFFECT),
    )(*ex["srcs"], *ex["lands"], ex["send"], ex["recv"], after)
    if slab_source:
        return list(zip(res[n:], res[:n]))
    me = _my_block()
    out = []
    for src, land in zip(res[:n], res[n:]):
        if land.ndim == 3:
            own, at = src[None], (me, 0, 0)
        else:
            own, at = src, (me * src.shape[0], 0)
        out.append(lax.dynamic_update_slice(land, own, at))
    return out


BIG = ("w_in", "w_out", "w_gate", "w_up", "w_down")
BIG_SHARD = {"w_in": (1024, 578), "w_out": (256, 1024), "w_gate": (1024, 352), "w_up": (1024, 352),
             "w_down": (352, 1024)}
BIG_SHARD_AXIS = {"w_in": 1, "w_out": 0, "w_gate": 1, "w_up": 1, "w_down": 0}
BIG_ADAM_ROWS = {"w_in": 256, "w_out": 128, "w_gate": 256, "w_up": 256, "w_down": 176}


def _join(parts, axis):
    if axis == 0:
        return parts.reshape((-1,) + parts.shape[2:])
    return jnp.concatenate([parts[j] for j in range(N_DEV)], axis=1)


def _split(full, axis):
    if axis == 0:
        return full.reshape((N_DEV, full.shape[0] // N_DEV) + full.shape[1:])
    c = full.shape[1] // N_DEV
    return jnp.stack([full[:, c * j:c * (j + 1)] for j in range(N_DEV)])


SMALL = (("lru_wa", 65536), ("lru_wx", 65536), ("pre_mix_norm", 1024), ("lru_conv_w", 4096), ("lru_conv_b", 1024),
         ("lru_ba", 1024), ("lru_bx", 1024), ("lru_lambda", 1024), ("lru_out_norm", 1024), ("ssd_conv_w", 6144),
         ("ssd_conv_b", 1536), ("ssd_dt_bias", 16), ("ssd_a_log", 16), ("ssd_d", 16), ("ssd_out_norm", 1024),
         ("post_mix_norm", 1024), ("pre_ffn_norm", 1024), ("post_ffn_norm", 1024), ("loss", 1))
REPLICATED = tuple(n for n, _ in SMALL if n not in ("lru_conv_w", "ssd_conv_w", "loss"))
SMALL_ROW = {}
for _name, _size in SMALL:
    SMALL_ROW[_name] = (sum(-(-s // 1024) for n, s in SMALL[:len(SMALL_ROW)]), -(-_size // 1024))


def _pack_small(d):
    rows = [jnp.pad(d[n].astype(F32).reshape(-1), (0, SMALL_ROW[n][1] * 1024 - s)).reshape(-1, 1024) for n, s in SMALL]
    used = sum(r.shape[0] for r in rows)
    return jnp.concatenate(rows + [jnp.zeros((SMALL_ROWS - used, 1024), F32)], axis=0)


def _small_entry(p, name):
    row, rows = SMALL_ROW[name]
    return p[row:row + rows].reshape(-1)[:dict(SMALL)[name]]


def _blockdiag4(w):
    on_diag = jnp.eye(4, dtype=w.dtype)[None, :, None, :, None]
    return (w.reshape(4, 4, 64, 1, 64) * on_diag).reshape(4, 256, 256)


def _diag_blocks(g):
    on_diag = jnp.eye(4, dtype=g.dtype)[None, :, None, :, None]
    return jnp.sum(g.reshape(4, 4, 64, 4, 64) * on_diag, axis=3).reshape(16, 64, 64)


def _local_step(x, target, w_in, P, rest_weights, emit, emit_small, start_token=None):
    cut = IN_MAIN - (N_DEV - 1) * (IN_COLS // N_DEV)
    w_main = jnp.concatenate([w_in[j] for j in range(N_DEV - 1)] + [w_in[N_DEV - 1][:, :cut]], axis=1)
    w_dt = jnp.pad(w_in[N_DEV - 1][:, cut:], ((0, 0), (0, LANE - SSD_HEADS)))
    pad16 = lambda v: jnp.pad(v.reshape(1, SSD_HEADS), ((0, 0), (0, LANE - SSD_HEADS)))
    dt_bias, a_log = pad16(P["ssd_dt_bias"]), pad16(P["ssd_a_log"])
    d_e = jnp.repeat(P["ssd_d"].reshape(SSD_HEADS), SSD_HEAD_DIM).reshape(1, SSD_INNER)
    expand = (jnp.arange(LANE)[:, None] == (jnp.arange(SSD_INNER)[None, :] // SSD_HEAD_DIM)).astype(BF16)
    wa_bd = _blockdiag4(P["lru_wa"].astype(BF16))
    wx_bd = _blockdiag4(P["lru_wx"].astype(BF16))
    vec = lambda n: P[n].reshape(1, -1)

    after = lambda v, tok: v if tok is None else v + tok[0:1, 0:1]

    h, proj, dtp = _inproj_fwd(x, after(vec("pre_mix_norm"), start_token), w_main, w_dt)
    lx, hl, y_lru, *gates = _lru_fwd(proj, P["lru_conv_w"], vec("lru_conv_b"), wa_bd, wx_bd, vec("lru_ba"),
                                     vec("lru_bx"), vec("lru_lambda"), vec("lru_out_norm"))
    xc, y, y_ssd, sprev = _ssd_fwd(proj, dtp, P["ssd_conv_w"], vec("ssd_conv_b"), dt_bias, a_log, d_e,
                                   vec("ssd_out_norm"), expand)
    W = rest_weights(y_ssd)
    mix, x1, h2 = _outproj_fwd(x, y_lru, y_ssd, W["w_out"], vec("post_mix_norm"), vec("pre_ffn_norm"))
    dx1, act, df, dgt, dup, dg_pf, dg_ff, loss = _ffn_fwd_bwd(
        x1, h2, target, W["w_gate"], W["w_up"], W["w_down"], vec("pre_ffn_norm"), vec("post_ffn_norm"))
    tok = emit("ffn", {"w_gate": _tn_matmul_slabs(h2, [dgt], "dw_gate", D_FF // N_DEV),
                       "w_up": _tn_matmul_slabs(h2, [dup], "dw_up", D_FF // N_DEV),
                       "w_down": _tn_matmul(act, [df], "dw_down")[0]})
    dy_lru, dy_ssd, dmix, dg_pm = _outproj_bwd(dx1, mix, W["w_out"], after(vec("post_mix_norm"), tok))
    tok = emit("out", {"w_out": jnp.concatenate([_tn_matmul(y_lru, [dmix], "dw_out_lru")[0],
                                                 _tn_matmul(y_ssd, [dmix], "dw_out_ssd")[0]], axis=0)})
    dp_ssd, ddtp, dcw_s, dcb_s, dbias, dA, dD_e, dg_ssd = _ssd_bwd(
        dy_ssd, proj, dtp, xc, y, sprev, P["ssd_conv_w"], dt_bias, a_log, d_e,
        after(vec("ssd_out_norm"), tok), expand)
    dp_lru, dpa, dpx, dcw_l, dcb_l, dba, dbx, dlam, dg_lru = _lru_bwd(
        dy_lru, proj, lx, hl, gates, P["lru_conv_w"], wa_bd, wx_bd, vec("lru_lambda"), vec("lru_out_norm"))
    tok = emit("in", {"w_in": _tn_matmul_slabs(h, [dp_lru, dp_ssd, ddtp], "dw_in", IN_COLS // N_DEV)})

    a_neg = -jnp.exp(P["ssd_a_log"].reshape(SSD_HEADS))
    dwa, dwx = _tn_blockdiag(lx, dpa, dpx, "dw_lru_gates")
    small = {
        "pre_mix_norm": jnp.zeros((1, D_MODEL), F32), "lru_conv_w": dcw_l, "lru_conv_b": dcb_l,
        "lru_wa": _diag_blocks(dwa), "lru_ba": dba,
        "lru_wx": _diag_blocks(dwx), "lru_bx": dbx,
        "lru_lambda": dlam, "lru_out_norm": dg_lru, "ssd_conv_w": dcw_s, "ssd_conv_b": dcb_s,
        "ssd_dt_bias": dbias[0, :SSD_HEADS], "ssd_a_log": dA[0, :SSD_HEADS] * a_neg,
        "ssd_d": jnp.sum(dD_e.reshape(SSD_HEADS, SSD_HEAD_DIM), axis=1), "ssd_out_norm": dg_ssd,
        "post_mix_norm": dg_pm, "pre_ffn_norm": dg_pf, "post_ffn_norm": dg_ff, "loss": loss[0, 0:1],
    }
    tok = after(after(vec("pre_mix_norm"), tok), emit_small(small))
    grad_x, dg_pre = _inproj_bwd(dp_lru, dp_ssd, ddtp, dx1, x, tok, w_main, w_dt)
    return grad_x, dg_pre


def kernel(x, pre_mix_norm, w_in, lru_conv_w, lru_conv_b, lru_wa, lru_ba, lru_wx, lru_bx, lru_lambda, lru_out_norm, ssd_conv_w, ssd_conv_b, ssd_dt_bias, ssd_a_log, ssd_d, ssd_out_norm, w_out, post_mix_norm, pre_ffn_norm, w_gate, w_up, w_down, post_ffn_norm, loss_target, m_pre_mix_norm, m_w_in, m_lru_conv_w, m_lru_conv_b, m_lru_wa, m_lru_ba, m_lru_wx, m_lru_bx, m_lru_lambda, m_lru_out_norm, m_ssd_conv_w, m_ssd_conv_b, m_ssd_dt_bias, m_ssd_a_log, m_ssd_d, m_ssd_out_norm, m_w_out, m_post_mix_norm, m_pre_ffn_norm, m_w_gate, m_w_up, m_w_down, m_post_ffn_norm, v_pre_mix_norm, v_w_in, v_lru_conv_w, v_lru_conv_b, v_lru_wa, v_lru_ba, v_lru_wx, v_lru_bx, v_lru_lambda, v_lru_out_norm, v_ssd_conv_w, v_ssd_conv_b, v_ssd_dt_bias, v_ssd_a_log, v_ssd_d, v_ssd_out_norm, v_w_out, v_post_mix_norm, v_pre_ffn_norm, v_w_gate, v_w_up, v_w_down, v_post_ffn_norm):
    a = dict(locals())
    names = [n for n, _ in SMALL if n != "loss"] + list(BIG)
    w = {n: a[n][0] for n in names}
    m = {n: a["m_" + n][0] for n in names}
    v = {n: a["v_" + n][0] for n in names}

    cpack = jnp.concatenate([w["lru_conv_w"], w["ssd_conv_w"], jnp.zeros((4, 64), F32)], axis=1)
    cpack = jnp.pad(cpack, ((0, 4), (0, 0)))
    g_in, cg = _all_gather([w["w_in"].astype(BF16), cpack], "all_gather_w_in")
    P = {n: w[n] for n in REPLICATED}
    P["lru_conv_w"] = _join(cg[:, 0:4, 0:128], 1)
    P["ssd_conv_w"] = _join(cg[:, 0:4, 128:320], 1)

    rest = [n for n in BIG if n != "w_in"]
    zero = jnp.minimum(jnp.abs(cg[0, 0, 0]), 0.0)
    ex_w = _exchange_start([(w[n] + zero).astype(BF16) for n in rest], "weights_start", slab_source=False,
                           axes=[BIG_SHARD_AXIS[n] for n in rest])

    def rest_weights(after):
        lands = _exchange_wait(ex_w, after, "weights_wait")
        return {n: _join(p, 1) if p.ndim == 3 else p for n, p in zip(rest, lands)}

    pending = []

    def emit(group, grads):
        ex = _exchange_start([g if g.ndim == 3 else _split(g, BIG_SHARD_AXIS[n]) for n, g in grads.items()],
                             "grads_start_" + group, slab_source=True)
        pending.append((group, list(grads), ex))
        return ex["token"]

    def emit_small(small):
        ex = _exchange_start([_pack_small(small)], "small_start", slab_source=False, axes=[0])
        pending.append(("small", None, ex))
        return ex["token"]

    grad_x, dg_pre = _local_step(x[0], loss_target[0], g_in, P, rest_weights, emit, emit_small, ex_w["token"])

    ex_small = pending.pop()[2]
    ex_pre = _exchange_start([jnp.pad(dg_pre, ((0, 7), (0, 0)))], "pre_mix_norm_start", slab_source=False, axes=[0])

    outs = {}
    done = ex_pre["token"]
    me1 = _my_block().astype(jnp.int32).reshape(1)
    for group, group_names, ex in pending:
        for n, (lands, slabs) in zip(group_names, _exchange_wait(ex, done, "grads_wait_" + group)):
            res = _adamw_shard(lands, slabs, me1, w[n], m[n], v[n], "adamw_" + n, BIG_ADAM_ROWS[n])
            done = res[0]
            for kind, r in zip(("grad", "delta", "new_m", "new_v"), res):
                outs[kind + "_" + n] = r

    got_pre = _exchange_wait(ex_pre, done, "pre_mix_norm_wait")[0].reshape(N_DEV, 8, 1024)
    got_small = _exchange_wait(ex_small, got_pre, "small_wait")[0].reshape(N_DEV, SMALL_ROWS, 1024)
    row = SMALL_ROW["pre_mix_norm"][0]
    got_small = got_small.at[:, row:row + 1, :].set(got_pre[:, 0:1, :])

    two_d = lambda t: t.reshape(-1, 1024) if t.ndim == 3 else t.reshape(1, -1)
    res = _adamw_small(got_small, [(SMALL_ROW[n][0], two_d(w[n]), two_d(m[n]), two_d(v[n])) for n in REPLICATED])
    g_pack = res[0]
    for i, n in enumerate(REPLICATED):
        for kind, r in zip(("grad", "delta", "new_m", "new_v"), res[1 + 4 * i:5 + 4 * i]):
            outs[kind + "_" + n] = r.reshape(w[n].shape)

    me = _my_block()
    gl = lax.dynamic_slice(_small_entry(g_pack, "lru_conv_w").reshape(4, LRU_W), (0, me * 128), (4, 128))
    gs = lax.dynamic_slice(_small_entry(g_pack, "ssd_conv_w").reshape(4, SSD_CONV_CH), (0, me * 192), (4, 192))
    cat = lambda d: jnp.pad(jnp.concatenate([d["lru_conv_w"], d["ssd_conv_w"]], axis=1), ((0, 4), (0, 64)))
    res = _adamw(cat({"lru_conv_w": gl, "ssd_conv_w": gs})[None], cat(w), cat(m), cat(v), "adamw_conv", 8)
    for kind, r in zip(("grad", "delta", "new_m", "new_v"), res):
        outs[kind + "_lru_conv_w"] = r[0:4, 0:128]
        outs[kind + "_ssd_conv_w"] = r[0:4, 128:320]

    order = ["pre_mix_norm", "w_in", "lru_conv_w", "lru_conv_b", "lru_wa", "lru_ba", "lru_wx", "lru_bx", "lru_lambda",
             "lru_out_norm", "ssd_conv_w", "ssd_conv_b", "ssd_dt_bias", "ssd_a_log", "ssd_d", "ssd_out_norm", "w_out",
             "post_mix_norm", "pre_ffn_norm", "w_gate", "w_up", "w_down", "post_ffn_norm"]
    result = [_small_entry(g_pack, "loss").reshape(()), grad_x[None]]
    for kind in ("grad", "delta", "new_m", "new_v"):
        result += [outs[kind + "_" + n][None] for n in order]
    return tuple(result)
```

```python
import functools

import jax
import jax.numpy as jnp
from jax import lax
from jax.experimental import pallas as pl
from jax.experimental.pallas import tpu as pltpu

F32 = jnp.float32
BF16 = jnp.bfloat16
EPS = 1e-6
N_DEV = 8
D_MODEL = 1024
LRU_W = 1024
SSD_INNER = 1024
SSD_HEADS = 16
SSD_HEAD_DIM = 64
SSD_STATE = 128
SSD_GROUPS = 2
SSD_CONV_CH = 1536
CHUNK = 128
D_FF = 2816
FF_CHUNK = 2816
IN_MAIN = 4608
IN_COLS = 4624
LANE = 128
TT = 256
TP = 512
VMEM_LIMIT = 56 * 1024 * 1024
ADAM_LR, ADAM_B1, ADAM_B2, ADAM_EPS, ADAM_WD, ADAM_STEP = 0.001, 0.9, 0.999, 1e-08, 0.01, 10
SMALL_ROWS = 160

NT = (((1,), (1,)), ((), ()))
TN = (((0,), (0,)), ((), ()))


def _params(n_grid):
    return pltpu.CompilerParams(dimension_semantics=("arbitrary",) * n_grid, vmem_limit_bytes=VMEM_LIMIT)


def _dot(a, b, dims=None, precision=None):
    if dims is None:
        return jnp.dot(a, b, preferred_element_type=F32, precision=precision)
    return lax.dot_general(a, b, dims, preferred_element_type=F32, precision=precision)


def _split_bf16(x, terms):
    out = []
    for _ in range(terms - 1):
        p = x.astype(BF16)
        out.append(p)
        x = x - p.astype(F32)
    return out + [x.astype(BF16)]


def _dot_sel(x, sel, dims=None, terms=2, sel_first=False):
    parts = [_dot(sel, p, dims) if sel_first else _dot(p, sel, dims) for p in _split_bf16(x, terms)]
    return functools.reduce(lambda a, b: a + b, parts)


def _sigmoid(x):
    return 0.5 * jnp.tanh(0.5 * x) + 0.5


def _softplus(x):
    e = jnp.exp(-jnp.abs(x))
    l1p = jnp.where(e < 1e-3, e * (1.0 - e * (0.5 - e * (1.0 / 3.0))), jnp.log(1.0 + e))
    return jnp.maximum(x, 0.0) + l1p


def _neg_expm1(x):
    series = -x * (1.0 + x * (0.5 + x * (1.0 / 6.0 + x * (1.0 / 24.0))))
    return jnp.where(x > -0.01, series, 1.0 - jnp.exp(x))


_GELU_C = 0.7978845608028654


def _gelu(x):
    t = jnp.tanh(_GELU_C * (x + 0.044715 * x * x * x))
    return 0.5 * x * (1.0 + t), t


def _gelu_grad(x, t):
    return 0.5 * (1.0 + t) + 0.5 * x * (1.0 - t * t) * _GELU_C * (1.0 + 3.0 * 0.044715 * x * x)


def _rms(x):
    return lax.rsqrt(jnp.mean(x * x, axis=-1, keepdims=True) + EPS)


def _rms_bwd(dyn, x, rn):
    return rn * dyn - x * (rn * rn * rn) * jnp.mean(dyn * x, axis=-1, keepdims=True)


def _row(x, r):
    idx = lax.broadcasted_iota(jnp.int32, x.shape, 0)
    return jnp.sum(jnp.where(idx == r, x, 0.0), axis=0, keepdims=True)


def _shift_down(cur, prev8, j):
    s = pltpu.roll(cur, j, 0)
    p = pltpu.roll(prev8, j, 0)
    r8 = lax.broadcasted_iota(jnp.int32, prev8.shape, 0)
    top = jnp.where(r8 < j, p, s[0:8])
    return jnp.concatenate([top, s[8:]], axis=0)


def _shift_up(cur, next8, j):
    n = cur.shape[0]
    s = pltpu.roll(cur, n - j, 0)
    p = pltpu.roll(next8, 8 - j, 0)
    r8 = lax.broadcasted_iota(jnp.int32, next8.shape, 0)
    bot = jnp.where(r8 >= 8 - j, p, s[n - 8:n])
    return jnp.concatenate([s[:n - 8], bot], axis=0)


def _scan_fwd(a, u):
    n = a.shape[0]
    row = lax.broadcasted_iota(jnp.int32, a.shape, 0)
    k = 1
    while k < n:
        ok = row >= k
        a_s = jnp.where(ok, pltpu.roll(a, k, 0), 1.0)
        u_s = jnp.where(ok, pltpu.roll(u, k, 0), 0.0)
        u = a * u_s + u
        a = a * a_s
        k *= 2
    return a, u


def _scan_bwd(b, d):
    n = b.shape[0]
    row = lax.broadcasted_iota(jnp.int32, b.shape, 0)
    k = 1
    while k < n:
        ok = row < n - k
        b_s = jnp.where(ok, pltpu.roll(b, n - k, 0), 1.0)
        d_s = jnp.where(ok, pltpu.roll(d, n - k, 0), 0.0)
        d = b * d_s + d
        b = b * b_s
        k *= 2
    return b, d


def _scan_tile(a, u, carry, a_s, u_s, reverse):
    n, c = a.shape
    groups = n // 8
    r8 = lax.broadcasted_iota(jnp.int32, a.shape, 0) & 7
    in_group = lambda x, k: pltpu.roll(x.reshape(groups, 8, c), k, 1).reshape(n, c)
    for k in (1, 2, 4):
        ok = (r8 < 8 - k) if reverse else (r8 >= k)
        shift = 8 - k if reverse else k
        a_n = jnp.where(ok, in_group(a, shift), 1.0)
        u_n = jnp.where(ok, in_group(u, shift), 0.0)
        u = a * u_n + u
        a = a * a_n
    nl = c // LANE
    for j in range(nl):
        a_s[j] = a[:, LANE * j:LANE * (j + 1)]
        u_s[j] = u[:, LANE * j:LANE * (j + 1)]
    end = 0 if reverse else 7
    ends = lambda ref, j: ref[pl.ds(j, 1), pl.ds(end, groups, stride=8), :].reshape(groups, LANE)
    ga = jnp.concatenate([ends(a_s, j) for j in range(nl)], axis=1)
    gu = jnp.concatenate([ends(u_s, j) for j in range(nl)], axis=1)
    gacc, gh = (_scan_bwd if reverse else _scan_fwd)(ga, gu)
    gh = gh + gacc * carry
    grow = lax.broadcasted_iota(jnp.int32, gh.shape, 0)
    if reverse:
        cin = jnp.where(grow == groups - 1, carry, pltpu.roll(gh, groups - 1, 0))
    else:
        cin = jnp.where(grow == 0, carry, pltpu.roll(gh, 1, 0))
    spread = ((lax.broadcasted_iota(jnp.int32, (n, LANE), 0) >> 3)
              == lax.broadcasted_iota(jnp.int32, (n, LANE), 1)).astype(BF16)
    cin = jnp.concatenate([cin, jnp.zeros((LANE - groups, c), F32)], axis=0)
    return u + a * _dot_sel(cin, spread, terms=3, sel_first=True)


def _inproj_fwd(x, g_pre, w_main, w_dt):
    T = x.shape[0]
    TT = TP

    def body(x_ref, g_ref, wm_hbm, wd_hbm, h_ref, proj_ref, dtp_ref, wm, wd, sem):
        @pl.when(pl.program_id(0) == 0)
        def _():
            c1 = pltpu.make_async_copy(wm_hbm, wm, sem.at[0])
            c2 = pltpu.make_async_copy(wd_hbm, wd, sem.at[1])
            c1.start()
            c2.start()
            c1.wait()
            c2.wait()

        xv = x_ref[...]
        h = (xv * _rms(xv) * g_ref[...]).astype(BF16)
        h_ref[...] = h
        proj_ref[...] = _dot(h, wm[...])
        dtp_ref[...] = _dot(h, wd[...])

    return pl.pallas_call(
        body, name="inproj_fwd", grid=(T // TT,),
        in_specs=[pl.BlockSpec((TT, D_MODEL), lambda i: (i, 0)),
                  pl.BlockSpec((1, D_MODEL), lambda i: (0, 0)),
                  pl.BlockSpec(memory_space=pl.ANY), pl.BlockSpec(memory_space=pl.ANY)],
        out_specs=[pl.BlockSpec((TT, D_MODEL), lambda i: (i, 0)),
                   pl.BlockSpec((TT, IN_MAIN), lambda i: (i, 0)),
                   pl.BlockSpec((TT, LANE), lambda i: (i, 0))],
        out_shape=[jax.ShapeDtypeStruct((T, D_MODEL), BF16), jax.ShapeDtypeStruct((T, IN_MAIN), F32),
                   jax.ShapeDtypeStruct((T, LANE), F32)],
        scratch_shapes=[pltpu.VMEM((D_MODEL, IN_MAIN), BF16), pltpu.VMEM((D_MODEL, LANE), BF16),
                        pltpu.SemaphoreType.DMA((2,))],
        compiler_params=_params(1),
    )(x, g_pre, w_main, w_dt)


def _inproj_bwd(dp_lru, dp_ssd, ddtp, dx1, x, g_pre, w_main, w_dt):
    T = x.shape[0]
    TT = TP

    def body(dl_ref, ds_ref, dd_ref, dx1_ref, x_ref, g_ref, wm_hbm, wd_hbm, gx_ref, dg_ref, wm, wd, sem):
        @pl.when(pl.program_id(0) == 0)
        def _():
            c1 = pltpu.make_async_copy(wm_hbm, wm, sem.at[0])
            c2 = pltpu.make_async_copy(wd_hbm, wd, sem.at[1])
            c1.start()
            c2.start()
            c1.wait()
            c2.wait()
            dg_ref[...] = jnp.zeros_like(dg_ref)

        dh = _dot(dl_ref[...], wm[:, 0:2048], NT)
        dh += _dot(ds_ref[...], wm[:, 2048:IN_MAIN], NT)
        dh += _dot(dd_ref[...], wd[...], NT)
        xv = x_ref[...]
        rn = _rms(xv)
        dg_ref[...] += jnp.sum(dh * xv * rn, axis=0, keepdims=True)
        gx_ref[...] = dx1_ref[...] + _rms_bwd(dh * g_ref[...], xv, rn)

    return pl.pallas_call(
        body, name="inproj_bwd", grid=(T // TT,),
        in_specs=[pl.BlockSpec((TT, 2048), lambda i: (i, 0)),
                  pl.BlockSpec((TT, 2560), lambda i: (i, 0)),
                  pl.BlockSpec((TT, LANE), lambda i: (i, 0)),
                  pl.BlockSpec((TT, D_MODEL), lambda i: (i, 0)),
                  pl.BlockSpec((TT, D_MODEL), lambda i: (i, 0)),
                  pl.BlockSpec((1, D_MODEL), lambda i: (0, 0)),
                  pl.BlockSpec(memory_space=pl.ANY), pl.BlockSpec(memory_space=pl.ANY)],
        out_specs=[pl.BlockSpec((TT, D_MODEL), lambda i: (i, 0)),
                   pl.BlockSpec((1, D_MODEL), lambda i: (0, 0))],
        out_shape=[jax.ShapeDtypeStruct((T, D_MODEL), F32), jax.ShapeDtypeStruct((1, D_MODEL), F32)],
        scratch_shapes=[pltpu.VMEM((D_MODEL, IN_MAIN), BF16), pltpu.VMEM((D_MODEL, LANE), BF16),
                        pltpu.SemaphoreType.DMA((2,))],
        compiler_params=_params(1),
    )(dp_lru, dp_ssd, ddtp, dx1, x, g_pre, w_main, w_dt)


def _lru_gates(lx, wa_ref, wx_ref, ba, bx, lam):
    lxb = lx.astype(BF16)
    pa = jnp.concatenate([_dot(lxb[:, 256 * k:256 * (k + 1)], wa_ref[k]) for k in range(4)], axis=1) + ba
    px = jnp.concatenate([_dot(lxb[:, 256 * k:256 * (k + 1)], wx_ref[k]) for k in range(4)], axis=1) + bx
    r = _sigmoid(pa)
    ig = _sigmoid(px)
    sp = _softplus(-lam)
    log_a = -8.0 * r * sp
    a = jnp.exp(log_a)
    mult = jnp.sqrt(_neg_expm1(2.0 * log_a))
    return r, ig, sp, a, mult


def _lru_fwd(proj, conv_w, conv_b, wa_bd, wx_bd, ba, bx, lam, g_lru):
    T = proj.shape[0]

    def body(cx_ref, gate_ref, cw_ref, cb_ref, wa_ref, wx_ref, ba_ref, bx_ref, lam_ref, g_ref,
             lx_ref, hl_ref, y_ref, r_ref, ig_ref, a_ref, mult_ref, tail, hcar, sa, su):
        @pl.when(pl.program_id(0) == 0)
        def _():
            tail[...] = jnp.zeros_like(tail)
            hcar[...] = jnp.zeros_like(hcar)

        cx = cx_ref[...]
        prev8 = tail[...]
        lx = cb_ref[...] + cw_ref[3:4, :] * cx
        for j in range(1, 4):
            lx += cw_ref[3 - j:4 - j, :] * _shift_down(cx, prev8, j)
        tail[...] = cx[TT - 8:TT]
        lx_ref[...] = lx
        r, ig, sp, a, mult = _lru_gates(lx, wa_ref, wx_ref, ba_ref[...], bx_ref[...], lam_ref[...])
        r_ref[...] = r
        ig_ref[...] = ig
        a_ref[...] = a
        mult_ref[...] = mult
        h = _scan_tile(a, mult * (ig * lx), hcar[...], sa, su, reverse=False)
        hl_ref[...] = h
        hcar[...] = hl_ref[TT - 1:TT, :]
        ge, _ = _gelu(gate_ref[...])
        p = h * ge
        y_ref[...] = (p * _rms(p) * g_ref[...]).astype(BF16)

    vec = pl.BlockSpec((1, LRU_W), lambda i: (0, 0))
    bd = pl.BlockSpec((4, 256, 256), lambda i: (0, 0, 0))
    tile = pl.BlockSpec((TT, LRU_W), lambda i: (i, 0))
    f32 = jax.ShapeDtypeStruct((T, LRU_W), F32)
    return pl.pallas_call(
        body, name="lru_fwd", grid=(T // TT,),
        in_specs=[tile, pl.BlockSpec((TT, LRU_W), lambda i: (i, 1)),
                  pl.BlockSpec((4, LRU_W), lambda i: (0, 0)), vec, bd, bd, vec, vec, vec, vec],
        out_specs=[tile] * 7,
        out_shape=[f32, f32, jax.ShapeDtypeStruct((T, LRU_W), BF16), f32, f32, f32, f32],
        scratch_shapes=[pltpu.VMEM((8, LRU_W), F32), pltpu.VMEM((1, LRU_W), F32)]
                       + [pltpu.VMEM((LRU_W // LANE, TT, LANE), F32)] * 2,
        compiler_params=_params(1),
    )(proj, proj, conv_w, conv_b, wa_bd, wx_bd, ba, bx, lam, g_lru)


def _lru_bwd(dy, proj, lx, hl, gates, conv_w, wa_bd, wx_bd, lam, g_lru):
    T = proj.shape[0]
    nt = T // TT

    def body(dy_ref, cx_ref, gate_ref, lx_ref, hl_ref, halo_ref, r_ref, ig_ref, a_ref, mult_ref, cw_ref, wa_ref,
             wx_ref, lam_ref, g_ref, dp_ref, dpa_ref, dpx_ref, dcw_ref, dcb_ref, dba_ref, dbx_ref, dlam_ref, dg_ref,
             gcar, acar, head, sa, su):
        i = pl.program_id(0)

        @pl.when(i == 0)
        def _():
            gcar[...] = jnp.zeros_like(gcar)
            acar[...] = jnp.zeros_like(acar)
            head[...] = jnp.zeros_like(head)
            for ref in (dcw_ref, dcb_ref, dba_ref, dbx_ref, dlam_ref, dg_ref):
                ref[...] = jnp.zeros_like(ref)

        lx = lx_ref[...]
        h = hl_ref[...]
        gate = gate_ref[...]
        cx = cx_ref[...]
        lam = lam_ref[...]
        r, ig, a, mult = r_ref[...], ig_ref[...], a_ref[...], mult_ref[...]
        sp = _softplus(-lam)
        ge, th = _gelu(gate)
        p = h * ge
        rn = _rms(p)
        dyv = dy_ref[...]
        dg_ref[...] += jnp.sum(dyv * p * rn, axis=0, keepdims=True)
        dp = _rms_bwd(dyv * g_ref[...], p, rn)
        dp_ref[:, LRU_W:2 * LRU_W] = (dp * h * _gelu_grad(gate, th)).astype(BF16)
        dh = dp * ge
        row = lax.broadcasted_iota(jnp.int32, a.shape, 0)
        b = jnp.where(row == TT - 1, acar[...], pltpu.roll(a, TT - 1, 0))
        g = _scan_tile(b, dh, gcar[...], sa, su, reverse=True)
        gcar[...] = _row(g[0:8], 0)
        acar[...] = _row(a[0:8], 0)
        h_last_prev = halo_ref[7:8, :] * (i < nt - 1).astype(F32)
        hprev = jnp.where(row == 0, h_last_prev, pltpu.roll(h, 1, 0))
        da = g * hprev
        dm2 = (g * (ig * lx)) * 0.5 / mult
        dlog_a = da * a - 2.0 * a * a * dm2
        dlam_ref[...] += jnp.sum(-8.0 * r * dlog_a, axis=0, keepdims=True) * (-_sigmoid(-lam))
        dpa = (-8.0 * sp * dlog_a) * r * (1.0 - r)
        dpx = (g * mult * lx) * ig * (1.0 - ig)
        dba_ref[...] += jnp.sum(dpa, axis=0, keepdims=True)
        dbx_ref[...] += jnp.sum(dpx, axis=0, keepdims=True)
        dpab = dpa.astype(BF16)
        dpxb = dpx.astype(BF16)
        dpa_ref[...] = dpab
        dpx_ref[...] = dpxb
        dlx = g * mult * ig + jnp.concatenate(
            [_dot(dpab[:, 256 * k:256 * (k + 1)], wa_ref[k], NT) + _dot(dpxb[:, 256 * k:256 * (k + 1)], wx_ref[k], NT)
             for k in range(4)], axis=1)
        nxt = head[...]
        dcb_ref[...] += jnp.sum(dlx, axis=0, keepdims=True)
        dcx = cw_ref[3:4, :] * dlx
        dcw_ref[3:4, :] += jnp.sum(cx * dlx, axis=0, keepdims=True)
        for j in range(1, 4):
            sh = _shift_up(dlx, nxt, j)
            dcx += cw_ref[3 - j:4 - j, :] * sh
            dcw_ref[3 - j:4 - j, :] += jnp.sum(cx * sh, axis=0, keepdims=True)
        head[...] = dlx[0:8]
        dp_ref[:, 0:LRU_W] = dcx.astype(BF16)

    rev = lambda i: (nt - 1 - i, 0)
    vec = pl.BlockSpec((1, LRU_W), lambda i: (0, 0))
    bd = pl.BlockSpec((4, 256, 256), lambda i: (0, 0, 0))
    tile = pl.BlockSpec((TT, LRU_W), rev)
    halo = pl.BlockSpec((8, LRU_W), lambda i: (jnp.maximum((nt - 1 - i) * (TT // 8) - 1, 0), 0))
    cw = pl.BlockSpec((4, LRU_W), lambda i: (0, 0))
    return pl.pallas_call(
        body, name="lru_bwd", grid=(nt,),
        in_specs=[tile, tile, pl.BlockSpec((TT, LRU_W), lambda i: (nt - 1 - i, 1)), tile, tile, halo,
                  tile, tile, tile, tile, cw, bd, bd, vec, vec],
        out_specs=[pl.BlockSpec((TT, 2 * LRU_W), rev), tile, tile, cw, vec, vec, vec, vec, vec],
        out_shape=[jax.ShapeDtypeStruct((T, 2 * LRU_W), BF16), jax.ShapeDtypeStruct((T, LRU_W), BF16),
                   jax.ShapeDtypeStruct((T, LRU_W), BF16), jax.ShapeDtypeStruct((4, LRU_W), F32)]
                  + [jax.ShapeDtypeStruct((1, LRU_W), F32)] * 5,
        scratch_shapes=[pltpu.VMEM((1, LRU_W), F32), pltpu.VMEM((1, LRU_W), F32), pltpu.VMEM((8, LRU_W), F32)]
                       + [pltpu.VMEM((LRU_W // LANE, TT, LANE), F32)] * 2,
        compiler_params=_params(1),
    )(dy, proj, proj, lx, hl, hl, *gates, conv_w, wa_bd, wx_bd, lam, g_lru)


def _ssd_chunk_terms(xc, dtp, bias, alog, expand):
    sg = _sigmoid(xc)
    xbc = xc * sg
    pre = dtp + bias
    dt = _softplus(pre)
    A = -jnp.exp(alog)
    ri = lax.broadcasted_iota(jnp.int32, (CHUNK, CHUNK), 0)
    ci = lax.broadcasted_iota(jnp.int32, (CHUNK, CHUNK), 1)
    tril = (ri >= ci).astype(BF16)
    cs = _dot_sel(dt * A, tril, terms=3, sel_first=True)
    cs_last = _row(cs, CHUNK - 1)
    ecs = jnp.exp(cs)
    dec = jnp.exp(cs_last - cs)
    return dict(sg=sg, xbc=xbc, pre=pre, dt=dt, A=A, cs=cs, csT=cs.T, ecs=ecs, dec=dec, ri=ri, ci=ci,
                dt_e=_dot_sel(dt, expand), ecs_e=_dot_sel(ecs, expand), dec_e=_dot_sel(dec, expand))


def _head_lambda(t, h, cst_ref):
    col = jnp.sum(jnp.where(t["ci"] == h, t["cs"], 0.0), axis=1, keepdims=True)
    return jnp.exp(jnp.where(t["ri"] >= t["ci"], col - cst_ref[h:h + 1, :], -1e30))


def _ssd_fwd(proj, dtp, conv_w, conv_b, dt_bias, a_log, d_e, g_ssd, expand):
    T = proj.shape[0]
    nc = T // CHUNK

    def body(z_ref, xp_ref, dtp_ref, cw_ref, cb_ref, bias_ref, alog_ref, de_ref, g_ref, ex_ref,
             xc_ref, y_ref, yn_ref, sprev_ref, tail, S, cst):
        @pl.when(pl.program_id(0) == 0)
        def _():
            tail[...] = jnp.zeros_like(tail)
            S[...] = jnp.zeros_like(S)

        xp = xp_ref[...]
        prev8 = tail[...]
        xc = cb_ref[...] + cw_ref[3:4, :] * xp
        for j in range(1, 4):
            xc += cw_ref[3 - j:4 - j, :] * _shift_down(xp, prev8, j)
        tail[...] = xp[CHUNK - 8:CHUNK]
        xc_ref[...] = xc
        t = _ssd_chunk_terms(xc, dtp_ref[...], bias_ref[...], alog_ref[...], ex_ref[...])
        cst[...] = t["csT"]
        xbc = t["xbc"]
        sx = xbc[:, 0:SSD_INNER]
        Bb = xbc[:, SSD_INNER:SSD_INNER + 256].astype(BF16)
        Cb = xbc[:, SSD_INNER + 256:SSD_CONV_CH].astype(BF16)
        X = t["dt_e"] * sx
        lane = lax.broadcasted_iota(jnp.int32, (CHUNK, LANE), 1)
        G = [_dot(Cb[:, 128 * g:128 * (g + 1)], Bb[:, 128 * g:128 * (g + 1)], NT) for g in range(SSD_GROUPS)]
        for k in range(SSD_HEADS // 2):
            Xp = X[:, 128 * k:128 * (k + 1)]
            acc = jnp.zeros((CHUNK, LANE), F32)
            for half in range(2):
                M = (G[k // 4] * _head_lambda(t, 2 * k + half, cst)).astype(BF16)
                Xh = jnp.where((lane >= 64) if half else (lane < 64), Xp, 0.0).astype(BF16)
                acc += _dot(M, Xh)
            y_ref[:, 128 * k:128 * (k + 1)] = acc
        sprev_ref[0] = S[...]
        eL_e = _row(t["ecs_e"], CHUNK - 1)
        Xd = (X * t["dec_e"]).astype(BF16)
        for g in range(SSD_GROUPS):
            sl = slice(512 * g, 512 * (g + 1))
            Sg = S[:, sl]
            y_ref[:, sl] += t["ecs_e"][:, sl] * _dot(Cb[:, 128 * g:128 * (g + 1)], Sg.astype(BF16))
            S[:, sl] = eL_e[:, sl] * Sg + _dot(Bb[:, 128 * g:128 * (g + 1)], Xd[:, sl], TN)
        y = y_ref[...] + de_ref[...] * sx
        y_ref[...] = y
        z = z_ref[...]
        q = y * (z * _sigmoid(z))
        yn_ref[...] = (q * _rms(q) * g_ref[...]).astype(BF16)

    c0 = lambda i: (0, 0)
    return pl.pallas_call(
        body, name="ssd_fwd", grid=(nc,),
        in_specs=[pl.BlockSpec((CHUNK, SSD_INNER), lambda i: (i, 2)),
                  pl.BlockSpec((CHUNK, SSD_CONV_CH), lambda i: (i, 2)),
                  pl.BlockSpec((CHUNK, LANE), lambda i: (i, 0)),
                  pl.BlockSpec((4, SSD_CONV_CH), c0), pl.BlockSpec((1, SSD_CONV_CH), c0),
                  pl.BlockSpec((1, LANE), c0), pl.BlockSpec((1, LANE), c0),
                  pl.BlockSpec((1, SSD_INNER), c0), pl.BlockSpec((1, SSD_INNER), c0),
                  pl.BlockSpec((LANE, SSD_INNER), c0)],
        out_specs=[pl.BlockSpec((CHUNK, SSD_CONV_CH), lambda i: (i, 0)),
                   pl.BlockSpec((CHUNK, SSD_INNER), lambda i: (i, 0)),
                   pl.BlockSpec((CHUNK, SSD_INNER), lambda i: (i, 0)),
                   pl.BlockSpec((1, SSD_STATE, SSD_INNER), lambda i: (i, 0, 0))],
        out_shape=[jax.ShapeDtypeStruct((T, SSD_CONV_CH), F32), jax.ShapeDtypeStruct((T, SSD_INNER), F32),
                   jax.ShapeDtypeStruct((T, SSD_INNER), BF16),
                   jax.ShapeDtypeStruct((nc, SSD_STATE, SSD_INNER), F32)],
        scratch_shapes=[pltpu.VMEM((8, SSD_CONV_CH), F32), pltpu.VMEM((SSD_STATE, SSD_INNER), F32),
                        pltpu.VMEM((CHUNK, CHUNK), F32)],
        compiler_params=_params(1),
    )(proj, proj, dtp, conv_w, conv_b, dt_bias, a_log, d_e, g_ssd, expand)


def _ssd_bwd(dyn, proj, dtp, xc, y, sprev, conv_w, dt_bias, a_log, d_e, g_ssd, expand):
    T = proj.shape[0]
    nc = T // CHUNK

    def body(dyn_ref, z_ref, xp_ref, dtp_ref, xc_ref, y_ref, sprev_ref, cw_ref, bias_ref, alog_ref, de_ref,
             g_ref, ex_ref, dp_ref, ddtp_ref, dcw_ref, dcb_ref, dbias_ref, dA_ref, dD_ref, dg_ref,
             dS, head, dX_s, dxbc_s, cst, dcst):
        @pl.when(pl.program_id(0) == 0)
        def _():
            dS[...] = jnp.zeros_like(dS)
            head[...] = jnp.zeros_like(head)
            dcst[...] = jnp.zeros_like(dcst)
            for ref in (dcw_ref, dcb_ref, dbias_ref, dA_ref, dD_ref, dg_ref):
                ref[...] = jnp.zeros_like(ref)

        ex = ex_ref[...]
        xc = xc_ref[...]
        t = _ssd_chunk_terms(xc, dtp_ref[...], bias_ref[...], alog_ref[...], ex)
        cst[...] = t["csT"]
        ri, ci = t["ri"], t["ci"]
        xbc = t["xbc"]
        sx = xbc[:, 0:SSD_INNER]
        Bb = xbc[:, SSD_INNER:SSD_INNER + 256].astype(BF16)
        Cb = xbc[:, SSD_INNER + 256:SSD_CONV_CH].astype(BF16)
        X = t["dt_e"] * sx
        z = z_ref[...]
        sz = _sigmoid(z)
        siluz = z * sz
        yv = y_ref[...]
        q = yv * siluz
        rn = _rms(q)
        dynv = dyn_ref[...]
        dg_ref[...] += jnp.sum(dynv * q * rn, axis=0, keepdims=True)
        dq = _rms_bwd(dynv * g_ref[...], q, rn)
        dp_ref[:, 0:SSD_INNER] = (dq * yv * (sz * (1.0 + z * (1.0 - sz)))).astype(BF16)
        dY = dq * siluz
        dD_ref[...] += jnp.sum(dY * sx, axis=0, keepdims=True)
        dYb = dY.astype(BF16)
        lane = lax.broadcasted_iota(jnp.int32, (CHUNK, LANE), 1)
        dcs = jnp.zeros((CHUNK, CHUNK), F32)
        Xb = X.astype(BF16)
        for g in range(SSD_GROUPS):
            Bg = Bb[:, 128 * g:128 * (g + 1)]
            Cg = Cb[:, 128 * g:128 * (g + 1)]
            G = _dot(Cg, Bg, NT)
            dGsum = jnp.zeros((CHUNK, CHUNK), F32)
            for k in range(4 * g, 4 * g + 4):
                Xp = Xb[:, 128 * k:128 * (k + 1)]
                dYp = dY[:, 128 * k:128 * (k + 1)]
                dXp = jnp.zeros((CHUNK, LANE), F32)
                for half in range(2):
                    h = 2 * k + half
                    lam = _head_lambda(t, h, cst)
                    M = G * lam
                    dYh = jnp.where((lane >= 64) if half else (lane < 64), dYp, 0.0).astype(BF16)
                    dM = _dot(dYh, Xp, NT)
                    W = dM * M
                    dcs += jnp.where(ci == h, jnp.sum(W, axis=1, keepdims=True), 0.0)
                    dcst[h:h + 1, :] = jnp.sum(W, axis=0, keepdims=True)
                    dGsum += dM * lam
                    dXp += _dot(M.astype(BF16), dYh, TN)
                dX_s[:, 128 * k:128 * (k + 1)] = dXp
            dGb = dGsum.astype(BF16)
            dxbc_s[:, SSD_INNER + 256 + 128 * g:SSD_INNER + 256 + 128 * (g + 1)] = _dot(dGb, Bg)
            dxbc_s[:, SSD_INNER + 128 * g:SSD_INNER + 128 * (g + 1)] = _dot(dGb, Cg, TN)
        dcs = dcs - dcst[...].T
        Sp = sprev_ref[0]
        dSv = dS[...]
        ecs_e, dec_e = t["ecs_e"], t["dec_e"]
        eL_e = _row(ecs_e, CHUNK - 1)
        dYe = dY * ecs_e
        dYeb = dYe.astype(BF16)
        Xd = X * dec_e
        Xdb = Xd.astype(BF16)
        for g in range(SSD_GROUPS):
            sl = slice(512 * g, 512 * (g + 1))
            Bg = Bb[:, 128 * g:128 * (g + 1)]
            Cg = Cb[:, 128 * g:128 * (g + 1)]
            Spb = Sp[:, sl].astype(BF16)
            dSb = dSv[:, sl].astype(BF16)
            CS = _dot(Cg, Spb)
            BS = _dot(Bg, dSb)
            dxbc_s[:, SSD_INNER + 256 + 128 * g:SSD_INNER + 256 + 128 * (g + 1)] += _dot(dYeb[:, sl], Spb, NT)
            dxbc_s[:, SSD_INNER + 128 * g:SSD_INNER + 128 * (g + 1)] += _dot(Xdb[:, sl], dSb, NT)
            dS[:, sl] = eL_e[:, sl] * dSv[:, sl] + _dot(Cg, dYeb[:, sl], TN)
            dX_s[:, sl] += dec_e[:, sl] * BS
            dcs += _dot_sel(dYe[:, sl] * CS, ex[:, sl], NT, terms=1)
            tdec = _dot_sel(X[:, sl] * BS, ex[:, sl], NT, terms=1) * t["dec"]
            dcs -= tdec
            last = jnp.sum(tdec, axis=0, keepdims=True)
            last += jnp.sum(_dot_sel(Sp[:, sl] * dSv[:, sl], ex[:, sl], NT, terms=1), axis=0, keepdims=True) \
                * _row(t["ecs"], CHUNK - 1)
            dcs += jnp.where(ri == CHUNK - 1, last, 0.0)
        triu = (ci >= ri).astype(BF16)
        da = _dot_sel(dcs, triu, terms=3, sel_first=True)
        dX = dX_s[...]
        ddt = da * t["A"] + _dot_sel(dX * sx, ex, NT, terms=1)
        dA_ref[...] += jnp.sum(da * t["dt"], axis=0, keepdims=True)
        ddtp = ddt * _sigmoid(t["pre"])
        dbias_ref[...] += jnp.sum(ddtp, axis=0, keepdims=True)
        ddtp_ref[...] = ddtp.astype(BF16)
        dxbc_s[:, 0:SSD_INNER] = dX * t["dt_e"] + de_ref[...] * dY
        sg = t["sg"]
        dxc = dxbc_s[...] * (sg * (1.0 + xc * (1.0 - sg)))
        xp = xp_ref[...]
        nxt = head[...]
        dcb_ref[...] += jnp.sum(dxc, axis=0, keepdims=True)
        dpre = cw_ref[3:4, :] * dxc
        dcw_ref[3:4, :] += jnp.sum(xp * dxc, axis=0, keepdims=True)
        for j in range(1, 4):
            sh = _shift_up(dxc, nxt, j)
            dpre += cw_ref[3 - j:4 - j, :] * sh
            dcw_ref[3 - j:4 - j, :] += jnp.sum(xp * sh, axis=0, keepdims=True)
        head[...] = dxc[0:8]
        dp_ref[:, SSD_INNER:SSD_INNER + SSD_CONV_CH] = dpre.astype(BF16)

    c0 = lambda i: (0, 0)
    rev = lambda i: (nc - 1 - i, 0)
    return pl.pallas_call(
        body, name="ssd_bwd", grid=(nc,),
        in_specs=[pl.BlockSpec((CHUNK, SSD_INNER), rev),
                  pl.BlockSpec((CHUNK, SSD_INNER), lambda i: (nc - 1 - i, 2)),
                  pl.BlockSpec((CHUNK, SSD_CONV_CH), lambda i: (nc - 1 - i, 2)),
                  pl.BlockSpec((CHUNK, LANE), rev),
                  pl.BlockSpec((CHUNK, SSD_CONV_CH), rev),
                  pl.BlockSpec((CHUNK, SSD_INNER), rev),
                  pl.BlockSpec((1, SSD_STATE, SSD_INNER), lambda i: (nc - 1 - i, 0, 0)),
                  pl.BlockSpec((4, SSD_CONV_CH), c0), pl.BlockSpec((1, LANE), c0), pl.BlockSpec((1, LANE), c0),
                  pl.BlockSpec((1, SSD_INNER), c0), pl.BlockSpec((1, SSD_INNER), c0),
                  pl.BlockSpec((LANE, SSD_INNER), c0)],
        out_specs=[pl.BlockSpec((CHUNK, 2560), rev), pl.BlockSpec((CHUNK, LANE), rev),
                   pl.BlockSpec((4, SSD_CONV_CH), c0), pl.BlockSpec((1, SSD_CONV_CH), c0),
                   pl.BlockSpec((1, LANE), c0), pl.BlockSpec((1, LANE), c0),
                   pl.BlockSpec((1, SSD_INNER), c0), pl.BlockSpec((1, SSD_INNER), c0)],
        out_shape=[jax.ShapeDtypeStruct((T, 2560), BF16), jax.ShapeDtypeStruct((T, LANE), BF16),
                   jax.ShapeDtypeStruct((4, SSD_CONV_CH), F32), jax.ShapeDtypeStruct((1, SSD_CONV_CH), F32),
                   jax.ShapeDtypeStruct((1, LANE), F32), jax.ShapeDtypeStruct((1, LANE), F32),
                   jax.ShapeDtypeStruct((1, SSD_INNER), F32), jax.ShapeDtypeStruct((1, SSD_INNER), F32)],
        scratch_shapes=[pltpu.VMEM((SSD_STATE, SSD_INNER), F32), pltpu.VMEM((8, SSD_CONV_CH), F32),
                        pltpu.VMEM((CHUNK, SSD_INNER), F32), pltpu.VMEM((CHUNK, SSD_CONV_CH), F32),
                        pltpu.VMEM((CHUNK, CHUNK), F32), pltpu.VMEM((CHUNK, CHUNK), F32)],
        compiler_params=_params(1),
    )(dyn, proj, proj, dtp, xc, y, sprev, conv_w, dt_bias, a_log, d_e, g_ssd, expand)


def _outproj_fwd(x, y_lru, y_ssd, w_out, g_pm, g_pf):
    T = x.shape[0]
    TT = TP

    def body(x_ref, yl_ref, ys_ref, wo_ref, gpm_ref, gpf_ref, mix_ref, x1_ref, h2_ref):
        mix = _dot(yl_ref[...], wo_ref[0:LRU_W, :]) + _dot(ys_ref[...], wo_ref[LRU_W:2 * LRU_W, :])
        mix_ref[...] = mix
        x1 = x_ref[...] + mix * _rms(mix) * gpm_ref[...]
        x1_ref[...] = x1
        h2_ref[...] = (x1 * _rms(x1) * gpf_ref[...]).astype(BF16)

    tile = pl.BlockSpec((TT, D_MODEL), lambda i: (i, 0))
    vec = pl.BlockSpec((1, D_MODEL), lambda i: (0, 0))
    return pl.pallas_call(
        body, name="outproj_fwd", grid=(T // TT,),
        in_specs=[tile, tile, tile, pl.BlockSpec((2 * LRU_W, D_MODEL), lambda i: (0, 0)), vec, vec],
        out_specs=[tile, tile, tile],
        out_shape=[jax.ShapeDtypeStruct((T, D_MODEL), F32), jax.ShapeDtypeStruct((T, D_MODEL), F32),
                   jax.ShapeDtypeStruct((T, D_MODEL), BF16)],
        compiler_params=_params(1),
    )(x, y_lru, y_ssd, w_out, g_pm, g_pf)


def _ffn_fwd_bwd(x1, h2, target, w_gate, w_up, w_down, g_pf, g_ff):
    T = x1.shape[0]

    def body(x1_ref, h2_ref, tg_ref, wg_hbm, wu_hbm, wd_hbm, gpf_ref, gff_ref,
             dx1_ref, act_ref, df_ref, dgt_ref, dup_ref, dgpf_ref, dgff_ref, loss_ref,
             wg, wu, wd, gt_s, up_s, sem):
        @pl.when(pl.program_id(0) == 0)
        def _():
            cps = [pltpu.make_async_copy(s, d, sem.at[n]) for n, (s, d) in
                   enumerate(((wg_hbm, wg), (wu_hbm, wu), (wd_hbm, wd)))]
            for c in cps:
                c.start()
            for c in cps:
                c.wait()
            for ref in (dgpf_ref, dgff_ref, loss_ref):
                ref[...] = jnp.zeros_like(ref)

        h2 = h2_ref[...]
        f = jnp.zeros((TT, D_MODEL), F32)
        for c in range(D_FF // FF_CHUNK):
            sl = slice(FF_CHUNK * c, FF_CHUNK * (c + 1))
            gt = _dot(h2, wg[:, sl])
            up = _dot(h2, wu[:, sl])
            gt_s[:, sl] = gt
            up_s[:, sl] = up
            act = (gt * _sigmoid(gt) * up).astype(BF16)
            act_ref[:, sl] = act
            f += _dot(act, wd[sl, :])
        x1 = x1_ref[...]
        rnf = _rms(f)
        e = x1 + f * rnf * gff_ref[...] - tg_ref[...]
        part = 0.5 * jnp.sum(jnp.sum(e * e, axis=1, keepdims=True), axis=0, keepdims=True) * (1.0 / D_MODEL)
        lane = lax.broadcasted_iota(jnp.int32, (1, LANE), 1)
        loss_ref[...] += jnp.where(lane == 0, part, 0.0)
        dx2 = e * (1.0 / D_MODEL)
        dgff_ref[...] += jnp.sum(dx2 * f * rnf, axis=0, keepdims=True)
        df = _rms_bwd(dx2 * gff_ref[...], f, rnf).astype(BF16)
        df_ref[...] = df
        dh2 = jnp.zeros((TT, D_MODEL), F32)
        for c in range(D_FF // FF_CHUNK):
            sl = slice(FF_CHUNK * c, FF_CHUNK * (c + 1))
            dact = _dot(df, wd[sl, :], NT)
            gt = gt_s[:, sl]
            up = up_s[:, sl]
            sg = _sigmoid(gt)
            dgt = (dact * up * (sg * (1.0 + gt * (1.0 - sg)))).astype(BF16)
            dup = (dact * gt * sg).astype(BF16)
            dgt_ref[:, sl] = dgt
            dup_ref[:, sl] = dup
            dh2 += _dot(dgt, wg[:, sl], NT) + _dot(dup, wu[:, sl], NT)
        rn2 = _rms(x1)
        dgpf_ref[...] += jnp.sum(dh2 * x1 * rn2, axis=0, keepdims=True)
        dx1_ref[...] = dx2 + _rms_bwd(dh2 * gpf_ref[...], x1, rn2)

    tile = pl.BlockSpec((TT, D_MODEL), lambda i: (i, 0))
    wide = pl.BlockSpec((TT, D_FF), lambda i: (i, 0))
    vec = pl.BlockSpec((1, D_MODEL), lambda i: (0, 0))
    hbm = pl.BlockSpec(memory_space=pl.ANY)
    return pl.pallas_call(
        body, name="ffn_fwd_bwd", grid=(T // TT,),
        in_specs=[tile, tile, tile, hbm, hbm, hbm, vec, vec],
        out_specs=[tile, wide, tile, wide, wide, vec, vec, pl.BlockSpec((1, LANE), lambda i: (0, 0))],
        out_shape=[jax.ShapeDtypeStruct((T, D_MODEL), F32), jax.ShapeDtypeStruct((T, D_FF), BF16),
                   jax.ShapeDtypeStruct((T, D_MODEL), BF16), jax.ShapeDtypeStruct((T, D_FF), BF16),
                   jax.ShapeDtypeStruct((T, D_FF), BF16), jax.ShapeDtypeStruct((1, D_MODEL), F32),
                   jax.ShapeDtypeStruct((1, D_MODEL), F32), jax.ShapeDtypeStruct((1, LANE), F32)],
        scratch_shapes=[pltpu.VMEM((D_MODEL, D_FF), BF16), pltpu.VMEM((D_MODEL, D_FF), BF16),
                        pltpu.VMEM((D_FF, D_MODEL), BF16), pltpu.VMEM((TT, D_FF), F32),
                        pltpu.VMEM((TT, D_FF), F32), pltpu.SemaphoreType.DMA((3,))],
        compiler_params=_params(1),
    )(x1, h2, target, w_gate, w_up, w_down, g_pf, g_ff)


def _outproj_bwd(dx1, mix, w_out, g_pm):
    T = dx1.shape[0]
    TT = TP

    def body(dx1_ref, mix_ref, wo_ref, gpm_ref, dyl_ref, dys_ref, dmix_ref, dg_ref):
        @pl.when(pl.program_id(0) == 0)
        def _():
            dg_ref[...] = jnp.zeros_like(dg_ref)

        mix = mix_ref[...]
        rn = _rms(mix)
        dx1v = dx1_ref[...]
        dg_ref[...] += jnp.sum(dx1v * mix * rn, axis=0, keepdims=True)
        dmix = _rms_bwd(dx1v * gpm_ref[...], mix, rn).astype(BF16)
        dmix_ref[...] = dmix
        dyl_ref[...] = _dot(dmix, wo_ref[0:LRU_W, :], NT)
        dys_ref[...] = _dot(dmix, wo_ref[LRU_W:2 * LRU_W, :], NT)

    tile = pl.BlockSpec((TT, D_MODEL), lambda i: (i, 0))
    vec = pl.BlockSpec((1, D_MODEL), lambda i: (0, 0))
    return pl.pallas_call(
        body, name="outproj_bwd", grid=(T // TT,),
        in_specs=[tile, tile, pl.BlockSpec((2 * LRU_W, D_MODEL), lambda i: (0, 0)), vec],
        out_specs=[tile, tile, tile, vec],
        out_shape=[jax.ShapeDtypeStruct((T, D_MODEL), F32), jax.ShapeDtypeStruct((T, D_MODEL), F32),
                   jax.ShapeDtypeStruct((T, D_MODEL), BF16), jax.ShapeDtypeStruct((1, D_MODEL), F32)],
        compiler_params=_params(1),
    )(dx1, mix, w_out, g_pm)


def _tn_matmul(a, bs, name, tk=512):
    T, M = a.shape
    nk = T // tk
    nb = len(bs)

    def body(*refs):
        a_ref, b_refs, o_refs, accs = refs[0], refs[1:1 + nb], refs[1 + nb:1 + 2 * nb], refs[1 + 2 * nb:]
        k = pl.program_id(0)

        @pl.when(k == 0)
        def _():
            for acc in accs:
                acc[...] = jnp.zeros_like(acc)

        av = a_ref[...].astype(BF16)
        for b_ref, acc in zip(b_refs, accs):
            acc[...] += _dot(av, b_ref[...], TN)

        @pl.when(k == nk - 1)
        def _():
            for o_ref, acc in zip(o_refs, accs):
                o_ref[...] = acc[...].astype(BF16)

    return pl.pallas_call(
        body, name=name, grid=(nk,),
        in_specs=[pl.BlockSpec((tk, M), lambda k: (k, 0))]
                 + [pl.BlockSpec((tk, b.shape[1]), lambda k: (k, 0)) for b in bs],
        out_specs=[pl.BlockSpec((M, b.shape[1]), lambda k: (0, 0)) for b in bs],
        out_shape=[jax.ShapeDtypeStruct((M, b.shape[1]), BF16) for b in bs],
        scratch_shapes=[pltpu.VMEM((M, b.shape[1]), F32) for b in bs],
        compiler_params=_params(1),
    )(a, *bs)


def _tn_matmul_slabs(a, bs, name, slab, tk=512):
    T, M = a.shape
    nk = T // tk
    nb = len(bs)
    offs = [sum(b.shape[1] for b in bs[:i]) for i in range(nb + 1)]

    def body(*refs):
        a_ref, b_refs, o_ref, acc = refs[0], refs[1:1 + nb], refs[1 + nb], refs[2 + nb]
        k = pl.program_id(0)

        @pl.when(k == 0)
        def _():
            acc[...] = jnp.zeros_like(acc)

        av = a_ref[...].astype(BF16)
        for i, b_ref in enumerate(b_refs):
            acc[:, offs[i]:offs[i + 1]] += _dot(av, b_ref[...], TN)

        @pl.when(k == nk - 1)
        def _():
            for j in range(N_DEV):
                o_ref[j] = acc[:, slab * j:slab * (j + 1)].astype(BF16)

    return pl.pallas_call(
        body, name=name, grid=(nk,),
        in_specs=[pl.BlockSpec((tk, M), lambda k: (k, 0))]
                 + [pl.BlockSpec((tk, b.shape[1]), lambda k: (k, 0)) for b in bs],
        out_specs=pl.BlockSpec((N_DEV, M, slab), lambda k: (0, 0, 0)),
        out_shape=jax.ShapeDtypeStruct((N_DEV, M, slab), BF16),
        scratch_shapes=[pltpu.VMEM((M, offs[-1]), F32)],
        compiler_params=_params(1),
    )(a, *bs)


def _tn_blockdiag(a, b1, b2, name, tk=1024):
    T = a.shape[0]
    tk = min(tk, T)

    def body(a_ref, b1_ref, b2_ref, o1_ref, o2_ref):
        @pl.when(pl.program_id(0) == 0)
        def _():
            o1_ref[...] = jnp.zeros_like(o1_ref)
            o2_ref[...] = jnp.zeros_like(o2_ref)

        for j in range(4):
            sl = slice(256 * j, 256 * (j + 1))
            av = a_ref[:, sl].astype(BF16)
            o1_ref[j] += _dot(av, b1_ref[:, sl], TN)
            o2_ref[j] += _dot(av, b2_ref[:, sl], TN)

    blk = pl.BlockSpec((tk, LRU_W), lambda k: (k, 0))
    out = pl.BlockSpec((4, 256, 256), lambda k: (0, 0, 0))
    return pl.pallas_call(
        body, name=name, grid=(T // tk,),
        in_specs=[blk, blk, blk], out_specs=[out, out],
        out_shape=[jax.ShapeDtypeStruct((4, 256, 256), F32)] * 2,
        compiler_params=_params(1),
    )(a, b1, b2)


def _adamw_update(g, w, m, v):
    nm = ADAM_B1 * m + (1.0 - ADAM_B1) * g
    nv = ADAM_B2 * v + (1.0 - ADAM_B2) * (g * g)
    m_hat = nm / (1.0 - ADAM_B1 ** ADAM_STEP)
    v_hat = nv / (1.0 - ADAM_B2 ** ADAM_STEP)
    return -ADAM_LR * (m_hat / (jnp.sqrt(v_hat) + ADAM_EPS) + ADAM_WD * w), nm, nv


def _adamw_small(parts, params):
    n = len(params)

    def body(*refs):
        p_ref, wmv = refs[0], refs[1:1 + 3 * n]
        gs_ref, outs = refs[1 + 3 * n], refs[2 + 3 * n:]
        g = p_ref[0]
        for k in range(1, N_DEV):
            g = g + p_ref[k]
        gs_ref[...] = g
        for i, (row, w, _, _) in enumerate(params):
            rows, width = w.shape
            if width <= 1024:
                gp = gs_ref[row:row + rows, 0:width]
            else:
                gp = jnp.concatenate([gs_ref[row:row + 1, :], gs_ref[row + 1:row + 2, 0:width - 1024]], axis=1)
            d, nm, nv = _adamw_update(gp, wmv[3 * i][...], wmv[3 * i + 1][...], wmv[3 * i + 2][...])
            for ref, val in zip(outs[4 * i:4 * i + 4], (gp, d, nm, nv)):
                ref[...] = val

    full = lambda s: pl.BlockSpec(s, lambda: (0,) * len(s))
    shapes = [w.shape for _, w, _, _ in params]
    return pl.pallas_call(
        body, name="adamw_small",
        in_specs=[full(parts.shape)] + [full(s) for s in shapes for _ in range(3)],
        out_specs=[full(parts.shape[1:])] + [full(s) for s in shapes for _ in range(4)],
        out_shape=[jax.ShapeDtypeStruct(parts.shape[1:], F32)]
                  + [jax.ShapeDtypeStruct(s, F32) for s in shapes for _ in range(4)],
        compiler_params=pltpu.CompilerParams(vmem_limit_bytes=VMEM_LIMIT),
    )(parts, *[a for _, w, m, v in params for a in (w, m, v)])


def _adamw_shard(lands, slabs, me, w, m, v, name, tr):
    P, R, C = lands.shape

    def body(me_ref, p_ref, own_ref, w_ref, m_ref, v_ref, g_ref, d_ref, nm_ref, nv_ref):
        g = None
        for k in range(P):
            part = jnp.where(me_ref[0] == k, own_ref[0], p_ref[k]).astype(F32)
            g = part if g is None else g + part
        g_ref[...] = g
        d_ref[...], nm_ref[...], nv_ref[...] = _adamw_update(g, w_ref[...], m_ref[...], v_ref[...])

    blk = pl.BlockSpec((tr, C), lambda i, me_ref: (i, 0))
    return pl.pallas_call(
        body, name=name,
        grid_spec=pltpu.PrefetchScalarGridSpec(
            num_scalar_prefetch=1, grid=(R // tr,),
            in_specs=[pl.BlockSpec((P, tr, C), lambda i, me_ref: (0, i, 0)),
                      pl.BlockSpec((1, tr, C), lambda i, me_ref: (me_ref[0], i, 0)), blk, blk, blk],
            out_specs=[blk, blk, blk, blk]),
        out_shape=[jax.ShapeDtypeStruct((R, C), F32)] * 4,
        compiler_params=_params(1),
    )(me, lands, slabs, w, m, v)


def _adamw(parts, w, m, v, name, tr):
    P, R, C = parts.shape

    def body(p_ref, w_ref, m_ref, v_ref, g_ref, d_ref, nm_ref, nv_ref):
        g = p_ref[0].astype(F32)
        for k in range(1, P):
            g = g + p_ref[k].astype(F32)
        g_ref[...] = g
        d_ref[...], nm_ref[...], nv_ref[...] = _adamw_update(g, w_ref[...], m_ref[...], v_ref[...])

    blk = pl.BlockSpec((tr, C), lambda i: (i, 0))
    return pl.pallas_call(
        body, name=name, grid=(R // tr,),
        in_specs=[pl.BlockSpec((P, tr, C), lambda i: (0, i, 0)), blk, blk, blk],
        out_specs=[blk, blk, blk, blk],
        out_shape=[jax.ShapeDtypeStruct((R, C), F32)] * 4,
        compiler_params=_params(1),
    )(parts, w, m, v)


def _peer(k):
    x, y, c = lax.axis_index("x"), lax.axis_index("y"), lax.axis_index("c")
    px = x ^ ((k >> 2) & 1)
    py = y ^ ((k >> 1) & 1)
    pc = c ^ (k & 1)
    return (px, py, pc), 4 * px + 2 * py + pc


def _my_block():
    return 4 * lax.axis_index("x") + 2 * lax.axis_index("y") + lax.axis_index("c")


def _all_gather(shards, name):
    n = len(shards)

    def body(*refs):
        ins, outs = refs[:n], refs[n:2 * n]
        send, recv, loc = refs[2 * n:]
        x, y, c = lax.axis_index("x"), lax.axis_index("y"), lax.axis_index("c")
        sibling = (x, y, 1 - c)
        chips = [(1 - x, y), (x, 1 - y), (1 - x, 1 - y)]
        slot = lambda px, py, pc: 4 * px + 2 * py + pc

        def copy(a, k, block, to, src=None):
            dst = outs[a].at[slot(*block)]
            return pltpu.make_async_remote_copy(
                src_ref=dst if src is None else src, dst_ref=dst, send_sem=send.at[a, k], recv_sem=recv.at[a, k],
                device_id=to, device_id_type=pl.DeviceIdType.MESH)

        mine = [pltpu.make_async_copy(ins[a], outs[a].at[slot(x, y, c)], loc.at[a]) for a in range(n)]
        for cp in mine:
            cp.start()
        first = []
        for a in range(n):
            first.append(copy(a, 0, (x, y, c), sibling, src=ins[a]))
            first += [copy(a, 1 + j, (x, y, c), (*chip, c), src=ins[a]) for j, chip in enumerate(chips)]
        for cp in first:
            cp.start()
        passed = []
        for j, chip in enumerate(chips):
            for a in range(n):
                copy(a, 1 + j, (*chip, c), (x, y, c)).wait_recv()
                fwd = copy(a, 4 + j, (*chip, c), sibling)
                fwd.start()
                passed.append(fwd)
        for a in range(n):
            copy(a, 0, sibling, (x, y, c)).wait_recv()
            for j, chip in enumerate(chips):
                copy(a, 4 + j, (*chip, 1 - c), (x, y, c)).wait_recv()
        for cp in first + passed:
            cp.wait_send()
        for cp in mine:
            cp.wait()

    hbm = pl.BlockSpec(memory_space=pl.ANY)
    return pl.pallas_call(
        body, name=name,
        in_specs=[hbm] * n, out_specs=[hbm] * n,
        out_shape=[jax.ShapeDtypeStruct((N_DEV,) + s.shape, s.dtype) for s in shards],
        scratch_shapes=[pltpu.SemaphoreType.DMA((n, N_DEV - 1)), pltpu.SemaphoreType.DMA((n, N_DEV - 1)),
                        pltpu.SemaphoreType.DMA((n,))],
    )(*shards)


_HBM = pl.BlockSpec(memory_space=pltpu.HBM)
_SEM = pl.BlockSpec(memory_space=pltpu.SEMAPHORE)
_EFFECT = pltpu.SideEffectType.DATAFLOW_SIDE_EFFECTING


def _direct_copies(srcs, lands, send, recv, slab_source):
    me = _my_block()
    cps = []
    for k in range(1, N_DEV):
        to, blk = _peer(k)
        for a, (src, land) in enumerate(zip(srcs, lands)):
            if slab_source:
                s, d = src.at[blk], land.at[me]
            elif land.ndim == 3:
                s, d = src, land.at[me]
            else:
                s, d = src, land.at[pl.ds(pl.multiple_of(me * src.shape[0], 16), src.shape[0]), :]
            cps.append(pltpu.make_async_remote_copy(
                src_ref=s, dst_ref=d,
                send_sem=send.at[a * (N_DEV - 1) + k - 1], recv_sem=recv.at[a * (N_DEV - 1) + k - 1],
                device_id=to, device_id_type=pl.DeviceIdType.MESH))
    return cps


def _exchange_start(srcs, name, slab_source, axes=None):
    n = len(srcs)
    if slab_source:
        shapes = [s.shape for s in srcs]
    else:
        shapes = [(N_DEV,) + s.shape if ax == 1 else (N_DEV * s.shape[0], s.shape[1]) for s, ax in zip(srcs, axes)]
    lands = [pltpu.with_memory_space_constraint(lax.empty(sh, s.dtype), pltpu.HBM) for sh, s in zip(shapes, srcs)]

    def body(*refs):
        ins, land_in = refs[:n], refs[n:2 * n]
        send, recv = refs[2 * n], refs[2 * n + 1]
        token = refs[4 * n + 2]
        for cp in _direct_copies(ins, land_in, send, recv, slab_source):
            cp.start()
        token[...] = jnp.zeros_like(token)

    sems = pltpu.SemaphoreType.DMA((n * (N_DEV - 1),))
    res = pl.pallas_call(
        body, name=name,
        out_shape=(sems, sems, *[pltpu.HBM(s.shape, s.dtype) for s in srcs],
                   *[pltpu.HBM(l.shape, l.dtype) for l in lands], jax.ShapeDtypeStruct((8, LANE), F32)),
        in_specs=[_HBM] * (2 * n),
        out_specs=(_SEM, _SEM, *[_HBM] * (2 * n), pl.BlockSpec(memory_space=pltpu.VMEM)),
        input_output_aliases={i: 2 + i for i in range(2 * n)},
        compiler_params=pltpu.CompilerParams(has_side_effects=_EFFECT),
    )(*[pltpu.with_memory_space_constraint(s, pltpu.HBM) for s in srcs], *lands)
    return dict(send=res[0], recv=res[1], srcs=res[2:2 + n], lands=res[2 + n:2 + 2 * n], token=res[-1],
                slab_source=slab_source)


def _exchange_wait(ex, after, name):
    n = len(ex["srcs"])
    slab_source = ex["slab_source"]

    def body(*refs):
        ins, lands = refs[:n], refs[n:2 * n]
        send, recv = refs[2 * n], refs[2 * n + 1]
        for cp in _direct_copies(ins, lands, send, recv, slab_source):
            cp.wait_send()
            cp.wait_recv()

    res = pl.pallas_call(
        body, name=name,
        out_shape=tuple(pltpu.HBM(s.shape, s.dtype) for s in list(ex["srcs"]) + list(ex["lands"])),
        in_specs=[_HBM] * (2 * n) + [_SEM, _SEM, pl.BlockSpec(memory_space=pl.ANY)],
        out_specs=tuple([_HBM] * (2 * n)),
        input_output_aliases={i: i for i in range(2 * n)},
        compiler_params=pltpu.CompilerParams(has_side_effects=_EFFECT),
    )(*ex["srcs"], *ex["lands"], ex["send"], ex["recv"], after)
    if slab_source:
        return list(zip(res[n:], res[:n]))
    me = _my_block()
    out = []
    for src, land in zip(res[:n], res[n:]):
        if land.ndim == 3:
            own, at = src[None], (me, 0, 0)
        else:
            own, at = src, (me * src.shape[0], 0)
        out.append(lax.dynamic_update_slice(land, own, at))
    return out


BIG = ("w_in", "w_out", "w_gate", "w_up", "w_down")
BIG_SHARD = {"w_in": (1024, 578), "w_out": (256, 1024), "w_gate": (1024, 352), "w_up": (1024, 352),
             "w_down": (352, 1024)}
BIG_SHARD_AXIS = {"w_in": 1, "w_out": 0, "w_gate": 1, "w_up": 1, "w_down": 0}
BIG_ADAM_ROWS = {"w_in": 256, "w_out": 128, "w_gate": 256, "w_up": 256, "w_down": 176}


def _join(parts, axis):
    if axis == 0:
        return parts.reshape((-1,) + parts.shape[2:])
    return jnp.concatenate([parts[j] for j in range(N_DEV)], axis=1)


def _split(full, axis):
    if axis == 0:
        return full.reshape((N_DEV, full.shape[0] // N_DEV) + full.shape[1:])
    c = full.shape[1] // N_DEV
    return jnp.stack([full[:, c * j:c * (j + 1)] for j in range(N_DEV)])


SMALL = (("lru_wa", 65536), ("lru_wx", 65536), ("pre_mix_norm", 1024), ("lru_conv_w", 4096), ("lru_conv_b", 1024),
         ("lru_ba", 1024), ("lru_bx", 1024), ("lru_lambda", 1024), ("lru_out_norm", 1024), ("ssd_conv_w", 6144),
         ("ssd_conv_b", 1536), ("ssd_dt_bias", 16), ("ssd_a_log", 16), ("ssd_d", 16), ("ssd_out_norm", 1024),
         ("post_mix_norm", 1024), ("pre_ffn_norm", 1024), ("post_ffn_norm", 1024), ("loss", 1))
REPLICATED = tuple(n for n, _ in SMALL if n not in ("lru_conv_w", "ssd_conv_w", "loss"))
SMALL_ROW = {}
for _name, _size in SMALL:
    SMALL_ROW[_name] = (sum(-(-s // 1024) for n, s in SMALL[:len(SMALL_ROW)]), -(-_size // 1024))


def _pack_small(d):
    rows = [jnp.pad(d[n].astype(F32).reshape(-1), (0, SMALL_ROW[n][1] * 1024 - s)).reshape(-1, 1024) for n, s in SMALL]
    used = sum(r.shape[0] for r in rows)
    return jnp.concatenate(rows + [jnp.zeros((SMALL_ROWS - used, 1024), F32)], axis=0)


def _small_entry(p, name):
    row, rows = SMALL_ROW[name]
    return p[row:row + rows].reshape(-1)[:dict(SMALL)[name]]


def _blockdiag4(w):
    on_diag = jnp.eye(4, dtype=w.dtype)[None, :, None, :, None]
    return (w.reshape(4, 4, 64, 1, 64) * on_diag).reshape(4, 256, 256)


def _diag_blocks(g):
    on_diag = jnp.eye(4, dtype=g.dtype)[None, :, None, :, None]
    return jnp.sum(g.reshape(4, 4, 64, 4, 64) * on_diag, axis=3).reshape(16, 64, 64)


def _local_step(x, target, w_in, P, rest_weights, emit, emit_small, start_token=None):
    cut = IN_MAIN - (N_DEV - 1) * (IN_COLS // N_DEV)
    w_main = jnp.concatenate([w_in[j] for j in range(N_DEV - 1)] + [w_in[N_DEV - 1][:, :cut]], axis=1)
    w_dt = jnp.pad(w_in[N_DEV - 1][:, cut:], ((0, 0), (0, LANE - SSD_HEADS)))
    pad16 = lambda v: jnp.pad(v.reshape(1, SSD_HEADS), ((0, 0), (0, LANE - SSD_HEADS)))
    dt_bias, a_log = pad16(P["ssd_dt_bias"]), pad16(P["ssd_a_log"])
    d_e = jnp.repeat(P["ssd_d"].reshape(SSD_HEADS), SSD_HEAD_DIM).reshape(1, SSD_INNER)
    expand = (jnp.arange(LANE)[:, None] == (jnp.arange(SSD_INNER)[None, :] // SSD_HEAD_DIM)).astype(BF16)
    wa_bd = _blockdiag4(P["lru_wa"].astype(BF16))
    wx_bd = _blockdiag4(P["lru_wx"].astype(BF16))
    vec = lambda n: P[n].reshape(1, -1)

    after = lambda v, tok: v if tok is None else v + tok[0:1, 0:1]

    h, proj, dtp = _inproj_fwd(x, after(vec("pre_mix_norm"), start_token), w_main, w_dt)
    lx, hl, y_lru, *gates = _lru_fwd(proj, P["lru_conv_w"], vec("lru_conv_b"), wa_bd, wx_bd, vec("lru_ba"),
                                     vec("lru_bx"), vec("lru_lambda"), vec("lru_out_norm"))
    xc, y, y_ssd, sprev = _ssd_fwd(proj, dtp, P["ssd_conv_w"], vec("ssd_conv_b"), dt_bias, a_log, d_e,
                                   vec("ssd_out_norm"), expand)
    W = rest_weights(y_ssd)
    mix, x1, h2 = _outproj_fwd(x, y_lru, y_ssd, W["w_out"], vec("post_mix_norm"), vec("pre_ffn_norm"))
    dx1, act, df, dgt, dup, dg_pf, dg_ff, loss = _ffn_fwd_bwd(
        x1, h2, target, W["w_gate"], W["w_up"], W["w_down"], vec("pre_ffn_norm"), vec("post_ffn_norm"))
    tok = emit("ffn", {"w_gate": _tn_matmul_slabs(h2, [dgt], "dw_gate", D_FF // N_DEV),
                       "w_up": _tn_matmul_slabs(h2, [dup], "dw_up", D_FF // N_DEV),
                       "w_down": _tn_matmul(act, [df], "dw_down")[0]})
    dy_lru, dy_ssd, dmix, dg_pm = _outproj_bwd(dx1, mix, W["w_out"], after(vec("post_mix_norm"), tok))
    tok = emit("out", {"w_out": jnp.concatenate([_tn_matmul(y_lru, [dmix], "dw_out_lru")[0],
                                                 _tn_matmul(y_ssd, [dmix], "dw_out_ssd")[0]], axis=0)})
    dp_ssd, ddtp, dcw_s, dcb_s, dbias, dA, dD_e, dg_ssd = _ssd_bwd(
        dy_ssd, proj, dtp, xc, y, sprev, P["ssd_conv_w"], dt_bias, a_log, d_e,
        after(vec("ssd_out_norm"), tok), expand)
    dp_lru, dpa, dpx, dcw_l, dcb_l, dba, dbx, dlam, dg_lru = _lru_bwd(
        dy_lru, proj, lx, hl, gates, P["lru_conv_w"], wa_bd, wx_bd, vec("lru_lambda"), vec("lru_out_norm"))
    tok = emit("in", {"w_in": _tn_matmul_slabs(h, [dp_lru, dp_ssd, ddtp], "dw_in", IN_COLS // N_DEV)})

    a_neg = -jnp.exp(P["ssd_a_log"].reshape(SSD_HEADS))
    dwa, dwx = _tn_blockdiag(lx, dpa, dpx, "dw_lru_gates")
    small = {
        "pre_mix_norm": jnp.zeros((1, D_MODEL), F32), "lru_conv_w": dcw_l, "lru_conv_b": dcb_l,
        "lru_wa": _diag_blocks(dwa), "lru_ba": dba,
        "lru_wx": _diag_blocks(dwx), "lru_bx": dbx,
        "lru_lambda": dlam, "lru_out_norm": dg_lru, "ssd_conv_w": dcw_s, "ssd_conv_b": dcb_s,
        "ssd_dt_bias": dbias[0, :SSD_HEADS], "ssd_a_log": dA[0, :SSD_HEADS] * a_neg,
        "ssd_d": jnp.sum(dD_e.reshape(SSD_HEADS, SSD_HEAD_DIM), axis=1), "ssd_out_norm": dg_ssd,
        "post_mix_norm": dg_pm, "pre_ffn_norm": dg_pf, "post_ffn_norm": dg_ff, "loss": loss[0, 0:1],
    }
    tok = after(after(vec("pre_mix_norm"), tok), emit_small(small))
    grad_x, dg_pre = _inproj_bwd(dp_lru, dp_ssd, ddtp, dx1, x, tok, w_main, w_dt)
    return grad_x, dg_pre


def kernel(x, pre_mix_norm, w_in, lru_conv_w, lru_conv_b, lru_wa, lru_ba, lru_wx, lru_bx, lru_lambda, lru_out_norm, ssd_conv_w, ssd_conv_b, ssd_dt_bias, ssd_a_log, ssd_d, ssd_out_norm, w_out, post_mix_norm, pre_ffn_norm, w_gate, w_up, w_down, post_ffn_norm, loss_target, m_pre_mix_norm, m_w_in, m_lru_conv_w, m_lru_conv_b, m_lru_wa, m_lru_ba, m_lru_wx, m_lru_bx, m_lru_lambda, m_lru_out_norm, m_ssd_conv_w, m_ssd_conv_b, m_ssd_dt_bias, m_ssd_a_log, m_ssd_d, m_ssd_out_norm, m_w_out, m_post_mix_norm, m_pre_ffn_norm, m_w_gate, m_w_up, m_w_down, m_post_ffn_norm, v_pre_mix_norm, v_w_in, v_lru_conv_w, v_lru_conv_b, v_lru_wa, v_lru_ba, v_lru_wx, v_lru_bx, v_lru_lambda, v_lru_out_norm, v_ssd_conv_w, v_ssd_conv_b, v_ssd_dt_bias, v_ssd_a_log, v_ssd_d, v_ssd_out_norm, v_w_out, v_post_mix_norm, v_pre_ffn_norm, v_w_gate, v_w_up, v_w_down, v_post_ffn_norm):
    a = dict(locals())
    names = [n for n, _ in SMALL if n != "loss"] + list(BIG)
    w = {n: a[n][0] for n in names}
    m = {n: a["m_" + n][0] for n in names}
    v = {n: a["v_" + n][0] for n in names}

    cpack = jnp.concatenate([w["lru_conv_w"], w["ssd_conv_w"], jnp.zeros((4, 64), F32)], axis=1)
    cpack = jnp.pad(cpack, ((0, 4), (0, 0)))
    g_in, cg = _all_gather([w["w_in"].astype(BF16), cpack], "all_gather_w_in")
    P = {n: w[n] for n in REPLICATED}
    P["lru_conv_w"] = _join(cg[:, 0:4, 0:128], 1)
    P["ssd_conv_w"] = _join(cg[:, 0:4, 128:320], 1)

    rest = [n for n in BIG if n != "w_in"]
    zero = jnp.minimum(jnp.abs(cg[0, 0, 0]), 0.0)
    ex_w = _exchange_start([(w[n] + zero).astype(BF16) for n in rest], "weights_start", slab_source=False,
                           axes=[BIG_SHARD_AXIS[n] for n in rest])

    def rest_weights(after):
        lands = _exchange_wait(ex_w, after, "weights_wait")
        return {n: _join(p, 1) if p.ndim == 3 else p for n, p in zip(rest, lands)}

    pending = []

    def emit(group, grads):
        ex = _exchange_start([g if g.ndim == 3 else _split(g, BIG_SHARD_AXIS[n]) for n, g in grads.items()],
                             "grads_start_" + group, slab_source=True)
        pending.append((group, list(grads), ex))
        return ex["token"]

    def emit_small(small):
        ex = _exchange_start([_pack_small(small)], "small_start", slab_source=False, axes=[0])
        pending.append(("small", None, ex))
        return ex["token"]

    grad_x, dg_pre = _local_step(x[0], loss_target[0], g_in, P, rest_weights, emit, emit_small, ex_w["token"])

    ex_small = pending.pop()[2]
    ex_pre = _exchange_start([jnp.pad(dg_pre, ((0, 7), (0, 0)))], "pre_mix_norm_start", slab_source=False, axes=[0])

    outs = {}
    done = ex_pre["token"]
    me1 = _my_block().astype(jnp.int32).reshape(1)
    for group, group_names, ex in pending:
        for n, (lands, slabs) in zip(group_names, _exchange_wait(ex, done, "grads_wait_" + group)):
            res = _adamw_shard(lands, slabs, me1, w[n], m[n], v[n], "adamw_" + n, BIG_ADAM_ROWS[n])
            done = res[0]
            for kind, r in zip(("grad", "delta", "new_m", "new_v"), res):
                outs[kind + "_" + n] = r

    got_pre = _exchange_wait(ex_pre, done, "pre_mix_norm_wait")[0].reshape(N_DEV, 8, 1024)
    got_small = _exchange_wait(ex_small, got_pre, "small_wait")[0].reshape(N_DEV, SMALL_ROWS, 1024)
    row = SMALL_ROW["pre_mix_norm"][0]
    got_small = got_small.at[:, row:row + 1, :].set(got_pre[:, 0:1, :])

    two_d = lambda t: t.reshape(-1, 1024) if t.ndim == 3 else t.reshape(1, -1)
    res = _adamw_small(got_small, [(SMALL_ROW[n][0], two_d(w[n]), two_d(m[n]), two_d(v[n])) for n in REPLICATED])
    g_pack = res[0]
    for i, n in enumerate(REPLICATED):
        for kind, r in zip(("grad", "delta", "new_m", "new_v"), res[1 + 4 * i:5 + 4 * i]):
            outs[kind + "_" + n] = r.reshape(w[n].shape)

    me = _my_block()
    gl = lax.dynamic_slice(_small_entry(g_pack, "lru_conv_w").reshape(4, LRU_W), (0, me * 128), (4, 128))
    gs = lax.dynamic_slice(_small_entry(g_pack, "ssd_conv_w").reshape(4, SSD_CONV_CH), (0, me * 192), (4, 192))
    cat = lambda d: jnp.pad(jnp.concatenate([d["lru_conv_w"], d["ssd_conv_w"]], axis=1), ((0, 4), (0, 64)))
    res = _adamw(cat({"lru_conv_w": gl, "ssd_conv_w": gs})[None], cat(w), cat(m), cat(v), "adamw_conv", 8)
    for kind, r in zip(("grad", "delta", "new_m", "new_v"), res):
        outs[kind + "_lru_conv_w"] = r[0:4, 0:128]
        outs[kind + "_ssd_conv_w"] = r[0:4, 128:320]

    order = ["pre_mix_norm", "w_in", "lru_conv_w", "lru_conv_b", "lru_wa", "lru_ba", "lru_wx", "lru_bx", "lru_lambda",
             "lru_out_norm", "ssd_conv_w", "ssd_conv_b", "ssd_dt_bias", "ssd_a_log", "ssd_d", "ssd_out_norm", "w_out",
             "post_mix_norm", "pre_ffn_norm", "w_gate", "w_up", "w_down", "post_ffn_norm"]
    result = [_small_entry(g_pack, "loss").reshape(()), grad_x[None]]
    for kind in ("grad", "delta", "new_m", "new_v"):
        result += [outs[kind + "_" + n][None] for n in order]
    return tuple(result)
```

```python
import functools

import jax
import jax.numpy as jnp
from jax import lax
from jax.experimental import pallas as pl
from jax.experimental.pallas import tpu as pltpu

F32 = jnp.float32
BF16 = jnp.bfloat16
EPS = 1e-6
N_DEV = 8
D_MODEL = 1024
LRU_W = 1024
SSD_INNER = 1024
SSD_HEADS = 16
SSD_HEAD_DIM = 64
SSD_STATE = 128
SSD_GROUPS = 2
SSD_CONV_CH = 1536
CHUNK = 128
D_FF = 2816
FF_CHUNK = 2816
IN_MAIN = 4608
IN_COLS = 4624
LANE = 128
TT = 256
TP = 512
VMEM_LIMIT = 56 * 1024 * 1024
ADAM_LR, ADAM_B1, ADAM_B2, ADAM_EPS, ADAM_WD, ADAM_STEP = 0.001, 0.9, 0.999, 1e-08, 0.01, 10
SMALL_ROWS = 160

NT = (((1,), (1,)), ((), ()))
TN = (((0,), (0,)), ((), ()))


def _params(n_grid):
    return pltpu.CompilerParams(dimension_semantics=("arbitrary",) * n_grid, vmem_limit_bytes=VMEM_LIMIT)


def _dot(a, b, dims=None, precision=None):
    if dims is None:
        return jnp.dot(a, b, preferred_element_type=F32, precision=precision)
    return lax.dot_general(a, b, dims, preferred_element_type=F32, precision=precision)


def _split_bf16(x, terms):
    out = []
    for _ in range(terms - 1):
        p = x.astype(BF16)
        out.append(p)
        x = x - p.astype(F32)
    return out + [x.astype(BF16)]


def _dot_sel(x, sel, dims=None, terms=2, sel_first=False):
    parts = [_dot(sel, p, dims) if sel_first else _dot(p, sel, dims) for p in _split_bf16(x, terms)]
    return functools.reduce(lambda a, b: a + b, parts)


def _sigmoid(x):
    return 0.5 * jnp.tanh(0.5 * x) + 0.5


def _softplus(x):
    e = jnp.exp(-jnp.abs(x))
    l1p = jnp.where(e < 1e-3, e * (1.0 - e * (0.5 - e * (1.0 / 3.0))), jnp.log(1.0 + e))
    return jnp.maximum(x, 0.0) + l1p


def _neg_expm1(x):
    series = -x * (1.0 + x * (0.5 + x * (1.0 / 6.0 + x * (1.0 / 24.0))))
    return jnp.where(x > -0.01, series, 1.0 - jnp.exp(x))


_GELU_C = 0.7978845608028654


def _gelu(x):
    t = jnp.tanh(_GELU_C * (x + 0.044715 * x * x * x))
    return 0.5 * x * (1.0 + t), t


def _gelu_grad(x, t):
    return 0.5 * (1.0 + t) + 0.5 * x * (1.0 - t * t) * _GELU_C * (1.0 + 3.0 * 0.044715 * x * x)


def _rms(x):
    return lax.rsqrt(jnp.mean(x * x, axis=-1, keepdims=True) + EPS)


def _rms_bwd(dyn, x, rn):
    return rn * dyn - x * (rn * rn * rn) * jnp.mean(dyn * x, axis=-1, keepdims=True)


def _row(x, r):
    idx = lax.broadcasted_iota(jnp.int32, x.shape, 0)
    return jnp.sum(jnp.where(idx == r, x, 0.0), axis=0, keepdims=True)


def _shift_down(cur, prev8, j):
    s = pltpu.roll(cur, j, 0)
    p = pltpu.roll(prev8, j, 0)
    r8 = lax.broadcasted_iota(jnp.int32, prev8.shape, 0)
    top = jnp.where(r8 < j, p, s[0:8])
    return jnp.concatenate([top, s[8:]], axis=0)


def _shift_up(cur, next8, j):
    n = cur.shape[0]
    s = pltpu.roll(cur, n - j, 0)
    p = pltpu.roll(next8, 8 - j, 0)
    r8 = lax.broadcasted_iota(jnp.int32, next8.shape, 0)
    bot = jnp.where(r8 >= 8 - j, p, s[n - 8:n])
    return jnp.concatenate([s[:n - 8], bot], axis=0)


def _scan_fwd(a, u):
    n = a.shape[0]
    row = lax.broadcasted_iota(jnp.int32, a.shape, 0)
    k = 1
    while k < n:
        ok = row >= k
        a_s = jnp.where(ok, pltpu.roll(a, k, 0), 1.0)
        u_s = jnp.where(ok, pltpu.roll(u, k, 0), 0.0)
        u = a * u_s + u
        a = a * a_s
        k *= 2
    return a, u


def _scan_bwd(b, d):
    n = b.shape[0]
    row = lax.broadcasted_iota(jnp.int32, b.shape, 0)
    k = 1
    while k < n:
        ok = row < n - k
        b_s = jnp.where(ok, pltpu.roll(b, n - k, 0), 1.0)
        d_s = jnp.where(ok, pltpu.roll(d, n - k, 0), 0.0)
        d = b * d_s + d
        b = b * b_s
        k *= 2
    return b, d


def _scan_tile(a, u, carry, a_s, u_s, reverse):
    n, c = a.shape
    groups = n // 8
    r8 = lax.broadcasted_iota(jnp.int32, a.shape, 0) & 7
    in_group = lambda x, k: pltpu.roll(x.reshape(groups, 8, c), k, 1).reshape(n, c)
    for k in (1, 2, 4):
        ok = (r8 < 8 - k) if reverse else (r8 >= k)
        shift = 8 - k if reverse else k
        a_n = jnp.where(ok, in_group(a, shift), 1.0)
        u_n = jnp.where(ok, in_group(u, shift), 0.0)
        u = a * u_n + u
        a = a * a_n
    nl = c // LANE
    for j in range(nl):
        a_s[j] = a[:, LANE * j:LANE * (j + 1)]
        u_s[j] = u[:, LANE * j:LANE * (j + 1)]
    end = 0 if reverse else 7
    ends = lambda ref, j: ref[pl.ds(j, 1), pl.ds(end, groups, stride=8), :].reshape(groups, LANE)
    ga = jnp.concatenate([ends(a_s, j) for j in range(nl)], axis=1)
    gu = jnp.concatenate([ends(u_s, j) for j in range(nl)], axis=1)
    gacc, gh = (_scan_bwd if reverse else _scan_fwd)(ga, gu)
    gh = gh + gacc * carry
    grow = lax.broadcasted_iota(jnp.int32, gh.shape, 0)
    if reverse:
        cin = jnp.where(grow == groups - 1, carry, pltpu.roll(gh, groups - 1, 0))
    else:
        cin = jnp.where(grow == 0, carry, pltpu.roll(gh, 1, 0))
    spread = ((lax.broadcasted_iota(jnp.int32, (n, LANE), 0) >> 3)
              == lax.broadcasted_iota(jnp.int32, (n, LANE), 1)).astype(BF16)
    cin = jnp.concatenate([cin, jnp.zeros((LANE - groups, c), F32)], axis=0)
    return u + a * _dot_sel(cin, spread, terms=3, sel_first=True)


def _inproj_fwd(x, g_pre, w_main, w_dt):
    T = x.shape[0]
    TT = TP

    def body(x_ref, g_ref, wm_hbm, wd_hbm, h_ref, proj_ref, dtp_ref, wm, wd, sem):
        @pl.when(pl.program_id(0) == 0)
        def _():
            c1 = pltpu.make_async_copy(wm_hbm, wm, sem.at[0])
            c2 = pltpu.make_async_copy(wd_hbm, wd, sem.at[1])
            c1.start()
            c2.start()
            c1.wait()
            c2.wait()

        xv = x_ref[...]
        h = (xv * _rms(xv) * g_ref[...]).astype(BF16)
        h_ref[...] = h
        proj_ref[...] = _dot(h, wm[...])
        dtp_ref[...] = _dot(h, wd[...])

    return pl.pallas_call(
        body, name="inproj_fwd", grid=(T // TT,),
        in_specs=[pl.BlockSpec((TT, D_MODEL), lambda i: (i, 0)),
                  pl.BlockSpec((1, D_MODEL), lambda i: (0, 0)),
                  pl.BlockSpec(memory_space=pl.ANY), pl.BlockSpec(memory_space=pl.ANY)],
        out_specs=[pl.BlockSpec((TT, D_MODEL), lambda i: (i, 0)),
                   pl.BlockSpec((TT, IN_MAIN), lambda i: (i, 0)),
                   pl.BlockSpec((TT, LANE), lambda i: (i, 0))],
        out_shape=[jax.ShapeDtypeStruct((T, D_MODEL), BF16), jax.ShapeDtypeStruct((T, IN_MAIN), F32),
                   jax.ShapeDtypeStruct((T, LANE), F32)],
        scratch_shapes=[pltpu.VMEM((D_MODEL, IN_MAIN), BF16), pltpu.VMEM((D_MODEL, LANE), BF16),
                        pltpu.SemaphoreType.DMA((2,))],
        compiler_params=_params(1),
    )(x, g_pre, w_main, w_dt)


def _inproj_bwd(dp_lru, dp_ssd, ddtp, dx1, x, g_pre, w_main, w_dt):
    T = x.shape[0]
    TT = TP

    def body(dl_ref, ds_ref, dd_ref, dx1_ref, x_ref, g_ref, wm_hbm, wd_hbm, gx_ref, dg_ref, wm, wd, sem):
        @pl.when(pl.program_id(0) == 0)
        def _():
            c1 = pltpu.make_async_copy(wm_hbm, wm, sem.at[0])
            c2 = pltpu.make_async_copy(wd_hbm, wd, sem.at[1])
            c1.start()
            c2.start()
            c1.wait()
            c2.wait()
            dg_ref[...] = jnp.zeros_like(dg_ref)

        dh = _dot(dl_ref[...], wm[:, 0:2048], NT)
        dh += _dot(ds_ref[...], wm[:, 2048:IN_MAIN], NT)
        dh += _dot(dd_ref[...], wd[...], NT)
        xv = x_ref[...]
        rn = _rms(xv)
        dg_ref[...] += jnp.sum(dh * xv * rn, axis=0, keepdims=True)
        gx_ref[...] = dx1_ref[...] + _rms_bwd(dh * g_ref[...], xv, rn)

    return pl.pallas_call(
        body, name="inproj_bwd", grid=(T // TT,),
        in_specs=[pl.BlockSpec((TT, 2048), lambda i: (i, 0)),
                  pl.BlockSpec((TT, 2560), lambda i: (i, 0)),
                  pl.BlockSpec((TT, LANE), lambda i: (i, 0)),
                  pl.BlockSpec((TT, D_MODEL), lambda i: (i, 0)),
                  pl.BlockSpec((TT, D_MODEL), lambda i: (i, 0)),
                  pl.BlockSpec((1, D_MODEL), lambda i: (0, 0)),
                  pl.BlockSpec(memory_space=pl.ANY), pl.BlockSpec(memory_space=pl.ANY)],
        out_specs=[pl.BlockSpec((TT, D_MODEL), lambda i: (i, 0)),
                   pl.BlockSpec((1, D_MODEL), lambda i: (0, 0))],
        out_shape=[jax.ShapeDtypeStruct((T, D_MODEL), F32), jax.ShapeDtypeStruct((1, D_MODEL), F32)],
        scratch_shapes=[pltpu.VMEM((D_MODEL, IN_MAIN), BF16), pltpu.VMEM((D_MODEL, LANE), BF16),
                        pltpu.SemaphoreType.DMA((2,))],
        compiler_params=_params(1),
    )(dp_lru, dp_ssd, ddtp, dx1, x, g_pre, w_main, w_dt)


def _lru_gates(lx, wa_ref, wx_ref, ba, bx, lam):
    lxb = lx.astype(BF16)
    pa = jnp.concatenate([_dot(lxb[:, 256 * k:256 * (k + 1)], wa_ref[k]) for k in range(4)], axis=1) + ba
    px = jnp.concatenate([_dot(lxb[:, 256 * k:256 * (k + 1)], wx_ref[k]) for k in range(4)], axis=1) + bx
    r = _sigmoid(pa)
    ig = _sigmoid(px)
    sp = _softplus(-lam)
    log_a = -8.0 * r * sp
    a = jnp.exp(log_a)
    mult = jnp.sqrt(_neg_expm1(2.0 * log_a))
    return r, ig, sp, a, mult


def _lru_fwd(proj, conv_w, conv_b, wa_bd, wx_bd, ba, bx, lam, g_lru):
    T = proj.shape[0]

    def body(cx_ref, gate_ref, cw_ref, cb_ref, wa_ref, wx_ref, ba_ref, bx_ref, lam_ref, g_ref,
             lx_ref, hl_ref, y_ref, r_ref, ig_ref, a_ref, mult_ref, tail, hcar, sa, su):
        @pl.when(pl.program_id(0) == 0)
        def _():
            tail[...] = jnp.zeros_like(tail)
            hcar[...] = jnp.zeros_like(hcar)

        cx = cx_ref[...]
        prev8 = tail[...]
        lx = cb_ref[...] + cw_ref[3:4, :] * cx
        for j in range(1, 4):
            lx += cw_ref[3 - j:4 - j, :] * _shift_down(cx, prev8, j)
        tail[...] = cx[TT - 8:TT]
        lx_ref[...] = lx
        r, ig, sp, a, mult = _lru_gates(lx, wa_ref, wx_ref, ba_ref[...], bx_ref[...], lam_ref[...])
        r_ref[...] = r
        ig_ref[...] = ig
        a_ref[...] = a
        mult_ref[...] = mult
        h = _scan_tile(a, mult * (ig * lx), hcar[...], sa, su, reverse=False)
        hl_ref[...] = h
        hcar[...] = hl_ref[TT - 1:TT, :]
        ge, _ = _gelu(gate_ref[...])
        p = h * ge
        y_ref[...] = (p * _rms(p) * g_ref[...]).astype(BF16)

    vec = pl.BlockSpec((1, LRU_W), lambda i: (0, 0))
    bd = pl.BlockSpec((4, 256, 256), lambda i: (0, 0, 0))
    tile = pl.BlockSpec((TT, LRU_W), lambda i: (i, 0))
    f32 = jax.ShapeDtypeStruct((T, LRU_W), F32)
    return pl.pallas_call(
        body, name="lru_fwd", grid=(T // TT,),
        in_specs=[tile, pl.BlockSpec((TT, LRU_W), lambda i: (i, 1)),
                  pl.BlockSpec((4, LRU_W), lambda i: (0, 0)), vec, bd, bd, vec, vec, vec, vec],
        out_specs=[tile] * 7,
        out_shape=[f32, f32, jax.ShapeDtypeStruct((T, LRU_W), BF16), f32, f32, f32, f32],
        scratch_shapes=[pltpu.VMEM((8, LRU_W), F32), pltpu.VMEM((1, LRU_W), F32)]
                       + [pltpu.VMEM((LRU_W // LANE, TT, LANE), F32)] * 2,
        compiler_params=_params(1),
    )(proj, proj, conv_w, conv_b, wa_bd, wx_bd, ba, bx, lam, g_lru)


def _lru_bwd(dy, proj, lx, hl, gates, conv_w, wa_bd, wx_bd, lam, g_lru):
    T = proj.shape[0]
    nt = T // TT

    def body(dy_ref, cx_ref, gate_ref, lx_ref, hl_ref, halo_ref, r_ref, ig_ref, a_ref, mult_ref, cw_ref, wa_ref,
             wx_ref, lam_ref, g_ref, dp_ref, dpa_ref, dpx_ref, dcw_ref, dcb_ref, dba_ref, dbx_ref, dlam_ref, dg_ref,
             gcar, acar, head, sa, su):
        i = pl.program_id(0)

        @pl.when(i == 0)
        def _():
            gcar[...] = jnp.zeros_like(gcar)
            acar[...] = jnp.zeros_like(acar)
            head[...] = jnp.zeros_like(head)
            for ref in (dcw_ref, dcb_ref, dba_ref, dbx_ref, dlam_ref, dg_ref):
                ref[...] = jnp.zeros_like(ref)

        lx = lx_ref[...]
        h = hl_ref[...]
        gate = gate_ref[...]
        cx = cx_ref[...]
        lam = lam_ref[...]
        r, ig, a, mult = r_ref[...], ig_ref[...], a_ref[...], mult_ref[...]
        sp = _softplus(-lam)
        ge, th = _gelu(gate)
        p = h * ge
        rn = _rms(p)
        dyv = dy_ref[...]
        dg_ref[...] += jnp.sum(dyv * p * rn, axis=0, keepdims=True)
        dp = _rms_bwd(dyv * g_ref[...], p, rn)
        dp_ref[:, LRU_W:2 * LRU_W] = (dp * h * _gelu_grad(gate, th)).astype(BF16)
        dh = dp * ge
        row = lax.broadcasted_iota(jnp.int32, a.shape, 0)
        b = jnp.where(row == TT - 1, acar[...], pltpu.roll(a, TT - 1, 0))
        g = _scan_tile(b, dh, gcar[...], sa, su, reverse=True)
        gcar[...] = _row(g[0:8], 0)
        acar[...] = _row(a[0:8], 0)
        h_last_prev = halo_ref[7:8, :] * (i < nt - 1).astype(F32)
        hprev = jnp.where(row == 0, h_last_prev, pltpu.roll(h, 1, 0))
        da = g * hprev
        dm2 = (g * (ig * lx)) * 0.5 / mult
        dlog_a = da * a - 2.0 * a * a * dm2
        dlam_ref[...] += jnp.sum(-8.0 * r * dlog_a, axis=0, keepdims=True) * (-_sigmoid(-lam))
        dpa = (-8.0 * sp * dlog_a) * r * (1.0 - r)
        dpx = (g * mult * lx) * ig * (1.0 - ig)
        dba_ref[...] += jnp.sum(dpa, axis=0, keepdims=True)
        dbx_ref[...] += jnp.sum(dpx, axis=0, keepdims=True)
        dpab = dpa.astype(BF16)
        dpxb = dpx.astype(BF16)
        dpa_ref[...] = dpab
        dpx_ref[...] = dpxb
        dlx = g * mult * ig + jnp.concatenate(
            [_dot(dpab[:, 256 * k:256 * (k + 1)], wa_ref[k], NT) + _dot(dpxb[:, 256 * k:256 * (k + 1)], wx_ref[k], NT)
             for k in range(4)], axis=1)
        nxt = head[...]
        dcb_ref[...] += jnp.sum(dlx, axis=0, keepdims=True)
        dcx = cw_ref[3:4, :] * dlx
        dcw_ref[3:4, :] += jnp.sum(cx * dlx, axis=0, keepdims=True)
        for j in range(1, 4):
            sh = _shift_up(dlx, nxt, j)
            dcx += cw_ref[3 - j:4 - j, :] * sh
            dcw_ref[3 - j:4 - j, :] += jnp.sum(cx * sh, axis=0, keepdims=True)
        head[...] = dlx[0:8]
        dp_ref[:, 0:LRU_W] = dcx.astype(BF16)

    rev = lambda i: (nt - 1 - i, 0)
    vec = pl.BlockSpec((1, LRU_W), lambda i: (0, 0))
    bd = pl.BlockSpec((4, 256, 256), lambda i: (0, 0, 0))
    tile = pl.BlockSpec((TT, LRU_W), rev)
    halo = pl.BlockSpec((8, LRU_W), lambda i: (jnp.maximum((nt - 1 - i) * (TT // 8) - 1, 0), 0))
    cw = pl.BlockSpec((4, LRU_W), lambda i: (0, 0))
    return pl.pallas_call(
        body, name="lru_bwd", grid=(nt,),
        in_specs=[tile, tile, pl.BlockSpec((TT, LRU_W), lambda i: (nt - 1 - i, 1)), tile, tile, halo,
                  tile, tile, tile, tile, cw, bd, bd, vec, vec],
        out_specs=[pl.BlockSpec((TT, 2 * LRU_W), rev), tile, tile, cw, vec, vec, vec, vec, vec],
        out_shape=[jax.ShapeDtypeStruct((T, 2 * LRU_W), BF16), jax.ShapeDtypeStruct((T, LRU_W), BF16),
                   jax.ShapeDtypeStruct((T, LRU_W), BF16), jax.ShapeDtypeStruct((4, LRU_W), F32)]
                  + [jax.ShapeDtypeStruct((1, LRU_W), F32)] * 5,
        scratch_shapes=[pltpu.VMEM((1, LRU_W), F32), pltpu.VMEM((1, LRU_W), F32), pltpu.VMEM((8, LRU_W), F32)]
                       + [pltpu.VMEM((LRU_W // LANE, TT, LANE), F32)] * 2,
        compiler_params=_params(1),
    )(dy, proj, proj, lx, hl, hl, *gates, conv_w, wa_bd, wx_bd, lam, g_lru)


def _ssd_chunk_terms(xc, dtp, bias, alog, expand):
    sg = _sigmoid(xc)
    xbc = xc * sg
    pre = dtp + bias
    dt = _softplus(pre)
    A = -jnp.exp(alog)
    ri = lax.broadcasted_iota(jnp.int32, (CHUNK, CHUNK), 0)
    ci = lax.broadcasted_iota(jnp.int32, (CHUNK, CHUNK), 1)
    tril = (ri >= ci).astype(BF16)
    cs = _dot_sel(dt * A, tril, terms=3, sel_first=True)
    cs_last = _row(cs, CHUNK - 1)
    ecs = jnp.exp(cs)
    dec = jnp.exp(cs_last - cs)
    return dict(sg=sg, xbc=xbc, pre=pre, dt=dt, A=A, cs=cs, csT=cs.T, ecs=ecs, dec=dec, ri=ri, ci=ci,
                dt_e=_dot_sel(dt, expand), ecs_e=_dot_sel(ecs, expand), dec_e=_dot_sel(dec, expand))


def _head_lambda(t, h, cst_ref):
    col = jnp.sum(jnp.where(t["ci"] == h, t["cs"], 0.0), axis=1, keepdims=True)
    return jnp.exp(jnp.where(t["ri"] >= t["ci"], col - cst_ref[h:h + 1, :], -1e30))


def _ssd_fwd(proj, dtp, conv_w, conv_b, dt_bias, a_log, d_e, g_ssd, expand):
    T = proj.shape[0]
    nc = T // CHUNK

    def body(z_ref, xp_ref, dtp_ref, cw_ref, cb_ref, bias_ref, alog_ref, de_ref, g_ref, ex_ref,
             xc_ref, y_ref, yn_ref, sprev_ref, tail, S, cst):
        @pl.when(pl.program_id(0) == 0)
        def _():
            tail[...] = jnp.zeros_like(tail)
            S[...] = jnp.zeros_like(S)

        xp = xp_ref[...]
        prev8 = tail[...]
        xc = cb_ref[...] + cw_ref[3:4, :] * xp
        for j in range(1, 4):
            xc += cw_ref[3 - j:4 - j, :] * _shift_down(xp, prev8, j)
        tail[...] = xp[CHUNK - 8:CHUNK]
        xc_ref[...] = xc
        t = _ssd_chunk_terms(xc, dtp_ref[...], bias_ref[...], alog_ref[...], ex_ref[...])
        cst[...] = t["csT"]
        xbc = t["xbc"]
        sx = xbc[:, 0:SSD_INNER]
        Bb = xbc[:, SSD_INNER:SSD_INNER + 256].astype(BF16)
        Cb = xbc[:, SSD_INNER + 256:SSD_CONV_CH].astype(BF16)
        X = t["dt_e"] * sx
        lane = lax.broadcasted_iota(jnp.int32, (CHUNK, LANE), 1)
        G = [_dot(Cb[:, 128 * g:128 * (g + 1)], Bb[:, 128 * g:128 * (g + 1)], NT) for g in range(SSD_GROUPS)]
        for k in range(SSD_HEADS // 2):
            Xp = X[:, 128 * k:128 * (k + 1)]
            acc = jnp.zeros((CHUNK, LANE), F32)
            for half in range(2):
                M = (G[k // 4] * _head_lambda(t, 2 * k + half, cst)).astype(BF16)
                Xh = jnp.where((lane >= 64) if half else (lane < 64), Xp, 0.0).astype(BF16)
                acc += _dot(M, Xh)
            y_ref[:, 128 * k:128 * (k + 1)] = acc
        sprev_ref[0] = S[...]
        eL_e = _row(t["ecs_e"], CHUNK - 1)
        Xd = (X * t["dec_e"]).astype(BF16)
        for g in range(SSD_GROUPS):
            sl = slice(512 * g, 512 * (g + 1))
            Sg = S[:, sl]
            y_ref[:, sl] += t["ecs_e"][:, sl] * _dot(Cb[:, 128 * g:128 * (g + 1)], Sg.astype(BF16))
            S[:, sl] = eL_e[:, sl] * Sg + _dot(Bb[:, 128 * g:128 * (g + 1)], Xd[:, sl], TN)
        y = y_ref[...] + de_ref[...] * sx
        y_ref[...] = y
        z = z_ref[...]
        q = y * (z * _sigmoid(z))
        yn_ref[...] = (q * _rms(q) * g_ref[...]).astype(BF16)

    c0 = lambda i: (0, 0)
    return pl.pallas_call(
        body, name="ssd_fwd", grid=(nc,),
        in_specs=[pl.BlockSpec((CHUNK, SSD_INNER), lambda i: (i, 2)),
                  pl.BlockSpec((CHUNK, SSD_CONV_CH), lambda i: (i, 2)),
                  pl.BlockSpec((CHUNK, LANE), lambda i: (i, 0)),
                  pl.BlockSpec((4, SSD_CONV_CH), c0), pl.BlockSpec((1, SSD_CONV_CH), c0),
                  pl.BlockSpec((1, LANE), c0), pl.BlockSpec((1, LANE), c0),
                  pl.BlockSpec((1, SSD_INNER), c0), pl.BlockSpec((1, SSD_INNER), c0),
                  pl.BlockSpec((LANE, SSD_INNER), c0)],
        out_specs=[pl.BlockSpec((CHUNK, SSD_CONV_CH), lambda i: (i, 0)),
                   pl.BlockSpec((CHUNK, SSD_INNER), lambda i: (i, 0)),
                   pl.BlockSpec((CHUNK, SSD_INNER), lambda i: (i, 0)),
                   pl.BlockSpec((1, SSD_STATE, SSD_INNER), lambda i: (i, 0, 0))],
        out_shape=[jax.ShapeDtypeStruct((T, SSD_CONV_CH), F32), jax.ShapeDtypeStruct((T, SSD_INNER), F32),
                   jax.ShapeDtypeStruct((T, SSD_INNER), BF16),
                   jax.ShapeDtypeStruct((nc, SSD_STATE, SSD_INNER), F32)],
        scratch_shapes=[pltpu.VMEM((8, SSD_CONV_CH), F32), pltpu.VMEM((SSD_STATE, SSD_INNER), F32),
                        pltpu.VMEM((CHUNK, CHUNK), F32)],
        compiler_params=_params(1),
    )(proj, proj, dtp, conv_w, conv_b, dt_bias, a_log, d_e, g_ssd, expand)


def _ssd_bwd(dyn, proj, dtp, xc, y, sprev, conv_w, dt_bias, a_log, d_e, g_ssd, expand):
    T = proj.shape[0]
    nc = T // CHUNK

    def body(dyn_ref, z_ref, xp_ref, dtp_ref, xc_ref, y_ref, sprev_ref, cw_ref, bias_ref, alog_ref, de_ref,
             g_ref, ex_ref, dp_ref, ddtp_ref, dcw_ref, dcb_ref, dbias_ref, dA_ref, dD_ref, dg_ref,
             dS, head, dX_s, dxbc_s, cst, dcst):
        @pl.when(pl.program_id(0) == 0)
        def _():
            dS[...] = jnp.zeros_like(dS)
            head[...] = jnp.zeros_like(head)
            dcst[...] = jnp.zeros_like(dcst)
            for ref in (dcw_ref, dcb_ref, dbias_ref, dA_ref, dD_ref, dg_ref):
                ref[...] = jnp.zeros_like(ref)

        ex = ex_ref[...]
        xc = xc_ref[...]
        t = _ssd_chunk_terms(xc, dtp_ref[...], bias_ref[...], alog_ref[...], ex)
        cst[...] = t["csT"]
        ri, ci = t["ri"], t["ci"]
        xbc = t["xbc"]
        sx = xbc[:, 0:SSD_INNER]
        Bb = xbc[:, SSD_INNER:SSD_INNER + 256].astype(BF16)
        Cb = xbc[:, SSD_INNER + 256:SSD_CONV_CH].astype(BF16)
        X = t["dt_e"] * sx
        z = z_ref[...]
        sz = _sigmoid(z)
        siluz = z * sz
        yv = y_ref[...]
        q = yv * siluz
        rn = _rms(q)
        dynv = dyn_ref[...]
        dg_ref[...] += jnp.sum(dynv * q * rn, axis=0, keepdims=True)
        dq = _rms_bwd(dynv * g_ref[...], q, rn)
        dp_ref[:, 0:SSD_INNER] = (dq * yv * (sz * (1.0 + z * (1.0 - sz)))).astype(BF16)
        dY = dq * siluz
        dD_ref[...] += jnp.sum(dY * sx, axis=0, keepdims=True)
        dYb = dY.astype(BF16)
        lane = lax.broadcasted_iota(jnp.int32, (CHUNK, LANE), 1)
        dcs = jnp.zeros((CHUNK, CHUNK), F32)
        Xb = X.astype(BF16)
        for g in range(SSD_GROUPS):
            Bg = Bb[:, 128 * g:128 * (g + 1)]
            Cg = Cb[:, 128 * g:128 * (g + 1)]
            G = _dot(Cg, Bg, NT)
            dGsum = jnp.zeros((CHUNK, CHUNK), F32)
            for k in range(4 * g, 4 * g + 4):
                Xp = Xb[:, 128 * k:128 * (k + 1)]
                dYp = dY[:, 128 * k:128 * (k + 1)]
                dXp = jnp.zeros((CHUNK, LANE), F32)
                for half in range(2):
                    h = 2 * k + half
                    lam = _head_lambda(t, h, cst)
                    M = G * lam
                    dYh = jnp.where((lane >= 64) if half else (lane < 64), dYp, 0.0).astype(BF16)
                    dM = _dot(dYh, Xp, NT)
                    W = dM * M
                    dcs += jnp.where(ci == h, jnp.sum(W, axis=1, keepdims=True), 0.0)
                    dcst[h:h + 1, :] = jnp.sum(W, axis=0, keepdims=True)
                    dGsum += dM * lam
                    dXp += _dot(M.astype(BF16), dYh, TN)
                dX_s[:, 128 * k:128 * (k + 1)] = dXp
            dGb = dGsum.astype(BF16)
            dxbc_s[:, SSD_INNER + 256 + 128 * g:SSD_INNER + 256 + 128 * (g + 1)] = _dot(dGb, Bg)
            dxbc_s[:, SSD_INNER + 128 * g:SSD_INNER + 128 * (g + 1)] = _dot(dGb, Cg, TN)
        dcs = dcs - dcst[...].T
        Sp = sprev_ref[0]
        dSv = dS[...]
        ecs_e, dec_e = t["ecs_e"], t["dec_e"]
        eL_e = _row(ecs_e, CHUNK - 1)
        dYe = dY * ecs_e
        dYeb = dYe.astype(BF16)
        Xd = X * dec_e
        Xdb = Xd.astype(BF16)
        for g in range(SSD_GROUPS):
            sl = slice(512 * g, 512 * (g + 1))
            Bg = Bb[:, 128 * g:128 * (g + 1)]
            Cg = Cb[:, 128 * g:128 * (g + 1)]
            Spb = Sp[:, sl].astype(BF16)
            dSb = dSv[:, sl].astype(BF16)
            CS = _dot(Cg, Spb)
            BS = _dot(Bg, dSb)
            dxbc_s[:, SSD_INNER + 256 + 128 * g:SSD_INNER + 256 + 128 * (g + 1)] += _dot(dYeb[:, sl], Spb, NT)
            dxbc_s[:, SSD_INNER + 128 * g:SSD_INNER + 128 * (g + 1)] += _dot(Xdb[:, sl], dSb, NT)
            dS[:, sl] = eL_e[:, sl] * dSv[:, sl] + _dot(Cg, dYeb[:, sl], TN)
            dX_s[:, sl] += dec_e[:, sl] * BS
            dcs += _dot_sel(dYe[:, sl] * CS, ex[:, sl], NT, terms=1)
            tdec = _dot_sel(X[:, sl] * BS, ex[:, sl], NT, terms=1) * t["dec"]
            dcs -= tdec
            last = jnp.sum(tdec, axis=0, keepdims=True)
            last += jnp.sum(_dot_sel(Sp[:, sl] * dSv[:, sl], ex[:, sl], NT, terms=1), axis=0, keepdims=True) \
                * _row(t["ecs"], CHUNK - 1)
            dcs += jnp.where(ri == CHUNK - 1, last, 0.0)
        triu = (ci >= ri).astype(BF16)
        da = _dot_sel(dcs, triu, terms=3, sel_first=True)
        dX = dX_s[...]
        ddt = da * t["A"] + _dot_sel(dX * sx, ex, NT, terms=1)
        dA_ref[...] += jnp.sum(da * t["dt"], axis=0, keepdims=True)
        ddtp = ddt * _sigmoid(t["pre"])
        dbias_ref[...] += jnp.sum(ddtp, axis=0, keepdims=True)
        ddtp_ref[...] = ddtp.astype(BF16)
        dxbc_s[:, 0:SSD_INNER] = dX * t["dt_e"] + de_ref[...] * dY
        sg = t["sg"]
        dxc = dxbc_s[...] * (sg * (1.0 + xc * (1.0 - sg)))
        xp = xp_ref[...]
        nxt = head[...]
        dcb_ref[...] += jnp.sum(dxc, axis=0, keepdims=True)
        dpre = cw_ref[3:4, :] * dxc
        dcw_ref[3:4, :] += jnp.sum(xp * dxc, axis=0, keepdims=True)
        for j in range(1, 4):
            sh = _shift_up(dxc, nxt, j)
            dpre += cw_ref[3 - j:4 - j, :] * sh
            dcw_ref[3 - j:4 - j, :] += jnp.sum(xp * sh, axis=0, keepdims=True)
        head[...] = dxc[0:8]
        dp_ref[:, SSD_INNER:SSD_INNER + SSD_CONV_CH] = dpre.astype(BF16)

    c0 = lambda i: (0, 0)
    rev = lambda i: (nc - 1 - i, 0)
    return pl.pallas_call(
        body, name="ssd_bwd", grid=(nc,),
        in_specs=[pl.BlockSpec((CHUNK, SSD_INNER), rev),
                  pl.BlockSpec((CHUNK, SSD_INNER), lambda i: (nc - 1 - i, 2)),
                  pl.BlockSpec((CHUNK, SSD_CONV_CH), lambda i: (nc - 1 - i, 2)),
                  pl.BlockSpec((CHUNK, LANE), rev),
                  pl.BlockSpec((CHUNK, SSD_CONV_CH), rev),
                  pl.BlockSpec((CHUNK, SSD_INNER), rev),
                  pl.BlockSpec((1, SSD_STATE, SSD_INNER), lambda i: (nc - 1 - i, 0, 0)),
                  pl.BlockSpec((4, SSD_CONV_CH), c0), pl.BlockSpec((1, LANE), c0), pl.BlockSpec((1, LANE), c0),
                  pl.BlockSpec((1, SSD_INNER), c0), pl.BlockSpec((1, SSD_INNER), c0),
                  pl.BlockSpec((LANE, SSD_INNER), c0)],
        out_specs=[pl.BlockSpec((CHUNK, 2560), rev), pl.BlockSpec((CHUNK, LANE), rev),
                   pl.BlockSpec((4, SSD_CONV_CH), c0), pl.BlockSpec((1, SSD_CONV_CH), c0),
                   pl.BlockSpec((1, LANE), c0), pl.BlockSpec((1, LANE), c0),
                   pl.BlockSpec((1, SSD_INNER), c0), pl.BlockSpec((1, SSD_INNER), c0)],
        out_shape=[jax.ShapeDtypeStruct((T, 2560), BF16), jax.ShapeDtypeStruct((T, LANE), BF16),
                   jax.ShapeDtypeStruct((4, SSD_CONV_CH), F32), jax.ShapeDtypeStruct((1, SSD_CONV_CH), F32),
                   jax.ShapeDtypeStruct((1, LANE), F32), jax.ShapeDtypeStruct((1, LANE), F32),
                   jax.ShapeDtypeStruct((1, SSD_INNER), F32), jax.ShapeDtypeStruct((1, SSD_INNER), F32)],
        scratch_shapes=[pltpu.VMEM((SSD_STATE, SSD_INNER), F32), pltpu.VMEM((8, SSD_CONV_CH), F32),
                        pltpu.VMEM((CHUNK, SSD_INNER), F32), pltpu.VMEM((CHUNK, SSD_CONV_CH), F32),
                        pltpu.VMEM((CHUNK, CHUNK), F32), pltpu.VMEM((CHUNK, CHUNK), F32)],
        compiler_params=_params(1),
    )(dyn, proj, proj, dtp, xc, y, sprev, conv_w, dt_bias, a_log, d_e, g_ssd, expand)


def _outproj_fwd(x, y_lru, y_ssd, w_out, g_pm, g_pf):
    T = x.shape[0]
    TT = TP

    def body(x_ref, yl_ref, ys_ref, wo_ref, gpm_ref, gpf_ref, mix_ref, x1_ref, h2_ref):
        mix = _dot(yl_ref[...], wo_ref[0:LRU_W, :]) + _dot(ys_ref[...], wo_ref[LRU_W:2 * LRU_W, :])
        mix_ref[...] = mix
        x1 = x_ref[...] + mix * _rms(mix) * gpm_ref[...]
        x1_ref[...] = x1
        h2_ref[...] = (x1 * _rms(x1) * gpf_ref[...]).astype(BF16)

    tile = pl.BlockSpec((TT, D_MODEL), lambda i: (i, 0))
    vec = pl.BlockSpec((1, D_MODEL), lambda i: (0, 0))
    return pl.pallas_call(
        body, name="outproj_fwd", grid=(T // TT,),
        in_specs=[tile, tile, tile, pl.BlockSpec((2 * LRU_W, D_MODEL), lambda i: (0, 0)), vec, vec],
        out_specs=[tile, tile, tile],
        out_shape=[jax.ShapeDtypeStruct((T, D_MODEL), F32), jax.ShapeDtypeStruct((T, D_MODEL), F32),
                   jax.ShapeDtypeStruct((T, D_MODEL), BF16)],
        compiler_params=_params(1),
    )(x, y_lru, y_ssd, w_out, g_pm, g_pf)


def _ffn_fwd_bwd(x1, h2, target, w_gate, w_up, w_down, g_pf, g_ff):
    T = x1.shape[0]

    def body(x1_ref, h2_ref, tg_ref, wg_hbm, wu_hbm, wd_hbm, gpf_ref, gff_ref,
             dx1_ref, act_ref, df_ref, dgt_ref, dup_ref, dgpf_ref, dgff_ref, loss_ref,
             wg, wu, wd, gt_s, up_s, sem):
        @pl.when(pl.program_id(0) == 0)
        def _():
            cps = [pltpu.make_async_copy(s, d, sem.at[n]) for n, (s, d) in
                   enumerate(((wg_hbm, wg), (wu_hbm, wu), (wd_hbm, wd)))]
            for c in cps:
                c.start()
            for c in cps:
                c.wait()
            for ref in (dgpf_ref, dgff_ref, loss_ref):
                ref[...] = jnp.zeros_like(ref)

        h2 = h2_ref[...]
        f = jnp.zeros((TT, D_MODEL), F32)
        for c in range(D_FF // FF_CHUNK):
            sl = slice(FF_CHUNK * c, FF_CHUNK * (c + 1))
            gt = _dot(h2, wg[:, sl])
            up = _dot(h2, wu[:, sl])
            gt_s[:, sl] = gt
            up_s[:, sl] = up
            act = (gt * _sigmoid(gt) * up).astype(BF16)
            act_ref[:, sl] = act
            f += _dot(act, wd[sl, :])
        x1 = x1_ref[...]
        rnf = _rms(f)
        e = x1 + f * rnf * gff_ref[...] - tg_ref[...]
        part = 0.5 * jnp.sum(jnp.sum(e * e, axis=1, keepdims=True), axis=0, keepdims=True) * (1.0 / D_MODEL)
        lane = lax.broadcasted_iota(jnp.int32, (1, LANE), 1)
        loss_ref[...] += jnp.where(lane == 0, part, 0.0)
        dx2 = e * (1.0 / D_MODEL)
        dgff_ref[...] += jnp.sum(dx2 * f * rnf, axis=0, keepdims=True)
        df = _rms_bwd(dx2 * gff_ref[...], f, rnf).astype(BF16)
        df_ref[...] = df
        dh2 = jnp.zeros((TT, D_MODEL), F32)
        for c in range(D_FF // FF_CHUNK):
            sl = slice(FF_CHUNK * c, FF_CHUNK * (c + 1))
            dact = _dot(df, wd[sl, :], NT)
            gt = gt_s[:, sl]
            up = up_s[:, sl]
            sg = _sigmoid(gt)
            dgt = (dact * up * (sg * (1.0 + gt * (1.0 - sg)))).astype(BF16)
            dup = (dact * gt * sg).astype(BF16)
            dgt_ref[:, sl] = dgt
            dup_ref[:, sl] = dup
            dh2 += _dot(dgt, wg[:, sl], NT) + _dot(dup, wu[:, sl], NT)
        rn2 = _rms(x1)
        dgpf_ref[...] += jnp.sum(dh2 * x1 * rn2, axis=0, keepdims=True)
        dx1_ref[...] = dx2 + _rms_bwd(dh2 * gpf_ref[...], x1, rn2)

    tile = pl.BlockSpec((TT, D_MODEL), lambda i: (i, 0))
    wide = pl.BlockSpec((TT, D_FF), lambda i: (i, 0))
    vec = pl.BlockSpec((1, D_MODEL), lambda i: (0, 0))
    hbm = pl.BlockSpec(memory_space=pl.ANY)
    return pl.pallas_call(
        body, name="ffn_fwd_bwd", grid=(T // TT,),
        in_specs=[tile, tile, tile, hbm, hbm, hbm, vec, vec],
        out_specs=[tile, wide, tile, wide, wide, vec, vec, pl.BlockSpec((1, LANE), lambda i: (0, 0))],
        out_shape=[jax.ShapeDtypeStruct((T, D_MODEL), F32), jax.ShapeDtypeStruct((T, D_FF), BF16),
                   jax.ShapeDtypeStruct((T, D_MODEL), BF16), jax.ShapeDtypeStruct((T, D_FF), BF16),
                   jax.ShapeDtypeStruct((T, D_FF), BF16), jax.ShapeDtypeStruct((1, D_MODEL), F32),
                   jax.ShapeDtypeStruct((1, D_MODEL), F32), jax.ShapeDtypeStruct((1, LANE), F32)],
        scratch_shapes=[pltpu.VMEM((D_MODEL, D_FF), BF16), pltpu.VMEM((D_MODEL, D_FF), BF16),
                        pltpu.VMEM((D_FF, D_MODEL), BF16), pltpu.VMEM((TT, D_FF), F32),
                        pltpu.VMEM((TT, D_FF), F32), pltpu.SemaphoreType.DMA((3,))],
        compiler_params=_params(1),
    )(x1, h2, target, w_gate, w_up, w_down, g_pf, g_ff)


def _outproj_bwd(dx1, mix, w_out, g_pm):
    T = dx1.shape[0]
    TT = TP

    def body(dx1_ref, mix_ref, wo_ref, gpm_ref, dyl_ref, dys_ref, dmix_ref, dg_ref):
        @pl.when(pl.program_id(0) == 0)
        def _():
            dg_ref[...] = jnp.zeros_like(dg_ref)

        mix = mix_ref[...]
        rn = _rms(mix)
        dx1v = dx1_ref[...]
        dg_ref[...] += jnp.sum(dx1v * mix * rn, axis=0, keepdims=True)
        dmix = _rms_bwd(dx1v * gpm_ref[...], mix, rn).astype(BF16)
        dmix_ref[...] = dmix
        dyl_ref[...] = _dot(dmix, wo_ref[0:LRU_W, :], NT)
        dys_ref[...] = _dot(dmix, wo_ref[LRU_W:2 * LRU_W, :], NT)

    tile = pl.BlockSpec((TT, D_MODEL), lambda i: (i, 0))
    vec = pl.BlockSpec((1, D_MODEL), lambda i: (0, 0))
    return pl.pallas_call(
        body, name="outproj_bwd", grid=(T // TT,),
        in_specs=[tile, tile, pl.BlockSpec((2 * LRU_W, D_MODEL), lambda i: (0, 0)), vec],
        out_specs=[tile, tile, tile, vec],
        out_shape=[jax.ShapeDtypeStruct((T, D_MODEL), F32), jax.ShapeDtypeStruct((T, D_MODEL), F32),
                   jax.ShapeDtypeStruct((T, D_MODEL), BF16), jax.ShapeDtypeStruct((1, D_MODEL), F32)],
        compiler_params=_params(1),
    )(dx1, mix, w_out, g_pm)


def _tn_matmul(a, bs, name, tk=512):
    T, M = a.shape
    tk = min(tk, T)
    nk = T // tk
    nb = len(bs)

    def body(*refs):
        a_ref, b_refs, o_refs, accs = refs[0], refs[1:1 + nb], refs[1 + nb:1 + 2 * nb], refs[1 + 2 * nb:]
        k = pl.program_id(0)

        @pl.when(k == 0)
        def _():
            for acc in accs:
                acc[...] = jnp.zeros_like(acc)

        av = a_ref[...].astype(BF16)
        for b_ref, acc in zip(b_refs, accs):
            acc[...] += _dot(av, b_ref[...], TN)

        @pl.when(k == nk - 1)
        def _():
            for o_ref, acc in zip(o_refs, accs):
                o_ref[...] = acc[...].astype(BF16)

    return pl.pallas_call(
        body, name=name, grid=(nk,),
        in_specs=[pl.BlockSpec((tk, M), lambda k: (k, 0))]
                 + [pl.BlockSpec((tk, b.shape[1]), lambda k: (k, 0)) for b in bs],
        out_specs=[pl.BlockSpec((M, b.shape[1]), lambda k: (0, 0)) for b in bs],
        out_shape=[jax.ShapeDtypeStruct((M, b.shape[1]), BF16) for b in bs],
        scratch_shapes=[pltpu.VMEM((M, b.shape[1]), F32) for b in bs],
        compiler_params=_params(1),
    )(a, *bs)


def _tn_matmul_slabs(a, bs, name, slab, tk=512):
    T, M = a.shape
    tk = min(tk, T)
    nk = T // tk
    nb = len(bs)
    offs = [sum(b.shape[1] for b in bs[:i]) for i in range(nb + 1)]

    def body(*refs):
        a_ref, b_refs, o_ref, acc = refs[0], refs[1:1 + nb], refs[1 + nb], refs[2 + nb]
        k = pl.program_id(0)

        @pl.when(k == 0)
        def _():
            acc[...] = jnp.zeros_like(acc)

        av = a_ref[...].astype(BF16)
        for i, b_ref in enumerate(b_refs):
            acc[:, offs[i]:offs[i + 1]] += _dot(av, b_ref[...], TN)

        @pl.when(k == nk - 1)
        def _():
            for j in range(N_DEV):
                o_ref[j] = acc[:, slab * j:slab * (j + 1)].astype(BF16)

    return pl.pallas_call(
        body, name=name, grid=(nk,),
        in_specs=[pl.BlockSpec((tk, M), lambda k: (k, 0))]
                 + [pl.BlockSpec((tk, b.shape[1]), lambda k: (k, 0)) for b in bs],
        out_specs=pl.BlockSpec((N_DEV, M, slab), lambda k: (0, 0, 0)),
        out_shape=jax.ShapeDtypeStruct((N_DEV, M, slab), BF16),
        scratch_shapes=[pltpu.VMEM((M, offs[-1]), F32)],
        compiler_params=_params(1),
    )(a, *bs)


def _tn_blockdiag(a, b1, b2, name, tk=1024):
    T = a.shape[0]
    tk = min(tk, T)

    def body(a_ref, b1_ref, b2_ref, o1_ref, o2_ref):
        @pl.when(pl.program_id(0) == 0)
        def _():
            o1_ref[...] = jnp.zeros_like(o1_ref)
            o2_ref[...] = jnp.zeros_like(o2_ref)

        for j in range(4):
            sl = slice(256 * j, 256 * (j + 1))
            av = a_ref[:, sl].astype(BF16)
            o1_ref[j] += _dot(av, b1_ref[:, sl], TN)
            o2_ref[j] += _dot(av, b2_ref[:, sl], TN)

    blk = pl.BlockSpec((tk, LRU_W), lambda k: (k, 0))
    out = pl.BlockSpec((4, 256, 256), lambda k: (0, 0, 0))
    return pl.pallas_call(
        body, name=name, grid=(T // tk,),
        in_specs=[blk, blk, blk], out_specs=[out, out],
        out_shape=[jax.ShapeDtypeStruct((4, 256, 256), F32)] * 2,
        compiler_params=_params(1),
    )(a, b1, b2)


def _adamw_update(g, w, m, v):
    nm = ADAM_B1 * m + (1.0 - ADAM_B1) * g
    nv = ADAM_B2 * v + (1.0 - ADAM_B2) * (g * g)
    m_hat = nm / (1.0 - ADAM_B1 ** ADAM_STEP)
    v_hat = nv / (1.0 - ADAM_B2 ** ADAM_STEP)
    return -ADAM_LR * (m_hat / (jnp.sqrt(v_hat) + ADAM_EPS) + ADAM_WD * w), nm, nv


def _adamw_small(parts, params):
    n = len(params)

    def body(*refs):
        p_ref, wmv = refs[0], refs[1:1 + 3 * n]
        gs_ref, outs = refs[1 + 3 * n], refs[2 + 3 * n:]
        g = p_ref[0]
        for k in range(1, N_DEV):
            g = g + p_ref[k]
        gs_ref[...] = g
        for i, (row, w, _, _) in enumerate(params):
            rows, width = w.shape
            if width <= 1024:
                gp = gs_ref[row:row + rows, 0:width]
            else:
                gp = jnp.concatenate([gs_ref[row:row + 1, :], gs_ref[row + 1:row + 2, 0:width - 1024]], axis=1)
            d, nm, nv = _adamw_update(gp, wmv[3 * i][...], wmv[3 * i + 1][...], wmv[3 * i + 2][...])
            for ref, val in zip(outs[4 * i:4 * i + 4], (gp, d, nm, nv)):
                ref[...] = val

    full = lambda s: pl.BlockSpec(s, lambda: (0,) * len(s))
    shapes = [w.shape for _, w, _, _ in params]
    return pl.pallas_call(
        body, name="adamw_small",
        in_specs=[full(parts.shape)] + [full(s) for s in shapes for _ in range(3)],
        out_specs=[full(parts.shape[1:])] + [full(s) for s in shapes for _ in range(4)],
        out_shape=[jax.ShapeDtypeStruct(parts.shape[1:], F32)]
                  + [jax.ShapeDtypeStruct(s, F32) for s in shapes for _ in range(4)],
        compiler_params=pltpu.CompilerParams(vmem_limit_bytes=VMEM_LIMIT),
    )(parts, *[a for _, w, m, v in params for a in (w, m, v)])


def _adamw_shard(lands, slabs, me, w, m, v, name, tr):
    P, R, C = lands.shape

    def body(me_ref, p_ref, own_ref, w_ref, m_ref, v_ref, g_ref, d_ref, nm_ref, nv_ref):
        g = None
        for k in range(P):
            part = jnp.where(me_ref[0] == k, own_ref[0], p_ref[k]).astype(F32)
            g = part if g is None else g + part
        g_ref[...] = g
        d_ref[...], nm_ref[...], nv_ref[...] = _adamw_update(g, w_ref[...], m_ref[...], v_ref[...])

    blk = pl.BlockSpec((tr, C), lambda i, me_ref: (i, 0))
    return pl.pallas_call(
        body, name=name,
        grid_spec=pltpu.PrefetchScalarGridSpec(
            num_scalar_prefetch=1, grid=(R // tr,),
            in_specs=[pl.BlockSpec((P, tr, C), lambda i, me_ref: (0, i, 0)),
                      pl.BlockSpec((1, tr, C), lambda i, me_ref: (me_ref[0], i, 0)), blk, blk, blk],
            out_specs=[blk, blk, blk, blk]),
        out_shape=[jax.ShapeDtypeStruct((R, C), F32)] * 4,
        compiler_params=_params(1),
    )(me, lands, slabs, w, m, v)


def _adamw(parts, w, m, v, name, tr):
    P, R, C = parts.shape

    def body(p_ref, w_ref, m_ref, v_ref, g_ref, d_ref, nm_ref, nv_ref):
        g = p_ref[0].astype(F32)
        for k in range(1, P):
            g = g + p_ref[k].astype(F32)
        g_ref[...] = g
        d_ref[...], nm_ref[...], nv_ref[...] = _adamw_update(g, w_ref[...], m_ref[...], v_ref[...])

    blk = pl.BlockSpec((tr, C), lambda i: (i, 0))
    return pl.pallas_call(
        body, name=name, grid=(R // tr,),
        in_specs=[pl.BlockSpec((P, tr, C), lambda i: (0, i, 0)), blk, blk, blk],
        out_specs=[blk, blk, blk, blk],
        out_shape=[jax.ShapeDtypeStruct((R, C), F32)] * 4,
        compiler_params=_params(1),
    )(parts, w, m, v)


def _peer(k):
    x, y, c = lax.axis_index("x"), lax.axis_index("y"), lax.axis_index("c")
    px = x ^ ((k >> 2) & 1)
    py = y ^ ((k >> 1) & 1)
    pc = c ^ (k & 1)
    return (px, py, pc), 4 * px + 2 * py + pc


def _my_block():
    return 4 * lax.axis_index("x") + 2 * lax.axis_index("y") + lax.axis_index("c")


def _all_gather(shards, name):
    n = len(shards)

    def body(*refs):
        ins, outs = refs[:n], refs[n:2 * n]
        send, recv, loc = refs[2 * n:]
        x, y, c = lax.axis_index("x"), lax.axis_index("y"), lax.axis_index("c")
        sibling = (x, y, 1 - c)
        chips = [(1 - x, y), (x, 1 - y), (1 - x, 1 - y)]
        slot = lambda px, py, pc: 4 * px + 2 * py + pc

        def copy(a, k, block, to, src=None):
            dst = outs[a].at[slot(*block)]
            return pltpu.make_async_remote_copy(
                src_ref=dst if src is None else src, dst_ref=dst, send_sem=send.at[a, k], recv_sem=recv.at[a, k],
                device_id=to, device_id_type=pl.DeviceIdType.MESH)

        mine = [pltpu.make_async_copy(ins[a], outs[a].at[slot(x, y, c)], loc.at[a]) for a in range(n)]
        for cp in mine:
            cp.start()
        first = []
        for a in range(n):
            first.append(copy(a, 0, (x, y, c), sibling, src=ins[a]))
            first += [copy(a, 1 + j, (x, y, c), (*chip, c), src=ins[a]) for j, chip in enumerate(chips)]
        for cp in first:
            cp.start()
        passed = []
        for j, chip in enumerate(chips):
            for a in range(n):
                copy(a, 1 + j, (*chip, c), (x, y, c)).wait_recv()
                fwd = copy(a, 4 + j, (*chip, c), sibling)
                fwd.start()
                passed.append(fwd)
        for a in range(n):
            copy(a, 0, sibling, (x, y, c)).wait_recv()
            for j, chip in enumerate(chips):
                copy(a, 4 + j, (*chip, 1 - c), (x, y, c)).wait_recv()
        for cp in first + passed:
            cp.wait_send()
        for cp in mine:
            cp.wait()

    hbm = pl.BlockSpec(memory_space=pl.ANY)
    return pl.pallas_call(
        body, name=name,
        in_specs=[hbm] * n, out_specs=[hbm] * n,
        out_shape=[jax.ShapeDtypeStruct((N_DEV,) + s.shape, s.dtype) for s in shards],
        scratch_shapes=[pltpu.SemaphoreType.DMA((n, N_DEV - 1)), pltpu.SemaphoreType.DMA((n, N_DEV - 1)),
                        pltpu.SemaphoreType.DMA((n,))],
    )(*shards)


_HBM = pl.BlockSpec(memory_space=pltpu.HBM)
_SEM = pl.BlockSpec(memory_space=pltpu.SEMAPHORE)
_EFFECT = pltpu.SideEffectType.DATAFLOW_SIDE_EFFECTING


def _direct_copies(srcs, lands, send, recv, slab_source):
    me = _my_block()
    cps = []
    for k in range(1, N_DEV):
        to, blk = _peer(k)
        for a, (src, land) in enumerate(zip(srcs, lands)):
            if slab_source:
                s, d = src.at[blk], land.at[me]
            elif land.ndim == 3:
                s, d = src, land.at[me]
            else:
                s, d = src, land.at[pl.ds(pl.multiple_of(me * src.shape[0], 16), src.shape[0]), :]
            cps.append(pltpu.make_async_remote_copy(
                src_ref=s, dst_ref=d,
                send_sem=send.at[a * (N_DEV - 1) + k - 1], recv_sem=recv.at[a * (N_DEV - 1) + k - 1],
                device_id=to, device_id_type=pl.DeviceIdType.MESH))
    return cps


def _exchange_start(srcs, name, slab_source, axes=None):
    n = len(srcs)
    if slab_source:
        shapes = [s.shape for s in srcs]
    else:
        shapes = [(N_DEV,) + s.shape if ax == 1 else (N_DEV * s.shape[0], s.shape[1]) for s, ax in zip(srcs, axes)]
    lands = [pltpu.with_memory_space_constraint(lax.empty(sh, s.dtype), pltpu.HBM) for sh, s in zip(shapes, srcs)]

    def body(*refs):
        ins, land_in = refs[:n], refs[n:2 * n]
        send, recv = refs[2 * n], refs[2 * n + 1]
        token = refs[4 * n + 2]
        for cp in _direct_copies(ins, land_in, send, recv, slab_source):
            cp.start()
        token[...] = jnp.zeros_like(token)

    sems = pltpu.SemaphoreType.DMA((n * (N_DEV - 1),))
    res = pl.pallas_call(
        body, name=name,
        out_shape=(sems, sems, *[pltpu.HBM(s.shape, s.dtype) for s in srcs],
                   *[pltpu.HBM(l.shape, l.dtype) for l in lands], jax.ShapeDtypeStruct((8, LANE), F32)),
        in_specs=[_HBM] * (2 * n),
        out_specs=(_SEM, _SEM, *[_HBM] * (2 * n), pl.BlockSpec(memory_space=pltpu.VMEM)),
        input_output_aliases={i: 2 + i for i in range(2 * n)},
        compiler_params=pltpu.CompilerParams(has_side_effects=_EFFECT),
    )(*[pltpu.with_memory_space_constraint(s, pltpu.HBM) for s in srcs], *lands)
    return dict(send=res[0], recv=res[1], srcs=res[2:2 + n], lands=res[2 + n:2 + 2 * n], token=res[-1],
                slab_source=slab_source)


def _exchange_wait(ex, after, name):
    n = len(ex["srcs"])
    slab_source = ex["slab_source"]

    def body(*refs):
        ins, lands = refs[:n], refs[n:2 * n]
        send, recv = refs[2 * n], refs[2 * n + 1]
        for cp in _direct_copies(ins, lands, send, recv, slab_source):
            cp.wait_send()
            cp.wait_recv()

    res = pl.pallas_call(
        body, name=name,
        out_shape=tuple(pltpu.HBM(s.shape, s.dtype) for s in list(ex["srcs"]) + list(ex["lands"])),
        in_specs=[_HBM] * (2 * n) + [_SEM, _SEM, pl.BlockSpec(memory_space=pl.ANY)],
        out_specs=tuple([_HBM] * (2 * n)),
        input_output_aliases={i: i for i in range(2 * n)},
        compiler_params=pltpu.CompilerParams(has_side_effects=_EFFECT),
    )(*ex["srcs"], *ex["lands"], ex["send"], ex["recv"], after)
    if slab_source:
        return list(zip(res[n:], res[:n]))
    me = _my_block()
    out = []
    for src, land in zip(res[:n], res[n:]):
        if land.ndim == 3:
            own, at = src[None], (me, 0, 0)
        else:
            own, at = src, (me * src.shape[0], 0)
        out.append(lax.dynamic_update_slice(land, own, at))
    return out


BIG = ("w_in", "w_out", "w_gate", "w_up", "w_down")
BIG_SHARD = {"w_in": (1024, 578), "w_out": (256, 1024), "w_gate": (1024, 352), "w_up": (1024, 352),
             "w_down": (352, 1024)}
BIG_SHARD_AXIS = {"w_in": 1, "w_out": 0, "w_gate": 1, "w_up": 1, "w_down": 0}
BIG_ADAM_ROWS = {"w_in": 256, "w_out": 128, "w_gate": 256, "w_up": 256, "w_down": 176}


def _join(parts, axis):
    if axis == 0:
        return parts.reshape((-1,) + parts.shape[2:])
    return jnp.concatenate([parts[j] for j in range(N_DEV)], axis=1)


def _split(full, axis):
    if axis == 0:
        return full.reshape((N_DEV, full.shape[0] // N_DEV) + full.shape[1:])
    c = full.shape[1] // N_DEV
    return jnp.stack([full[:, c * j:c * (j + 1)] for j in range(N_DEV)])


SMALL = (("lru_wa", 65536), ("lru_wx", 65536), ("pre_mix_norm", 1024), ("lru_conv_w", 4096), ("lru_conv_b", 1024),
         ("lru_ba", 1024), ("lru_bx", 1024), ("lru_lambda", 1024), ("lru_out_norm", 1024), ("ssd_conv_w", 6144),
         ("ssd_conv_b", 1536), ("ssd_dt_bias", 16), ("ssd_a_log", 16), ("ssd_d", 16), ("ssd_out_norm", 1024),
         ("post_mix_norm", 1024), ("pre_ffn_norm", 1024), ("post_ffn_norm", 1024), ("loss", 1))
REPLICATED = tuple(n for n, _ in SMALL if n not in ("lru_conv_w", "ssd_conv_w", "loss"))
SMALL_ROW = {}
for _name, _size in SMALL:
    SMALL_ROW[_name] = (sum(-(-s // 1024) for n, s in SMALL[:len(SMALL_ROW)]), -(-_size // 1024))


def _pack_small(d):
    rows = [jnp.pad(d[n].astype(F32).reshape(-1), (0, SMALL_ROW[n][1] * 1024 - s)).reshape(-1, 1024) for n, s in SMALL]
    used = sum(r.shape[0] for r in rows)
    return jnp.concatenate(rows + [jnp.zeros((SMALL_ROWS - used, 1024), F32)], axis=0)


def _small_entry(p, name):
    row, rows = SMALL_ROW[name]
    return p[row:row + rows].reshape(-1)[:dict(SMALL)[name]]


def _blockdiag4(w):
    on_diag = jnp.eye(4, dtype=w.dtype)[None, :, None, :, None]
    return (w.reshape(4, 4, 64, 1, 64) * on_diag).reshape(4, 256, 256)


def _diag_blocks(g):
    on_diag = jnp.eye(4, dtype=g.dtype)[None, :, None, :, None]
    return jnp.sum(g.reshape(4, 4, 64, 4, 64) * on_diag, axis=3).reshape(16, 64, 64)


def _local_step(x, target, w_in, P, rest_weights, emit, emit_small, start_token=None):
    cut = IN_MAIN - (N_DEV - 1) * (IN_COLS // N_DEV)
    w_main = jnp.concatenate([w_in[j] for j in range(N_DEV - 1)] + [w_in[N_DEV - 1][:, :cut]], axis=1)
    w_dt = jnp.pad(w_in[N_DEV - 1][:, cut:], ((0, 0), (0, LANE - SSD_HEADS)))
    pad16 = lambda v: jnp.pad(v.reshape(1, SSD_HEADS), ((0, 0), (0, LANE - SSD_HEADS)))
    dt_bias, a_log = pad16(P["ssd_dt_bias"]), pad16(P["ssd_a_log"])
    d_e = jnp.repeat(P["ssd_d"].reshape(SSD_HEADS), SSD_HEAD_DIM).reshape(1, SSD_INNER)
    expand = (jnp.arange(LANE)[:, None] == (jnp.arange(SSD_INNER)[None, :] // SSD_HEAD_DIM)).astype(BF16)
    wa_bd = _blockdiag4(P["lru_wa"].astype(BF16))
    wx_bd = _blockdiag4(P["lru_wx"].astype(BF16))
    vec = lambda n: P[n].reshape(1, -1)

    after = lambda v, tok: v if tok is None else v + tok[0:1, 0:1]

    h, proj, dtp = _inproj_fwd(x, after(vec("pre_mix_norm"), start_token), w_main, w_dt)
    lx, hl, y_lru, *gates = _lru_fwd(proj, P["lru_conv_w"], vec("lru_conv_b"), wa_bd, wx_bd, vec("lru_ba"),
                                     vec("lru_bx"), vec("lru_lambda"), vec("lru_out_norm"))
    xc, y, y_ssd, sprev = _ssd_fwd(proj, dtp, P["ssd_conv_w"], vec("ssd_conv_b"), dt_bias, a_log, d_e,
                                   vec("ssd_out_norm"), expand)
    W = rest_weights(y_ssd)
    mix, x1, h2 = _outproj_fwd(x, y_lru, y_ssd, W["w_out"], vec("post_mix_norm"), vec("pre_ffn_norm"))
    dx1, act, df, dgt, dup, dg_pf, dg_ff, loss = _ffn_fwd_bwd(
        x1, h2, target, W["w_gate"], W["w_up"], W["w_down"], vec("pre_ffn_norm"), vec("post_ffn_norm"))
    tok = emit("ffn", {"w_gate": _tn_matmul_slabs(h2, [dgt], "dw_gate", D_FF // N_DEV, tk=1024),
                       "w_up": _tn_matmul_slabs(h2, [dup], "dw_up", D_FF // N_DEV, tk=1024),
                       "w_down": _tn_matmul(act, [df], "dw_down", tk=1024)[0]})
    dy_lru, dy_ssd, dmix, dg_pm = _outproj_bwd(dx1, mix, W["w_out"], after(vec("post_mix_norm"), tok))
    tok = emit("out", {"w_out": jnp.concatenate([_tn_matmul(y_lru, [dmix], "dw_out_lru", tk=1024)[0],
                                                 _tn_matmul(y_ssd, [dmix], "dw_out_ssd", tk=1024)[0]], axis=0)})
    dp_ssd, ddtp, dcw_s, dcb_s, dbias, dA, dD_e, dg_ssd = _ssd_bwd(
        dy_ssd, proj, dtp, xc, y, sprev, P["ssd_conv_w"], dt_bias, a_log, d_e,
        after(vec("ssd_out_norm"), tok), expand)
    dp_lru, dpa, dpx, dcw_l, dcb_l, dba, dbx, dlam, dg_lru = _lru_bwd(
        dy_lru, proj, lx, hl, gates, P["lru_conv_w"], wa_bd, wx_bd, vec("lru_lambda"), vec("lru_out_norm"))
    tok = emit("in", {"w_in": _tn_matmul_slabs(h, [dp_lru, dp_ssd, ddtp], "dw_in", IN_COLS // N_DEV)})

    a_neg = -jnp.exp(P["ssd_a_log"].reshape(SSD_HEADS))
    dwa, dwx = _tn_blockdiag(lx, dpa, dpx, "dw_lru_gates")
    small = {
        "pre_mix_norm": jnp.zeros((1, D_MODEL), F32), "lru_conv_w": dcw_l, "lru_conv_b": dcb_l,
        "lru_wa": _diag_blocks(dwa), "lru_ba": dba,
        "lru_wx": _diag_blocks(dwx), "lru_bx": dbx,
        "lru_lambda": dlam, "lru_out_norm": dg_lru, "ssd_conv_w": dcw_s, "ssd_conv_b": dcb_s,
        "ssd_dt_bias": dbias[0, :SSD_HEADS], "ssd_a_log": dA[0, :SSD_HEADS] * a_neg,
        "ssd_d": jnp.sum(dD_e.reshape(SSD_HEADS, SSD_HEAD_DIM), axis=1), "ssd_out_norm": dg_ssd,
        "post_mix_norm": dg_pm, "pre_ffn_norm": dg_pf, "post_ffn_norm": dg_ff, "loss": loss[0, 0:1],
    }
    tok = after(after(vec("pre_mix_norm"), tok), emit_small(small))
    grad_x, dg_pre = _inproj_bwd(dp_lru, dp_ssd, ddtp, dx1, x, tok, w_main, w_dt)
    return grad_x, dg_pre


def kernel(x, pre_mix_norm, w_in, lru_conv_w, lru_conv_b, lru_wa, lru_ba, lru_wx, lru_bx, lru_lambda, lru_out_norm, ssd_conv_w, ssd_conv_b, ssd_dt_bias, ssd_a_log, ssd_d, ssd_out_norm, w_out, post_mix_norm, pre_ffn_norm, w_gate, w_up, w_down, post_ffn_norm, loss_target, m_pre_mix_norm, m_w_in, m_lru_conv_w, m_lru_conv_b, m_lru_wa, m_lru_ba, m_lru_wx, m_lru_bx, m_lru_lambda, m_lru_out_norm, m_ssd_conv_w, m_ssd_conv_b, m_ssd_dt_bias, m_ssd_a_log, m_ssd_d, m_ssd_out_norm, m_w_out, m_post_mix_norm, m_pre_ffn_norm, m_w_gate, m_w_up, m_w_down, m_post_ffn_norm, v_pre_mix_norm, v_w_in, v_lru_conv_w, v_lru_conv_b, v_lru_wa, v_lru_ba, v_lru_wx, v_lru_bx, v_lru_lambda, v_lru_out_norm, v_ssd_conv_w, v_ssd_conv_b, v_ssd_dt_bias, v_ssd_a_log, v_ssd_d, v_ssd_out_norm, v_w_out, v_post_mix_norm, v_pre_ffn_norm, v_w_gate, v_w_up, v_w_down, v_post_ffn_norm):
    a = dict(locals())
    names = [n for n, _ in SMALL if n != "loss"] + list(BIG)
    w = {n: a[n][0] for n in names}
    m = {n: a["m_" + n][0] for n in names}
    v = {n: a["v_" + n][0] for n in names}

    cpack = jnp.concatenate([w["lru_conv_w"], w["ssd_conv_w"], jnp.zeros((4, 64), F32)], axis=1)
    cpack = jnp.pad(cpack, ((0, 4), (0, 0)))
    g_in, cg = _all_gather([w["w_in"].astype(BF16), cpack], "all_gather_w_in")
    P = {n: w[n] for n in REPLICATED}
    P["lru_conv_w"] = _join(cg[:, 0:4, 0:128], 1)
    P["ssd_conv_w"] = _join(cg[:, 0:4, 128:320], 1)

    rest = [n for n in BIG if n != "w_in"]
    zero = jnp.minimum(jnp.abs(cg[0, 0, 0]), 0.0)
    ex_w = _exchange_start([(w[n] + zero).astype(BF16) for n in rest], "weights_start", slab_source=False,
                           axes=[BIG_SHARD_AXIS[n] for n in rest])

    def rest_weights(after):
        lands = _exchange_wait(ex_w, after, "weights_wait")
        return {n: _join(p, 1) if p.ndim == 3 else p for n, p in zip(rest, lands)}

    pending = []

    def emit(group, grads):
        ex = _exchange_start([g if g.ndim == 3 else _split(g, BIG_SHARD_AXIS[n]) for n, g in grads.items()],
                             "grads_start_" + group, slab_source=True)
        pending.append((group, list(grads), ex))
        return ex["token"]

    def emit_small(small):
        ex = _exchange_start([_pack_small(small)], "small_start", slab_source=False, axes=[0])
        pending.append(("small", None, ex))
        return ex["token"]

    grad_x, dg_pre = _local_step(x[0], loss_target[0], g_in, P, rest_weights, emit, emit_small, ex_w["token"])

    ex_small = pending.pop()[2]
    ex_pre = _exchange_start([jnp.pad(dg_pre, ((0, 7), (0, 0)))], "pre_mix_norm_start", slab_source=False, axes=[0])

    outs = {}
    done = ex_pre["token"]
    me1 = _my_block().astype(jnp.int32).reshape(1)
    for group, group_names, ex in pending:
        for n, (lands, slabs) in zip(group_names, _exchange_wait(ex, done, "grads_wait_" + group)):
            res = _adamw_shard(lands, slabs, me1, w[n], m[n], v[n], "adamw_" + n, BIG_ADAM_ROWS[n])
            done = res[0]
            for kind, r in zip(("grad", "delta", "new_m", "new_v"), res):
                outs[kind + "_" + n] = r

    got_pre = _exchange_wait(ex_pre, done, "pre_mix_norm_wait")[0].reshape(N_DEV, 8, 1024)
    got_small = _exchange_wait(ex_small, got_pre, "small_wait")[0].reshape(N_DEV, SMALL_ROWS, 1024)
    row = SMALL_ROW["pre_mix_norm"][0]
    got_small = got_small.at[:, row:row + 1, :].set(got_pre[:, 0:1, :])

    two_d = lambda t: t.reshape(-1, 1024) if t.ndim == 3 else t.reshape(1, -1)
    res = _adamw_small(got_small, [(SMALL_ROW[n][0], two_d(w[n]), two_d(m[n]), two_d(v[n])) for n in REPLICATED])
    g_pack = res[0]
    for i, n in enumerate(REPLICATED):
        for kind, r in zip(("grad", "delta", "new_m", "new_v"), res[1 + 4 * i:5 + 4 * i]):
            outs[kind + "_" + n] = r.reshape(w[n].shape)

    me = _my_block()
    gl = lax.dynamic_slice(_small_entry(g_pack, "lru_conv_w").reshape(4, LRU_W), (0, me * 128), (4, 128))
    gs = lax.dynamic_slice(_small_entry(g_pack, "ssd_conv_w").reshape(4, SSD_CONV_CH), (0, me * 192), (4, 192))
    cat = lambda d: jnp.pad(jnp.concatenate([d["lru_conv_w"], d["ssd_conv_w"]], axis=1), ((0, 4), (0, 64)))
    res = _adamw(cat({"lru_conv_w": gl, "ssd_conv_w": gs})[None], cat(w), cat(m), cat(v), "adamw_conv", 8)
    for kind, r in zip(("grad", "delta", "new_m", "new_v"), res):
        outs[kind + "_lru_conv_w"] = r[0:4, 0:128]
        outs[kind + "_ssd_conv_w"] = r[0:4, 128:320]

    order = ["pre_mix_norm", "w_in", "lru_conv_w", "lru_conv_b", "lru_wa", "lru_ba", "lru_wx", "lru_bx", "lru_lambda",
             "lru_out_norm", "ssd_conv_w", "ssd_conv_b", "ssd_dt_bias", "ssd_a_log", "ssd_d", "ssd_out_norm", "w_out",
             "post_mix_norm", "pre_ffn_norm", "w_gate", "w_up", "w_down", "post_ffn_norm"]
    result = [_small_entry(g_pack, "loss").reshape(()), grad_x[None]]
    for kind in ("grad", "delta", "new_m", "new_v"):
        result += [outs[kind + "_" + n][None] for n in order]
    return tuple(result)
```

```python
import functools

import jax
import jax.numpy as jnp
from jax import lax
from jax.experimental import pallas as pl
from jax.experimental.pallas import tpu as pltpu

F32 = jnp.float32
BF16 = jnp.bfloat16
EPS = 1e-6
N_DEV = 8
D_MODEL = 1024
LRU_W = 1024
SSD_INNER = 1024
SSD_HEADS = 16
SSD_HEAD_DIM = 64
SSD_STATE = 128
SSD_GROUPS = 2
SSD_CONV_CH = 1536
CHUNK = 128
D_FF = 2816
FF_CHUNK = 2816
IN_MAIN = 4608
IN_COLS = 4624
LANE = 128
TT = 256
TP = 512
VMEM_LIMIT = 56 * 1024 * 1024
ADAM_LR, ADAM_B1, ADAM_B2, ADAM_EPS, ADAM_WD, ADAM_STEP = 0.001, 0.9, 0.999, 1e-08, 0.01, 10
SMALL_ROWS = 160

NT = (((1,), (1,)), ((), ()))
TN = (((0,), (0,)), ((), ()))


def _params(n_grid):
    return pltpu.CompilerParams(dimension_semantics=("arbitrary",) * n_grid, vmem_limit_bytes=VMEM_LIMIT)


def _dot(a, b, dims=None, precision=None):
    if dims is None:
        return jnp.dot(a, b, preferred_element_type=F32, precision=precision)
    return lax.dot_general(a, b, dims, preferred_element_type=F32, precision=precision)


def _split_bf16(x, terms):
    out = []
    for _ in range(terms - 1):
        p = x.astype(BF16)
        out.append(p)
        x = x - p.astype(F32)
    return out + [x.astype(BF16)]


def _dot_sel(x, sel, dims=None, terms=2, sel_first=False):
    parts = [_dot(sel, p, dims) if sel_first else _dot(p, sel, dims) for p in _split_bf16(x, terms)]
    return functools.reduce(lambda a, b: a + b, parts)


def _sigmoid(x):
    return 0.5 * jnp.tanh(0.5 * x) + 0.5


def _softplus(x):
    e = jnp.exp(-jnp.abs(x))
    l1p = jnp.where(e < 1e-3, e * (1.0 - e * (0.5 - e * (1.0 / 3.0))), jnp.log(1.0 + e))
    return jnp.maximum(x, 0.0) + l1p


def _neg_expm1(x):
    series = -x * (1.0 + x * (0.5 + x * (1.0 / 6.0 + x * (1.0 / 24.0))))
    return jnp.where(x > -0.01, series, 1.0 - jnp.exp(x))


_GELU_C = 0.7978845608028654


def _gelu(x):
    t = jnp.tanh(_GELU_C * (x + 0.044715 * x * x * x))
    return 0.5 * x * (1.0 + t), t


def _gelu_grad(x, t):
    return 0.5 * (1.0 + t) + 0.5 * x * (1.0 - t * t) * _GELU_C * (1.0 + 3.0 * 0.044715 * x * x)


def _rms(x):
    return lax.rsqrt(jnp.mean(x * x, axis=-1, keepdims=True) + EPS)


def _rms_bwd(dyn, x, rn):
    return rn * dyn - x * (rn * rn * rn) * jnp.mean(dyn * x, axis=-1, keepdims=True)


def _row(x, r):
    idx = lax.broadcasted_iota(jnp.int32, x.shape, 0)
    return jnp.sum(jnp.where(idx == r, x, 0.0), axis=0, keepdims=True)


def _shift_down(cur, prev8, j):
    s = pltpu.roll(cur, j, 0)
    p = pltpu.roll(prev8, j, 0)
    r8 = lax.broadcasted_iota(jnp.int32, prev8.shape, 0)
    top = jnp.where(r8 < j, p, s[0:8])
    return jnp.concatenate([top, s[8:]], axis=0)


def _shift_up(cur, next8, j):
    n = cur.shape[0]
    s = pltpu.roll(cur, n - j, 0)
    p = pltpu.roll(next8, 8 - j, 0)
    r8 = lax.broadcasted_iota(jnp.int32, next8.shape, 0)
    bot = jnp.where(r8 >= 8 - j, p, s[n - 8:n])
    return jnp.concatenate([s[:n - 8], bot], axis=0)


def _scan_fwd(a, u):
    n = a.shape[0]
    row = lax.broadcasted_iota(jnp.int32, a.shape, 0)
    k = 1
    while k < n:
        ok = row >= k
        a_s = jnp.where(ok, pltpu.roll(a, k, 0), 1.0)
        u_s = jnp.where(ok, pltpu.roll(u, k, 0), 0.0)
        u = a * u_s + u
        a = a * a_s
        k *= 2
    return a, u


def _scan_bwd(b, d):
    n = b.shape[0]
    row = lax.broadcasted_iota(jnp.int32, b.shape, 0)
    k = 1
    while k < n:
        ok = row < n - k
        b_s = jnp.where(ok, pltpu.roll(b, n - k, 0), 1.0)
        d_s = jnp.where(ok, pltpu.roll(d, n - k, 0), 0.0)
        d = b * d_s + d
        b = b * b_s
        k *= 2
    return b, d


def _scan_tile(a, u, carry, a_s, u_s, reverse):
    n, c = a.shape
    groups = n // 8
    r8 = lax.broadcasted_iota(jnp.int32, a.shape, 0) & 7
    in_group = lambda x, k: pltpu.roll(x.reshape(groups, 8, c), k, 1).reshape(n, c)
    for k in (1, 2, 4):
        ok = (r8 < 8 - k) if reverse else (r8 >= k)
        shift = 8 - k if reverse else k
        a_n = jnp.where(ok, in_group(a, shift), 1.0)
        u_n = jnp.where(ok, in_group(u, shift), 0.0)
        u = a * u_n + u
        a = a * a_n
    nl = c // LANE
    for j in range(nl):
        a_s[j] = a[:, LANE * j:LANE * (j + 1)]
        u_s[j] = u[:, LANE * j:LANE * (j + 1)]
    end = 0 if reverse else 7
    ends = lambda ref, j: ref[pl.ds(j, 1), pl.ds(end, groups, stride=8), :].reshape(groups, LANE)
    ga = jnp.concatenate([ends(a_s, j) for j in range(nl)], axis=1)
    gu = jnp.concatenate([ends(u_s, j) for j in range(nl)], axis=1)
    gacc, gh = (_scan_bwd if reverse else _scan_fwd)(ga, gu)
    gh = gh + gacc * carry
    grow = lax.broadcasted_iota(jnp.int32, gh.shape, 0)
    if reverse:
        cin = jnp.where(grow == groups - 1, carry, pltpu.roll(gh, groups - 1, 0))
    else:
        cin = jnp.where(grow == 0, carry, pltpu.roll(gh, 1, 0))
    spread = ((lax.broadcasted_iota(jnp.int32, (n, LANE), 0) >> 3)
              == lax.broadcasted_iota(jnp.int32, (n, LANE), 1)).astype(BF16)
    cin = jnp.concatenate([cin, jnp.zeros((LANE - groups, c), F32)], axis=0)
    return u + a * _dot_sel(cin, spread, terms=3, sel_first=True)


def _inproj_fwd(x, g_pre, w_main, w_dt):
    T = x.shape[0]
    TT = TP

    def body(x_ref, g_ref, wm_hbm, wd_hbm, h_ref, proj_ref, dtp_ref, wm, wd, sem):
        @pl.when(pl.program_id(0) == 0)
        def _():
            c1 = pltpu.make_async_copy(wm_hbm, wm, sem.at[0])
            c2 = pltpu.make_async_copy(wd_hbm, wd, sem.at[1])
            c1.start()
            c2.start()
            c1.wait()
            c2.wait()

        xv = x_ref[...]
        h = (xv * _rms(xv) * g_ref[...]).astype(BF16)
        h_ref[...] = h
        proj_ref[...] = _dot(h, wm[...])
        dtp_ref[...] = _dot(h, wd[...])

    return pl.pallas_call(
        body, name="inproj_fwd", grid=(T // TT,),
        in_specs=[pl.BlockSpec((TT, D_MODEL), lambda i: (i, 0)),
                  pl.BlockSpec((1, D_MODEL), lambda i: (0, 0)),
                  pl.BlockSpec(memory_space=pl.ANY), pl.BlockSpec(memory_space=pl.ANY)],
        out_specs=[pl.BlockSpec((TT, D_MODEL), lambda i: (i, 0)),
                   pl.BlockSpec((TT, IN_MAIN), lambda i: (i, 0)),
                   pl.BlockSpec((TT, LANE), lambda i: (i, 0))],
        out_shape=[jax.ShapeDtypeStruct((T, D_MODEL), BF16), jax.ShapeDtypeStruct((T, IN_MAIN), F32),
                   jax.ShapeDtypeStruct((T, LANE), F32)],
        scratch_shapes=[pltpu.VMEM((D_MODEL, IN_MAIN), BF16), pltpu.VMEM((D_MODEL, LANE), BF16),
                        pltpu.SemaphoreType.DMA((2,))],
        compiler_params=_params(1),
    )(x, g_pre, w_main, w_dt)


def _inproj_bwd(dp_lru, dp_ssd, ddtp, dx1, x, g_pre, w_main, w_dt):
    T = x.shape[0]
    TT = TP

    def body(dl_ref, ds_ref, dd_ref, dx1_ref, x_ref, g_ref, wm_hbm, wd_hbm, gx_ref, dg_ref, wm, wd, sem):
        @pl.when(pl.program_id(0) == 0)
        def _():
            c1 = pltpu.make_async_copy(wm_hbm, wm, sem.at[0])
            c2 = pltpu.make_async_copy(wd_hbm, wd, sem.at[1])
            c1.start()
            c2.start()
            c1.wait()
            c2.wait()
            dg_ref[...] = jnp.zeros_like(dg_ref)

        dh = _dot(dl_ref[...], wm[:, 0:2048], NT)
        dh += _dot(ds_ref[...], wm[:, 2048:IN_MAIN], NT)
        dh += _dot(dd_ref[...], wd[...], NT)
        xv = x_ref[...]
        rn = _rms(xv)
        dg_ref[...] += jnp.sum(dh * xv * rn, axis=0, keepdims=True)
        gx_ref[...] = dx1_ref[...] + _rms_bwd(dh * g_ref[...], xv, rn)

    return pl.pallas_call(
        body, name="inproj_bwd", grid=(T // TT,),
        in_specs=[pl.BlockSpec((TT, 2048), lambda i: (i, 0)),
                  pl.BlockSpec((TT, 2560), lambda i: (i, 0)),
                  pl.BlockSpec((TT, LANE), lambda i: (i, 0)),
                  pl.BlockSpec((TT, D_MODEL), lambda i: (i, 0)),
                  pl.BlockSpec((TT, D_MODEL), lambda i: (i, 0)),
                  pl.BlockSpec((1, D_MODEL), lambda i: (0, 0)),
                  pl.BlockSpec(memory_space=pl.ANY), pl.BlockSpec(memory_space=pl.ANY)],
        out_specs=[pl.BlockSpec((TT, D_MODEL), lambda i: (i, 0)),
                   pl.BlockSpec((1, D_MODEL), lambda i: (0, 0))],
        out_shape=[jax.ShapeDtypeStruct((T, D_MODEL), F32), jax.ShapeDtypeStruct((1, D_MODEL), F32)],
        scratch_shapes=[pltpu.VMEM((D_MODEL, IN_MAIN), BF16), pltpu.VMEM((D_MODEL, LANE), BF16),
                        pltpu.SemaphoreType.DMA((2,))],
        compiler_params=_params(1),
    )(dp_lru, dp_ssd, ddtp, dx1, x, g_pre, w_main, w_dt)


def _lru_gates(lx, wa_ref, wx_ref, ba, bx, lam):
    lxb = lx.astype(BF16)
    pa = jnp.concatenate([_dot(lxb[:, 256 * k:256 * (k + 1)], wa_ref[k]) for k in range(4)], axis=1) + ba
    px = jnp.concatenate([_dot(lxb[:, 256 * k:256 * (k + 1)], wx_ref[k]) for k in range(4)], axis=1) + bx
    r = _sigmoid(pa)
    ig = _sigmoid(px)
    sp = _softplus(-lam)
    log_a = -8.0 * r * sp
    a = jnp.exp(log_a)
    mult = jnp.sqrt(_neg_expm1(2.0 * log_a))
    return r, ig, sp, a, mult


def _lru_fwd(proj, conv_w, conv_b, wa_bd, wx_bd, ba, bx, lam, g_lru):
    T = proj.shape[0]

    def body(cx_ref, gate_ref, cw_ref, cb_ref, wa_ref, wx_ref, ba_ref, bx_ref, lam_ref, g_ref,
             lx_ref, hl_ref, y_ref, r_ref, ig_ref, a_ref, mult_ref, tail, hcar, sa, su):
        @pl.when(pl.program_id(0) == 0)
        def _():
            tail[...] = jnp.zeros_like(tail)
            hcar[...] = jnp.zeros_like(hcar)

        cx = cx_ref[...]
        prev8 = tail[...]
        lx = cb_ref[...] + cw_ref[3:4, :] * cx
        for j in range(1, 4):
            lx += cw_ref[3 - j:4 - j, :] * _shift_down(cx, prev8, j)
        tail[...] = cx[TT - 8:TT]
        lx_ref[...] = lx
        r, ig, sp, a, mult = _lru_gates(lx, wa_ref, wx_ref, ba_ref[...], bx_ref[...], lam_ref[...])
        r_ref[...] = r
        ig_ref[...] = ig
        a_ref[...] = a
        mult_ref[...] = mult
        h = _scan_tile(a, mult * (ig * lx), hcar[...], sa, su, reverse=False)
        hl_ref[...] = h
        hcar[...] = hl_ref[TT - 1:TT, :]
        ge, _ = _gelu(gate_ref[...])
        p = h * ge
        y_ref[...] = (p * _rms(p) * g_ref[...]).astype(BF16)

    vec = pl.BlockSpec((1, LRU_W), lambda i: (0, 0))
    bd = pl.BlockSpec((4, 256, 256), lambda i: (0, 0, 0))
    tile = pl.BlockSpec((TT, LRU_W), lambda i: (i, 0))
    f32 = jax.ShapeDtypeStruct((T, LRU_W), F32)
    return pl.pallas_call(
        body, name="lru_fwd", grid=(T // TT,),
        in_specs=[tile, pl.BlockSpec((TT, LRU_W), lambda i: (i, 1)),
                  pl.BlockSpec((4, LRU_W), lambda i: (0, 0)), vec, bd, bd, vec, vec, vec, vec],
        out_specs=[tile] * 7,
        out_shape=[f32, f32, jax.ShapeDtypeStruct((T, LRU_W), BF16), f32, f32, f32, f32],
        scratch_shapes=[pltpu.VMEM((8, LRU_W), F32), pltpu.VMEM((1, LRU_W), F32)]
                       + [pltpu.VMEM((LRU_W // LANE, TT, LANE), F32)] * 2,
        compiler_params=_params(1),
    )(proj, proj, conv_w, conv_b, wa_bd, wx_bd, ba, bx, lam, g_lru)


def _lru_bwd(dy, proj, lx, hl, gates, conv_w, wa_bd, wx_bd, lam, g_lru):
    T = proj.shape[0]
    nt = T // TT

    def body(dy_ref, cx_ref, gate_ref, lx_ref, hl_ref, halo_ref, r_ref, ig_ref, a_ref, mult_ref, cw_ref, wa_ref,
             wx_ref, lam_ref, g_ref, dp_ref, dpa_ref, dpx_ref, dcw_ref, dcb_ref, dba_ref, dbx_ref, dlam_ref, dg_ref,
             gcar, acar, head, sa, su):
        i = pl.program_id(0)

        @pl.when(i == 0)
        def _():
            gcar[...] = jnp.zeros_like(gcar)
            acar[...] = jnp.zeros_like(acar)
            head[...] = jnp.zeros_like(head)
            for ref in (dcw_ref, dcb_ref, dba_ref, dbx_ref, dlam_ref, dg_ref):
                ref[...] = jnp.zeros_like(ref)

        lx = lx_ref[...]
        h = hl_ref[...]
        gate = gate_ref[...]
        cx = cx_ref[...]
        lam = lam_ref[...]
        r, ig, a, mult = r_ref[...], ig_ref[...], a_ref[...], mult_ref[...]
        sp = _softplus(-lam)
        ge, th = _gelu(gate)
        p = h * ge
        rn = _rms(p)
        dyv = dy_ref[...]
        dg_ref[...] += jnp.sum(dyv * p * rn, axis=0, keepdims=True)
        dp = _rms_bwd(dyv * g_ref[...], p, rn)
        dp_ref[:, LRU_W:2 * LRU_W] = (dp * h * _gelu_grad(gate, th)).astype(BF16)
        dh = dp * ge
        row = lax.broadcasted_iota(jnp.int32, a.shape, 0)
        b = jnp.where(row == TT - 1, acar[...], pltpu.roll(a, TT - 1, 0))
        g = _scan_tile(b, dh, gcar[...], sa, su, reverse=True)
        gcar[...] = _row(g[0:8], 0)
        acar[...] = _row(a[0:8], 0)
        h_last_prev = halo_ref[7:8, :] * (i < nt - 1).astype(F32)
        hprev = jnp.where(row == 0, h_last_prev, pltpu.roll(h, 1, 0))
        da = g * hprev
        dm2 = (g * (ig * lx)) * 0.5 / mult
        dlog_a = da * a - 2.0 * a * a * dm2
        dlam_ref[...] += jnp.sum(-8.0 * r * dlog_a, axis=0, keepdims=True) * (-_sigmoid(-lam))
        dpa = (-8.0 * sp * dlog_a) * r * (1.0 - r)
        dpx = (g * mult * lx) * ig * (1.0 - ig)
        dba_ref[...] += jnp.sum(dpa, axis=0, keepdims=True)
        dbx_ref[...] += jnp.sum(dpx, axis=0, keepdims=True)
        dpab = dpa.astype(BF16)
        dpxb = dpx.astype(BF16)
        dpa_ref[...] = dpab
        dpx_ref[...] = dpxb
        dlx = g * mult * ig + jnp.concatenate(
            [_dot(dpab[:, 256 * k:256 * (k + 1)], wa_ref[k], NT) + _dot(dpxb[:, 256 * k:256 * (k + 1)], wx_ref[k], NT)
             for k in range(4)], axis=1)
        nxt = head[...]
        dcb_ref[...] += jnp.sum(dlx, axis=0, keepdims=True)
        dcx = cw_ref[3:4, :] * dlx
        dcw_ref[3:4, :] += jnp.sum(cx * dlx, axis=0, keepdims=True)
        for j in range(1, 4):
            sh = _shift_up(dlx, nxt, j)
            dcx += cw_ref[3 - j:4 - j, :] * sh
            dcw_ref[3 - j:4 - j, :] += jnp.sum(cx * sh, axis=0, keepdims=True)
        head[...] = dlx[0:8]
        dp_ref[:, 0:LRU_W] = dcx.astype(BF16)

    rev = lambda i: (nt - 1 - i, 0)
    vec = pl.BlockSpec((1, LRU_W), lambda i: (0, 0))
    bd = pl.BlockSpec((4, 256, 256), lambda i: (0, 0, 0))
    tile = pl.BlockSpec((TT, LRU_W), rev)
    halo = pl.BlockSpec((8, LRU_W), lambda i: (jnp.maximum((nt - 1 - i) * (TT // 8) - 1, 0), 0))
    cw = pl.BlockSpec((4, LRU_W), lambda i: (0, 0))
    return pl.pallas_call(
        body, name="lru_bwd", grid=(nt,),
        in_specs=[tile, tile, pl.BlockSpec((TT, LRU_W), lambda i: (nt - 1 - i, 1)), tile, tile, halo,
                  tile, tile, tile, tile, cw, bd, bd, vec, vec],
        out_specs=[pl.BlockSpec((TT, 2 * LRU_W), rev), tile, tile, cw, vec, vec, vec, vec, vec],
        out_shape=[jax.ShapeDtypeStruct((T, 2 * LRU_W), BF16), jax.ShapeDtypeStruct((T, LRU_W), BF16),
                   jax.ShapeDtypeStruct((T, LRU_W), BF16), jax.ShapeDtypeStruct((4, LRU_W), F32)]
                  + [jax.ShapeDtypeStruct((1, LRU_W), F32)] * 5,
        scratch_shapes=[pltpu.VMEM((1, LRU_W), F32), pltpu.VMEM((1, LRU_W), F32), pltpu.VMEM((8, LRU_W), F32)]
                       + [pltpu.VMEM((LRU_W // LANE, TT, LANE), F32)] * 2,
        compiler_params=_params(1),
    )(dy, proj, proj, lx, hl, hl, *gates, conv_w, wa_bd, wx_bd, lam, g_lru)


def _ssd_chunk_terms(xc, dtp, bias, alog, expand):
    sg = _sigmoid(xc)
    xbc = xc * sg
    pre = dtp + bias
    dt = _softplus(pre)
    A = -jnp.exp(alog)
    ri = lax.broadcasted_iota(jnp.int32, (CHUNK, CHUNK), 0)
    ci = lax.broadcasted_iota(jnp.int32, (CHUNK, CHUNK), 1)
    tril = (ri >= ci).astype(BF16)
    cs = _dot_sel(dt * A, tril, terms=3, sel_first=True)
    cs_last = _row(cs, CHUNK - 1)
    ecs = jnp.exp(cs)
    dec = jnp.exp(cs_last - cs)
    return dict(sg=sg, xbc=xbc, pre=pre, dt=dt, A=A, cs=cs, csT=cs.T, ecs=ecs, dec=dec, ri=ri, ci=ci,
                dt_e=_dot_sel(dt, expand), ecs_e=_dot_sel(ecs, expand), dec_e=_dot_sel(dec, expand))


def _head_lambda(t, h, cst_ref):
    col = jnp.sum(jnp.where(t["ci"] == h, t["cs"], 0.0), axis=1, keepdims=True)
    return jnp.exp(jnp.where(t["ri"] >= t["ci"], col - cst_ref[h:h + 1, :], -1e30))


def _ssd_fwd(proj, dtp, conv_w, conv_b, dt_bias, a_log, d_e, g_ssd, expand):
    T = proj.shape[0]
    nc = T // CHUNK

    def body(z_ref, xp_ref, dtp_ref, cw_ref, cb_ref, bias_ref, alog_ref, de_ref, g_ref, ex_ref,
             xc_ref, y_ref, yn_ref, sprev_ref, tail, S, cst):
        @pl.when(pl.program_id(0) == 0)
        def _():
            tail[...] = jnp.zeros_like(tail)
            S[...] = jnp.zeros_like(S)

        xp = xp_ref[...]
        prev8 = tail[...]
        xc = cb_ref[...] + cw_ref[3:4, :] * xp
        for j in range(1, 4):
            xc += cw_ref[3 - j:4 - j, :] * _shift_down(xp, prev8, j)
        tail[...] = xp[CHUNK - 8:CHUNK]
        xc_ref[...] = xc
        t = _ssd_chunk_terms(xc, dtp_ref[...], bias_ref[...], alog_ref[...], ex_ref[...])
        cst[...] = t["csT"]
        xbc = t["xbc"]
        sx = xbc[:, 0:SSD_INNER]
        Bb = xbc[:, SSD_INNER:SSD_INNER + 256].astype(BF16)
        Cb = xbc[:, SSD_INNER + 256:SSD_CONV_CH].astype(BF16)
        X = t["dt_e"] * sx
        lane = lax.broadcasted_iota(jnp.int32, (CHUNK, LANE), 1)
        G = [_dot(Cb[:, 128 * g:128 * (g + 1)], Bb[:, 128 * g:128 * (g + 1)], NT) for g in range(SSD_GROUPS)]
        for k in range(SSD_HEADS // 2):
            Xp = X[:, 128 * k:128 * (k + 1)]
            acc = jnp.zeros((CHUNK, LANE), F32)
            for half in range(2):
                M = (G[k // 4] * _head_lambda(t, 2 * k + half, cst)).astype(BF16)
                Xh = jnp.where((lane >= 64) if half else (lane < 64), Xp, 0.0).astype(BF16)
                acc += _dot(M, Xh)
            y_ref[:, 128 * k:128 * (k + 1)] = acc
        sprev_ref[0] = S[...]
        eL_e = _row(t["ecs_e"], CHUNK - 1)
        Xd = (X * t["dec_e"]).astype(BF16)
        for g in range(SSD_GROUPS):
            sl = slice(512 * g, 512 * (g + 1))
            Sg = S[:, sl]
            y_ref[:, sl] += t["ecs_e"][:, sl] * _dot(Cb[:, 128 * g:128 * (g + 1)], Sg.astype(BF16))
            S[:, sl] = eL_e[:, sl] * Sg + _dot(Bb[:, 128 * g:128 * (g + 1)], Xd[:, sl], TN)
        y = y_ref[...] + de_ref[...] * sx
        y_ref[...] = y
        z = z_ref[...]
        q = y * (z * _sigmoid(z))
        yn_ref[...] = (q * _rms(q) * g_ref[...]).astype(BF16)

    c0 = lambda i: (0, 0)
    return pl.pallas_call(
        body, name="ssd_fwd", grid=(nc,),
        in_specs=[pl.BlockSpec((CHUNK, SSD_INNER), lambda i: (i, 2)),
                  pl.BlockSpec((CHUNK, SSD_CONV_CH), lambda i: (i, 2)),
                  pl.BlockSpec((CHUNK, LANE), lambda i: (i, 0)),
                  pl.BlockSpec((4, SSD_CONV_CH), c0), pl.BlockSpec((1, SSD_CONV_CH), c0),
                  pl.BlockSpec((1, LANE), c0), pl.BlockSpec((1, LANE), c0),
                  pl.BlockSpec((1, SSD_INNER), c0), pl.BlockSpec((1, SSD_INNER), c0),
                  pl.BlockSpec((LANE, SSD_INNER), c0)],
        out_specs=[pl.BlockSpec((CHUNK, SSD_CONV_CH), lambda i: (i, 0)),
                   pl.BlockSpec((CHUNK, SSD_INNER), lambda i: (i, 0)),
                   pl.BlockSpec((CHUNK, SSD_INNER), lambda i: (i, 0)),
                   pl.BlockSpec((1, SSD_STATE, SSD_INNER), lambda i: (i, 0, 0))],
        out_shape=[jax.ShapeDtypeStruct((T, SSD_CONV_CH), F32), jax.ShapeDtypeStruct((T, SSD_INNER), F32),
                   jax.ShapeDtypeStruct((T, SSD_INNER), BF16),
                   jax.ShapeDtypeStruct((nc, SSD_STATE, SSD_INNER), F32)],
        scratch_shapes=[pltpu.VMEM((8, SSD_CONV_CH), F32), pltpu.VMEM((SSD_STATE, SSD_INNER), F32),
                        pltpu.VMEM((CHUNK, CHUNK), F32)],
        compiler_params=_params(1),
    )(proj, proj, dtp, conv_w, conv_b, dt_bias, a_log, d_e, g_ssd, expand)


def _ssd_bwd(dyn, proj, dtp, xc, y, sprev, conv_w, dt_bias, a_log, d_e, g_ssd, expand):
    T = proj.shape[0]
    nc = T // CHUNK

    def body(dyn_ref, z_ref, xp_ref, dtp_ref, xc_ref, y_ref, sprev_ref, cw_ref, bias_ref, alog_ref, de_ref,
             g_ref, ex_ref, dp_ref, ddtp_ref, dcw_ref, dcb_ref, dbias_ref, dA_ref, dD_ref, dg_ref,
             dS, head, dX_s, dxbc_s, cst, dcst):
        @pl.when(pl.program_id(0) == 0)
        def _():
            dS[...] = jnp.zeros_like(dS)
            head[...] = jnp.zeros_like(head)
            dcst[...] = jnp.zeros_like(dcst)
            for ref in (dcw_ref, dcb_ref, dbias_ref, dA_ref, dD_ref, dg_ref):
                ref[...] = jnp.zeros_like(ref)

        ex = ex_ref[...]
        xc = xc_ref[...]
        t = _ssd_chunk_terms(xc, dtp_ref[...], bias_ref[...], alog_ref[...], ex)
        cst[...] = t["csT"]
        ri, ci = t["ri"], t["ci"]
        xbc = t["xbc"]
        sx = xbc[:, 0:SSD_INNER]
        Bb = xbc[:, SSD_INNER:SSD_INNER + 256].astype(BF16)
        Cb = xbc[:, SSD_INNER + 256:SSD_CONV_CH].astype(BF16)
        X = t["dt_e"] * sx
        z = z_ref[...]
        sz = _sigmoid(z)
        siluz = z * sz
        yv = y_ref[...]
        q = yv * siluz
        rn = _rms(q)
        dynv = dyn_ref[...]
        dg_ref[...] += jnp.sum(dynv * q * rn, axis=0, keepdims=True)
        dq = _rms_bwd(dynv * g_ref[...], q, rn)
        dp_ref[:, 0:SSD_INNER] = (dq * yv * (sz * (1.0 + z * (1.0 - sz)))).astype(BF16)
        dY = dq * siluz
        dD_ref[...] += jnp.sum(dY * sx, axis=0, keepdims=True)
        dYb = dY.astype(BF16)
        lane = lax.broadcasted_iota(jnp.int32, (CHUNK, LANE), 1)
        dcs = jnp.zeros((CHUNK, CHUNK), F32)
        Xb = X.astype(BF16)
        for g in range(SSD_GROUPS):
            Bg = Bb[:, 128 * g:128 * (g + 1)]
            Cg = Cb[:, 128 * g:128 * (g + 1)]
            G = _dot(Cg, Bg, NT)
            dGsum = jnp.zeros((CHUNK, CHUNK), F32)
            for k in range(4 * g, 4 * g + 4):
                Xp = Xb[:, 128 * k:128 * (k + 1)]
                dYp = dY[:, 128 * k:128 * (k + 1)]
                dXp = jnp.zeros((CHUNK, LANE), F32)
                for half in range(2):
                    h = 2 * k + half
                    lam = _head_lambda(t, h, cst)
                    M = G * lam
                    dYh = jnp.where((lane >= 64) if half else (lane < 64), dYp, 0.0).astype(BF16)
                    dM = _dot(dYh, Xp, NT)
                    W = dM * M
                    dcs += jnp.where(ci == h, jnp.sum(W, axis=1, keepdims=True), 0.0)
                    dcst[h:h + 1, :] = jnp.sum(W, axis=0, keepdims=True)
                    dGsum += dM * lam
                    dXp += _dot(M.astype(BF16), dYh, TN)
                dX_s[:, 128 * k:128 * (k + 1)] = dXp
            dGb = dGsum.astype(BF16)
            dxbc_s[:, SSD_INNER + 256 + 128 * g:SSD_INNER + 256 + 128 * (g + 1)] = _dot(dGb, Bg)
            dxbc_s[:, SSD_INNER + 128 * g:SSD_INNER + 128 * (g + 1)] = _dot(dGb, Cg, TN)
        dcs = dcs - dcst[...].T
        Sp = sprev_ref[0]
        dSv = dS[...]
        ecs_e, dec_e = t["ecs_e"], t["dec_e"]
        eL_e = _row(ecs_e, CHUNK - 1)
        dYe = dY * ecs_e
        dYeb = dYe.astype(BF16)
        Xd = X * dec_e
        Xdb = Xd.astype(BF16)
        for g in range(SSD_GROUPS):
            sl = slice(512 * g, 512 * (g + 1))
            Bg = Bb[:, 128 * g:128 * (g + 1)]
            Cg = Cb[:, 128 * g:128 * (g + 1)]
            Spb = Sp[:, sl].astype(BF16)
            dSb = dSv[:, sl].astype(BF16)
            CS = _dot(Cg, Spb)
            BS = _dot(Bg, dSb)
            dxbc_s[:, SSD_INNER + 256 + 128 * g:SSD_INNER + 256 + 128 * (g + 1)] += _dot(dYeb[:, sl], Spb, NT)
            dxbc_s[:, SSD_INNER + 128 * g:SSD_INNER + 128 * (g + 1)] += _dot(Xdb[:, sl], dSb, NT)
            dS[:, sl] = eL_e[:, sl] * dSv[:, sl] + _dot(Cg, dYeb[:, sl], TN)
            dX_s[:, sl] += dec_e[:, sl] * BS
            dcs += _dot_sel(dYe[:, sl] * CS, ex[:, sl], NT, terms=1)
            tdec = _dot_sel(X[:, sl] * BS, ex[:, sl], NT, terms=1) * t["dec"]
            dcs -= tdec
            last = jnp.sum(tdec, axis=0, keepdims=True)
            last += jnp.sum(_dot_sel(Sp[:, sl] * dSv[:, sl], ex[:, sl], NT, terms=1), axis=0, keepdims=True) \
                * _row(t["ecs"], CHUNK - 1)
            dcs += jnp.where(ri == CHUNK - 1, last, 0.0)
        triu = (ci >= ri).astype(BF16)
        da = _dot_sel(dcs, triu, terms=3, sel_first=True)
        dX = dX_s[...]
        ddt = da * t["A"] + _dot_sel(dX * sx, ex, NT, terms=1)
        dA_ref[...] += jnp.sum(da * t["dt"], axis=0, keepdims=True)
        ddtp = ddt * _sigmoid(t["pre"])
        dbias_ref[...] += jnp.sum(ddtp, axis=0, keepdims=True)
        ddtp_ref[...] = ddtp.astype(BF16)
        dxbc_s[:, 0:SSD_INNER] = dX * t["dt_e"] + de_ref[...] * dY
        sg = t["sg"]
        dxc = dxbc_s[...] * (sg * (1.0 + xc * (1.0 - sg)))
        xp = xp_ref[...]
        nxt = head[...]
        dcb_ref[...] += jnp.sum(dxc, axis=0, keepdims=True)
        dpre = cw_ref[3:4, :] * dxc
        dcw_ref[3:4, :] += jnp.sum(xp * dxc, axis=0, keepdims=True)
        for j in range(1, 4):
            sh = _shift_up(dxc, nxt, j)
            dpre += cw_ref[3 - j:4 - j, :] * sh
            dcw_ref[3 - j:4 - j, :] += jnp.sum(xp * sh, axis=0, keepdims=True)
        head[...] = dxc[0:8]
        dp_ref[:, SSD_INNER:SSD_INNER + SSD_CONV_CH] = dpre.astype(BF16)

    c0 = lambda i: (0, 0)
    rev = lambda i: (nc - 1 - i, 0)
    return pl.pallas_call(
        body, name="ssd_bwd", grid=(nc,),
        in_specs=[pl.BlockSpec((CHUNK, SSD_INNER), rev),
                  pl.BlockSpec((CHUNK, SSD_INNER), lambda i: (nc - 1 - i, 2)),
                  pl.BlockSpec((CHUNK, SSD_CONV_CH), lambda i: (nc - 1 - i, 2)),
                  pl.BlockSpec((CHUNK, LANE), rev),
                  pl.BlockSpec((CHUNK, SSD_CONV_CH), rev),
                  pl.BlockSpec((CHUNK, SSD_INNER), rev),
                  pl.BlockSpec((1, SSD_STATE, SSD_INNER), lambda i: (nc - 1 - i, 0, 0)),
                  pl.BlockSpec((4, SSD_CONV_CH), c0), pl.BlockSpec((1, LANE), c0), pl.BlockSpec((1, LANE), c0),
                  pl.BlockSpec((1, SSD_INNER), c0), pl.BlockSpec((1, SSD_INNER), c0),
                  pl.BlockSpec((LANE, SSD_INNER), c0)],
        out_specs=[pl.BlockSpec((CHUNK, 2560), rev), pl.BlockSpec((CHUNK, LANE), rev),
                   pl.BlockSpec((4, SSD_CONV_CH), c0), pl.BlockSpec((1, SSD_CONV_CH), c0),
                   pl.BlockSpec((1, LANE), c0), pl.BlockSpec((1, LANE), c0),
                   pl.BlockSpec((1, SSD_INNER), c0), pl.BlockSpec((1, SSD_INNER), c0)],
        out_shape=[jax.ShapeDtypeStruct((T, 2560), BF16), jax.ShapeDtypeStruct((T, LANE), BF16),
                   jax.ShapeDtypeStruct((4, SSD_CONV_CH), F32), jax.ShapeDtypeStruct((1, SSD_CONV_CH), F32),
                   jax.ShapeDtypeStruct((1, LANE), F32), jax.ShapeDtypeStruct((1, LANE), F32),
                   jax.ShapeDtypeStruct((1, SSD_INNER), F32), jax.ShapeDtypeStruct((1, SSD_INNER), F32)],
        scratch_shapes=[pltpu.VMEM((SSD_STATE, SSD_INNER), F32), pltpu.VMEM((8, SSD_CONV_CH), F32),
                        pltpu.VMEM((CHUNK, SSD_INNER), F32), pltpu.VMEM((CHUNK, SSD_CONV_CH), F32),
                        pltpu.VMEM((CHUNK, CHUNK), F32), pltpu.VMEM((CHUNK, CHUNK), F32)],
        compiler_params=_params(1),
    )(dyn, proj, proj, dtp, xc, y, sprev, conv_w, dt_bias, a_log, d_e, g_ssd, expand)


def _outproj_fwd(x, y_lru, y_ssd, w_out, g_pm, g_pf):
    T = x.shape[0]
    TT = TP

    def body(x_ref, yl_ref, ys_ref, wo_ref, gpm_ref, gpf_ref, mix_ref, x1_ref, h2_ref):
        mix = _dot(yl_ref[...], wo_ref[0:LRU_W, :]) + _dot(ys_ref[...], wo_ref[LRU_W:2 * LRU_W, :])
        mix_ref[...] = mix
        x1 = x_ref[...] + mix * _rms(mix) * gpm_ref[...]
        x1_ref[...] = x1
        h2_ref[...] = (x1 * _rms(x1) * gpf_ref[...]).astype(BF16)

    tile = pl.BlockSpec((TT, D_MODEL), lambda i: (i, 0))
    vec = pl.BlockSpec((1, D_MODEL), lambda i: (0, 0))
    return pl.pallas_call(
        body, name="outproj_fwd", grid=(T // TT,),
        in_specs=[tile, tile, tile, pl.BlockSpec((2 * LRU_W, D_MODEL), lambda i: (0, 0)), vec, vec],
        out_specs=[tile, tile, tile],
        out_shape=[jax.ShapeDtypeStruct((T, D_MODEL), F32), jax.ShapeDtypeStruct((T, D_MODEL), F32),
                   jax.ShapeDtypeStruct((T, D_MODEL), BF16)],
        compiler_params=_params(1),
    )(x, y_lru, y_ssd, w_out, g_pm, g_pf)


def _ffn_fwd_bwd(x1, h2, target, w_gate, w_up, w_down, g_pf, g_ff):
    T = x1.shape[0]

    def body(x1_ref, h2_ref, tg_ref, wg_hbm, wu_hbm, wd_hbm, gpf_ref, gff_ref,
             dx1_ref, act_ref, df_ref, dgt_ref, dup_ref, dgpf_ref, dgff_ref, loss_ref,
             wg, wu, wd, gt_s, up_s, sem):
        @pl.when(pl.program_id(0) == 0)
        def _():
            cps = [pltpu.make_async_copy(s, d, sem.at[n]) for n, (s, d) in
                   enumerate(((wg_hbm, wg), (wu_hbm, wu), (wd_hbm, wd)))]
            for c in cps:
                c.start()
            for c in cps:
                c.wait()
            for ref in (dgpf_ref, dgff_ref, loss_ref):
                ref[...] = jnp.zeros_like(ref)

        h2 = h2_ref[...]
        f = jnp.zeros((TT, D_MODEL), F32)
        for c in range(D_FF // FF_CHUNK):
            sl = slice(FF_CHUNK * c, FF_CHUNK * (c + 1))
            gt = _dot(h2, wg[:, sl])
            up = _dot(h2, wu[:, sl])
            gt_s[:, sl] = gt
            up_s[:, sl] = up
            act = (gt * _sigmoid(gt) * up).astype(BF16)
            act_ref[:, sl] = act
            f += _dot(act, wd[sl, :])
        x1 = x1_ref[...]
        rnf = _rms(f)
        e = x1 + f * rnf * gff_ref[...] - tg_ref[...]
        part = 0.5 * jnp.sum(jnp.sum(e * e, axis=1, keepdims=True), axis=0, keepdims=True) * (1.0 / D_MODEL)
        lane = lax.broadcasted_iota(jnp.int32, (1, LANE), 1)
        loss_ref[...] += jnp.where(lane == 0, part, 0.0)
        dx2 = e * (1.0 / D_MODEL)
        dgff_ref[...] += jnp.sum(dx2 * f * rnf, axis=0, keepdims=True)
        df = _rms_bwd(dx2 * gff_ref[...], f, rnf).astype(BF16)
        df_ref[...] = df
        dh2 = jnp.zeros((TT, D_MODEL), F32)
        for c in range(D_FF // FF_CHUNK):
            sl = slice(FF_CHUNK * c, FF_CHUNK * (c + 1))
            dact = _dot(df, wd[sl, :], NT)
            gt = gt_s[:, sl]
            up = up_s[:, sl]
            sg = _sigmoid(gt)
            dgt = (dact * up * (sg * (1.0 + gt * (1.0 - sg)))).astype(BF16)
            dup = (dact * gt * sg).astype(BF16)
            dgt_ref[:, sl] = dgt
            dup_ref[:, sl] = dup
            dh2 += _dot(dgt, wg[:, sl], NT) + _dot(dup, wu[:, sl], NT)
        rn2 = _rms(x1)
        dgpf_ref[...] += jnp.sum(dh2 * x1 * rn2, axis=0, keepdims=True)
        dx1_ref[...] = dx2 + _rms_bwd(dh2 * gpf_ref[...], x1, rn2)

    tile = pl.BlockSpec((TT, D_MODEL), lambda i: (i, 0))
    wide = pl.BlockSpec((TT, D_FF), lambda i: (i, 0))
    vec = pl.BlockSpec((1, D_MODEL), lambda i: (0, 0))
    hbm = pl.BlockSpec(memory_space=pl.ANY)
    return pl.pallas_call(
        body, name="ffn_fwd_bwd", grid=(T // TT,),
        in_specs=[tile, tile, tile, hbm, hbm, hbm, vec, vec],
        out_specs=[tile, wide, tile, wide, wide, vec, vec, pl.BlockSpec((1, LANE), lambda i: (0, 0))],
        out_shape=[jax.ShapeDtypeStruct((T, D_MODEL), F32), jax.ShapeDtypeStruct((T, D_FF), BF16),
                   jax.ShapeDtypeStruct((T, D_MODEL), BF16), jax.ShapeDtypeStruct((T, D_FF), BF16),
                   jax.ShapeDtypeStruct((T, D_FF), BF16), jax.ShapeDtypeStruct((1, D_MODEL), F32),
                   jax.ShapeDtypeStruct((1, D_MODEL), F32), jax.ShapeDtypeStruct((1, LANE), F32)],
        scratch_shapes=[pltpu.VMEM((D_MODEL, D_FF), BF16), pltpu.VMEM((D_MODEL, D_FF), BF16),
                        pltpu.VMEM((D_FF, D_MODEL), BF16), pltpu.VMEM((TT, D_FF), F32),
                        pltpu.VMEM((TT, D_FF), F32), pltpu.SemaphoreType.DMA((3,))],
        compiler_params=_params(1),
    )(x1, h2, target, w_gate, w_up, w_down, g_pf, g_ff)


def _outproj_bwd(dx1, mix, w_out, g_pm):
    T = dx1.shape[0]
    TT = TP

    def body(dx1_ref, mix_ref, wo_ref, gpm_ref, dyl_ref, dys_ref, dmix_ref, dg_ref):
        @pl.when(pl.program_id(0) == 0)
        def _():
            dg_ref[...] = jnp.zeros_like(dg_ref)

        mix = mix_ref[...]
        rn = _rms(mix)
        dx1v = dx1_ref[...]
        dg_ref[...] += jnp.sum(dx1v * mix * rn, axis=0, keepdims=True)
        dmix = _rms_bwd(dx1v * gpm_ref[...], mix, rn).astype(BF16)
        dmix_ref[...] = dmix
        dyl_ref[...] = _dot(dmix, wo_ref[0:LRU_W, :], NT)
        dys_ref[...] = _dot(dmix, wo_ref[LRU_W:2 * LRU_W, :], NT)

    tile = pl.BlockSpec((TT, D_MODEL), lambda i: (i, 0))
    vec = pl.BlockSpec((1, D_MODEL), lambda i: (0, 0))
    return pl.pallas_call(
        body, name="outproj_bwd", grid=(T // TT,),
        in_specs=[tile, tile, pl.BlockSpec((2 * LRU_W, D_MODEL), lambda i: (0, 0)), vec],
        out_specs=[tile, tile, tile, vec],
        out_shape=[jax.ShapeDtypeStruct((T, D_MODEL), F32), jax.ShapeDtypeStruct((T, D_MODEL), F32),
                   jax.ShapeDtypeStruct((T, D_MODEL), BF16), jax.ShapeDtypeStruct((1, D_MODEL), F32)],
        compiler_params=_params(1),
    )(dx1, mix, w_out, g_pm)


def _tn_matmul(a, bs, name, tk=512):
    T, M = a.shape
    tk = min(tk, T)
    nk = T // tk
    nb = len(bs)

    def body(*refs):
        a_ref, b_refs, o_refs, accs = refs[0], refs[1:1 + nb], refs[1 + nb:1 + 2 * nb], refs[1 + 2 * nb:]
        k = pl.program_id(0)

        @pl.when(k == 0)
        def _():
            for acc in accs:
                acc[...] = jnp.zeros_like(acc)

        av = a_ref[...].astype(BF16)
        for b_ref, acc in zip(b_refs, accs):
            acc[...] += _dot(av, b_ref[...], TN)

        @pl.when(k == nk - 1)
        def _():
            for o_ref, acc in zip(o_refs, accs):
                o_ref[...] = acc[...].astype(BF16)

    return pl.pallas_call(
        body, name=name, grid=(nk,),
        in_specs=[pl.BlockSpec((tk, M), lambda k: (k, 0))]
                 + [pl.BlockSpec((tk, b.shape[1]), lambda k: (k, 0)) for b in bs],
        out_specs=[pl.BlockSpec((M, b.shape[1]), lambda k: (0, 0)) for b in bs],
        out_shape=[jax.ShapeDtypeStruct((M, b.shape[1]), BF16) for b in bs],
        scratch_shapes=[pltpu.VMEM((M, b.shape[1]), F32) for b in bs],
        compiler_params=_params(1),
    )(a, *bs)


def _tn_matmul_slabs(a, bs, name, slab, tk=512):
    T, M = a.shape
    tk = min(tk, T)
    nk = T // tk
    nb = len(bs)
    offs = [sum(b.shape[1] for b in bs[:i]) for i in range(nb + 1)]

    def body(*refs):
        a_ref, b_refs, o_ref, acc = refs[0], refs[1:1 + nb], refs[1 + nb], refs[2 + nb]
        k = pl.program_id(0)

        @pl.when(k == 0)
        def _():
            acc[...] = jnp.zeros_like(acc)

        av = a_ref[...].astype(BF16)
        for i, b_ref in enumerate(b_refs):
            acc[:, offs[i]:offs[i + 1]] += _dot(av, b_ref[...], TN)

        @pl.when(k == nk - 1)
        def _():
            for j in range(N_DEV):
                o_ref[j] = acc[:, slab * j:slab * (j + 1)].astype(BF16)

    return pl.pallas_call(
        body, name=name, grid=(nk,),
        in_specs=[pl.BlockSpec((tk, M), lambda k: (k, 0))]
                 + [pl.BlockSpec((tk, b.shape[1]), lambda k: (k, 0)) for b in bs],
        out_specs=pl.BlockSpec((N_DEV, M, slab), lambda k: (0, 0, 0)),
        out_shape=jax.ShapeDtypeStruct((N_DEV, M, slab), BF16),
        scratch_shapes=[pltpu.VMEM((M, offs[-1]), F32)],
        compiler_params=_params(1),
    )(a, *bs)


def _tn_blockdiag(a, b1, b2, name, tk=2048):
    T = a.shape[0]
    tk = min(tk, T)

    def body(a_ref, b1_ref, b2_ref, o1_ref, o2_ref):
        @pl.when(pl.program_id(0) == 0)
        def _():
            o1_ref[...] = jnp.zeros_like(o1_ref)
            o2_ref[...] = jnp.zeros_like(o2_ref)

        for j in range(4):
            sl = slice(256 * j, 256 * (j + 1))
            av = a_ref[:, sl].astype(BF16)
            o1_ref[j] += _dot(av, b1_ref[:, sl], TN)
            o2_ref[j] += _dot(av, b2_ref[:, sl], TN)

    blk = pl.BlockSpec((tk, LRU_W), lambda k: (k, 0))
    out = pl.BlockSpec((4, 256, 256), lambda k: (0, 0, 0))
    return pl.pallas_call(
        body, name=name, grid=(T // tk,),
        in_specs=[blk, blk, blk], out_specs=[out, out],
        out_shape=[jax.ShapeDtypeStruct((4, 256, 256), F32)] * 2,
        compiler_params=_params(1),
    )(a, b1, b2)


def _adamw_update(g, w, m, v):
    nm = ADAM_B1 * m + (1.0 - ADAM_B1) * g
    nv = ADAM_B2 * v + (1.0 - ADAM_B2) * (g * g)
    m_hat = nm / (1.0 - ADAM_B1 ** ADAM_STEP)
    v_hat = nv / (1.0 - ADAM_B2 ** ADAM_STEP)
    return -ADAM_LR * (m_hat / (jnp.sqrt(v_hat) + ADAM_EPS) + ADAM_WD * w), nm, nv


def _adamw_small(parts, params):
    n = len(params)

    def body(*refs):
        p_ref, wmv = refs[0], refs[1:1 + 3 * n]
        gs_ref, outs = refs[1 + 3 * n], refs[2 + 3 * n:]
        g = p_ref[0]
        for k in range(1, N_DEV):
            g = g + p_ref[k]
        gs_ref[...] = g
        for i, (row, w, _, _) in enumerate(params):
            rows, width = w.shape
            if width <= 1024:
                gp = gs_ref[row:row + rows, 0:width]
            else:
                gp = jnp.concatenate([gs_ref[row:row + 1, :], gs_ref[row + 1:row + 2, 0:width - 1024]], axis=1)
            d, nm, nv = _adamw_update(gp, wmv[3 * i][...], wmv[3 * i + 1][...], wmv[3 * i + 2][...])
            for ref, val in zip(outs[4 * i:4 * i + 4], (gp, d, nm, nv)):
                ref[...] = val

    full = lambda s: pl.BlockSpec(s, lambda: (0,) * len(s))
    shapes = [w.shape for _, w, _, _ in params]
    return pl.pallas_call(
        body, name="adamw_small",
        in_specs=[full(parts.shape)] + [full(s) for s in shapes for _ in range(3)],
        out_specs=[full(parts.shape[1:])] + [full(s) for s in shapes for _ in range(4)],
        out_shape=[jax.ShapeDtypeStruct(parts.shape[1:], F32)]
                  + [jax.ShapeDtypeStruct(s, F32) for s in shapes for _ in range(4)],
        compiler_params=pltpu.CompilerParams(vmem_limit_bytes=VMEM_LIMIT),
    )(parts, *[a for _, w, m, v in params for a in (w, m, v)])


def _adamw_shard(lands, slabs, me, w, m, v, name, tr):
    P, R, C = lands.shape

    def body(me_ref, p_ref, own_ref, w_ref, m_ref, v_ref, g_ref, d_ref, nm_ref, nv_ref):
        g = None
        for k in range(P):
            part = jnp.where(me_ref[0] == k, own_ref[0], p_ref[k]).astype(F32)
            g = part if g is None else g + part
        g_ref[...] = g
        d_ref[...], nm_ref[...], nv_ref[...] = _adamw_update(g, w_ref[...], m_ref[...], v_ref[...])

    blk = pl.BlockSpec((tr, C), lambda i, me_ref: (i, 0))
    return pl.pallas_call(
        body, name=name,
        grid_spec=pltpu.PrefetchScalarGridSpec(
            num_scalar_prefetch=1, grid=(R // tr,),
            in_specs=[pl.BlockSpec((P, tr, C), lambda i, me_ref: (0, i, 0)),
                      pl.BlockSpec((1, tr, C), lambda i, me_ref: (me_ref[0], i, 0)), blk, blk, blk],
            out_specs=[blk, blk, blk, blk]),
        out_shape=[jax.ShapeDtypeStruct((R, C), F32)] * 4,
        compiler_params=_params(1),
    )(me, lands, slabs, w, m, v)


def _adamw(parts, w, m, v, name, tr):
    P, R, C = parts.shape

    def body(p_ref, w_ref, m_ref, v_ref, g_ref, d_ref, nm_ref, nv_ref):
        g = p_ref[0].astype(F32)
        for k in range(1, P):
            g = g + p_ref[k].astype(F32)
        g_ref[...] = g
        d_ref[...], nm_ref[...], nv_ref[...] = _adamw_update(g, w_ref[...], m_ref[...], v_ref[...])

    blk = pl.BlockSpec((tr, C), lambda i: (i, 0))
    return pl.pallas_call(
        body, name=name, grid=(R // tr,),
        in_specs=[pl.BlockSpec((P, tr, C), lambda i: (0, i, 0)), blk, blk, blk],
        out_specs=[blk, blk, blk, blk],
        out_shape=[jax.ShapeDtypeStruct((R, C), F32)] * 4,
        compiler_params=_params(1),
    )(parts, w, m, v)


def _peer(k):
    x, y, c = lax.axis_index("x"), lax.axis_index("y"), lax.axis_index("c")
    px = x ^ ((k >> 2) & 1)
    py = y ^ ((k >> 1) & 1)
    pc = c ^ (k & 1)
    return (px, py, pc), 4 * px + 2 * py + pc


def _my_block():
    return 4 * lax.axis_index("x") + 2 * lax.axis_index("y") + lax.axis_index("c")


def _all_gather(shards, name):
    n = len(shards)

    def body(*refs):
        ins, outs = refs[:n], refs[n:2 * n]
        send, recv, loc = refs[2 * n:]
        x, y, c = lax.axis_index("x"), lax.axis_index("y"), lax.axis_index("c")
        sibling = (x, y, 1 - c)
        chips = [(1 - x, y), (x, 1 - y), (1 - x, 1 - y)]
        slot = lambda px, py, pc: 4 * px + 2 * py + pc

        def copy(a, k, block, to, src=None):
            dst = outs[a].at[slot(*block)]
            return pltpu.make_async_remote_copy(
                src_ref=dst if src is None else src, dst_ref=dst, send_sem=send.at[a, k], recv_sem=recv.at[a, k],
                device_id=to, device_id_type=pl.DeviceIdType.MESH)

        mine = [pltpu.make_async_copy(ins[a], outs[a].at[slot(x, y, c)], loc.at[a]) for a in range(n)]
        for cp in mine:
            cp.start()
        first = []
        for a in range(n):
            first.append(copy(a, 0, (x, y, c), sibling, src=ins[a]))
            first += [copy(a, 1 + j, (x, y, c), (*chip, c), src=ins[a]) for j, chip in enumerate(chips)]
        for cp in first:
            cp.start()
        passed = []
        for j, chip in enumerate(chips):
            for a in range(n):
                copy(a, 1 + j, (*chip, c), (x, y, c)).wait_recv()
                fwd = copy(a, 4 + j, (*chip, c), sibling)
                fwd.start()
                passed.append(fwd)
        for a in range(n):
            copy(a, 0, sibling, (x, y, c)).wait_recv()
            for j, chip in enumerate(chips):
                copy(a, 4 + j, (*chip, 1 - c), (x, y, c)).wait_recv()
        for cp in first + passed:
            cp.wait_send()
        for cp in mine:
            cp.wait()

    hbm = pl.BlockSpec(memory_space=pl.ANY)
    return pl.pallas_call(
        body, name=name,
        in_specs=[hbm] * n, out_specs=[hbm] * n,
        out_shape=[jax.ShapeDtypeStruct((N_DEV,) + s.shape, s.dtype) for s in shards],
        scratch_shapes=[pltpu.SemaphoreType.DMA((n, N_DEV - 1)), pltpu.SemaphoreType.DMA((n, N_DEV - 1)),
                        pltpu.SemaphoreType.DMA((n,))],
    )(*shards)


_HBM = pl.BlockSpec(memory_space=pltpu.HBM)
_SEM = pl.BlockSpec(memory_space=pltpu.SEMAPHORE)
_EFFECT = pltpu.SideEffectType.DATAFLOW_SIDE_EFFECTING


def _direct_copies(srcs, lands, send, recv, slab_source):
    me = _my_block()
    cps = []
    for k in range(1, N_DEV):
        to, blk = _peer(k)
        for a, (src, land) in enumerate(zip(srcs, lands)):
            if slab_source:
                s, d = src.at[blk], land.at[me]
            elif land.ndim == 3:
                s, d = src, land.at[me]
            else:
                s, d = src, land.at[pl.ds(pl.multiple_of(me * src.shape[0], 16), src.shape[0]), :]
            cps.append(pltpu.make_async_remote_copy(
                src_ref=s, dst_ref=d,
                send_sem=send.at[a * (N_DEV - 1) + k - 1], recv_sem=recv.at[a * (N_DEV - 1) + k - 1],
                device_id=to, device_id_type=pl.DeviceIdType.MESH))
    return cps


def _exchange_start(srcs, name, slab_source, axes=None):
    n = len(srcs)
    if slab_source:
        shapes = [s.shape for s in srcs]
    else:
        shapes = [(N_DEV,) + s.shape if ax == 1 else (N_DEV * s.shape[0], s.shape[1]) for s, ax in zip(srcs, axes)]
    lands = [pltpu.with_memory_space_constraint(lax.empty(sh, s.dtype), pltpu.HBM) for sh, s in zip(shapes, srcs)]

    def body(*refs):
        ins, land_in = refs[:n], refs[n:2 * n]
        send, recv = refs[2 * n], refs[2 * n + 1]
        token = refs[4 * n + 2]
        for cp in _direct_copies(ins, land_in, send, recv, slab_source):
            cp.start()
        token[...] = jnp.zeros_like(token)

    sems = pltpu.SemaphoreType.DMA((n * (N_DEV - 1),))
    res = pl.pallas_call(
        body, name=name,
        out_shape=(sems, sems, *[pltpu.HBM(s.shape, s.dtype) for s in srcs],
                   *[pltpu.HBM(l.shape, l.dtype) for l in lands], jax.ShapeDtypeStruct((8, LANE), F32)),
        in_specs=[_HBM] * (2 * n),
        out_specs=(_SEM, _SEM, *[_HBM] * (2 * n), pl.BlockSpec(memory_space=pltpu.VMEM)),
        input_output_aliases={i: 2 + i for i in range(2 * n)},
        compiler_params=pltpu.CompilerParams(has_side_effects=_EFFECT),
    )(*[pltpu.with_memory_space_constraint(s, pltpu.HBM) for s in srcs], *lands)
    return dict(send=res[0], recv=res[1], srcs=res[2:2 + n], lands=res[2 + n:2 + 2 * n], token=res[-1],
                slab_source=slab_source)


def _exchange_wait(ex, after, name):
    n = len(ex["srcs"])
    slab_source = ex["slab_source"]

    def body(*refs):
        ins, lands = refs[:n], refs[n:2 * n]
        send, recv = refs[2 * n], refs[2 * n + 1]
        for cp in _direct_copies(ins, lands, send, recv, slab_source):
            cp.wait_send()
            cp.wait_recv()

    res = pl.pallas_call(
        body, name=name,
        out_shape=tuple(pltpu.HBM(s.shape, s.dtype) for s in list(ex["srcs"]) + list(ex["lands"])),
        in_specs=[_HBM] * (2 * n) + [_SEM, _SEM, pl.BlockSpec(memory_space=pl.ANY)],
        out_specs=tuple([_HBM] * (2 * n)),
        input_output_aliases={i: i for i in range(2 * n)},
        compiler_params=pltpu.CompilerParams(has_side_effects=_EFFECT),
    )(*ex["srcs"], *ex["lands"], ex["send"], ex["recv"], after)
    if slab_source:
        return list(zip(res[n:], res[:n]))
    me = _my_block()
    out = []
    for src, land in zip(res[:n], res[n:]):
        if land.ndim == 3:
            own, at = src[None], (me, 0, 0)
        else:
            own, at = src, (me * src.shape[0], 0)
        out.append(lax.dynamic_update_slice(land, own, at))
    return out


BIG = ("w_in", "w_out", "w_gate", "w_up", "w_down")
BIG_SHARD = {"w_in": (1024, 578), "w_out": (256, 1024), "w_gate": (1024, 352), "w_up": (1024, 352),
             "w_down": (352, 1024)}
BIG_SHARD_AXIS = {"w_in": 1, "w_out": 0, "w_gate": 1, "w_up": 1, "w_down": 0}
BIG_ADAM_ROWS = {"w_in": 512, "w_out": 256, "w_gate": 512, "w_up": 512, "w_down": 352}


def _join(parts, axis):
    if axis == 0:
        return parts.reshape((-1,) + parts.shape[2:])
    return jnp.concatenate([parts[j] for j in range(N_DEV)], axis=1)


def _split(full, axis):
    if axis == 0:
        return full.reshape((N_DEV, full.shape[0] // N_DEV) + full.shape[1:])
    c = full.shape[1] // N_DEV
    return jnp.stack([full[:, c * j:c * (j + 1)] for j in range(N_DEV)])


SMALL = (("lru_wa", 65536), ("lru_wx", 65536), ("pre_mix_norm", 1024), ("lru_conv_w", 4096), ("lru_conv_b", 1024),
         ("lru_ba", 1024), ("lru_bx", 1024), ("lru_lambda", 1024), ("lru_out_norm", 1024), ("ssd_conv_w", 6144),
         ("ssd_conv_b", 1536), ("ssd_dt_bias", 16), ("ssd_a_log", 16), ("ssd_d", 16), ("ssd_out_norm", 1024),
         ("post_mix_norm", 1024), ("pre_ffn_norm", 1024), ("post_ffn_norm", 1024), ("loss", 1))
REPLICATED = tuple(n for n, _ in SMALL if n not in ("lru_conv_w", "ssd_conv_w", "loss"))
SMALL_ROW = {}
for _name, _size in SMALL:
    SMALL_ROW[_name] = (sum(-(-s // 1024) for n, s in SMALL[:len(SMALL_ROW)]), -(-_size // 1024))


def _pack_small(d):
    rows = [jnp.pad(d[n].astype(F32).reshape(-1), (0, SMALL_ROW[n][1] * 1024 - s)).reshape(-1, 1024) for n, s in SMALL]
    used = sum(r.shape[0] for r in rows)
    return jnp.concatenate(rows + [jnp.zeros((SMALL_ROWS - used, 1024), F32)], axis=0)


def _small_entry(p, name):
    row, rows = SMALL_ROW[name]
    return p[row:row + rows].reshape(-1)[:dict(SMALL)[name]]


def _blockdiag4(w):
    on_diag = jnp.eye(4, dtype=w.dtype)[None, :, None, :, None]
    return (w.reshape(4, 4, 64, 1, 64) * on_diag).reshape(4, 256, 256)


def _diag_blocks(g):
    on_diag = jnp.eye(4, dtype=g.dtype)[None, :, None, :, None]
    return jnp.sum(g.reshape(4, 4, 64, 4, 64) * on_diag, axis=3).reshape(16, 64, 64)


def _local_step(x, target, w_in, P, rest_weights, emit, emit_small, start_token=None):
    cut = IN_MAIN - (N_DEV - 1) * (IN_COLS // N_DEV)
    w_main = jnp.concatenate([w_in[j] for j in range(N_DEV - 1)] + [w_in[N_DEV - 1][:, :cut]], axis=1)
    w_dt = jnp.pad(w_in[N_DEV - 1][:, cut:], ((0, 0), (0, LANE - SSD_HEADS)))
    pad16 = lambda v: jnp.pad(v.reshape(1, SSD_HEADS), ((0, 0), (0, LANE - SSD_HEADS)))
    dt_bias, a_log = pad16(P["ssd_dt_bias"]), pad16(P["ssd_a_log"])
    d_e = jnp.repeat(P["ssd_d"].reshape(SSD_HEADS), SSD_HEAD_DIM).reshape(1, SSD_INNER)
    expand = (jnp.arange(LANE)[:, None] == (jnp.arange(SSD_INNER)[None, :] // SSD_HEAD_DIM)).astype(BF16)
    wa_bd = _blockdiag4(P["lru_wa"].astype(BF16))
    wx_bd = _blockdiag4(P["lru_wx"].astype(BF16))
    vec = lambda n: P[n].reshape(1, -1)

    after = lambda v, tok: v if tok is None else v + tok[0:1, 0:1]

    h, proj, dtp = _inproj_fwd(x, after(vec("pre_mix_norm"), start_token), w_main, w_dt)
    lx, hl, y_lru, *gates = _lru_fwd(proj, P["lru_conv_w"], vec("lru_conv_b"), wa_bd, wx_bd, vec("lru_ba"),
                                     vec("lru_bx"), vec("lru_lambda"), vec("lru_out_norm"))
    xc, y, y_ssd, sprev = _ssd_fwd(proj, dtp, P["ssd_conv_w"], vec("ssd_conv_b"), dt_bias, a_log, d_e,
                                   vec("ssd_out_norm"), expand)
    W = rest_weights(y_ssd)
    mix, x1, h2 = _outproj_fwd(x, y_lru, y_ssd, W["w_out"], vec("post_mix_norm"), vec("pre_ffn_norm"))
    dx1, act, df, dgt, dup, dg_pf, dg_ff, loss = _ffn_fwd_bwd(
        x1, h2, target, W["w_gate"], W["w_up"], W["w_down"], vec("pre_ffn_norm"), vec("post_ffn_norm"))
    tok = emit("ffn", {"w_gate": _tn_matmul_slabs(h2, [dgt], "dw_gate", D_FF // N_DEV, tk=1024),
                       "w_up": _tn_matmul_slabs(h2, [dup], "dw_up", D_FF // N_DEV, tk=1024),
                       "w_down": _tn_matmul(act, [df], "dw_down", tk=1024)[0]})
    dy_lru, dy_ssd, dmix, dg_pm = _outproj_bwd(dx1, mix, W["w_out"], after(vec("post_mix_norm"), tok))
    tok = emit("out", {"w_out": jnp.concatenate([_tn_matmul(y_lru, [dmix], "dw_out_lru", tk=1024)[0],
                                                 _tn_matmul(y_ssd, [dmix], "dw_out_ssd", tk=1024)[0]], axis=0)})
    dp_ssd, ddtp, dcw_s, dcb_s, dbias, dA, dD_e, dg_ssd = _ssd_bwd(
        dy_ssd, proj, dtp, xc, y, sprev, P["ssd_conv_w"], dt_bias, a_log, d_e,
        after(vec("ssd_out_norm"), tok), expand)
    dp_lru, dpa, dpx, dcw_l, dcb_l, dba, dbx, dlam, dg_lru = _lru_bwd(
        dy_lru, proj, lx, hl, gates, P["lru_conv_w"], wa_bd, wx_bd, vec("lru_lambda"), vec("lru_out_norm"))
    tok = emit("in", {"w_in": _tn_matmul_slabs(h, [dp_lru, dp_ssd, ddtp], "dw_in", IN_COLS // N_DEV)})

    a_neg = -jnp.exp(P["ssd_a_log"].reshape(SSD_HEADS))
    dwa, dwx = _tn_blockdiag(lx, dpa, dpx, "dw_lru_gates")
    small = {
        "pre_mix_norm": jnp.zeros((1, D_MODEL), F32), "lru_conv_w": dcw_l, "lru_conv_b": dcb_l,
        "lru_wa": _diag_blocks(dwa), "lru_ba": dba,
        "lru_wx": _diag_blocks(dwx), "lru_bx": dbx,
        "lru_lambda": dlam, "lru_out_norm": dg_lru, "ssd_conv_w": dcw_s, "ssd_conv_b": dcb_s,
        "ssd_dt_bias": dbias[0, :SSD_HEADS], "ssd_a_log": dA[0, :SSD_HEADS] * a_neg,
        "ssd_d": jnp.sum(dD_e.reshape(SSD_HEADS, SSD_HEAD_DIM), axis=1), "ssd_out_norm": dg_ssd,
        "post_mix_norm": dg_pm, "pre_ffn_norm": dg_pf, "post_ffn_norm": dg_ff, "loss": loss[0, 0:1],
    }
    tok = after(after(vec("pre_mix_norm"), tok), emit_small(small))
    grad_x, dg_pre = _inproj_bwd(dp_lru, dp_ssd, ddtp, dx1, x, tok, w_main, w_dt)
    return grad_x, dg_pre


def kernel(x, pre_mix_norm, w_in, lru_conv_w, lru_conv_b, lru_wa, lru_ba, lru_wx, lru_bx, lru_lambda, lru_out_norm, ssd_conv_w, ssd_conv_b, ssd_dt_bias, ssd_a_log, ssd_d, ssd_out_norm, w_out, post_mix_norm, pre_ffn_norm, w_gate, w_up, w_down, post_ffn_norm, loss_target, m_pre_mix_norm, m_w_in, m_lru_conv_w, m_lru_conv_b, m_lru_wa, m_lru_ba, m_lru_wx, m_lru_bx, m_lru_lambda, m_lru_out_norm, m_ssd_conv_w, m_ssd_conv_b, m_ssd_dt_bias, m_ssd_a_log, m_ssd_d, m_ssd_out_norm, m_w_out, m_post_mix_norm, m_pre_ffn_norm, m_w_gate, m_w_up, m_w_down, m_post_ffn_norm, v_pre_mix_norm, v_w_in, v_lru_conv_w, v_lru_conv_b, v_lru_wa, v_lru_ba, v_lru_wx, v_lru_bx, v_lru_lambda, v_lru_out_norm, v_ssd_conv_w, v_ssd_conv_b, v_ssd_dt_bias, v_ssd_a_log, v_ssd_d, v_ssd_out_norm, v_w_out, v_post_mix_norm, v_pre_ffn_norm, v_w_gate, v_w_up, v_w_down, v_post_ffn_norm):
    a = dict(locals())
    names = [n for n, _ in SMALL if n != "loss"] + list(BIG)
    w = {n: a[n][0] for n in names}
    m = {n: a["m_" + n][0] for n in names}
    v = {n: a["v_" + n][0] for n in names}

    cpack = jnp.concatenate([w["lru_conv_w"], w["ssd_conv_w"], jnp.zeros((4, 64), F32)], axis=1)
    cpack = jnp.pad(cpack, ((0, 4), (0, 0)))
    g_in, cg = _all_gather([w["w_in"].astype(BF16), cpack], "all_gather_w_in")
    P = {n: w[n] for n in REPLICATED}
    P["lru_conv_w"] = _join(cg[:, 0:4, 0:128], 1)
    P["ssd_conv_w"] = _join(cg[:, 0:4, 128:320], 1)

    rest = [n for n in BIG if n != "w_in"]
    zero = jnp.minimum(jnp.abs(cg[0, 0, 0]), 0.0)
    ex_w = _exchange_start([(w[n] + zero).astype(BF16) for n in rest], "weights_start", slab_source=False,
                           axes=[BIG_SHARD_AXIS[n] for n in rest])

    def rest_weights(after):
        lands = _exchange_wait(ex_w, after, "weights_wait")
        return {n: _join(p, 1) if p.ndim == 3 else p for n, p in zip(rest, lands)}

    pending = []

    def emit(group, grads):
        ex = _exchange_start([g if g.ndim == 3 else _split(g, BIG_SHARD_AXIS[n]) for n, g in grads.items()],
                             "grads_start_" + group, slab_source=True)
        pending.append((group, list(grads), ex))
        return ex["token"]

    def emit_small(small):
        ex = _exchange_start([_pack_small(small)], "small_start", slab_source=False, axes=[0])
        pending.append(("small", None, ex))
        return ex["token"]

    grad_x, dg_pre = _local_step(x[0], loss_target[0], g_in, P, rest_weights, emit, emit_small, ex_w["token"])

    ex_small = pending.pop()[2]
    ex_pre = _exchange_start([jnp.pad(dg_pre, ((0, 7), (0, 0)))], "pre_mix_norm_start", slab_source=False, axes=[0])

    outs = {}
    done = ex_pre["token"]
    me1 = _my_block().astype(jnp.int32).reshape(1)
    for group, group_names, ex in pending:
        for n, (lands, slabs) in zip(group_names, _exchange_wait(ex, done, "grads_wait_" + group)):
            res = _adamw_shard(lands, slabs, me1, w[n], m[n], v[n], "adamw_" + n, BIG_ADAM_ROWS[n])
            done = res[0]
            for kind, r in zip(("grad", "delta", "new_m", "new_v"), res):
                outs[kind + "_" + n] = r

    got_pre = _exchange_wait(ex_pre, done, "pre_mix_norm_wait")[0].reshape(N_DEV, 8, 1024)
    got_small = _exchange_wait(ex_small, got_pre, "small_wait")[0].reshape(N_DEV, SMALL_ROWS, 1024)
    row = SMALL_ROW["pre_mix_norm"][0]
    got_small = got_small.at[:, row:row + 1, :].set(got_pre[:, 0:1, :])

    two_d = lambda t: t.reshape(-1, 1024) if t.ndim == 3 else t.reshape(1, -1)
    res = _adamw_small(got_small, [(SMALL_ROW[n][0], two_d(w[n]), two_d(m[n]), two_d(v[n])) for n in REPLICATED])
    g_pack = res[0]
    for i, n in enumerate(REPLICATED):
        for kind, r in zip(("grad", "delta", "new_m", "new_v"), res[1 + 4 * i:5 + 4 * i]):
            outs[kind + "_" + n] = r.reshape(w[n].shape)

    me = _my_block()
    gl = lax.dynamic_slice(_small_entry(g_pack, "lru_conv_w").reshape(4, LRU_W), (0, me * 128), (4, 128))
    gs = lax.dynamic_slice(_small_entry(g_pack, "ssd_conv_w").reshape(4, SSD_CONV_CH), (0, me * 192), (4, 192))
    cat = lambda d: jnp.pad(jnp.concatenate([d["lru_conv_w"], d["ssd_conv_w"]], axis=1), ((0, 4), (0, 64)))
    res = _adamw(cat({"lru_conv_w": gl, "ssd_conv_w": gs})[None], cat(w), cat(m), cat(v), "adamw_conv", 8)
    for kind, r in zip(("grad", "delta", "new_m", "new_v"), res):
        outs[kind + "_lru_conv_w"] = r[0:4, 0:128]
        outs[kind + "_ssd_conv_w"] = r[0:4, 128:320]

    order = ["pre_mix_norm", "w_in", "lru_conv_w", "lru_conv_b", "lru_wa", "lru_ba", "lru_wx", "lru_bx", "lru_lambda",
             "lru_out_norm", "ssd_conv_w", "ssd_conv_b", "ssd_dt_bias", "ssd_a_log", "ssd_d", "ssd_out_norm", "w_out",
             "post_mix_norm", "pre_ffn_norm", "w_gate", "w_up", "w_down", "post_ffn_norm"]
    result = [_small_entry(g_pack, "loss").reshape(()), grad_x[None]]
    for kind in ("grad", "delta", "new_m", "new_v"):
        result += [outs[kind + "_" + n][None] for n in order]
    return tuple(result)
```

```python
import functools

import jax
import jax.numpy as jnp
from jax import lax
from jax.experimental import pallas as pl
from jax.experimental.pallas import tpu as pltpu

F32 = jnp.float32
BF16 = jnp.bfloat16
EPS = 1e-6
N_DEV = 8
D_MODEL = 1024
LRU_W = 1024
SSD_INNER = 1024
SSD_HEADS = 16
SSD_HEAD_DIM = 64
SSD_STATE = 128
SSD_GROUPS = 2
SSD_CONV_CH = 1536
CHUNK = 128
D_FF = 2816
FF_CHUNK = 2816
IN_MAIN = 4608
IN_COLS = 4624
LANE = 128
TT = 256
TP = 512
VMEM_LIMIT = 56 * 1024 * 1024
ADAM_LR, ADAM_B1, ADAM_B2, ADAM_EPS, ADAM_WD, ADAM_STEP = 0.001, 0.9, 0.999, 1e-08, 0.01, 10
SMALL_ROWS = 160

NT = (((1,), (1,)), ((), ()))
TN = (((0,), (0,)), ((), ()))


def _params(n_grid):
    return pltpu.CompilerParams(dimension_semantics=("arbitrary",) * n_grid, vmem_limit_bytes=VMEM_LIMIT)


def _dot(a, b, dims=None, precision=None):
    if dims is None:
        return jnp.dot(a, b, preferred_element_type=F32, precision=precision)
    return lax.dot_general(a, b, dims, preferred_element_type=F32, precision=precision)


def _split_bf16(x, terms):
    out = []
    for _ in range(terms - 1):
        p = x.astype(BF16)
        out.append(p)
        x = x - p.astype(F32)
    return out + [x.astype(BF16)]


def _dot_sel(x, sel, dims=None, terms=2, sel_first=False):
    parts = [_dot(sel, p, dims) if sel_first else _dot(p, sel, dims) for p in _split_bf16(x, terms)]
    return functools.reduce(lambda a, b: a + b, parts)


def _sigmoid(x):
    return 0.5 * jnp.tanh(0.5 * x) + 0.5


def _softplus(x):
    e = jnp.exp(-jnp.abs(x))
    l1p = jnp.where(e < 1e-3, e * (1.0 - e * (0.5 - e * (1.0 / 3.0))), jnp.log(1.0 + e))
    return jnp.maximum(x, 0.0) + l1p


def _neg_expm1(x):
    series = -x * (1.0 + x * (0.5 + x * (1.0 / 6.0 + x * (1.0 / 24.0))))
    return jnp.where(x > -0.01, series, 1.0 - jnp.exp(x))


_GELU_C = 0.7978845608028654


def _gelu(x):
    t = jnp.tanh(_GELU_C * (x + 0.044715 * x * x * x))
    return 0.5 * x * (1.0 + t), t


def _gelu_grad(x, t):
    return 0.5 * (1.0 + t) + 0.5 * x * (1.0 - t * t) * _GELU_C * (1.0 + 3.0 * 0.044715 * x * x)


def _rms(x):
    return lax.rsqrt(jnp.mean(x * x, axis=-1, keepdims=True) + EPS)


def _rms_bwd(dyn, x, rn):
    return rn * dyn - x * (rn * rn * rn) * jnp.mean(dyn * x, axis=-1, keepdims=True)


def _row(x, r):
    idx = lax.broadcasted_iota(jnp.int32, x.shape, 0)
    return jnp.sum(jnp.where(idx == r, x, 0.0), axis=0, keepdims=True)


def _shift_down(cur, prev8, j):
    s = pltpu.roll(cur, j, 0)
    p = pltpu.roll(prev8, j, 0)
    r8 = lax.broadcasted_iota(jnp.int32, prev8.shape, 0)
    top = jnp.where(r8 < j, p, s[0:8])
    return jnp.concatenate([top, s[8:]], axis=0)


def _shift_up(cur, next8, j):
    n = cur.shape[0]
    s = pltpu.roll(cur, n - j, 0)
    p = pltpu.roll(next8, 8 - j, 0)
    r8 = lax.broadcasted_iota(jnp.int32, next8.shape, 0)
    bot = jnp.where(r8 >= 8 - j, p, s[n - 8:n])
    return jnp.concatenate([s[:n - 8], bot], axis=0)


def _scan_fwd(a, u):
    n = a.shape[0]
    row = lax.broadcasted_iota(jnp.int32, a.shape, 0)
    k = 1
    while k < n:
        ok = row >= k
        a_s = jnp.where(ok, pltpu.roll(a, k, 0), 1.0)
        u_s = jnp.where(ok, pltpu.roll(u, k, 0), 0.0)
        u = a * u_s + u
        a = a * a_s
        k *= 2
    return a, u


def _scan_bwd(b, d):
    n = b.shape[0]
    row = lax.broadcasted_iota(jnp.int32, b.shape, 0)
    k = 1
    while k < n:
        ok = row < n - k
        b_s = jnp.where(ok, pltpu.roll(b, n - k, 0), 1.0)
        d_s = jnp.where(ok, pltpu.roll(d, n - k, 0), 0.0)
        d = b * d_s + d
        b = b * b_s
        k *= 2
    return b, d


def _scan_tile(a, u, carry, a_s, u_s, reverse):
    n, c = a.shape
    groups = n // 8
    r8 = lax.broadcasted_iota(jnp.int32, a.shape, 0) & 7
    in_group = lambda x, k: pltpu.roll(x.reshape(groups, 8, c), k, 1).reshape(n, c)
    for k in (1, 2, 4):
        ok = (r8 < 8 - k) if reverse else (r8 >= k)
        shift = 8 - k if reverse else k
        a_n = jnp.where(ok, in_group(a, shift), 1.0)
        u_n = jnp.where(ok, in_group(u, shift), 0.0)
        u = a * u_n + u
        a = a * a_n
    nl = c // LANE
    for j in range(nl):
        a_s[j] = a[:, LANE * j:LANE * (j + 1)]
        u_s[j] = u[:, LANE * j:LANE * (j + 1)]
    end = 0 if reverse else 7
    ends = lambda ref, j: ref[pl.ds(j, 1), pl.ds(end, groups, stride=8), :].reshape(groups, LANE)
    ga = jnp.concatenate([ends(a_s, j) for j in range(nl)], axis=1)
    gu = jnp.concatenate([ends(u_s, j) for j in range(nl)], axis=1)
    gacc, gh = (_scan_bwd if reverse else _scan_fwd)(ga, gu)
    gh = gh + gacc * carry
    grow = lax.broadcasted_iota(jnp.int32, gh.shape, 0)
    if reverse:
        cin = jnp.where(grow == groups - 1, carry, pltpu.roll(gh, groups - 1, 0))
    else:
        cin = jnp.where(grow == 0, carry, pltpu.roll(gh, 1, 0))
    spread = ((lax.broadcasted_iota(jnp.int32, (n, LANE), 0) >> 3)
              == lax.broadcasted_iota(jnp.int32, (n, LANE), 1)).astype(BF16)
    cin = jnp.concatenate([cin, jnp.zeros((LANE - groups, c), F32)], axis=0)
    return u + a * _dot_sel(cin, spread, terms=3, sel_first=True)


def _inproj_fwd(x, g_pre, w_main, w_dt):
    T = x.shape[0]
    TT = TP

    def body(x_ref, g_ref, wm_hbm, wd_hbm, h_ref, proj_ref, dtp_ref, wm, wd, sem):
        @pl.when(pl.program_id(0) == 0)
        def _():
            c1 = pltpu.make_async_copy(wm_hbm, wm, sem.at[0])
            c2 = pltpu.make_async_copy(wd_hbm, wd, sem.at[1])
            c1.start()
            c2.start()
            c1.wait()
            c2.wait()

        xv = x_ref[...]
        h = (xv * _rms(xv) * g_ref[...]).astype(BF16)
        h_ref[...] = h
        proj_ref[...] = _dot(h, wm[...])
        dtp_ref[...] = _dot(h, wd[...])

    return pl.pallas_call(
        body, name="inproj_fwd", grid=(T // TT,),
        in_specs=[pl.BlockSpec((TT, D_MODEL), lambda i: (i, 0)),
                  pl.BlockSpec((1, D_MODEL), lambda i: (0, 0)),
                  pl.BlockSpec(memory_space=pl.ANY), pl.BlockSpec(memory_space=pl.ANY)],
        out_specs=[pl.BlockSpec((TT, D_MODEL), lambda i: (i, 0)),
                   pl.BlockSpec((TT, IN_MAIN), lambda i: (i, 0)),
                   pl.BlockSpec((TT, LANE), lambda i: (i, 0))],
        out_shape=[jax.ShapeDtypeStruct((T, D_MODEL), BF16), jax.ShapeDtypeStruct((T, IN_MAIN), F32),
                   jax.ShapeDtypeStruct((T, LANE), F32)],
        scratch_shapes=[pltpu.VMEM((D_MODEL, IN_MAIN), BF16), pltpu.VMEM((D_MODEL, LANE), BF16),
                        pltpu.SemaphoreType.DMA((2,))],
        compiler_params=_params(1),
    )(x, g_pre, w_main, w_dt)


def _inproj_bwd(dp_lru, dp_ssd, ddtp, dx1, x, g_pre, w_main, w_dt):
    T = x.shape[0]
    TT = TP

    def body(dl_ref, ds_ref, dd_ref, dx1_ref, x_ref, g_ref, wm_hbm, wd_hbm, gx_ref, dg_ref, wm, wd, sem):
        @pl.when(pl.program_id(0) == 0)
        def _():
            c1 = pltpu.make_async_copy(wm_hbm, wm, sem.at[0])
            c2 = pltpu.make_async_copy(wd_hbm, wd, sem.at[1])
            c1.start()
            c2.start()
            c1.wait()
            c2.wait()
            dg_ref[...] = jnp.zeros_like(dg_ref)

        dh = _dot(dl_ref[...], wm[:, 0:2048], NT)
        dh += _dot(ds_ref[...], wm[:, 2048:IN_MAIN], NT)
        dh += _dot(dd_ref[...], wd[...], NT)
        xv = x_ref[...]
        rn = _rms(xv)
        dg_ref[...] += jnp.sum(dh * xv * rn, axis=0, keepdims=True)
        gx_ref[...] = dx1_ref[...] + _rms_bwd(dh * g_ref[...], xv, rn)

    return pl.pallas_call(
        body, name="inproj_bwd", grid=(T // TT,),
        in_specs=[pl.BlockSpec((TT, 2048), lambda i: (i, 0)),
                  pl.BlockSpec((TT, 2560), lambda i: (i, 0)),
                  pl.BlockSpec((TT, LANE), lambda i: (i, 0)),
                  pl.BlockSpec((TT, D_MODEL), lambda i: (i, 0)),
                  pl.BlockSpec((TT, D_MODEL), lambda i: (i, 0)),
                  pl.BlockSpec((1, D_MODEL), lambda i: (0, 0)),
                  pl.BlockSpec(memory_space=pl.ANY), pl.BlockSpec(memory_space=pl.ANY)],
        out_specs=[pl.BlockSpec((TT, D_MODEL), lambda i: (i, 0)),
                   pl.BlockSpec((1, D_MODEL), lambda i: (0, 0))],
        out_shape=[jax.ShapeDtypeStruct((T, D_MODEL), F32), jax.ShapeDtypeStruct((1, D_MODEL), F32)],
        scratch_shapes=[pltpu.VMEM((D_MODEL, IN_MAIN), BF16), pltpu.VMEM((D_MODEL, LANE), BF16),
                        pltpu.SemaphoreType.DMA((2,))],
        compiler_params=_params(1),
    )(dp_lru, dp_ssd, ddtp, dx1, x, g_pre, w_main, w_dt)


def _lru_gates(lx, wa_ref, wx_ref, ba, bx, lam):
    lxb = lx.astype(BF16)
    pa = jnp.concatenate([_dot(lxb[:, 256 * k:256 * (k + 1)], wa_ref[k]) for k in range(4)], axis=1) + ba
    px = jnp.concatenate([_dot(lxb[:, 256 * k:256 * (k + 1)], wx_ref[k]) for k in range(4)], axis=1) + bx
    r = _sigmoid(pa)
    ig = _sigmoid(px)
    sp = _softplus(-lam)
    log_a = -8.0 * r * sp
    a = jnp.exp(log_a)
    mult = jnp.sqrt(_neg_expm1(2.0 * log_a))
    return r, ig, sp, a, mult


def _lru_fwd(proj, conv_w, conv_b, wa_bd, wx_bd, ba, bx, lam, g_lru):
    T = proj.shape[0]

    def body(cx_ref, gate_ref, cw_ref, cb_ref, wa_ref, wx_ref, ba_ref, bx_ref, lam_ref, g_ref,
             lx_ref, hl_ref, y_ref, r_ref, ig_ref, a_ref, mult_ref, tail, hcar, sa, su):
        @pl.when(pl.program_id(0) == 0)
        def _():
            tail[...] = jnp.zeros_like(tail)
            hcar[...] = jnp.zeros_like(hcar)

        cx = cx_ref[...]
        prev8 = tail[...]
        lx = cb_ref[...] + cw_ref[3:4, :] * cx
        for j in range(1, 4):
            lx += cw_ref[3 - j:4 - j, :] * _shift_down(cx, prev8, j)
        tail[...] = cx[TT - 8:TT]
        lx_ref[...] = lx
        r, ig, sp, a, mult = _lru_gates(lx, wa_ref, wx_ref, ba_ref[...], bx_ref[...], lam_ref[...])
        r_ref[...] = r
        ig_ref[...] = ig
        a_ref[...] = a
        mult_ref[...] = mult
        h = _scan_tile(a, mult * (ig * lx), hcar[...], sa, su, reverse=False)
        hl_ref[...] = h
        hcar[...] = hl_ref[TT - 1:TT, :]
        ge, _ = _gelu(gate_ref[...])
        p = h * ge
        y_ref[...] = (p * _rms(p) * g_ref[...]).astype(BF16)

    vec = pl.BlockSpec((1, LRU_W), lambda i: (0, 0))
    bd = pl.BlockSpec((4, 256, 256), lambda i: (0, 0, 0))
    tile = pl.BlockSpec((TT, LRU_W), lambda i: (i, 0))
    f32 = jax.ShapeDtypeStruct((T, LRU_W), F32)
    return pl.pallas_call(
        body, name="lru_fwd", grid=(T // TT,),
        in_specs=[tile, pl.BlockSpec((TT, LRU_W), lambda i: (i, 1)),
                  pl.BlockSpec((4, LRU_W), lambda i: (0, 0)), vec, bd, bd, vec, vec, vec, vec],
        out_specs=[tile] * 7,
        out_shape=[f32, f32, jax.ShapeDtypeStruct((T, LRU_W), BF16), f32, f32, f32, f32],
        scratch_shapes=[pltpu.VMEM((8, LRU_W), F32), pltpu.VMEM((1, LRU_W), F32)]
                       + [pltpu.VMEM((LRU_W // LANE, TT, LANE), F32)] * 2,
        compiler_params=_params(1),
    )(proj, proj, conv_w, conv_b, wa_bd, wx_bd, ba, bx, lam, g_lru)


def _lru_bwd(dy, proj, lx, hl, gates, conv_w, wa_bd, wx_bd, lam, g_lru):
    T = proj.shape[0]
    nt = T // TT

    def body(dy_ref, cx_ref, gate_ref, lx_ref, hl_ref, halo_ref, r_ref, ig_ref, a_ref, mult_ref, cw_ref, wa_ref,
             wx_ref, lam_ref, g_ref, dp_ref, dpa_ref, dpx_ref, dcw_ref, dcb_ref, dba_ref, dbx_ref, dlam_ref, dg_ref,
             gcar, acar, head, sa, su):
        i = pl.program_id(0)

        @pl.when(i == 0)
        def _():
            gcar[...] = jnp.zeros_like(gcar)
            acar[...] = jnp.zeros_like(acar)
            head[...] = jnp.zeros_like(head)
            for ref in (dcw_ref, dcb_ref, dba_ref, dbx_ref, dlam_ref, dg_ref):
                ref[...] = jnp.zeros_like(ref)

        lx = lx_ref[...]
        h = hl_ref[...]
        gate = gate_ref[...]
        cx = cx_ref[...]
        lam = lam_ref[...]
        r, ig, a, mult = r_ref[...], ig_ref[...], a_ref[...], mult_ref[...]
        sp = _softplus(-lam)
        ge, th = _gelu(gate)
        p = h * ge
        rn = _rms(p)
        dyv = dy_ref[...]
        dg_ref[...] += jnp.sum(dyv * p * rn, axis=0, keepdims=True)
        dp = _rms_bwd(dyv * g_ref[...], p, rn)
        dp_ref[:, LRU_W:2 * LRU_W] = (dp * h * _gelu_grad(gate, th)).astype(BF16)
        dh = dp * ge
        row = lax.broadcasted_iota(jnp.int32, a.shape, 0)
        b = jnp.where(row == TT - 1, acar[...], pltpu.roll(a, TT - 1, 0))
        g = _scan_tile(b, dh, gcar[...], sa, su, reverse=True)
        gcar[...] = _row(g[0:8], 0)
        acar[...] = _row(a[0:8], 0)
        h_last_prev = halo_ref[7:8, :] * (i < nt - 1).astype(F32)
        hprev = jnp.where(row == 0, h_last_prev, pltpu.roll(h, 1, 0))
        da = g * hprev
        dm2 = (g * (ig * lx)) * 0.5 / mult
        dlog_a = da * a - 2.0 * a * a * dm2
        dlam_ref[...] += jnp.sum(-8.0 * r * dlog_a, axis=0, keepdims=True) * (-_sigmoid(-lam))
        dpa = (-8.0 * sp * dlog_a) * r * (1.0 - r)
        dpx = (g * mult * lx) * ig * (1.0 - ig)
        dba_ref[...] += jnp.sum(dpa, axis=0, keepdims=True)
        dbx_ref[...] += jnp.sum(dpx, axis=0, keepdims=True)
        dpab = dpa.astype(BF16)
        dpxb = dpx.astype(BF16)
        dpa_ref[...] = dpab
        dpx_ref[...] = dpxb
        dlx = g * mult * ig + jnp.concatenate(
            [_dot(dpab[:, 256 * k:256 * (k + 1)], wa_ref[k], NT) + _dot(dpxb[:, 256 * k:256 * (k + 1)], wx_ref[k], NT)
             for k in range(4)], axis=1)
        nxt = head[...]
        dcb_ref[...] += jnp.sum(dlx, axis=0, keepdims=True)
        dcx = cw_ref[3:4, :] * dlx
        dcw_ref[3:4, :] += jnp.sum(cx * dlx, axis=0, keepdims=True)
        for j in range(1, 4):
            sh = _shift_up(dlx, nxt, j)
            dcx += cw_ref[3 - j:4 - j, :] * sh
            dcw_ref[3 - j:4 - j, :] += jnp.sum(cx * sh, axis=0, keepdims=True)
        head[...] = dlx[0:8]
        dp_ref[:, 0:LRU_W] = dcx.astype(BF16)

    rev = lambda i: (nt - 1 - i, 0)
    vec = pl.BlockSpec((1, LRU_W), lambda i: (0, 0))
    bd = pl.BlockSpec((4, 256, 256), lambda i: (0, 0, 0))
    tile = pl.BlockSpec((TT, LRU_W), rev)
    halo = pl.BlockSpec((8, LRU_W), lambda i: (jnp.maximum((nt - 1 - i) * (TT // 8) - 1, 0), 0))
    cw = pl.BlockSpec((4, LRU_W), lambda i: (0, 0))
    return pl.pallas_call(
        body, name="lru_bwd", grid=(nt,),
        in_specs=[tile, tile, pl.BlockSpec((TT, LRU_W), lambda i: (nt - 1 - i, 1)), tile, tile, halo,
                  tile, tile, tile, tile, cw, bd, bd, vec, vec],
        out_specs=[pl.BlockSpec((TT, 2 * LRU_W), rev), tile, tile, cw, vec, vec, vec, vec, vec],
        out_shape=[jax.ShapeDtypeStruct((T, 2 * LRU_W), BF16), jax.ShapeDtypeStruct((T, LRU_W), BF16),
                   jax.ShapeDtypeStruct((T, LRU_W), BF16), jax.ShapeDtypeStruct((4, LRU_W), F32)]
                  + [jax.ShapeDtypeStruct((1, LRU_W), F32)] * 5,
        scratch_shapes=[pltpu.VMEM((1, LRU_W), F32), pltpu.VMEM((1, LRU_W), F32), pltpu.VMEM((8, LRU_W), F32)]
                       + [pltpu.VMEM((LRU_W // LANE, TT, LANE), F32)] * 2,
        compiler_params=_params(1),
    )(dy, proj, proj, lx, hl, hl, *gates, conv_w, wa_bd, wx_bd, lam, g_lru)


def _ssd_chunk_terms(xc, dtp, bias, alog, expand):
    sg = _sigmoid(xc)
    xbc = xc * sg
    pre = dtp + bias
    dt = _softplus(pre)
    A = -jnp.exp(alog)
    ri = lax.broadcasted_iota(jnp.int32, (CHUNK, CHUNK), 0)
    ci = lax.broadcasted_iota(jnp.int32, (CHUNK, CHUNK), 1)
    tril = (ri >= ci).astype(BF16)
    cs = _dot_sel(dt * A, tril, terms=3, sel_first=True)
    cs_last = _row(cs, CHUNK - 1)
    ecs = jnp.exp(cs)
    dec = jnp.exp(cs_last - cs)
    return dict(sg=sg, xbc=xbc, pre=pre, dt=dt, A=A, cs=cs, csT=cs.T, ecs=ecs, dec=dec, ri=ri, ci=ci,
                dt_e=_dot_sel(dt, expand), ecs_e=_dot_sel(ecs, expand), dec_e=_dot_sel(dec, expand))


def _head_lambda(t, h, cst_ref):
    col = jnp.sum(jnp.where(t["ci"] == h, t["cs"], 0.0), axis=1, keepdims=True)
    return jnp.exp(jnp.where(t["ri"] >= t["ci"], col - cst_ref[h:h + 1, :], -1e30))


def _ssd_fwd(proj, dtp, conv_w, conv_b, dt_bias, a_log, d_e, g_ssd, expand):
    T = proj.shape[0]
    nc = T // CHUNK

    def body(z_ref, xp_ref, dtp_ref, cw_ref, cb_ref, bias_ref, alog_ref, de_ref, g_ref, ex_ref,
             xc_ref, y_ref, yn_ref, sprev_ref, tail, S, cst):
        @pl.when(pl.program_id(0) == 0)
        def _():
            tail[...] = jnp.zeros_like(tail)
            S[...] = jnp.zeros_like(S)

        xp = xp_ref[...]
        prev8 = tail[...]
        xc = cb_ref[...] + cw_ref[3:4, :] * xp
        for j in range(1, 4):
            xc += cw_ref[3 - j:4 - j, :] * _shift_down(xp, prev8, j)
        tail[...] = xp[CHUNK - 8:CHUNK]
        xc_ref[...] = xc
        t = _ssd_chunk_terms(xc, dtp_ref[...], bias_ref[...], alog_ref[...], ex_ref[...])
        cst[...] = t["csT"]
        xbc = t["xbc"]
        sx = xbc[:, 0:SSD_INNER]
        Bb = xbc[:, SSD_INNER:SSD_INNER + 256].astype(BF16)
        Cb = xbc[:, SSD_INNER + 256:SSD_CONV_CH].astype(BF16)
        X = t["dt_e"] * sx
        lane = lax.broadcasted_iota(jnp.int32, (CHUNK, LANE), 1)
        G = [_dot(Cb[:, 128 * g:128 * (g + 1)], Bb[:, 128 * g:128 * (g + 1)], NT) for g in range(SSD_GROUPS)]
        for k in range(SSD_HEADS // 2):
            Xp = X[:, 128 * k:128 * (k + 1)]
            acc = jnp.zeros((CHUNK, LANE), F32)
            for half in range(2):
                M = (G[k // 4] * _head_lambda(t, 2 * k + half, cst)).astype(BF16)
                Xh = jnp.where((lane >= 64) if half else (lane < 64), Xp, 0.0).astype(BF16)
                acc += _dot(M, Xh)
            y_ref[:, 128 * k:128 * (k + 1)] = acc
        sprev_ref[0] = S[...]
        eL_e = _row(t["ecs_e"], CHUNK - 1)
        Xd = (X * t["dec_e"]).astype(BF16)
        for g in range(SSD_GROUPS):
            sl = slice(512 * g, 512 * (g + 1))
            Sg = S[:, sl]
            y_ref[:, sl] += t["ecs_e"][:, sl] * _dot(Cb[:, 128 * g:128 * (g + 1)], Sg.astype(BF16))
            S[:, sl] = eL_e[:, sl] * Sg + _dot(Bb[:, 128 * g:128 * (g + 1)], Xd[:, sl], TN)
        y = y_ref[...] + de_ref[...] * sx
        y_ref[...] = y
        z = z_ref[...]
        q = y * (z * _sigmoid(z))
        yn_ref[...] = (q * _rms(q) * g_ref[...]).astype(BF16)

    c0 = lambda i: (0, 0)
    return pl.pallas_call(
        body, name="ssd_fwd", grid=(nc,),
        in_specs=[pl.BlockSpec((CHUNK, SSD_INNER), lambda i: (i, 2)),
                  pl.BlockSpec((CHUNK, SSD_CONV_CH), lambda i: (i, 2)),
                  pl.BlockSpec((CHUNK, LANE), lambda i: (i, 0)),
                  pl.BlockSpec((4, SSD_CONV_CH), c0), pl.BlockSpec((1, SSD_CONV_CH), c0),
                  pl.BlockSpec((1, LANE), c0), pl.BlockSpec((1, LANE), c0),
                  pl.BlockSpec((1, SSD_INNER), c0), pl.BlockSpec((1, SSD_INNER), c0),
                  pl.BlockSpec((LANE, SSD_INNER), c0)],
        out_specs=[pl.BlockSpec((CHUNK, SSD_CONV_CH), lambda i: (i, 0)),
                   pl.BlockSpec((CHUNK, SSD_INNER), lambda i: (i, 0)),
                   pl.BlockSpec((CHUNK, SSD_INNER), lambda i: (i, 0)),
                   pl.BlockSpec((1, SSD_STATE, SSD_INNER), lambda i: (i, 0, 0))],
        out_shape=[jax.ShapeDtypeStruct((T, SSD_CONV_CH), F32), jax.ShapeDtypeStruct((T, SSD_INNER), F32),
                   jax.ShapeDtypeStruct((T, SSD_INNER), BF16),
                   jax.ShapeDtypeStruct((nc, SSD_STATE, SSD_INNER), F32)],
        scratch_shapes=[pltpu.VMEM((8, SSD_CONV_CH), F32), pltpu.VMEM((SSD_STATE, SSD_INNER), F32),
                        pltpu.VMEM((CHUNK, CHUNK), F32)],
        compiler_params=_params(1),
    )(proj, proj, dtp, conv_w, conv_b, dt_bias, a_log, d_e, g_ssd, expand)


def _ssd_bwd(dyn, proj, dtp, xc, y, sprev, conv_w, dt_bias, a_log, d_e, g_ssd, expand):
    T = proj.shape[0]
    nc = T // CHUNK

    def body(dyn_ref, z_ref, xp_ref, dtp_ref, xc_ref, y_ref, sprev_ref, cw_ref, bias_ref, alog_ref, de_ref,
             g_ref, ex_ref, dp_ref, ddtp_ref, dcw_ref, dcb_ref, dbias_ref, dA_ref, dD_ref, dg_ref,
             dS, head, dX_s, dxbc_s, cst, dcst):
        @pl.when(pl.program_id(0) == 0)
        def _():
            dS[...] = jnp.zeros_like(dS)
            head[...] = jnp.zeros_like(head)
            dcst[...] = jnp.zeros_like(dcst)
            for ref in (dcw_ref, dcb_ref, dbias_ref, dA_ref, dD_ref, dg_ref):
                ref[...] = jnp.zeros_like(ref)

        ex = ex_ref[...]
        xc = xc_ref[...]
        t = _ssd_chunk_terms(xc, dtp_ref[...], bias_ref[...], alog_ref[...], ex)
        cst[...] = t["csT"]
        ri, ci = t["ri"], t["ci"]
        xbc = t["xbc"]
        sx = xbc[:, 0:SSD_INNER]
        Bb = xbc[:, SSD_INNER:SSD_INNER + 256].astype(BF16)
        Cb = xbc[:, SSD_INNER + 256:SSD_CONV_CH].astype(BF16)
        X = t["dt_e"] * sx
        z = z_ref[...]
        sz = _sigmoid(z)
        siluz = z * sz
        yv = y_ref[...]
        q = yv * siluz
        rn = _rms(q)
        dynv = dyn_ref[...]
        dg_ref[...] += jnp.sum(dynv * q * rn, axis=0, keepdims=True)
        dq = _rms_bwd(dynv * g_ref[...], q, rn)
        dp_ref[:, 0:SSD_INNER] = (dq * yv * (sz * (1.0 + z * (1.0 - sz)))).astype(BF16)
        dY = dq * siluz
        dD_ref[...] += jnp.sum(dY * sx, axis=0, keepdims=True)
        dYb = dY.astype(BF16)
        lane = lax.broadcasted_iota(jnp.int32, (CHUNK, LANE), 1)
        dcs = jnp.zeros((CHUNK, CHUNK), F32)
        Xb = X.astype(BF16)
        for g in range(SSD_GROUPS):
            Bg = Bb[:, 128 * g:128 * (g + 1)]
            Cg = Cb[:, 128 * g:128 * (g + 1)]
            G = _dot(Cg, Bg, NT)
            dGsum = jnp.zeros((CHUNK, CHUNK), F32)
            for k in range(4 * g, 4 * g + 4):
                Xp = Xb[:, 128 * k:128 * (k + 1)]
                dYp = dY[:, 128 * k:128 * (k + 1)]
                dXp = jnp.zeros((CHUNK, LANE), F32)
                for half in range(2):
                    h = 2 * k + half
                    lam = _head_lambda(t, h, cst)
                    M = G * lam
                    dYh = jnp.where((lane >= 64) if half else (lane < 64), dYp, 0.0).astype(BF16)
                    dM = _dot(dYh, Xp, NT)
                    W = dM * M
                    dcs += jnp.where(ci == h, jnp.sum(W, axis=1, keepdims=True), 0.0)
                    dcst[h:h + 1, :] = jnp.sum(W, axis=0, keepdims=True)
                    dGsum += dM * lam
                    dXp += _dot(M.astype(BF16), dYh, TN)
                dX_s[:, 128 * k:128 * (k + 1)] = dXp
            dGb = dGsum.astype(BF16)
            dxbc_s[:, SSD_INNER + 256 + 128 * g:SSD_INNER + 256 + 128 * (g + 1)] = _dot(dGb, Bg)
            dxbc_s[:, SSD_INNER + 128 * g:SSD_INNER + 128 * (g + 1)] = _dot(dGb, Cg, TN)
        dcs = dcs - dcst[...].T
        Sp = sprev_ref[0]
        dSv = dS[...]
        ecs_e, dec_e = t["ecs_e"], t["dec_e"]
        eL_e = _row(ecs_e, CHUNK - 1)
        dYe = dY * ecs_e
        dYeb = dYe.astype(BF16)
        Xd = X * dec_e
        Xdb = Xd.astype(BF16)
        for g in range(SSD_GROUPS):
            sl = slice(512 * g, 512 * (g + 1))
            Bg = Bb[:, 128 * g:128 * (g + 1)]
            Cg = Cb[:, 128 * g:128 * (g + 1)]
            Spb = Sp[:, sl].astype(BF16)
            dSb = dSv[:, sl].astype(BF16)
            CS = _dot(Cg, Spb)
            BS = _dot(Bg, dSb)
            dxbc_s[:, SSD_INNER + 256 + 128 * g:SSD_INNER + 256 + 128 * (g + 1)] += _dot(dYeb[:, sl], Spb, NT)
            dxbc_s[:, SSD_INNER + 128 * g:SSD_INNER + 128 * (g + 1)] += _dot(Xdb[:, sl], dSb, NT)
            dS[:, sl] = eL_e[:, sl] * dSv[:, sl] + _dot(Cg, dYeb[:, sl], TN)
            dX_s[:, sl] += dec_e[:, sl] * BS
            dcs += _dot_sel(dYe[:, sl] * CS, ex[:, sl], NT, terms=1)
            tdec = _dot_sel(X[:, sl] * BS, ex[:, sl], NT, terms=1) * t["dec"]
            dcs -= tdec
            last = jnp.sum(tdec, axis=0, keepdims=True)
            last += jnp.sum(_dot_sel(Sp[:, sl] * dSv[:, sl], ex[:, sl], NT, terms=1), axis=0, keepdims=True) \
                * _row(t["ecs"], CHUNK - 1)
            dcs += jnp.where(ri == CHUNK - 1, last, 0.0)
        triu = (ci >= ri).astype(BF16)
        da = _dot_sel(dcs, triu, terms=3, sel_first=True)
        dX = dX_s[...]
        ddt = da * t["A"] + _dot_sel(dX * sx, ex, NT, terms=1)
        dA_ref[...] += jnp.sum(da * t["dt"], axis=0, keepdims=True)
        ddtp = ddt * _sigmoid(t["pre"])
        dbias_ref[...] += jnp.sum(ddtp, axis=0, keepdims=True)
        ddtp_ref[...] = ddtp.astype(BF16)
        dxbc_s[:, 0:SSD_INNER] = dX * t["dt_e"] + de_ref[...] * dY
        sg = t["sg"]
        dxc = dxbc_s[...] * (sg * (1.0 + xc * (1.0 - sg)))
        xp = xp_ref[...]
        nxt = head[...]
        dcb_ref[...] += jnp.sum(dxc, axis=0, keepdims=True)
        dpre = cw_ref[3:4, :] * dxc
        dcw_ref[3:4, :] += jnp.sum(xp * dxc, axis=0, keepdims=True)
        for j in range(1, 4):
            sh = _shift_up(dxc, nxt, j)
            dpre += cw_ref[3 - j:4 - j, :] * sh
            dcw_ref[3 - j:4 - j, :] += jnp.sum(xp * sh, axis=0, keepdims=True)
        head[...] = dxc[0:8]
        dp_ref[:, SSD_INNER:SSD_INNER + SSD_CONV_CH] = dpre.astype(BF16)

    c0 = lambda i: (0, 0)
    rev = lambda i: (nc - 1 - i, 0)
    return pl.pallas_call(
        body, name="ssd_bwd", grid=(nc,),
        in_specs=[pl.BlockSpec((CHUNK, SSD_INNER), rev),
                  pl.BlockSpec((CHUNK, SSD_INNER), lambda i: (nc - 1 - i, 2)),
                  pl.BlockSpec((CHUNK, SSD_CONV_CH), lambda i: (nc - 1 - i, 2)),
                  pl.BlockSpec((CHUNK, LANE), rev),
                  pl.BlockSpec((CHUNK, SSD_CONV_CH), rev),
                  pl.BlockSpec((CHUNK, SSD_INNER), rev),
                  pl.BlockSpec((1, SSD_STATE, SSD_INNER), lambda i: (nc - 1 - i, 0, 0)),
                  pl.BlockSpec((4, SSD_CONV_CH), c0), pl.BlockSpec((1, LANE), c0), pl.BlockSpec((1, LANE), c0),
                  pl.BlockSpec((1, SSD_INNER), c0), pl.BlockSpec((1, SSD_INNER), c0),
                  pl.BlockSpec((LANE, SSD_INNER), c0)],
        out_specs=[pl.BlockSpec((CHUNK, 2560), rev), pl.BlockSpec((CHUNK, LANE), rev),
                   pl.BlockSpec((4, SSD_CONV_CH), c0), pl.BlockSpec((1, SSD_CONV_CH), c0),
                   pl.BlockSpec((1, LANE), c0), pl.BlockSpec((1, LANE), c0),
                   pl.BlockSpec((1, SSD_INNER), c0), pl.BlockSpec((1, SSD_INNER), c0)],
        out_shape=[jax.ShapeDtypeStruct((T, 2560), BF16), jax.ShapeDtypeStruct((T, LANE), BF16),
                   jax.ShapeDtypeStruct((4, SSD_CONV_CH), F32), jax.ShapeDtypeStruct((1, SSD_CONV_CH), F32),
                   jax.ShapeDtypeStruct((1, LANE), F32), jax.ShapeDtypeStruct((1, LANE), F32),
                   jax.ShapeDtypeStruct((1, SSD_INNER), F32), jax.ShapeDtypeStruct((1, SSD_INNER), F32)],
        scratch_shapes=[pltpu.VMEM((SSD_STATE, SSD_INNER), F32), pltpu.VMEM((8, SSD_CONV_CH), F32),
                        pltpu.VMEM((CHUNK, SSD_INNER), F32), pltpu.VMEM((CHUNK, SSD_CONV_CH), F32),
                        pltpu.VMEM((CHUNK, CHUNK), F32), pltpu.VMEM((CHUNK, CHUNK), F32)],
        compiler_params=_params(1),
    )(dyn, proj, proj, dtp, xc, y, sprev, conv_w, dt_bias, a_log, d_e, g_ssd, expand)


def _outproj_fwd(x, y_lru, y_ssd, w_out, g_pm, g_pf):
    T = x.shape[0]
    TT = TP

    def body(x_ref, yl_ref, ys_ref, wo_ref, gpm_ref, gpf_ref, mix_ref, x1_ref, h2_ref):
        mix = _dot(yl_ref[...], wo_ref[0:LRU_W, :]) + _dot(ys_ref[...], wo_ref[LRU_W:2 * LRU_W, :])
        mix_ref[...] = mix
        x1 = x_ref[...] + mix * _rms(mix) * gpm_ref[...]
        x1_ref[...] = x1
        h2_ref[...] = (x1 * _rms(x1) * gpf_ref[...]).astype(BF16)

    tile = pl.BlockSpec((TT, D_MODEL), lambda i: (i, 0))
    vec = pl.BlockSpec((1, D_MODEL), lambda i: (0, 0))
    return pl.pallas_call(
        body, name="outproj_fwd", grid=(T // TT,),
        in_specs=[tile, tile, tile, pl.BlockSpec((2 * LRU_W, D_MODEL), lambda i: (0, 0)), vec, vec],
        out_specs=[tile, tile, tile],
        out_shape=[jax.ShapeDtypeStruct((T, D_MODEL), F32), jax.ShapeDtypeStruct((T, D_MODEL), F32),
                   jax.ShapeDtypeStruct((T, D_MODEL), BF16)],
        compiler_params=_params(1),
    )(x, y_lru, y_ssd, w_out, g_pm, g_pf)


def _ffn_fwd_bwd(x1, h2, target, w_gate, w_up, w_down, g_pf, g_ff):
    T = x1.shape[0]

    def body(x1_ref, h2_ref, tg_ref, wg_hbm, wu_hbm, wd_hbm, gpf_ref, gff_ref,
             dx1_ref, act_ref, df_ref, dgt_ref, dup_ref, dgpf_ref, dgff_ref, loss_ref,
             wg, wu, wd, gt_s, up_s, sem):
        @pl.when(pl.program_id(0) == 0)
        def _():
            cps = [pltpu.make_async_copy(s, d, sem.at[n]) for n, (s, d) in
                   enumerate(((wg_hbm, wg), (wu_hbm, wu), (wd_hbm, wd)))]
            for c in cps:
                c.start()
            for c in cps:
                c.wait()
            for ref in (dgpf_ref, dgff_ref, loss_ref):
                ref[...] = jnp.zeros_like(ref)

        h2 = h2_ref[...]
        f = jnp.zeros((TT, D_MODEL), F32)
        for c in range(D_FF // FF_CHUNK):
            sl = slice(FF_CHUNK * c, FF_CHUNK * (c + 1))
            gt = _dot(h2, wg[:, sl])
            up = _dot(h2, wu[:, sl])
            gt_s[:, sl] = gt
            up_s[:, sl] = up
            act = (gt * _sigmoid(gt) * up).astype(BF16)
            act_ref[:, sl] = act
            f += _dot(act, wd[sl, :])
        x1 = x1_ref[...]
        rnf = _rms(f)
        e = x1 + f * rnf * gff_ref[...] - tg_ref[...]
        part = 0.5 * jnp.sum(jnp.sum(e * e, axis=1, keepdims=True), axis=0, keepdims=True) * (1.0 / D_MODEL)
        lane = lax.broadcasted_iota(jnp.int32, (1, LANE), 1)
        loss_ref[...] += jnp.where(lane == 0, part, 0.0)
        dx2 = e * (1.0 / D_MODEL)
        dgff_ref[...] += jnp.sum(dx2 * f * rnf, axis=0, keepdims=True)
        df = _rms_bwd(dx2 * gff_ref[...], f, rnf).astype(BF16)
        df_ref[...] = df
        dh2 = jnp.zeros((TT, D_MODEL), F32)
        for c in range(D_FF // FF_CHUNK):
            sl = slice(FF_CHUNK * c, FF_CHUNK * (c + 1))
            dact = _dot(df, wd[sl, :], NT)
            gt = gt_s[:, sl]
            up = up_s[:, sl]
            sg = _sigmoid(gt)
            dgt = (dact * up * (sg * (1.0 + gt * (1.0 - sg)))).astype(BF16)
            dup = (dact * gt * sg).astype(BF16)
            dgt_ref[:, sl] = dgt
            dup_ref[:, sl] = dup
            dh2 += _dot(dgt, wg[:, sl], NT) + _dot(dup, wu[:, sl], NT)
        rn2 = _rms(x1)
        dgpf_ref[...] += jnp.sum(dh2 * x1 * rn2, axis=0, keepdims=True)
        dx1_ref[...] = dx2 + _rms_bwd(dh2 * gpf_ref[...], x1, rn2)

    tile = pl.BlockSpec((TT, D_MODEL), lambda i: (i, 0))
    wide = pl.BlockSpec((TT, D_FF), lambda i: (i, 0))
    vec = pl.BlockSpec((1, D_MODEL), lambda i: (0, 0))
    hbm = pl.BlockSpec(memory_space=pl.ANY)
    return pl.pallas_call(
        body, name="ffn_fwd_bwd", grid=(T // TT,),
        in_specs=[tile, tile, tile, hbm, hbm, hbm, vec, vec],
        out_specs=[tile, wide, tile, wide, wide, vec, vec, pl.BlockSpec((1, LANE), lambda i: (0, 0))],
        out_shape=[jax.ShapeDtypeStruct((T, D_MODEL), F32), jax.ShapeDtypeStruct((T, D_FF), BF16),
                   jax.ShapeDtypeStruct((T, D_MODEL), BF16), jax.ShapeDtypeStruct((T, D_FF), BF16),
                   jax.ShapeDtypeStruct((T, D_FF), BF16), jax.ShapeDtypeStruct((1, D_MODEL), F32),
                   jax.ShapeDtypeStruct((1, D_MODEL), F32), jax.ShapeDtypeStruct((1, LANE), F32)],
        scratch_shapes=[pltpu.VMEM((D_MODEL, D_FF), BF16), pltpu.VMEM((D_MODEL, D_FF), BF16),
                        pltpu.VMEM((D_FF, D_MODEL), BF16), pltpu.VMEM((TT, D_FF), F32),
                        pltpu.VMEM((TT, D_FF), F32), pltpu.SemaphoreType.DMA((3,))],
        compiler_params=_params(1),
    )(x1, h2, target, w_gate, w_up, w_down, g_pf, g_ff)


def _outproj_bwd(dx1, mix, w_out, g_pm):
    T = dx1.shape[0]
    TT = TP

    def body(dx1_ref, mix_ref, wo_ref, gpm_ref, dyl_ref, dys_ref, dmix_ref, dg_ref):
        @pl.when(pl.program_id(0) == 0)
        def _():
            dg_ref[...] = jnp.zeros_like(dg_ref)

        mix = mix_ref[...]
        rn = _rms(mix)
        dx1v = dx1_ref[...]
        dg_ref[...] += jnp.sum(dx1v * mix * rn, axis=0, keepdims=True)
        dmix = _rms_bwd(dx1v * gpm_ref[...], mix, rn).astype(BF16)
        dmix_ref[...] = dmix
        dyl_ref[...] = _dot(dmix, wo_ref[0:LRU_W, :], NT)
        dys_ref[...] = _dot(dmix, wo_ref[LRU_W:2 * LRU_W, :], NT)

    tile = pl.BlockSpec((TT, D_MODEL), lambda i: (i, 0))
    vec = pl.BlockSpec((1, D_MODEL), lambda i: (0, 0))
    return pl.pallas_call(
        body, name="outproj_bwd", grid=(T // TT,),
        in_specs=[tile, tile, pl.BlockSpec((2 * LRU_W, D_MODEL), lambda i: (0, 0)), vec],
        out_specs=[tile, tile, tile, vec],
        out_shape=[jax.ShapeDtypeStruct((T, D_MODEL), F32), jax.ShapeDtypeStruct((T, D_MODEL), F32),
                   jax.ShapeDtypeStruct((T, D_MODEL), BF16), jax.ShapeDtypeStruct((1, D_MODEL), F32)],
        compiler_params=_params(1),
    )(dx1, mix, w_out, g_pm)


def _tn_matmul(a, bs, name, tk=512):
    T, M = a.shape
    tk = min(tk, T)
    nk = T // tk
    nb = len(bs)

    def body(*refs):
        a_ref, b_refs, o_refs, accs = refs[0], refs[1:1 + nb], refs[1 + nb:1 + 2 * nb], refs[1 + 2 * nb:]
        k = pl.program_id(0)

        @pl.when(k == 0)
        def _():
            for acc in accs:
                acc[...] = jnp.zeros_like(acc)

        av = a_ref[...].astype(BF16)
        for b_ref, acc in zip(b_refs, accs):
            acc[...] += _dot(av, b_ref[...], TN)

        @pl.when(k == nk - 1)
        def _():
            for o_ref, acc in zip(o_refs, accs):
                o_ref[...] = acc[...].astype(BF16)

    return pl.pallas_call(
        body, name=name, grid=(nk,),
        in_specs=[pl.BlockSpec((tk, M), lambda k: (k, 0))]
                 + [pl.BlockSpec((tk, b.shape[1]), lambda k: (k, 0)) for b in bs],
        out_specs=[pl.BlockSpec((M, b.shape[1]), lambda k: (0, 0)) for b in bs],
        out_shape=[jax.ShapeDtypeStruct((M, b.shape[1]), BF16) for b in bs],
        scratch_shapes=[pltpu.VMEM((M, b.shape[1]), F32) for b in bs],
        compiler_params=_params(1),
    )(a, *bs)


def _tn_matmul_slabs(a, bs, name, slab, tk=512):
    T, M = a.shape
    tk = min(tk, T)
    nk = T // tk
    nb = len(bs)
    offs = [sum(b.shape[1] for b in bs[:i]) for i in range(nb + 1)]

    def body(*refs):
        a_ref, b_refs, o_ref, acc = refs[0], refs[1:1 + nb], refs[1 + nb], refs[2 + nb]
        k = pl.program_id(0)

        @pl.when(k == 0)
        def _():
            acc[...] = jnp.zeros_like(acc)

        av = a_ref[...].astype(BF16)
        for i, b_ref in enumerate(b_refs):
            acc[:, offs[i]:offs[i + 1]] += _dot(av, b_ref[...], TN)

        @pl.when(k == nk - 1)
        def _():
            for j in range(N_DEV):
                o_ref[j] = acc[:, slab * j:slab * (j + 1)].astype(BF16)

    return pl.pallas_call(
        body, name=name, grid=(nk,),
        in_specs=[pl.BlockSpec((tk, M), lambda k: (k, 0))]
                 + [pl.BlockSpec((tk, b.shape[1]), lambda k: (k, 0)) for b in bs],
        out_specs=pl.BlockSpec((N_DEV, M, slab), lambda k: (0, 0, 0)),
        out_shape=jax.ShapeDtypeStruct((N_DEV, M, slab), BF16),
        scratch_shapes=[pltpu.VMEM((M, offs[-1]), F32)],
        compiler_params=_params(1),
    )(a, *bs)


def _tn_blockdiag(a, b1, b2, name, tk=1024):
    T = a.shape[0]
    tk = min(tk, T)

    def body(a_ref, b1_ref, b2_ref, o1_ref, o2_ref):
        @pl.when(pl.program_id(0) == 0)
        def _():
            o1_ref[...] = jnp.zeros_like(o1_ref)
            o2_ref[...] = jnp.zeros_like(o2_ref)

        for j in range(4):
            sl = slice(256 * j, 256 * (j + 1))
            av = a_ref[:, sl].astype(BF16)
            o1_ref[j] += _dot(av, b1_ref[:, sl], TN)
            o2_ref[j] += _dot(av, b2_ref[:, sl], TN)

    blk = pl.BlockSpec((tk, LRU_W), lambda k: (k, 0))
    out = pl.BlockSpec((4, 256, 256), lambda k: (0, 0, 0))
    return pl.pallas_call(
        body, name=name, grid=(T // tk,),
        in_specs=[blk, blk, blk], out_specs=[out, out],
        out_shape=[jax.ShapeDtypeStruct((4, 256, 256), F32)] * 2,
        compiler_params=_params(1),
    )(a, b1, b2)


def _adamw_update(g, w, m, v):
    nm = ADAM_B1 * m + (1.0 - ADAM_B1) * g
    nv = ADAM_B2 * v + (1.0 - ADAM_B2) * (g * g)
    m_hat = nm / (1.0 - ADAM_B1 ** ADAM_STEP)
    v_hat = nv / (1.0 - ADAM_B2 ** ADAM_STEP)
    return -ADAM_LR * (m_hat / (jnp.sqrt(v_hat) + ADAM_EPS) + ADAM_WD * w), nm, nv


def _adamw_small(parts, params):
    n = len(params)

    def body(*refs):
        p_ref, wmv = refs[0], refs[1:1 + 3 * n]
        gs_ref, outs = refs[1 + 3 * n], refs[2 + 3 * n:]
        g = p_ref[0]
        for k in range(1, N_DEV):
            g = g + p_ref[k]
        gs_ref[...] = g
        for i, (row, w, _, _) in enumerate(params):
            rows, width = w.shape
            if width <= 1024:
                gp = gs_ref[row:row + rows, 0:width]
            else:
                gp = jnp.concatenate([gs_ref[row:row + 1, :], gs_ref[row + 1:row + 2, 0:width - 1024]], axis=1)
            d, nm, nv = _adamw_update(gp, wmv[3 * i][...], wmv[3 * i + 1][...], wmv[3 * i + 2][...])
            for ref, val in zip(outs[4 * i:4 * i + 4], (gp, d, nm, nv)):
                ref[...] = val

    full = lambda s: pl.BlockSpec(s, lambda: (0,) * len(s))
    shapes = [w.shape for _, w, _, _ in params]
    return pl.pallas_call(
        body, name="adamw_small",
        in_specs=[full(parts.shape)] + [full(s) for s in shapes for _ in range(3)],
        out_specs=[full(parts.shape[1:])] + [full(s) for s in shapes for _ in range(4)],
        out_shape=[jax.ShapeDtypeStruct(parts.shape[1:], F32)]
                  + [jax.ShapeDtypeStruct(s, F32) for s in shapes for _ in range(4)],
        compiler_params=pltpu.CompilerParams(vmem_limit_bytes=VMEM_LIMIT),
    )(parts, *[a for _, w, m, v in params for a in (w, m, v)])


def _adamw_shard(lands, slabs, me, w, m, v, name, tr):
    P, R, C = lands.shape

    def body(me_ref, p_ref, own_ref, w_ref, m_ref, v_ref, g_ref, d_ref, nm_ref, nv_ref):
        g = None
        for k in range(P):
            part = jnp.where(me_ref[0] == k, own_ref[0], p_ref[k]).astype(F32)
            g = part if g is None else g + part
        g_ref[...] = g
        d_ref[...], nm_ref[...], nv_ref[...] = _adamw_update(g, w_ref[...], m_ref[...], v_ref[...])

    blk = pl.BlockSpec((tr, C), lambda i, me_ref: (i, 0))
    return pl.pallas_call(
        body, name=name,
        grid_spec=pltpu.PrefetchScalarGridSpec(
            num_scalar_prefetch=1, grid=(R // tr,),
            in_specs=[pl.BlockSpec((P, tr, C), lambda i, me_ref: (0, i, 0)),
                      pl.BlockSpec((1, tr, C), lambda i, me_ref: (me_ref[0], i, 0)), blk, blk, blk],
            out_specs=[blk, blk, blk, blk]),
        out_shape=[jax.ShapeDtypeStruct((R, C), F32)] * 4,
        compiler_params=_params(1),
    )(me, lands, slabs, w, m, v)


def _adamw(parts, w, m, v, name, tr):
    P, R, C = parts.shape

    def body(p_ref, w_ref, m_ref, v_ref, g_ref, d_ref, nm_ref, nv_ref):
        g = p_ref[0].astype(F32)
        for k in range(1, P):
            g = g + p_ref[k].astype(F32)
        g_ref[...] = g
        d_ref[...], nm_ref[...], nv_ref[...] = _adamw_update(g, w_ref[...], m_ref[...], v_ref[...])

    blk = pl.BlockSpec((tr, C), lambda i: (i, 0))
    return pl.pallas_call(
        body, name=name, grid=(R // tr,),
        in_specs=[pl.BlockSpec((P, tr, C), lambda i: (0, i, 0)), blk, blk, blk],
        out_specs=[blk, blk, blk, blk],
        out_shape=[jax.ShapeDtypeStruct((R, C), F32)] * 4,
        compiler_params=_params(1),
    )(parts, w, m, v)


def _peer(k):
    x, y, c = lax.axis_index("x"), lax.axis_index("y"), lax.axis_index("c")
    px = x ^ ((k >> 2) & 1)
    py = y ^ ((k >> 1) & 1)
    pc = c ^ (k & 1)
    return (px, py, pc), 4 * px + 2 * py + pc


def _my_block():
    return 4 * lax.axis_index("x") + 2 * lax.axis_index("y") + lax.axis_index("c")


def _all_gather(shards, name):
    n = len(shards)

    def body(*refs):
        ins, outs = refs[:n], refs[n:2 * n]
        send, recv, loc = refs[2 * n:]
        x, y, c = lax.axis_index("x"), lax.axis_index("y"), lax.axis_index("c")
        sibling = (x, y, 1 - c)
        chips = [(1 - x, y), (x, 1 - y), (1 - x, 1 - y)]
        slot = lambda px, py, pc: 4 * px + 2 * py + pc

        def copy(a, k, block, to, src=None):
            dst = outs[a].at[slot(*block)]
            return pltpu.make_async_remote_copy(
                src_ref=dst if src is None else src, dst_ref=dst, send_sem=send.at[a, k], recv_sem=recv.at[a, k],
                device_id=to, device_id_type=pl.DeviceIdType.MESH)

        mine = [pltpu.make_async_copy(ins[a], outs[a].at[slot(x, y, c)], loc.at[a]) for a in range(n)]
        for cp in mine:
            cp.start()
        first = []
        for a in range(n):
            first.append(copy(a, 0, (x, y, c), sibling, src=ins[a]))
            first += [copy(a, 1 + j, (x, y, c), (*chip, c), src=ins[a]) for j, chip in enumerate(chips)]
        for cp in first:
            cp.start()
        passed = []
        for j, chip in enumerate(chips):
            for a in range(n):
                copy(a, 1 + j, (*chip, c), (x, y, c)).wait_recv()
                fwd = copy(a, 4 + j, (*chip, c), sibling)
                fwd.start()
                passed.append(fwd)
        for a in range(n):
            copy(a, 0, sibling, (x, y, c)).wait_recv()
            for j, chip in enumerate(chips):
                copy(a, 4 + j, (*chip, 1 - c), (x, y, c)).wait_recv()
        for cp in first + passed:
            cp.wait_send()
        for cp in mine:
            cp.wait()

    hbm = pl.BlockSpec(memory_space=pl.ANY)
    return pl.pallas_call(
        body, name=name,
        in_specs=[hbm] * n, out_specs=[hbm] * n,
        out_shape=[jax.ShapeDtypeStruct((N_DEV,) + s.shape, s.dtype) for s in shards],
        scratch_shapes=[pltpu.SemaphoreType.DMA((n, N_DEV - 1)), pltpu.SemaphoreType.DMA((n, N_DEV - 1)),
                        pltpu.SemaphoreType.DMA((n,))],
    )(*shards)


_HBM = pl.BlockSpec(memory_space=pltpu.HBM)
_SEM = pl.BlockSpec(memory_space=pltpu.SEMAPHORE)
_EFFECT = pltpu.SideEffectType.DATAFLOW_SIDE_EFFECTING


def _direct_copies(srcs, lands, send, recv, slab_source):
    me = _my_block()
    cps = []
    for k in range(1, N_DEV):
        to, blk = _peer(k)
        for a, (src, land) in enumerate(zip(srcs, lands)):
            if slab_source:
                s, d = src.at[blk], land.at[me]
            elif land.ndim == 3:
                s, d = src, land.at[me]
            else:
                s, d = src, land.at[pl.ds(pl.multiple_of(me * src.shape[0], 16), src.shape[0]), :]
            cps.append(pltpu.make_async_remote_copy(
                src_ref=s, dst_ref=d,
                send_sem=send.at[a * (N_DEV - 1) + k - 1], recv_sem=recv.at[a * (N_DEV - 1) + k - 1],
                device_id=to, device_id_type=pl.DeviceIdType.MESH))
    return cps


def _exchange_start(srcs, name, slab_source, axes=None):
    n = len(srcs)
    if slab_source:
        shapes = [s.shape for s in srcs]
    else:
        shapes = [(N_DEV,) + s.shape if ax == 1 else (N_DEV * s.shape[0], s.shape[1]) for s, ax in zip(srcs, axes)]
    lands = [pltpu.with_memory_space_constraint(lax.empty(sh, s.dtype), pltpu.HBM) for sh, s in zip(shapes, srcs)]

    def body(*refs):
        ins, land_in = refs[:n], refs[n:2 * n]
        send, recv = refs[2 * n], refs[2 * n + 1]
        token = refs[4 * n + 2]
        for cp in _direct_copies(ins, land_in, send, recv, slab_source):
            cp.start()
        token[...] = jnp.zeros_like(token)

    sems = pltpu.SemaphoreType.DMA((n * (N_DEV - 1),))
    res = pl.pallas_call(
        body, name=name,
        out_shape=(sems, sems, *[pltpu.HBM(s.shape, s.dtype) for s in srcs],
                   *[pltpu.HBM(l.shape, l.dtype) for l in lands], jax.ShapeDtypeStruct((8, LANE), F32)),
        in_specs=[_HBM] * (2 * n),
        out_specs=(_SEM, _SEM, *[_HBM] * (2 * n), pl.BlockSpec(memory_space=pltpu.VMEM)),
        input_output_aliases={i: 2 + i for i in range(2 * n)},
        compiler_params=pltpu.CompilerParams(has_side_effects=_EFFECT),
    )(*[pltpu.with_memory_space_constraint(s, pltpu.HBM) for s in srcs], *lands)
    return dict(send=res[0], recv=res[1], srcs=res[2:2 + n], lands=res[2 + n:2 + 2 * n], token=res[-1],
                slab_source=slab_source)


def _exchange_wait(ex, after, name):
    n = len(ex["srcs"])
    slab_source = ex["slab_source"]

    def body(*refs):
        ins, lands = refs[:n], refs[n:2 * n]
        send, recv = refs[2 * n], refs[2 * n + 1]
        for cp in _direct_copies(ins, lands, send, recv, slab_source):
            cp.wait_send()
            cp.wait_recv()

    res = pl.pallas_call(
        body, name=name,
        out_shape=tuple(pltpu.HBM(s.shape, s.dtype) for s in list(ex["srcs"]) + list(ex["lands"])),
        in_specs=[_HBM] * (2 * n) + [_SEM, _SEM, pl.BlockSpec(memory_space=pl.ANY)],
        out_specs=tuple([_HBM] * (2 * n)),
        input_output_aliases={i: i for i in range(2 * n)},
        compiler_params=pltpu.CompilerParams(has_side_effects=_EFFECT),
    )(*ex["srcs"], *ex["lands"], ex["send"], ex["recv"], after)
    if slab_source:
        return list(zip(res[n:], res[:n]))
    me = _my_block()
    out = []
    for src, land in zip(res[:n], res[n:]):
        if land.ndim == 3:
            own, at = src[None], (me, 0, 0)
        else:
            own, at = src, (me * src.shape[0], 0)
        out.append(lax.dynamic_update_slice(land, own, at))
    return out


BIG = ("w_in", "w_out", "w_gate", "w_up", "w_down")
BIG_SHARD = {"w_in": (1024, 578), "w_out": (256, 1024), "w_gate": (1024, 352), "w_up": (1024, 352),
             "w_down": (352, 1024)}
BIG_SHARD_AXIS = {"w_in": 1, "w_out": 0, "w_gate": 1, "w_up": 1, "w_down": 0}
BIG_ADAM_ROWS = {"w_in": 256, "w_out": 128, "w_gate": 256, "w_up": 256, "w_down": 176}


def _join(parts, axis):
    if axis == 0:
        return parts.reshape((-1,) + parts.shape[2:])
    return jnp.concatenate([parts[j] for j in range(N_DEV)], axis=1)


def _split(full, axis):
    if axis == 0:
        return full.reshape((N_DEV, full.shape[0] // N_DEV) + full.shape[1:])
    c = full.shape[1] // N_DEV
    return jnp.stack([full[:, c * j:c * (j + 1)] for j in range(N_DEV)])


SMALL = (("lru_wa", 65536), ("lru_wx", 65536), ("pre_mix_norm", 1024), ("lru_conv_w", 4096), ("lru_conv_b", 1024),
         ("lru_ba", 1024), ("lru_bx", 1024), ("lru_lambda", 1024), ("lru_out_norm", 1024), ("ssd_conv_w", 6144),
         ("ssd_conv_b", 1536), ("ssd_dt_bias", 16), ("ssd_a_log", 16), ("ssd_d", 16), ("ssd_out_norm", 1024),
         ("post_mix_norm", 1024), ("pre_ffn_norm", 1024), ("post_ffn_norm", 1024), ("loss", 1))
REPLICATED = tuple(n for n, _ in SMALL if n not in ("lru_conv_w", "ssd_conv_w", "loss"))
SMALL_ROW = {}
for _name, _size in SMALL:
    SMALL_ROW[_name] = (sum(-(-s // 1024) for n, s in SMALL[:len(SMALL_ROW)]), -(-_size // 1024))


def _pack_small(d):
    rows = [jnp.pad(d[n].astype(F32).reshape(-1), (0, SMALL_ROW[n][1] * 1024 - s)).reshape(-1, 1024) for n, s in SMALL]
    used = sum(r.shape[0] for r in rows)
    return jnp.concatenate(rows + [jnp.zeros((SMALL_ROWS - used, 1024), F32)], axis=0)


def _small_entry(p, name):
    row, rows = SMALL_ROW[name]
    return p[row:row + rows].reshape(-1)[:dict(SMALL)[name]]


def _blockdiag4(w):
    on_diag = jnp.eye(4, dtype=w.dtype)[None, :, None, :, None]
    return (w.reshape(4, 4, 64, 1, 64) * on_diag).reshape(4, 256, 256)


def _diag_blocks(g):
    on_diag = jnp.eye(4, dtype=g.dtype)[None, :, None, :, None]
    return jnp.sum(g.reshape(4, 4, 64, 4, 64) * on_diag, axis=3).reshape(16, 64, 64)


def _local_step(x, target, w_in, P, rest_weights, emit, emit_small, start_token=None):
    cut = IN_MAIN - (N_DEV - 1) * (IN_COLS // N_DEV)
    w_main = jnp.concatenate([w_in[j] for j in range(N_DEV - 1)] + [w_in[N_DEV - 1][:, :cut]], axis=1)
    w_dt = jnp.pad(w_in[N_DEV - 1][:, cut:], ((0, 0), (0, LANE - SSD_HEADS)))
    pad16 = lambda v: jnp.pad(v.reshape(1, SSD_HEADS), ((0, 0), (0, LANE - SSD_HEADS)))
    dt_bias, a_log = pad16(P["ssd_dt_bias"]), pad16(P["ssd_a_log"])
    d_e = jnp.repeat(P["ssd_d"].reshape(SSD_HEADS), SSD_HEAD_DIM).reshape(1, SSD_INNER)
    expand = (jnp.arange(LANE)[:, None] == (jnp.arange(SSD_INNER)[None, :] // SSD_HEAD_DIM)).astype(BF16)
    wa_bd = _blockdiag4(P["lru_wa"].astype(BF16))
    wx_bd = _blockdiag4(P["lru_wx"].astype(BF16))
    vec = lambda n: P[n].reshape(1, -1)

    after = lambda v, tok: v if tok is None else v + tok[0:1, 0:1]

    h, proj, dtp = _inproj_fwd(x, after(vec("pre_mix_norm"), start_token), w_main, w_dt)
    lx, hl, y_lru, *gates = _lru_fwd(proj, P["lru_conv_w"], vec("lru_conv_b"), wa_bd, wx_bd, vec("lru_ba"),
                                     vec("lru_bx"), vec("lru_lambda"), vec("lru_out_norm"))
    xc, y, y_ssd, sprev = _ssd_fwd(proj, dtp, P["ssd_conv_w"], vec("ssd_conv_b"), dt_bias, a_log, d_e,
                                   vec("ssd_out_norm"), expand)
    W = rest_weights(y_ssd)
    mix, x1, h2 = _outproj_fwd(x, y_lru, y_ssd, W["w_out"], vec("post_mix_norm"), vec("pre_ffn_norm"))
    dx1, act, df, dgt, dup, dg_pf, dg_ff, loss = _ffn_fwd_bwd(
        x1, h2, target, W["w_gate"], W["w_up"], W["w_down"], vec("pre_ffn_norm"), vec("post_ffn_norm"))
    tok = emit("ffn", {"w_gate": _tn_matmul_slabs(h2, [dgt], "dw_gate", D_FF // N_DEV, tk=1024),
                       "w_up": _tn_matmul_slabs(h2, [dup], "dw_up", D_FF // N_DEV, tk=1024),
                       "w_down": _tn_matmul(act, [df], "dw_down", tk=1024)[0]})
    dy_lru, dy_ssd, dmix, dg_pm = _outproj_bwd(dx1, mix, W["w_out"], after(vec("post_mix_norm"), tok))
    tok = emit("out", {"w_out": jnp.concatenate([_tn_matmul(y_lru, [dmix], "dw_out_lru", tk=1024)[0],
                                                 _tn_matmul(y_ssd, [dmix], "dw_out_ssd", tk=1024)[0]], axis=0)})
    dp_ssd, ddtp, dcw_s, dcb_s, dbias, dA, dD_e, dg_ssd = _ssd_bwd(
        dy_ssd, proj, dtp, xc, y, sprev, P["ssd_conv_w"], dt_bias, a_log, d_e,
        after(vec("ssd_out_norm"), tok), expand)
    dp_lru, dpa, dpx, dcw_l, dcb_l, dba, dbx, dlam, dg_lru = _lru_bwd(
        dy_lru, proj, lx, hl, gates, P["lru_conv_w"], wa_bd, wx_bd, vec("lru_lambda"), vec("lru_out_norm"))
    tok = emit("in", {"w_in": _tn_matmul_slabs(h, [dp_lru, dp_ssd, ddtp], "dw_in", IN_COLS // N_DEV)})

    a_neg = -jnp.exp(P["ssd_a_log"].reshape(SSD_HEADS))
    dwa, dwx = _tn_blockdiag(lx, dpa, dpx, "dw_lru_gates")
    small = {
        "pre_mix_norm": jnp.zeros((1, D_MODEL), F32), "lru_conv_w": dcw_l, "lru_conv_b": dcb_l,
        "lru_wa": _diag_blocks(dwa), "lru_ba": dba,
        "lru_wx": _diag_blocks(dwx), "lru_bx": dbx,
        "lru_lambda": dlam, "lru_out_norm": dg_lru, "ssd_conv_w": dcw_s, "ssd_conv_b": dcb_s,
        "ssd_dt_bias": dbias[0, :SSD_HEADS], "ssd_a_log": dA[0, :SSD_HEADS] * a_neg,
        "ssd_d": jnp.sum(dD_e.reshape(SSD_HEADS, SSD_HEAD_DIM), axis=1), "ssd_out_norm": dg_ssd,
        "post_mix_norm": dg_pm, "pre_ffn_norm": dg_pf, "post_ffn_norm": dg_ff, "loss": loss[0, 0:1],
    }
    tok = after(after(vec("pre_mix_norm"), tok), emit_small(small))
    grad_x, dg_pre = _inproj_bwd(dp_lru, dp_ssd, ddtp, dx1, x, tok, w_main, w_dt)
    return grad_x, dg_pre


def kernel(x, pre_mix_norm, w_in, lru_conv_w, lru_conv_b, lru_wa, lru_ba, lru_wx, lru_bx, lru_lambda, lru_out_norm, ssd_conv_w, ssd_conv_b, ssd_dt_bias, ssd_a_log, ssd_d, ssd_out_norm, w_out, post_mix_norm, pre_ffn_norm, w_gate, w_up, w_down, post_ffn_norm, loss_target, m_pre_mix_norm, m_w_in, m_lru_conv_w, m_lru_conv_b, m_lru_wa, m_lru_ba, m_lru_wx, m_lru_bx, m_lru_lambda, m_lru_out_norm, m_ssd_conv_w, m_ssd_conv_b, m_ssd_dt_bias, m_ssd_a_log, m_ssd_d, m_ssd_out_norm, m_w_out, m_post_mix_norm, m_pre_ffn_norm, m_w_gate, m_w_up, m_w_down, m_post_ffn_norm, v_pre_mix_norm, v_w_in, v_lru_conv_w, v_lru_conv_b, v_lru_wa, v_lru_ba, v_lru_wx, v_lru_bx, v_lru_lambda, v_lru_out_norm, v_ssd_conv_w, v_ssd_conv_b, v_ssd_dt_bias, v_ssd_a_log, v_ssd_d, v_ssd_out_norm, v_w_out, v_post_mix_norm, v_pre_ffn_norm, v_w_gate, v_w_up, v_w_down, v_post_ffn_norm):
    a = dict(locals())
    names = [n for n, _ in SMALL if n != "loss"] + list(BIG)
    w = {n: a[n][0] for n in names}
    m = {n: a["m_" + n][0] for n in names}
    v = {n: a["v_" + n][0] for n in names}

    cpack = jnp.concatenate([w["lru_conv_w"], w["ssd_conv_w"], jnp.zeros((4, 64), F32)], axis=1)
    cpack = jnp.pad(cpack, ((0, 4), (0, 0)))

    rest = [n for n in BIG if n != "w_in"]
    ex_w = _exchange_start([w[n].astype(BF16) for n in rest], "weights_start", slab_source=False,
                           axes=[BIG_SHARD_AXIS[n] for n in rest])
    g_in, cg = _all_gather([w["w_in"].astype(BF16), cpack + ex_w["token"][0:1, 0:1]], "all_gather_w_in")
    P = {n: w[n] for n in REPLICATED}
    P["lru_conv_w"] = _join(cg[:, 0:4, 0:128], 1)
    P["ssd_conv_w"] = _join(cg[:, 0:4, 128:320], 1)

    def rest_weights(after):
        lands = _exchange_wait(ex_w, after, "weights_wait")
        return {n: _join(p, 1) if p.ndim == 3 else p for n, p in zip(rest, lands)}

    pending = []

    def emit(group, grads):
        ex = _exchange_start([g if g.ndim == 3 else _split(g, BIG_SHARD_AXIS[n]) for n, g in grads.items()],
                             "grads_start_" + group, slab_source=True)
        pending.append((group, list(grads), ex))
        return ex["token"]

    def emit_small(small):
        ex = _exchange_start([_pack_small(small)], "small_start", slab_source=False, axes=[0])
        pending.append(("small", None, ex))
        return ex["token"]

    grad_x, dg_pre = _local_step(x[0], loss_target[0], g_in, P, rest_weights, emit, emit_small, ex_w["token"])

    ex_small = pending.pop()[2]
    ex_pre = _exchange_start([jnp.pad(dg_pre, ((0, 7), (0, 0)))], "pre_mix_norm_start", slab_source=False, axes=[0])

    outs = {}
    done = ex_pre["token"]
    me1 = _my_block().astype(jnp.int32).reshape(1)
    for group, group_names, ex in pending:
        for n, (lands, slabs) in zip(group_names, _exchange_wait(ex, done, "grads_wait_" + group)):
            res = _adamw_shard(lands, slabs, me1, w[n], m[n], v[n], "adamw_" + n, BIG_ADAM_ROWS[n])
            done = res[0]
            for kind, r in zip(("grad", "delta", "new_m", "new_v"), res):
                outs[kind + "_" + n] = r

    got_pre = _exchange_wait(ex_pre, done, "pre_mix_norm_wait")[0].reshape(N_DEV, 8, 1024)
    got_small = _exchange_wait(ex_small, got_pre, "small_wait")[0].reshape(N_DEV, SMALL_ROWS, 1024)
    row = SMALL_ROW["pre_mix_norm"][0]
    got_small = got_small.at[:, row:row + 1, :].set(got_pre[:, 0:1, :])

    two_d = lambda t: t.reshape(-1, 1024) if t.ndim == 3 else t.reshape(1, -1)
    res = _adamw_small(got_small, [(SMALL_ROW[n][0], two_d(w[n]), two_d(m[n]), two_d(v[n])) for n in REPLICATED])
    g_pack = res[0]
    for i, n in enumerate(REPLICATED):
        for kind, r in zip(("grad", "delta", "new_m", "new_v"), res[1 + 4 * i:5 + 4 * i]):
            outs[kind + "_" + n] = r.reshape(w[n].shape)

    me = _my_block()
    gl = lax.dynamic_slice(_small_entry(g_pack, "lru_conv_w").reshape(4, LRU_W), (0, me * 128), (4, 128))
    gs = lax.dynamic_slice(_small_entry(g_pack, "ssd_conv_w").reshape(4, SSD_CONV_CH), (0, me * 192), (4, 192))
    cat = lambda d: jnp.pad(jnp.concatenate([d["lru_conv_w"], d["ssd_conv_w"]], axis=1), ((0, 4), (0, 64)))
    res = _adamw(cat({"lru_conv_w": gl, "ssd_conv_w": gs})[None], cat(w), cat(m), cat(v), "adamw_conv", 8)
    for kind, r in zip(("grad", "delta", "new_m", "new_v"), res):
        outs[kind + "_lru_conv_w"] = r[0:4, 0:128]
        outs[kind + "_ssd_conv_w"] = r[0:4, 128:320]

    order = ["pre_mix_norm", "w_in", "lru_conv_w", "lru_conv_b", "lru_wa", "lru_ba", "lru_wx", "lru_bx", "lru_lambda",
             "lru_out_norm", "ssd_conv_w", "ssd_conv_b", "ssd_dt_bias", "ssd_a_log", "ssd_d", "ssd_out_norm", "w_out",
             "post_mix_norm", "pre_ffn_norm", "w_gate", "w_up", "w_down", "post_ffn_norm"]
    result = [_small_entry(g_pack, "loss").reshape(()), grad_x[None]]
    for kind in ("grad", "delta", "new_m", "new_v"):
        result += [outs[kind + "_" + n][None] for n in order]
    return tuple(result)
```
